```python
import jax, jax.numpy as jnp
from jax import lax
import numpy as np

D_MODEL = 1024
BATCH = 8
SEQ = 2048
DEPTH = 4

HEAD_DIM = 64
N_HEADS = D_MODEL // HEAD_DIM
ATTN_WIDTH = N_HEADS * HEAD_DIM
SWA_KV_HEADS = max(1, N_HEADS // 8)
SWA_GROUP = N_HEADS // SWA_KV_HEADS
SWA_WINDOW = 128
ROPE_THETA = 500000.0
ROPE_DIM = HEAD_DIM // 4
D_FF = 4 * D_MODEL
PLE_DIM = 256
Q_BLOCK = 128
N_MIXERS = 3
RMS_EPS = 1e-6
NEG_INF = -1e30
IN_COLS = (3 * ATTN_WIDTH,
           ATTN_WIDTH + 2 * SWA_KV_HEADS * HEAD_DIM,
           3 * ATTN_WIDTH + N_HEADS)

kernel_name = "interleaved_sb_swa_fox_hybrid"


def rms_norm(x, g):
    xf = x.astype(jnp.float32)
    y = xf * lax.rsqrt(jnp.mean(xf * xf, axis=-1, keepdims=True) + RMS_EPS)
    return (y * g.astype(jnp.float32)).astype(x.dtype)


def partial_rope(x, pos):
    half = ROPE_DIM // 2
    inv_freq = ROPE_THETA ** (-jnp.arange(half, dtype=jnp.float32) / half)
    ang = pos.astype(jnp.float32)[:, None] * inv_freq[None, :]
    cos = jnp.cos(ang)[None, :, None, :]
    sin = jnp.sin(ang)[None, :, None, :]
    xr = x[..., :ROPE_DIM].astype(jnp.float32)
    x1, x2 = xr[..., :half], xr[..., half:]
    rot = jnp.concatenate([x1 * cos - x2 * sin, x2 * cos + x1 * sin], axis=-1).astype(x.dtype)
    return jnp.concatenate([rot, x[..., ROPE_DIM:]], axis=-1)


def stick_breaking_attention(h, w_in):
    B, S, _ = h.shape
    proj = h @ w_in
    q = proj[..., :ATTN_WIDTH].reshape(B, S, N_HEADS, HEAD_DIM)
    k = proj[..., ATTN_WIDTH:2 * ATTN_WIDTH].reshape(B, S, N_HEADS, HEAD_DIM)
    v = proj[..., 2 * ATTN_WIDTH:].reshape(B, S, N_HEADS, HEAD_DIM)
    scale = HEAD_DIM ** -0.5
    outs = []
    for blk in range(S // Q_BLOCK):
        q0, q1 = blk * Q_BLOCK, (blk + 1) * Q_BLOCK
        z = jnp.einsum('bqhd,bkhd->bhqk', q[:, q0:q1], k[:, :q1]).astype(jnp.float32) * scale
        t = q0 + jnp.arange(Q_BLOCK)
        s = jnp.arange(q1)
        strict = s[None, :] < t[:, None]
        log_beta = jax.nn.log_sigmoid(z)
        log_one_minus = jnp.where(strict, jax.nn.log_sigmoid(-z), 0.0)
        tail = lax.cumsum(log_one_minus, axis=3, reverse=True) - log_one_minus
        w = jnp.where(strict, jnp.exp(log_beta + tail), 0.0)
        outs.append(jnp.einsum('bhqk,bkhd->bqhd', w.astype(v.dtype), v[:, :q1]))
    return jnp.concatenate(outs, axis=1).reshape(B, S, ATTN_WIDTH)


def sliding_window_sink_attention(h, w_in, sinks, pos):
    B, S, _ = h.shape
    nkv = SWA_KV_HEADS * HEAD_DIM
    proj = h @ w_in
    q = proj[..., :ATTN_WIDTH].reshape(B, S, N_HEADS, HEAD_DIM)
    k = proj[..., ATTN_WIDTH:ATTN_WIDTH + nkv].reshape(B, S, SWA_KV_HEADS, HEAD_DIM)
    v = proj[..., ATTN_WIDTH + nkv:].reshape(B, S, SWA_KV_HEADS, HEAD_DIM)
    q = partial_rope(q, pos)
    k = partial_rope(k, pos)
    nb = S // Q_BLOCK
    pad = ((0, 0), (Q_BLOCK, 0), (0, 0), (0, 0))
    kp = jnp.pad(k, pad).reshape(B, nb + 1, Q_BLOCK, SWA_KV_HEADS, HEAD_DIM)
    vp = jnp.pad(v, pad).reshape(B, nb + 1, Q_BLOCK, SWA_KV_HEADS, HEAD_DIM)
    kb = jnp.concatenate([kp[:, :-1], kp[:, 1:]], axis=2)
    vb = jnp.concatenate([vp[:, :-1], vp[:, 1:]], axis=2)
    qb = q.reshape(B, nb, Q_BLOCK, SWA_KV_HEADS, SWA_GROUP, HEAD_DIM)
    logits = jnp.einsum('bnqkgd,bnckd->bnkgqc', qb, kb).astype(jnp.float32) * (HEAD_DIM ** -0.5)
    blk = jnp.arange(nb)[:, None, None]
    q_pos = blk * Q_BLOCK + jnp.arange(Q_BLOCK)[None, :, None]
    k_pos = blk * Q_BLOCK + jnp.arange(2 * Q_BLOCK)[None, None, :] - Q_BLOCK
    diff = q_pos - k_pos
    mask = (diff >= 0) & (diff < SWA_WINDOW) & (k_pos >= 0)
    logits = jnp.where(mask[None, :, None, None], logits, NEG_INF)
    sink = sinks.astype(jnp.float32).reshape(SWA_KV_HEADS, SWA_GROUP)[None, None, :, :, None, None]
    m = jnp.maximum(jnp.max(logits, axis=-1, keepdims=True), sink)
    e = jnp.exp(logits - m)
    probs = e / (jnp.sum(e, axis=-1, keepdims=True) + jnp.exp(sink - m))
    o = jnp.einsum('bnkgqc,bnckd->bnqkgd', probs.astype(vb.dtype), vb)
    return o.reshape(B, S, ATTN_WIDTH)


def forgetting_attention(h, w_in, b_forget):
    B, S, _ = h.shape
    proj = h @ w_in
    q = proj[..., :ATTN_WIDTH].reshape(B, S, N_HEADS, HEAD_DIM)
    k = proj[..., ATTN_WIDTH:2 * ATTN_WIDTH].reshape(B, S, N_HEADS, HEAD_DIM)
    v = proj[..., 2 * ATTN_WIDTH:3 * ATTN_WIDTH].reshape(B, S, N_HEADS, HEAD_DIM)
    log_f = jax.nn.log_sigmoid(proj[..., 3 * ATTN_WIDTH:].astype(jnp.float32)
                               + b_forget.astype(jnp.float32))
    cum = lax.cumsum(log_f, axis=1).transpose(0, 2, 1)
    scale = HEAD_DIM ** -0.5
    outs = []
    for blk in range(S // Q_BLOCK):
        q0, q1 = blk * Q_BLOCK, (blk + 1) * Q_BLOCK
        logits = jnp.einsum('bqhd,bkhd->bhqk', q[:, q0:q1], k[:, :q1]).astype(jnp.float32) * scale
        logits = logits + cum[:, :, q0:q1, None] - cum[:, :, None, :q1]
        t = q0 + jnp.arange(Q_BLOCK)
        s = jnp.arange(q1)
        causal = s[None, :] <= t[:, None]
        probs = jax.nn.softmax(jnp.where(causal, logits, NEG_INF), axis=-1)
        outs.append(jnp.einsum('bhqk,bkhd->bqhd', probs.astype(v.dtype), v[:, :q1]))
    return jnp.concatenate(outs, axis=1).reshape(B, S, ATTN_WIDTH)


def squared_relu_mlp(h, w_up, w_down):
    return jnp.square(jax.nn.relu(h @ w_up)) @ w_down


def per_layer_input(x, p_i, ple_norm, w_gate, w_proj):
    gate = jax.nn.sigmoid((rms_norm(x, ple_norm) @ w_gate).astype(jnp.float32)).astype(x.dtype)
    return (p_i @ w_proj) * gate


def _fwd_setup_inputs(seed: int = 0) -> dict:
    key = jax.random.key(seed)
    keys = iter(jax.random.split(key, 64))

    def nrm(shape, scale):
        return jax.random.normal(next(keys), shape, jnp.float32) * scale

    def gain():
        return 1.0 + nrm((D_MODEL,), 0.02)

    inp = {"x": nrm((BATCH, SEQ, D_MODEL), 1.0),
           "p": nrm((DEPTH, BATCH, SEQ, PLE_DIM), 1.0)}
    for i in range(DEPTH):
        kind = i % N_MIXERS
        inp[f"attn_norm_{i}"] = gain()
        inp[f"w_in_{i}"] = nrm((D_MODEL, IN_COLS[kind]), D_MODEL ** -0.5)
        inp[f"w_out_{i}"] = nrm((ATTN_WIDTH, D_MODEL), ATTN_WIDTH ** -0.5)
        if kind == 1:
            inp[f"sinks_{i}"] = nrm((N_HEADS,), 0.5)
        if kind == 2:
            inp[f"b_forget_{i}"] = jax.random.uniform(next(keys), (N_HEADS,), jnp.float32, 1.0, 4.0)
        inp[f"mlp_norm_{i}"] = gain()
        inp[f"w_up_{i}"] = nrm((D_MODEL, D_FF), D_MODEL ** -0.5)
        inp[f"w_down_{i}"] = nrm((D_FF, D_MODEL), D_FF ** -0.5)
        inp[f"ple_norm_{i}"] = gain()
        inp[f"w_ple_gate_{i}"] = nrm((D_MODEL, D_MODEL), D_MODEL ** -0.5)
        inp[f"w_ple_proj_{i}"] = nrm((PLE_DIM, D_MODEL), PLE_DIM ** -0.5)
    inp["final_norm"] = gain()
    return inp


def _fwd_reference(x, p,
              attn_norm_0, w_in_0, w_out_0, mlp_norm_0, w_up_0, w_down_0, ple_norm_0, w_ple_gate_0, w_ple_proj_0,
              attn_norm_1, w_in_1, w_out_1, sinks_1, mlp_norm_1, w_up_1, w_down_1, ple_norm_1, w_ple_gate_1, w_ple_proj_1,
              attn_norm_2, w_in_2, w_out_2, b_forget_2, mlp_norm_2, w_up_2, w_down_2, ple_norm_2, w_ple_gate_2, w_ple_proj_2,
              attn_norm_3, w_in_3, w_out_3, mlp_norm_3, w_up_3, w_down_3, ple_norm_3, w_ple_gate_3, w_ple_proj_3,
              final_norm):
    layers = [
        (attn_norm_0, w_in_0, w_out_0, None, mlp_norm_0, w_up_0, w_down_0, ple_norm_0, w_ple_gate_0, w_ple_proj_0),
        (attn_norm_1, w_in_1, w_out_1, sinks_1, mlp_norm_1, w_up_1, w_down_1, ple_norm_1, w_ple_gate_1, w_ple_proj_1),
        (attn_norm_2, w_in_2, w_out_2, b_forget_2, mlp_norm_2, w_up_2, w_down_2, ple_norm_2, w_ple_gate_2, w_ple_proj_2),
        (attn_norm_3, w_in_3, w_out_3, None, mlp_norm_3, w_up_3, w_down_3, ple_norm_3, w_ple_gate_3, w_ple_proj_3),
    ]
    pos = jnp.arange(x.shape[1], dtype=jnp.int32)
    for i in range(DEPTH):
        an, wi, wo, extra, mn, wu, wd, pn, wg, wp = layers[i]
        kind = i % N_MIXERS
        h = rms_norm(x, an)
        if kind == 0:
            a = stick_breaking_attention(h, wi)
        elif kind == 1:
            a = sliding_window_sink_attention(h, wi, extra, pos)
        else:
            a = forgetting_attention(h, wi, extra)
        x = x + a @ wo
        x = x + squared_relu_mlp(rms_norm(x, mn), wu, wd)
        x = x + per_layer_input(x, p[i], pn, wg, wp)
    return rms_norm(x, final_norm)


import jax as _jax
import jax.numpy as _jnp

TWIN_FORMAT = 'train_step'
FWD_PARAMS = ['x', 'p', 'attn_norm_0', 'w_in_0', 'w_out_0', 'mlp_norm_0', 'w_up_0', 'w_down_0', 'ple_norm_0', 'w_ple_gate_0', 'w_ple_proj_0', 'attn_norm_1', 'w_in_1', 'w_out_1', 'sinks_1', 'mlp_norm_1', 'w_up_1', 'w_down_1', 'ple_norm_1', 'w_ple_gate_1', 'w_ple_proj_1', 'attn_norm_2', 'w_in_2', 'w_out_2', 'b_forget_2', 'mlp_norm_2', 'w_up_2', 'w_down_2', 'ple_norm_2', 'w_ple_gate_2', 'w_ple_proj_2', 'attn_norm_3', 'w_in_3', 'w_out_3', 'mlp_norm_3', 'w_up_3', 'w_down_3', 'ple_norm_3', 'w_ple_gate_3', 'w_ple_proj_3', 'final_norm']
TWIN_WEIGHTS = ['attn_norm_0', 'w_in_0', 'w_out_0', 'mlp_norm_0', 'w_up_0', 'w_down_0', 'ple_norm_0', 'w_ple_gate_0', 'w_ple_proj_0', 'attn_norm_1', 'w_in_1', 'w_out_1', 'sinks_1', 'mlp_norm_1', 'w_up_1', 'w_down_1', 'ple_norm_1', 'w_ple_gate_1', 'w_ple_proj_1', 'attn_norm_2', 'w_in_2', 'w_out_2', 'b_forget_2', 'mlp_norm_2', 'w_up_2', 'w_down_2', 'ple_norm_2', 'w_ple_gate_2', 'w_ple_proj_2', 'attn_norm_3', 'w_in_3', 'w_out_3', 'mlp_norm_3', 'w_up_3', 'w_down_3', 'ple_norm_3', 'w_ple_gate_3', 'w_ple_proj_3', 'final_norm']
TWIN_DIFF_INPUT = 'x'
TWIN_INPUTS = ['x', 'p', 'attn_norm_0', 'w_in_0', 'w_out_0', 'mlp_norm_0', 'w_up_0', 'w_down_0', 'ple_norm_0', 'w_ple_gate_0', 'w_ple_proj_0', 'attn_norm_1', 'w_in_1', 'w_out_1', 'sinks_1', 'mlp_norm_1', 'w_up_1', 'w_down_1', 'ple_norm_1', 'w_ple_gate_1', 'w_ple_proj_1', 'attn_norm_2', 'w_in_2', 'w_out_2', 'b_forget_2', 'mlp_norm_2', 'w_up_2', 'w_down_2', 'ple_norm_2', 'w_ple_gate_2', 'w_ple_proj_2', 'attn_norm_3', 'w_in_3', 'w_out_3', 'mlp_norm_3', 'w_up_3', 'w_down_3', 'ple_norm_3', 'w_ple_gate_3', 'w_ple_proj_3', 'final_norm', 'loss_target', 'm_attn_norm_0', 'm_w_in_0', 'm_w_out_0', 'm_mlp_norm_0', 'm_w_up_0', 'm_w_down_0', 'm_ple_norm_0', 'm_w_ple_gate_0', 'm_w_ple_proj_0', 'm_attn_norm_1', 'm_w_in_1', 'm_w_out_1', 'm_sinks_1', 'm_mlp_norm_1', 'm_w_up_1', 'm_w_down_1', 'm_ple_norm_1', 'm_w_ple_gate_1', 'm_w_ple_proj_1', 'm_attn_norm_2', 'm_w_in_2', 'm_w_out_2', 'm_b_forget_2', 'm_mlp_norm_2', 'm_w_up_2', 'm_w_down_2', 'm_ple_norm_2', 'm_w_ple_gate_2', 'm_w_ple_proj_2', 'm_attn_norm_3', 'm_w_in_3', 'm_w_out_3', 'm_mlp_norm_3', 'm_w_up_3', 'm_w_down_3', 'm_ple_norm_3', 'm_w_ple_gate_3', 'm_w_ple_proj_3', 'm_final_norm', 'v_attn_norm_0', 'v_w_in_0', 'v_w_out_0', 'v_mlp_norm_0', 'v_w_up_0', 'v_w_down_0', 'v_ple_norm_0', 'v_w_ple_gate_0', 'v_w_ple_proj_0', 'v_attn_norm_1', 'v_w_in_1', 'v_w_out_1', 'v_sinks_1', 'v_mlp_norm_1', 'v_w_up_1', 'v_w_down_1', 'v_ple_norm_1', 'v_w_ple_gate_1', 'v_w_ple_proj_1', 'v_attn_norm_2', 'v_w_in_2', 'v_w_out_2', 'v_b_forget_2', 'v_mlp_norm_2', 'v_w_up_2', 'v_w_down_2', 'v_ple_norm_2', 'v_w_ple_gate_2', 'v_w_ple_proj_2', 'v_attn_norm_3', 'v_w_in_3', 'v_w_out_3', 'v_mlp_norm_3', 'v_w_up_3', 'v_w_down_3', 'v_ple_norm_3', 'v_w_ple_gate_3', 'v_w_ple_proj_3', 'v_final_norm']
TWIN_OUTPUTS = ['loss', 'grad_x', 'grad_attn_norm_0', 'grad_w_in_0', 'grad_w_out_0', 'grad_mlp_norm_0', 'grad_w_up_0', 'grad_w_down_0', 'grad_ple_norm_0', 'grad_w_ple_gate_0', 'grad_w_ple_proj_0', 'grad_attn_norm_1', 'grad_w_in_1', 'grad_w_out_1', 'grad_sinks_1', 'grad_mlp_norm_1', 'grad_w_up_1', 'grad_w_down_1', 'grad_ple_norm_1', 'grad_w_ple_gate_1', 'grad_w_ple_proj_1', 'grad_attn_norm_2', 'grad_w_in_2', 'grad_w_out_2', 'grad_b_forget_2', 'grad_mlp_norm_2', 'grad_w_up_2', 'grad_w_down_2', 'grad_ple_norm_2', 'grad_w_ple_gate_2', 'grad_w_ple_proj_2', 'grad_attn_norm_3', 'grad_w_in_3', 'grad_w_out_3', 'grad_mlp_norm_3', 'grad_w_up_3', 'grad_w_down_3', 'grad_ple_norm_3', 'grad_w_ple_gate_3', 'grad_w_ple_proj_3', 'grad_final_norm', 'delta_attn_norm_0', 'delta_w_in_0', 'delta_w_out_0', 'delta_mlp_norm_0', 'delta_w_up_0', 'delta_w_down_0', 'delta_ple_norm_0', 'delta_w_ple_gate_0', 'delta_w_ple_proj_0', 'delta_attn_norm_1', 'delta_w_in_1', 'delta_w_out_1', 'delta_sinks_1', 'delta_mlp_norm_1', 'delta_w_up_1', 'delta_w_down_1', 'delta_ple_norm_1', 'delta_w_ple_gate_1', 'delta_w_ple_proj_1', 'delta_attn_norm_2', 'delta_w_in_2', 'delta_w_out_2', 'delta_b_forget_2', 'delta_mlp_norm_2', 'delta_w_up_2', 'delta_w_down_2', 'delta_ple_norm_2', 'delta_w_ple_gate_2', 'delta_w_ple_proj_2', 'delta_attn_norm_3', 'delta_w_in_3', 'delta_w_out_3', 'delta_mlp_norm_3', 'delta_w_up_3', 'delta_w_down_3', 'delta_ple_norm_3', 'delta_w_ple_gate_3', 'delta_w_ple_proj_3', 'delta_final_norm', 'new_m_attn_norm_0', 'new_m_w_in_0', 'new_m_w_out_0', 'new_m_mlp_norm_0', 'new_m_w_up_0', 'new_m_w_down_0', 'new_m_ple_norm_0', 'new_m_w_ple_gate_0', 'new_m_w_ple_proj_0', 'new_m_attn_norm_1', 'new_m_w_in_1', 'new_m_w_out_1', 'new_m_sinks_1', 'new_m_mlp_norm_1', 'new_m_w_up_1', 'new_m_w_down_1', 'new_m_ple_norm_1', 'new_m_w_ple_gate_1', 'new_m_w_ple_proj_1', 'new_m_attn_norm_2', 'new_m_w_in_2', 'new_m_w_out_2', 'new_m_b_forget_2', 'new_m_mlp_norm_2', 'new_m_w_up_2', 'new_m_w_down_2', 'new_m_ple_norm_2', 'new_m_w_ple_gate_2', 'new_m_w_ple_proj_2', 'new_m_attn_norm_3', 'new_m_w_in_3', 'new_m_w_out_3', 'new_m_mlp_norm_3', 'new_m_w_up_3', 'new_m_w_down_3', 'new_m_ple_norm_3', 'new_m_w_ple_gate_3', 'new_m_w_ple_proj_3', 'new_m_final_norm', 'new_v_attn_norm_0', 'new_v_w_in_0', 'new_v_w_out_0', 'new_v_mlp_norm_0', 'new_v_w_up_0', 'new_v_w_down_0', 'new_v_ple_norm_0', 'new_v_w_ple_gate_0', 'new_v_w_ple_proj_0', 'new_v_attn_norm_1', 'new_v_w_in_1', 'new_v_w_out_1', 'new_v_sinks_1', 'new_v_mlp_norm_1', 'new_v_w_up_1', 'new_v_w_down_1', 'new_v_ple_norm_1', 'new_v_w_ple_gate_1', 'new_v_w_ple_proj_1', 'new_v_attn_norm_2', 'new_v_w_in_2', 'new_v_w_out_2', 'new_v_b_forget_2', 'new_v_mlp_norm_2', 'new_v_w_up_2', 'new_v_w_down_2', 'new_v_ple_norm_2', 'new_v_w_ple_gate_2', 'new_v_w_ple_proj_2', 'new_v_attn_norm_3', 'new_v_w_in_3', 'new_v_w_out_3', 'new_v_mlp_norm_3', 'new_v_w_up_3', 'new_v_w_down_3', 'new_v_ple_norm_3', 'new_v_w_ple_gate_3', 'new_v_w_ple_proj_3', 'new_v_final_norm']
TWIN_LEAF_KINDS = {'loss': 'loss', 'grad_x': 'grad_x', 'grad_attn_norm_0': 'grad_w', 'grad_w_in_0': 'grad_w', 'grad_w_out_0': 'grad_w', 'grad_mlp_norm_0': 'grad_w', 'grad_w_up_0': 'grad_w', 'grad_w_down_0': 'grad_w', 'grad_ple_norm_0': 'grad_w', 'grad_w_ple_gate_0': 'grad_w', 'grad_w_ple_proj_0': 'grad_w', 'grad_attn_norm_1': 'grad_w', 'grad_w_in_1': 'grad_w', 'grad_w_out_1': 'grad_w', 'grad_sinks_1': 'grad_w', 'grad_mlp_norm_1': 'grad_w', 'grad_w_up_1': 'grad_w', 'grad_w_down_1': 'grad_w', 'grad_ple_norm_1': 'grad_w', 'grad_w_ple_gate_1': 'grad_w', 'grad_w_ple_proj_1': 'grad_w', 'grad_attn_norm_2': 'grad_w', 'grad_w_in_2': 'grad_w', 'grad_w_out_2': 'grad_w', 'grad_b_forget_2': 'grad_w', 'grad_mlp_norm_2': 'grad_w', 'grad_w_up_2': 'grad_w', 'grad_w_down_2': 'grad_w', 'grad_ple_norm_2': 'grad_w', 'grad_w_ple_gate_2': 'grad_w', 'grad_w_ple_proj_2': 'grad_w', 'grad_attn_norm_3': 'grad_w', 'grad_w_in_3': 'grad_w', 'grad_w_out_3': 'grad_w', 'grad_mlp_norm_3': 'grad_w', 'grad_w_up_3': 'grad_w', 'grad_w_down_3': 'grad_w', 'grad_ple_norm_3': 'grad_w', 'grad_w_ple_gate_3': 'grad_w', 'grad_w_ple_proj_3': 'grad_w', 'grad_final_norm': 'grad_w', 'delta_attn_norm_0': 'delta_w', 'delta_w_in_0': 'delta_w', 'delta_w_out_0': 'delta_w', 'delta_mlp_norm_0': 'delta_w', 'delta_w_up_0': 'delta_w', 'delta_w_down_0': 'delta_w', 'delta_ple_norm_0': 'delta_w', 'delta_w_ple_gate_0': 'delta_w', 'delta_w_ple_proj_0': 'delta_w', 'delta_attn_norm_1': 'delta_w', 'delta_w_in_1': 'delta_w', 'delta_w_out_1': 'delta_w', 'delta_sinks_1': 'delta_w', 'delta_mlp_norm_1': 'delta_w', 'delta_w_up_1': 'delta_w', 'delta_w_down_1': 'delta_w', 'delta_ple_norm_1': 'delta_w', 'delta_w_ple_gate_1': 'delta_w', 'delta_w_ple_proj_1': 'delta_w', 'delta_attn_norm_2': 'delta_w', 'delta_w_in_2': 'delta_w', 'delta_w_out_2': 'delta_w', 'delta_b_forget_2': 'delta_w', 'delta_mlp_norm_2': 'delta_w', 'delta_w_up_2': 'delta_w', 'delta_w_down_2': 'delta_w', 'delta_ple_norm_2': 'delta_w', 'delta_w_ple_gate_2': 'delta_w', 'delta_w_ple_proj_2': 'delta_w', 'delta_attn_norm_3': 'delta_w', 'delta_w_in_3': 'delta_w', 'delta_w_out_3': 'delta_w', 'delta_mlp_norm_3': 'delta_w', 'delta_w_up_3': 'delta_w', 'delta_w_down_3': 'delta_w', 'delta_ple_norm_3': 'delta_w', 'delta_w_ple_gate_3': 'delta_w', 'delta_w_ple_proj_3': 'delta_w', 'delta_final_norm': 'delta_w', 'new_m_attn_norm_0': 'new_m', 'new_m_w_in_0': 'new_m', 'new_m_w_out_0': 'new_m', 'new_m_mlp_norm_0': 'new_m', 'new_m_w_up_0': 'new_m', 'new_m_w_down_0': 'new_m', 'new_m_ple_norm_0': 'new_m', 'new_m_w_ple_gate_0': 'new_m', 'new_m_w_ple_proj_0': 'new_m', 'new_m_attn_norm_1': 'new_m', 'new_m_w_in_1': 'new_m', 'new_m_w_out_1': 'new_m', 'new_m_sinks_1': 'new_m', 'new_m_mlp_norm_1': 'new_m', 'new_m_w_up_1': 'new_m', 'new_m_w_down_1': 'new_m', 'new_m_ple_norm_1': 'new_m', 'new_m_w_ple_gate_1': 'new_m', 'new_m_w_ple_proj_1': 'new_m', 'new_m_attn_norm_2': 'new_m', 'new_m_w_in_2': 'new_m', 'new_m_w_out_2': 'new_m', 'new_m_b_forget_2': 'new_m', 'new_m_mlp_norm_2': 'new_m', 'new_m_w_up_2': 'new_m', 'new_m_w_down_2': 'new_m', 'new_m_ple_norm_2': 'new_m', 'new_m_w_ple_gate_2': 'new_m', 'new_m_w_ple_proj_2': 'new_m', 'new_m_attn_norm_3': 'new_m', 'new_m_w_in_3': 'new_m', 'new_m_w_out_3': 'new_m', 'new_m_mlp_norm_3': 'new_m', 'new_m_w_up_3': 'new_m', 'new_m_w_down_3': 'new_m', 'new_m_ple_norm_3': 'new_m', 'new_m_w_ple_gate_3': 'new_m', 'new_m_w_ple_proj_3': 'new_m', 'new_m_final_norm': 'new_m', 'new_v_attn_norm_0': 'new_v', 'new_v_w_in_0': 'new_v', 'new_v_w_out_0': 'new_v', 'new_v_mlp_norm_0': 'new_v', 'new_v_w_up_0': 'new_v', 'new_v_w_down_0': 'new_v', 'new_v_ple_norm_0': 'new_v', 'new_v_w_ple_gate_0': 'new_v', 'new_v_w_ple_proj_0': 'new_v', 'new_v_attn_norm_1': 'new_v', 'new_v_w_in_1': 'new_v', 'new_v_w_out_1': 'new_v', 'new_v_sinks_1': 'new_v', 'new_v_mlp_norm_1': 'new_v', 'new_v_w_up_1': 'new_v', 'new_v_w_down_1': 'new_v', 'new_v_ple_norm_1': 'new_v', 'new_v_w_ple_gate_1': 'new_v', 'new_v_w_ple_proj_1': 'new_v', 'new_v_attn_norm_2': 'new_v', 'new_v_w_in_2': 'new_v', 'new_v_w_out_2': 'new_v', 'new_v_b_forget_2': 'new_v', 'new_v_mlp_norm_2': 'new_v', 'new_v_w_up_2': 'new_v', 'new_v_w_down_2': 'new_v', 'new_v_ple_norm_2': 'new_v', 'new_v_w_ple_gate_2': 'new_v', 'new_v_w_ple_proj_2': 'new_v', 'new_v_attn_norm_3': 'new_v', 'new_v_w_in_3': 'new_v', 'new_v_w_out_3': 'new_v', 'new_v_mlp_norm_3': 'new_v', 'new_v_w_up_3': 'new_v', 'new_v_w_down_3': 'new_v', 'new_v_ple_norm_3': 'new_v', 'new_v_w_ple_gate_3': 'new_v', 'new_v_w_ple_proj_3': 'new_v', 'new_v_final_norm': 'new_v'}


def _forward(args):
    return _fwd_reference(*[args[k] for k in FWD_PARAMS])


def _output_shape():
    out = _jax.eval_shape(lambda: _forward(_fwd_setup_inputs(0)))
    return out.shape, out.dtype

N_MICROBATCH = 1
ADAM_LR = 0.001
ADAM_B1 = 0.9
ADAM_B2 = 0.999
ADAM_EPS = 1e-08
ADAM_WD = 0.01
ADAM_STEP = 10
PER_EXAMPLE_BATCH_AXIS = {'x': 0, 'p': 1, 'loss_target': 0}
SHARED_INPUTS = []
_WEIGHT_DTYPES = {'attn_norm_0': _jnp.float32, 'w_in_0': _jnp.float32, 'w_out_0': _jnp.float32, 'mlp_norm_0': _jnp.float32, 'w_up_0': _jnp.float32, 'w_down_0': _jnp.float32, 'ple_norm_0': _jnp.float32, 'w_ple_gate_0': _jnp.float32, 'w_ple_proj_0': _jnp.float32, 'attn_norm_1': _jnp.float32, 'w_in_1': _jnp.float32, 'w_out_1': _jnp.float32, 'sinks_1': _jnp.float32, 'mlp_norm_1': _jnp.float32, 'w_up_1': _jnp.float32, 'w_down_1': _jnp.float32, 'ple_norm_1': _jnp.float32, 'w_ple_gate_1': _jnp.float32, 'w_ple_proj_1': _jnp.float32, 'attn_norm_2': _jnp.float32, 'w_in_2': _jnp.float32, 'w_out_2': _jnp.float32, 'b_forget_2': _jnp.float32, 'mlp_norm_2': _jnp.float32, 'w_up_2': _jnp.float32, 'w_down_2': _jnp.float32, 'ple_norm_2': _jnp.float32, 'w_ple_gate_2': _jnp.float32, 'w_ple_proj_2': _jnp.float32, 'attn_norm_3': _jnp.float32, 'w_in_3': _jnp.float32, 'w_out_3': _jnp.float32, 'mlp_norm_3': _jnp.float32, 'w_up_3': _jnp.float32, 'w_down_3': _jnp.float32, 'ple_norm_3': _jnp.float32, 'w_ple_gate_3': _jnp.float32, 'w_ple_proj_3': _jnp.float32, 'final_norm': _jnp.float32}
MOMENT_SCALE = {'attn_norm_0': 9.005234e-02, 'w_in_0': 5.120462e-02, 'w_out_0': 7.429665e-02, 'mlp_norm_0': 1.059562e-01, 'w_up_0': 5.221762e-02, 'w_down_0': 1.086797e-01, 'ple_norm_0': 1.553591e-02, 'w_ple_gate_0': 1.561984e-02, 'w_ple_proj_0': 3.980934e-02, 'attn_norm_1': 3.284696e-02, 'w_in_1': 3.205943e-02, 'w_out_1': 3.965482e-02, 'sinks_1': 1.864513e-02, 'mlp_norm_1': 8.319319e-02, 'w_up_1': 4.206203e-02, 'w_down_1': 8.467461e-02, 'ple_norm_1': 1.232134e-02, 'w_ple_gate_1': 1.217228e-02, 'w_ple_proj_1': 3.097004e-02, 'attn_norm_2': 4.449619e-02, 'w_in_2': 2.586054e-02, 'w_out_2': 4.187212e-02, 'b_forget_2': 8.411857e-02, 'mlp_norm_2': 7.149860e-02, 'w_up_2': 3.628458e-02, 'w_down_2': 7.478183e-02, 'ple_norm_2': 1.076095e-02, 'w_ple_gate_2': 1.012704e-02, 'w_ple_proj_2': 2.571885e-02, 'attn_norm_3': 5.101330e-02, 'w_in_3': 2.713976e-02, 'w_out_3': 4.462551e-02, 'mlp_norm_3': 6.236529e-02, 'w_up_3': 3.110722e-02, 'w_down_3': 6.519368e-02, 'ple_norm_3': 8.911904e-03, 'w_ple_gate_3': 9.059847e-03, 'w_ple_proj_3': 2.217531e-02, 'final_norm': 1.657780e+01}


def _to_microbatches(a, axis):
    t = _jnp.moveaxis(a, axis, 0)
    t = t.reshape((N_MICROBATCH, t.shape[0] // N_MICROBATCH) + t.shape[1:])
    return _jnp.moveaxis(t, 1, axis + 1)


def setup_inputs(seed: int = 0) -> dict:
    inp = _fwd_setup_inputs(seed)
    key = _jax.random.fold_in(_jax.random.key(seed), 7919)
    shape, _ = _output_shape()
    out = dict(inp)
    out["loss_target"] = _jax.random.normal(_jax.random.fold_in(key, 0), shape, _jnp.float32)
    for i, name in enumerate(TWIN_WEIGHTS):
        w = inp[name].astype(_jnp.float32)
        if MOMENT_SCALE is None:
            s = _jnp.sqrt(_jnp.mean(_jnp.square(w)) + 1e-30)
        else:
            s = MOMENT_SCALE[name]
        km, kv = _jax.random.split(_jax.random.fold_in(key, i + 1))
        out[name] = w
        out["m_" + name] = s * _jax.random.normal(km, w.shape, _jnp.float32)
        out["v_" + name] = (s * s) * _jax.random.uniform(kv, w.shape, _jnp.float32, 0.5, 1.5)
    if N_MICROBATCH > 1:
        for name, axis in PER_EXAMPLE_BATCH_AXIS.items():
            out[name] = _to_microbatches(out[name], axis)
    return {'x': out['x'], 'p': out['p'], 'attn_norm_0': out['attn_norm_0'], 'w_in_0': out['w_in_0'], 'w_out_0': out['w_out_0'], 'mlp_norm_0': out['mlp_norm_0'], 'w_up_0': out['w_up_0'], 'w_down_0': out['w_down_0'], 'ple_norm_0': out['ple_norm_0'], 'w_ple_gate_0': out['w_ple_gate_0'], 'w_ple_proj_0': out['w_ple_proj_0'], 'attn_norm_1': out['attn_norm_1'], 'w_in_1': out['w_in_1'], 'w_out_1': out['w_out_1'], 'sinks_1': out['sinks_1'], 'mlp_norm_1': out['mlp_norm_1'], 'w_up_1': out['w_up_1'], 'w_down_1': out['w_down_1'], 'ple_norm_1': out['ple_norm_1'], 'w_ple_gate_1': out['w_ple_gate_1'], 'w_ple_proj_1': out['w_ple_proj_1'], 'attn_norm_2': out['attn_norm_2'], 'w_in_2': out['w_in_2'], 'w_out_2': out['w_out_2'], 'b_forget_2': out['b_forget_2'], 'mlp_norm_2': out['mlp_norm_2'], 'w_up_2': out['w_up_2'], 'w_down_2': out['w_down_2'], 'ple_norm_2': out['ple_norm_2'], 'w_ple_gate_2': out['w_ple_gate_2'], 'w_ple_proj_2': out['w_ple_proj_2'], 'attn_norm_3': out['attn_norm_3'], 'w_in_3': out['w_in_3'], 'w_out_3': out['w_out_3'], 'mlp_norm_3': out['mlp_norm_3'], 'w_up_3': out['w_up_3'], 'w_down_3': out['w_down_3'], 'ple_norm_3': out['ple_norm_3'], 'w_ple_gate_3': out['w_ple_gate_3'], 'w_ple_proj_3': out['w_ple_proj_3'], 'final_norm': out['final_norm'], 'loss_target': out['loss_target'], 'm_attn_norm_0': out['m_attn_norm_0'], 'm_w_in_0': out['m_w_in_0'], 'm_w_out_0': out['m_w_out_0'], 'm_mlp_norm_0': out['m_mlp_norm_0'], 'm_w_up_0': out['m_w_up_0'], 'm_w_down_0': out['m_w_down_0'], 'm_ple_norm_0': out['m_ple_norm_0'], 'm_w_ple_gate_0': out['m_w_ple_gate_0'], 'm_w_ple_proj_0': out['m_w_ple_proj_0'], 'm_attn_norm_1': out['m_attn_norm_1'], 'm_w_in_1': out['m_w_in_1'], 'm_w_out_1': out['m_w_out_1'], 'm_sinks_1': out['m_sinks_1'], 'm_mlp_norm_1': out['m_mlp_norm_1'], 'm_w_up_1': out['m_w_up_1'], 'm_w_down_1': out['m_w_down_1'], 'm_ple_norm_1': out['m_ple_norm_1'], 'm_w_ple_gate_1': out['m_w_ple_gate_1'], 'm_w_ple_proj_1': out['m_w_ple_proj_1'], 'm_attn_norm_2': out['m_attn_norm_2'], 'm_w_in_2': out['m_w_in_2'], 'm_w_out_2': out['m_w_out_2'], 'm_b_forget_2': out['m_b_forget_2'], 'm_mlp_norm_2': out['m_mlp_norm_2'], 'm_w_up_2': out['m_w_up_2'], 'm_w_down_2': out['m_w_down_2'], 'm_ple_norm_2': out['m_ple_norm_2'], 'm_w_ple_gate_2': out['m_w_ple_gate_2'], 'm_w_ple_proj_2': out['m_w_ple_proj_2'], 'm_attn_norm_3': out['m_attn_norm_3'], 'm_w_in_3': out['m_w_in_3'], 'm_w_out_3': out['m_w_out_3'], 'm_mlp_norm_3': out['m_mlp_norm_3'], 'm_w_up_3': out['m_w_up_3'], 'm_w_down_3': out['m_w_down_3'], 'm_ple_norm_3': out['m_ple_norm_3'], 'm_w_ple_gate_3': out['m_w_ple_gate_3'], 'm_w_ple_proj_3': out['m_w_ple_proj_3'], 'm_final_norm': out['m_final_norm'], 'v_attn_norm_0': out['v_attn_norm_0'], 'v_w_in_0': out['v_w_in_0'], 'v_w_out_0': out['v_w_out_0'], 'v_mlp_norm_0': out['v_mlp_norm_0'], 'v_w_up_0': out['v_w_up_0'], 'v_w_down_0': out['v_w_down_0'], 'v_ple_norm_0': out['v_ple_norm_0'], 'v_w_ple_gate_0': out['v_w_ple_gate_0'], 'v_w_ple_proj_0': out['v_w_ple_proj_0'], 'v_attn_norm_1': out['v_attn_norm_1'], 'v_w_in_1': out['v_w_in_1'], 'v_w_out_1': out['v_w_out_1'], 'v_sinks_1': out['v_sinks_1'], 'v_mlp_norm_1': out['v_mlp_norm_1'], 'v_w_up_1': out['v_w_up_1'], 'v_w_down_1': out['v_w_down_1'], 'v_ple_norm_1': out['v_ple_norm_1'], 'v_w_ple_gate_1': out['v_w_ple_gate_1'], 'v_w_ple_proj_1': out['v_w_ple_proj_1'], 'v_attn_norm_2': out['v_attn_norm_2'], 'v_w_in_2': out['v_w_in_2'], 'v_w_out_2': out['v_w_out_2'], 'v_b_forget_2': out['v_b_forget_2'], 'v_mlp_norm_2': out['v_mlp_norm_2'], 'v_w_up_2': out['v_w_up_2'], 'v_w_down_2': out['v_w_down_2'], 'v_ple_norm_2': out['v_ple_norm_2'], 'v_w_ple_gate_2': out['v_w_ple_gate_2'], 'v_w_ple_proj_2': out['v_w_ple_proj_2'], 'v_attn_norm_3': out['v_attn_norm_3'], 'v_w_in_3': out['v_w_in_3'], 'v_w_out_3': out['v_w_out_3'], 'v_mlp_norm_3': out['v_mlp_norm_3'], 'v_w_up_3': out['v_w_up_3'], 'v_w_down_3': out['v_w_down_3'], 'v_ple_norm_3': out['v_ple_norm_3'], 'v_w_ple_gate_3': out['v_w_ple_gate_3'], 'v_w_ple_proj_3': out['v_w_ple_proj_3'], 'v_final_norm': out['v_final_norm']}


def _loss(weights, diff, rest, loss_target):
    with _jax.named_scope("forward"):
        args = {**rest, TWIN_DIFF_INPUT: diff, **{k: w.astype(_WEIGHT_DTYPES[k]) for k, w in weights.items()}}
        y = _forward(args)
    with _jax.named_scope("loss_head"):
        err = _jnp.square(y.astype(_jnp.float32) - loss_target)
        return 0.5 * _jnp.sum(_jnp.mean(err, axis=-1)) if err.ndim else 0.5 * err


def _adamw(w, g, m, v):
    m = ADAM_B1 * m + (1.0 - ADAM_B1) * g
    v = ADAM_B2 * v + (1.0 - ADAM_B2) * _jnp.square(g)
    m_hat = m / (1.0 - ADAM_B1 ** ADAM_STEP)
    v_hat = v / (1.0 - ADAM_B2 ** ADAM_STEP)
    delta = -ADAM_LR * (m_hat / (_jnp.sqrt(v_hat) + ADAM_EPS) + ADAM_WD * w)
    return delta, m, v


def reference(x, p, attn_norm_0, w_in_0, w_out_0, mlp_norm_0, w_up_0, w_down_0, ple_norm_0, w_ple_gate_0, w_ple_proj_0, attn_norm_1, w_in_1, w_out_1, sinks_1, mlp_norm_1, w_up_1, w_down_1, ple_norm_1, w_ple_gate_1, w_ple_proj_1, attn_norm_2, w_in_2, w_out_2, b_forget_2, mlp_norm_2, w_up_2, w_down_2, ple_norm_2, w_ple_gate_2, w_ple_proj_2, attn_norm_3, w_in_3, w_out_3, mlp_norm_3, w_up_3, w_down_3, ple_norm_3, w_ple_gate_3, w_ple_proj_3, final_norm, loss_target, m_attn_norm_0, m_w_in_0, m_w_out_0, m_mlp_norm_0, m_w_up_0, m_w_down_0, m_ple_norm_0, m_w_ple_gate_0, m_w_ple_proj_0, m_attn_norm_1, m_w_in_1, m_w_out_1, m_sinks_1, m_mlp_norm_1, m_w_up_1, m_w_down_1, m_ple_norm_1, m_w_ple_gate_1, m_w_ple_proj_1, m_attn_norm_2, m_w_in_2, m_w_out_2, m_b_forget_2, m_mlp_norm_2, m_w_up_2, m_w_down_2, m_ple_norm_2, m_w_ple_gate_2, m_w_ple_proj_2, m_attn_norm_3, m_w_in_3, m_w_out_3, m_mlp_norm_3, m_w_up_3, m_w_down_3, m_ple_norm_3, m_w_ple_gate_3, m_w_ple_proj_3, m_final_norm, v_attn_norm_0, v_w_in_0, v_w_out_0, v_mlp_norm_0, v_w_up_0, v_w_down_0, v_ple_norm_0, v_w_ple_gate_0, v_w_ple_proj_0, v_attn_norm_1, v_w_in_1, v_w_out_1, v_sinks_1, v_mlp_norm_1, v_w_up_1, v_w_down_1, v_ple_norm_1, v_w_ple_gate_1, v_w_ple_proj_1, v_attn_norm_2, v_w_in_2, v_w_out_2, v_b_forget_2, v_mlp_norm_2, v_w_up_2, v_w_down_2, v_ple_norm_2, v_w_ple_gate_2, v_w_ple_proj_2, v_attn_norm_3, v_w_in_3, v_w_out_3, v_mlp_norm_3, v_w_up_3, v_w_down_3, v_ple_norm_3, v_w_ple_gate_3, v_w_ple_proj_3, v_final_norm):
    given = dict(x=x, p=p, attn_norm_0=attn_norm_0, w_in_0=w_in_0, w_out_0=w_out_0, mlp_norm_0=mlp_norm_0, w_up_0=w_up_0, w_down_0=w_down_0, ple_norm_0=ple_norm_0, w_ple_gate_0=w_ple_gate_0, w_ple_proj_0=w_ple_proj_0, attn_norm_1=attn_norm_1, w_in_1=w_in_1, w_out_1=w_out_1, sinks_1=sinks_1, mlp_norm_1=mlp_norm_1, w_up_1=w_up_1, w_down_1=w_down_1, ple_norm_1=ple_norm_1, w_ple_gate_1=w_ple_gate_1, w_ple_proj_1=w_ple_proj_1, attn_norm_2=attn_norm_2, w_in_2=w_in_2, w_out_2=w_out_2, b_forget_2=b_forget_2, mlp_norm_2=mlp_norm_2, w_up_2=w_up_2, w_down_2=w_down_2, ple_norm_2=ple_norm_2, w_ple_gate_2=w_ple_gate_2, w_ple_proj_2=w_ple_proj_2, attn_norm_3=attn_norm_3, w_in_3=w_in_3, w_out_3=w_out_3, mlp_norm_3=mlp_norm_3, w_up_3=w_up_3, w_down_3=w_down_3, ple_norm_3=ple_norm_3, w_ple_gate_3=w_ple_gate_3, w_ple_proj_3=w_ple_proj_3, final_norm=final_norm, loss_target=loss_target, m_attn_norm_0=m_attn_norm_0, m_w_in_0=m_w_in_0, m_w_out_0=m_w_out_0, m_mlp_norm_0=m_mlp_norm_0, m_w_up_0=m_w_up_0, m_w_down_0=m_w_down_0, m_ple_norm_0=m_ple_norm_0, m_w_ple_gate_0=m_w_ple_gate_0, m_w_ple_proj_0=m_w_ple_proj_0, m_attn_norm_1=m_attn_norm_1, m_w_in_1=m_w_in_1, m_w_out_1=m_w_out_1, m_sinks_1=m_sinks_1, m_mlp_norm_1=m_mlp_norm_1, m_w_up_1=m_w_up_1, m_w_down_1=m_w_down_1, m_ple_norm_1=m_ple_norm_1, m_w_ple_gate_1=m_w_ple_gate_1, m_w_ple_proj_1=m_w_ple_proj_1, m_attn_norm_2=m_attn_norm_2, m_w_in_2=m_w_in_2, m_w_out_2=m_w_out_2, m_b_forget_2=m_b_forget_2, m_mlp_norm_2=m_mlp_norm_2, m_w_up_2=m_w_up_2, m_w_down_2=m_w_down_2, m_ple_norm_2=m_ple_norm_2, m_w_ple_gate_2=m_w_ple_gate_2, m_w_ple_proj_2=m_w_ple_proj_2, m_attn_norm_3=m_attn_norm_3, m_w_in_3=m_w_in_3, m_w_out_3=m_w_out_3, m_mlp_norm_3=m_mlp_norm_3, m_w_up_3=m_w_up_3, m_w_down_3=m_w_down_3, m_ple_norm_3=m_ple_norm_3, m_w_ple_gate_3=m_w_ple_gate_3, m_w_ple_proj_3=m_w_ple_proj_3, m_final_norm=m_final_norm, v_attn_norm_0=v_attn_norm_0, v_w_in_0=v_w_in_0, v_w_out_0=v_w_out_0, v_mlp_norm_0=v_mlp_norm_0, v_w_up_0=v_w_up_0, v_w_down_0=v_w_down_0, v_ple_norm_0=v_ple_norm_0, v_w_ple_gate_0=v_w_ple_gate_0, v_w_ple_proj_0=v_w_ple_proj_0, v_attn_norm_1=v_attn_norm_1, v_w_in_1=v_w_in_1, v_w_out_1=v_w_out_1, v_sinks_1=v_sinks_1, v_mlp_norm_1=v_mlp_norm_1, v_w_up_1=v_w_up_1, v_w_down_1=v_w_down_1, v_ple_norm_1=v_ple_norm_1, v_w_ple_gate_1=v_w_ple_gate_1, v_w_ple_proj_1=v_w_ple_proj_1, v_attn_norm_2=v_attn_norm_2, v_w_in_2=v_w_in_2, v_w_out_2=v_w_out_2, v_b_forget_2=v_b_forget_2, v_mlp_norm_2=v_mlp_norm_2, v_w_up_2=v_w_up_2, v_w_down_2=v_w_down_2, v_ple_norm_2=v_ple_norm_2, v_w_ple_gate_2=v_w_ple_gate_2, v_w_ple_proj_2=v_w_ple_proj_2, v_attn_norm_3=v_attn_norm_3, v_w_in_3=v_w_in_3, v_w_out_3=v_w_out_3, v_mlp_norm_3=v_mlp_norm_3, v_w_up_3=v_w_up_3, v_w_down_3=v_w_down_3, v_ple_norm_3=v_ple_norm_3, v_w_ple_gate_3=v_w_ple_gate_3, v_w_ple_proj_3=v_w_ple_proj_3, v_final_norm=v_final_norm)
    weights = {n: given[n] for n in TWIN_WEIGHTS}
    shared = {n: given[n] for n in SHARED_INPUTS}
    per_example = {n: given[n] for n in ['x', 'p']}
    grad_fn = _jax.value_and_grad(_loss, argnums=(0, 1))

    def one_microbatch(ex, loss_target):
        ex = dict(ex)
        diff = ex.pop(TWIN_DIFF_INPUT)
        return grad_fn(weights, diff, {**shared, **ex}, loss_target)

    if N_MICROBATCH == 1:
        loss, (grad_w, grad_x) = one_microbatch(per_example, given["loss_target"])
    else:
        def body(carry, xs):
            loss_sum, grad_sum = carry
            l_k, (gw_k, gx_k) = one_microbatch(xs[0], xs[1])
            with _jax.named_scope("update"):
                return (loss_sum + l_k, _jax.tree.map(_jnp.add, grad_sum, gw_k)), gx_k

        init = (_jnp.zeros((), _jnp.float32), _jax.tree.map(_jnp.zeros_like, weights))
        (loss, grad_w), grad_x = _jax.lax.scan(body, init, (per_example, given["loss_target"]))
    with _jax.named_scope("update"):
        delta_w, new_m, new_v = {}, {}, {}
        for n in TWIN_WEIGHTS:
            delta_w[n], new_m[n], new_v[n] = _adamw(weights[n], grad_w[n], given["m_" + n], given["v_" + n])
    return (loss, grad_x, *[grad_w[n] for n in TWIN_WEIGHTS], *[delta_w[n] for n in TWIN_WEIGHTS],
            *[new_m[n] for n in TWIN_WEIGHTS], *[new_v[n] for n in TWIN_WEIGHTS])
```

```python
import jax
import jax.numpy as jnp
from jax import lax
from jax.experimental import pallas as pl
from jax.experimental.pallas import tpu as pltpu

F32 = jnp.float32
BF16 = jnp.bfloat16

D_MODEL = 1024
N_HEADS = 16
HEAD_DIM = 64
SWA_KV_HEADS = 2
SWA_GROUP = 8
SWA_WINDOW = 128
ROPE_THETA = 500000.0
ROPE_DIM = 16
RMS_EPS = 1e-6
NEG_INF = -1e30
ATTN_SCALE = HEAD_DIM ** -0.5
N_CHIPS = 4
N_DEVICES = 8

SLAB = 256
ATT_BLK = 128
ROW_TILE = 256
V7X_VMEM_LIMIT = 56 * 1024 * 1024

ADAM_LR, ADAM_B1, ADAM_B2, ADAM_EPS, ADAM_WD, ADAM_STEP = 0.001, 0.9, 0.999, 1e-08, 0.01, 10


def _cparams(sem=None):
    return pltpu.CompilerParams(dimension_semantics=sem, vmem_limit_bytes=V7X_VMEM_LIMIT)


def _dot(a, b):
    return jnp.dot(a, b, preferred_element_type=F32)


def _dot_nt(a, b):
    return lax.dot_general(a, b, (((1,), (1,)), ((), ())), preferred_element_type=F32)


def _dot_tn(a, b):
    return lax.dot_general(a, b, (((0,), (0,)), ((), ())), preferred_element_type=F32)


def mm_nn(a, wg, t, *, name, epi=None, extras=(), out_dtypes=(BF16,)):
    off, K, ns, row = t
    M = a.shape[0]
    sb = off // K
    ne, no = len(extras), len(out_dtypes)
    if row:
        grid = (ns, N_CHIPS)
        a_map = lambda q, j: (0, j)
        b_map = lambda q, j: (j, sb + q, 0)
        o_map = lambda q, j: (0, q)
        n_out = ns * SLAB
        sem = ("parallel", "arbitrary")
    else:
        grid = (N_CHIPS, ns)
        a_map = lambda j, q: (0, 0)
        b_map = lambda j, q: (j, sb + q, 0)
        o_map = lambda j, q: (0, j * ns + q)
        n_out = N_CHIPS * ns * SLAB
        sem = ("parallel", "parallel")

    def body(a_ref, b_ref, *rest):
        ex, outs = rest[:ne], rest[ne:ne + no]
        part = _dot(a_ref[...], b_ref[...])

        def finish(acc):
            res = epi(acc, *[e[...] for e in ex]) if epi is not None else (acc,)
            for o, r in zip(outs, res):
                o[...] = r.astype(o.dtype)

        if row:
            acc_ref = rest[-1]
            j = pl.program_id(1)

            @pl.when(j == 0)
            def _():
                acc_ref[...] = part

            @pl.when(j > 0)
            def _():
                acc_ref[...] += part

            @pl.when(j == N_CHIPS - 1)
            def _():
                finish(acc_ref[...])
        else:
            finish(part)

    tile = pl.BlockSpec((M, SLAB), o_map)
    return pl.pallas_call(
        body, name=name, grid=grid,
        in_specs=[pl.BlockSpec((M, K), a_map), pl.BlockSpec((None, K, SLAB), b_map)] + [tile] * ne,
        out_specs=[tile] * no,
        out_shape=[jax.ShapeDtypeStruct((M, n_out), d) for d in out_dtypes],
        scratch_shapes=[pltpu.VMEM((M, SLAB), F32)] if row else [],
        compiler_params=_cparams(sem),
    )(a, wg, *extras)


def mm_nt(dy, wg, t, *, name, epi=None, extras=(), out_dtypes=(BF16,)):
    off, K, ns, row = t
    M = dy.shape[0]
    tm = min(1024, M)
    sb = off // K
    ne, no = len(extras), len(out_dtypes)
    grid = (M // tm, N_CHIPS, ns)
    b_map = lambda i, j, q: (j, sb + q, 0)
    if row:
        dy_map = lambda i, j, q: (i, q)
        o_map = lambda i, j, q: (i, j)
        n_out = N_CHIPS * K
        sem = ("parallel", "parallel", "arbitrary")
    else:
        dy_map = lambda i, j, q: (i, j * ns + q)
        o_map = lambda i, j, q: (i, 0)
        n_out = K
        sem = ("parallel", "arbitrary", "arbitrary")

    def body(dy_ref, b_ref, *rest):
        ex, outs, acc_ref = rest[:ne], rest[ne:ne + no], rest[-1]
        j, q = pl.program_id(1), pl.program_id(2)
        if row:
            first, last = q == 0, q == ns - 1
        else:
            first = jnp.logical_and(j == 0, q == 0)
            last = jnp.logical_and(j == N_CHIPS - 1, q == ns - 1)
        part = _dot_nt(dy_ref[...], b_ref[...])

        @pl.when(first)
        def _():
            acc_ref[...] = part

        @pl.when(jnp.logical_not(first))
        def _():
            acc_ref[...] += part

        @pl.when(last)
        def _():
            acc = acc_ref[...]
            res = epi(acc, *[e[...] for e in ex]) if epi is not None else (acc,)
            for o, r in zip(outs, res):
                o[...] = r.astype(o.dtype)

    tile = pl.BlockSpec((tm, K), o_map)
    return pl.pallas_call(
        body, name=name, grid=grid,
        in_specs=[pl.BlockSpec((tm, SLAB), dy_map), pl.BlockSpec((None, K, SLAB), b_map)] + [tile] * ne,
        out_specs=[tile] * no,
        out_shape=[jax.ShapeDtypeStruct((M, n_out), d) for d in out_dtypes],
        scratch_shapes=[pltpu.VMEM((tm, K), F32)],
        compiler_params=_cparams(sem),
    )(dy, wg, *extras)


def mm_tn(x, dy, g, t, *, name):
    off, K, ns, row = t
    S = x.shape[0]
    sb = off // K
    if row:
        x_map = lambda j, q: (0, j)
        dy_map = lambda j, q: (0, q)
    else:
        x_map = lambda j, q: (0, 0)
        dy_map = lambda j, q: (0, j * ns + q)

    def body(g_in, x_ref, dy_ref, o_ref):
        del g_in
        o_ref[...] = _dot_tn(x_ref[...], dy_ref[...]).astype(o_ref.dtype)

    return pl.pallas_call(
        body, name=name, grid=(N_CHIPS, ns),
        in_specs=[pl.BlockSpec(memory_space=pl.ANY), pl.BlockSpec((S, K), x_map), pl.BlockSpec((S, SLAB), dy_map)],
        out_specs=pl.BlockSpec((None, K, SLAB), lambda j, q: (j, sb + q, 0)),
        out_shape=jax.ShapeDtypeStruct(g.shape, g.dtype),
        input_output_aliases={0: 0},
        compiler_params=_cparams(("parallel", "parallel")),
    )(g, x, dy)


def ew(fn, ins, out_dtypes, *, name, bcast=()):
    S = ins[0].shape[0]
    tr = min(ROW_TILE, S)
    cols = ins[0].shape[1]
    ni, nb = len(ins), len(bcast)

    def body(*refs):
        res = fn(*[r[...] for r in refs[:ni + nb]])
        for o, r in zip(refs[ni + nb:], res):
            o[...] = r.astype(o.dtype)

    return pl.pallas_call(
        body, name=name, grid=(S // tr,),
        in_specs=[pl.BlockSpec((tr, a.shape[1]), lambda i: (i, 0)) for a in ins]
        + [pl.BlockSpec(b.shape, lambda i: (0, 0)) for b in bcast],
        out_specs=[pl.BlockSpec((tr, cols), lambda i: (i, 0)) for _ in out_dtypes],
        out_shape=[jax.ShapeDtypeStruct((S, cols), d) for d in out_dtypes],
        compiler_params=_cparams(("parallel",)),
    )(*ins, *bcast)


def _rstd(x):
    return lax.rsqrt(jnp.mean(x * x, axis=-1, keepdims=True) + RMS_EPS)


def _sigmoid(x):
    return 1.0 / (1.0 + jnp.exp(-x))


def _log_sigmoid(z):
    return jnp.minimum(z, 0.0) - jnp.log(1.0 + jnp.exp(-jnp.abs(z)))


def rms_fwd(x, g, *, name):
    return ew(lambda xv, gv: (xv * _rstd(xv) * gv,), [x], [BF16], name=name, bcast=[g])[0]


def _rms_bwd_tile(xv, gv, dh):
    rstd = _rstd(xv)
    xhat = xv * rstd
    gd = dh * gv
    dx = rstd * (gd - xhat * jnp.mean(xhat * gd, axis=-1, keepdims=True))
    return dx, jnp.sum(dh * xhat, axis=0, keepdims=True)


def rms_bwd(x, g, dh, dres, *, name):
    S, D = x.shape
    tr = min(ROW_TILE, S)

    def body(x_ref, g_ref, dh_ref, dres_ref, dx_ref, dxb_ref, dg_ref):
        i = pl.program_id(0)
        dx, dg = _rms_bwd_tile(x_ref[...], g_ref[...], dh_ref[...])
        dx = dx + dres_ref[...]
        dx_ref[...] = dx
        dxb_ref[...] = dx.astype(BF16)

        @pl.when(i == 0)
        def _():
            dg_ref[...] = dg

        @pl.when(i > 0)
        def _():
            dg_ref[...] += dg

    row = pl.BlockSpec((tr, D), lambda i: (i, 0))
    one = pl.BlockSpec((1, D), lambda i: (0, 0))
    return pl.pallas_call(
        body, name=name, grid=(S // tr,),
        in_specs=[row, one, row, row], out_specs=[row, row, one],
        out_shape=[jax.ShapeDtypeStruct((S, D), F32), jax.ShapeDtypeStruct((S, D), BF16),
                   jax.ShapeDtypeStruct((1, D), F32)],
        compiler_params=_cparams(("arbitrary",)),
    )(x, g, dh, dres)


def loss_head(x, g, target, *, name):
    S, D = x.shape
    tr = min(ROW_TILE, S)

    def body(x_ref, g_ref, t_ref, dx_ref, dg_ref, loss_ref):
        i = pl.program_id(0)
        xv, gv = x_ref[...], g_ref[...]
        err = xv * _rstd(xv) * gv - t_ref[...]
        part = 0.5 * jnp.sum(jnp.mean(err * err, axis=-1, keepdims=True), axis=0, keepdims=True)
        dx, dg = _rms_bwd_tile(xv, gv, err * (1.0 / D))
        dx_ref[...] = dx
        part = jnp.broadcast_to(part, loss_ref.shape)

        @pl.when(i == 0)
        def _():
            dg_ref[...] = dg
            loss_ref[...] = part

        @pl.when(i > 0)
        def _():
            dg_ref[...] += dg
            loss_ref[...] += part

    row = pl.BlockSpec((tr, D), lambda i: (i, 0))
    one = pl.BlockSpec((1, D), lambda i: (0, 0))
    return pl.pallas_call(
        body, name=name, grid=(S // tr,),
        in_specs=[row, one, row], out_specs=[row, one, pl.BlockSpec((1, 128), lambda i: (0, 0))],
        out_shape=[jax.ShapeDtypeStruct((S, D), F32), jax.ShapeDtypeStruct((1, D), F32),
                   jax.ShapeDtypeStruct((1, 128), F32)],
        compiler_params=_cparams(("arbitrary",)),
    )(x, g, target)


def rope_tables(S, n_cols):
    half = ROPE_DIM // 2
    inv_freq = ROPE_THETA ** (-jnp.arange(half, dtype=F32) / half)
    ang = jnp.arange(S, dtype=F32)[:, None] * inv_freq[None, :]
    cos, sin = jnp.cos(ang), jnp.sin(ang)
    z = jnp.zeros((S, HEAD_DIM - ROPE_DIM), F32)
    zh = jnp.zeros((S, half), F32)
    c = jnp.concatenate([cos, cos, jnp.ones_like(z)], axis=1)
    sa = jnp.concatenate([zh, sin, z], axis=1)
    sb = jnp.concatenate([-sin, zh, z], axis=1)
    return [jnp.tile(t, (1, n_cols // HEAD_DIM)) for t in (c, sa, sb)]


def rope_fwd(xqk, tables, *, name):
    n, half = xqk.shape[1], ROPE_DIM // 2

    def fn(x, c, sa, sb):
        return (x * c + pltpu.roll(x, half, 1) * sa + pltpu.roll(x, n - half, 1) * sb,)

    return ew(fn, [xqk] + list(tables), [BF16], name=name)[0]


def rope_bwd(dy, tables, *, name):
    n, half = dy.shape[1], ROPE_DIM // 2

    def fn(d, c, sa, sb):
        return (d * c + pltpu.roll(d * sa, n - half, 1) + pltpu.roll(d * sb, half, 1),)

    return ew(fn, [dy] + list(tables), [BF16], name=name)[0]


def _split3(x):
    h1 = x.astype(BF16)
    r1 = x - h1.astype(F32)
    h2 = r1.astype(BF16)
    return h1, h2, (r1 - h2.astype(F32)).astype(BF16)


def _split2(x):
    h1 = x.astype(BF16)
    return h1, (x - h1.astype(F32)).astype(BF16)


def _tri(n, cmp):
    r = lax.broadcasted_iota(jnp.int32, (n, n), 0)
    c = lax.broadcasted_iota(jnp.int32, (n, n), 1)
    return cmp(r, c).astype(BF16)


def fox_gate_fwd(fl, b, *, name):
    S, W = fl.shape
    tr = min(ROW_TILE, S)

    def body(fl_ref, b_ref, cum_ref, carry):
        i = pl.program_id(0)

        @pl.when(i == 0)
        def _():
            carry[...] = jnp.zeros_like(carry)

        lower = _tri(tr, lambda r, c: r >= c)
        cs = carry[...]
        for piece in _split3(_log_sigmoid(fl_ref[...] + b_ref[...])):
            cs = cs + _dot(lower, piece)
        cum_ref[...] = cs
        carry[...] = cs[tr - 1:tr, :]

    return pl.pallas_call(
        body, name=name, grid=(S // tr,),
        in_specs=[pl.BlockSpec((tr, W), lambda i: (i, 0)), pl.BlockSpec((1, W), lambda i: (0, 0))],
        out_specs=pl.BlockSpec((tr, W), lambda i: (i, 0)),
        out_shape=jax.ShapeDtypeStruct((S, W), F32),
        scratch_shapes=[pltpu.VMEM((1, W), F32)],
        compiler_params=_cparams(("arbitrary",)),
    )(fl, b)


def fox_gate_bwd(dcum, fl, b, *, name):
    S, W = fl.shape
    tr = min(ROW_TILE, S)
    nb = S // tr

    def body(dc_ref, fl_ref, b_ref, dfl_ref, db_ref, carry):
        i = pl.program_id(0)

        @pl.when(i == 0)
        def _():
            carry[...] = jnp.zeros_like(carry)

        upper = _tri(tr, lambda r, c: r <= c)
        cs = carry[...]
        for piece in _split3(dc_ref[...]):
            cs = cs + _dot(upper, piece)
        carry[...] = cs[0:1, :]
        dfl = cs * _sigmoid(-(fl_ref[...] + b_ref[...]))
        dfl_ref[...] = dfl
        db = jnp.sum(dfl, axis=0, keepdims=True)

        @pl.when(i == 0)
        def _():
            db_ref[...] = db

        @pl.when(i > 0)
        def _():
            db_ref[...] += db

    rev = pl.BlockSpec((tr, W), lambda i: (nb - 1 - i, 0))
    one = pl.BlockSpec((1, W), lambda i: (0, 0))
    return pl.pallas_call(
        body, name=name, grid=(nb,),
        in_specs=[rev, rev, one], out_specs=[rev, one],
        out_shape=[jax.ShapeDtypeStruct((S, W), F32), jax.ShapeDtypeStruct((1, W), F32)],
        scratch_shapes=[pltpu.VMEM((1, W), F32)],
        compiler_params=_cparams(("arbitrary",)),
    )(dcum, fl, b)


def _blk_iota(tq, tk):
    return (lax.broadcasted_iota(jnp.int32, (tq, tk), 0), lax.broadcasted_iota(jnp.int32, (tq, tk), 1))


def _cs2(x, tri):
    h1, h2 = _split2(x)
    return _dot(h1, tri) + _dot(h2, tri)


def _sb_block(q, k, i, kb, row, col):
    z = _dot_nt(q, k) * ATTN_SCALE
    strict = (col + kb * ATT_BLK) < (row + i * ATT_BLK)
    sp = jnp.log(1.0 + jnp.exp(-jnp.abs(z)))
    lb = jnp.minimum(z, 0.0) - sp
    lom = jnp.where(strict, jnp.minimum(-z, 0.0) - sp, 0.0)
    return lb, lom, strict


def _head_specs(S, tq):
    qspec = pl.BlockSpec((None, tq, HEAD_DIM), lambda h, i: (h, i, 0))
    kvspec = pl.BlockSpec((None, S, HEAD_DIM), lambda h, i: (h, 0, 0))
    vec = pl.BlockSpec((None, tq, 1), lambda h, i: (h, i, 0))
    return qspec, kvspec, vec


def sb_fwd(q, k, v, *, name):
    H, S, _ = q.shape
    tq = tk = ATT_BLK
    qspec, kvspec, vec = _head_specs(S, tq)

    def body(q_ref, k_ref, v_ref, o_ref, t_ref):
        i = pl.program_id(1)
        qv = q_ref[...]
        row, col = _blk_iota(tq, tk)
        below = (row > col).astype(BF16)

        def step(n, carry):
            r_sum, acc = carry
            kb = i - n
            ks = pl.multiple_of(kb * tk, tk)
            lb, lom, strict = _sb_block(qv, k_ref[pl.ds(ks, tk), :], i, kb, row, col)
            tail = _cs2(lom, below) + r_sum
            w = jnp.where(strict, jnp.exp(lb + tail), 0.0)
            acc = acc + _dot(w.astype(BF16), v_ref[pl.ds(ks, tk), :])
            return r_sum + jnp.sum(lom, axis=1, keepdims=True), acc

        r_sum, acc = lax.fori_loop(0, i + 1, step, (jnp.zeros((tq, 1), F32), jnp.zeros((tq, HEAD_DIM), F32)))
        o_ref[...] = acc.astype(o_ref.dtype)
        t_ref[...] = r_sum

    return pl.pallas_call(
        body, name=name, grid=(H, S // tq),
        in_specs=[qspec, kvspec, kvspec], out_specs=[qspec, vec],
        out_shape=[jax.ShapeDtypeStruct((H, S, HEAD_DIM), BF16), jax.ShapeDtypeStruct((H, S, 1), F32)],
        compiler_params=_cparams(("parallel", "arbitrary")),
    )(q, k, v)


def sb_bwd(q, k, v, tot, do, *, name):
    H, S, _ = q.shape
    tq = tk = ATT_BLK
    qspec, kvspec, vec = _head_specs(S, tq)

    def body(q_ref, k_ref, v_ref, t_ref, do_ref, dq_ref, dk_ref, dv_ref):
        i = pl.program_id(1)

        @pl.when(i == 0)
        def _():
            dk_ref[...] = jnp.zeros_like(dk_ref)
            dv_ref[...] = jnp.zeros_like(dv_ref)

        qv, dov, t_all = q_ref[...], do_ref[...], t_ref[...]
        row, col = _blk_iota(tq, tk)
        upto = (row <= col).astype(BF16)
        before = (row < col).astype(BF16)

        def step(kb, carry):
            p_sum, e_sum, dq = carry
            ks = pl.multiple_of(kb * tk, tk)
            kv = k_ref[pl.ds(ks, tk), :]
            vv = v_ref[pl.ds(ks, tk), :]
            lb, lom, strict = _sb_block(qv, kv, i, kb, row, col)
            tail = t_all - p_sum - _cs2(lom, upto)
            w = jnp.where(strict, jnp.exp(lb + tail), 0.0)
            e = _dot_nt(dov, vv) * w
            e_before = e_sum + _cs2(e, before)
            beta = jnp.exp(lb)
            dz = jnp.where(strict, e * (1.0 - beta) - e_before * beta, 0.0) * ATTN_SCALE
            dzb = dz.astype(BF16)
            dk_ref[pl.ds(ks, tk), :] += _dot_tn(dzb, qv)
            dv_ref[pl.ds(ks, tk), :] += _dot_tn(w.astype(BF16), dov)
            return (p_sum + jnp.sum(lom, axis=1, keepdims=True), e_sum + jnp.sum(e, axis=1, keepdims=True),
                    dq + _dot(dzb, kv))

        zero = jnp.zeros((tq, 1), F32)
        _, _, dq = lax.fori_loop(0, i + 1, step, (zero, zero, jnp.zeros((tq, HEAD_DIM), F32)))
        dq_ref[...] = dq

    full = jax.ShapeDtypeStruct((H, S, HEAD_DIM), F32)
    return pl.pallas_call(
        body, name=name, grid=(H, S // tq),
        in_specs=[qspec, kvspec, kvspec, vec, qspec], out_specs=[qspec, kvspec, kvspec],
        out_shape=[full, full, full],
        compiler_params=_cparams(("parallel", "arbitrary")),
    )(q, k, v, tot, do)


def _fox_logits(q, k, cq, ck, i, kb, row, col):
    s = _dot_nt(q, k) * ATTN_SCALE + cq - ck
    causal = (col + kb * ATT_BLK) <= (row + i * ATT_BLK)
    return jnp.where(causal, s, NEG_INF), causal


def fox_fwd(q, k, v, cq, ck, *, name):
    H, S, _ = q.shape
    tq = tk = ATT_BLK
    qspec, kvspec, vec = _head_specs(S, tq)
    ckspec = pl.BlockSpec((None, S // tk, 1, tk), lambda h, i: (h, 0, 0, 0))

    def body(q_ref, k_ref, v_ref, cq_ref, ck_ref, o_ref, lse_ref):
        i = pl.program_id(1)
        qv, cqv = q_ref[...], cq_ref[...]
        row, col = _blk_iota(tq, tk)

        def step(kb, carry):
            m, l, acc = carry
            ks = pl.multiple_of(kb * tk, tk)
            s, _ = _fox_logits(qv, k_ref[pl.ds(ks, tk), :], cqv, ck_ref[kb], i, kb, row, col)
            m_new = jnp.maximum(m, jnp.max(s, axis=1, keepdims=True))
            alpha = jnp.exp(m - m_new)
            p = jnp.exp(s - m_new)
            l = alpha * l + jnp.sum(p, axis=1, keepdims=True)
            acc = alpha * acc + _dot(p.astype(BF16), v_ref[pl.ds(ks, tk), :])
            return m_new, l, acc

        m, l, acc = lax.fori_loop(0, i + 1, step, (jnp.full((tq, 1), NEG_INF, F32), jnp.zeros((tq, 1), F32),
                                                   jnp.zeros((tq, HEAD_DIM), F32)))
        o_ref[...] = (acc / l).astype(o_ref.dtype)
        lse_ref[...] = m + jnp.log(l)

    return pl.pallas_call(
        body, name=name, grid=(H, S // tq),
        in_specs=[qspec, kvspec, kvspec, vec, ckspec], out_specs=[qspec, vec],
        out_shape=[jax.ShapeDtypeStruct((H, S, HEAD_DIM), BF16), jax.ShapeDtypeStruct((H, S, 1), F32)],
        compiler_params=_cparams(("parallel", "arbitrary")),
    )(q, k, v, cq, ck)


def fox_bwd(q, k, v, o, lse, cq, ck, do, *, name):
    H, S, _ = q.shape
    tq = tk = ATT_BLK
    qspec, kvspec, vec = _head_specs(S, tq)
    ckspec = pl.BlockSpec((None, S // tk, 1, tk), lambda h, i: (h, 0, 0, 0))

    def body(q_ref, k_ref, v_ref, o_ref, lse_ref, cq_ref, ck_ref, do_ref, dq_ref, dk_ref, dv_ref, dcq_ref, dck_ref):
        i = pl.program_id(1)

        @pl.when(i == 0)
        def _():
            dk_ref[...] = jnp.zeros_like(dk_ref)
            dv_ref[...] = jnp.zeros_like(dv_ref)
            dck_ref[...] = jnp.zeros_like(dck_ref)

        qv, dov, cqv, lsev = q_ref[...], do_ref[...], cq_ref[...], lse_ref[...]
        delta = jnp.sum(dov.astype(F32) * o_ref[...].astype(F32), axis=1, keepdims=True)
        row, col = _blk_iota(tq, tk)

        def step(kb, carry):
            dq, dcq = carry
            ks = pl.multiple_of(kb * tk, tk)
            kv = k_ref[pl.ds(ks, tk), :]
            vv = v_ref[pl.ds(ks, tk), :]
            s, causal = _fox_logits(qv, kv, cqv, ck_ref[kb], i, kb, row, col)
            p = jnp.where(causal, jnp.exp(s - lsev), 0.0)
            ds = p * (_dot_nt(dov, vv) - delta)
            dck_ref[kb] += jnp.sum(ds, axis=0, keepdims=True)
            dsb = ds.astype(BF16)
            dk_ref[pl.ds(ks, tk), :] += _dot_tn(dsb, qv) * ATTN_SCALE
            dv_ref[pl.ds(ks, tk), :] += _dot_tn(p.astype(BF16), dov)
            return dq + _dot(dsb, kv) * ATTN_SCALE, dcq + jnp.sum(ds, axis=1, keepdims=True)

        dq, dcq = lax.fori_loop(0, i + 1, step, (jnp.zeros((tq, HEAD_DIM), F32), jnp.zeros((tq, 1), F32)))
        dq_ref[...] = dq
        dcq_ref[...] = dcq

    full = jax.ShapeDtypeStruct((H, S, HEAD_DIM), F32)
    return pl.pallas_call(
        body, name=name, grid=(H, S // tq),
        in_specs=[qspec, kvspec, kvspec, qspec, vec, vec, ckspec, qspec],
        out_specs=[qspec, kvspec, kvspec, vec, ckspec],
        out_shape=[full, full, full, jax.ShapeDtypeStruct((H, S, 1), F32),
                   jax.ShapeDtypeStruct((H, S // tk, 1, tk), F32)],
        compiler_params=_cparams(("parallel", "arbitrary")),
    )(q, k, v, o, lse, cq, ck, do)


def _swa_specs(S, tq):
    qspec = pl.BlockSpec((None, SWA_GROUP, tq, HEAD_DIM), lambda g, i: (g, 0, i, 0))
    kvspec = pl.BlockSpec((None, S + SWA_WINDOW, HEAD_DIM), lambda g, i: (g, 0, 0))
    vec = pl.BlockSpec((None, SWA_GROUP, tq, 1), lambda g, i: (g, 0, i, 0))
    sink = pl.BlockSpec((None, SWA_GROUP * tq, 1), lambda g, i: (g, 0, 0))
    return qspec, kvspec, vec, sink


def _swa_logits(q2, kw, i, tq):
    rows = q2.shape[0]
    r = lax.broadcasted_iota(jnp.int32, (rows, 2 * tq), 0)
    c = lax.broadcasted_iota(jnp.int32, (rows, 2 * tq), 1)
    diff = (r & (tq - 1)) + tq - c
    ok = (diff >= 0) & (diff < SWA_WINDOW) & (c + (i - 1) * tq >= 0)
    return jnp.where(ok, _dot_nt(q2, kw) * ATTN_SCALE, NEG_INF), ok


def swa_fwd(q, kp, vp, sink, *, name):
    _, G, S, _ = q.shape
    tq = ATT_BLK
    qspec, kvspec, vec, sinkspec = _swa_specs(S, tq)

    def body(q_ref, k_ref, v_ref, s_ref, o_ref, lse_ref):
        i = pl.program_id(1)
        q2 = q_ref[...].reshape(G * tq, HEAD_DIM)
        ws = pl.multiple_of(i * tq, tq)
        logits, _ = _swa_logits(q2, k_ref[pl.ds(ws, 2 * tq), :], i, tq)
        sk = s_ref[...]
        m = jnp.maximum(jnp.max(logits, axis=1, keepdims=True), sk)
        e = jnp.exp(logits - m)
        den = jnp.sum(e, axis=1, keepdims=True) + jnp.exp(sk - m)
        o = _dot((e / den).astype(BF16), v_ref[pl.ds(ws, 2 * tq), :])
        o_ref[...] = o.reshape(G, tq, HEAD_DIM).astype(o_ref.dtype)
        lse_ref[...] = (m + jnp.log(den)).reshape(G, tq, 1)

    return pl.pallas_call(
        body, name=name, grid=(SWA_KV_HEADS, S // tq),
        in_specs=[qspec, kvspec, kvspec, sinkspec], out_specs=[qspec, vec],
        out_shape=[jax.ShapeDtypeStruct(q.shape, BF16), jax.ShapeDtypeStruct((SWA_KV_HEADS, G, S, 1), F32)],
        compiler_params=_cparams(("parallel", "arbitrary")),
    )(q, kp, vp, sink)


def swa_bwd(q, kp, vp, sink, o, lse, do, *, name):
    _, G, S, _ = q.shape
    tq = ATT_BLK
    qspec, kvspec, vec, sinkspec = _swa_specs(S, tq)

    def body(q_ref, k_ref, v_ref, s_ref, o_ref, lse_ref, do_ref, dq_ref, dk_ref, dv_ref, dsink_ref):
        i = pl.program_id(1)

        @pl.when(i == 0)
        def _():
            dk_ref[...] = jnp.zeros_like(dk_ref)
            dv_ref[...] = jnp.zeros_like(dv_ref)

        q2 = q_ref[...].reshape(G * tq, HEAD_DIM)
        do2 = do_ref[...].reshape(G * tq, HEAD_DIM)
        o2 = o_ref[...].reshape(G * tq, HEAD_DIM)
        lse2 = lse_ref[...].reshape(G * tq, 1)
        ws = pl.multiple_of(i * tq, tq)
        kw = k_ref[pl.ds(ws, 2 * tq), :]
        vw = v_ref[pl.ds(ws, 2 * tq), :]
        logits, ok = _swa_logits(q2, kw, i, tq)
        p = jnp.where(ok, jnp.exp(logits - lse2), 0.0)
        delta = jnp.sum(do2.astype(F32) * o2.astype(F32), axis=1, keepdims=True)
        ds = p * (_dot_nt(do2, vw) - delta)
        dsb = ds.astype(BF16)
        dq_ref[...] = (_dot(dsb, kw) * ATTN_SCALE).reshape(G, tq, HEAD_DIM)
        dk_ref[pl.ds(ws, 2 * tq), :] += _dot_tn(dsb, q2) * ATTN_SCALE
        dv_ref[pl.ds(ws, 2 * tq), :] += _dot_tn(p.astype(BF16), do2)
        dsink_ref[...] = (-jnp.exp(s_ref[...] - lse2) * delta).reshape(G, tq, 1)

    kvshape = jax.ShapeDtypeStruct(kp.shape, F32)
    return pl.pallas_call(
        body, name=name, grid=(SWA_KV_HEADS, S // tq),
        in_specs=[qspec, kvspec, kvspec, sinkspec, qspec, vec, qspec],
        out_specs=[qspec, kvspec, kvspec, vec],
        out_shape=[jax.ShapeDtypeStruct(q.shape, F32), kvshape, kvshape,
                   jax.ShapeDtypeStruct((SWA_KV_HEADS, G, S, 1), F32)],
        compiler_params=_cparams(("parallel", "arbitrary")),
    )(q, kp, vp, sink, o, lse, do)


def _adamw_tile(w, g, m, v):
    m = ADAM_B1 * m + (1.0 - ADAM_B1) * g
    v = ADAM_B2 * v + (1.0 - ADAM_B2) * (g * g)
    m_hat = m / (1.0 - ADAM_B1 ** ADAM_STEP)
    v_hat = v / (1.0 - ADAM_B2 ** ADAM_STEP)
    delta = -ADAM_LR * (m_hat / (jnp.sqrt(v_hat) + ADAM_EPS) + ADAM_WD * w)
    return g, delta, m, v


def adamw(gfull, t, w, m, v, *, name):
    off, K, ns, _ = t
    sb = off // K
    nat = pl.BlockSpec((K, SLAB), lambda q: (0, q))

    def body(g_ref, w_ref, m_ref, v_ref, *outs):
        for o, r in zip(outs, _adamw_tile(w_ref[...], g_ref[...], m_ref[...], v_ref[...])):
            o[...] = r

    return pl.pallas_call(
        body, name=name, grid=(ns,),
        in_specs=[pl.BlockSpec((K, SLAB), lambda q: (sb + q, 0)), nat, nat, nat],
        out_specs=[nat] * 4, out_shape=[jax.ShapeDtypeStruct(w.shape, F32)] * 4,
        compiler_params=_cparams(("parallel",)),
    )(gfull, w, m, v)


def adamw_small(g, w, m, v, *, name):
    def body(g_ref, w_ref, m_ref, v_ref, *outs):
        for o, r in zip(outs, _adamw_tile(w_ref[...], g_ref[...], m_ref[...], v_ref[...])):
            o[...] = r

    return pl.pallas_call(body, name=name, out_shape=[jax.ShapeDtypeStruct(w.shape, F32)] * 4)(g, w, m, v)


MESH = pl.DeviceIdType.MESH
HBM = pl.BlockSpec(memory_space=pl.ANY)


def _place():
    x, y, c = lax.axis_index("x"), lax.axis_index("y"), lax.axis_index("c")
    others = [(1 - x, y), (x, 1 - y), (1 - x, 1 - y)]
    return x, y, c, others


def _rcopy(src, dst, send_sems, recv_sems, k, to):
    return pltpu.make_async_remote_copy(src_ref=src, dst_ref=dst, send_sem=send_sems.at[k], recv_sem=recv_sems.at[k],
                                        device_id=to, device_id_type=MESH)


def _dma_sems(*counts):
    return [pltpu.SemaphoreType.DMA((n,)) for n in counts]


def allgather_weights(shards, *, name):
    n = len(shards)

    def body(*refs):
        ins, outs = refs[:n], refs[n:2 * n]
        send_sems, recv_sems, local_sems = refs[2 * n:]
        x, y, c, others = _place()
        me, sibling = 2 * x + y, (x, y, 1 - c)
        local = [pltpu.make_async_copy(ins[i], outs[i].at[me], local_sems.at[i]) for i in range(n)]
        for cp in local:
            cp.start()
        first, passed = [], []
        for i in range(n):
            h = shards[i].shape[0] // 2
            mine = pl.ds(c * h, h)
            for f, (px, py) in enumerate(others):
                first.append(_rcopy(ins[i].at[mine], outs[i].at[me, mine], send_sems, recv_sems, 6 * i + f, (px, py, c)))
                first[-1].start()
        for i in range(n):
            h = shards[i].shape[0] // 2
            for f, (px, py) in enumerate(others):
                blk = outs[i].at[2 * px + py, pl.ds(c * h, h)]
                _rcopy(blk, blk, send_sems, recv_sems, 6 * i + f, (px, py, c)).wait_recv()
                passed.append(_rcopy(blk, blk, send_sems, recv_sems, 6 * i + 3 + f, sibling))
                passed[-1].start()
        for i in range(n):
            h = shards[i].shape[0] // 2
            for f, (px, py) in enumerate(others):
                blk = outs[i].at[2 * px + py, pl.ds((1 - c) * h, h)]
                _rcopy(blk, blk, send_sems, recv_sems, 6 * i + 3 + f, sibling).wait_recv()
        for cp in first + passed:
            cp.wait_send()
        for cp in local:
            cp.wait()

    return pl.pallas_call(
        body, name=name, in_specs=[HBM] * n, out_specs=[HBM] * n,
        out_shape=[jax.ShapeDtypeStruct((N_CHIPS,) + s.shape, s.dtype) for s in shards],
        scratch_shapes=_dma_sems(6 * n, 6 * n, n),
    )(*shards)


def swap_halves(grads, *, name):
    n = len(grads)

    def body(*refs):
        ins, own, theirs = refs[:n], refs[n:2 * n], refs[2 * n:3 * n]
        send_sems, recv_sems, local_sems = refs[3 * n:]
        x, y, c, _ = _place()
        cps, loc = [], []
        for i in range(n):
            h = grads[i].shape[1] // 2
            loc.append(pltpu.make_async_copy(ins[i].at[:, pl.ds(c * h, h)], own[i], local_sems.at[i]))
            loc[-1].start()
            cps.append(_rcopy(ins[i].at[:, pl.ds((1 - c) * h, h)], theirs[i], send_sems, recv_sems, i, (x, y, 1 - c)))
            cps[-1].start()
        for cp in cps + loc:
            cp.wait()

    half = [jax.ShapeDtypeStruct((N_CHIPS, g.shape[1] // 2, SLAB), g.dtype) for g in grads]
    return pl.pallas_call(body, name=name, in_specs=[HBM] * n, out_specs=[HBM] * (2 * n), out_shape=half + half,
                          scratch_shapes=_dma_sems(n, n, n))(*grads)


def scatter_chips(parts, *, name):
    n = len(parts)

    def body(*refs):
        ins, mine, got = refs[:n], refs[n:2 * n], refs[2 * n:3 * n]
        send_sems, recv_sems, local_sems = refs[3 * n:]
        x, y, c, others = _place()
        cps, loc = [], []
        for i in range(n):
            loc.append(pltpu.make_async_copy(ins[i].at[2 * x + y], mine[i], local_sems.at[i]))
            loc[-1].start()
            for f, (px, py) in enumerate(others):
                cps.append(_rcopy(ins[i].at[2 * px + py], got[i].at[f], send_sems, recv_sems, 3 * i + f, (px, py, c)))
                cps[-1].start()
        for cp in cps + loc:
            cp.wait()

    return pl.pallas_call(
        body, name=name, in_specs=[HBM] * n, out_specs=[HBM] * (2 * n),
        out_shape=[jax.ShapeDtypeStruct(p.shape[1:], p.dtype) for p in parts]
        + [jax.ShapeDtypeStruct((3,) + p.shape[1:], p.dtype) for p in parts],
        scratch_shapes=_dma_sems(3 * n, 3 * n, n),
    )(*parts)


def join_halves(halves, *, name):
    n = len(halves)

    def body(*refs):
        ins, outs = refs[:n], refs[n:2 * n]
        send_sems, recv_sems, local_sems = refs[2 * n:]
        x, y, c, _ = _place()
        sibling = (x, y, 1 - c)
        cps, loc = [], []
        for i in range(n):
            h = halves[i].shape[0]
            loc.append(pltpu.make_async_copy(ins[i], outs[i].at[pl.ds(c * h, h)], local_sems.at[i]))
            loc[-1].start()
            snd = _rcopy(ins[i], outs[i].at[pl.ds(c * h, h)], send_sems, recv_sems, i, sibling)
            snd.start()
            cps.append((snd, _rcopy(ins[i], outs[i].at[pl.ds((1 - c) * h, h)], send_sems, recv_sems, i, sibling)))
        for snd, rcv in cps:
            snd.wait_send()
            rcv.wait_recv()
        for cp in loc:
            cp.wait()

    return pl.pallas_call(
        body, name=name, in_specs=[HBM] * n, out_specs=[HBM] * n,
        out_shape=[jax.ShapeDtypeStruct((2 * hh.shape[0], SLAB), hh.dtype) for hh in halves],
        scratch_shapes=_dma_sems(n, n, n),
    )(*halves)


def allreduce_small(v, *, name):
    rows, n = v.shape

    def body(x_ref, sum_ref, all_ref, send_sems, recv_sems, local_sem):
        x, y, c, others = _place()
        me, sibling = (x, y, c), (x, y, 1 - c)

        def blk(px, py, pc):
            return all_ref.at[pl.ds((4 * px + 2 * py + pc) * rows, rows), :]

        def copy(k, block, to, src=None):
            return _rcopy(blk(*block) if src is None else src, blk(*block), send_sems, recv_sems, k, to)

        mine = pltpu.make_async_copy(x_ref, blk(*me), local_sem)
        mine.start()
        first = [copy(0, me, sibling, src=x_ref)]
        first += [copy(1 + f, me, (*chip, c), src=x_ref) for f, chip in enumerate(others)]
        for cp in first:
            cp.start()
        passed = [copy(4 + f, (*chip, c), sibling) for f, chip in enumerate(others)]
        for f, chip in enumerate(others):
            copy(1 + f, (*chip, c), me).wait_recv()
            passed[f].start()
        copy(0, sibling, me).wait_recv()
        for f, chip in enumerate(others):
            copy(4 + f, (*chip, 1 - c), me).wait_recv()
        for cp in first + passed:
            cp.wait_send()
        mine.wait()
        acc = all_ref[pl.ds(0, rows), :]
        for d in range(1, N_DEVICES):
            acc = acc + all_ref[pl.ds(d * rows, rows), :]
        sum_ref[...] = acc

    vm = pl.BlockSpec(memory_space=pltpu.VMEM)
    return pl.pallas_call(
        body, name=name, in_specs=[vm], out_specs=[vm, vm],
        out_shape=[jax.ShapeDtypeStruct((rows, n), F32), jax.ShapeDtypeStruct((N_DEVICES * rows, n), F32)],
        scratch_shapes=_dma_sems(7, 7) + [pltpu.SemaphoreType.DMA],
    )(v)[0]


def add_pairs(own, theirs, *, name):
    h = own.shape[1]
    spec = pl.BlockSpec((None, h, SLAB), lambda k: (k, 0, 0))

    def body(a_ref, b_ref, o_ref):
        o_ref[...] = (a_ref[...].astype(F32) + b_ref[...].astype(F32)).astype(o_ref.dtype)

    return pl.pallas_call(body, name=name, grid=(N_CHIPS,), in_specs=[spec, spec], out_specs=spec,
                          out_shape=jax.ShapeDtypeStruct(own.shape, own.dtype),
                          compiler_params=_cparams(("parallel",)))(own, theirs)


def add_chips(mine, got, *, name):
    h = mine.shape[0]
    tr = h // 2

    def body(a_ref, b_ref, o_ref):
        acc = a_ref[...].astype(F32)
        for f in range(3):
            acc = acc + b_ref[f].astype(F32)
        o_ref[...] = acc

    return pl.pallas_call(
        body, name=name, grid=(2,),
        in_specs=[pl.BlockSpec((tr, SLAB), lambda i: (i, 0)), pl.BlockSpec((3, tr, SLAB), lambda i: (0, i, 0))],
        out_specs=pl.BlockSpec((tr, SLAB), lambda i: (i, 0)),
        out_shape=jax.ShapeDtypeStruct((h, SLAB), F32),
        compiler_params=_cparams(("parallel",)))(mine, got)


DEPTH = 4
MIXER = (0, 1, 2, 0)
W_IN_COLS = (768, 320, 772)
W_IN_PAD = (768, 512, 1024)
MATS = ("up", "down", "inp", "out", "gate", "proj")
MAT_ARG = dict(up="w_up", down="w_down", inp="w_in", out="w_out", gate="w_ple_gate", proj="w_ple_proj")
GAINS = ("attn_norm", "mlp_norm", "ple_norm")
N_SMALL = 16
KINDS = ("grad_", "delta_", "new_m_", "new_v_")


def _layout(kind):
    ns_in = W_IN_PAD[kind] // SLAB
    off = 8192 + 1024 * ns_in
    lay = dict(up=(0, 1024, 4, False), down=(4096, 1024, 4, True), inp=(8192, 1024, ns_in, False),
               out=(off, 256, 4, True), gate=(off + 1024, 256, 4, True), proj=(off + 2048, 256, 1, False))
    return lay, off + 2304


def _to_slabs(w):
    k, c = w.shape
    return w.reshape(k, c // SLAB, SLAB).transpose(1, 0, 2).reshape(-1, SLAB)


def _pad_cols(w, n):
    return jnp.pad(w, ((0, 0), (0, n - w.shape[1])))


def _heads(x2d, n):
    return x2d.reshape(x2d.shape[0], n, HEAD_DIM).transpose(1, 0, 2)


def _unheads(x3d):
    n, s, _ = x3d.shape
    return x3d.transpose(1, 0, 2).reshape(s, n * HEAD_DIM)


def _chip_cols(x2d, c, cpad):
    s = x2d.shape[0]
    return jnp.pad(x2d.reshape(s, N_CHIPS, c), ((0, 0), (0, 0), (0, cpad - c))).reshape(s, N_CHIPS * cpad)


def _unchip_cols(x2d, c, cpad):
    s = x2d.shape[0]
    return x2d.reshape(s, N_CHIPS, cpad)[:, :, :c].reshape(s, N_CHIPS * c)


def _add_res(acc, res):
    return (acc + res,)


def _relu2(acc):
    return acc, jnp.square(jnp.maximum(acc, 0.0))


def _relu2_bwd(acc, u):
    return (acc * (2.0 * jnp.maximum(u.astype(F32), 0.0)),)


def _ple_fwd(acc, x2, pp):
    return x2 + pp * _sigmoid(acc), acc


def _ple_bwd(dx, pp, gl):
    gate = _sigmoid(gl)
    return dx * gate, dx * pp * gate * (1.0 - gate)


def _layer_fwd(i, kind, x0, p_bf, wg, lay, gains, extra, tabs):
    s = x0.shape[0]
    an, mn, pn = gains
    sv = dict(x0=x0)
    h1 = rms_fwd(x0, an, name=f"attn_norm_{i}")
    if kind == 0:
        proj = mm_nn(h1, wg, lay["inp"], name=f"w_in_{i}")[0]
        qkv = proj.reshape(s, 3, N_HEADS, HEAD_DIM).transpose(1, 2, 0, 3)
        o, tot = sb_fwd(qkv[0], qkv[1], qkv[2], name=f"sb_fwd_{i}")
        sv.update(qkv=qkv, tot=tot)
    elif kind == 1:
        projp = mm_nn(h1, wg, lay["inp"], name=f"w_in_{i}", out_dtypes=(F32,))[0]
        proj = _unchip_cols(projp, W_IN_COLS[1], W_IN_PAD[1])
        nq = N_HEADS * HEAD_DIM
        nqk = nq + SWA_KV_HEADS * HEAD_DIM
        qk = rope_fwd(proj[:, :nqk], tabs, name=f"rope_{i}")
        q = _heads(qk[:, :nq], N_HEADS).reshape(SWA_KV_HEADS, SWA_GROUP, s, HEAD_DIM)
        front = ((0, 0), (SWA_WINDOW, 0), (0, 0))
        kp = jnp.pad(_heads(qk[:, nq:], SWA_KV_HEADS), front)
        vp = jnp.pad(_heads(proj[:, nqk:].astype(BF16), SWA_KV_HEADS), front)
        sink = jnp.repeat(extra.reshape(SWA_KV_HEADS, SWA_GROUP), ATT_BLK, axis=1)[:, :, None]
        o4, lse = swa_fwd(q, kp, vp, sink, name=f"swa_fwd_{i}")
        o = o4.reshape(N_HEADS, s, HEAD_DIM)
        sv.update(q=q, kp=kp, vp=vp, sink=sink, o4=o4, lse=lse)
    else:
        projp = mm_nn(h1, wg, lay["inp"], name=f"w_in_{i}", out_dtypes=(F32,))[0]
        proj = _unchip_cols(projp, W_IN_COLS[2], W_IN_PAD[2])
        nqkv = 3 * N_HEADS * HEAD_DIM
        qkv = proj[:, :nqkv].astype(BF16).reshape(s, 3, N_HEADS, HEAD_DIM).transpose(1, 2, 0, 3)
        fl = _pad_cols(proj[:, nqkv:], 128)
        bp = _pad_cols(extra[None], 128)
        cum_t = fox_gate_fwd(fl, bp, name=f"gate_fwd_{i}")[:, :N_HEADS].T
        cq = cum_t[:, :, None]
        ck = cum_t.reshape(N_HEADS, s // ATT_BLK, 1, ATT_BLK)
        o, lse = fox_fwd(qkv[0], qkv[1], qkv[2], cq, ck, name=f"fox_fwd_{i}")
        sv.update(qkv=qkv, fl=fl, bp=bp, cq=cq, ck=ck, o=o, lse=lse)
    a = _unheads(o)
    x1 = mm_nn(a, wg, lay["out"], name=f"w_out_{i}", epi=_add_res, extras=(x0,), out_dtypes=(F32,))[0]
    h2 = rms_fwd(x1, mn, name=f"mlp_norm_{i}")
    u, r = mm_nn(h2, wg, lay["up"], name=f"w_up_{i}", epi=_relu2, out_dtypes=(BF16, BF16))
    x2 = mm_nn(r, wg, lay["down"], name=f"w_down_{i}", epi=_add_res, extras=(x1,), out_dtypes=(F32,))[0]
    h3 = rms_fwd(x2, pn, name=f"ple_norm_{i}")
    pp = mm_nn(p_bf, wg, lay["proj"], name=f"w_ple_proj_{i}", out_dtypes=(F32,))[0]
    x3, gl = mm_nn(h3, wg, lay["gate"], name=f"w_ple_gate_{i}", epi=_ple_fwd, extras=(x2, pp), out_dtypes=(F32, F32))
    sv.update(h1=h1, a=a, x1=x1, h2=h2, u=u, r=r, x2=x2, h3=h3, pp=pp, gl=gl)
    return x3, sv


def _layer_bwd(i, kind, dx3, sv, p_bf, wg, lay, n_rows, gains, tabs):
    s = dx3.shape[0]
    an, mn, pn = gains
    g = lax.empty((N_CHIPS, n_rows, SLAB), BF16)
    d_pp, d_gl = ew(_ple_bwd, [dx3, sv["pp"], sv["gl"]], [BF16, BF16], name=f"ple_bwd_{i}")
    g = mm_tn(p_bf, d_pp, g, lay["proj"], name=f"dw_ple_proj_{i}")
    g = mm_tn(sv["h3"], d_gl, g, lay["gate"], name=f"dw_ple_gate_{i}")
    d_h3 = mm_nt(d_gl, wg, lay["gate"], name=f"dx_ple_gate_{i}", out_dtypes=(F32,))[0]
    dx2, dx2b, d_pn = rms_bwd(sv["x2"], pn, d_h3, dx3, name=f"ple_norm_bwd_{i}")
    g = mm_tn(sv["r"], dx2b, g, lay["down"], name=f"dw_down_{i}")
    d_u = mm_nt(dx2b, wg, lay["down"], name=f"dx_down_{i}", epi=_relu2_bwd, extras=(sv["u"],))[0]
    g = mm_tn(sv["h2"], d_u, g, lay["up"], name=f"dw_up_{i}")
    d_h2 = mm_nt(d_u, wg, lay["up"], name=f"dx_up_{i}", out_dtypes=(F32,))[0]
    dx1, dx1b, d_mn = rms_bwd(sv["x1"], mn, d_h2, dx2, name=f"mlp_norm_bwd_{i}")
    g = mm_tn(sv["a"], dx1b, g, lay["out"], name=f"dw_out_{i}")
    d_a = mm_nt(dx1b, wg, lay["out"], name=f"dx_out_{i}")[0]
    do = _heads(d_a, N_HEADS)
    d_extra = None
    if kind == 0:
        qkv = sv["qkv"]
        dq, dk, dv = sb_bwd(qkv[0], qkv[1], qkv[2], sv["tot"], do, name=f"sb_bwd_{i}")
        d_proj = jnp.stack([dq, dk, dv]).transpose(2, 0, 1, 3).reshape(s, 3 * N_HEADS * HEAD_DIM).astype(BF16)
    elif kind == 1:
        do4 = do.reshape(SWA_KV_HEADS, SWA_GROUP, s, HEAD_DIM)
        dq, dkp, dvp, dsr = swa_bwd(sv["q"], sv["kp"], sv["vp"], sv["sink"], sv["o4"], sv["lse"], do4, name=f"swa_bwd_{i}")
        dqk = jnp.concatenate([_unheads(dq.reshape(N_HEADS, s, HEAD_DIM)), _unheads(dkp[:, SWA_WINDOW:])], axis=1)
        dqk = rope_bwd(dqk, tabs, name=f"rope_bwd_{i}")
        d_proj = jnp.concatenate([dqk, _unheads(dvp[:, SWA_WINDOW:]).astype(BF16)], axis=1)
        d_proj = _chip_cols(d_proj, W_IN_COLS[1], W_IN_PAD[1])
        d_extra = jnp.sum(dsr[..., 0], axis=2).reshape(N_HEADS)
    else:
        qkv = sv["qkv"]
        dq, dk, dv, dcq, dck = fox_bwd(qkv[0], qkv[1], qkv[2], sv["o"], sv["lse"], sv["cq"], sv["ck"], do, name=f"fox_bwd_{i}")
        dcum = _pad_cols((dcq[:, :, 0] - dck.reshape(N_HEADS, s)).T, 128)
        dfl, dbp = fox_gate_bwd(dcum, sv["fl"], sv["bp"], name=f"gate_bwd_{i}")
        d_qkv = jnp.stack([dq, dk, dv]).transpose(2, 0, 1, 3).reshape(s, 3 * N_HEADS * HEAD_DIM)
        d_proj = jnp.concatenate([d_qkv, dfl[:, :N_HEADS]], axis=1).astype(BF16)
        d_proj = _chip_cols(d_proj, W_IN_COLS[2], W_IN_PAD[2])
        d_extra = dbp[0, :N_HEADS]
    g = mm_tn(sv["h1"], d_proj, g, lay["inp"], name=f"dw_in_{i}")
    d_h1 = mm_nt(d_proj, wg, lay["inp"], name=f"dx_in_{i}", out_dtypes=(F32,))[0]
    dx0, _, d_an = rms_bwd(sv["x0"], an, d_h1, dx1, name=f"attn_norm_bwd_{i}")
    return dx0, g, (d_an, d_mn, d_pn), d_extra


def _small_rows(a, prefix):
    rows = [a[f"{prefix}{n}_{i}"] for i in range(DEPTH) for n in GAINS] + [a[f"{prefix}final_norm"]]
    rows += [_pad_cols(a[f"{prefix}{n}"][None], D_MODEL)[0] for n in ("sinks_1", "b_forget_2")]
    return jnp.stack(rows + [jnp.zeros((D_MODEL,), F32)])


def _train_step(a):
    x = a["x"][0]
    tabs = rope_tables(x.shape[0], (N_HEADS + SWA_KV_HEADS) * HEAD_DIM)
    lays = [_layout(k) for k in MIXER]

    def natural(prefix, i, m):
        w = a[f"{prefix}{MAT_ARG[m]}_{i}"]
        return _pad_cols(w, W_IN_PAD[MIXER[i]]) if m == "inp" else w

    packed = [jnp.concatenate([_to_slabs(natural("", i, m).astype(BF16)) for m in MATS], axis=0) for i in range(DEPTH)]
    wgs = allgather_weights(packed, name="allgather_weights")

    gains = [tuple(a[f"{n}_{i}"][None] for n in GAINS) for i in range(DEPTH)]
    extras = [None, a["sinks_1"], a["b_forget_2"], None]
    p_bf = [a["p"][i, 0].astype(BF16) for i in range(DEPTH)]

    saved = []
    for i in range(DEPTH):
        x, sv = _layer_fwd(i, MIXER[i], x, p_bf[i], wgs[i], lays[i][0], gains[i], extras[i], tabs)
        saved.append(sv)
    dx, d_final, loss = loss_head(x, a["final_norm"][None], a["loss_target"][0], name="loss_head")

    small = [None] * N_SMALL
    small[12] = d_final[0]
    small[15] = _pad_cols(loss[:, :1], D_MODEL)[0]
    grads = [None] * DEPTH
    for i in reversed(range(DEPTH)):
        dx, grads[i], d_gains, d_extra = _layer_bwd(i, MIXER[i], dx, saved[i], p_bf[i], wgs[i], lays[i][0], lays[i][1],
                                                    gains[i], tabs)
        for j in range(3):
            small[3 * i + j] = d_gains[j][0]
        if d_extra is not None:
            small[12 + MIXER[i]] = _pad_cols(d_extra[None], D_MODEL)[0]
    small = allreduce_small(jnp.stack(small), name="allreduce_small")

    halves = swap_halves(grads, name="swap_halves")
    pair = [add_pairs(halves[i], halves[DEPTH + i], name=f"add_pairs_{i}") for i in range(DEPTH)]
    parts = scatter_chips(pair, name="scatter_chips")
    summed = [add_chips(parts[i], parts[DEPTH + i], name=f"add_chips_{i}") for i in range(DEPTH)]
    gfull = join_halves(summed, name="join_halves")

    out = {"loss": small[15, 0], "grad_x": dx[None]}
    for i in range(DEPTH):
        for m in MATS:
            res = adamw(gfull[i], lays[i][0][m], natural("", i, m), natural("m_", i, m), natural("v_", i, m),
                        name=f"adamw_{MAT_ARG[m]}_{i}")
            cols = a[f"{MAT_ARG[m]}_{i}"].shape[1]
            for kd, r in zip(KINDS, res):
                out[f"{kd}{MAT_ARG[m]}_{i}"] = r[:, :cols]
    res = adamw_small(small, _small_rows(a, ""), _small_rows(a, "m_"), _small_rows(a, "v_"), name="adamw_small")
    for kd, r in zip(KINDS, res):
        for i in range(DEPTH):
            for j, n in enumerate(GAINS):
                out[f"{kd}{n}_{i}"] = r[3 * i + j]
        out[f"{kd}final_norm"] = r[12]
        out[f"{kd}sinks_1"] = r[13, :N_HEADS]
        out[f"{kd}b_forget_2"] = r[14, :N_HEADS]
    return out


def _weight_names():
    names = []
    for i in range(DEPTH):
        names += [f"attn_norm_{i}", f"w_in_{i}", f"w_out_{i}"] + [[], ["sinks_1"], ["b_forget_2"]][MIXER[i]]
        names += [f"mlp_norm_{i}", f"w_up_{i}", f"w_down_{i}", f"ple_norm_{i}", f"w_ple_gate_{i}", f"w_ple_proj_{i}"]
    return names + ["final_norm"]


def kernel(x, p, attn_norm_0, w_in_0, w_out_0, mlp_norm_0, w_up_0, w_down_0, ple_norm_0, w_ple_gate_0, w_ple_proj_0, attn_norm_1, w_in_1, w_out_1, sinks_1, mlp_norm_1, w_up_1, w_down_1, ple_norm_1, w_ple_gate_1, w_ple_proj_1, attn_norm_2, w_in_2, w_out_2, b_forget_2, mlp_norm_2, w_up_2, w_down_2, ple_norm_2, w_ple_gate_2, w_ple_proj_2, attn_norm_3, w_in_3, w_out_3, mlp_norm_3, w_up_3, w_down_3, ple_norm_3, w_ple_gate_3, w_ple_proj_3, final_norm, loss_target, m_attn_norm_0, m_w_in_0, m_w_out_0, m_mlp_norm_0, m_w_up_0, m_w_down_0, m_ple_norm_0, m_w_ple_gate_0, m_w_ple_proj_0, m_attn_norm_1, m_w_in_1, m_w_out_1, m_sinks_1, m_mlp_norm_1, m_w_up_1, m_w_down_1, m_ple_norm_1, m_w_ple_gate_1, m_w_ple_proj_1, m_attn_norm_2, m_w_in_2, m_w_out_2, m_b_forget_2, m_mlp_norm_2, m_w_up_2, m_w_down_2, m_ple_norm_2, m_w_ple_gate_2, m_w_ple_proj_2, m_attn_norm_3, m_w_in_3, m_w_out_3, m_mlp_norm_3, m_w_up_3, m_w_down_3, m_ple_norm_3, m_w_ple_gate_3, m_w_ple_proj_3, m_final_norm, v_attn_norm_0, v_w_in_0, v_w_out_0, v_mlp_norm_0, v_w_up_0, v_w_down_0, v_ple_norm_0, v_w_ple_gate_0, v_w_ple_proj_0, v_attn_norm_1, v_w_in_1, v_w_out_1, v_sinks_1, v_mlp_norm_1, v_w_up_1, v_w_down_1, v_ple_norm_1, v_w_ple_gate_1, v_w_ple_proj_1, v_attn_norm_2, v_w_in_2, v_w_out_2, v_b_forget_2, v_mlp_norm_2, v_w_up_2, v_w_down_2, v_ple_norm_2, v_w_ple_gate_2, v_w_ple_proj_2, v_attn_norm_3, v_w_in_3, v_w_out_3, v_mlp_norm_3, v_w_up_3, v_w_down_3, v_ple_norm_3, v_w_ple_gate_3, v_w_ple_proj_3, v_final_norm):
    out = _train_step(dict(locals()))
    return (out["loss"], out["grad_x"], *[out[kd + n] for kd in KINDS for n in _weight_names()])
```

```python
import jax
import jax.numpy as jnp
from jax import lax
from jax.experimental import pallas as pl
from jax.experimental.pallas import tpu as pltpu

F32 = jnp.float32
BF16 = jnp.bfloat16

D_MODEL = 1024
N_HEADS = 16
HEAD_DIM = 64
SWA_KV_HEADS = 2
SWA_GROUP = 8
SWA_WINDOW = 128
ROPE_THETA = 500000.0
ROPE_DIM = 16
RMS_EPS = 1e-6
NEG_INF = -1e30
ATTN_SCALE = HEAD_DIM ** -0.5
N_CHIPS = 4
N_DEVICES = 8

SLAB = 256
ATT_BLK = 128
ATT_BQ = 512
ROW_TILE = 256
V7X_VMEM_LIMIT = 56 * 1024 * 1024

ADAM_LR, ADAM_B1, ADAM_B2, ADAM_EPS, ADAM_WD, ADAM_STEP = 0.001, 0.9, 0.999, 1e-08, 0.01, 10


def _cparams(sem=None):
    return pltpu.CompilerParams(dimension_semantics=sem, vmem_limit_bytes=V7X_VMEM_LIMIT)


def _dot(a, b):
    return jnp.dot(a, b, preferred_element_type=F32)


def _dot_nt(a, b):
    return lax.dot_general(a, b, (((1,), (1,)), ((), ())), preferred_element_type=F32)


def _dot_tn(a, b):
    return lax.dot_general(a, b, (((0,), (0,)), ((), ())), preferred_element_type=F32)


def mm_nn(a, wg, t, *, name, epi=None, extras=(), out_dtypes=(BF16,)):
    off, K, ns, row = t
    M = a.shape[0]
    sb = off // K
    ne, no = len(extras), len(out_dtypes)
    if row:
        grid = (ns, N_CHIPS)
        a_map = lambda q, j: (0, j)
        b_map = lambda q, j: (j, sb + q, 0)
        o_map = lambda q, j: (0, q)
        n_out = ns * SLAB
        sem = ("parallel", "arbitrary")
    else:
        grid = (N_CHIPS, ns)
        a_map = lambda j, q: (0, 0)
        b_map = lambda j, q: (j, sb + q, 0)
        o_map = lambda j, q: (0, j * ns + q)
        n_out = N_CHIPS * ns * SLAB
        sem = ("parallel", "parallel")

    def body(a_ref, b_ref, *rest):
        ex, outs = rest[:ne], rest[ne:ne + no]
        part = _dot(a_ref[...], b_ref[...])

        def finish(acc):
            res = epi(acc, *[e[...] for e in ex]) if epi is not None else (acc,)
            for o, r in zip(outs, res):
                o[...] = r.astype(o.dtype)

        if row:
            acc_ref = rest[-1]
            j = pl.program_id(1)

            @pl.when(j == 0)
            def _():
                acc_ref[...] = part

            @pl.when(j > 0)
            def _():
                acc_ref[...] += part

            @pl.when(j == N_CHIPS - 1)
            def _():
                finish(acc_ref[...])
        else:
            finish(part)

    tile = pl.BlockSpec((M, SLAB), o_map)
    return pl.pallas_call(
        body, name=name, grid=grid,
        in_specs=[pl.BlockSpec((M, K), a_map), pl.BlockSpec((None, K, SLAB), b_map)] + [tile] * ne,
        out_specs=[tile] * no,
        out_shape=[jax.ShapeDtypeStruct((M, n_out), d) for d in out_dtypes],
        scratch_shapes=[pltpu.VMEM((M, SLAB), F32)] if row else [],
        compiler_params=_cparams(sem),
    )(a, wg, *extras)


def mm_nt(dy, wg, t, *, name, epi=None, extras=(), out_dtypes=(BF16,)):
    off, K, ns, row = t
    M = dy.shape[0]
    tm = min(1024, M)
    sb = off // K
    ne, no = len(extras), len(out_dtypes)
    grid = (M // tm, N_CHIPS, ns)
    b_map = lambda i, j, q: (j, sb + q, 0)
    if row:
        dy_map = lambda i, j, q: (i, q)
        o_map = lambda i, j, q: (i, j)
        n_out = N_CHIPS * K
        sem = ("parallel", "parallel", "arbitrary")
    else:
        dy_map = lambda i, j, q: (i, j * ns + q)
        o_map = lambda i, j, q: (i, 0)
        n_out = K
        sem = ("parallel", "arbitrary", "arbitrary")

    def body(dy_ref, b_ref, *rest):
        ex, outs, acc_ref = rest[:ne], rest[ne:ne + no], rest[-1]
        j, q = pl.program_id(1), pl.program_id(2)
        if row:
            first, last = q == 0, q == ns - 1
        else:
            first = jnp.logical_and(j == 0, q == 0)
            last = jnp.logical_and(j == N_CHIPS - 1, q == ns - 1)
        part = _dot_nt(dy_ref[...], b_ref[...])

        @pl.when(first)
        def _():
            acc_ref[...] = part

        @pl.when(jnp.logical_not(first))
        def _():
            acc_ref[...] += part

        @pl.when(last)
        def _():
            acc = acc_ref[...]
            res = epi(acc, *[e[...] for e in ex]) if epi is not None else (acc,)
            for o, r in zip(outs, res):
                o[...] = r.astype(o.dtype)

    tile = pl.BlockSpec((tm, K), o_map)
    return pl.pallas_call(
        body, name=name, grid=grid,
        in_specs=[pl.BlockSpec((tm, SLAB), dy_map), pl.BlockSpec((None, K, SLAB), b_map)] + [tile] * ne,
        out_specs=[tile] * no,
        out_shape=[jax.ShapeDtypeStruct((M, n_out), d) for d in out_dtypes],
        scratch_shapes=[pltpu.VMEM((tm, K), F32)],
        compiler_params=_cparams(sem),
    )(dy, wg, *extras)


def mm_tn(x, dy, g, t, *, name):
    off, K, ns, row = t
    S = x.shape[0]
    sb = off // K
    if row:
        x_map = lambda j, q: (0, j)
        dy_map = lambda j, q: (0, q)
    else:
        x_map = lambda j, q: (0, 0)
        dy_map = lambda j, q: (0, j * ns + q)

    def body(g_in, x_ref, dy_ref, o_ref):
        del g_in
        o_ref[...] = _dot_tn(x_ref[...], dy_ref[...]).astype(o_ref.dtype)

    return pl.pallas_call(
        body, name=name, grid=(N_CHIPS, ns),
        in_specs=[pl.BlockSpec(memory_space=pl.ANY), pl.BlockSpec((S, K), x_map), pl.BlockSpec((S, SLAB), dy_map)],
        out_specs=pl.BlockSpec((None, K, SLAB), lambda j, q: (j, sb + q, 0)),
        out_shape=jax.ShapeDtypeStruct(g.shape, g.dtype),
        input_output_aliases={0: 0},
        compiler_params=_cparams(("parallel", "parallel")),
    )(g, x, dy)


def ew(fn, ins, out_dtypes, *, name, bcast=()):
    S = ins[0].shape[0]
    tr = min(ROW_TILE, S)
    cols = ins[0].shape[1]
    ni, nb = len(ins), len(bcast)

    def body(*refs):
        res = fn(*[r[...] for r in refs[:ni + nb]])
        for o, r in zip(refs[ni + nb:], res):
            o[...] = r.astype(o.dtype)

    return pl.pallas_call(
        body, name=name, grid=(S // tr,),
        in_specs=[pl.BlockSpec((tr, a.shape[1]), lambda i: (i, 0)) for a in ins]
        + [pl.BlockSpec(b.shape, lambda i: (0, 0)) for b in bcast],
        out_specs=[pl.BlockSpec((tr, cols), lambda i: (i, 0)) for _ in out_dtypes],
        out_shape=[jax.ShapeDtypeStruct((S, cols), d) for d in out_dtypes],
        compiler_params=_cparams(("parallel",)),
    )(*ins, *bcast)


def _rstd(x):
    return lax.rsqrt(jnp.mean(x * x, axis=-1, keepdims=True) + RMS_EPS)


def _sigmoid(x):
    return 1.0 / (1.0 + jnp.exp(-x))


def _log_sigmoid(z):
    return jnp.minimum(z, 0.0) - jnp.log(1.0 + jnp.exp(-jnp.abs(z)))


def rms_fwd(x, g, *, name):
    return ew(lambda xv, gv: (xv * _rstd(xv) * gv,), [x], [BF16], name=name, bcast=[g])[0]


def _rms_bwd_tile(xv, gv, dh):
    rstd = _rstd(xv)
    xhat = xv * rstd
    gd = dh * gv
    dx = rstd * (gd - xhat * jnp.mean(xhat * gd, axis=-1, keepdims=True))
    return dx, jnp.sum(dh * xhat, axis=0, keepdims=True)


def rms_bwd(x, g, dh, dres, *, name):
    S, D = x.shape
    tr = min(ROW_TILE, S)

    def body(x_ref, g_ref, dh_ref, dres_ref, dx_ref, dxb_ref, dg_ref):
        i = pl.program_id(0)
        dx, dg = _rms_bwd_tile(x_ref[...], g_ref[...], dh_ref[...])
        dx = dx + dres_ref[...]
        dx_ref[...] = dx
        dxb_ref[...] = dx.astype(BF16)

        @pl.when(i == 0)
        def _():
            dg_ref[...] = dg

        @pl.when(i > 0)
        def _():
            dg_ref[...] += dg

    row = pl.BlockSpec((tr, D), lambda i: (i, 0))
    one = pl.BlockSpec((1, D), lambda i: (0, 0))
    return pl.pallas_call(
        body, name=name, grid=(S // tr,),
        in_specs=[row, one, row, row], out_specs=[row, row, one],
        out_shape=[jax.ShapeDtypeStruct((S, D), F32), jax.ShapeDtypeStruct((S, D), BF16),
                   jax.ShapeDtypeStruct((1, D), F32)],
        compiler_params=_cparams(("arbitrary",)),
    )(x, g, dh, dres)


def loss_head(x, g, target, *, name):
    S, D = x.shape
    tr = min(ROW_TILE, S)

    def body(x_ref, g_ref, t_ref, dx_ref, dg_ref, loss_ref):
        i = pl.program_id(0)
        xv, gv = x_ref[...], g_ref[...]
        err = xv * _rstd(xv) * gv - t_ref[...]
        part = 0.5 * jnp.sum(jnp.mean(err * err, axis=-1, keepdims=True), axis=0, keepdims=True)
        dx, dg = _rms_bwd_tile(xv, gv, err * (1.0 / D))
        dx_ref[...] = dx
        part = jnp.broadcast_to(part, loss_ref.shape)

        @pl.when(i == 0)
        def _():
            dg_ref[...] = dg
            loss_ref[...] = part

        @pl.when(i > 0)
        def _():
            dg_ref[...] += dg
            loss_ref[...] += part

    row = pl.BlockSpec((tr, D), lambda i: (i, 0))
    one = pl.BlockSpec((1, D), lambda i: (0, 0))
    return pl.pallas_call(
        body, name=name, grid=(S // tr,),
        in_specs=[row, one, row], out_specs=[row, one, pl.BlockSpec((1, 128), lambda i: (0, 0))],
        out_shape=[jax.ShapeDtypeStruct((S, D), F32), jax.ShapeDtypeStruct((1, D), F32),
                   jax.ShapeDtypeStruct((1, 128), F32)],
        compiler_params=_cparams(("arbitrary",)),
    )(x, g, target)


def rope_tables(S, n_cols):
    half = ROPE_DIM // 2
    inv_freq = ROPE_THETA ** (-jnp.arange(half, dtype=F32) / half)
    ang = jnp.arange(S, dtype=F32)[:, None] * inv_freq[None, :]
    cos, sin = jnp.cos(ang), jnp.sin(ang)
    z = jnp.zeros((S, HEAD_DIM - ROPE_DIM), F32)
    zh = jnp.zeros((S, half), F32)
    c = jnp.concatenate([cos, cos, jnp.ones_like(z)], axis=1)
    sa = jnp.concatenate([zh, sin, z], axis=1)
    sb = jnp.concatenate([-sin, zh, z], axis=1)
    return [jnp.tile(t, (1, n_cols // HEAD_DIM)) for t in (c, sa, sb)]


def rope_fwd(xqk, tables, *, name):
    n, half = xqk.shape[1], ROPE_DIM // 2

    def fn(x, c, sa, sb):
        return (x * c + pltpu.roll(x, half, 1) * sa + pltpu.roll(x, n - half, 1) * sb,)

    return ew(fn, [xqk] + list(tables), [BF16], name=name)[0]


def rope_bwd(dy, tables, *, name):
    n, half = dy.shape[1], ROPE_DIM // 2

    def fn(d, c, sa, sb):
        return (d * c + pltpu.roll(d * sa, n - half, 1) + pltpu.roll(d * sb, half, 1),)

    return ew(fn, [dy] + list(tables), [BF16], name=name)[0]


def _split3(x):
    h1 = x.astype(BF16)
    r1 = x - h1.astype(F32)
    h2 = r1.astype(BF16)
    return h1, h2, (r1 - h2.astype(F32)).astype(BF16)


def _split2(x):
    h1 = x.astype(BF16)
    return h1, (x - h1.astype(F32)).astype(BF16)


def _tri(n, cmp):
    r = lax.broadcasted_iota(jnp.int32, (n, n), 0)
    c = lax.broadcasted_iota(jnp.int32, (n, n), 1)
    return cmp(r, c).astype(BF16)


def fox_gate_fwd(fl, b, *, name):
    S, W = fl.shape
    tr = min(ROW_TILE, S)

    def body(fl_ref, b_ref, cum_ref, carry):
        i = pl.program_id(0)

        @pl.when(i == 0)
        def _():
            carry[...] = jnp.zeros_like(carry)

        lower = _tri(tr, lambda r, c: r >= c)
        cs = carry[...]
        for piece in _split3(_log_sigmoid(fl_ref[...] + b_ref[...])):
            cs = cs + _dot(lower, piece)
        cum_ref[...] = cs
        carry[...] = cs[tr - 1:tr, :]

    return pl.pallas_call(
        body, name=name, grid=(S // tr,),
        in_specs=[pl.BlockSpec((tr, W), lambda i: (i, 0)), pl.BlockSpec((1, W), lambda i: (0, 0))],
        out_specs=pl.BlockSpec((tr, W), lambda i: (i, 0)),
        out_shape=jax.ShapeDtypeStruct((S, W), F32),
        scratch_shapes=[pltpu.VMEM((1, W), F32)],
        compiler_params=_cparams(("arbitrary",)),
    )(fl, b)


def fox_gate_bwd(dcum, fl, b, *, name):
    S, W = fl.shape
    tr = min(ROW_TILE, S)
    nb = S // tr

    def body(dc_ref, fl_ref, b_ref, dfl_ref, db_ref, carry):
        i = pl.program_id(0)

        @pl.when(i == 0)
        def _():
            carry[...] = jnp.zeros_like(carry)

        upper = _tri(tr, lambda r, c: r <= c)
        cs = carry[...]
        for piece in _split3(dc_ref[...]):
            cs = cs + _dot(upper, piece)
        carry[...] = cs[0:1, :]
        dfl = cs * _sigmoid(-(fl_ref[...] + b_ref[...]))
        dfl_ref[...] = dfl
        db = jnp.sum(dfl, axis=0, keepdims=True)

        @pl.when(i == 0)
        def _():
            db_ref[...] = db

        @pl.when(i > 0)
        def _():
            db_ref[...] += db

    rev = pl.BlockSpec((tr, W), lambda i: (nb - 1 - i, 0))
    one = pl.BlockSpec((1, W), lambda i: (0, 0))
    return pl.pallas_call(
        body, name=name, grid=(nb,),
        in_specs=[rev, rev, one], out_specs=[rev, one],
        out_shape=[jax.ShapeDtypeStruct((S, W), F32), jax.ShapeDtypeStruct((1, W), F32)],
        scratch_shapes=[pltpu.VMEM((1, W), F32)],
        compiler_params=_cparams(("arbitrary",)),
    )(dcum, fl, b)


def _blk_iota(tq, tk):
    return (lax.broadcasted_iota(jnp.int32, (tq, tk), 0), lax.broadcasted_iota(jnp.int32, (tq, tk), 1))


def _cs2(x, tri):
    h1, h2 = _split2(x)
    return _dot(h1, tri) + _dot(h2, tri)


def _sb_block(q, k, i, kb, row, col):
    z = _dot_nt(q, k) * ATTN_SCALE
    strict = (col + kb * ATT_BLK) < (row + i * q.shape[0])
    sp = jnp.log(1.0 + jnp.exp(-jnp.abs(z)))
    lb = jnp.minimum(z, 0.0) - sp
    lom = jnp.where(strict, jnp.minimum(-z, 0.0) - sp, 0.0)
    return lb, lom, strict


def _head_specs(S, tq):
    qspec = pl.BlockSpec((None, tq, HEAD_DIM), lambda h, i: (h, i, 0))
    kvspec = pl.BlockSpec((None, S, HEAD_DIM), lambda h, i: (h, 0, 0))
    vec = pl.BlockSpec((None, tq, 1), lambda h, i: (h, i, 0))
    return qspec, kvspec, vec


def sb_fwd(q, k, v, *, name):
    H, S, _ = q.shape
    tq, tk = min(ATT_BQ, S), ATT_BLK
    qspec, kvspec, vec = _head_specs(S, tq)

    def body(q_ref, k_ref, v_ref, o_ref, t_ref):
        i = pl.program_id(1)
        qv = q_ref[...]
        row, col = _blk_iota(tq, tk)
        below = _tri(tk, lambda r, c: r > c)

        def step(n, carry):
            r_sum, acc = carry
            kb = (i + 1) * (tq // tk) - 1 - n
            ks = pl.multiple_of(kb * tk, tk)
            lb, lom, strict = _sb_block(qv, k_ref[pl.ds(ks, tk), :], i, kb, row, col)
            tail = _cs2(lom, below) + r_sum
            w = jnp.where(strict, jnp.exp(lb + tail), 0.0)
            acc = acc + _dot(w.astype(BF16), v_ref[pl.ds(ks, tk), :])
            return r_sum + jnp.sum(lom, axis=1, keepdims=True), acc

        r_sum, acc = lax.fori_loop(0, (i + 1) * (tq // tk), step, (jnp.zeros((tq, 1), F32), jnp.zeros((tq, HEAD_DIM), F32)))
        o_ref[...] = acc.astype(o_ref.dtype)
        t_ref[...] = r_sum

    return pl.pallas_call(
        body, name=name, grid=(H, S // tq),
        in_specs=[qspec, kvspec, kvspec], out_specs=[qspec, vec],
        out_shape=[jax.ShapeDtypeStruct((H, S, HEAD_DIM), BF16), jax.ShapeDtypeStruct((H, S, 1), F32)],
        compiler_params=_cparams(("parallel", "arbitrary")),
    )(q, k, v)


def sb_bwd(q, k, v, tot, do, *, name):
    H, S, _ = q.shape
    tq, tk = min(ATT_BQ, S), ATT_BLK
    qspec, kvspec, vec = _head_specs(S, tq)

    def body(q_ref, k_ref, v_ref, t_ref, do_ref, dq_ref, dk_ref, dv_ref):
        i = pl.program_id(1)

        @pl.when(i == 0)
        def _():
            dk_ref[...] = jnp.zeros_like(dk_ref)
            dv_ref[...] = jnp.zeros_like(dv_ref)

        qv, dov, t_all = q_ref[...], do_ref[...], t_ref[...]
        row, col = _blk_iota(tq, tk)
        upto = _tri(tk, lambda r, c: r <= c)
        before = _tri(tk, lambda r, c: r < c)

        def step(kb, carry):
            p_sum, e_sum, dq = carry
            ks = pl.multiple_of(kb * tk, tk)
            kv = k_ref[pl.ds(ks, tk), :]
            vv = v_ref[pl.ds(ks, tk), :]
            lb, lom, strict = _sb_block(qv, kv, i, kb, row, col)
            tail = t_all - p_sum - _cs2(lom, upto)
            w = jnp.where(strict, jnp.exp(lb + tail), 0.0)
            e = _dot_nt(dov, vv) * w
            e_before = e_sum + _cs2(e, before)
            beta = jnp.exp(lb)
            dz = jnp.where(strict, e * (1.0 - beta) - e_before * beta, 0.0) * ATTN_SCALE
            dzb = dz.astype(BF16)
            dk_ref[pl.ds(ks, tk), :] += _dot_tn(dzb, qv)
            dv_ref[pl.ds(ks, tk), :] += _dot_tn(w.astype(BF16), dov)
            return (p_sum + jnp.sum(lom, axis=1, keepdims=True), e_sum + jnp.sum(e, axis=1, keepdims=True),
                    dq + _dot(dzb, kv))

        zero = jnp.zeros((tq, 1), F32)
        _, _, dq = lax.fori_loop(0, (i + 1) * (tq // tk), step, (zero, zero, jnp.zeros((tq, HEAD_DIM), F32)))
        dq_ref[...] = dq

    full = jax.ShapeDtypeStruct((H, S, HEAD_DIM), F32)
    return pl.pallas_call(
        body, name=name, grid=(H, S // tq),
        in_specs=[qspec, kvspec, kvspec, vec, qspec], out_specs=[qspec, kvspec, kvspec],
        out_shape=[full, full, full],
        compiler_params=_cparams(("parallel", "arbitrary")),
    )(q, k, v, tot, do)


def _fox_logits(q, k, cq, ck, i, kb, row, col):
    s = _dot_nt(q, k) * ATTN_SCALE + cq - ck
    causal = (col + kb * ATT_BLK) <= (row + i * q.shape[0])
    return jnp.where(causal, s, NEG_INF), causal


def fox_fwd(q, k, v, cq, ck, *, name):
    H, S, _ = q.shape
    tq, tk = min(ATT_BQ, S), ATT_BLK
    qspec, kvspec, vec = _head_specs(S, tq)
    ckspec = pl.BlockSpec((None, S // tk, 1, tk), lambda h, i: (h, 0, 0, 0))

    def body(q_ref, k_ref, v_ref, cq_ref, ck_ref, o_ref, lse_ref):
        i = pl.program_id(1)
        qv, cqv = q_ref[...], cq_ref[...]
        row, col = _blk_iota(tq, tk)

        def step(kb, carry):
            m, l, acc = carry
            ks = pl.multiple_of(kb * tk, tk)
            s, _ = _fox_logits(qv, k_ref[pl.ds(ks, tk), :], cqv, ck_ref[kb], i, kb, row, col)
            m_new = jnp.maximum(m, jnp.max(s, axis=1, keepdims=True))
            alpha = jnp.exp(m - m_new)
            p = jnp.exp(s - m_new)
            l = alpha * l + jnp.sum(p, axis=1, keepdims=True)
            acc = alpha * acc + _dot(p.astype(BF16), v_ref[pl.ds(ks, tk), :])
            return m_new, l, acc

        m, l, acc = lax.fori_loop(0, (i + 1) * (tq // tk), step, (jnp.full((tq, 1), NEG_INF, F32), jnp.zeros((tq, 1), F32),
                                                   jnp.zeros((tq, HEAD_DIM), F32)))
        o_ref[...] = (acc / l).astype(o_ref.dtype)
        lse_ref[...] = m + jnp.log(l)

    return pl.pallas_call(
        body, name=name, grid=(H, S // tq),
        in_specs=[qspec, kvspec, kvspec, vec, ckspec], out_specs=[qspec, vec],
        out_shape=[jax.ShapeDtypeStruct((H, S, HEAD_DIM), BF16), jax.ShapeDtypeStruct((H, S, 1), F32)],
        compiler_params=_cparams(("parallel", "arbitrary")),
    )(q, k, v, cq, ck)


def fox_bwd(q, k, v, o, lse, cq, ck, do, *, name):
    H, S, _ = q.shape
    tq, tk = min(ATT_BQ, S), ATT_BLK
    qspec, kvspec, vec = _head_specs(S, tq)
    ckspec = pl.BlockSpec((None, S // tk, 1, tk), lambda h, i: (h, 0, 0, 0))

    def body(q_ref, k_ref, v_ref, o_ref, lse_ref, cq_ref, ck_ref, do_ref, dq_ref, dk_ref, dv_ref, dcq_ref, dck_ref):
        i = pl.program_id(1)

        @pl.when(i == 0)
        def _():
            dk_ref[...] = jnp.zeros_like(dk_ref)
            dv_ref[...] = jnp.zeros_like(dv_ref)
            dck_ref[...] = jnp.zeros_like(dck_ref)

        qv, dov, cqv, lsev = q_ref[...], do_ref[...], cq_ref[...], lse_ref[...]
        delta = jnp.sum(dov.astype(F32) * o_ref[...].astype(F32), axis=1, keepdims=True)
        row, col = _blk_iota(tq, tk)

        def step(kb, carry):
            dq, dcq = carry
            ks = pl.multiple_of(kb * tk, tk)
            kv = k_ref[pl.ds(ks, tk), :]
            vv = v_ref[pl.ds(ks, tk), :]
            s, causal = _fox_logits(qv, kv, cqv, ck_ref[kb], i, kb, row, col)
            p = jnp.where(causal, jnp.exp(s - lsev), 0.0)
            ds = p * (_dot_nt(dov, vv) - delta)
            dck_ref[kb] += jnp.sum(ds, axis=0, keepdims=True)
            dsb = ds.astype(BF16)
            dk_ref[pl.ds(ks, tk), :] += _dot_tn(dsb, qv) * ATTN_SCALE
            dv_ref[pl.ds(ks, tk), :] += _dot_tn(p.astype(BF16), dov)
            return dq + _dot(dsb, kv) * ATTN_SCALE, dcq + jnp.sum(ds, axis=1, keepdims=True)

        dq, dcq = lax.fori_loop(0, (i + 1) * (tq // tk), step, (jnp.zeros((tq, HEAD_DIM), F32), jnp.zeros((tq, 1), F32)))
        dq_ref[...] = dq
        dcq_ref[...] = dcq

    full = jax.ShapeDtypeStruct((H, S, HEAD_DIM), F32)
    return pl.pallas_call(
        body, name=name, grid=(H, S // tq),
        in_specs=[qspec, kvspec, kvspec, qspec, vec, vec, ckspec, qspec],
        out_specs=[qspec, kvspec, kvspec, vec, ckspec],
        out_shape=[full, full, full, jax.ShapeDtypeStruct((H, S, 1), F32),
                   jax.ShapeDtypeStruct((H, S // tk, 1, tk), F32)],
        compiler_params=_cparams(("parallel", "arbitrary")),
    )(q, k, v, o, lse, cq, ck, do)


def _swa_specs(S, tq):
    qspec = pl.BlockSpec((None, SWA_GROUP, tq, HEAD_DIM), lambda g, i: (g, 0, i, 0))
    kvspec = pl.BlockSpec((None, S + SWA_WINDOW, HEAD_DIM), lambda g, i: (g, 0, 0))
    vec = pl.BlockSpec((None, SWA_GROUP, tq, 1), lambda g, i: (g, 0, i, 0))
    sink = pl.BlockSpec((None, SWA_GROUP * tq, 1), lambda g, i: (g, 0, 0))
    return qspec, kvspec, vec, sink


def _swa_logits(q2, kw, i, tq):
    rows = q2.shape[0]
    r = lax.broadcasted_iota(jnp.int32, (rows, 2 * tq), 0)
    c = lax.broadcasted_iota(jnp.int32, (rows, 2 * tq), 1)
    diff = (r & (tq - 1)) + tq - c
    ok = (diff >= 0) & (diff < SWA_WINDOW) & (c + (i - 1) * tq >= 0)
    return jnp.where(ok, _dot_nt(q2, kw) * ATTN_SCALE, NEG_INF), ok


def swa_fwd(q, kp, vp, sink, *, name):
    _, G, S, _ = q.shape
    tq = ATT_BLK
    qspec, kvspec, vec, sinkspec = _swa_specs(S, tq)

    def body(q_ref, k_ref, v_ref, s_ref, o_ref, lse_ref):
        i = pl.program_id(1)
        q2 = q_ref[...].reshape(G * tq, HEAD_DIM)
        ws = pl.multiple_of(i * tq, tq)
        logits, _ = _swa_logits(q2, k_ref[pl.ds(ws, 2 * tq), :], i, tq)
        sk = s_ref[...]
        m = jnp.maximum(jnp.max(logits, axis=1, keepdims=True), sk)
        e = jnp.exp(logits - m)
        den = jnp.sum(e, axis=1, keepdims=True) + jnp.exp(sk - m)
        o = _dot((e / den).astype(BF16), v_ref[pl.ds(ws, 2 * tq), :])
        o_ref[...] = o.reshape(G, tq, HEAD_DIM).astype(o_ref.dtype)
        lse_ref[...] = (m + jnp.log(den)).reshape(G, tq, 1)

    return pl.pallas_call(
        body, name=name, grid=(SWA_KV_HEADS, S // tq),
        in_specs=[qspec, kvspec, kvspec, sinkspec], out_specs=[qspec, vec],
        out_shape=[jax.ShapeDtypeStruct(q.shape, BF16), jax.ShapeDtypeStruct((SWA_KV_HEADS, G, S, 1), F32)],
        compiler_params=_cparams(("parallel", "arbitrary")),
    )(q, kp, vp, sink)


def swa_bwd(q, kp, vp, sink, o, lse, do, *, name):
    _, G, S, _ = q.shape
    tq = ATT_BLK
    qspec, kvspec, vec, sinkspec = _swa_specs(S, tq)

    def body(q_ref, k_ref, v_ref, s_ref, o_ref, lse_ref, do_ref, dq_ref, dk_ref, dv_ref, dsink_ref):
        i = pl.program_id(1)

        @pl.when(i == 0)
        def _():
            dk_ref[...] = jnp.zeros_like(dk_ref)
            dv_ref[...] = jnp.zeros_like(dv_ref)

        q2 = q_ref[...].reshape(G * tq, HEAD_DIM)
        do2 = do_ref[...].reshape(G * tq, HEAD_DIM)
        o2 = o_ref[...].reshape(G * tq, HEAD_DIM)
        lse2 = lse_ref[...].reshape(G * tq, 1)
        ws = pl.multiple_of(i * tq, tq)
        kw = k_ref[pl.ds(ws, 2 * tq), :]
        vw = v_ref[pl.ds(ws, 2 * tq), :]
        logits, ok = _swa_logits(q2, kw, i, tq)
        p = jnp.where(ok, jnp.exp(logits - lse2), 0.0)
        delta = jnp.sum(do2.astype(F32) * o2.astype(F32), axis=1, keepdims=True)
        ds = p * (_dot_nt(do2, vw) - delta)
        dsb = ds.astype(BF16)
        dq_ref[...] = (_dot(dsb, kw) * ATTN_SCALE).reshape(G, tq, HEAD_DIM)
        dk_ref[pl.ds(ws, 2 * tq), :] += _dot_tn(dsb, q2) * ATTN_SCALE
        dv_ref[pl.ds(ws, 2 * tq), :] += _dot_tn(p.astype(BF16), do2)
        dsink_ref[...] = (-jnp.exp(s_ref[...] - lse2) * delta).reshape(G, tq, 1)

    kvshape = jax.ShapeDtypeStruct(kp.shape, F32)
    return pl.pallas_call(
        body, name=name, grid=(SWA_KV_HEADS, S // tq),
        in_specs=[qspec, kvspec, kvspec, sinkspec, qspec, vec, qspec],
        out_specs=[qspec, kvspec, kvspec, vec],
        out_shape=[jax.ShapeDtypeStruct(q.shape, F32), kvshape, kvshape,
                   jax.ShapeDtypeStruct((SWA_KV_HEADS, G, S, 1), F32)],
        compiler_params=_cparams(("parallel", "arbitrary")),
    )(q, kp, vp, sink, o, lse, do)


def _adamw_tile(w, g, m, v):
    m = ADAM_B1 * m + (1.0 - ADAM_B1) * g
    v = ADAM_B2 * v + (1.0 - ADAM_B2) * (g * g)
    m_hat = m / (1.0 - ADAM_B1 ** ADAM_STEP)
    v_hat = v / (1.0 - ADAM_B2 ** ADAM_STEP)
    delta = -ADAM_LR * (m_hat / (jnp.sqrt(v_hat) + ADAM_EPS) + ADAM_WD * w)
    return g, delta, m, v


def adamw(gfull, t, w, m, v, *, name):
    off, K, ns, _ = t
    sb = off // K
    nat = pl.BlockSpec((K, SLAB), lambda q: (0, q))

    def body(g_ref, w_ref, m_ref, v_ref, *outs):
        for o, r in zip(outs, _adamw_tile(w_ref[...], g_ref[...], m_ref[...], v_ref[...])):
            o[...] = r

    return pl.pallas_call(
        body, name=name, grid=(ns,),
        in_specs=[pl.BlockSpec((K, SLAB), lambda q: (sb + q, 0)), nat, nat, nat],
        out_specs=[nat] * 4, out_shape=[jax.ShapeDtypeStruct(w.shape, F32)] * 4,
        compiler_params=_cparams(("parallel",)),
    )(gfull, w, m, v)


def adamw_small(g, w, m, v, *, name):
    def body(g_ref, w_ref, m_ref, v_ref, *outs):
        for o, r in zip(outs, _adamw_tile(w_ref[...], g_ref[...], m_ref[...], v_ref[...])):
            o[...] = r

    return pl.pallas_call(body, name=name, out_shape=[jax.ShapeDtypeStruct(w.shape, F32)] * 4)(g, w, m, v)


MESH = pl.DeviceIdType.MESH
HBM = pl.BlockSpec(memory_space=pl.ANY)


def _place():
    x, y, c = lax.axis_index("x"), lax.axis_index("y"), lax.axis_index("c")
    others = [(1 - x, y), (x, 1 - y), (1 - x, 1 - y)]
    return x, y, c, others


def _rcopy(src, dst, send_sems, recv_sems, k, to):
    return pltpu.make_async_remote_copy(src_ref=src, dst_ref=dst, send_sem=send_sems.at[k], recv_sem=recv_sems.at[k],
                                        device_id=to, device_id_type=MESH)


def _dma_sems(*counts):
    return [pltpu.SemaphoreType.DMA((n,)) for n in counts]


DMA_UNIT_ROWS = 128
DMA_PIECES = 4
DMA_PIECES_LOCAL = 8


def _row_pieces(h, n):
    units = h // DMA_UNIT_ROWS
    n = min(n, units)
    base, extra = divmod(units, n)
    sizes = [(base + (k < extra)) * DMA_UNIT_ROWS for k in range(n)]
    return [(sum(sizes[:k]), sizes[k]) for k in range(n)]


def _start_pieces(make, h, n):
    for s0, sz in _row_pieces(h, n):
        make(s0, sz).start()
    return make(0, h)


def allgather_weights(shards, *, name):
    n = len(shards)

    def body(*refs):
        ins, outs = refs[:n], refs[n:2 * n]
        send_sems, recv_sems, local_sems = refs[2 * n:]
        x, y, c, others = _place()
        me, sibling = 2 * x + y, (x, y, 1 - c)

        def own_rows(i, s0, sz):
            return ins[i].at[pl.ds(s0, sz)], outs[i].at[me, pl.ds(s0, sz)]

        def landed(i, chip, s0, sz):
            return outs[i].at[2 * chip[0] + chip[1], pl.ds(s0, sz)]

        local, first, passed = [], [], []
        for i in range(n):
            local.append(_start_pieces(lambda s0, sz: pltpu.make_async_copy(*own_rows(i, s0, sz), local_sems.at[i]),
                                       shards[i].shape[0], DMA_PIECES_LOCAL))
        for i in range(n):
            h = shards[i].shape[0] // 2
            for f, chip in enumerate(others):
                first.append(_start_pieces(
                    lambda s0, sz: _rcopy(*own_rows(i, c * h + s0, sz), send_sems, recv_sems, 6 * i + f, (*chip, c)),
                    h, DMA_PIECES))
        for i in range(n):
            h = shards[i].shape[0] // 2
            for f, chip in enumerate(others):
                blk = landed(i, chip, c * h, h)
                _rcopy(blk, blk, send_sems, recv_sems, 6 * i + f, (*chip, c)).wait_recv()
                passed.append(_start_pieces(
                    lambda s0, sz: _rcopy(landed(i, chip, c * h + s0, sz), landed(i, chip, c * h + s0, sz),
                                          send_sems, recv_sems, 6 * i + 3 + f, sibling), h, DMA_PIECES))
        for i in range(n):
            h = shards[i].shape[0] // 2
            for f, (px, py) in enumerate(others):
                blk = outs[i].at[2 * px + py, pl.ds((1 - c) * h, h)]
                _rcopy(blk, blk, send_sems, recv_sems, 6 * i + 3 + f, sibling).wait_recv()
        for cp in first + passed:
            cp.wait_send()
        for cp in local:
            cp.wait()

    return pl.pallas_call(
        body, name=name, in_specs=[HBM] * n, out_specs=[HBM] * n,
        out_shape=[jax.ShapeDtypeStruct((N_CHIPS,) + s.shape, s.dtype) for s in shards],
        scratch_shapes=_dma_sems(6 * n, 6 * n, n),
    )(*shards)


def swap_halves(grads, *, name):
    n = len(grads)

    def body(*refs):
        ins, own, theirs = refs[:n], refs[n:2 * n], refs[2 * n:3 * n]
        send_sems, recv_sems, local_sems = refs[3 * n:]
        x, y, c, _ = _place()
        for i in range(n):
            h = grads[i].shape[1] // 2
            for k in range(N_CHIPS):
                for s0, sz in _row_pieces(h, DMA_PIECES):
                    pltpu.make_async_copy(ins[i].at[k, pl.ds(c * h + s0, sz)], own[i].at[k, pl.ds(s0, sz)],
                                          local_sems.at[i]).start()
                    _rcopy(ins[i].at[k, pl.ds((1 - c) * h + s0, sz)], theirs[i].at[k, pl.ds(s0, sz)],
                           send_sems, recv_sems, i, (x, y, 1 - c)).start()
        for i in range(n):
            h = grads[i].shape[1] // 2
            _rcopy(ins[i].at[:, pl.ds((1 - c) * h, h)], theirs[i], send_sems, recv_sems, i, (x, y, 1 - c)).wait()
            pltpu.make_async_copy(ins[i].at[:, pl.ds(c * h, h)], own[i], local_sems.at[i]).wait()

    half = [jax.ShapeDtypeStruct((N_CHIPS, g.shape[1] // 2, SLAB), g.dtype) for g in grads]
    return pl.pallas_call(body, name=name, in_specs=[HBM] * n, out_specs=[HBM] * (2 * n), out_shape=half + half,
                          scratch_shapes=_dma_sems(n, n, n))(*grads)


def scatter_chips(parts, *, name):
    n = len(parts)

    def body(*refs):
        ins, mine, got = refs[:n], refs[n:2 * n], refs[2 * n:3 * n]
        send_sems, recv_sems, local_sems = refs[3 * n:]
        x, y, c, others = _place()
        cps = []
        for i in range(n):
            h = parts[i].shape[1]
            cps.append(_start_pieces(
                lambda s0, sz: pltpu.make_async_copy(ins[i].at[2 * x + y, pl.ds(s0, sz)], mine[i].at[pl.ds(s0, sz)],
                                                     local_sems.at[i]), h, DMA_PIECES_LOCAL))
            for f, (px, py) in enumerate(others):
                cps.append(_start_pieces(
                    lambda s0, sz: _rcopy(ins[i].at[2 * px + py, pl.ds(s0, sz)], got[i].at[f, pl.ds(s0, sz)],
                                          send_sems, recv_sems, 3 * i + f, (px, py, c)), h, DMA_PIECES))
        for cp in cps:
            cp.wait()

    return pl.pallas_call(
        body, name=name, in_specs=[HBM] * n, out_specs=[HBM] * (2 * n),
        out_shape=[jax.ShapeDtypeStruct(p.shape[1:], p.dtype) for p in parts]
        + [jax.ShapeDtypeStruct((3,) + p.shape[1:], p.dtype) for p in parts],
        scratch_shapes=_dma_sems(3 * n, 3 * n, n),
    )(*parts)


def join_halves(halves, *, name):
    n = len(halves)

    def body(*refs):
        ins, outs = refs[:n], refs[n:2 * n]
        send_sems, recv_sems, local_sems = refs[2 * n:]
        x, y, c, _ = _place()
        sibling = (x, y, 1 - c)
        cps = []
        for i in range(n):
            h = halves[i].shape[0]

            def to_half_c(s0, sz):
                return ins[i].at[pl.ds(s0, sz)], outs[i].at[pl.ds(c * h + s0, sz)]

            loc = _start_pieces(lambda s0, sz: pltpu.make_async_copy(*to_half_c(s0, sz), local_sems.at[i]),
                                h, 2 * DMA_PIECES_LOCAL)
            snd = _start_pieces(lambda s0, sz: _rcopy(*to_half_c(s0, sz), send_sems, recv_sems, i, sibling),
                                h, 2 * DMA_PIECES_LOCAL)
            cps.append((loc, snd, _rcopy(ins[i], outs[i].at[pl.ds((1 - c) * h, h)], send_sems, recv_sems, i, sibling)))
        for loc, snd, rcv in cps:
            snd.wait_send()
            rcv.wait_recv()
            loc.wait()

    return pl.pallas_call(
        body, name=name, in_specs=[HBM] * n, out_specs=[HBM] * n,
        out_shape=[jax.ShapeDtypeStruct((2 * hh.shape[0], SLAB), hh.dtype) for hh in halves],
        scratch_shapes=_dma_sems(n, n, n),
    )(*halves)


def allreduce_small(v, *, name):
    rows, n = v.shape

    def body(x_ref, sum_ref, all_ref, send_sems, recv_sems, local_sem):
        x, y, c, others = _place()
        me, sibling = (x, y, c), (x, y, 1 - c)

        def blk(px, py, pc):
            return all_ref.at[pl.ds((4 * px + 2 * py + pc) * rows, rows), :]

        def copy(k, block, to, src=None):
            return _rcopy(blk(*block) if src is None else src, blk(*block), send_sems, recv_sems, k, to)

        mine = pltpu.make_async_copy(x_ref, blk(*me), local_sem)
        mine.start()
        first = [copy(0, me, sibling, src=x_ref)]
        first += [copy(1 + f, me, (*chip, c), src=x_ref) for f, chip in enumerate(others)]
        for cp in first:
            cp.start()
        passed = [copy(4 + f, (*chip, c), sibling) for f, chip in enumerate(others)]
        for f, chip in enumerate(others):
            copy(1 + f, (*chip, c), me).wait_recv()
            passed[f].start()
        copy(0, sibling, me).wait_recv()
        for f, chip in enumerate(others):
            copy(4 + f, (*chip, 1 - c), me).wait_recv()
        for cp in first + passed:
            cp.wait_send()
        mine.wait()
        acc = all_ref[pl.ds(0, rows), :]
        for d in range(1, N_DEVICES):
            acc = acc + all_ref[pl.ds(d * rows, rows), :]
        sum_ref[...] = acc

    vm = pl.BlockSpec(memory_space=pltpu.VMEM)
    return pl.pallas_call(
        body, name=name, in_specs=[vm], out_specs=[vm, vm],
        out_shape=[jax.ShapeDtypeStruct((rows, n), F32), jax.ShapeDtypeStruct((N_DEVICES * rows, n), F32)],
        scratch_shapes=_dma_sems(7, 7) + [pltpu.SemaphoreType.DMA],
    )(v)[0]


def add_pairs(own, theirs, *, name):
    h = own.shape[1]
    spec = pl.BlockSpec((None, h, SLAB), lambda k: (k, 0, 0))

    def body(a_ref, b_ref, o_ref):
        o_ref[...] = (a_ref[...].astype(F32) + b_ref[...].astype(F32)).astype(o_ref.dtype)

    return pl.pallas_call(body, name=name, grid=(N_CHIPS,), in_specs=[spec, spec], out_specs=spec,
                          out_shape=jax.ShapeDtypeStruct(own.shape, own.dtype),
                          compiler_params=_cparams(("parallel",)))(own, theirs)


def add_chips(mine, got, *, name):
    h = mine.shape[0]
    tr = h // 2

    def body(a_ref, b_ref, o_ref):
        acc = a_ref[...].astype(F32)
        for f in range(3):
            acc = acc + b_ref[f].astype(F32)
        o_ref[...] = acc

    return pl.pallas_call(
        body, name=name, grid=(2,),
        in_specs=[pl.BlockSpec((tr, SLAB), lambda i: (i, 0)), pl.BlockSpec((3, tr, SLAB), lambda i: (0, i, 0))],
        out_specs=pl.BlockSpec((tr, SLAB), lambda i: (i, 0)),
        out_shape=jax.ShapeDtypeStruct((h, SLAB), F32),
        compiler_params=_cparams(("parallel",)))(mine, got)


DEPTH = 4
MIXER = (0, 1, 2, 0)
W_IN_COLS = (768, 320, 772)
W_IN_PAD = (768, 512, 1024)
MATS = ("up", "down", "inp", "out", "gate", "proj")
MAT_ARG = dict(up="w_up", down="w_down", inp="w_in", out="w_out", gate="w_ple_gate", proj="w_ple_proj")
GAINS = ("attn_norm", "mlp_norm", "ple_norm")
N_SMALL = 16
KINDS = ("grad_", "delta_", "new_m_", "new_v_")


def _layout(kind):
    ns_in = W_IN_PAD[kind] // SLAB
    off = 8192 + 1024 * ns_in
    lay = dict(up=(0, 1024, 4, False), down=(4096, 1024, 4, True), inp=(8192, 1024, ns_in, False),
               out=(off, 256, 4, True), gate=(off + 1024, 256, 4, True), proj=(off + 2048, 256, 1, False))
    return lay, off + 2304


def _to_slabs(w):
    k, c = w.shape
    return w.reshape(k, c // SLAB, SLAB).transpose(1, 0, 2).reshape(-1, SLAB)


def _pad_cols(w, n):
    return jnp.pad(w, ((0, 0), (0, n - w.shape[1])))


def _heads(x2d, n):
    return x2d.reshape(x2d.shape[0], n, HEAD_DIM).transpose(1, 0, 2)


def _unheads(x3d):
    n, s, _ = x3d.shape
    return x3d.transpose(1, 0, 2).reshape(s, n * HEAD_DIM)


def _chip_cols(x2d, c, cpad):
    s = x2d.shape[0]
    return jnp.pad(x2d.reshape(s, N_CHIPS, c), ((0, 0), (0, 0), (0, cpad - c))).reshape(s, N_CHIPS * cpad)


def _unchip_cols(x2d, c, cpad):
    s = x2d.shape[0]
    return x2d.reshape(s, N_CHIPS, cpad)[:, :, :c].reshape(s, N_CHIPS * c)


def _add_res(acc, res):
    return (acc + res,)


def _relu2(acc):
    return acc, jnp.square(jnp.maximum(acc, 0.0))


def _relu2_bwd(acc, u):
    return (acc * (2.0 * jnp.maximum(u.astype(F32), 0.0)),)


def _ple_fwd(acc, x2, pp):
    return x2 + pp * _sigmoid(acc), acc


def _ple_bwd(dx, pp, gl):
    gate = _sigmoid(gl)
    return dx * gate, dx * pp * gate * (1.0 - gate)


def _layer_fwd(i, kind, x0, p_bf, wg, lay, gains, extra, tabs):
    s = x0.shape[0]
    an, mn, pn = gains
    sv = dict(x0=x0)
    h1 = rms_fwd(x0, an, name=f"attn_norm_{i}")
    if kind == 0:
        proj = mm_nn(h1, wg, lay["inp"], name=f"w_in_{i}")[0]
        qkv = proj.reshape(s, 3, N_HEADS, HEAD_DIM).transpose(1, 2, 0, 3)
        o, tot = sb_fwd(qkv[0], qkv[1], qkv[2], name=f"sb_fwd_{i}")
        sv.update(qkv=qkv, tot=tot)
    elif kind == 1:
        projp = mm_nn(h1, wg, lay["inp"], name=f"w_in_{i}", out_dtypes=(F32,))[0]
        proj = _unchip_cols(projp, W_IN_COLS[1], W_IN_PAD[1])
        nq = N_HEADS * HEAD_DIM
        nqk = nq + SWA_KV_HEADS * HEAD_DIM
        qk = rope_fwd(proj[:, :nqk], tabs, name=f"rope_{i}")
        q = _heads(qk[:, :nq], N_HEADS).reshape(SWA_KV_HEADS, SWA_GROUP, s, HEAD_DIM)
        front = ((0, 0), (SWA_WINDOW, 0), (0, 0))
        kp = jnp.pad(_heads(qk[:, nq:], SWA_KV_HEADS), front)
        vp = jnp.pad(_heads(proj[:, nqk:].astype(BF16), SWA_KV_HEADS), front)
        sink = jnp.repeat(extra.reshape(SWA_KV_HEADS, SWA_GROUP), ATT_BLK, axis=1)[:, :, None]
        o4, lse = swa_fwd(q, kp, vp, sink, name=f"swa_fwd_{i}")
        o = o4.reshape(N_HEADS, s, HEAD_DIM)
        sv.update(q=q, kp=kp, vp=vp, sink=sink, o4=o4, lse=lse)
    else:
        projp = mm_nn(h1, wg, lay["inp"], name=f"w_in_{i}", out_dtypes=(F32,))[0]
        proj = _unchip_cols(projp, W_IN_COLS[2], W_IN_PAD[2])
        nqkv = 3 * N_HEADS * HEAD_DIM
        qkv = proj[:, :nqkv].astype(BF16).reshape(s, 3, N_HEADS, HEAD_DIM).transpose(1, 2, 0, 3)
        fl = _pad_cols(proj[:, nqkv:], 128)
        bp = _pad_cols(extra[None], 128)
        cum_t = fox_gate_fwd(fl, bp, name=f"gate_fwd_{i}")[:, :N_HEADS].T
        cq = cum_t[:, :, None]
        ck = cum_t.reshape(N_HEADS, s // ATT_BLK, 1, ATT_BLK)
        o, lse = fox_fwd(qkv[0], qkv[1], qkv[2], cq, ck, name=f"fox_fwd_{i}")
        sv.update(qkv=qkv, fl=fl, bp=bp, cq=cq, ck=ck, o=o, lse=lse)
    a = _unheads(o)
    x1 = mm_nn(a, wg, lay["out"], name=f"w_out_{i}", epi=_add_res, extras=(x0,), out_dtypes=(F32,))[0]
    h2 = rms_fwd(x1, mn, name=f"mlp_norm_{i}")
    u, r = mm_nn(h2, wg, lay["up"], name=f"w_up_{i}", epi=_relu2, out_dtypes=(BF16, BF16))
    x2 = mm_nn(r, wg, lay["down"], name=f"w_down_{i}", epi=_add_res, extras=(x1,), out_dtypes=(F32,))[0]
    h3 = rms_fwd(x2, pn, name=f"ple_norm_{i}")
    pp = mm_nn(p_bf, wg, lay["proj"], name=f"w_ple_proj_{i}", out_dtypes=(F32,))[0]
    x3, gl = mm_nn(h3, wg, lay["gate"], name=f"w_ple_gate_{i}", epi=_ple_fwd, extras=(x2, pp), out_dtypes=(F32, F32))
    sv.update(h1=h1, a=a, x1=x1, h2=h2, u=u, r=r, x2=x2, h3=h3, pp=pp, gl=gl)
    return x3, sv


def _layer_bwd(i, kind, dx3, sv, p_bf, wg, lay, n_rows, gains, tabs):
    s = dx3.shape[0]
    an, mn, pn = gains
    g = lax.empty((N_CHIPS, n_rows, SLAB), BF16)
    d_pp, d_gl = ew(_ple_bwd, [dx3, sv["pp"], sv["gl"]], [BF16, BF16], name=f"ple_bwd_{i}")
    g = mm_tn(p_bf, d_pp, g, lay["proj"], name=f"dw_ple_proj_{i}")
    g = mm_tn(sv["h3"], d_gl, g, lay["gate"], name=f"dw_ple_gate_{i}")
    d_h3 = mm_nt(d_gl, wg, lay["gate"], name=f"dx_ple_gate_{i}", out_dtypes=(F32,))[0]
    dx2, dx2b, d_pn = rms_bwd(sv["x2"], pn, d_h3, dx3, name=f"ple_norm_bwd_{i}")
    g = mm_tn(sv["r"], dx2b, g, lay["down"], name=f"dw_down_{i}")
    d_u = mm_nt(dx2b, wg, lay["down"], name=f"dx_down_{i}", epi=_relu2_bwd, extras=(sv["u"],))[0]
    g = mm_tn(sv["h2"], d_u, g, lay["up"], name=f"dw_up_{i}")
    d_h2 = mm_nt(d_u, wg, lay["up"], name=f"dx_up_{i}", out_dtypes=(F32,))[0]
    dx1, dx1b, d_mn = rms_bwd(sv["x1"], mn, d_h2, dx2, name=f"mlp_norm_bwd_{i}")
    g = mm_tn(sv["a"], dx1b, g, lay["out"], name=f"dw_out_{i}")
    d_a = mm_nt(dx1b, wg, lay["out"], name=f"dx_out_{i}")[0]
    do = _heads(d_a, N_HEADS)
    d_extra = None
    if kind == 0:
        qkv = sv["qkv"]
        dq, dk, dv = sb_bwd(qkv[0], qkv[1], qkv[2], sv["tot"], do, name=f"sb_bwd_{i}")
        d_proj = jnp.stack([dq, dk, dv]).transpose(2, 0, 1, 3).reshape(s, 3 * N_HEADS * HEAD_DIM).astype(BF16)
    elif kind == 1:
        do4 = do.reshape(SWA_KV_HEADS, SWA_GROUP, s, HEAD_DIM)
        dq, dkp, dvp, dsr = swa_bwd(sv["q"], sv["kp"], sv["vp"], sv["sink"], sv["o4"], sv["lse"], do4, name=f"swa_bwd_{i}")
        dqk = jnp.concatenate([_unheads(dq.reshape(N_HEADS, s, HEAD_DIM)), _unheads(dkp[:, SWA_WINDOW:])], axis=1)
        dqk = rope_bwd(dqk, tabs, name=f"rope_bwd_{i}")
        d_proj = jnp.concatenate([dqk, _unheads(dvp[:, SWA_WINDOW:]).astype(BF16)], axis=1)
        d_proj = _chip_cols(d_proj, W_IN_COLS[1], W_IN_PAD[1])
        d_extra = jnp.sum(dsr[..., 0], axis=2).reshape(N_HEADS)
    else:
        qkv = sv["qkv"]
        dq, dk, dv, dcq, dck = fox_bwd(qkv[0], qkv[1], qkv[2], sv["o"], sv["lse"], sv["cq"], sv["ck"], do, name=f"fox_bwd_{i}")
        dcum = _pad_cols((dcq[:, :, 0] - dck.reshape(N_HEADS, s)).T, 128)
        dfl, dbp = fox_gate_bwd(dcum, sv["fl"], sv["bp"], name=f"gate_bwd_{i}")
        d_qkv = jnp.stack([dq, dk, dv]).transpose(2, 0, 1, 3).reshape(s, 3 * N_HEADS * HEAD_DIM)
        d_proj = jnp.concatenate([d_qkv, dfl[:, :N_HEADS]], axis=1).astype(BF16)
        d_proj = _chip_cols(d_proj, W_IN_COLS[2], W_IN_PAD[2])
        d_extra = dbp[0, :N_HEADS]
    g = mm_tn(sv["h1"], d_proj, g, lay["inp"], name=f"dw_in_{i}")
    d_h1 = mm_nt(d_proj, wg, lay["inp"], name=f"dx_in_{i}", out_dtypes=(F32,))[0]
    dx0, _, d_an = rms_bwd(sv["x0"], an, d_h1, dx1, name=f"attn_norm_bwd_{i}")
    return dx0, g, (d_an, d_mn, d_pn), d_extra


def _small_rows(a, prefix):
    rows = [a[f"{prefix}{n}_{i}"] for i in range(DEPTH) for n in GAINS] + [a[f"{prefix}final_norm"]]
    rows += [_pad_cols(a[f"{prefix}{n}"][None], D_MODEL)[0] for n in ("sinks_1", "b_forget_2")]
    return jnp.stack(rows + [jnp.zeros((D_MODEL,), F32)])


def _train_step(a):
    x = a["x"][0]
    tabs = rope_tables(x.shape[0], (N_HEADS + SWA_KV_HEADS) * HEAD_DIM)
    lays = [_layout(k) for k in MIXER]

    def natural(prefix, i, m):
        w = a[f"{prefix}{MAT_ARG[m]}_{i}"]
        return _pad_cols(w, W_IN_PAD[MIXER[i]]) if m == "inp" else w

    packed = [jnp.concatenate([_to_slabs(natural("", i, m).astype(BF16)) for m in MATS], axis=0) for i in range(DEPTH)]
    wgs = allgather_weights(packed, name="allgather_weights")

    gains = [tuple(a[f"{n}_{i}"][None] for n in GAINS) for i in range(DEPTH)]
    extras = [None, a["sinks_1"], a["b_forget_2"], None]
    p_bf = [a["p"][i, 0].astype(BF16) for i in range(DEPTH)]

    saved = []
    for i in range(DEPTH):
        x, sv = _layer_fwd(i, MIXER[i], x, p_bf[i], wgs[i], lays[i][0], gains[i], extras[i], tabs)
        saved.append(sv)
    dx, d_final, loss = loss_head(x, a["final_norm"][None], a["loss_target"][0], name="loss_head")

    small = [None] * N_SMALL
    small[12] = d_final[0]
    small[15] = _pad_cols(loss[:, :1], D_MODEL)[0]
    grads = [None] * DEPTH
    for i in reversed(range(DEPTH)):
        dx, grads[i], d_gains, d_extra = _layer_bwd(i, MIXER[i], dx, saved[i], p_bf[i], wgs[i], lays[i][0], lays[i][1],
                                                    gains[i], tabs)
        for j in range(3):
            small[3 * i + j] = d_gains[j][0]
        if d_extra is not None:
            small[12 + MIXER[i]] = _pad_cols(d_extra[None], D_MODEL)[0]
    small = allreduce_small(jnp.stack(small), name="allreduce_small")

    halves = swap_halves(grads, name="swap_halves")
    pair = [add_pairs(halves[i], halves[DEPTH + i], name=f"add_pairs_{i}") for i in range(DEPTH)]
    parts = scatter_chips(pair, name="scatter_chips")
    summed = [add_chips(parts[i], parts[DEPTH + i], name=f"add_chips_{i}") for i in range(DEPTH)]
    gfull = join_halves(summed, name="join_halves")

    out = {"loss": small[15, 0], "grad_x": dx[None]}
    for i in range(DEPTH):
        for m in MATS:
            res = adamw(gfull[i], lays[i][0][m], natural("", i, m), natural("m_", i, m), natural("v_", i, m),
                        name=f"adamw_{MAT_ARG[m]}_{i}")
            cols = a[f"{MAT_ARG[m]}_{i}"].shape[1]
            for kd, r in zip(KINDS, res):
                out[f"{kd}{MAT_ARG[m]}_{i}"] = r[:, :cols]
    res = adamw_small(small, _small_rows(a, ""), _small_rows(a, "m_"), _small_rows(a, "v_"), name="adamw_small")
    for kd, r in zip(KINDS, res):
        for i in range(DEPTH):
            for j, n in enumerate(GAINS):
                out[f"{kd}{n}_{i}"] = r[3 * i + j]
        out[f"{kd}final_norm"] = r[12]
        out[f"{kd}sinks_1"] = r[13, :N_HEADS]
        out[f"{kd}b_forget_2"] = r[14, :N_HEADS]
    return out


def _weight_names():
    names = []
    for i in range(DEPTH):
        names += [f"attn_norm_{i}", f"w_in_{i}", f"w_out_{i}"] + [[], ["sinks_1"], ["b_forget_2"]][MIXER[i]]
        names += [f"mlp_norm_{i}", f"w_up_{i}", f"w_down_{i}", f"ple_norm_{i}", f"w_ple_gate_{i}", f"w_ple_proj_{i}"]
    return names + ["final_norm"]


def kernel(x, p, attn_norm_0, w_in_0, w_out_0, mlp_norm_0, w_up_0, w_down_0, ple_norm_0, w_ple_gate_0, w_ple_proj_0, attn_norm_1, w_in_1, w_out_1, sinks_1, mlp_norm_1, w_up_1, w_down_1, ple_norm_1, w_ple_gate_1, w_ple_proj_1, attn_norm_2, w_in_2, w_out_2, b_forget_2, mlp_norm_2, w_up_2, w_down_2, ple_norm_2, w_ple_gate_2, w_ple_proj_2, attn_norm_3, w_in_3, w_out_3, mlp_norm_3, w_up_3, w_down_3, ple_norm_3, w_ple_gate_3, w_ple_proj_3, final_norm, loss_target, m_attn_norm_0, m_w_in_0, m_w_out_0, m_mlp_norm_0, m_w_up_0, m_w_down_0, m_ple_norm_0, m_w_ple_gate_0, m_w_ple_proj_0, m_attn_norm_1, m_w_in_1, m_w_out_1, m_sinks_1, m_mlp_norm_1, m_w_up_1, m_w_down_1, m_ple_norm_1, m_w_ple_gate_1, m_w_ple_proj_1, m_attn_norm_2, m_w_in_2, m_w_out_2, m_b_forget_2, m_mlp_norm_2, m_w_up_2, m_w_down_2, m_ple_norm_2, m_w_ple_gate_2, m_w_ple_proj_2, m_attn_norm_3, m_w_in_3, m_w_out_3, m_mlp_norm_3, m_w_up_3, m_w_down_3, m_ple_norm_3, m_w_ple_gate_3, m_w_ple_proj_3, m_final_norm, v_attn_norm_0, v_w_in_0, v_w_out_0, v_mlp_norm_0, v_w_up_0, v_w_down_0, v_ple_norm_0, v_w_ple_gate_0, v_w_ple_proj_0, v_attn_norm_1, v_w_in_1, v_w_out_1, v_sinks_1, v_mlp_norm_1, v_w_up_1, v_w_down_1, v_ple_norm_1, v_w_ple_gate_1, v_w_ple_proj_1, v_attn_norm_2, v_w_in_2, v_w_out_2, v_b_forget_2, v_mlp_norm_2, v_w_up_2, v_w_down_2, v_ple_norm_2, v_w_ple_gate_2, v_w_ple_proj_2, v_attn_norm_3, v_w_in_3, v_w_out_3, v_mlp_norm_3, v_w_up_3, v_w_down_3, v_ple_norm_3, v_w_ple_gate_3, v_w_ple_proj_3, v_final_norm):
    out = _train_step(dict(locals()))
    return (out["loss"], out["grad_x"], *[out[kd + n] for kd in KINDS for n in _weight_names()])
```

```python
import jax
import jax.numpy as jnp
from jax import lax
from jax.experimental import pallas as pl
from jax.experimental.pallas import tpu as pltpu

F32 = jnp.float32
BF16 = jnp.bfloat16

D_MODEL = 1024
N_HEADS = 16
HEAD_DIM = 64
SWA_KV_HEADS = 2
SWA_GROUP = 8
SWA_WINDOW = 128
ROPE_THETA = 500000.0
ROPE_DIM = 16
RMS_EPS = 1e-6
NEG_INF = -1e30
ATTN_SCALE = HEAD_DIM ** -0.5
N_CHIPS = 4
N_DEVICES = 8

SLAB = 256
ATT_BLK = 128
ATT_BQ = 512
ATT_BK = 512
ROW_TILE = 256
V7X_VMEM_LIMIT = 56 * 1024 * 1024

ADAM_LR, ADAM_B1, ADAM_B2, ADAM_EPS, ADAM_WD, ADAM_STEP = 0.001, 0.9, 0.999, 1e-08, 0.01, 10


def _cparams(sem=None):
    return pltpu.CompilerParams(dimension_semantics=sem, vmem_limit_bytes=V7X_VMEM_LIMIT)


def _dot(a, b):
    return jnp.dot(a, b, preferred_element_type=F32)


def _dot_nt(a, b):
    return lax.dot_general(a, b, (((1,), (1,)), ((), ())), preferred_element_type=F32)


def _dot_tn(a, b):
    return lax.dot_general(a, b, (((0,), (0,)), ((), ())), preferred_element_type=F32)


def mm_nn(a, wg, t, *, name, epi=None, extras=(), out_dtypes=(BF16,)):
    off, K, ns, row = t
    M = a.shape[0]
    sb = off // K
    ne, no = len(extras), len(out_dtypes)
    if row:
        grid = (ns, N_CHIPS)
        a_map = lambda q, j: (0, j)
        b_map = lambda q, j: (j, sb + q, 0)
        o_map = lambda q, j: (0, q)
        n_out = ns * SLAB
        sem = ("parallel", "arbitrary")
    else:
        grid = (N_CHIPS, ns)
        a_map = lambda j, q: (0, 0)
        b_map = lambda j, q: (j, sb + q, 0)
        o_map = lambda j, q: (0, j * ns + q)
        n_out = N_CHIPS * ns * SLAB
        sem = ("parallel", "parallel")

    def body(a_ref, b_ref, *rest):
        ex, outs = rest[:ne], rest[ne:ne + no]
        part = _dot(a_ref[...], b_ref[...])

        def finish(acc):
            res = epi(acc, *[e[...] for e in ex]) if epi is not None else (acc,)
            for o, r in zip(outs, res):
                o[...] = r.astype(o.dtype)

        if row:
            acc_ref = rest[-1]
            j = pl.program_id(1)

            @pl.when(j == 0)
            def _():
                acc_ref[...] = part

            @pl.when(j > 0)
            def _():
                acc_ref[...] += part

            @pl.when(j == N_CHIPS - 1)
            def _():
                finish(acc_ref[...])
        else:
            finish(part)

    tile = pl.BlockSpec((M, SLAB), o_map)
    return pl.pallas_call(
        body, name=name, grid=grid,
        in_specs=[pl.BlockSpec((M, K), a_map), pl.BlockSpec((None, K, SLAB), b_map)] + [tile] * ne,
        out_specs=[tile] * no,
        out_shape=[jax.ShapeDtypeStruct((M, n_out), d) for d in out_dtypes],
        scratch_shapes=[pltpu.VMEM((M, SLAB), F32)] if row else [],
        compiler_params=_cparams(sem),
    )(a, wg, *extras)


def mm_nt(dy, wg, t, *, name, epi=None, extras=(), out_dtypes=(BF16,)):
    off, K, ns, row = t
    M = dy.shape[0]
    tm = min(1024, M)
    sb = off // K
    ne, no = len(extras), len(out_dtypes)
    grid = (M // tm, N_CHIPS, ns)
    b_map = lambda i, j, q: (j, sb + q, 0)
    if row:
        dy_map = lambda i, j, q: (i, q)
        o_map = lambda i, j, q: (i, j)
        n_out = N_CHIPS * K
        sem = ("parallel", "parallel", "arbitrary")
    else:
        dy_map = lambda i, j, q: (i, j * ns + q)
        o_map = lambda i, j, q: (i, 0)
        n_out = K
        sem = ("parallel", "arbitrary", "arbitrary")

    def body(dy_ref, b_ref, *rest):
        ex, outs, acc_ref = rest[:ne], rest[ne:ne + no], rest[-1]
        j, q = pl.program_id(1), pl.program_id(2)
        if row:
            first, last = q == 0, q == ns - 1
        else:
            first = jnp.logical_and(j == 0, q == 0)
            last = jnp.logical_and(j == N_CHIPS - 1, q == ns - 1)
        part = _dot_nt(dy_ref[...], b_ref[...])

        @pl.when(first)
        def _():
            acc_ref[...] = part

        @pl.when(jnp.logical_not(first))
        def _():
            acc_ref[...] += part

        @pl.when(last)
        def _():
            acc = acc_ref[...]
            res = epi(acc, *[e[...] for e in ex]) if epi is not None else (acc,)
            for o, r in zip(outs, res):
                o[...] = r.astype(o.dtype)

    tile = pl.BlockSpec((tm, K), o_map)
    return pl.pallas_call(
        body, name=name, grid=grid,
        in_specs=[pl.BlockSpec((tm, SLAB), dy_map), pl.BlockSpec((None, K, SLAB), b_map)] + [tile] * ne,
        out_specs=[tile] * no,
        out_shape=[jax.ShapeDtypeStruct((M, n_out), d) for d in out_dtypes],
        scratch_shapes=[pltpu.VMEM((tm, K), F32)],
        compiler_params=_cparams(sem),
    )(dy, wg, *extras)


def mm_tn(x, dy, g, t, *, name):
    off, K, ns, row = t
    S = x.shape[0]
    sb = off // K
    if row:
        x_map = lambda j, q: (0, j)
        dy_map = lambda j, q: (0, q)
    else:
        x_map = lambda j, q: (0, 0)
        dy_map = lambda j, q: (0, j * ns + q)

    def body(g_in, x_ref, dy_ref, o_ref):
        del g_in
        o_ref[...] = _dot_tn(x_ref[...], dy_ref[...]).astype(o_ref.dtype)

    return pl.pallas_call(
        body, name=name, grid=(N_CHIPS, ns),
        in_specs=[pl.BlockSpec(memory_space=pl.ANY), pl.BlockSpec((S, K), x_map), pl.BlockSpec((S, SLAB), dy_map)],
        out_specs=pl.BlockSpec((None, K, SLAB), lambda j, q: (j, sb + q, 0)),
        out_shape=jax.ShapeDtypeStruct(g.shape, g.dtype),
        input_output_aliases={0: 0},
        compiler_params=_cparams(("parallel", "parallel")),
    )(g, x, dy)


def ew(fn, ins, out_dtypes, *, name, bcast=()):
    S = ins[0].shape[0]
    tr = min(ROW_TILE, S)
    cols = ins[0].shape[1]
    ni, nb = len(ins), len(bcast)

    def body(*refs):
        res = fn(*[r[...] for r in refs[:ni + nb]])
        for o, r in zip(refs[ni + nb:], res):
            o[...] = r.astype(o.dtype)

    return pl.pallas_call(
        body, name=name, grid=(S // tr,),
        in_specs=[pl.BlockSpec((tr, a.shape[1]), lambda i: (i, 0)) for a in ins]
        + [pl.BlockSpec(b.shape, lambda i: (0, 0)) for b in bcast],
        out_specs=[pl.BlockSpec((tr, cols), lambda i: (i, 0)) for _ in out_dtypes],
        out_shape=[jax.ShapeDtypeStruct((S, cols), d) for d in out_dtypes],
        compiler_params=_cparams(("parallel",)),
    )(*ins, *bcast)


def _rstd(x):
    return lax.rsqrt(jnp.mean(x * x, axis=-1, keepdims=True) + RMS_EPS)


def _sigmoid(x):
    return 1.0 / (1.0 + jnp.exp(-x))


def _log_sigmoid(z):
    return jnp.minimum(z, 0.0) - jnp.log(1.0 + jnp.exp(-jnp.abs(z)))


def rms_fwd(x, g, *, name):
    return ew(lambda xv, gv: (xv * _rstd(xv) * gv,), [x], [BF16], name=name, bcast=[g])[0]


def _rms_bwd_tile(xv, gv, dh):
    rstd = _rstd(xv)
    xhat = xv * rstd
    gd = dh * gv
    dx = rstd * (gd - xhat * jnp.mean(xhat * gd, axis=-1, keepdims=True))
    return dx, jnp.sum(dh * xhat, axis=0, keepdims=True)


def rms_bwd(x, g, dh, dres, *, name):
    S, D = x.shape
    tr = min(ROW_TILE, S)

    def body(x_ref, g_ref, dh_ref, dres_ref, dx_ref, dxb_ref, dg_ref):
        i = pl.program_id(0)
        dx, dg = _rms_bwd_tile(x_ref[...], g_ref[...], dh_ref[...])
        dx = dx + dres_ref[...]
        dx_ref[...] = dx
        dxb_ref[...] = dx.astype(BF16)

        @pl.when(i == 0)
        def _():
            dg_ref[...] = dg

        @pl.when(i > 0)
        def _():
            dg_ref[...] += dg

    row = pl.BlockSpec((tr, D), lambda i: (i, 0))
    one = pl.BlockSpec((1, D), lambda i: (0, 0))
    return pl.pallas_call(
        body, name=name, grid=(S // tr,),
        in_specs=[row, one, row, row], out_specs=[row, row, one],
        out_shape=[jax.ShapeDtypeStruct((S, D), F32), jax.ShapeDtypeStruct((S, D), BF16),
                   jax.ShapeDtypeStruct((1, D), F32)],
        compiler_params=_cparams(("arbitrary",)),
    )(x, g, dh, dres)


def loss_head(x, g, target, *, name):
    S, D = x.shape
    tr = min(ROW_TILE, S)

    def body(x_ref, g_ref, t_ref, dx_ref, dg_ref, loss_ref):
        i = pl.program_id(0)
        xv, gv = x_ref[...], g_ref[...]
        err = xv * _rstd(xv) * gv - t_ref[...]
        part = 0.5 * jnp.sum(jnp.mean(err * err, axis=-1, keepdims=True), axis=0, keepdims=True)
        dx, dg = _rms_bwd_tile(xv, gv, err * (1.0 / D))
        dx_ref[...] = dx
        part = jnp.broadcast_to(part, loss_ref.shape)

        @pl.when(i == 0)
        def _():
            dg_ref[...] = dg
            loss_ref[...] = part

        @pl.when(i > 0)
        def _():
            dg_ref[...] += dg
            loss_ref[...] += part

    row = pl.BlockSpec((tr, D), lambda i: (i, 0))
    one = pl.BlockSpec((1, D), lambda i: (0, 0))
    return pl.pallas_call(
        body, name=name, grid=(S // tr,),
        in_specs=[row, one, row], out_specs=[row, one, pl.BlockSpec((1, 128), lambda i: (0, 0))],
        out_shape=[jax.ShapeDtypeStruct((S, D), F32), jax.ShapeDtypeStruct((1, D), F32),
                   jax.ShapeDtypeStruct((1, 128), F32)],
        compiler_params=_cparams(("arbitrary",)),
    )(x, g, target)


def rope_tables(S, n_cols):
    half = ROPE_DIM // 2
    inv_freq = ROPE_THETA ** (-jnp.arange(half, dtype=F32) / half)
    ang = jnp.arange(S, dtype=F32)[:, None] * inv_freq[None, :]
    cos, sin = jnp.cos(ang), jnp.sin(ang)
    z = jnp.zeros((S, HEAD_DIM - ROPE_DIM), F32)
    zh = jnp.zeros((S, half), F32)
    c = jnp.concatenate([cos, cos, jnp.ones_like(z)], axis=1)
    sa = jnp.concatenate([zh, sin, z], axis=1)
    sb = jnp.concatenate([-sin, zh, z], axis=1)
    return [jnp.tile(t, (1, n_cols // HEAD_DIM)) for t in (c, sa, sb)]


def rope_fwd(xqk, tables, *, name):
    n, half = xqk.shape[1], ROPE_DIM // 2

    def fn(x, c, sa, sb):
        return (x * c + pltpu.roll(x, half, 1) * sa + pltpu.roll(x, n - half, 1) * sb,)

    return ew(fn, [xqk] + list(tables), [BF16], name=name)[0]


def rope_bwd(dy, tables, *, name):
    n, half = dy.shape[1], ROPE_DIM // 2

    def fn(d, c, sa, sb):
        return (d * c + pltpu.roll(d * sa, n - half, 1) + pltpu.roll(d * sb, half, 1),)

    return ew(fn, [dy] + list(tables), [BF16], name=name)[0]


def _split3(x):
    h1 = x.astype(BF16)
    r1 = x - h1.astype(F32)
    h2 = r1.astype(BF16)
    return h1, h2, (r1 - h2.astype(F32)).astype(BF16)


def _split2(x):
    h1 = x.astype(BF16)
    return h1, (x - h1.astype(F32)).astype(BF16)


def _tri(n, cmp):
    r = lax.broadcasted_iota(jnp.int32, (n, n), 0)
    c = lax.broadcasted_iota(jnp.int32, (n, n), 1)
    return cmp(r, c).astype(BF16)


def fox_gate_fwd(fl, b, *, name):
    S, W = fl.shape
    tr = min(ROW_TILE, S)

    def body(fl_ref, b_ref, cum_ref, carry):
        i = pl.program_id(0)

        @pl.when(i == 0)
        def _():
            carry[...] = jnp.zeros_like(carry)

        lower = _tri(tr, lambda r, c: r >= c)
        cs = carry[...]
        for piece in _split3(_log_sigmoid(fl_ref[...] + b_ref[...])):
            cs = cs + _dot(lower, piece)
        cum_ref[...] = cs
        carry[...] = cs[tr - 1:tr, :]

    return pl.pallas_call(
        body, name=name, grid=(S // tr,),
        in_specs=[pl.BlockSpec((tr, W), lambda i: (i, 0)), pl.BlockSpec((1, W), lambda i: (0, 0))],
        out_specs=pl.BlockSpec((tr, W), lambda i: (i, 0)),
        out_shape=jax.ShapeDtypeStruct((S, W), F32),
        scratch_shapes=[pltpu.VMEM((1, W), F32)],
        compiler_params=_cparams(("arbitrary",)),
    )(fl, b)


def fox_gate_bwd(dcum, fl, b, *, name):
    S, W = fl.shape
    tr = min(ROW_TILE, S)
    nb = S // tr

    def body(dc_ref, fl_ref, b_ref, dfl_ref, db_ref, carry):
        i = pl.program_id(0)

        @pl.when(i == 0)
        def _():
            carry[...] = jnp.zeros_like(carry)

        upper = _tri(tr, lambda r, c: r <= c)
        cs = carry[...]
        for piece in _split3(dc_ref[...]):
            cs = cs + _dot(upper, piece)
        carry[...] = cs[0:1, :]
        dfl = cs * _sigmoid(-(fl_ref[...] + b_ref[...]))
        dfl_ref[...] = dfl
        db = jnp.sum(dfl, axis=0, keepdims=True)

        @pl.when(i == 0)
        def _():
            db_ref[...] = db

        @pl.when(i > 0)
        def _():
            db_ref[...] += db

    rev = pl.BlockSpec((tr, W), lambda i: (nb - 1 - i, 0))
    one = pl.BlockSpec((1, W), lambda i: (0, 0))
    return pl.pallas_call(
        body, name=name, grid=(nb,),
        in_specs=[rev, rev, one], out_specs=[rev, one],
        out_shape=[jax.ShapeDtypeStruct((S, W), F32), jax.ShapeDtypeStruct((1, W), F32)],
        scratch_shapes=[pltpu.VMEM((1, W), F32)],
        compiler_params=_cparams(("arbitrary",)),
    )(dcum, fl, b)


def _blk_iota(tq, tk):
    return (lax.broadcasted_iota(jnp.int32, (tq, tk), 0), lax.broadcasted_iota(jnp.int32, (tq, tk), 1))


def _cs(xb, tri):
    return _dot(xb, tri)


def _rowsum(xb):
    return jnp.sum(xb.astype(F32), axis=1, keepdims=True)


def _sb_block(qs, k, cmr, shift):
    z = _dot_nt(qs, k)
    strict = cmr < shift
    lb = jnp.minimum(z, 0.0) - jnp.log(1.0 + jnp.exp(-jnp.abs(z)))
    lom = jnp.where(strict, lb - z, 0.0).astype(BF16)
    return lb, lom, strict


def _att_tiles(S):
    return min(ATT_BQ, S), min(ATT_BK, S)


def _head_specs(S, tq):
    qspec = pl.BlockSpec((None, tq, HEAD_DIM), lambda h, i: (h, i, 0))
    kvspec = pl.BlockSpec((None, S, HEAD_DIM), lambda h, i: (h, 0, 0))
    vec = pl.BlockSpec((None, tq, 1), lambda h, i: (h, i, 0))
    return qspec, kvspec, vec


def sb_fwd(q, k, v, *, name):
    H, S, _ = q.shape
    tq, tk = _att_tiles(S)
    qspec, kvspec, vec = _head_specs(S, tq)

    def body(q_ref, k_ref, v_ref, o_ref, t_ref):
        i = pl.program_id(1)
        qs = q_ref[...] * ATTN_SCALE
        row, col = _blk_iota(tq, tk)
        cmr = col - row
        below = _tri(tk, lambda r, c: r > c)
        nkb = (i + 1) * (tq // tk)

        def step(n, carry):
            r_sum, acc = carry
            kb = nkb - 1 - n
            ks = pl.multiple_of(kb * tk, tk)
            lb, lom, strict = _sb_block(qs, k_ref[pl.ds(ks, tk), :], cmr, i * tq - kb * tk)
            w = jnp.where(strict, jnp.exp(lb + _cs(lom, below) + r_sum), 0.0)
            acc = acc + _dot(w.astype(BF16), v_ref[pl.ds(ks, tk), :])
            return r_sum + _rowsum(lom), acc

        r_sum, acc = lax.fori_loop(0, nkb, step, (jnp.zeros((tq, 1), F32), jnp.zeros((tq, HEAD_DIM), F32)))
        o_ref[...] = acc.astype(o_ref.dtype)
        t_ref[...] = r_sum

    return pl.pallas_call(
        body, name=name, grid=(H, S // tq),
        in_specs=[qspec, kvspec, kvspec], out_specs=[qspec, vec],
        out_shape=[jax.ShapeDtypeStruct((H, S, HEAD_DIM), BF16), jax.ShapeDtypeStruct((H, S, 1), F32)],
        compiler_params=_cparams(("parallel", "arbitrary")),
    )(q, k, v)


def sb_bwd(q, k, v, tot, do, *, name):
    H, S, _ = q.shape
    tq, tk = _att_tiles(S)
    qspec, kvspec, vec = _head_specs(S, tq)

    def body(q_ref, k_ref, v_ref, t_ref, do_ref, dq_ref, dk_ref, dv_ref):
        i = pl.program_id(1)

        @pl.when(i == 0)
        def _():
            dk_ref[...] = jnp.zeros_like(dk_ref)
            dv_ref[...] = jnp.zeros_like(dv_ref)

        qs, dov, t_all = q_ref[...] * ATTN_SCALE, do_ref[...], t_ref[...]
        row, col = _blk_iota(tq, tk)
        cmr = col - row
        upto = _tri(tk, lambda r, c: r <= c)
        before = _tri(tk, lambda r, c: r < c)

        def step(kb, carry):
            p_sum, e_sum, dq = carry
            ks = pl.multiple_of(kb * tk, tk)
            kv = k_ref[pl.ds(ks, tk), :]
            vv = v_ref[pl.ds(ks, tk), :]
            lb, lom, strict = _sb_block(qs, kv, cmr, i * tq - kb * tk)
            tail = t_all - p_sum - _cs(lom, upto)
            w = jnp.where(strict, jnp.exp(lb + tail), 0.0)
            e = _dot_nt(dov, vv) * w
            eb = e.astype(BF16)
            e_before = e_sum + _cs(eb, before)
            beta = jnp.exp(lb)
            dzb = jnp.where(strict, e - (e + e_before) * beta, 0.0).astype(BF16)
            dk_ref[pl.ds(ks, tk), :] += _dot_tn(dzb, qs)
            dv_ref[pl.ds(ks, tk), :] += _dot_tn(w.astype(BF16), dov)
            return (p_sum + _rowsum(lom), e_sum + _rowsum(eb),
                    dq + _dot(dzb, kv))

        zero = jnp.zeros((tq, 1), F32)
        _, _, dq = lax.fori_loop(0, (i + 1) * (tq // tk), step, (zero, zero, jnp.zeros((tq, HEAD_DIM), F32)))
        dq_ref[...] = dq * ATTN_SCALE

    full = jax.ShapeDtypeStruct((H, S, HEAD_DIM), F32)
    return pl.pallas_call(
        body, name=name, grid=(H, S // tq),
        in_specs=[qspec, kvspec, kvspec, vec, qspec], out_specs=[qspec, kvspec, kvspec],
        out_shape=[full, full, full],
        compiler_params=_cparams(("parallel", "arbitrary")),
    )(q, k, v, tot, do)


def _fox_logits(qs, k, cq, ck, cmr, shift):
    causal = cmr <= shift
    return jnp.where(causal, _dot_nt(qs, k) + cq - ck, NEG_INF), causal


def fox_fwd(q, k, v, cq, ck, *, name):
    H, S, _ = q.shape
    tq, tk = _att_tiles(S)
    qspec, kvspec, vec = _head_specs(S, tq)
    ckspec = pl.BlockSpec((None, S // tk, 1, tk), lambda h, i: (h, 0, 0, 0))

    def body(q_ref, k_ref, v_ref, cq_ref, ck_ref, o_ref, lse_ref):
        i = pl.program_id(1)
        qs, cqv = q_ref[...] * ATTN_SCALE, cq_ref[...]
        row, col = _blk_iota(tq, tk)
        cmr = col - row

        def step(kb, carry):
            m, l, acc = carry
            ks = pl.multiple_of(kb * tk, tk)
            s, _ = _fox_logits(qs, k_ref[pl.ds(ks, tk), :], cqv, ck_ref[kb], cmr, i * tq - kb * tk)
            m_new = jnp.maximum(m, jnp.max(s, axis=1, keepdims=True))
            alpha = jnp.exp(m - m_new)
            p = jnp.exp(s - m_new)
            l = alpha * l + jnp.sum(p, axis=1, keepdims=True)
            acc = alpha * acc + _dot(p.astype(BF16), v_ref[pl.ds(ks, tk), :])
            return m_new, l, acc

        m, l, acc = lax.fori_loop(0, (i + 1) * (tq // tk), step,
                                  (jnp.full((tq, 1), NEG_INF, F32), jnp.zeros((tq, 1), F32), jnp.zeros((tq, HEAD_DIM), F32)))
        o_ref[...] = (acc / l).astype(o_ref.dtype)
        lse_ref[...] = m + jnp.log(l)

    return pl.pallas_call(
        body, name=name, grid=(H, S // tq),
        in_specs=[qspec, kvspec, kvspec, vec, ckspec], out_specs=[qspec, vec],
        out_shape=[jax.ShapeDtypeStruct((H, S, HEAD_DIM), BF16), jax.ShapeDtypeStruct((H, S, 1), F32)],
        compiler_params=_cparams(("parallel", "arbitrary")),
    )(q, k, v, cq, ck)


def fox_bwd(q, k, v, o, lse, cq, ck, do, *, name):
    H, S, _ = q.shape
    tq, tk = _att_tiles(S)
    qspec, kvspec, vec = _head_specs(S, tq)
    ckspec = pl.BlockSpec((None, S // tk, 1, tk), lambda h, i: (h, 0, 0, 0))

    def body(q_ref, k_ref, v_ref, o_ref, lse_ref, cq_ref, ck_ref, do_ref, dq_ref, dk_ref, dv_ref, dcq_ref, dck_ref):
        i = pl.program_id(1)

        @pl.when(i == 0)
        def _():
            dk_ref[...] = jnp.zeros_like(dk_ref)
            dv_ref[...] = jnp.zeros_like(dv_ref)
            dck_ref[...] = jnp.zeros_like(dck_ref)

        qs, dov, cqv, lsev = q_ref[...] * ATTN_SCALE, do_ref[...], cq_ref[...], lse_ref[...]
        delta = jnp.sum(dov.astype(F32) * o_ref[...].astype(F32), axis=1, keepdims=True)
        row, col = _blk_iota(tq, tk)
        cmr = col - row

        def step(kb, carry):
            dq, dcq = carry
            ks = pl.multiple_of(kb * tk, tk)
            kv = k_ref[pl.ds(ks, tk), :]
            vv = v_ref[pl.ds(ks, tk), :]
            s, causal = _fox_logits(qs, kv, cqv, ck_ref[kb], cmr, i * tq - kb * tk)
            p = jnp.where(causal, jnp.exp(s - lsev), 0.0)
            ds = p * (_dot_nt(dov, vv) - delta)
            dck_ref[kb] += jnp.sum(ds, axis=0, keepdims=True)
            dsb = ds.astype(BF16)
            dk_ref[pl.ds(ks, tk), :] += _dot_tn(dsb, qs)
            dv_ref[pl.ds(ks, tk), :] += _dot_tn(p.astype(BF16), dov)
            return dq + _dot(dsb, kv), dcq + jnp.sum(ds, axis=1, keepdims=True)

        dq, dcq = lax.fori_loop(0, (i + 1) * (tq // tk), step, (jnp.zeros((tq, HEAD_DIM), F32), jnp.zeros((tq, 1), F32)))
        dq_ref[...] = dq * ATTN_SCALE
        dcq_ref[...] = dcq

    full = jax.ShapeDtypeStruct((H, S, HEAD_DIM), F32)
    return pl.pallas_call(
        body, name=name, grid=(H, S // tq),
        in_specs=[qspec, kvspec, kvspec, qspec, vec, vec, ckspec, qspec],
        out_specs=[qspec, kvspec, kvspec, vec, ckspec],
        out_shape=[full, full, full, jax.ShapeDtypeStruct((H, S, 1), F32),
                   jax.ShapeDtypeStruct((H, S // tk, 1, tk), F32)],
        compiler_params=_cparams(("parallel", "arbitrary")),
    )(q, k, v, o, lse, cq, ck, do)


def _swa_specs(S, tq):
    qspec = pl.BlockSpec((None, SWA_GROUP, tq, HEAD_DIM), lambda g, i: (g, 0, i, 0))
    kvspec = pl.BlockSpec((None, S + SWA_WINDOW, HEAD_DIM), lambda g, i: (g, 0, 0))
    vec = pl.BlockSpec((None, SWA_GROUP, tq, 1), lambda g, i: (g, 0, i, 0))
    sink = pl.BlockSpec((None, SWA_GROUP * tq, 1), lambda g, i: (g, 0, 0))
    return qspec, kvspec, vec, sink


def _swa_logits(q2, kw, i, tq):
    rows = q2.shape[0]
    r = lax.broadcasted_iota(jnp.int32, (rows, 2 * tq), 0)
    c = lax.broadcasted_iota(jnp.int32, (rows, 2 * tq), 1)
    diff = (r & (tq - 1)) + tq - c
    ok = (diff >= 0) & (diff < SWA_WINDOW) & (c + (i - 1) * tq >= 0)
    return jnp.where(ok, _dot_nt(q2, kw) * ATTN_SCALE, NEG_INF), ok


def swa_fwd(q, kp, vp, sink, *, name):
    _, G, S, _ = q.shape
    tq = ATT_BLK
    qspec, kvspec, vec, sinkspec = _swa_specs(S, tq)

    def body(q_ref, k_ref, v_ref, s_ref, o_ref, lse_ref):
        i = pl.program_id(1)
        q2 = q_ref[...].reshape(G * tq, HEAD_DIM)
        ws = pl.multiple_of(i * tq, tq)
        logits, _ = _swa_logits(q2, k_ref[pl.ds(ws, 2 * tq), :], i, tq)
        sk = s_ref[...]
        m = jnp.maximum(jnp.max(logits, axis=1, keepdims=True), sk)
        e = jnp.exp(logits - m)
        den = jnp.sum(e, axis=1, keepdims=True) + jnp.exp(sk - m)
        o = _dot((e / den).astype(BF16), v_ref[pl.ds(ws, 2 * tq), :])
        o_ref[...] = o.reshape(G, tq, HEAD_DIM).astype(o_ref.dtype)
        lse_ref[...] = (m + jnp.log(den)).reshape(G, tq, 1)

    return pl.pallas_call(
        body, name=name, grid=(SWA_KV_HEADS, S // tq),
        in_specs=[qspec, kvspec, kvspec, sinkspec], out_specs=[qspec, vec],
        out_shape=[jax.ShapeDtypeStruct(q.shape, BF16), jax.ShapeDtypeStruct((SWA_KV_HEADS, G, S, 1), F32)],
        compiler_params=_cparams(("parallel", "arbitrary")),
    )(q, kp, vp, sink)


def swa_bwd(q, kp, vp, sink, o, lse, do, *, name):
    _, G, S, _ = q.shape
    tq = ATT_BLK
    qspec, kvspec, vec, sinkspec = _swa_specs(S, tq)

    def body(q_ref, k_ref, v_ref, s_ref, o_ref, lse_ref, do_ref, dq_ref, dk_ref, dv_ref, dsink_ref):
        i = pl.program_id(1)

        @pl.when(i == 0)
        def _():
            dk_ref[...] = jnp.zeros_like(dk_ref)
            dv_ref[...] = jnp.zeros_like(dv_ref)

        q2 = q_ref[...].reshape(G * tq, HEAD_DIM)
        do2 = do_ref[...].reshape(G * tq, HEAD_DIM)
        o2 = o_ref[...].reshape(G * tq, HEAD_DIM)
        lse2 = lse_ref[...].reshape(G * tq, 1)
        ws = pl.multiple_of(i * tq, tq)
        kw = k_ref[pl.ds(ws, 2 * tq), :]
        vw = v_ref[pl.ds(ws, 2 * tq), :]
        logits, ok = _swa_logits(q2, kw, i, tq)
        p = jnp.where(ok, jnp.exp(logits - lse2), 0.0)
        delta = jnp.sum(do2.astype(F32) * o2.astype(F32), axis=1, keepdims=True)
        ds = p * (_dot_nt(do2, vw) - delta)
        dsb = ds.astype(BF16)
        dq_ref[...] = (_dot(dsb, kw) * ATTN_SCALE).reshape(G, tq, HEAD_DIM)
        dk_ref[pl.ds(ws, 2 * tq), :] += _dot_tn(dsb, q2) * ATTN_SCALE
        dv_ref[pl.ds(ws, 2 * tq), :] += _dot_tn(p.astype(BF16), do2)
        dsink_ref[...] = (-jnp.exp(s_ref[...] - lse2) * delta).reshape(G, tq, 1)

    kvshape = jax.ShapeDtypeStruct(kp.shape, F32)
    return pl.pallas_call(
        body, name=name, grid=(SWA_KV_HEADS, S // tq),
        in_specs=[qspec, kvspec, kvspec, sinkspec, qspec, vec, qspec],
        out_specs=[qspec, kvspec, kvspec, vec],
        out_shape=[jax.ShapeDtypeStruct(q.shape, F32), kvshape, kvshape,
                   jax.ShapeDtypeStruct((SWA_KV_HEADS, G, S, 1), F32)],
        compiler_params=_cparams(("parallel", "arbitrary")),
    )(q, kp, vp, sink, o, lse, do)


def _adamw_tile(w, g, m, v):
    m = ADAM_B1 * m + (1.0 - ADAM_B1) * g
    v = ADAM_B2 * v + (1.0 - ADAM_B2) * (g * g)
    m_hat = m / (1.0 - ADAM_B1 ** ADAM_STEP)
    v_hat = v / (1.0 - ADAM_B2 ** ADAM_STEP)
    delta = -ADAM_LR * (m_hat / (jnp.sqrt(v_hat) + ADAM_EPS) + ADAM_WD * w)
    return g, delta, m, v


def adamw(gfull, t, w, m, v, *, name):
    off, K, ns, _ = t
    sb = off // K
    nat = pl.BlockSpec((K, SLAB), lambda q: (0, q))

    def body(g_ref, w_ref, m_ref, v_ref, *outs):
        for o, r in zip(outs, _adamw_tile(w_ref[...], g_ref[...], m_ref[...], v_ref[...])):
            o[...] = r

    return pl.pallas_call(
        body, name=name, grid=(ns,),
        in_specs=[pl.BlockSpec((K, SLAB), lambda q: (sb + q, 0)), nat, nat, nat],
        out_specs=[nat] * 4, out_shape=[jax.ShapeDtypeStruct(w.shape, F32)] * 4,
        compiler_params=_cparams(("parallel",)),
    )(gfull, w, m, v)


def adamw_small(g, w, m, v, *, name):
    def body(g_ref, w_ref, m_ref, v_ref, *outs):
        for o, r in zip(outs, _adamw_tile(w_ref[...], g_ref[...], m_ref[...], v_ref[...])):
            o[...] = r

    return pl.pallas_call(body, name=name, out_shape=[jax.ShapeDtypeStruct(w.shape, F32)] * 4)(g, w, m, v)


MESH = pl.DeviceIdType.MESH
HBM = pl.BlockSpec(memory_space=pl.ANY)


def _place():
    x, y, c = lax.axis_index("x"), lax.axis_index("y"), lax.axis_index("c")
    others = [(1 - x, y), (x, 1 - y), (1 - x, 1 - y)]
    return x, y, c, others


def _rcopy(src, dst, send_sems, recv_sems, k, to):
    return pltpu.make_async_remote_copy(src_ref=src, dst_ref=dst, send_sem=send_sems.at[k], recv_sem=recv_sems.at[k],
                                        device_id=to, device_id_type=MESH)


def _dma_sems(*counts):
    return [pltpu.SemaphoreType.DMA((n,)) for n in counts]


DMA_UNIT_ROWS = 128
DMA_PIECES = 4
DMA_PIECES_LOCAL = 8


def _row_pieces(h, n):
    units = h // DMA_UNIT_ROWS
    n = min(n, units)
    base, extra = divmod(units, n)
    sizes = [(base + (k < extra)) * DMA_UNIT_ROWS for k in range(n)]
    return [(sum(sizes[:k]), sizes[k]) for k in range(n)]


def _start_pieces(make, h, n):
    for s0, sz in _row_pieces(h, n):
        make(s0, sz).start()
    return make(0, h)


def allgather_weights(bufs, *, name):
    n = len(bufs)

    def body(*refs):
        outs = refs[n:2 * n]
        send_sems, recv_sems = refs[2 * n:]
        x, y, c, others = _place()
        sibling = (x, y, 1 - c)

        def rows(i, chip, s0, sz):
            return outs[i].at[2 * chip[0] + chip[1], pl.ds(s0, sz)]

        first, passed = [], []
        for i in range(n):
            h = bufs[i].shape[1] // 2
            for f, chip in enumerate(others):
                first.append(_start_pieces(
                    lambda s0, sz: _rcopy(rows(i, (x, y), c * h + s0, sz), rows(i, (x, y), c * h + s0, sz),
                                          send_sems, recv_sems, 6 * i + f, (*chip, c)), h, DMA_PIECES))
        for i in range(n):
            h = bufs[i].shape[1] // 2
            for f, chip in enumerate(others):
                blk = rows(i, chip, c * h, h)
                _rcopy(blk, blk, send_sems, recv_sems, 6 * i + f, (*chip, c)).wait_recv()
                passed.append(_start_pieces(
                    lambda s0, sz: _rcopy(rows(i, chip, c * h + s0, sz), rows(i, chip, c * h + s0, sz),
                                          send_sems, recv_sems, 6 * i + 3 + f, sibling), h, DMA_PIECES))
        for i in range(n):
            h = bufs[i].shape[1] // 2
            for f, chip in enumerate(others):
                blk = rows(i, chip, (1 - c) * h, h)
                _rcopy(blk, blk, send_sems, recv_sems, 6 * i + 3 + f, sibling).wait_recv()
        for cp in first + passed:
            cp.wait_send()

    return pl.pallas_call(
        body, name=name, in_specs=[HBM] * n, out_specs=[HBM] * n,
        out_shape=[jax.ShapeDtypeStruct(b.shape, b.dtype) for b in bufs],
        input_output_aliases={i: i for i in range(n)},
        scratch_shapes=_dma_sems(6 * n, 6 * n),
    )(*bufs)


def swap_halves(grads, *, name):
    n = len(grads)

    def body(*refs):
        ins, theirs = refs[:n], refs[n:2 * n]
        send_sems, recv_sems = refs[2 * n:]
        x, y, c, _ = _place()
        for i in range(n):
            h = grads[i].shape[1] // 2
            for k in range(N_CHIPS):
                for s0, sz in _row_pieces(h, DMA_PIECES):
                    _rcopy(ins[i].at[k, pl.ds((1 - c) * h + s0, sz)], theirs[i].at[k, pl.ds(s0, sz)],
                           send_sems, recv_sems, i, (x, y, 1 - c)).start()
        for i in range(n):
            h = grads[i].shape[1] // 2
            _rcopy(ins[i].at[:, pl.ds((1 - c) * h, h)], theirs[i], send_sems, recv_sems, i, (x, y, 1 - c)).wait()

    return pl.pallas_call(
        body, name=name, in_specs=[HBM] * n, out_specs=[HBM] * n,
        out_shape=[jax.ShapeDtypeStruct((N_CHIPS, g.shape[1] // 2, SLAB), g.dtype) for g in grads],
        scratch_shapes=_dma_sems(n, n))(*grads)


def scatter_chips(parts, *, name):
    n = len(parts)

    def body(*refs):
        ins, got = refs[:n], refs[n:2 * n]
        send_sems, recv_sems = refs[2 * n:]
        x, y, c, others = _place()
        cps = []
        for i in range(n):
            for f, (px, py) in enumerate(others):
                cps.append(_start_pieces(
                    lambda s0, sz: _rcopy(ins[i].at[2 * px + py, pl.ds(s0, sz)], got[i].at[f, pl.ds(s0, sz)],
                                          send_sems, recv_sems, 3 * i + f, (px, py, c)), parts[i].shape[1], DMA_PIECES))
        for cp in cps:
            cp.wait()

    return pl.pallas_call(
        body, name=name, in_specs=[HBM] * n, out_specs=[HBM] * n,
        out_shape=[jax.ShapeDtypeStruct((3,) + p.shape[1:], p.dtype) for p in parts],
        scratch_shapes=_dma_sems(3 * n, 3 * n),
    )(*parts)


def join_halves(bufs, *, name):
    n = len(bufs)

    def body(*refs):
        outs = refs[n:2 * n]
        send_sems, recv_sems = refs[2 * n:]
        x, y, c, _ = _place()
        sibling = (x, y, 1 - c)
        cps = []
        for i in range(n):
            h = bufs[i].shape[0] // 2
            snd = _start_pieces(
                lambda s0, sz: _rcopy(outs[i].at[pl.ds(c * h + s0, sz)], outs[i].at[pl.ds(c * h + s0, sz)],
                                      send_sems, recv_sems, i, sibling), h, 2 * DMA_PIECES_LOCAL)
            theirs = outs[i].at[pl.ds((1 - c) * h, h)]
            cps.append((snd, _rcopy(theirs, theirs, send_sems, recv_sems, i, sibling)))
        for snd, rcv in cps:
            snd.wait_send()
            rcv.wait_recv()

    return pl.pallas_call(
        body, name=name, in_specs=[HBM] * n, out_specs=[HBM] * n,
        out_shape=[jax.ShapeDtypeStruct(b.shape, b.dtype) for b in bufs],
        input_output_aliases={i: i for i in range(n)},
        scratch_shapes=_dma_sems(n, n),
    )(*bufs)


def allreduce_small(v, *, name):
    rows, n = v.shape

    def body(x_ref, sum_ref, all_ref, send_sems, recv_sems, local_sem):
        x, y, c, others = _place()
        me, sibling = (x, y, c), (x, y, 1 - c)

        def blk(px, py, pc):
            return all_ref.at[pl.ds((4 * px + 2 * py + pc) * rows, rows), :]

        def copy(k, block, to, src=None):
            return _rcopy(blk(*block) if src is None else src, blk(*block), send_sems, recv_sems, k, to)

        mine = pltpu.make_async_copy(x_ref, blk(*me), local_sem)
        mine.start()
        first = [copy(0, me, sibling, src=x_ref)]
        first += [copy(1 + f, me, (*chip, c), src=x_ref) for f, chip in enumerate(others)]
        for cp in first:
            cp.start()
        passed = [copy(4 + f, (*chip, c), sibling) for f, chip in enumerate(others)]
        for f, chip in enumerate(others):
            copy(1 + f, (*chip, c), me).wait_recv()
            passed[f].start()
        copy(0, sibling, me).wait_recv()
        for f, chip in enumerate(others):
            copy(4 + f, (*chip, 1 - c), me).wait_recv()
        for cp in first + passed:
            cp.wait_send()
        mine.wait()
        acc = all_ref[pl.ds(0, rows), :]
        for d in range(1, N_DEVICES):
            acc = acc + all_ref[pl.ds(d * rows, rows), :]
        sum_ref[...] = acc

    vm = pl.BlockSpec(memory_space=pltpu.VMEM)
    return pl.pallas_call(
        body, name=name, in_specs=[vm], out_specs=[vm, vm],
        out_shape=[jax.ShapeDtypeStruct((rows, n), F32), jax.ShapeDtypeStruct((N_DEVICES * rows, n), F32)],
        scratch_shapes=_dma_sems(7, 7) + [pltpu.SemaphoreType.DMA],
    )(v)[0]


def add_pairs(grad, theirs, where, *, name):
    h = theirs.shape[1]
    spec = pl.BlockSpec((None, h, SLAB), lambda k, w: (k, 0, 0))

    def body(w_ref, a_ref, b_ref, o_ref):
        del w_ref
        o_ref[...] = (a_ref[...].astype(F32) + b_ref[...].astype(F32)).astype(o_ref.dtype)

    return pl.pallas_call(
        body, name=name,
        grid_spec=pltpu.PrefetchScalarGridSpec(
            num_scalar_prefetch=1, grid=(N_CHIPS,),
            in_specs=[pl.BlockSpec((None, h, SLAB), lambda k, w: (k, w[1], 0)), spec], out_specs=spec),
        out_shape=jax.ShapeDtypeStruct(theirs.shape, theirs.dtype),
        compiler_params=_cparams(("parallel",)))(where, grad, theirs)


def add_chips(pair, got, where, *, name):
    h = pair.shape[1]
    tr = h // 2

    def body(w_ref, a_ref, b_ref, o_ref):
        del w_ref
        acc = a_ref[...].astype(F32)
        for f in range(3):
            acc = acc + b_ref[f].astype(F32)
        o_ref[...] = acc

    return pl.pallas_call(
        body, name=name,
        grid_spec=pltpu.PrefetchScalarGridSpec(
            num_scalar_prefetch=1, grid=(2,),
            in_specs=[pl.BlockSpec((None, tr, SLAB), lambda i, w: (w[0], i, 0)),
                      pl.BlockSpec((3, tr, SLAB), lambda i, w: (0, i, 0))],
            out_specs=pl.BlockSpec((tr, SLAB), lambda i, w: (2 * w[1] + i, 0))),
        out_shape=jax.ShapeDtypeStruct((2 * h, SLAB), F32),
        compiler_params=_cparams(("parallel",)))(where, pair, got)


DEPTH = 4
MIXER = (0, 1, 2, 0)
W_IN_COLS = (768, 320, 772)
W_IN_PAD = (768, 512, 1024)
MATS = ("up", "down", "inp", "out", "gate", "proj")
MAT_ARG = dict(up="w_up", down="w_down", inp="w_in", out="w_out", gate="w_ple_gate", proj="w_ple_proj")
GAINS = ("attn_norm", "mlp_norm", "ple_norm")
N_SMALL = 16
KINDS = ("grad_", "delta_", "new_m_", "new_v_")


def _layout(kind):
    ns_in = W_IN_PAD[kind] // SLAB
    off = 8192 + 1024 * ns_in
    lay = dict(up=(0, 1024, 4, False), down=(4096, 1024, 4, True), inp=(8192, 1024, ns_in, False),
               out=(off, 256, 4, True), gate=(off + 1024, 256, 4, True), proj=(off + 2048, 256, 1, False))
    return lay, off + 2304


def _to_slabs(w):
    k, c = w.shape
    return w.reshape(k, c // SLAB, SLAB).transpose(1, 0, 2).reshape(-1, SLAB)


def _pad_cols(w, n):
    return jnp.pad(w, ((0, 0), (0, n - w.shape[1])))


def _heads(x2d, n):
    return x2d.reshape(x2d.shape[0], n, HEAD_DIM).transpose(1, 0, 2)


def _unheads(x3d):
    n, s, _ = x3d.shape
    return x3d.transpose(1, 0, 2).reshape(s, n * HEAD_DIM)


def _chip_cols(x2d, c, cpad):
    s = x2d.shape[0]
    return jnp.pad(x2d.reshape(s, N_CHIPS, c), ((0, 0), (0, 0), (0, cpad - c))).reshape(s, N_CHIPS * cpad)


def _unchip_cols(x2d, c, cpad):
    s = x2d.shape[0]
    return x2d.reshape(s, N_CHIPS, cpad)[:, :, :c].reshape(s, N_CHIPS * c)


def _add_res(acc, res):
    return (acc + res,)


def _relu2(acc):
    return acc, jnp.square(jnp.maximum(acc, 0.0))


def _relu2_bwd(acc, u):
    return (acc * (2.0 * jnp.maximum(u.astype(F32), 0.0)),)


def _ple_fwd(acc, x2, pp):
    return x2 + pp * _sigmoid(acc), acc


def _ple_bwd(dx, pp, gl):
    gate = _sigmoid(gl)
    return dx * gate, dx * pp * gate * (1.0 - gate)


def _layer_fwd(i, kind, x0, p_bf, wg, lay, gains, extra, tabs):
    s = x0.shape[0]
    an, mn, pn = gains
    sv = dict(x0=x0)
    h1 = rms_fwd(x0, an, name=f"attn_norm_{i}")
    if kind == 0:
        proj = mm_nn(h1, wg, lay["inp"], name=f"w_in_{i}")[0]
        qkv = proj.reshape(s, 3, N_HEADS, HEAD_DIM).transpose(1, 2, 0, 3)
        o, tot = sb_fwd(qkv[0], qkv[1], qkv[2], name=f"sb_fwd_{i}")
        sv.update(qkv=qkv, tot=tot)
    elif kind == 1:
        projp = mm_nn(h1, wg, lay["inp"], name=f"w_in_{i}", out_dtypes=(F32,))[0]
        proj = _unchip_cols(projp, W_IN_COLS[1], W_IN_PAD[1])
        nq = N_HEADS * HEAD_DIM
        nqk = nq + SWA_KV_HEADS * HEAD_DIM
        qk = rope_fwd(proj[:, :nqk], tabs, name=f"rope_{i}")
        q = _heads(qk[:, :nq], N_HEADS).reshape(SWA_KV_HEADS, SWA_GROUP, s, HEAD_DIM)
        front = ((0, 0), (SWA_WINDOW, 0), (0, 0))
        kp = jnp.pad(_heads(qk[:, nq:], SWA_KV_HEADS), front)
        vp = jnp.pad(_heads(proj[:, nqk:].astype(BF16), SWA_KV_HEADS), front)
        sink = jnp.repeat(extra.reshape(SWA_KV_HEADS, SWA_GROUP), ATT_BLK, axis=1)[:, :, None]
        o4, lse = swa_fwd(q, kp, vp, sink, name=f"swa_fwd_{i}")
        o = o4.reshape(N_HEADS, s, HEAD_DIM)
        sv.update(q=q, kp=kp, vp=vp, sink=sink, o4=o4, lse=lse)
    else:
        projp = mm_nn(h1, wg, lay["inp"], name=f"w_in_{i}", out_dtypes=(F32,))[0]
        proj = _unchip_cols(projp, W_IN_COLS[2], W_IN_PAD[2])
        nqkv = 3 * N_HEADS * HEAD_DIM
        qkv = proj[:, :nqkv].astype(BF16).reshape(s, 3, N_HEADS, HEAD_DIM).transpose(1, 2, 0, 3)
        fl = _pad_cols(proj[:, nqkv:], 128)
        bp = _pad_cols(extra[None], 128)
        cum_t = fox_gate_fwd(fl, bp, name=f"gate_fwd_{i}")[:, :N_HEADS].T
        cq = cum_t[:, :, None]
        ck = cum_t.reshape(N_HEADS, s // min(ATT_BK, s), 1, min(ATT_BK, s))
        o, lse = fox_fwd(qkv[0], qkv[1], qkv[2], cq, ck, name=f"fox_fwd_{i}")
        sv.update(qkv=qkv, fl=fl, bp=bp, cq=cq, ck=ck, o=o, lse=lse)
    a = _unheads(o)
    x1 = mm_nn(a, wg, lay["out"], name=f"w_out_{i}", epi=_add_res, extras=(x0,), out_dtypes=(F32,))[0]
    h2 = rms_fwd(x1, mn, name=f"mlp_norm_{i}")
    u, r = mm_nn(h2, wg, lay["up"], name=f"w_up_{i}", epi=_relu2, out_dtypes=(BF16, BF16))
    x2 = mm_nn(r, wg, lay["down"], name=f"w_down_{i}", epi=_add_res, extras=(x1,), out_dtypes=(F32,))[0]
    h3 = rms_fwd(x2, pn, name=f"ple_norm_{i}")
    pp = mm_nn(p_bf, wg, lay["proj"], name=f"w_ple_proj_{i}", out_dtypes=(F32,))[0]
    x3, gl = mm_nn(h3, wg, lay["gate"], name=f"w_ple_gate_{i}", epi=_ple_fwd, extras=(x2, pp), out_dtypes=(F32, F32))
    sv.update(h1=h1, a=a, x1=x1, h2=h2, u=u, r=r, x2=x2, h3=h3, pp=pp, gl=gl)
    return x3, sv


def _layer_bwd(i, kind, dx3, sv, p_bf, wg, lay, n_rows, gains, tabs):
    s = dx3.shape[0]
    an, mn, pn = gains
    g = lax.empty((N_CHIPS, n_rows, SLAB), BF16)
    d_pp, d_gl = ew(_ple_bwd, [dx3, sv["pp"], sv["gl"]], [BF16, BF16], name=f"ple_bwd_{i}")
    g = mm_tn(p_bf, d_pp, g, lay["proj"], name=f"dw_ple_proj_{i}")
    g = mm_tn(sv["h3"], d_gl, g, lay["gate"], name=f"dw_ple_gate_{i}")
    d_h3 = mm_nt(d_gl, wg, lay["gate"], name=f"dx_ple_gate_{i}", out_dtypes=(F32,))[0]
    dx2, dx2b, d_pn = rms_bwd(sv["x2"], pn, d_h3, dx3, name=f"ple_norm_bwd_{i}")
    g = mm_tn(sv["r"], dx2b, g, lay["down"], name=f"dw_down_{i}")
    d_u = mm_nt(dx2b, wg, lay["down"], name=f"dx_down_{i}", epi=_relu2_bwd, extras=(sv["u"],))[0]
    g = mm_tn(sv["h2"], d_u, g, lay["up"], name=f"dw_up_{i}")
    d_h2 = mm_nt(d_u, wg, lay["up"], name=f"dx_up_{i}", out_dtypes=(F32,))[0]
    dx1, dx1b, d_mn = rms_bwd(sv["x1"], mn, d_h2, dx2, name=f"mlp_norm_bwd_{i}")
    g = mm_tn(sv["a"], dx1b, g, lay["out"], name=f"dw_out_{i}")
    d_a = mm_nt(dx1b, wg, lay["out"], name=f"dx_out_{i}")[0]
    do = _heads(d_a, N_HEADS)
    d_extra = None
    if kind == 0:
        qkv = sv["qkv"]
        dq, dk, dv = sb_bwd(qkv[0], qkv[1], qkv[2], sv["tot"], do, name=f"sb_bwd_{i}")
        d_proj = jnp.stack([dq, dk, dv]).transpose(2, 0, 1, 3).reshape(s, 3 * N_HEADS * HEAD_DIM).astype(BF16)
    elif kind == 1:
        do4 = do.reshape(SWA_KV_HEADS, SWA_GROUP, s, HEAD_DIM)
        dq, dkp, dvp, dsr = swa_bwd(sv["q"], sv["kp"], sv["vp"], sv["sink"], sv["o4"], sv["lse"], do4, name=f"swa_bwd_{i}")
        dqk = jnp.concatenate([_unheads(dq.reshape(N_HEADS, s, HEAD_DIM)), _unheads(dkp[:, SWA_WINDOW:])], axis=1)
        dqk = rope_bwd(dqk, tabs, name=f"rope_bwd_{i}")
        d_proj = jnp.concatenate([dqk, _unheads(dvp[:, SWA_WINDOW:]).astype(BF16)], axis=1)
        d_proj = _chip_cols(d_proj, W_IN_COLS[1], W_IN_PAD[1])
        d_extra = jnp.sum(dsr[..., 0], axis=2).reshape(N_HEADS)
    else:
        qkv = sv["qkv"]
        dq, dk, dv, dcq, dck = fox_bwd(qkv[0], qkv[1], qkv[2], sv["o"], sv["lse"], sv["cq"], sv["ck"], do, name=f"fox_bwd_{i}")
        dcum = _pad_cols((dcq[:, :, 0] - dck.reshape(N_HEADS, s)).T, 128)
        dfl, dbp = fox_gate_bwd(dcum, sv["fl"], sv["bp"], name=f"gate_bwd_{i}")
        d_qkv = jnp.stack([dq, dk, dv]).transpose(2, 0, 1, 3).reshape(s, 3 * N_HEADS * HEAD_DIM)
        d_proj = jnp.concatenate([d_qkv, dfl[:, :N_HEADS]], axis=1).astype(BF16)
        d_proj = _chip_cols(d_proj, W_IN_COLS[2], W_IN_PAD[2])
        d_extra = dbp[0, :N_HEADS]
    g = mm_tn(sv["h1"], d_proj, g, lay["inp"], name=f"dw_in_{i}")
    d_h1 = mm_nt(d_proj, wg, lay["inp"], name=f"dx_in_{i}", out_dtypes=(F32,))[0]
    dx0, _, d_an = rms_bwd(sv["x0"], an, d_h1, dx1, name=f"attn_norm_bwd_{i}")
    return dx0, g, (d_an, d_mn, d_pn), d_extra


def _small_rows(a, prefix):
    rows = [a[f"{prefix}{n}_{i}"] for i in range(DEPTH) for n in GAINS] + [a[f"{prefix}final_norm"]]
    rows += [_pad_cols(a[f"{prefix}{n}"][None], D_MODEL)[0] for n in ("sinks_1", "b_forget_2")]
    return jnp.stack(rows + [jnp.zeros((D_MODEL,), F32)])


def _train_step(a):
    x = a["x"][0]
    tabs = rope_tables(x.shape[0], (N_HEADS + SWA_KV_HEADS) * HEAD_DIM)
    lays = [_layout(k) for k in MIXER]

    def natural(prefix, i, m):
        w = a[f"{prefix}{MAT_ARG[m]}_{i}"]
        return _pad_cols(w, W_IN_PAD[MIXER[i]]) if m == "inp" else w

    chip = 2 * lax.axis_index("x") + lax.axis_index("y")
    where = jnp.stack([chip, lax.axis_index("c")]).astype(jnp.int32)
    packed = [jnp.concatenate([_to_slabs(natural("", i, m).astype(BF16)) for m in MATS], axis=0) for i in range(DEPTH)]
    wgs = allgather_weights(
        [lax.dynamic_update_slice(lax.empty((N_CHIPS,) + pk.shape, BF16), pk[None], (chip, 0, 0)) for pk in packed],
        name="allgather_weights")

    gains = [tuple(a[f"{n}_{i}"][None] for n in GAINS) for i in range(DEPTH)]
    extras = [None, a["sinks_1"], a["b_forget_2"], None]
    p_bf = [a["p"][i, 0].astype(BF16) for i in range(DEPTH)]

    saved = []
    for i in range(DEPTH):
        x, sv = _layer_fwd(i, MIXER[i], x, p_bf[i], wgs[i], lays[i][0], gains[i], extras[i], tabs)
        saved.append(sv)
    dx, d_final, loss = loss_head(x, a["final_norm"][None], a["loss_target"][0], name="loss_head")

    small = [None] * N_SMALL
    small[12] = d_final[0]
    small[15] = _pad_cols(loss[:, :1], D_MODEL)[0]
    grads = [None] * DEPTH
    for i in reversed(range(DEPTH)):
        dx, grads[i], d_gains, d_extra = _layer_bwd(i, MIXER[i], dx, saved[i], p_bf[i], wgs[i], lays[i][0], lays[i][1],
                                                    gains[i], tabs)
        for j in range(3):
            small[3 * i + j] = d_gains[j][0]
        if d_extra is not None:
            small[12 + MIXER[i]] = _pad_cols(d_extra[None], D_MODEL)[0]
    small = allreduce_small(jnp.stack(small), name="allreduce_small")

    theirs = swap_halves(grads, name="swap_halves")
    pair = [add_pairs(grads[i], theirs[i], where, name=f"add_pairs_{i}") for i in range(DEPTH)]
    got = scatter_chips(pair, name="scatter_chips")
    gfull = join_halves([add_chips(pair[i], got[i], where, name=f"add_chips_{i}") for i in range(DEPTH)],
                        name="join_halves")

    out = {"loss": small[15, 0], "grad_x": dx[None]}
    for i in range(DEPTH):
        for m in MATS:
            res = adamw(gfull[i], lays[i][0][m], natural("", i, m), natural("m_", i, m), natural("v_", i, m),
                        name=f"adamw_{MAT_ARG[m]}_{i}")
            cols = a[f"{MAT_ARG[m]}_{i}"].shape[1]
            for kd, r in zip(KINDS, res):
                out[f"{kd}{MAT_ARG[m]}_{i}"] = r[:, :cols]
    res = adamw_small(small, _small_rows(a, ""), _small_rows(a, "m_"), _small_rows(a, "v_"), name="adamw_small")
    for kd, r in zip(KINDS, res):
        for i in range(DEPTH):
            for j, n in enumerate(GAINS):
                out[f"{kd}{n}_{i}"] = r[3 * i + j]
        out[f"{kd}final_norm"] = r[12]
        out[f"{kd}sinks_1"] = r[13, :N_HEADS]
        out[f"{kd}b_forget_2"] = r[14, :N_HEADS]
    return out


def _weight_names():
    names = []
    for i in range(DEPTH):
        names += [f"attn_norm_{i}", f"w_in_{i}", f"w_out_{i}"] + [[], ["sinks_1"], ["b_forget_2"]][MIXER[i]]
        names += [f"mlp_norm_{i}", f"w_up_{i}", f"w_down_{i}", f"ple_norm_{i}", f"w_ple_gate_{i}", f"w_ple_proj_{i}"]
    return names + ["final_norm"]


def kernel(x, p, attn_norm_0, w_in_0, w_out_0, mlp_norm_0, w_up_0, w_down_0, ple_norm_0, w_ple_gate_0, w_ple_proj_0, attn_norm_1, w_in_1, w_out_1, sinks_1, mlp_norm_1, w_up_1, w_down_1, ple_norm_1, w_ple_gate_1, w_ple_proj_1, attn_norm_2, w_in_2, w_out_2, b_forget_2, mlp_norm_2, w_up_2, w_down_2, ple_norm_2, w_ple_gate_2, w_ple_proj_2, attn_norm_3, w_in_3, w_out_3, mlp_norm_3, w_up_3, w_down_3, ple_norm_3, w_ple_gate_3, w_ple_proj_3, final_norm, loss_target, m_attn_norm_0, m_w_in_0, m_w_out_0, m_mlp_norm_0, m_w_up_0, m_w_down_0, m_ple_norm_0, m_w_ple_gate_0, m_w_ple_proj_0, m_attn_norm_1, m_w_in_1, m_w_out_1, m_sinks_1, m_mlp_norm_1, m_w_up_1, m_w_down_1, m_ple_norm_1, m_w_ple_gate_1, m_w_ple_proj_1, m_attn_norm_2, m_w_in_2, m_w_out_2, m_b_forget_2, m_mlp_norm_2, m_w_up_2, m_w_down_2, m_ple_norm_2, m_w_ple_gate_2, m_w_ple_proj_2, m_attn_norm_3, m_w_in_3, m_w_out_3, m_mlp_norm_3, m_w_up_3, m_w_down_3, m_ple_norm_3, m_w_ple_gate_3, m_w_ple_proj_3, m_final_norm, v_attn_norm_0, v_w_in_0, v_w_out_0, v_mlp_norm_0, v_w_up_0, v_w_down_0, v_ple_norm_0, v_w_ple_gate_0, v_w_ple_proj_0, v_attn_norm_1, v_w_in_1, v_w_out_1, v_sinks_1, v_mlp_norm_1, v_w_up_1, v_w_down_1, v_ple_norm_1, v_w_ple_gate_1, v_w_ple_proj_1, v_attn_norm_2, v_w_in_2, v_w_out_2, v_b_forget_2, v_mlp_norm_2, v_w_up_2, v_w_down_2, v_ple_norm_2, v_w_ple_gate_2, v_w_ple_proj_2, v_attn_norm_3, v_w_in_3, v_w_out_3, v_mlp_norm_3, v_w_up_3, v_w_down_3, v_ple_norm_3, v_w_ple_gate_3, v_w_ple_proj_3, v_final_norm):
    out = _train_step(dict(locals()))
    return (out["loss"], out["grad_x"], *[out[kd + n] for kd in KINDS for n in _weight_names()])
```

```python
import jax
import jax.numpy as jnp
from jax import lax
from jax.experimental import pallas as pl
from jax.experimental.pallas import tpu as pltpu

F32 = jnp.float32
BF16 = jnp.bfloat16

D_MODEL = 1024
N_HEADS = 16
HEAD_DIM = 64
SWA_KV_HEADS = 2
SWA_GROUP = 8
SWA_WINDOW = 128
ROPE_THETA = 500000.0
ROPE_DIM = 16
RMS_EPS = 1e-6
NEG_INF = -1e30
ATTN_SCALE = HEAD_DIM ** -0.5
N_CHIPS = 4
N_DEVICES = 8

SLAB = 256
ATT_BLK = 128
ATT_BQ = 512
ATT_BK = 512
ROW_TILE = 256
V7X_VMEM_LIMIT = 56 * 1024 * 1024

ADAM_LR, ADAM_B1, ADAM_B2, ADAM_EPS, ADAM_WD, ADAM_STEP = 0.001, 0.9, 0.999, 1e-08, 0.01, 10


def _cparams(sem=None):
    return pltpu.CompilerParams(dimension_semantics=sem, vmem_limit_bytes=V7X_VMEM_LIMIT)


def _dot(a, b):
    return jnp.dot(a, b, preferred_element_type=F32)


def _dot_nt(a, b):
    return lax.dot_general(a, b, (((1,), (1,)), ((), ())), preferred_element_type=F32)


def _dot_tn(a, b):
    return lax.dot_general(a, b, (((0,), (0,)), ((), ())), preferred_element_type=F32)


def _row_tile(M, K):
    return min(M, 1024) if K >= 1024 else M


def _finish(epi, acc, ex, outs):
    res = epi(acc, *[e[...] for e in ex]) if epi is not None else (acc,)
    for o, r in zip(outs, res):
        o[...] = r.astype(o.dtype)


def mm_nn(a, wg, t, *, name, epi=None, extras=(), out_dtypes=(BF16,)):
    off, K, ns, row = t
    M = a.shape[0]
    sb = off // K
    ne, no = len(extras), len(out_dtypes)
    if row:
        tm = _row_tile(M, K)
        nb = N_CHIPS
        grid = (M // tm, ns)
        a_spec = pl.BlockSpec((tm, N_CHIPS * K), lambda i, q: (i, 0))
        b_specs = [pl.BlockSpec((None, K, SLAB), lambda i, q, j=j: (j, sb + q, 0)) for j in range(nb)]
        tile = pl.BlockSpec((tm, SLAB), lambda i, q: (i, q))
        n_out = ns * SLAB
    else:
        nb = 1
        grid = (N_CHIPS, ns)
        a_spec = pl.BlockSpec((M, K), lambda j, q: (0, 0))
        b_specs = [pl.BlockSpec((None, K, SLAB), lambda j, q: (j, sb + q, 0))]
        tile = pl.BlockSpec((M, SLAB), lambda j, q: (0, j * ns + q))
        n_out = N_CHIPS * ns * SLAB

    def body(a_ref, *rest):
        bs, ex, outs = rest[:nb], rest[nb:nb + ne], rest[nb + ne:]
        acc = _dot(a_ref[:, pl.ds(0, K)], bs[0][...])
        for j in range(1, nb):
            acc = acc + _dot(a_ref[:, pl.ds(j * K, K)], bs[j][...])
        _finish(epi, acc, ex, outs)

    return pl.pallas_call(
        body, name=name, grid=grid,
        in_specs=[a_spec] + b_specs + [tile] * ne, out_specs=[tile] * no,
        out_shape=[jax.ShapeDtypeStruct((M, n_out), d) for d in out_dtypes],
        compiler_params=_cparams(("parallel", "parallel")),
    )(a, *([wg] * nb), *extras)


def mm_nt(dy, wg, t, *, name, epi=None, extras=(), out_dtypes=(BF16,)):
    off, K, ns, row = t
    M = dy.shape[0]
    tm = _row_tile(M, K)
    sb = off // K
    ne, no = len(extras), len(out_dtypes)
    grid = (M // tm, N_CHIPS)
    b_specs = [pl.BlockSpec((None, K, SLAB), lambda i, j, q=q: (j, sb + q, 0)) for q in range(ns)]
    if row:
        dy_spec = pl.BlockSpec((tm, ns * SLAB), lambda i, j: (i, 0))
        tile = pl.BlockSpec((tm, K), lambda i, j: (i, j))
        n_out = N_CHIPS * K
        sem = ("parallel", "parallel")
    else:
        dy_spec = pl.BlockSpec((tm, ns * SLAB), lambda i, j: (i, j))
        tile = pl.BlockSpec((tm, K), lambda i, j: (i, 0))
        n_out = K
        sem = ("parallel", "arbitrary")

    def body(dy_ref, *rest):
        bs, ex, outs = rest[:ns], rest[ns:ns + ne], rest[ns + ne:ns + ne + no]
        part = _dot_nt(dy_ref[:, pl.ds(0, SLAB)], bs[0][...])
        for q in range(1, ns):
            part = part + _dot_nt(dy_ref[:, pl.ds(q * SLAB, SLAB)], bs[q][...])
        if row:
            _finish(epi, part, ex, outs)
        else:
            acc_ref = rest[-1]
            j = pl.program_id(1)

            @pl.when(j == 0)
            def _():
                acc_ref[...] = part

            @pl.when(j > 0)
            def _():
                acc_ref[...] += part

            @pl.when(j == N_CHIPS - 1)
            def _():
                _finish(epi, acc_ref[...], ex, outs)

    return pl.pallas_call(
        body, name=name, grid=grid,
        in_specs=[dy_spec] + b_specs + [tile] * ne, out_specs=[tile] * no,
        out_shape=[jax.ShapeDtypeStruct((M, n_out), d) for d in out_dtypes],
        scratch_shapes=[] if row else [pltpu.VMEM((tm, K), F32)],
        compiler_params=_cparams(sem),
    )(dy, *([wg] * ns), *extras)


def mm_tn(x, dy, g, t, *, name):
    off, K, ns, row = t
    S = x.shape[0]
    sb = off // K
    if row:
        x_map = lambda j, q: (0, j)
        dy_map = lambda j, q: (0, q)
    else:
        x_map = lambda j, q: (0, 0)
        dy_map = lambda j, q: (0, j * ns + q)

    def body(g_in, x_ref, dy_ref, o_ref):
        del g_in
        o_ref[...] = _dot_tn(x_ref[...], dy_ref[...]).astype(o_ref.dtype)

    return pl.pallas_call(
        body, name=name, grid=(N_CHIPS, ns),
        in_specs=[pl.BlockSpec(memory_space=pl.ANY), pl.BlockSpec((S, K), x_map), pl.BlockSpec((S, SLAB), dy_map)],
        out_specs=pl.BlockSpec((None, K, SLAB), lambda j, q: (j, sb + q, 0)),
        out_shape=jax.ShapeDtypeStruct(g.shape, g.dtype),
        input_output_aliases={0: 0},
        compiler_params=_cparams(("parallel", "parallel")),
    )(g, x, dy)


def ew(fn, ins, out_dtypes, *, name, bcast=()):
    S = ins[0].shape[0]
    tr = min(ROW_TILE, S)
    cols = ins[0].shape[1]
    ni, nb = len(ins), len(bcast)

    def body(*refs):
        res = fn(*[r[...] for r in refs[:ni + nb]])
        for o, r in zip(refs[ni + nb:], res):
            o[...] = r.astype(o.dtype)

    return pl.pallas_call(
        body, name=name, grid=(S // tr,),
        in_specs=[pl.BlockSpec((tr, a.shape[1]), lambda i: (i, 0)) for a in ins]
        + [pl.BlockSpec(b.shape, lambda i: (0, 0)) for b in bcast],
        out_specs=[pl.BlockSpec((tr, cols), lambda i: (i, 0)) for _ in out_dtypes],
        out_shape=[jax.ShapeDtypeStruct((S, cols), d) for d in out_dtypes],
        compiler_params=_cparams(("parallel",)),
    )(*ins, *bcast)


def _rstd(x):
    return lax.rsqrt(jnp.mean(x * x, axis=-1, keepdims=True) + RMS_EPS)


def _sigmoid(x):
    return 1.0 / (1.0 + jnp.exp(-x))


def _log_sigmoid(z):
    return jnp.minimum(z, 0.0) - jnp.log(1.0 + jnp.exp(-jnp.abs(z)))


def rms_fwd(x, g, *, name):
    return ew(lambda xv, gv: (xv * _rstd(xv) * gv,), [x], [BF16], name=name, bcast=[g])[0]


def _rms_bwd_tile(xv, gv, dh):
    rstd = _rstd(xv)
    xhat = xv * rstd
    gd = dh * gv
    dx = rstd * (gd - xhat * jnp.mean(xhat * gd, axis=-1, keepdims=True))
    return dx, jnp.sum(dh * xhat, axis=0, keepdims=True)


def rms_bwd(x, g, dh, dres, *, name):
    S, D = x.shape
    tr = min(ROW_TILE, S)

    def body(x_ref, g_ref, dh_ref, dres_ref, dx_ref, dxb_ref, dg_ref):
        i = pl.program_id(0)
        dx, dg = _rms_bwd_tile(x_ref[...], g_ref[...], dh_ref[...])
        dx = dx + dres_ref[...]
        dx_ref[...] = dx
        dxb_ref[...] = dx.astype(BF16)

        @pl.when(i == 0)
        def _():
            dg_ref[...] = dg

        @pl.when(i > 0)
        def _():
            dg_ref[...] += dg

    row = pl.BlockSpec((tr, D), lambda i: (i, 0))
    one = pl.BlockSpec((1, D), lambda i: (0, 0))
    return pl.pallas_call(
        body, name=name, grid=(S // tr,),
        in_specs=[row, one, row, row], out_specs=[row, row, one],
        out_shape=[jax.ShapeDtypeStruct((S, D), F32), jax.ShapeDtypeStruct((S, D), BF16),
                   jax.ShapeDtypeStruct((1, D), F32)],
        compiler_params=_cparams(("arbitrary",)),
    )(x, g, dh, dres)


def loss_head(x, g, target, *, name):
    S, D = x.shape
    tr = min(ROW_TILE, S)

    def body(x_ref, g_ref, t_ref, dx_ref, dg_ref, loss_ref):
        i = pl.program_id(0)
        xv, gv = x_ref[...], g_ref[...]
        err = xv * _rstd(xv) * gv - t_ref[...]
        part = 0.5 * jnp.sum(jnp.mean(err * err, axis=-1, keepdims=True), axis=0, keepdims=True)
        dx, dg = _rms_bwd_tile(xv, gv, err * (1.0 / D))
        dx_ref[...] = dx
        part = jnp.broadcast_to(part, loss_ref.shape)

        @pl.when(i == 0)
        def _():
            dg_ref[...] = dg
            loss_ref[...] = part

        @pl.when(i > 0)
        def _():
            dg_ref[...] += dg
            loss_ref[...] += part

    row = pl.BlockSpec((tr, D), lambda i: (i, 0))
    one = pl.BlockSpec((1, D), lambda i: (0, 0))
    return pl.pallas_call(
        body, name=name, grid=(S // tr,),
        in_specs=[row, one, row], out_specs=[row, one, pl.BlockSpec((1, 128), lambda i: (0, 0))],
        out_shape=[jax.ShapeDtypeStruct((S, D), F32), jax.ShapeDtypeStruct((1, D), F32),
                   jax.ShapeDtypeStruct((1, 128), F32)],
        compiler_params=_cparams(("arbitrary",)),
    )(x, g, target)


def rope_tables(S, n_cols):
    half = ROPE_DIM // 2
    inv_freq = ROPE_THETA ** (-jnp.arange(half, dtype=F32) / half)
    ang = jnp.arange(S, dtype=F32)[:, None] * inv_freq[None, :]
    cos, sin = jnp.cos(ang), jnp.sin(ang)
    z = jnp.zeros((S, HEAD_DIM - ROPE_DIM), F32)
    zh = jnp.zeros((S, half), F32)
    c = jnp.concatenate([cos, cos, jnp.ones_like(z)], axis=1)
    sa = jnp.concatenate([zh, sin, z], axis=1)
    sb = jnp.concatenate([-sin, zh, z], axis=1)
    return [jnp.tile(t, (1, n_cols // HEAD_DIM)) for t in (c, sa, sb)]


def rope_fwd(xqk, tables, *, name):
    n, half = xqk.shape[1], ROPE_DIM // 2

    def fn(x, c, sa, sb):
        return (x * c + pltpu.roll(x, half, 1) * sa + pltpu.roll(x, n - half, 1) * sb,)

    return ew(fn, [xqk] + list(tables), [BF16], name=name)[0]


def rope_bwd(dy, tables, *, name):
    n, half = dy.shape[1], ROPE_DIM // 2

    def fn(d, c, sa, sb):
        return (d * c + pltpu.roll(d * sa, n - half, 1) + pltpu.roll(d * sb, half, 1),)

    return ew(fn, [dy] + list(tables), [BF16], name=name)[0]


def _split3(x):
    h1 = x.astype(BF16)
    r1 = x - h1.astype(F32)
    h2 = r1.astype(BF16)
    return h1, h2, (r1 - h2.astype(F32)).astype(BF16)


def _split2(x):
    h1 = x.astype(BF16)
    return h1, (x - h1.astype(F32)).astype(BF16)


def _tri(n, cmp):
    r = lax.broadcasted_iota(jnp.int32, (n, n), 0)
    c = lax.broadcasted_iota(jnp.int32, (n, n), 1)
    return cmp(r, c).astype(BF16)


def fox_gate_fwd(fl, b, *, name):
    S, W = fl.shape
    tr = min(ROW_TILE, S)

    def body(fl_ref, b_ref, cum_ref, carry):
        i = pl.program_id(0)

        @pl.when(i == 0)
        def _():
            carry[...] = jnp.zeros_like(carry)

        lower = _tri(tr, lambda r, c: r >= c)
        cs = carry[...]
        for piece in _split3(_log_sigmoid(fl_ref[...] + b_ref[...])):
            cs = cs + _dot(lower, piece)
        cum_ref[...] = cs
        carry[...] = cs[tr - 1:tr, :]

    return pl.pallas_call(
        body, name=name, grid=(S // tr,),
        in_specs=[pl.BlockSpec((tr, W), lambda i: (i, 0)), pl.BlockSpec((1, W), lambda i: (0, 0))],
        out_specs=pl.BlockSpec((tr, W), lambda i: (i, 0)),
        out_shape=jax.ShapeDtypeStruct((S, W), F32),
        scratch_shapes=[pltpu.VMEM((1, W), F32)],
        compiler_params=_cparams(("arbitrary",)),
    )(fl, b)


def fox_gate_bwd(dcum, fl, b, *, name):
    S, W = fl.shape
    tr = min(ROW_TILE, S)
    nb = S // tr

    def body(dc_ref, fl_ref, b_ref, dfl_ref, db_ref, carry):
        i = pl.program_id(0)

        @pl.when(i == 0)
        def _():
            carry[...] = jnp.zeros_like(carry)

        upper = _tri(tr, lambda r, c: r <= c)
        cs = carry[...]
        for piece in _split3(dc_ref[...]):
            cs = cs + _dot(upper, piece)
        carry[...] = cs[0:1, :]
        dfl = cs * _sigmoid(-(fl_ref[...] + b_ref[...]))
        dfl_ref[...] = dfl
        db = jnp.sum(dfl, axis=0, keepdims=True)

        @pl.when(i == 0)
        def _():
            db_ref[...] = db

        @pl.when(i > 0)
        def _():
            db_ref[...] += db

    rev = pl.BlockSpec((tr, W), lambda i: (nb - 1 - i, 0))
    one = pl.BlockSpec((1, W), lambda i: (0, 0))
    return pl.pallas_call(
        body, name=name, grid=(nb,),
        in_specs=[rev, rev, one], out_specs=[rev, one],
        out_shape=[jax.ShapeDtypeStruct((S, W), F32), jax.ShapeDtypeStruct((1, W), F32)],
        scratch_shapes=[pltpu.VMEM((1, W), F32)],
        compiler_params=_cparams(("arbitrary",)),
    )(dcum, fl, b)


def _blk_iota(tq, tk):
    return (lax.broadcasted_iota(jnp.int32, (tq, tk), 0), lax.broadcasted_iota(jnp.int32, (tq, tk), 1))


def _cs(xb, tri):
    return _dot(xb, tri)


def _rowsum(xb):
    return jnp.sum(xb.astype(F32), axis=1, keepdims=True)


def _sb_block(qs, k, cmr, shift):
    z = _dot_nt(qs, k)
    strict = cmr < shift
    lb = jnp.minimum(z, 0.0) - jnp.log(1.0 + jnp.exp(-jnp.abs(z)))
    lom = jnp.where(strict, lb - z, 0.0).astype(BF16)
    return lb, lom, strict


def _att_tiles(S):
    return min(ATT_BQ, S), min(ATT_BK, S)


def _head_specs(S, tq):
    qspec = pl.BlockSpec((None, tq, HEAD_DIM), lambda h, i: (h, i, 0))
    kvspec = pl.BlockSpec((None, S, HEAD_DIM), lambda h, i: (h, 0, 0))
    vec = pl.BlockSpec((None, tq, 1), lambda h, i: (h, i, 0))
    return qspec, kvspec, vec


def sb_fwd(q, k, v, *, name):
    H, S, _ = q.shape
    tq, tk = _att_tiles(S)
    qspec, kvspec, vec = _head_specs(S, tq)

    def body(q_ref, k_ref, v_ref, o_ref, t_ref):
        i = pl.program_id(1)
        qs = q_ref[...] * ATTN_SCALE
        row, col = _blk_iota(tq, tk)
        cmr = col - row
        below = _tri(tk, lambda r, c: r > c)
        nkb = (i + 1) * (tq // tk)

        def step(n, carry):
            r_sum, acc = carry
            kb = nkb - 1 - n
            ks = pl.multiple_of(kb * tk, tk)
            lb, lom, strict = _sb_block(qs, k_ref[pl.ds(ks, tk), :], cmr, i * tq - kb * tk)
            w = jnp.where(strict, jnp.exp(lb + _cs(lom, below) + r_sum), 0.0)
            acc = acc + _dot(w.astype(BF16), v_ref[pl.ds(ks, tk), :])
            return r_sum + _rowsum(lom), acc

        r_sum, acc = lax.fori_loop(0, nkb, step, (jnp.zeros((tq, 1), F32), jnp.zeros((tq, HEAD_DIM), F32)))
        o_ref[...] = acc.astype(o_ref.dtype)
        t_ref[...] = r_sum

    return pl.pallas_call(
        body, name=name, grid=(H, S // tq),
        in_specs=[qspec, kvspec, kvspec], out_specs=[qspec, vec],
        out_shape=[jax.ShapeDtypeStruct((H, S, HEAD_DIM), BF16), jax.ShapeDtypeStruct((H, S, 1), F32)],
        compiler_params=_cparams(("parallel", "arbitrary")),
    )(q, k, v)


def sb_bwd(q, k, v, tot, do, *, name):
    H, S, _ = q.shape
    tq, tk = _att_tiles(S)
    qspec, kvspec, vec = _head_specs(S, tq)

    def body(q_ref, k_ref, v_ref, t_ref, do_ref, dq_ref, dk_ref, dv_ref):
        i = pl.program_id(1)

        @pl.when(i == 0)
        def _():
            dk_ref[...] = jnp.zeros_like(dk_ref)
            dv_ref[...] = jnp.zeros_like(dv_ref)

        qs, dov, t_all = q_ref[...] * ATTN_SCALE, do_ref[...], t_ref[...]
        row, col = _blk_iota(tq, tk)
        cmr = col - row
        upto = _tri(tk, lambda r, c: r <= c)
        before = _tri(tk, lambda r, c: r < c)

        def step(kb, carry):
            p_sum, e_sum, dq = carry
            ks = pl.multiple_of(kb * tk, tk)
            kv = k_ref[pl.ds(ks, tk), :]
            vv = v_ref[pl.ds(ks, tk), :]
            lb, lom, strict = _sb_block(qs, kv, cmr, i * tq - kb * tk)
            tail = t_all - p_sum - _cs(lom, upto)
            w = jnp.where(strict, jnp.exp(lb + tail), 0.0)
            e = _dot_nt(dov, vv) * w
            eb = e.astype(BF16)
            e_before = e_sum + _cs(eb, before)
            beta = jnp.exp(lb)
            dzb = jnp.where(strict, e - (e + e_before) * beta, 0.0).astype(BF16)
            dk_ref[pl.ds(ks, tk), :] += _dot_tn(dzb, qs)
            dv_ref[pl.ds(ks, tk), :] += _dot_tn(w.astype(BF16), dov)
            return (p_sum + _rowsum(lom), e_sum + _rowsum(eb),
                    dq + _dot(dzb, kv))

        zero = jnp.zeros((tq, 1), F32)
        _, _, dq = lax.fori_loop(0, (i + 1) * (tq // tk), step, (zero, zero, jnp.zeros((tq, HEAD_DIM), F32)))
        dq_ref[...] = dq * ATTN_SCALE

    full = jax.ShapeDtypeStruct((H, S, HEAD_DIM), F32)
    return pl.pallas_call(
        body, name=name, grid=(H, S // tq),
        in_specs=[qspec, kvspec, kvspec, vec, qspec], out_specs=[qspec, kvspec, kvspec],
        out_shape=[full, full, full],
        compiler_params=_cparams(("parallel", "arbitrary")),
    )(q, k, v, tot, do)


def _fox_logits(qs, k, cq, ck, cmr, shift):
    causal = cmr <= shift
    return jnp.where(causal, _dot_nt(qs, k) + cq - ck, NEG_INF), causal


def fox_fwd(q, k, v, cq, ck, *, name):
    H, S, _ = q.shape
    tq, tk = _att_tiles(S)
    qspec, kvspec, vec = _head_specs(S, tq)
    ckspec = pl.BlockSpec((None, S // tk, 1, tk), lambda h, i: (h, 0, 0, 0))

    def body(q_ref, k_ref, v_ref, cq_ref, ck_ref, o_ref, lse_ref):
        i = pl.program_id(1)
        qs, cqv = q_ref[...] * ATTN_SCALE, cq_ref[...]
        row, col = _blk_iota(tq, tk)
        cmr = col - row

        def step(kb, carry):
            m, l, acc = carry
            ks = pl.multiple_of(kb * tk, tk)
            s, _ = _fox_logits(qs, k_ref[pl.ds(ks, tk), :], cqv, ck_ref[kb], cmr, i * tq - kb * tk)
            m_new = jnp.maximum(m, jnp.max(s, axis=1, keepdims=True))
            alpha = jnp.exp(m - m_new)
            p = jnp.exp(s - m_new)
            l = alpha * l + jnp.sum(p, axis=1, keepdims=True)
            acc = alpha * acc + _dot(p.astype(BF16), v_ref[pl.ds(ks, tk), :])
            return m_new, l, acc

        m, l, acc = lax.fori_loop(0, (i + 1) * (tq // tk), step,
                                  (jnp.full((tq, 1), NEG_INF, F32), jnp.zeros((tq, 1), F32), jnp.zeros((tq, HEAD_DIM), F32)))
        o_ref[...] = (acc / l).astype(o_ref.dtype)
        lse_ref[...] = m + jnp.log(l)

    return pl.pallas_call(
        body, name=name, grid=(H, S // tq),
        in_specs=[qspec, kvspec, kvspec, vec, ckspec], out_specs=[qspec, vec],
        out_shape=[jax.ShapeDtypeStruct((H, S, HEAD_DIM), BF16), jax.ShapeDtypeStruct((H, S, 1), F32)],
        compiler_params=_cparams(("parallel", "arbitrary")),
    )(q, k, v, cq, ck)


def fox_bwd(q, k, v, o, lse, cq, ck, do, *, name):
    H, S, _ = q.shape
    tq, tk = _att_tiles(S)
    qspec, kvspec, vec = _head_specs(S, tq)
    ckspec = pl.BlockSpec((None, S // tk, 1, tk), lambda h, i: (h, 0, 0, 0))

    def body(q_ref, k_ref, v_ref, o_ref, lse_ref, cq_ref, ck_ref, do_ref, dq_ref, dk_ref, dv_ref, dcq_ref, dck_ref):
        i = pl.program_id(1)

        @pl.when(i == 0)
        def _():
            dk_ref[...] = jnp.zeros_like(dk_ref)
            dv_ref[...] = jnp.zeros_like(dv_ref)
            dck_ref[...] = jnp.zeros_like(dck_ref)

        qs, dov, cqv, lsev = q_ref[...] * ATTN_SCALE, do_ref[...], cq_ref[...], lse_ref[...]
        delta = jnp.sum(dov.astype(F32) * o_ref[...].astype(F32), axis=1, keepdims=True)
        row, col = _blk_iota(tq, tk)
        cmr = col - row

        def step(kb, carry):
            dq, dcq = carry
            ks = pl.multiple_of(kb * tk, tk)
            kv = k_ref[pl.ds(ks, tk), :]
            vv = v_ref[pl.ds(ks, tk), :]
            s, causal = _fox_logits(qs, kv, cqv, ck_ref[kb], cmr, i * tq - kb * tk)
            p = jnp.where(causal, jnp.exp(s - lsev), 0.0)
            ds = p * (_dot_nt(dov, vv) - delta)
            dck_ref[kb] += jnp.sum(ds, axis=0, keepdims=True)
            dsb = ds.astype(BF16)
            dk_ref[pl.ds(ks, tk), :] += _dot_tn(dsb, qs)
            dv_ref[pl.ds(ks, tk), :] += _dot_tn(p.astype(BF16), dov)
            return dq + _dot(dsb, kv), dcq + jnp.sum(ds, axis=1, keepdims=True)

        dq, dcq = lax.fori_loop(0, (i + 1) * (tq // tk), step, (jnp.zeros((tq, HEAD_DIM), F32), jnp.zeros((tq, 1), F32)))
        dq_ref[...] = dq * ATTN_SCALE
        dcq_ref[...] = dcq

    full = jax.ShapeDtypeStruct((H, S, HEAD_DIM), F32)
    return pl.pallas_call(
        body, name=name, grid=(H, S // tq),
        in_specs=[qspec, kvspec, kvspec, qspec, vec, vec, ckspec, qspec],
        out_specs=[qspec, kvspec, kvspec, vec, ckspec],
        out_shape=[full, full, full, jax.ShapeDtypeStruct((H, S, 1), F32),
                   jax.ShapeDtypeStruct((H, S // tk, 1, tk), F32)],
        compiler_params=_cparams(("parallel", "arbitrary")),
    )(q, k, v, o, lse, cq, ck, do)


def _swa_specs(S, tq):
    qspec = pl.BlockSpec((None, SWA_GROUP, tq, HEAD_DIM), lambda g, i: (g, 0, i, 0))
    kvspec = pl.BlockSpec((None, S + SWA_WINDOW, HEAD_DIM), lambda g, i: (g, 0, 0))
    vec = pl.BlockSpec((None, SWA_GROUP, tq, 1), lambda g, i: (g, 0, i, 0))
    sink = pl.BlockSpec((None, SWA_GROUP * tq, 1), lambda g, i: (g, 0, 0))
    return qspec, kvspec, vec, sink


def _swa_logits(q2, kw, i, tq):
    rows = q2.shape[0]
    r = lax.broadcasted_iota(jnp.int32, (rows, 2 * tq), 0)
    c = lax.broadcasted_iota(jnp.int32, (rows, 2 * tq), 1)
    diff = (r & (tq - 1)) + tq - c
    ok = (diff >= 0) & (diff < SWA_WINDOW) & (c + (i - 1) * tq >= 0)
    return jnp.where(ok, _dot_nt(q2, kw) * ATTN_SCALE, NEG_INF), ok


def swa_fwd(q, kp, vp, sink, *, name):
    _, G, S, _ = q.shape
    tq = ATT_BLK
    qspec, kvspec, vec, sinkspec = _swa_specs(S, tq)

    def body(q_ref, k_ref, v_ref, s_ref, o_ref, lse_ref):
        i = pl.program_id(1)
        q2 = q_ref[...].reshape(G * tq, HEAD_DIM)
        ws = pl.multiple_of(i * tq, tq)
        logits, _ = _swa_logits(q2, k_ref[pl.ds(ws, 2 * tq), :], i, tq)
        sk = s_ref[...]
        m = jnp.maximum(jnp.max(logits, axis=1, keepdims=True), sk)
        e = jnp.exp(logits - m)
        den = jnp.sum(e, axis=1, keepdims=True) + jnp.exp(sk - m)
        o = _dot((e / den).astype(BF16), v_ref[pl.ds(ws, 2 * tq), :])
        o_ref[...] = o.reshape(G, tq, HEAD_DIM).astype(o_ref.dtype)
        lse_ref[...] = (m + jnp.log(den)).reshape(G, tq, 1)

    return pl.pallas_call(
        body, name=name, grid=(SWA_KV_HEADS, S // tq),
        in_specs=[qspec, kvspec, kvspec, sinkspec], out_specs=[qspec, vec],
        out_shape=[jax.ShapeDtypeStruct(q.shape, BF16), jax.ShapeDtypeStruct((SWA_KV_HEADS, G, S, 1), F32)],
        compiler_params=_cparams(("parallel", "arbitrary")),
    )(q, kp, vp, sink)


def swa_bwd(q, kp, vp, sink, o, lse, do, *, name):
    _, G, S, _ = q.shape
    tq = ATT_BLK
    qspec, kvspec, vec, sinkspec = _swa_specs(S, tq)

    def body(q_ref, k_ref, v_ref, s_ref, o_ref, lse_ref, do_ref, dq_ref, dk_ref, dv_ref, dsink_ref):
        i = pl.program_id(1)

        @pl.when(i == 0)
        def _():
            dk_ref[...] = jnp.zeros_like(dk_ref)
            dv_ref[...] = jnp.zeros_like(dv_ref)

        q2 = q_ref[...].reshape(G * tq, HEAD_DIM)
        do2 = do_ref[...].reshape(G * tq, HEAD_DIM)
        o2 = o_ref[...].reshape(G * tq, HEAD_DIM)
        lse2 = lse_ref[...].reshape(G * tq, 1)
        ws = pl.multiple_of(i * tq, tq)
        kw = k_ref[pl.ds(ws, 2 * tq), :]
        vw = v_ref[pl.ds(ws, 2 * tq), :]
        logits, ok = _swa_logits(q2, kw, i, tq)
        p = jnp.where(ok, jnp.exp(logits - lse2), 0.0)
        delta = jnp.sum(do2.astype(F32) * o2.astype(F32), axis=1, keepdims=True)
        ds = p * (_dot_nt(do2, vw) - delta)
        dsb = ds.astype(BF16)
        dq_ref[...] = (_dot(dsb, kw) * ATTN_SCALE).reshape(G, tq, HEAD_DIM)
        dk_ref[pl.ds(ws, 2 * tq), :] += _dot_tn(dsb, q2) * ATTN_SCALE
        dv_ref[pl.ds(ws, 2 * tq), :] += _dot_tn(p.astype(BF16), do2)
        dsink_ref[...] = (-jnp.exp(s_ref[...] - lse2) * delta).reshape(G, tq, 1)

    kvshape = jax.ShapeDtypeStruct(kp.shape, F32)
    return pl.pallas_call(
        body, name=name, grid=(SWA_KV_HEADS, S // tq),
        in_specs=[qspec, kvspec, kvspec, sinkspec, qspec, vec, qspec],
        out_specs=[qspec, kvspec, kvspec, vec],
        out_shape=[jax.ShapeDtypeStruct(q.shape, F32), kvshape, kvshape,
                   jax.ShapeDtypeStruct((SWA_KV_HEADS, G, S, 1), F32)],
        compiler_params=_cparams(("parallel", "arbitrary")),
    )(q, kp, vp, sink, o, lse, do)


def _adamw_tile(w, g, m, v):
    m = ADAM_B1 * m + (1.0 - ADAM_B1) * g
    v = ADAM_B2 * v + (1.0 - ADAM_B2) * (g * g)
    m_hat = m / (1.0 - ADAM_B1 ** ADAM_STEP)
    v_hat = v / (1.0 - ADAM_B2 ** ADAM_STEP)
    delta = -ADAM_LR * (m_hat / (jnp.sqrt(v_hat) + ADAM_EPS) + ADAM_WD * w)
    return g, delta, m, v


def adamw(gfull, t, w, m, v, *, name):
    off, K, ns, _ = t
    sb = off // K
    nat = pl.BlockSpec((K, SLAB), lambda q: (0, q))

    def body(g_ref, w_ref, m_ref, v_ref, *outs):
        for o, r in zip(outs, _adamw_tile(w_ref[...], g_ref[...], m_ref[...], v_ref[...])):
            o[...] = r

    return pl.pallas_call(
        body, name=name, grid=(ns,),
        in_specs=[pl.BlockSpec((K, SLAB), lambda q: (sb + q, 0)), nat, nat, nat],
        out_specs=[nat] * 4, out_shape=[jax.ShapeDtypeStruct(w.shape, F32)] * 4,
        compiler_params=_cparams(("parallel",)),
    )(gfull, w, m, v)


def adamw_small(g, w, m, v, *, name):
    def body(g_ref, w_ref, m_ref, v_ref, *outs):
        for o, r in zip(outs, _adamw_tile(w_ref[...], g_ref[...], m_ref[...], v_ref[...])):
            o[...] = r

    return pl.pallas_call(body, name=name, out_shape=[jax.ShapeDtypeStruct(w.shape, F32)] * 4)(g, w, m, v)


MESH = pl.DeviceIdType.MESH
HBM = pl.BlockSpec(memory_space=pl.ANY)


def _place():
    x, y, c = lax.axis_index("x"), lax.axis_index("y"), lax.axis_index("c")
    others = [(1 - x, y), (x, 1 - y), (1 - x, 1 - y)]
    return x, y, c, others


def _rcopy(src, dst, send_sems, recv_sems, k, to):
    return pltpu.make_async_remote_copy(src_ref=src, dst_ref=dst, send_sem=send_sems.at[k], recv_sem=recv_sems.at[k],
                                        device_id=to, device_id_type=MESH)


def _dma_sems(*counts):
    return [pltpu.SemaphoreType.DMA((n,)) for n in counts]


DMA_UNIT_ROWS = 128
DMA_PIECES = 4
DMA_PIECES_LOCAL = 8


def _row_pieces(h, n):
    units = h // DMA_UNIT_ROWS
    n = min(n, units)
    base, extra = divmod(units, n)
    sizes = [(base + (k < extra)) * DMA_UNIT_ROWS for k in range(n)]
    return [(sum(sizes[:k]), sizes[k]) for k in range(n)]


def _start_pieces(make, h, n):
    for s0, sz in _row_pieces(h, n):
        make(s0, sz).start()
    return make(0, h)


def allgather_weights(bufs, *, name):
    n = len(bufs)

    def body(*refs):
        outs = refs[n:2 * n]
        send_sems, recv_sems = refs[2 * n:]
        x, y, c, others = _place()
        sibling = (x, y, 1 - c)

        def rows(i, chip, s0, sz):
            return outs[i].at[2 * chip[0] + chip[1], pl.ds(s0, sz)]

        first, passed = [], []
        for i in range(n):
            h = bufs[i].shape[1] // 2
            for f, chip in enumerate(others):
                first.append(_start_pieces(
                    lambda s0, sz: _rcopy(rows(i, (x, y), c * h + s0, sz), rows(i, (x, y), c * h + s0, sz),
                                          send_sems, recv_sems, 6 * i + f, (*chip, c)), h, DMA_PIECES))
        for i in range(n):
            h = bufs[i].shape[1] // 2
            for f, chip in enumerate(others):
                blk = rows(i, chip, c * h, h)
                _rcopy(blk, blk, send_sems, recv_sems, 6 * i + f, (*chip, c)).wait_recv()
                passed.append(_start_pieces(
                    lambda s0, sz: _rcopy(rows(i, chip, c * h + s0, sz), rows(i, chip, c * h + s0, sz),
                                          send_sems, recv_sems, 6 * i + 3 + f, sibling), h, DMA_PIECES))
        for i in range(n):
            h = bufs[i].shape[1] // 2
            for f, chip in enumerate(others):
                blk = rows(i, chip, (1 - c) * h, h)
                _rcopy(blk, blk, send_sems, recv_sems, 6 * i + 3 + f, sibling).wait_recv()
        for cp in first + passed:
            cp.wait_send()

    return pl.pallas_call(
        body, name=name, in_specs=[HBM] * n, out_specs=[HBM] * n,
        out_shape=[jax.ShapeDtypeStruct(b.shape, b.dtype) for b in bufs],
        input_output_aliases={i: i for i in range(n)},
        scratch_shapes=_dma_sems(6 * n, 6 * n),
    )(*bufs)


def swap_halves(grads, *, name):
    n = len(grads)

    def body(*refs):
        ins, theirs = refs[:n], refs[n:2 * n]
        send_sems, recv_sems = refs[2 * n:]
        x, y, c, _ = _place()
        for i in range(n):
            h = grads[i].shape[1] // 2
            for k in range(N_CHIPS):
                for s0, sz in _row_pieces(h, DMA_PIECES):
                    _rcopy(ins[i].at[k, pl.ds((1 - c) * h + s0, sz)], theirs[i].at[k, pl.ds(s0, sz)],
                           send_sems, recv_sems, i, (x, y, 1 - c)).start()
        for i in range(n):
            h = grads[i].shape[1] // 2
            _rcopy(ins[i].at[:, pl.ds((1 - c) * h, h)], theirs[i], send_sems, recv_sems, i, (x, y, 1 - c)).wait()

    return pl.pallas_call(
        body, name=name, in_specs=[HBM] * n, out_specs=[HBM] * n,
        out_shape=[jax.ShapeDtypeStruct((N_CHIPS, g.shape[1] // 2, SLAB), g.dtype) for g in grads],
        scratch_shapes=_dma_sems(n, n))(*grads)


def scatter_chips(parts, *, name):
    n = len(parts)

    def body(*refs):
        ins, got = refs[:n], refs[n:2 * n]
        send_sems, recv_sems = refs[2 * n:]
        x, y, c, others = _place()
        cps = []
        for i in range(n):
            for f, (px, py) in enumerate(others):
                cps.append(_start_pieces(
                    lambda s0, sz: _rcopy(ins[i].at[2 * px + py, pl.ds(s0, sz)], got[i].at[f, pl.ds(s0, sz)],
                                          send_sems, recv_sems, 3 * i + f, (px, py, c)), parts[i].shape[1], DMA_PIECES))
        for cp in cps:
            cp.wait()

    return pl.pallas_call(
        body, name=name, in_specs=[HBM] * n, out_specs=[HBM] * n,
        out_shape=[jax.ShapeDtypeStruct((3,) + p.shape[1:], p.dtype) for p in parts],
        scratch_shapes=_dma_sems(3 * n, 3 * n),
    )(*parts)


def join_halves(bufs, *, name):
    n = len(bufs)

    def body(*refs):
        outs = refs[n:2 * n]
        send_sems, recv_sems = refs[2 * n:]
        x, y, c, _ = _place()
        sibling = (x, y, 1 - c)
        cps = []
        for i in range(n):
            h = bufs[i].shape[0] // 2
            snd = _start_pieces(
                lambda s0, sz: _rcopy(outs[i].at[pl.ds(c * h + s0, sz)], outs[i].at[pl.ds(c * h + s0, sz)],
                                      send_sems, recv_sems, i, sibling), h, 2 * DMA_PIECES_LOCAL)
            theirs = outs[i].at[pl.ds((1 - c) * h, h)]
            cps.append((snd, _rcopy(theirs, theirs, send_sems, recv_sems, i, sibling)))
        for snd, rcv in cps:
            snd.wait_send()
            rcv.wait_recv()

    return pl.pallas_call(
        body, name=name, in_specs=[HBM] * n, out_specs=[HBM] * n,
        out_shape=[jax.ShapeDtypeStruct(b.shape, b.dtype) for b in bufs],
        input_output_aliases={i: i for i in range(n)},
        scratch_shapes=_dma_sems(n, n),
    )(*bufs)


def allreduce_small(v, *, name):
    rows, n = v.shape

    def body(x_ref, sum_ref, all_ref, send_sems, recv_sems, local_sem):
        x, y, c, others = _place()
        me, sibling = (x, y, c), (x, y, 1 - c)

        def blk(px, py, pc):
            return all_ref.at[pl.ds((4 * px + 2 * py + pc) * rows, rows), :]

        def copy(k, block, to, src=None):
            return _rcopy(blk(*block) if src is None else src, blk(*block), send_sems, recv_sems, k, to)

        mine = pltpu.make_async_copy(x_ref, blk(*me), local_sem)
        mine.start()
        first = [copy(0, me, sibling, src=x_ref)]
        first += [copy(1 + f, me, (*chip, c), src=x_ref) for f, chip in enumerate(others)]
        for cp in first:
            cp.start()
        passed = [copy(4 + f, (*chip, c), sibling) for f, chip in enumerate(others)]
        for f, chip in enumerate(others):
            copy(1 + f, (*chip, c), me).wait_recv()
            passed[f].start()
        copy(0, sibling, me).wait_recv()
        for f, chip in enumerate(others):
            copy(4 + f, (*chip, 1 - c), me).wait_recv()
        for cp in first + passed:
            cp.wait_send()
        mine.wait()
        acc = all_ref[pl.ds(0, rows), :]
        for d in range(1, N_DEVICES):
            acc = acc + all_ref[pl.ds(d * rows, rows), :]
        sum_ref[...] = acc

    vm = pl.BlockSpec(memory_space=pltpu.VMEM)
    return pl.pallas_call(
        body, name=name, in_specs=[vm], out_specs=[vm, vm],
        out_shape=[jax.ShapeDtypeStruct((rows, n), F32), jax.ShapeDtypeStruct((N_DEVICES * rows, n), F32)],
        scratch_shapes=_dma_sems(7, 7) + [pltpu.SemaphoreType.DMA],
    )(v)[0]


def add_pairs(grad, theirs, where, *, name):
    h = theirs.shape[1]
    spec = pl.BlockSpec((None, h, SLAB), lambda k, w: (k, 0, 0))

    def body(w_ref, a_ref, b_ref, o_ref):
        del w_ref
        o_ref[...] = (a_ref[...].astype(F32) + b_ref[...].astype(F32)).astype(o_ref.dtype)

    return pl.pallas_call(
        body, name=name,
        grid_spec=pltpu.PrefetchScalarGridSpec(
            num_scalar_prefetch=1, grid=(N_CHIPS,),
            in_specs=[pl.BlockSpec((None, h, SLAB), lambda k, w: (k, w[1], 0)), spec], out_specs=spec),
        out_shape=jax.ShapeDtypeStruct(theirs.shape, theirs.dtype),
        compiler_params=_cparams(("parallel",)))(where, grad, theirs)


def add_chips(pair, got, where, *, name):
    h = pair.shape[1]
    tr = h // 2

    def body(w_ref, a_ref, b_ref, o_ref):
        del w_ref
        acc = a_ref[...].astype(F32)
        for f in range(3):
            acc = acc + b_ref[f].astype(F32)
        o_ref[...] = acc

    return pl.pallas_call(
        body, name=name,
        grid_spec=pltpu.PrefetchScalarGridSpec(
            num_scalar_prefetch=1, grid=(2,),
            in_specs=[pl.BlockSpec((None, tr, SLAB), lambda i, w: (w[0], i, 0)),
                      pl.BlockSpec((3, tr, SLAB), lambda i, w: (0, i, 0))],
            out_specs=pl.BlockSpec((tr, SLAB), lambda i, w: (2 * w[1] + i, 0))),
        out_shape=jax.ShapeDtypeStruct((2 * h, SLAB), F32),
        compiler_params=_cparams(("parallel",)))(where, pair, got)


DEPTH = 4
MIXER = (0, 1, 2, 0)
W_IN_COLS = (768, 320, 772)
W_IN_PAD = (768, 512, 1024)
MATS = ("up", "down", "inp", "out", "gate", "proj")
MAT_ARG = dict(up="w_up", down="w_down", inp="w_in", out="w_out", gate="w_ple_gate", proj="w_ple_proj")
GAINS = ("attn_norm", "mlp_norm", "ple_norm")
N_SMALL = 16
KINDS = ("grad_", "delta_", "new_m_", "new_v_")


def _layout(kind):
    ns_in = W_IN_PAD[kind] // SLAB
    off = 8192 + 1024 * ns_in
    lay = dict(up=(0, 1024, 4, False), down=(4096, 1024, 4, True), inp=(8192, 1024, ns_in, False),
               out=(off, 256, 4, True), gate=(off + 1024, 256, 4, True), proj=(off + 2048, 256, 1, False))
    return lay, off + 2304


def _to_slabs(w):
    k, c = w.shape
    return w.reshape(k, c // SLAB, SLAB).transpose(1, 0, 2).reshape(-1, SLAB)


def _pad_cols(w, n):
    return jnp.pad(w, ((0, 0), (0, n - w.shape[1])))


def _heads(x2d, n):
    return x2d.reshape(x2d.shape[0], n, HEAD_DIM).transpose(1, 0, 2)


def _unheads(x3d):
    n, s, _ = x3d.shape
    return x3d.transpose(1, 0, 2).reshape(s, n * HEAD_DIM)


def _chip_cols(x2d, c, cpad):
    s = x2d.shape[0]
    return jnp.pad(x2d.reshape(s, N_CHIPS, c), ((0, 0), (0, 0), (0, cpad - c))).reshape(s, N_CHIPS * cpad)


def _unchip_cols(x2d, c, cpad):
    s = x2d.shape[0]
    return x2d.reshape(s, N_CHIPS, cpad)[:, :, :c].reshape(s, N_CHIPS * c)


def _add_res(acc, res):
    return (acc + res,)


def _relu2(acc):
    return acc, jnp.square(jnp.maximum(acc, 0.0))


def _relu2_bwd(acc, u):
    return (acc * (2.0 * jnp.maximum(u.astype(F32), 0.0)),)


def _ple_fwd(acc, x2, pp):
    return x2 + pp * _sigmoid(acc), acc


def _ple_bwd(dx, pp, gl):
    gate = _sigmoid(gl)
    return dx * gate, dx * pp * gate * (1.0 - gate)


def _layer_fwd(i, kind, x0, p_bf, wg, lay, gains, extra, tabs):
    s = x0.shape[0]
    an, mn, pn = gains
    sv = dict(x0=x0)
    h1 = rms_fwd(x0, an, name=f"attn_norm_{i}")
    if kind == 0:
        proj = mm_nn(h1, wg, lay["inp"], name=f"w_in_{i}")[0]
        qkv = proj.reshape(s, 3, N_HEADS, HEAD_DIM).transpose(1, 2, 0, 3)
        o, tot = sb_fwd(qkv[0], qkv[1], qkv[2], name=f"sb_fwd_{i}")
        sv.update(qkv=qkv, tot=tot)
    elif kind == 1:
        projp = mm_nn(h1, wg, lay["inp"], name=f"w_in_{i}", out_dtypes=(F32,))[0]
        proj = _unchip_cols(projp, W_IN_COLS[1], W_IN_PAD[1])
        nq = N_HEADS * HEAD_DIM
        nqk = nq + SWA_KV_HEADS * HEAD_DIM
        qk = rope_fwd(proj[:, :nqk], tabs, name=f"rope_{i}")
        q = _heads(qk[:, :nq], N_HEADS).reshape(SWA_KV_HEADS, SWA_GROUP, s, HEAD_DIM)
        front = ((0, 0), (SWA_WINDOW, 0), (0, 0))
        kp = jnp.pad(_heads(qk[:, nq:], SWA_KV_HEADS), front)
        vp = jnp.pad(_heads(proj[:, nqk:].astype(BF16), SWA_KV_HEADS), front)
        sink = jnp.repeat(extra.reshape(SWA_KV_HEADS, SWA_GROUP), ATT_BLK, axis=1)[:, :, None]
        o4, lse = swa_fwd(q, kp, vp, sink, name=f"swa_fwd_{i}")
        o = o4.reshape(N_HEADS, s, HEAD_DIM)
        sv.update(q=q, kp=kp, vp=vp, sink=sink, o4=o4, lse=lse)
    else:
        projp = mm_nn(h1, wg, lay["inp"], name=f"w_in_{i}", out_dtypes=(F32,))[0]
        proj = _unchip_cols(projp, W_IN_COLS[2], W_IN_PAD[2])
        nqkv = 3 * N_HEADS * HEAD_DIM
        qkv = proj[:, :nqkv].astype(BF16).reshape(s, 3, N_HEADS, HEAD_DIM).transpose(1, 2, 0, 3)
        fl = _pad_cols(proj[:, nqkv:], 128)
        bp = _pad_cols(extra[None], 128)
        cum_t = fox_gate_fwd(fl, bp, name=f"gate_fwd_{i}")[:, :N_HEADS].T
        cq = cum_t[:, :, None]
        ck = cum_t.reshape(N_HEADS, s // min(ATT_BK, s), 1, min(ATT_BK, s))
        o, lse = fox_fwd(qkv[0], qkv[1], qkv[2], cq, ck, name=f"fox_fwd_{i}")
        sv.update(qkv=qkv, fl=fl, bp=bp, cq=cq, ck=ck, o=o, lse=lse)
    a = _unheads(o)
    x1 = mm_nn(a, wg, lay["out"], name=f"w_out_{i}", epi=_add_res, extras=(x0,), out_dtypes=(F32,))[0]
    h2 = rms_fwd(x1, mn, name=f"mlp_norm_{i}")
    u, r = mm_nn(h2, wg, lay["up"], name=f"w_up_{i}", epi=_relu2, out_dtypes=(BF16, BF16))
    x2 = mm_nn(r, wg, lay["down"], name=f"w_down_{i}", epi=_add_res, extras=(x1,), out_dtypes=(F32,))[0]
    h3 = rms_fwd(x2, pn, name=f"ple_norm_{i}")
    pp = mm_nn(p_bf, wg, lay["proj"], name=f"w_ple_proj_{i}", out_dtypes=(F32,))[0]
    x3, gl = mm_nn(h3, wg, lay["gate"], name=f"w_ple_gate_{i}", epi=_ple_fwd, extras=(x2, pp), out_dtypes=(F32, F32))
    sv.update(h1=h1, a=a, x1=x1, h2=h2, u=u, r=r, x2=x2, h3=h3, pp=pp, gl=gl)
    return x3, sv


def _layer_bwd(i, kind, dx3, sv, p_bf, wg, lay, n_rows, gains, tabs):
    s = dx3.shape[0]
    an, mn, pn = gains
    g = lax.empty((N_CHIPS, n_rows, SLAB), BF16)
    d_pp, d_gl = ew(_ple_bwd, [dx3, sv["pp"], sv["gl"]], [BF16, BF16], name=f"ple_bwd_{i}")
    g = mm_tn(p_bf, d_pp, g, lay["proj"], name=f"dw_ple_proj_{i}")
    g = mm_tn(sv["h3"], d_gl, g, lay["gate"], name=f"dw_ple_gate_{i}")
    d_h3 = mm_nt(d_gl, wg, lay["gate"], name=f"dx_ple_gate_{i}", out_dtypes=(F32,))[0]
    dx2, dx2b, d_pn = rms_bwd(sv["x2"], pn, d_h3, dx3, name=f"ple_norm_bwd_{i}")
    g = mm_tn(sv["r"], dx2b, g, lay["down"], name=f"dw_down_{i}")
    d_u = mm_nt(dx2b, wg, lay["down"], name=f"dx_down_{i}", epi=_relu2_bwd, extras=(sv["u"],))[0]
    g = mm_tn(sv["h2"], d_u, g, lay["up"], name=f"dw_up_{i}")
    d_h2 = mm_nt(d_u, wg, lay["up"], name=f"dx_up_{i}", out_dtypes=(F32,))[0]
    dx1, dx1b, d_mn = rms_bwd(sv["x1"], mn, d_h2, dx2, name=f"mlp_norm_bwd_{i}")
    g = mm_tn(sv["a"], dx1b, g, lay["out"], name=f"dw_out_{i}")
    d_a = mm_nt(dx1b, wg, lay["out"], name=f"dx_out_{i}")[0]
    do = _heads(d_a, N_HEADS)
    d_extra = None
    if kind == 0:
        qkv = sv["qkv"]
        dq, dk, dv = sb_bwd(qkv[0], qkv[1], qkv[2], sv["tot"], do, name=f"sb_bwd_{i}")
        d_proj = jnp.stack([dq, dk, dv]).transpose(2, 0, 1, 3).reshape(s, 3 * N_HEADS * HEAD_DIM).astype(BF16)
    elif kind == 1:
        do4 = do.reshape(SWA_KV_HEADS, SWA_GROUP, s, HEAD_DIM)
        dq, dkp, dvp, dsr = swa_bwd(sv["q"], sv["kp"], sv["vp"], sv["sink"], sv["o4"], sv["lse"], do4, name=f"swa_bwd_{i}")
        dqk = jnp.concatenate([_unheads(dq.reshape(N_HEADS, s, HEAD_DIM)), _unheads(dkp[:, SWA_WINDOW:])], axis=1)
        dqk = rope_bwd(dqk, tabs, name=f"rope_bwd_{i}")
        d_proj = jnp.concatenate([dqk, _unheads(dvp[:, SWA_WINDOW:]).astype(BF16)], axis=1)
        d_proj = _chip_cols(d_proj, W_IN_COLS[1], W_IN_PAD[1])
        d_extra = jnp.sum(dsr[..., 0], axis=2).reshape(N_HEADS)
    else:
        qkv = sv["qkv"]
        dq, dk, dv, dcq, dck = fox_bwd(qkv[0], qkv[1], qkv[2], sv["o"], sv["lse"], sv["cq"], sv["ck"], do, name=f"fox_bwd_{i}")
        dcum = _pad_cols((dcq[:, :, 0] - dck.reshape(N_HEADS, s)).T, 128)
        dfl, dbp = fox_gate_bwd(dcum, sv["fl"], sv["bp"], name=f"gate_bwd_{i}")
        d_qkv = jnp.stack([dq, dk, dv]).transpose(2, 0, 1, 3).reshape(s, 3 * N_HEADS * HEAD_DIM)
        d_proj = jnp.concatenate([d_qkv, dfl[:, :N_HEADS]], axis=1).astype(BF16)
        d_proj = _chip_cols(d_proj, W_IN_COLS[2], W_IN_PAD[2])
        d_extra = dbp[0, :N_HEADS]
    g = mm_tn(sv["h1"], d_proj, g, lay["inp"], name=f"dw_in_{i}")
    d_h1 = mm_nt(d_proj, wg, lay["inp"], name=f"dx_in_{i}", out_dtypes=(F32,))[0]
    dx0, _, d_an = rms_bwd(sv["x0"], an, d_h1, dx1, name=f"attn_norm_bwd_{i}")
    return dx0, g, (d_an, d_mn, d_pn), d_extra


def _small_rows(a, prefix):
    rows = [a[f"{prefix}{n}_{i}"] for i in range(DEPTH) for n in GAINS] + [a[f"{prefix}final_norm"]]
    rows += [_pad_cols(a[f"{prefix}{n}"][None], D_MODEL)[0] for n in ("sinks_1", "b_forget_2")]
    return jnp.stack(rows + [jnp.zeros((D_MODEL,), F32)])


def _train_step(a):
    x = a["x"][0]
    tabs = rope_tables(x.shape[0], (N_HEADS + SWA_KV_HEADS) * HEAD_DIM)
    lays = [_layout(k) for k in MIXER]

    def natural(prefix, i, m):
        w = a[f"{prefix}{MAT_ARG[m]}_{i}"]
        return _pad_cols(w, W_IN_PAD[MIXER[i]]) if m == "inp" else w

    chip = 2 * lax.axis_index("x") + lax.axis_index("y")
    where = jnp.stack([chip, lax.axis_index("c")]).astype(jnp.int32)
    packed = [jnp.concatenate([_to_slabs(natural("", i, m).astype(BF16)) for m in MATS], axis=0) for i in range(DEPTH)]
    wgs = allgather_weights(
        [lax.dynamic_update_slice(lax.empty((N_CHIPS,) + pk.shape, BF16), pk[None], (chip, 0, 0)) for pk in packed],
        name="allgather_weights")

    gains = [tuple(a[f"{n}_{i}"][None] for n in GAINS) for i in range(DEPTH)]
    extras = [None, a["sinks_1"], a["b_forget_2"], None]
    p_bf = [a["p"][i, 0].astype(BF16) for i in range(DEPTH)]

    saved = []
    for i in range(DEPTH):
        x, sv = _layer_fwd(i, MIXER[i], x, p_bf[i], wgs[i], lays[i][0], gains[i], extras[i], tabs)
        saved.append(sv)
    dx, d_final, loss = loss_head(x, a["final_norm"][None], a["loss_target"][0], name="loss_head")

    small = [None] * N_SMALL
    small[12] = d_final[0]
    small[15] = _pad_cols(loss[:, :1], D_MODEL)[0]
    grads = [None] * DEPTH
    for i in reversed(range(DEPTH)):
        dx, grads[i], d_gains, d_extra = _layer_bwd(i, MIXER[i], dx, saved[i], p_bf[i], wgs[i], lays[i][0], lays[i][1],
                                                    gains[i], tabs)
        for j in range(3):
            small[3 * i + j] = d_gains[j][0]
        if d_extra is not None:
            small[12 + MIXER[i]] = _pad_cols(d_extra[None], D_MODEL)[0]
    small = allreduce_small(jnp.stack(small), name="allreduce_small")

    theirs = swap_halves(grads, name="swap_halves")
    pair = [add_pairs(grads[i], theirs[i], where, name=f"add_pairs_{i}") for i in range(DEPTH)]
    got = scatter_chips(pair, name="scatter_chips")
    gfull = join_halves([add_chips(pair[i], got[i], where, name=f"add_chips_{i}") for i in range(DEPTH)],
                        name="join_halves")

    out = {"loss": small[15, 0], "grad_x": dx[None]}
    for i in range(DEPTH):
        for m in MATS:
            res = adamw(gfull[i], lays[i][0][m], natural("", i, m), natural("m_", i, m), natural("v_", i, m),
                        name=f"adamw_{MAT_ARG[m]}_{i}")
            cols = a[f"{MAT_ARG[m]}_{i}"].shape[1]
            for kd, r in zip(KINDS, res):
                out[f"{kd}{MAT_ARG[m]}_{i}"] = r[:, :cols]
    res = adamw_small(small, _small_rows(a, ""), _small_rows(a, "m_"), _small_rows(a, "v_"), name="adamw_small")
    for kd, r in zip(KINDS, res):
        for i in range(DEPTH):
            for j, n in enumerate(GAINS):
                out[f"{kd}{n}_{i}"] = r[3 * i + j]
        out[f"{kd}final_norm"] = r[12]
        out[f"{kd}sinks_1"] = r[13, :N_HEADS]
        out[f"{kd}b_forget_2"] = r[14, :N_HEADS]
    return out


def _weight_names():
    names = []
    for i in range(DEPTH):
        names += [f"attn_norm_{i}", f"w_in_{i}", f"w_out_{i}"] + [[], ["sinks_1"], ["b_forget_2"]][MIXER[i]]
        names += [f"mlp_norm_{i}", f"w_up_{i}", f"w_down_{i}", f"ple_norm_{i}", f"w_ple_gate_{i}", f"w_ple_proj_{i}"]
    return names + ["final_norm"]


def kernel(x, p, attn_norm_0, w_in_0, w_out_0, mlp_norm_0, w_up_0, w_down_0, ple_norm_0, w_ple_gate_0, w_ple_proj_0, attn_norm_1, w_in_1, w_out_1, sinks_1, mlp_norm_1, w_up_1, w_down_1, ple_norm_1, w_ple_gate_1, w_ple_proj_1, attn_norm_2, w_in_2, w_out_2, b_forget_2, mlp_norm_2, w_up_2, w_down_2, ple_norm_2, w_ple_gate_2, w_ple_proj_2, attn_norm_3, w_in_3, w_out_3, mlp_norm_3, w_up_3, w_down_3, ple_norm_3, w_ple_gate_3, w_ple_proj_3, final_norm, loss_target, m_attn_norm_0, m_w_in_0, m_w_out_0, m_mlp_norm_0, m_w_up_0, m_w_down_0, m_ple_norm_0, m_w_ple_gate_0, m_w_ple_proj_0, m_attn_norm_1, m_w_in_1, m_w_out_1, m_sinks_1, m_mlp_norm_1, m_w_up_1, m_w_down_1, m_ple_norm_1, m_w_ple_gate_1, m_w_ple_proj_1, m_attn_norm_2, m_w_in_2, m_w_out_2, m_b_forget_2, m_mlp_norm_2, m_w_up_2, m_w_down_2, m_ple_norm_2, m_w_ple_gate_2, m_w_ple_proj_2, m_attn_norm_3, m_w_in_3, m_w_out_3, m_mlp_norm_3, m_w_up_3, m_w_down_3, m_ple_norm_3, m_w_ple_gate_3, m_w_ple_proj_3, m_final_norm, v_attn_norm_0, v_w_in_0, v_w_out_0, v_mlp_norm_0, v_w_up_0, v_w_down_0, v_ple_norm_0, v_w_ple_gate_0, v_w_ple_proj_0, v_attn_norm_1, v_w_in_1, v_w_out_1, v_sinks_1, v_mlp_norm_1, v_w_up_1, v_w_down_1, v_ple_norm_1, v_w_ple_gate_1, v_w_ple_proj_1, v_attn_norm_2, v_w_in_2, v_w_out_2, v_b_forget_2, v_mlp_norm_2, v_w_up_2, v_w_down_2, v_ple_norm_2, v_w_ple_gate_2, v_w_ple_proj_2, v_attn_norm_3, v_w_in_3, v_w_out_3, v_mlp_norm_3, v_w_up_3, v_w_down_3, v_ple_norm_3, v_w_ple_gate_3, v_w_ple_proj_3, v_final_norm):
    out = _train_step(dict(locals()))
    return (out["loss"], out["grad_x"], *[out[kd + n] for kd in KINDS for n in _weight_names()])
```

```python
import jax
import jax.numpy as jnp
from jax import lax
from jax.experimental import pallas as pl
from jax.experimental.pallas import tpu as pltpu

F32 = jnp.float32
BF16 = jnp.bfloat16

D_MODEL = 1024
N_HEADS = 16
HEAD_DIM = 64
SWA_KV_HEADS = 2
SWA_GROUP = 8
SWA_WINDOW = 128
ROPE_THETA = 500000.0
ROPE_DIM = 16
RMS_EPS = 1e-6
NEG_INF = -1e30
ATTN_SCALE = HEAD_DIM ** -0.5
N_CHIPS = 4
N_DEVICES = 8

SLAB = 256
ATT_BLK = 128
ATT_BQ = 512
ATT_BK = 512
ROW_TILE = 256
V7X_VMEM_LIMIT = 56 * 1024 * 1024

ADAM_LR, ADAM_B1, ADAM_B2, ADAM_EPS, ADAM_WD, ADAM_STEP = 0.001, 0.9, 0.999, 1e-08, 0.01, 10


def _cparams(sem=None):
    return pltpu.CompilerParams(dimension_semantics=sem, vmem_limit_bytes=V7X_VMEM_LIMIT)


def _dot(a, b):
    return jnp.dot(a, b, preferred_element_type=F32)


def _dot_nt(a, b):
    return lax.dot_general(a, b, (((1,), (1,)), ((), ())), preferred_element_type=F32)


def _dot_tn(a, b):
    return lax.dot_general(a, b, (((0,), (0,)), ((), ())), preferred_element_type=F32)


def _row_tile(M, K):
    return min(M, 1024) if K >= 1024 else M


def _finish(epi, acc, ex, outs):
    res = epi(acc, *[e[...] for e in ex]) if epi is not None else (acc,)
    for o, r in zip(outs, res):
        o[...] = r.astype(o.dtype)


def mm_nn(a, wg, t, *, name, epi=None, extras=(), out_dtypes=(BF16,)):
    off, K, ns, row = t
    M = a.shape[0]
    sb = off // K
    ne, no = len(extras), len(out_dtypes)
    if row:
        tm = _row_tile(M, K)
        nb = N_CHIPS
        grid = (M // tm, ns)
        a_spec = pl.BlockSpec((tm, N_CHIPS * K), lambda i, q: (i, 0))
        b_specs = [pl.BlockSpec((None, K, SLAB), lambda i, q, j=j: (j, sb + q, 0)) for j in range(nb)]
        tile = pl.BlockSpec((tm, SLAB), lambda i, q: (i, q))
        n_out = ns * SLAB
    else:
        nb = 1
        grid = (N_CHIPS, ns)
        a_spec = pl.BlockSpec((M, K), lambda j, q: (0, 0))
        b_specs = [pl.BlockSpec((None, K, SLAB), lambda j, q: (j, sb + q, 0))]
        tile = pl.BlockSpec((M, SLAB), lambda j, q: (0, j * ns + q))
        n_out = N_CHIPS * ns * SLAB

    def body(a_ref, *rest):
        bs, ex, outs = rest[:nb], rest[nb:nb + ne], rest[nb + ne:]
        acc = _dot(a_ref[:, pl.ds(0, K)], bs[0][...])
        for j in range(1, nb):
            acc = acc + _dot(a_ref[:, pl.ds(j * K, K)], bs[j][...])
        _finish(epi, acc, ex, outs)

    return pl.pallas_call(
        body, name=name, grid=grid,
        in_specs=[a_spec] + b_specs + [tile] * ne, out_specs=[tile] * no,
        out_shape=[jax.ShapeDtypeStruct((M, n_out), d) for d in out_dtypes],
        compiler_params=_cparams(("parallel", "parallel")),
    )(a, *([wg] * nb), *extras)


def mm_nt(dy, wg, t, *, name, epi=None, extras=(), out_dtypes=(BF16,)):
    off, K, ns, row = t
    M = dy.shape[0]
    tm = _row_tile(M, K)
    sb = off // K
    ne, no = len(extras), len(out_dtypes)
    grid = (M // tm, N_CHIPS)
    b_specs = [pl.BlockSpec((None, K, SLAB), lambda i, j, q=q: (j, sb + q, 0)) for q in range(ns)]
    if row:
        dy_spec = pl.BlockSpec((tm, ns * SLAB), lambda i, j: (i, 0))
        tile = pl.BlockSpec((tm, K), lambda i, j: (i, j))
        n_out = N_CHIPS * K
        sem = ("parallel", "parallel")
    else:
        dy_spec = pl.BlockSpec((tm, ns * SLAB), lambda i, j: (i, j))
        tile = pl.BlockSpec((tm, K), lambda i, j: (i, 0))
        n_out = K
        sem = ("parallel", "arbitrary")

    def body(dy_ref, *rest):
        bs, ex, outs = rest[:ns], rest[ns:ns + ne], rest[ns + ne:ns + ne + no]
        part = _dot_nt(dy_ref[:, pl.ds(0, SLAB)], bs[0][...])
        for q in range(1, ns):
            part = part + _dot_nt(dy_ref[:, pl.ds(q * SLAB, SLAB)], bs[q][...])
        if row:
            _finish(epi, part, ex, outs)
        else:
            acc_ref = rest[-1]
            j = pl.program_id(1)

            @pl.when(j == 0)
            def _():
                acc_ref[...] = part

            @pl.when(j > 0)
            def _():
                acc_ref[...] += part

            @pl.when(j == N_CHIPS - 1)
            def _():
                _finish(epi, acc_ref[...], ex, outs)

    return pl.pallas_call(
        body, name=name, grid=grid,
        in_specs=[dy_spec] + b_specs + [tile] * ne, out_specs=[tile] * no,
        out_shape=[jax.ShapeDtypeStruct((M, n_out), d) for d in out_dtypes],
        scratch_shapes=[] if row else [pltpu.VMEM((tm, K), F32)],
        compiler_params=_cparams(sem),
    )(dy, *([wg] * ns), *extras)


def mm_tn(x, dy, g, t, *, name):
    off, K, ns, row = t
    S = x.shape[0]
    sb = off // K
    if row:
        x_map = lambda j, q: (0, j)
        dy_map = lambda j, q: (0, q)
    else:
        x_map = lambda j, q: (0, 0)
        dy_map = lambda j, q: (0, j * ns + q)

    def body(g_in, x_ref, dy_ref, o_ref):
        del g_in
        o_ref[...] = _dot_tn(x_ref[...], dy_ref[...]).astype(o_ref.dtype)

    return pl.pallas_call(
        body, name=name, grid=(N_CHIPS, ns),
        in_specs=[pl.BlockSpec(memory_space=pl.ANY), pl.BlockSpec((S, K), x_map), pl.BlockSpec((S, SLAB), dy_map)],
        out_specs=pl.BlockSpec((None, K, SLAB), lambda j, q: (j, sb + q, 0)),
        out_shape=jax.ShapeDtypeStruct(g.shape, g.dtype),
        input_output_aliases={0: 0},
        compiler_params=_cparams(("parallel", "parallel")),
    )(g, x, dy)


def ew(fn, ins, out_dtypes, *, name, bcast=()):
    S = ins[0].shape[0]
    tr = min(ROW_TILE, S)
    cols = ins[0].shape[1]
    ni, nb = len(ins), len(bcast)

    def body(*refs):
        res = fn(*[r[...] for r in refs[:ni + nb]])
        for o, r in zip(refs[ni + nb:], res):
            o[...] = r.astype(o.dtype)

    return pl.pallas_call(
        body, name=name, grid=(S // tr,),
        in_specs=[pl.BlockSpec((tr, a.shape[1]), lambda i: (i, 0)) for a in ins]
        + [pl.BlockSpec(b.shape, lambda i: (0, 0)) for b in bcast],
        out_specs=[pl.BlockSpec((tr, cols), lambda i: (i, 0)) for _ in out_dtypes],
        out_shape=[jax.ShapeDtypeStruct((S, cols), d) for d in out_dtypes],
        compiler_params=_cparams(("parallel",)),
    )(*ins, *bcast)


def _rstd(x):
    return lax.rsqrt(jnp.mean(x * x, axis=-1, keepdims=True) + RMS_EPS)


def _sigmoid(x):
    return 1.0 / (1.0 + jnp.exp(-x))


def _log_sigmoid(z):
    return jnp.minimum(z, 0.0) - jnp.log(1.0 + jnp.exp(-jnp.abs(z)))


def rms_fwd(x, g, *, name):
    return ew(lambda xv, gv: (xv * _rstd(xv) * gv,), [x], [BF16], name=name, bcast=[g])[0]


def _rms_bwd_tile(xv, gv, dh):
    rstd = _rstd(xv)
    xhat = xv * rstd
    gd = dh * gv
    dx = rstd * (gd - xhat * jnp.mean(xhat * gd, axis=-1, keepdims=True))
    return dx, jnp.sum(dh * xhat, axis=0, keepdims=True)


def rms_bwd(x, g, dh, dres, *, name):
    S, D = x.shape
    tr = min(ROW_TILE, S)

    def body(x_ref, g_ref, dh_ref, dres_ref, dx_ref, dxb_ref, dg_ref):
        i = pl.program_id(0)
        dx, dg = _rms_bwd_tile(x_ref[...], g_ref[...], dh_ref[...])
        dx = dx + dres_ref[...]
        dx_ref[...] = dx
        dxb_ref[...] = dx.astype(BF16)

        @pl.when(i == 0)
        def _():
            dg_ref[...] = dg

        @pl.when(i > 0)
        def _():
            dg_ref[...] += dg

    row = pl.BlockSpec((tr, D), lambda i: (i, 0))
    one = pl.BlockSpec((1, D), lambda i: (0, 0))
    return pl.pallas_call(
        body, name=name, grid=(S // tr,),
        in_specs=[row, one, row, row], out_specs=[row, row, one],
        out_shape=[jax.ShapeDtypeStruct((S, D), F32), jax.ShapeDtypeStruct((S, D), BF16),
                   jax.ShapeDtypeStruct((1, D), F32)],
        compiler_params=_cparams(("arbitrary",)),
    )(x, g, dh, dres)


def loss_head(x, g, target, *, name):
    S, D = x.shape
    tr = min(ROW_TILE, S)

    def body(x_ref, g_ref, t_ref, dx_ref, dg_ref, loss_ref):
        i = pl.program_id(0)
        xv, gv = x_ref[...], g_ref[...]
        err = xv * _rstd(xv) * gv - t_ref[...]
        part = 0.5 * jnp.sum(jnp.mean(err * err, axis=-1, keepdims=True), axis=0, keepdims=True)
        dx, dg = _rms_bwd_tile(xv, gv, err * (1.0 / D))
        dx_ref[...] = dx
        part = jnp.broadcast_to(part, loss_ref.shape)

        @pl.when(i == 0)
        def _():
            dg_ref[...] = dg
            loss_ref[...] = part

        @pl.when(i > 0)
        def _():
            dg_ref[...] += dg
            loss_ref[...] += part

    row = pl.BlockSpec((tr, D), lambda i: (i, 0))
    one = pl.BlockSpec((1, D), lambda i: (0, 0))
    return pl.pallas_call(
        body, name=name, grid=(S // tr,),
        in_specs=[row, one, row], out_specs=[row, one, pl.BlockSpec((1, 128), lambda i: (0, 0))],
        out_shape=[jax.ShapeDtypeStruct((S, D), F32), jax.ShapeDtypeStruct((1, D), F32),
                   jax.ShapeDtypeStruct((1, 128), F32)],
        compiler_params=_cparams(("arbitrary",)),
    )(x, g, target)


def rope_tables(S, n_cols):
    half = ROPE_DIM // 2
    inv_freq = ROPE_THETA ** (-jnp.arange(half, dtype=F32) / half)
    ang = jnp.arange(S, dtype=F32)[:, None] * inv_freq[None, :]
    cos, sin = jnp.cos(ang), jnp.sin(ang)
    z = jnp.zeros((S, HEAD_DIM - ROPE_DIM), F32)
    zh = jnp.zeros((S, half), F32)
    c = jnp.concatenate([cos, cos, jnp.ones_like(z)], axis=1)
    sa = jnp.concatenate([zh, sin, z], axis=1)
    sb = jnp.concatenate([-sin, zh, z], axis=1)
    return [jnp.tile(t, (1, n_cols // HEAD_DIM)) for t in (c, sa, sb)]


def rope_fwd(xqk, tables, *, name):
    n, half = xqk.shape[1], ROPE_DIM // 2

    def fn(x, c, sa, sb):
        return (x * c + pltpu.roll(x, half, 1) * sa + pltpu.roll(x, n - half, 1) * sb,)

    return ew(fn, [xqk] + list(tables), [BF16], name=name)[0]


def rope_bwd(dy, tables, *, name):
    n, half = dy.shape[1], ROPE_DIM // 2

    def fn(d, c, sa, sb):
        return (d * c + pltpu.roll(d * sa, n - half, 1) + pltpu.roll(d * sb, half, 1),)

    return ew(fn, [dy] + list(tables), [BF16], name=name)[0]


def _split3(x):
    h1 = x.astype(BF16)
    r1 = x - h1.astype(F32)
    h2 = r1.astype(BF16)
    return h1, h2, (r1 - h2.astype(F32)).astype(BF16)


def _split2(x):
    h1 = x.astype(BF16)
    return h1, (x - h1.astype(F32)).astype(BF16)


def _tri(n, cmp):
    r = lax.broadcasted_iota(jnp.int32, (n, n), 0)
    c = lax.broadcasted_iota(jnp.int32, (n, n), 1)
    return cmp(r, c).astype(BF16)


def fox_gate_fwd(fl, b, *, name):
    S, W = fl.shape
    tr = min(ROW_TILE, S)

    def body(fl_ref, b_ref, cum_ref, carry):
        i = pl.program_id(0)

        @pl.when(i == 0)
        def _():
            carry[...] = jnp.zeros_like(carry)

        lower = _tri(tr, lambda r, c: r >= c)
        cs = carry[...]
        for piece in _split3(_log_sigmoid(fl_ref[...] + b_ref[...])):
            cs = cs + _dot(lower, piece)
        cum_ref[...] = cs
        carry[...] = cs[tr - 1:tr, :]

    return pl.pallas_call(
        body, name=name, grid=(S // tr,),
        in_specs=[pl.BlockSpec((tr, W), lambda i: (i, 0)), pl.BlockSpec((1, W), lambda i: (0, 0))],
        out_specs=pl.BlockSpec((tr, W), lambda i: (i, 0)),
        out_shape=jax.ShapeDtypeStruct((S, W), F32),
        scratch_shapes=[pltpu.VMEM((1, W), F32)],
        compiler_params=_cparams(("arbitrary",)),
    )(fl, b)


def fox_gate_bwd(dcum, fl, b, *, name):
    S, W = fl.shape
    tr = min(ROW_TILE, S)
    nb = S // tr

    def body(dc_ref, fl_ref, b_ref, dfl_ref, db_ref, carry):
        i = pl.program_id(0)

        @pl.when(i == 0)
        def _():
            carry[...] = jnp.zeros_like(carry)

        upper = _tri(tr, lambda r, c: r <= c)
        cs = carry[...]
        for piece in _split3(dc_ref[...]):
            cs = cs + _dot(upper, piece)
        carry[...] = cs[0:1, :]
        dfl = cs * _sigmoid(-(fl_ref[...] + b_ref[...]))
        dfl_ref[...] = dfl
        db = jnp.sum(dfl, axis=0, keepdims=True)

        @pl.when(i == 0)
        def _():
            db_ref[...] = db

        @pl.when(i > 0)
        def _():
            db_ref[...] += db

    rev = pl.BlockSpec((tr, W), lambda i: (nb - 1 - i, 0))
    one = pl.BlockSpec((1, W), lambda i: (0, 0))
    return pl.pallas_call(
        body, name=name, grid=(nb,),
        in_specs=[rev, rev, one], out_specs=[rev, one],
        out_shape=[jax.ShapeDtypeStruct((S, W), F32), jax.ShapeDtypeStruct((1, W), F32)],
        scratch_shapes=[pltpu.VMEM((1, W), F32)],
        compiler_params=_cparams(("arbitrary",)),
    )(dcum, fl, b)


def _blk_iota(tq, tk):
    return (lax.broadcasted_iota(jnp.int32, (tq, tk), 0), lax.broadcasted_iota(jnp.int32, (tq, tk), 1))


def _cs(xb, tri):
    return _dot(xb, tri)


def _rowsum(xb):
    return jnp.sum(xb.astype(F32), axis=1, keepdims=True)


def _sb_block(qs, k, cmr, shift):
    z = _dot_nt(qs, k)
    strict = cmr < shift
    lb = jnp.minimum(z, 0.0) - jnp.log(1.0 + jnp.exp(-jnp.abs(z)))
    lom = jnp.where(strict, lb - z, 0.0).astype(BF16)
    return lb, lom, strict


def _att_tiles(S):
    return min(ATT_BQ, S), min(ATT_BK, S)


def _head_specs(S, tq):
    qspec = pl.BlockSpec((None, tq, HEAD_DIM), lambda h, i: (h, i, 0))
    kvspec = pl.BlockSpec((None, S, HEAD_DIM), lambda h, i: (h, 0, 0))
    vec = pl.BlockSpec((None, tq, 1), lambda h, i: (h, i, 0))
    return qspec, kvspec, vec


def sb_fwd(q, k, v, *, name):
    H, S, _ = q.shape
    tq, tk = _att_tiles(S)
    qspec, kvspec, vec = _head_specs(S, tq)

    def body(q_ref, k_ref, v_ref, o_ref, t_ref):
        i = pl.program_id(1)
        qs = q_ref[...] * ATTN_SCALE
        row, col = _blk_iota(tq, tk)
        cmr = col - row
        below = _tri(tk, lambda r, c: r > c)
        nkb = (i + 1) * (tq // tk)

        def step(n, carry):
            r_sum, acc = carry
            kb = nkb - 1 - n
            ks = pl.multiple_of(kb * tk, tk)
            lb, lom, strict = _sb_block(qs, k_ref[pl.ds(ks, tk), :], cmr, i * tq - kb * tk)
            w = jnp.where(strict, jnp.exp(lb + _cs(lom, below) + r_sum), 0.0)
            acc = acc + _dot(w.astype(BF16), v_ref[pl.ds(ks, tk), :])
            return r_sum + _rowsum(lom), acc

        r_sum, acc = lax.fori_loop(0, nkb, step, (jnp.zeros((tq, 1), F32), jnp.zeros((tq, HEAD_DIM), F32)))
        o_ref[...] = acc.astype(o_ref.dtype)
        t_ref[...] = r_sum

    return pl.pallas_call(
        body, name=name, grid=(H, S // tq),
        in_specs=[qspec, kvspec, kvspec], out_specs=[qspec, vec],
        out_shape=[jax.ShapeDtypeStruct((H, S, HEAD_DIM), BF16), jax.ShapeDtypeStruct((H, S, 1), F32)],
        compiler_params=_cparams(("parallel", "arbitrary")),
    )(q, k, v)


def sb_bwd(q, k, v, tot, do, *, name):
    H, S, _ = q.shape
    tq, tk = _att_tiles(S)
    qspec, kvspec, vec = _head_specs(S, tq)

    def body(q_ref, k_ref, v_ref, t_ref, do_ref, dq_ref, dk_ref, dv_ref):
        i = pl.program_id(1)

        @pl.when(i == 0)
        def _():
            dk_ref[...] = jnp.zeros_like(dk_ref)
            dv_ref[...] = jnp.zeros_like(dv_ref)

        qs, dov, t_all = q_ref[...] * ATTN_SCALE, do_ref[...], t_ref[...]
        row, col = _blk_iota(tq, tk)
        cmr = col - row
        upto = _tri(tk, lambda r, c: r <= c)
        before = _tri(tk, lambda r, c: r < c)

        def step(kb, carry):
            p_sum, e_sum, dq = carry
            ks = pl.multiple_of(kb * tk, tk)
            kv = k_ref[pl.ds(ks, tk), :]
            vv = v_ref[pl.ds(ks, tk), :]
            lb, lom, strict = _sb_block(qs, kv, cmr, i * tq - kb * tk)
            tail = t_all - p_sum - _cs(lom, upto)
            w = jnp.where(strict, jnp.exp(lb + tail), 0.0)
            e = _dot_nt(dov, vv) * w
            eb = e.astype(BF16)
            e_before = e_sum + _cs(eb, before)
            beta = jnp.exp(lb)
            dzb = jnp.where(strict, e - (e + e_before) * beta, 0.0).astype(BF16)
            dk_ref[pl.ds(ks, tk), :] += _dot_tn(dzb, qs)
            dv_ref[pl.ds(ks, tk), :] += _dot_tn(w.astype(BF16), dov)
            return (p_sum + _rowsum(lom), e_sum + _rowsum(eb),
                    dq + _dot(dzb, kv))

        zero = jnp.zeros((tq, 1), F32)
        _, _, dq = lax.fori_loop(0, (i + 1) * (tq // tk), step, (zero, zero, jnp.zeros((tq, HEAD_DIM), F32)))
        dq_ref[...] = dq * ATTN_SCALE

    full = jax.ShapeDtypeStruct((H, S, HEAD_DIM), F32)
    return pl.pallas_call(
        body, name=name, grid=(H, S // tq),
        in_specs=[qspec, kvspec, kvspec, vec, qspec], out_specs=[qspec, kvspec, kvspec],
        out_shape=[full, full, full],
        compiler_params=_cparams(("parallel", "arbitrary")),
    )(q, k, v, tot, do)


def _fox_logits(qs, k, cq, ck, cmr, shift):
    causal = cmr <= shift
    return jnp.where(causal, _dot_nt(qs, k) + cq - ck, NEG_INF), causal


def fox_fwd(q, k, v, cq, ck, *, name):
    H, S, _ = q.shape
    tq, tk = _att_tiles(S)
    qspec, kvspec, vec = _head_specs(S, tq)
    ckspec = pl.BlockSpec((None, S // tk, 1, tk), lambda h, i: (h, 0, 0, 0))

    def body(q_ref, k_ref, v_ref, cq_ref, ck_ref, o_ref, lse_ref):
        i = pl.program_id(1)
        qs, cqv = q_ref[...] * ATTN_SCALE, cq_ref[...]
        row, col = _blk_iota(tq, tk)
        cmr = col - row

        def step(kb, carry):
            m, l, acc = carry
            ks = pl.multiple_of(kb * tk, tk)
            s, _ = _fox_logits(qs, k_ref[pl.ds(ks, tk), :], cqv, ck_ref[kb], cmr, i * tq - kb * tk)
            m_new = jnp.maximum(m, jnp.max(s, axis=1, keepdims=True))
            alpha = jnp.exp(m - m_new)
            p = jnp.exp(s - m_new)
            l = alpha * l + jnp.sum(p, axis=1, keepdims=True)
            acc = alpha * acc + _dot(p.astype(BF16), v_ref[pl.ds(ks, tk), :])
            return m_new, l, acc

        m, l, acc = lax.fori_loop(0, (i + 1) * (tq // tk), step,
                                  (jnp.full((tq, 1), NEG_INF, F32), jnp.zeros((tq, 1), F32), jnp.zeros((tq, HEAD_DIM), F32)))
        o_ref[...] = (acc / l).astype(o_ref.dtype)
        lse_ref[...] = m + jnp.log(l)

    return pl.pallas_call(
        body, name=name, grid=(H, S // tq),
        in_specs=[qspec, kvspec, kvspec, vec, ckspec], out_specs=[qspec, vec],
        out_shape=[jax.ShapeDtypeStruct((H, S, HEAD_DIM), BF16), jax.ShapeDtypeStruct((H, S, 1), F32)],
        compiler_params=_cparams(("parallel", "arbitrary")),
    )(q, k, v, cq, ck)


def fox_bwd(q, k, v, o, lse, cq, ck, do, *, name):
    H, S, _ = q.shape
    tq, tk = _att_tiles(S)
    qspec, kvspec, vec = _head_specs(S, tq)
    ckspec = pl.BlockSpec((None, S // tk, 1, tk), lambda h, i: (h, 0, 0, 0))

    def body(q_ref, k_ref, v_ref, o_ref, lse_ref, cq_ref, ck_ref, do_ref, dq_ref, dk_ref, dv_ref, dcq_ref, dck_ref):
        i = pl.program_id(1)

        @pl.when(i == 0)
        def _():
            dk_ref[...] = jnp.zeros_like(dk_ref)
            dv_ref[...] = jnp.zeros_like(dv_ref)
            dck_ref[...] = jnp.zeros_like(dck_ref)

        qs, dov, cqv, lsev = q_ref[...] * ATTN_SCALE, do_ref[...], cq_ref[...], lse_ref[...]
        delta = jnp.sum(dov.astype(F32) * o_ref[...].astype(F32), axis=1, keepdims=True)
        row, col = _blk_iota(tq, tk)
        cmr = col - row

        def step(kb, carry):
            dq, dcq = carry
            ks = pl.multiple_of(kb * tk, tk)
            kv = k_ref[pl.ds(ks, tk), :]
            vv = v_ref[pl.ds(ks, tk), :]
            s, causal = _fox_logits(qs, kv, cqv, ck_ref[kb], cmr, i * tq - kb * tk)
            p = jnp.where(causal, jnp.exp(s - lsev), 0.0)
            ds = p * (_dot_nt(dov, vv) - delta)
            dck_ref[kb] += jnp.sum(ds, axis=0, keepdims=True)
            dsb = ds.astype(BF16)
            dk_ref[pl.ds(ks, tk), :] += _dot_tn(dsb, qs)
            dv_ref[pl.ds(ks, tk), :] += _dot_tn(p.astype(BF16), dov)
            return dq + _dot(dsb, kv), dcq + jnp.sum(ds, axis=1, keepdims=True)

        dq, dcq = lax.fori_loop(0, (i + 1) * (tq // tk), step, (jnp.zeros((tq, HEAD_DIM), F32), jnp.zeros((tq, 1), F32)))
        dq_ref[...] = dq * ATTN_SCALE
        dcq_ref[...] = dcq

    full = jax.ShapeDtypeStruct((H, S, HEAD_DIM), F32)
    return pl.pallas_call(
        body, name=name, grid=(H, S // tq),
        in_specs=[qspec, kvspec, kvspec, qspec, vec, vec, ckspec, qspec],
        out_specs=[qspec, kvspec, kvspec, vec, ckspec],
        out_shape=[full, full, full, jax.ShapeDtypeStruct((H, S, 1), F32),
                   jax.ShapeDtypeStruct((H, S // tk, 1, tk), F32)],
        compiler_params=_cparams(("parallel", "arbitrary")),
    )(q, k, v, o, lse, cq, ck, do)


def _swa_specs(S, tq):
    qspec = pl.BlockSpec((None, SWA_GROUP, tq, HEAD_DIM), lambda g, i: (g, 0, i, 0))
    kvspec = pl.BlockSpec((None, S + SWA_WINDOW, HEAD_DIM), lambda g, i: (g, 0, 0))
    vec = pl.BlockSpec((None, SWA_GROUP, tq, 1), lambda g, i: (g, 0, i, 0))
    sink = pl.BlockSpec((None, SWA_GROUP * tq, 1), lambda g, i: (g, 0, 0))
    return qspec, kvspec, vec, sink


def _swa_logits(q2, kw, i, tq):
    rows = q2.shape[0]
    r = lax.broadcasted_iota(jnp.int32, (rows, 2 * tq), 0)
    c = lax.broadcasted_iota(jnp.int32, (rows, 2 * tq), 1)
    diff = (r & (tq - 1)) + tq - c
    ok = (diff >= 0) & (diff < SWA_WINDOW) & (c + (i - 1) * tq >= 0)
    return jnp.where(ok, _dot_nt(q2, kw) * ATTN_SCALE, NEG_INF), ok


def swa_fwd(q, kp, vp, sink, *, name):
    _, G, S, _ = q.shape
    tq = ATT_BLK
    qspec, kvspec, vec, sinkspec = _swa_specs(S, tq)

    def body(q_ref, k_ref, v_ref, s_ref, o_ref, lse_ref):
        i = pl.program_id(1)
        q2 = q_ref[...].reshape(G * tq, HEAD_DIM)
        ws = pl.multiple_of(i * tq, tq)
        logits, _ = _swa_logits(q2, k_ref[pl.ds(ws, 2 * tq), :], i, tq)
        sk = s_ref[...]
        m = jnp.maximum(jnp.max(logits, axis=1, keepdims=True), sk)
        e = jnp.exp(logits - m)
        den = jnp.sum(e, axis=1, keepdims=True) + jnp.exp(sk - m)
        o = _dot((e / den).astype(BF16), v_ref[pl.ds(ws, 2 * tq), :])
        o_ref[...] = o.reshape(G, tq, HEAD_DIM).astype(o_ref.dtype)
        lse_ref[...] = (m + jnp.log(den)).reshape(G, tq, 1)

    return pl.pallas_call(
        body, name=name, grid=(SWA_KV_HEADS, S // tq),
        in_specs=[qspec, kvspec, kvspec, sinkspec], out_specs=[qspec, vec],
        out_shape=[jax.ShapeDtypeStruct(q.shape, BF16), jax.ShapeDtypeStruct((SWA_KV_HEADS, G, S, 1), F32)],
        compiler_params=_cparams(("parallel", "arbitrary")),
    )(q, kp, vp, sink)


def swa_bwd(q, kp, vp, sink, o, lse, do, *, name):
    _, G, S, _ = q.shape
    tq = ATT_BLK
    qspec, kvspec, vec, sinkspec = _swa_specs(S, tq)

    def body(q_ref, k_ref, v_ref, s_ref, o_ref, lse_ref, do_ref, dq_ref, dk_ref, dv_ref, dsink_ref):
        i = pl.program_id(1)

        @pl.when(i == 0)
        def _():
            dk_ref[...] = jnp.zeros_like(dk_ref)
            dv_ref[...] = jnp.zeros_like(dv_ref)

        q2 = q_ref[...].reshape(G * tq, HEAD_DIM)
        do2 = do_ref[...].reshape(G * tq, HEAD_DIM)
        o2 = o_ref[...].reshape(G * tq, HEAD_DIM)
        lse2 = lse_ref[...].reshape(G * tq, 1)
        ws = pl.multiple_of(i * tq, tq)
        kw = k_ref[pl.ds(ws, 2 * tq), :]
        vw = v_ref[pl.ds(ws, 2 * tq), :]
        logits, ok = _swa_logits(q2, kw, i, tq)
        p = jnp.where(ok, jnp.exp(logits - lse2), 0.0)
        delta = jnp.sum(do2.astype(F32) * o2.astype(F32), axis=1, keepdims=True)
        ds = p * (_dot_nt(do2, vw) - delta)
        dsb = ds.astype(BF16)
        dq_ref[...] = (_dot(dsb, kw) * ATTN_SCALE).reshape(G, tq, HEAD_DIM)
        dk_ref[pl.ds(ws, 2 * tq), :] += _dot_tn(dsb, q2) * ATTN_SCALE
        dv_ref[pl.ds(ws, 2 * tq), :] += _dot_tn(p.astype(BF16), do2)
        dsink_ref[...] = (-jnp.exp(s_ref[...] - lse2) * delta).reshape(G, tq, 1)

    kvshape = jax.ShapeDtypeStruct(kp.shape, F32)
    return pl.pallas_call(
        body, name=name, grid=(SWA_KV_HEADS, S // tq),
        in_specs=[qspec, kvspec, kvspec, sinkspec, qspec, vec, qspec],
        out_specs=[qspec, kvspec, kvspec, vec],
        out_shape=[jax.ShapeDtypeStruct(q.shape, F32), kvshape, kvshape,
                   jax.ShapeDtypeStruct((SWA_KV_HEADS, G, S, 1), F32)],
        compiler_params=_cparams(("parallel", "arbitrary")),
    )(q, kp, vp, sink, o, lse, do)


def _adamw_tile(w, g, m, v):
    m = ADAM_B1 * m + (1.0 - ADAM_B1) * g
    v = ADAM_B2 * v + (1.0 - ADAM_B2) * (g * g)
    m_hat = m / (1.0 - ADAM_B1 ** ADAM_STEP)
    v_hat = v / (1.0 - ADAM_B2 ** ADAM_STEP)
    delta = -ADAM_LR * (m_hat / (jnp.sqrt(v_hat) + ADAM_EPS) + ADAM_WD * w)
    return g, delta, m, v


def adamw(gfull, t, w, m, v, *, name):
    off, K, ns, _ = t
    sb = off // K
    nat = pl.BlockSpec((K, SLAB), lambda q: (0, q))

    def body(g_ref, w_ref, m_ref, v_ref, *outs):
        for o, r in zip(outs, _adamw_tile(w_ref[...], g_ref[...], m_ref[...], v_ref[...])):
            o[...] = r

    return pl.pallas_call(
        body, name=name, grid=(ns,),
        in_specs=[pl.BlockSpec((K, SLAB), lambda q: (sb + q, 0)), nat, nat, nat],
        out_specs=[nat] * 4, out_shape=[jax.ShapeDtypeStruct(w.shape, F32)] * 4,
        compiler_params=_cparams(("parallel",)),
    )(gfull, w, m, v)


def adamw_small(g, w, m, v, *, name):
    def body(g_ref, w_ref, m_ref, v_ref, *outs):
        for o, r in zip(outs, _adamw_tile(w_ref[...], g_ref[...], m_ref[...], v_ref[...])):
            o[...] = r

    return pl.pallas_call(body, name=name, out_shape=[jax.ShapeDtypeStruct(w.shape, F32)] * 4)(g, w, m, v)


MESH = pl.DeviceIdType.MESH
HBM = pl.BlockSpec(memory_space=pl.ANY)


def _place():
    x, y, c = lax.axis_index("x"), lax.axis_index("y"), lax.axis_index("c")
    others = [(1 - x, y), (x, 1 - y), (1 - x, 1 - y)]
    return x, y, c, others


def _rcopy(src, dst, send_sems, recv_sems, k, to):
    return pltpu.make_async_remote_copy(src_ref=src, dst_ref=dst, send_sem=send_sems.at[k], recv_sem=recv_sems.at[k],
                                        device_id=to, device_id_type=MESH)


def _dma_sems(*counts):
    return [pltpu.SemaphoreType.DMA((n,)) for n in counts]


DMA_UNIT_ROWS = 128
DMA_PIECES = 4
DMA_PIECES_LOCAL = 8


def _row_pieces(h, n):
    units = h // DMA_UNIT_ROWS
    n = min(n, units)
    base, extra = divmod(units, n)
    sizes = [(base + (k < extra)) * DMA_UNIT_ROWS for k in range(n)]
    return [(sum(sizes[:k]), sizes[k]) for k in range(n)]


def _start_pieces(make, h, n):
    for s0, sz in _row_pieces(h, n):
        make(s0, sz).start()
    return make(0, h)


SEM = pl.BlockSpec(memory_space=pltpu.SEMAPHORE)
SPLIT_COPY = pltpu.CompilerParams(has_side_effects=pltpu.SideEffectType.DATAFLOW_SIDE_EFFECTING)
N_OTHERS = 3


def _hbm(a):
    return pltpu.with_memory_space_constraint(a, pltpu.HBM)


def _chip_rows(buf, chip, s0, sz):
    return buf.at[2 * chip[0] + chip[1], pl.ds(s0, sz)]


def allgather_start(bufs, *, name):
    n = len(bufs)

    def body(*refs):
        ins, send, recv, token = refs[:n], refs[n:2 * n], refs[2 * n:3 * n], refs[4 * n]
        x, y, c, others = _place()
        for i in range(n):
            h = bufs[i].shape[1] // 2
            for f, chip in enumerate(others):
                for s0, sz in _row_pieces(h, DMA_PIECES):
                    mine = _chip_rows(ins[i], (x, y), c * h + s0, sz)
                    _rcopy(mine, mine, send[i], recv[i], f, (*chip, c)).start()
        token[...] = jnp.zeros_like(token)

    res = pl.pallas_call(
        body, name=name, in_specs=[HBM] * n,
        out_specs=[SEM] * (2 * n) + [HBM] * n + [pl.BlockSpec(memory_space=pltpu.VMEM)],
        out_shape=[pltpu.SemaphoreType.DMA((N_OTHERS,))] * (2 * n) + [pltpu.HBM(b.shape, b.dtype) for b in bufs]
        + [jax.ShapeDtypeStruct((1, D_MODEL), F32)],
        input_output_aliases={i: 2 * n + i for i in range(n)},
        compiler_params=SPLIT_COPY,
    )(*[_hbm(b) for b in bufs])
    return res[:n], res[n:2 * n], res[2 * n:3 * n], res[3 * n]


def allgather_wait(buf, send, recv, after, *, name):
    h = buf.shape[1] // 2

    def body(buf_ref, send_sems, recv_sems, after_ref, out_ref):
        del after_ref, out_ref
        x, y, c, others = _place()
        for f, chip in enumerate(others):
            mine = _chip_rows(buf_ref, (x, y), c * h, h)
            theirs = _chip_rows(buf_ref, chip, c * h, h)
            cp = _rcopy(mine, theirs, send_sems, recv_sems, f, (*chip, c))
            cp.wait_send()
            cp.wait_recv()

    return pl.pallas_call(
        body, name=name, in_specs=[HBM, SEM, SEM, HBM], out_specs=HBM,
        out_shape=pltpu.HBM(buf.shape, buf.dtype), input_output_aliases={0: 0},
        compiler_params=SPLIT_COPY,
    )(buf, send, recv, after)


def allgather_forward(buf, *, name):
    h = buf.shape[1] // 2

    def body(in_ref, out_ref, send_sems, recv_sems):
        del in_ref
        x, y, c, others = _place()
        sibling = (x, y, 1 - c)
        sent = []
        for f, chip in enumerate(others):
            sent.append(_start_pieces(
                lambda s0, sz: _rcopy(_chip_rows(out_ref, chip, c * h + s0, sz), _chip_rows(out_ref, chip, c * h + s0, sz),
                                      send_sems, recv_sems, f, sibling), h, DMA_PIECES))
        for f, chip in enumerate(others):
            blk = _chip_rows(out_ref, chip, (1 - c) * h, h)
            _rcopy(blk, blk, send_sems, recv_sems, f, sibling).wait_recv()
        for cp in sent:
            cp.wait_send()

    return pl.pallas_call(
        body, name=name, in_specs=[HBM], out_specs=HBM,
        out_shape=jax.ShapeDtypeStruct(buf.shape, buf.dtype), input_output_aliases={0: 0},
        scratch_shapes=_dma_sems(N_OTHERS, N_OTHERS),
    )(buf)


def swap_halves(grads, *, name):
    n = len(grads)

    def body(*refs):
        ins, theirs = refs[:n], refs[n:2 * n]
        send_sems, recv_sems = refs[2 * n:]
        x, y, c, _ = _place()
        for i in range(n):
            h = grads[i].shape[1] // 2
            for k in range(N_CHIPS):
                for s0, sz in _row_pieces(h, DMA_PIECES):
                    _rcopy(ins[i].at[k, pl.ds((1 - c) * h + s0, sz)], theirs[i].at[k, pl.ds(s0, sz)],
                           send_sems, recv_sems, i, (x, y, 1 - c)).start()
        for i in range(n):
            h = grads[i].shape[1] // 2
            _rcopy(ins[i].at[:, pl.ds((1 - c) * h, h)], theirs[i], send_sems, recv_sems, i, (x, y, 1 - c)).wait()

    return pl.pallas_call(
        body, name=name, in_specs=[HBM] * n, out_specs=[HBM] * n,
        out_shape=[jax.ShapeDtypeStruct((N_CHIPS, g.shape[1] // 2, SLAB), g.dtype) for g in grads],
        scratch_shapes=_dma_sems(n, n))(*grads)


def scatter_start(part, *, name):
    h = part.shape[1]

    def body(part_ref, land_ref, send, recv, part_out, land_out, token):
        del part_out, land_out
        x, y, c, others = _place()
        for f, chip in enumerate(others):
            for s0, sz in _row_pieces(h, DMA_PIECES):
                _rcopy(_chip_rows(part_ref, chip, s0, sz), land_ref.at[f, pl.ds(s0, sz)], send, recv, f, (*chip, c)).start()
        token[...] = jnp.zeros_like(token)

    land = lax.empty((N_OTHERS,) + part.shape[1:], part.dtype)
    return pl.pallas_call(
        body, name=name, in_specs=[HBM, HBM],
        out_specs=[SEM, SEM, HBM, HBM, pl.BlockSpec(memory_space=pltpu.VMEM)],
        out_shape=[pltpu.SemaphoreType.DMA((N_OTHERS,))] * 2 + [pltpu.HBM(part.shape, part.dtype), pltpu.HBM(land.shape, land.dtype),
                                                                 jax.ShapeDtypeStruct((1, D_MODEL), F32)],
        input_output_aliases={0: 2, 1: 3},
        compiler_params=SPLIT_COPY,
    )(_hbm(part), _hbm(land))


def scatter_wait(part, land, send, recv, after, *, name):
    h = part.shape[1]

    def body(part_ref, land_ref, send_sems, recv_sems, after_ref, part_out, land_out):
        del after_ref, part_out, land_out
        x, y, c, others = _place()
        for f, chip in enumerate(others):
            cp = _rcopy(_chip_rows(part_ref, chip, 0, h), land_ref.at[f], send_sems, recv_sems, f, (*chip, c))
            cp.wait_send()
            cp.wait_recv()

    return pl.pallas_call(
        body, name=name, in_specs=[HBM, HBM, SEM, SEM, HBM], out_specs=[HBM, HBM],
        out_shape=[pltpu.HBM(part.shape, part.dtype), pltpu.HBM(land.shape, land.dtype)],
        input_output_aliases={0: 0, 1: 1},
        compiler_params=SPLIT_COPY,
    )(part, land, send, recv, after)


def join_halves(bufs, *, name):
    n = len(bufs)

    def body(*refs):
        outs = refs[n:2 * n]
        send_sems, recv_sems = refs[2 * n:]
        x, y, c, _ = _place()
        sibling = (x, y, 1 - c)
        cps = []
        for i in range(n):
            h = bufs[i].shape[0] // 2
            snd = _start_pieces(
                lambda s0, sz: _rcopy(outs[i].at[pl.ds(c * h + s0, sz)], outs[i].at[pl.ds(c * h + s0, sz)],
                                      send_sems, recv_sems, i, sibling), h, 2 * DMA_PIECES_LOCAL)
            theirs = outs[i].at[pl.ds((1 - c) * h, h)]
            cps.append((snd, _rcopy(theirs, theirs, send_sems, recv_sems, i, sibling)))
        for snd, rcv in cps:
            snd.wait_send()
            rcv.wait_recv()

    return pl.pallas_call(
        body, name=name, in_specs=[HBM] * n, out_specs=[HBM] * n,
        out_shape=[jax.ShapeDtypeStruct(b.shape, b.dtype) for b in bufs],
        input_output_aliases={i: i for i in range(n)},
        scratch_shapes=_dma_sems(n, n),
    )(*bufs)


def allreduce_small(v, *, name):
    rows, n = v.shape

    def body(x_ref, sum_ref, all_ref, send_sems, recv_sems, local_sem):
        x, y, c, others = _place()
        me, sibling = (x, y, c), (x, y, 1 - c)

        def blk(px, py, pc):
            return all_ref.at[pl.ds((4 * px + 2 * py + pc) * rows, rows), :]

        def copy(k, block, to, src=None):
            return _rcopy(blk(*block) if src is None else src, blk(*block), send_sems, recv_sems, k, to)

        mine = pltpu.make_async_copy(x_ref, blk(*me), local_sem)
        mine.start()
        first = [copy(0, me, sibling, src=x_ref)]
        first += [copy(1 + f, me, (*chip, c), src=x_ref) for f, chip in enumerate(others)]
        for cp in first:
            cp.start()
        passed = [copy(4 + f, (*chip, c), sibling) for f, chip in enumerate(others)]
        for f, chip in enumerate(others):
            copy(1 + f, (*chip, c), me).wait_recv()
            passed[f].start()
        copy(0, sibling, me).wait_recv()
        for f, chip in enumerate(others):
            copy(4 + f, (*chip, 1 - c), me).wait_recv()
        for cp in first + passed:
            cp.wait_send()
        mine.wait()
        acc = all_ref[pl.ds(0, rows), :]
        for d in range(1, N_DEVICES):
            acc = acc + all_ref[pl.ds(d * rows, rows), :]
        sum_ref[...] = acc

    vm = pl.BlockSpec(memory_space=pltpu.VMEM)
    return pl.pallas_call(
        body, name=name, in_specs=[vm], out_specs=[vm, vm],
        out_shape=[jax.ShapeDtypeStruct((rows, n), F32), jax.ShapeDtypeStruct((N_DEVICES * rows, n), F32)],
        scratch_shapes=_dma_sems(7, 7) + [pltpu.SemaphoreType.DMA],
    )(v)[0]


def add_pairs(grad, theirs, where, *, name):
    h = theirs.shape[1]
    spec = pl.BlockSpec((None, h, SLAB), lambda k, w: (k, 0, 0))

    def body(w_ref, a_ref, b_ref, o_ref):
        del w_ref
        o_ref[...] = (a_ref[...].astype(F32) + b_ref[...].astype(F32)).astype(o_ref.dtype)

    return pl.pallas_call(
        body, name=name,
        grid_spec=pltpu.PrefetchScalarGridSpec(
            num_scalar_prefetch=1, grid=(N_CHIPS,),
            in_specs=[pl.BlockSpec((None, h, SLAB), lambda k, w: (k, w[1], 0)), spec], out_specs=spec),
        out_shape=jax.ShapeDtypeStruct(theirs.shape, theirs.dtype),
        compiler_params=_cparams(("parallel",)))(where, grad, theirs)


def add_chips(pair, got, where, *, name):
    h = pair.shape[1]
    tr = h // 2

    def body(w_ref, a_ref, b_ref, o_ref):
        del w_ref
        acc = a_ref[...].astype(F32)
        for f in range(3):
            acc = acc + b_ref[f].astype(F32)
        o_ref[...] = acc

    return pl.pallas_call(
        body, name=name,
        grid_spec=pltpu.PrefetchScalarGridSpec(
            num_scalar_prefetch=1, grid=(2,),
            in_specs=[pl.BlockSpec((None, tr, SLAB), lambda i, w: (w[0], i, 0)),
                      pl.BlockSpec((3, tr, SLAB), lambda i, w: (0, i, 0))],
            out_specs=pl.BlockSpec((tr, SLAB), lambda i, w: (2 * w[1] + i, 0))),
        out_shape=jax.ShapeDtypeStruct((2 * h, SLAB), F32),
        compiler_params=_cparams(("parallel",)))(where, pair, got)


DEPTH = 4
MIXER = (0, 1, 2, 0)
W_IN_COLS = (768, 320, 772)
W_IN_PAD = (768, 512, 1024)
MATS = ("up", "down", "inp", "out", "gate", "proj")
MAT_ARG = dict(up="w_up", down="w_down", inp="w_in", out="w_out", gate="w_ple_gate", proj="w_ple_proj")
GAINS = ("attn_norm", "mlp_norm", "ple_norm")
N_SMALL = 16
KINDS = ("grad_", "delta_", "new_m_", "new_v_")


def _layout(kind):
    ns_in = W_IN_PAD[kind] // SLAB
    off = 8192 + 1024 * ns_in
    lay = dict(up=(0, 1024, 4, False), down=(4096, 1024, 4, True), inp=(8192, 1024, ns_in, False),
               out=(off, 256, 4, True), gate=(off + 1024, 256, 4, True), proj=(off + 2048, 256, 1, False))
    return lay, off + 2304


def _to_slabs(w):
    k, c = w.shape
    return w.reshape(k, c // SLAB, SLAB).transpose(1, 0, 2).reshape(-1, SLAB)


def _pad_cols(w, n):
    return jnp.pad(w, ((0, 0), (0, n - w.shape[1])))


def _heads(x2d, n):
    return x2d.reshape(x2d.shape[0], n, HEAD_DIM).transpose(1, 0, 2)


def _unheads(x3d):
    n, s, _ = x3d.shape
    return x3d.transpose(1, 0, 2).reshape(s, n * HEAD_DIM)


def _chip_cols(x2d, c, cpad):
    s = x2d.shape[0]
    return jnp.pad(x2d.reshape(s, N_CHIPS, c), ((0, 0), (0, 0), (0, cpad - c))).reshape(s, N_CHIPS * cpad)


def _unchip_cols(x2d, c, cpad):
    s = x2d.shape[0]
    return x2d.reshape(s, N_CHIPS, cpad)[:, :, :c].reshape(s, N_CHIPS * c)


def _add_res(acc, res):
    return (acc + res,)


def _relu2(acc):
    return acc, jnp.square(jnp.maximum(acc, 0.0))


def _relu2_bwd(acc, u):
    return (acc * (2.0 * jnp.maximum(u.astype(F32), 0.0)),)


def _ple_fwd(acc, x2, pp):
    return x2 + pp * _sigmoid(acc), acc


def _ple_bwd(dx, pp, gl):
    gate = _sigmoid(gl)
    return dx * gate, dx * pp * gate * (1.0 - gate)


def _layer_fwd(i, kind, x0, p_bf, wg, lay, gains, extra, tabs):
    s = x0.shape[0]
    an, mn, pn = gains
    sv = dict(x0=x0)
    h1 = rms_fwd(x0, an, name=f"attn_norm_{i}")
    if kind == 0:
        proj = mm_nn(h1, wg, lay["inp"], name=f"w_in_{i}")[0]
        qkv = proj.reshape(s, 3, N_HEADS, HEAD_DIM).transpose(1, 2, 0, 3)
        o, tot = sb_fwd(qkv[0], qkv[1], qkv[2], name=f"sb_fwd_{i}")
        sv.update(qkv=qkv, tot=tot)
    elif kind == 1:
        projp = mm_nn(h1, wg, lay["inp"], name=f"w_in_{i}", out_dtypes=(F32,))[0]
        proj = _unchip_cols(projp, W_IN_COLS[1], W_IN_PAD[1])
        nq = N_HEADS * HEAD_DIM
        nqk = nq + SWA_KV_HEADS * HEAD_DIM
        qk = rope_fwd(proj[:, :nqk], tabs, name=f"rope_{i}")
        q = _heads(qk[:, :nq], N_HEADS).reshape(SWA_KV_HEADS, SWA_GROUP, s, HEAD_DIM)
        front = ((0, 0), (SWA_WINDOW, 0), (0, 0))
        kp = jnp.pad(_heads(qk[:, nq:], SWA_KV_HEADS), front)
        vp = jnp.pad(_heads(proj[:, nqk:].astype(BF16), SWA_KV_HEADS), front)
        sink = jnp.repeat(extra.reshape(SWA_KV_HEADS, SWA_GROUP), ATT_BLK, axis=1)[:, :, None]
        o4, lse = swa_fwd(q, kp, vp, sink, name=f"swa_fwd_{i}")
        o = o4.reshape(N_HEADS, s, HEAD_DIM)
        sv.update(q=q, kp=kp, vp=vp, sink=sink, o4=o4, lse=lse)
    else:
        projp = mm_nn(h1, wg, lay["inp"], name=f"w_in_{i}", out_dtypes=(F32,))[0]
        proj = _unchip_cols(projp, W_IN_COLS[2], W_IN_PAD[2])
        nqkv = 3 * N_HEADS * HEAD_DIM
        qkv = proj[:, :nqkv].astype(BF16).reshape(s, 3, N_HEADS, HEAD_DIM).transpose(1, 2, 0, 3)
        fl = _pad_cols(proj[:, nqkv:], 128)
        bp = _pad_cols(extra[None], 128)
        cum_t = fox_gate_fwd(fl, bp, name=f"gate_fwd_{i}")[:, :N_HEADS].T
        cq = cum_t[:, :, None]
        ck = cum_t.reshape(N_HEADS, s // min(ATT_BK, s), 1, min(ATT_BK, s))
        o, lse = fox_fwd(qkv[0], qkv[1], qkv[2], cq, ck, name=f"fox_fwd_{i}")
        sv.update(qkv=qkv, fl=fl, bp=bp, cq=cq, ck=ck, o=o, lse=lse)
    a = _unheads(o)
    x1 = mm_nn(a, wg, lay["out"], name=f"w_out_{i}", epi=_add_res, extras=(x0,), out_dtypes=(F32,))[0]
    h2 = rms_fwd(x1, mn, name=f"mlp_norm_{i}")
    u, r = mm_nn(h2, wg, lay["up"], name=f"w_up_{i}", epi=_relu2, out_dtypes=(BF16, BF16))
    x2 = mm_nn(r, wg, lay["down"], name=f"w_down_{i}", epi=_add_res, extras=(x1,), out_dtypes=(F32,))[0]
    h3 = rms_fwd(x2, pn, name=f"ple_norm_{i}")
    pp = mm_nn(p_bf, wg, lay["proj"], name=f"w_ple_proj_{i}", out_dtypes=(F32,))[0]
    x3, gl = mm_nn(h3, wg, lay["gate"], name=f"w_ple_gate_{i}", epi=_ple_fwd, extras=(x2, pp), out_dtypes=(F32, F32))
    sv.update(h1=h1, a=a, x1=x1, h2=h2, u=u, r=r, x2=x2, h3=h3, pp=pp, gl=gl)
    return x3, sv


def _layer_bwd(i, kind, dx3, sv, p_bf, wg, lay, n_rows, gains, tabs):
    s = dx3.shape[0]
    an, mn, pn = gains
    g = lax.empty((N_CHIPS, n_rows, SLAB), BF16)
    d_pp, d_gl = ew(_ple_bwd, [dx3, sv["pp"], sv["gl"]], [BF16, BF16], name=f"ple_bwd_{i}")
    g = mm_tn(p_bf, d_pp, g, lay["proj"], name=f"dw_ple_proj_{i}")
    g = mm_tn(sv["h3"], d_gl, g, lay["gate"], name=f"dw_ple_gate_{i}")
    d_h3 = mm_nt(d_gl, wg, lay["gate"], name=f"dx_ple_gate_{i}", out_dtypes=(F32,))[0]
    dx2, dx2b, d_pn = rms_bwd(sv["x2"], pn, d_h3, dx3, name=f"ple_norm_bwd_{i}")
    g = mm_tn(sv["r"], dx2b, g, lay["down"], name=f"dw_down_{i}")
    d_u = mm_nt(dx2b, wg, lay["down"], name=f"dx_down_{i}", epi=_relu2_bwd, extras=(sv["u"],))[0]
    g = mm_tn(sv["h2"], d_u, g, lay["up"], name=f"dw_up_{i}")
    d_h2 = mm_nt(d_u, wg, lay["up"], name=f"dx_up_{i}", out_dtypes=(F32,))[0]
    dx1, dx1b, d_mn = rms_bwd(sv["x1"], mn, d_h2, dx2, name=f"mlp_norm_bwd_{i}")
    g = mm_tn(sv["a"], dx1b, g, lay["out"], name=f"dw_out_{i}")
    d_a = mm_nt(dx1b, wg, lay["out"], name=f"dx_out_{i}")[0]
    do = _heads(d_a, N_HEADS)
    d_extra = None
    if kind == 0:
        qkv = sv["qkv"]
        dq, dk, dv = sb_bwd(qkv[0], qkv[1], qkv[2], sv["tot"], do, name=f"sb_bwd_{i}")
        d_proj = jnp.stack([dq, dk, dv]).transpose(2, 0, 1, 3).reshape(s, 3 * N_HEADS * HEAD_DIM).astype(BF16)
    elif kind == 1:
        do4 = do.reshape(SWA_KV_HEADS, SWA_GROUP, s, HEAD_DIM)
        dq, dkp, dvp, dsr = swa_bwd(sv["q"], sv["kp"], sv["vp"], sv["sink"], sv["o4"], sv["lse"], do4, name=f"swa_bwd_{i}")
        dqk = jnp.concatenate([_unheads(dq.reshape(N_HEADS, s, HEAD_DIM)), _unheads(dkp[:, SWA_WINDOW:])], axis=1)
        dqk = rope_bwd(dqk, tabs, name=f"rope_bwd_{i}")
        d_proj = jnp.concatenate([dqk, _unheads(dvp[:, SWA_WINDOW:]).astype(BF16)], axis=1)
        d_proj = _chip_cols(d_proj, W_IN_COLS[1], W_IN_PAD[1])
        d_extra = jnp.sum(dsr[..., 0], axis=2).reshape(N_HEADS)
    else:
        qkv = sv["qkv"]
        dq, dk, dv, dcq, dck = fox_bwd(qkv[0], qkv[1], qkv[2], sv["o"], sv["lse"], sv["cq"], sv["ck"], do, name=f"fox_bwd_{i}")
        dcum = _pad_cols((dcq[:, :, 0] - dck.reshape(N_HEADS, s)).T, 128)
        dfl, dbp = fox_gate_bwd(dcum, sv["fl"], sv["bp"], name=f"gate_bwd_{i}")
        d_qkv = jnp.stack([dq, dk, dv]).transpose(2, 0, 1, 3).reshape(s, 3 * N_HEADS * HEAD_DIM)
        d_proj = jnp.concatenate([d_qkv, dfl[:, :N_HEADS]], axis=1).astype(BF16)
        d_proj = _chip_cols(d_proj, W_IN_COLS[2], W_IN_PAD[2])
        d_extra = dbp[0, :N_HEADS]
    g = mm_tn(sv["h1"], d_proj, g, lay["inp"], name=f"dw_in_{i}")
    d_h1 = mm_nt(d_proj, wg, lay["inp"], name=f"dx_in_{i}", out_dtypes=(F32,))[0]
    dx0, _, d_an = rms_bwd(sv["x0"], an, d_h1, dx1, name=f"attn_norm_bwd_{i}")
    return dx0, g, (d_an, d_mn, d_pn), d_extra


def _small_rows(a, prefix):
    rows = [a[f"{prefix}{n}_{i}"] for i in range(DEPTH) for n in GAINS] + [a[f"{prefix}final_norm"]]
    rows += [_pad_cols(a[f"{prefix}{n}"][None], D_MODEL)[0] for n in ("sinks_1", "b_forget_2")]
    return jnp.stack(rows + [jnp.zeros((D_MODEL,), F32)])


def _train_step(a):
    x = a["x"][0]
    tabs = rope_tables(x.shape[0], (N_HEADS + SWA_KV_HEADS) * HEAD_DIM)
    lays = [_layout(k) for k in MIXER]

    def natural(prefix, i, m):
        w = a[f"{prefix}{MAT_ARG[m]}_{i}"]
        return _pad_cols(w, W_IN_PAD[MIXER[i]]) if m == "inp" else w

    chip = 2 * lax.axis_index("x") + lax.axis_index("y")
    where = jnp.stack([chip, lax.axis_index("c")]).astype(jnp.int32)
    packed = [jnp.concatenate([_to_slabs(natural("", i, m).astype(BF16)) for m in MATS], axis=0) for i in range(DEPTH)]
    sends, recvs, bufs, token = allgather_start(
        [lax.dynamic_update_slice(lax.empty((N_CHIPS,) + pk.shape, BF16), pk[None], (chip, 0, 0)) for pk in packed],
        name="allgather_start")

    gains = [tuple(a[f"{n}_{i}"][None] for n in GAINS) for i in range(DEPTH)]
    extras = [None, a["sinks_1"], a["b_forget_2"], None]
    p_bf = [a["p"][i, 0].astype(BF16) for i in range(DEPTH)]

    saved, wgs, after = [], [], token
    for i in range(DEPTH):
        landed = allgather_wait(bufs[i], sends[i], recvs[i], after, name=f"allgather_wait_{i}")
        wgs.append(allgather_forward(landed, name=f"allgather_forward_{i}"))
        x, sv = _layer_fwd(i, MIXER[i], x, p_bf[i], wgs[i], lays[i][0], gains[i], extras[i], tabs)
        saved.append(sv)
        after = x
    dx, d_final, loss = loss_head(x, a["final_norm"][None], a["loss_target"][0], name="loss_head")

    def finish(i, started, after):
        send, recv, part, land, _ = started
        part, got = scatter_wait(part, land, send, recv, after, name=f"scatter_wait_{i}")
        return join_halves([add_chips(part, got, where, name=f"add_chips_{i}")], name=f"join_halves_{i}")[0]

    small = [None] * N_SMALL
    small[12] = d_final[0]
    small[15] = _pad_cols(loss[:, :1], D_MODEL)[0]
    gfull = [None] * DEPTH
    started = None
    for i in reversed(range(DEPTH)):
        an, mn, pn = gains[i]
        if started is not None:
            pn = pn + started[4]
        dx, grad, d_gains, d_extra = _layer_bwd(i, MIXER[i], dx, saved[i], p_bf[i], wgs[i], lays[i][0], lays[i][1],
                                                (an, mn, pn), tabs)
        for j in range(3):
            small[3 * i + j] = d_gains[j][0]
        if d_extra is not None:
            small[12 + MIXER[i]] = _pad_cols(d_extra[None], D_MODEL)[0]
        if started is not None:
            gfull[i + 1] = finish(i + 1, started, dx)
        theirs = swap_halves([grad], name=f"swap_halves_{i}")[0]
        started = scatter_start(add_pairs(grad, theirs, where, name=f"add_pairs_{i}"), name=f"scatter_start_{i}")
    small = allreduce_small(jnp.stack(small), name="allreduce_small")
    gfull[0] = finish(0, started, small)

    out = {"loss": small[15, 0], "grad_x": dx[None]}
    for i in range(DEPTH):
        for m in MATS:
            res = adamw(gfull[i], lays[i][0][m], natural("", i, m), natural("m_", i, m), natural("v_", i, m),
                        name=f"adamw_{MAT_ARG[m]}_{i}")
            cols = a[f"{MAT_ARG[m]}_{i}"].shape[1]
            for kd, r in zip(KINDS, res):
                out[f"{kd}{MAT_ARG[m]}_{i}"] = r[:, :cols]
    res = adamw_small(small, _small_rows(a, ""), _small_rows(a, "m_"), _small_rows(a, "v_"), name="adamw_small")
    for kd, r in zip(KINDS, res):
        for i in range(DEPTH):
            for j, n in enumerate(GAINS):
                out[f"{kd}{n}_{i}"] = r[3 * i + j]
        out[f"{kd}final_norm"] = r[12]
        out[f"{kd}sinks_1"] = r[13, :N_HEADS]
        out[f"{kd}b_forget_2"] = r[14, :N_HEADS]
    return out


def _weight_names():
    names = []
    for i in range(DEPTH):
        names += [f"attn_norm_{i}", f"w_in_{i}", f"w_out_{i}"] + [[], ["sinks_1"], ["b_forget_2"]][MIXER[i]]
        names += [f"mlp_norm_{i}", f"w_up_{i}", f"w_down_{i}", f"ple_norm_{i}", f"w_ple_gate_{i}", f"w_ple_proj_{i}"]
    return names + ["final_norm"]


def kernel(x, p, attn_norm_0, w_in_0, w_out_0, mlp_norm_0, w_up_0, w_down_0, ple_norm_0, w_ple_gate_0, w_ple_proj_0, attn_norm_1, w_in_1, w_out_1, sinks_1, mlp_norm_1, w_up_1, w_down_1, ple_norm_1, w_ple_gate_1, w_ple_proj_1, attn_norm_2, w_in_2, w_out_2, b_forget_2, mlp_norm_2, w_up_2, w_down_2, ple_norm_2, w_ple_gate_2, w_ple_proj_2, attn_norm_3, w_in_3, w_out_3, mlp_norm_3, w_up_3, w_down_3, ple_norm_3, w_ple_gate_3, w_ple_proj_3, final_norm, loss_target, m_attn_norm_0, m_w_in_0, m_w_out_0, m_mlp_norm_0, m_w_up_0, m_w_down_0, m_ple_norm_0, m_w_ple_gate_0, m_w_ple_proj_0, m_attn_norm_1, m_w_in_1, m_w_out_1, m_sinks_1, m_mlp_norm_1, m_w_up_1, m_w_down_1, m_ple_norm_1, m_w_ple_gate_1, m_w_ple_proj_1, m_attn_norm_2, m_w_in_2, m_w_out_2, m_b_forget_2, m_mlp_norm_2, m_w_up_2, m_w_down_2, m_ple_norm_2, m_w_ple_gate_2, m_w_ple_proj_2, m_attn_norm_3, m_w_in_3, m_w_out_3, m_mlp_norm_3, m_w_up_3, m_w_down_3, m_ple_norm_3, m_w_ple_gate_3, m_w_ple_proj_3, m_final_norm, v_attn_norm_0, v_w_in_0, v_w_out_0, v_mlp_norm_0, v_w_up_0, v_w_down_0, v_ple_norm_0, v_w_ple_gate_0, v_w_ple_proj_0, v_attn_norm_1, v_w_in_1, v_w_out_1, v_sinks_1, v_mlp_norm_1, v_w_up_1, v_w_down_1, v_ple_norm_1, v_w_ple_gate_1, v_w_ple_proj_1, v_attn_norm_2, v_w_in_2, v_w_out_2, v_b_forget_2, v_mlp_norm_2, v_w_up_2, v_w_down_2, v_ple_norm_2, v_w_ple_gate_2, v_w_ple_proj_2, v_attn_norm_3, v_w_in_3, v_w_out_3, v_mlp_norm_3, v_w_up_3, v_w_down_3, v_ple_norm_3, v_w_ple_gate_3, v_w_ple_proj_3, v_final_norm):
    out = _train_step(dict(locals()))
    return (out["loss"], out["grad_x"], *[out[kd + n] for kd in KINDS for n in _weight_names()])
```

```python
import jax
import jax.numpy as jnp
from jax import lax
from jax.experimental import pallas as pl
from jax.experimental.pallas import tpu as pltpu

F32 = jnp.float32
BF16 = jnp.bfloat16

D_MODEL = 1024
N_HEADS = 16
HEAD_DIM = 64
SWA_KV_HEADS = 2
SWA_GROUP = 8
SWA_WINDOW = 128
ROPE_THETA = 500000.0
ROPE_DIM = 16
RMS_EPS = 1e-6
NEG_INF = -1e30
ATTN_SCALE = HEAD_DIM ** -0.5
N_CHIPS = 4
N_DEVICES = 8

SLAB = 256
ATT_BLK = 128
ATT_BQ = 512
ATT_BK = 512
ROW_TILE = 256
V7X_VMEM_LIMIT = 56 * 1024 * 1024

ADAM_LR, ADAM_B1, ADAM_B2, ADAM_EPS, ADAM_WD, ADAM_STEP = 0.001, 0.9, 0.999, 1e-08, 0.01, 10


def _cparams(sem=None):
    return pltpu.CompilerParams(dimension_semantics=sem, vmem_limit_bytes=V7X_VMEM_LIMIT)


def _dot(a, b):
    return jnp.dot(a, b, preferred_element_type=F32)


def _dot_nt(a, b):
    return lax.dot_general(a, b, (((1,), (1,)), ((), ())), preferred_element_type=F32)


def _dot_tn(a, b):
    return lax.dot_general(a, b, (((0,), (0,)), ((), ())), preferred_element_type=F32)


def _row_tile(M, K):
    return min(M, 1024) if K >= 1024 else M


def _finish(epi, acc, ex, outs):
    res = epi(acc, *[e[...] for e in ex]) if epi is not None else (acc,)
    for o, r in zip(outs, res):
        o[...] = r.astype(o.dtype)


def mm_nn(a, wg, t, *, name, epi=None, extras=(), out_dtypes=(BF16,)):
    off, K, ns, row = t
    M = a.shape[0]
    sb = off // K
    ne, no = len(extras), len(out_dtypes)
    if row:
        tm = _row_tile(M, K)
        nb = N_CHIPS
        grid = (M // tm, ns)
        a_spec = pl.BlockSpec((tm, N_CHIPS * K), lambda i, q: (i, 0))
        b_specs = [pl.BlockSpec((None, K, SLAB), lambda i, q, j=j: (j, sb + q, 0)) for j in range(nb)]
        tile = pl.BlockSpec((tm, SLAB), lambda i, q: (i, q))
        n_out = ns * SLAB
    else:
        nb = 1
        grid = (N_CHIPS, ns)
        a_spec = pl.BlockSpec((M, K), lambda j, q: (0, 0))
        b_specs = [pl.BlockSpec((None, K, SLAB), lambda j, q: (j, sb + q, 0))]
        tile = pl.BlockSpec((M, SLAB), lambda j, q: (0, j * ns + q))
        n_out = N_CHIPS * ns * SLAB

    def body(a_ref, *rest):
        bs, ex, outs = rest[:nb], rest[nb:nb + ne], rest[nb + ne:]
        acc = _dot(a_ref[:, pl.ds(0, K)], bs[0][...])
        for j in range(1, nb):
            acc = acc + _dot(a_ref[:, pl.ds(j * K, K)], bs[j][...])
        _finish(epi, acc, ex, outs)

    return pl.pallas_call(
        body, name=name, grid=grid,
        in_specs=[a_spec] + b_specs + [tile] * ne, out_specs=[tile] * no,
        out_shape=[jax.ShapeDtypeStruct((M, n_out), d) for d in out_dtypes],
        compiler_params=_cparams(("parallel", "parallel")),
    )(a, *([wg] * nb), *extras)


def mm_nt(dy, wg, t, *, name, epi=None, extras=(), out_dtypes=(BF16,)):
    off, K, ns, row = t
    M = dy.shape[0]
    tm = _row_tile(M, K)
    sb = off // K
    ne, no = len(extras), len(out_dtypes)
    grid = (M // tm, N_CHIPS)
    b_specs = [pl.BlockSpec((None, K, SLAB), lambda i, j, q=q: (j, sb + q, 0)) for q in range(ns)]
    if row:
        dy_spec = pl.BlockSpec((tm, ns * SLAB), lambda i, j: (i, 0))
        tile = pl.BlockSpec((tm, K), lambda i, j: (i, j))
        n_out = N_CHIPS * K
        sem = ("parallel", "parallel")
    else:
        dy_spec = pl.BlockSpec((tm, ns * SLAB), lambda i, j: (i, j))
        tile = pl.BlockSpec((tm, K), lambda i, j: (i, 0))
        n_out = K
        sem = ("parallel", "arbitrary")

    def body(dy_ref, *rest):
        bs, ex, outs = rest[:ns], rest[ns:ns + ne], rest[ns + ne:ns + ne + no]
        part = _dot_nt(dy_ref[:, pl.ds(0, SLAB)], bs[0][...])
        for q in range(1, ns):
            part = part + _dot_nt(dy_ref[:, pl.ds(q * SLAB, SLAB)], bs[q][...])
        if row:
            _finish(epi, part, ex, outs)
        else:
            acc_ref = rest[-1]
            j = pl.program_id(1)

            @pl.when(j == 0)
            def _():
                acc_ref[...] = part

            @pl.when(j > 0)
            def _():
                acc_ref[...] += part

            @pl.when(j == N_CHIPS - 1)
            def _():
                _finish(epi, acc_ref[...], ex, outs)

    return pl.pallas_call(
        body, name=name, grid=grid,
        in_specs=[dy_spec] + b_specs + [tile] * ne, out_specs=[tile] * no,
        out_shape=[jax.ShapeDtypeStruct((M, n_out), d) for d in out_dtypes],
        scratch_shapes=[] if row else [pltpu.VMEM((tm, K), F32)],
        compiler_params=_cparams(sem),
    )(dy, *([wg] * ns), *extras)


def mm_tn(x, dy, g, t, *, name):
    off, K, ns, row = t
    S = x.shape[0]
    sb = off // K
    if row:
        x_map = lambda j, q: (0, j)
        dy_map = lambda j, q: (0, q)
    else:
        x_map = lambda j, q: (0, 0)
        dy_map = lambda j, q: (0, j * ns + q)

    def body(g_in, x_ref, dy_ref, o_ref):
        del g_in
        o_ref[...] = _dot_tn(x_ref[...], dy_ref[...]).astype(o_ref.dtype)

    return pl.pallas_call(
        body, name=name, grid=(N_CHIPS, ns),
        in_specs=[pl.BlockSpec(memory_space=pl.ANY), pl.BlockSpec((S, K), x_map), pl.BlockSpec((S, SLAB), dy_map)],
        out_specs=pl.BlockSpec((None, K, SLAB), lambda j, q: (j, sb + q, 0)),
        out_shape=jax.ShapeDtypeStruct(g.shape, g.dtype),
        input_output_aliases={0: 0},
        compiler_params=_cparams(("parallel", "parallel")),
    )(g, x, dy)


def ew(fn, ins, out_dtypes, *, name, bcast=()):
    S = ins[0].shape[0]
    tr = min(ROW_TILE, S)
    cols = ins[0].shape[1]
    ni, nb = len(ins), len(bcast)

    def body(*refs):
        res = fn(*[r[...] for r in refs[:ni + nb]])
        for o, r in zip(refs[ni + nb:], res):
            o[...] = r.astype(o.dtype)

    return pl.pallas_call(
        body, name=name, grid=(S // tr,),
        in_specs=[pl.BlockSpec((tr, a.shape[1]), lambda i: (i, 0)) for a in ins]
        + [pl.BlockSpec(b.shape, lambda i: (0, 0)) for b in bcast],
        out_specs=[pl.BlockSpec((tr, cols), lambda i: (i, 0)) for _ in out_dtypes],
        out_shape=[jax.ShapeDtypeStruct((S, cols), d) for d in out_dtypes],
        compiler_params=_cparams(("parallel",)),
    )(*ins, *bcast)


def _rstd(x):
    return lax.rsqrt(jnp.mean(x * x, axis=-1, keepdims=True) + RMS_EPS)


def _sigmoid(x):
    return 1.0 / (1.0 + jnp.exp(-x))


def _log_sigmoid(z):
    return jnp.minimum(z, 0.0) - jnp.log(1.0 + jnp.exp(-jnp.abs(z)))


def rms_fwd(x, g, *, name):
    return ew(lambda xv, gv: (xv * _rstd(xv) * gv,), [x], [BF16], name=name, bcast=[g])[0]


def _rms_bwd_tile(xv, gv, dh):
    rstd = _rstd(xv)
    xhat = xv * rstd
    gd = dh * gv
    dx = rstd * (gd - xhat * jnp.mean(xhat * gd, axis=-1, keepdims=True))
    return dx, jnp.sum(dh * xhat, axis=0, keepdims=True)


def rms_bwd(x, g, dh, dres, *, name):
    S, D = x.shape
    tr = min(ROW_TILE, S)

    def body(x_ref, g_ref, dh_ref, dres_ref, dx_ref, dxb_ref, dg_ref):
        i = pl.program_id(0)
        dx, dg = _rms_bwd_tile(x_ref[...], g_ref[...], dh_ref[...])
        dx = dx + dres_ref[...]
        dx_ref[...] = dx
        dxb_ref[...] = dx.astype(BF16)

        @pl.when(i == 0)
        def _():
            dg_ref[...] = dg

        @pl.when(i > 0)
        def _():
            dg_ref[...] += dg

    row = pl.BlockSpec((tr, D), lambda i: (i, 0))
    one = pl.BlockSpec((1, D), lambda i: (0, 0))
    return pl.pallas_call(
        body, name=name, grid=(S // tr,),
        in_specs=[row, one, row, row], out_specs=[row, row, one],
        out_shape=[jax.ShapeDtypeStruct((S, D), F32), jax.ShapeDtypeStruct((S, D), BF16),
                   jax.ShapeDtypeStruct((1, D), F32)],
        compiler_params=_cparams(("arbitrary",)),
    )(x, g, dh, dres)


def loss_head(x, g, target, *, name):
    S, D = x.shape
    tr = min(ROW_TILE, S)

    def body(x_ref, g_ref, t_ref, dx_ref, dg_ref, loss_ref):
        i = pl.program_id(0)
        xv, gv = x_ref[...], g_ref[...]
        err = xv * _rstd(xv) * gv - t_ref[...]
        part = 0.5 * jnp.sum(jnp.mean(err * err, axis=-1, keepdims=True), axis=0, keepdims=True)
        dx, dg = _rms_bwd_tile(xv, gv, err * (1.0 / D))
        dx_ref[...] = dx
        part = jnp.broadcast_to(part, loss_ref.shape)

        @pl.when(i == 0)
        def _():
            dg_ref[...] = dg
            loss_ref[...] = part

        @pl.when(i > 0)
        def _():
            dg_ref[...] += dg
            loss_ref[...] += part

    row = pl.BlockSpec((tr, D), lambda i: (i, 0))
    one = pl.BlockSpec((1, D), lambda i: (0, 0))
    return pl.pallas_call(
        body, name=name, grid=(S // tr,),
        in_specs=[row, one, row], out_specs=[row, one, pl.BlockSpec((1, 128), lambda i: (0, 0))],
        out_shape=[jax.ShapeDtypeStruct((S, D), F32), jax.ShapeDtypeStruct((1, D), F32),
                   jax.ShapeDtypeStruct((1, 128), F32)],
        compiler_params=_cparams(("arbitrary",)),
    )(x, g, target)


def rope_tables(S, n_cols):
    half = ROPE_DIM // 2
    inv_freq = ROPE_THETA ** (-jnp.arange(half, dtype=F32) / half)
    ang = jnp.arange(S, dtype=F32)[:, None] * inv_freq[None, :]
    cos, sin = jnp.cos(ang), jnp.sin(ang)
    z = jnp.zeros((S, HEAD_DIM - ROPE_DIM), F32)
    zh = jnp.zeros((S, half), F32)
    c = jnp.concatenate([cos, cos, jnp.ones_like(z)], axis=1)
    sa = jnp.concatenate([zh, sin, z], axis=1)
    sb = jnp.concatenate([-sin, zh, z], axis=1)
    return [jnp.tile(t, (1, n_cols // HEAD_DIM)) for t in (c, sa, sb)]


def rope_fwd(xqk, tables, *, name):
    n, half = xqk.shape[1], ROPE_DIM // 2

    def fn(x, c, sa, sb):
        return (x * c + pltpu.roll(x, half, 1) * sa + pltpu.roll(x, n - half, 1) * sb,)

    return ew(fn, [xqk] + list(tables), [BF16], name=name)[0]


def rope_bwd(dy, tables, *, name):
    n, half = dy.shape[1], ROPE_DIM // 2

    def fn(d, c, sa, sb):
        return (d * c + pltpu.roll(d * sa, n - half, 1) + pltpu.roll(d * sb, half, 1),)

    return ew(fn, [dy] + list(tables), [BF16], name=name)[0]


def _split3(x):
    h1 = x.astype(BF16)
    r1 = x - h1.astype(F32)
    h2 = r1.astype(BF16)
    return h1, h2, (r1 - h2.astype(F32)).astype(BF16)


def _split2(x):
    h1 = x.astype(BF16)
    return h1, (x - h1.astype(F32)).astype(BF16)


def _tri(n, cmp):
    r = lax.broadcasted_iota(jnp.int32, (n, n), 0)
    c = lax.broadcasted_iota(jnp.int32, (n, n), 1)
    return cmp(r, c).astype(BF16)


def fox_gate_fwd(fl, b, *, name):
    S, W = fl.shape
    tr = min(ROW_TILE, S)

    def body(fl_ref, b_ref, cum_ref, carry):
        i = pl.program_id(0)

        @pl.when(i == 0)
        def _():
            carry[...] = jnp.zeros_like(carry)

        lower = _tri(tr, lambda r, c: r >= c)
        cs = carry[...]
        for piece in _split3(_log_sigmoid(fl_ref[...] + b_ref[...])):
            cs = cs + _dot(lower, piece)
        cum_ref[...] = cs
        carry[...] = cs[tr - 1:tr, :]

    return pl.pallas_call(
        body, name=name, grid=(S // tr,),
        in_specs=[pl.BlockSpec((tr, W), lambda i: (i, 0)), pl.BlockSpec((1, W), lambda i: (0, 0))],
        out_specs=pl.BlockSpec((tr, W), lambda i: (i, 0)),
        out_shape=jax.ShapeDtypeStruct((S, W), F32),
        scratch_shapes=[pltpu.VMEM((1, W), F32)],
        compiler_params=_cparams(("arbitrary",)),
    )(fl, b)


def fox_gate_bwd(dcum, fl, b, *, name):
    S, W = fl.shape
    tr = min(ROW_TILE, S)
    nb = S // tr

    def body(dc_ref, fl_ref, b_ref, dfl_ref, db_ref, carry):
        i = pl.program_id(0)

        @pl.when(i == 0)
        def _():
            carry[...] = jnp.zeros_like(carry)

        upper = _tri(tr, lambda r, c: r <= c)
        cs = carry[...]
        for piece in _split3(dc_ref[...]):
            cs = cs + _dot(upper, piece)
        carry[...] = cs[0:1, :]
        dfl = cs * _sigmoid(-(fl_ref[...] + b_ref[...]))
        dfl_ref[...] = dfl
        db = jnp.sum(dfl, axis=0, keepdims=True)

        @pl.when(i == 0)
        def _():
            db_ref[...] = db

        @pl.when(i > 0)
        def _():
            db_ref[...] += db

    rev = pl.BlockSpec((tr, W), lambda i: (nb - 1 - i, 0))
    one = pl.BlockSpec((1, W), lambda i: (0, 0))
    return pl.pallas_call(
        body, name=name, grid=(nb,),
        in_specs=[rev, rev, one], out_specs=[rev, one],
        out_shape=[jax.ShapeDtypeStruct((S, W), F32), jax.ShapeDtypeStruct((1, W), F32)],
        scratch_shapes=[pltpu.VMEM((1, W), F32)],
        compiler_params=_cparams(("arbitrary",)),
    )(dcum, fl, b)


def _blk_iota(tq, tk):
    return (lax.broadcasted_iota(jnp.int32, (tq, tk), 0), lax.broadcasted_iota(jnp.int32, (tq, tk), 1))


def _cs(xb, tri):
    return _dot(xb, tri)


def _rowsum(xb):
    return jnp.sum(xb.astype(F32), axis=1, keepdims=True)


def _sb_block(qs, k, cmr, shift):
    z = _dot_nt(qs, k)
    strict = cmr < shift
    lb = jnp.minimum(z, 0.0) - jnp.log(1.0 + jnp.exp(-jnp.abs(z)))
    lom = jnp.where(strict, lb - z, 0.0).astype(BF16)
    return lb, lom, strict


def _att_tiles(S):
    return min(ATT_BQ, S), min(ATT_BK, S)


def _head_specs(S, tq):
    qspec = pl.BlockSpec((None, tq, HEAD_DIM), lambda h, i: (h, i, 0))
    kvspec = pl.BlockSpec((None, S, HEAD_DIM), lambda h, i: (h, 0, 0))
    vec = pl.BlockSpec((None, tq, 1), lambda h, i: (h, i, 0))
    return qspec, kvspec, vec


def sb_fwd(q, k, v, *, name):
    H, S, _ = q.shape
    tq, tk = _att_tiles(S)
    qspec, kvspec, vec = _head_specs(S, tq)

    def body(q_ref, k_ref, v_ref, o_ref, t_ref):
        i = pl.program_id(1)
        qs = q_ref[...] * ATTN_SCALE
        row, col = _blk_iota(tq, tk)
        cmr = col - row
        below = _tri(tk, lambda r, c: r > c)
        nkb = (i + 1) * (tq // tk)

        def step(n, carry):
            r_sum, acc = carry
            kb = nkb - 1 - n
            ks = pl.multiple_of(kb * tk, tk)
            lb, lom, strict = _sb_block(qs, k_ref[pl.ds(ks, tk), :], cmr, i * tq - kb * tk)
            w = jnp.where(strict, jnp.exp(lb + _cs(lom, below) + r_sum), 0.0)
            acc = acc + _dot(w.astype(BF16), v_ref[pl.ds(ks, tk), :])
            return r_sum + _rowsum(lom), acc

        r_sum, acc = lax.fori_loop(0, nkb, step, (jnp.zeros((tq, 1), F32), jnp.zeros((tq, HEAD_DIM), F32)))
        o_ref[...] = acc.astype(o_ref.dtype)
        t_ref[...] = r_sum

    return pl.pallas_call(
        body, name=name, grid=(H, S // tq),
        in_specs=[qspec, kvspec, kvspec], out_specs=[qspec, vec],
        out_shape=[jax.ShapeDtypeStruct((H, S, HEAD_DIM), BF16), jax.ShapeDtypeStruct((H, S, 1), F32)],
        compiler_params=_cparams(("parallel", "arbitrary")),
    )(q, k, v)


def sb_bwd(q, k, v, tot, do, *, name):
    H, S, _ = q.shape
    tq, tk = _att_tiles(S)
    qspec, kvspec, vec = _head_specs(S, tq)

    def body(q_ref, k_ref, v_ref, t_ref, do_ref, dq_ref, dk_ref, dv_ref):
        i = pl.program_id(1)

        @pl.when(i == 0)
        def _():
            dk_ref[...] = jnp.zeros_like(dk_ref)
            dv_ref[...] = jnp.zeros_like(dv_ref)

        qs, dov, t_all = q_ref[...] * ATTN_SCALE, do_ref[...], t_ref[...]
        row, col = _blk_iota(tq, tk)
        cmr = col - row
        upto = _tri(tk, lambda r, c: r <= c)
        before = _tri(tk, lambda r, c: r < c)

        def step(kb, carry):
            p_sum, e_sum, dq = carry
            ks = pl.multiple_of(kb * tk, tk)
            kv = k_ref[pl.ds(ks, tk), :]
            vv = v_ref[pl.ds(ks, tk), :]
            lb, lom, strict = _sb_block(qs, kv, cmr, i * tq - kb * tk)
            tail = t_all - p_sum - _cs(lom, upto)
            w = jnp.where(strict, jnp.exp(lb + tail), 0.0)
            e = _dot_nt(dov, vv) * w
            eb = e.astype(BF16)
            e_before = e_sum + _cs(eb, before)
            beta = jnp.exp(lb)
            dzb = jnp.where(strict, e - (e + e_before) * beta, 0.0).astype(BF16)
            dk_ref[pl.ds(ks, tk), :] += _dot_tn(dzb, qs)
            dv_ref[pl.ds(ks, tk), :] += _dot_tn(w.astype(BF16), dov)
            return (p_sum + _rowsum(lom), e_sum + _rowsum(eb),
                    dq + _dot(dzb, kv))

        zero = jnp.zeros((tq, 1), F32)
        _, _, dq = lax.fori_loop(0, (i + 1) * (tq // tk), step, (zero, zero, jnp.zeros((tq, HEAD_DIM), F32)))
        dq_ref[...] = dq * ATTN_SCALE

    full = jax.ShapeDtypeStruct((H, S, HEAD_DIM), F32)
    return pl.pallas_call(
        body, name=name, grid=(H, S // tq),
        in_specs=[qspec, kvspec, kvspec, vec, qspec], out_specs=[qspec, kvspec, kvspec],
        out_shape=[full, full, full],
        compiler_params=_cparams(("parallel", "arbitrary")),
    )(q, k, v, tot, do)


def _fox_logits(qs, k, cq, ck, cmr, shift):
    causal = cmr <= shift
    return jnp.where(causal, _dot_nt(qs, k) + cq - ck, NEG_INF), causal


def fox_fwd(q, k, v, cq, ck, *, name):
    H, S, _ = q.shape
    tq, tk = _att_tiles(S)
    qspec, kvspec, vec = _head_specs(S, tq)
    ckspec = pl.BlockSpec((None, S // tk, 1, tk), lambda h, i: (h, 0, 0, 0))

    def body(q_ref, k_ref, v_ref, cq_ref, ck_ref, o_ref, lse_ref):
        i = pl.program_id(1)
        qs, cqv = q_ref[...] * ATTN_SCALE, cq_ref[...]
        row, col = _blk_iota(tq, tk)
        cmr = col - row

        def step(kb, carry):
            m, l, acc = carry
            ks = pl.multiple_of(kb * tk, tk)
            s, _ = _fox_logits(qs, k_ref[pl.ds(ks, tk), :], cqv, ck_ref[kb], cmr, i * tq - kb * tk)
            m_new = jnp.maximum(m, jnp.max(s, axis=1, keepdims=True))
            alpha = jnp.exp(m - m_new)
            p = jnp.exp(s - m_new)
            l = alpha * l + jnp.sum(p, axis=1, keepdims=True)
            acc = alpha * acc + _dot(p.astype(BF16), v_ref[pl.ds(ks, tk), :])
            return m_new, l, acc

        m, l, acc = lax.fori_loop(0, (i + 1) * (tq // tk), step,
                                  (jnp.full((tq, 1), NEG_INF, F32), jnp.zeros((tq, 1), F32), jnp.zeros((tq, HEAD_DIM), F32)))
        o_ref[...] = (acc / l).astype(o_ref.dtype)
        lse_ref[...] = m + jnp.log(l)

    return pl.pallas_call(
        body, name=name, grid=(H, S // tq),
        in_specs=[qspec, kvspec, kvspec, vec, ckspec], out_specs=[qspec, vec],
        out_shape=[jax.ShapeDtypeStruct((H, S, HEAD_DIM), BF16), jax.ShapeDtypeStruct((H, S, 1), F32)],
        compiler_params=_cparams(("parallel", "arbitrary")),
    )(q, k, v, cq, ck)


def fox_bwd(q, k, v, o, lse, cq, ck, do, *, name):
    H, S, _ = q.shape
    tq, tk = _att_tiles(S)
    qspec, kvspec, vec = _head_specs(S, tq)
    ckspec = pl.BlockSpec((None, S // tk, 1, tk), lambda h, i: (h, 0, 0, 0))

    def body(q_ref, k_ref, v_ref, o_ref, lse_ref, cq_ref, ck_ref, do_ref, dq_ref, dk_ref, dv_ref, dcq_ref, dck_ref):
        i = pl.program_id(1)

        @pl.when(i == 0)
        def _():
            dk_ref[...] = jnp.zeros_like(dk_ref)
            dv_ref[...] = jnp.zeros_like(dv_ref)
            dck_ref[...] = jnp.zeros_like(dck_ref)

        qs, dov, cqv, lsev = q_ref[...] * ATTN_SCALE, do_ref[...], cq_ref[...], lse_ref[...]
        delta = jnp.sum(dov.astype(F32) * o_ref[...].astype(F32), axis=1, keepdims=True)
        row, col = _blk_iota(tq, tk)
        cmr = col - row

        def step(kb, carry):
            dq, dcq = carry
            ks = pl.multiple_of(kb * tk, tk)
            kv = k_ref[pl.ds(ks, tk), :]
            vv = v_ref[pl.ds(ks, tk), :]
            s, causal = _fox_logits(qs, kv, cqv, ck_ref[kb], cmr, i * tq - kb * tk)
            p = jnp.where(causal, jnp.exp(s - lsev), 0.0)
            ds = p * (_dot_nt(dov, vv) - delta)
            dck_ref[kb] += jnp.sum(ds, axis=0, keepdims=True)
            dsb = ds.astype(BF16)
            dk_ref[pl.ds(ks, tk), :] += _dot_tn(dsb, qs)
            dv_ref[pl.ds(ks, tk), :] += _dot_tn(p.astype(BF16), dov)
            return dq + _dot(dsb, kv), dcq + jnp.sum(ds, axis=1, keepdims=True)

        dq, dcq = lax.fori_loop(0, (i + 1) * (tq // tk), step, (jnp.zeros((tq, HEAD_DIM), F32), jnp.zeros((tq, 1), F32)))
        dq_ref[...] = dq * ATTN_SCALE
        dcq_ref[...] = dcq

    full = jax.ShapeDtypeStruct((H, S, HEAD_DIM), F32)
    return pl.pallas_call(
        body, name=name, grid=(H, S // tq),
        in_specs=[qspec, kvspec, kvspec, qspec, vec, vec, ckspec, qspec],
        out_specs=[qspec, kvspec, kvspec, vec, ckspec],
        out_shape=[full, full, full, jax.ShapeDtypeStruct((H, S, 1), F32),
                   jax.ShapeDtypeStruct((H, S // tk, 1, tk), F32)],
        compiler_params=_cparams(("parallel", "arbitrary")),
    )(q, k, v, o, lse, cq, ck, do)


def _swa_specs(S, tq):
    qspec = pl.BlockSpec((None, SWA_GROUP, tq, HEAD_DIM), lambda g, i: (g, 0, i, 0))
    kvspec = pl.BlockSpec((None, S + SWA_WINDOW, HEAD_DIM), lambda g, i: (g, 0, 0))
    vec = pl.BlockSpec((None, SWA_GROUP, tq, 1), lambda g, i: (g, 0, i, 0))
    sink = pl.BlockSpec((None, SWA_GROUP * tq, 1), lambda g, i: (g, 0, 0))
    return qspec, kvspec, vec, sink


def _swa_logits(q2, kw, i, tq):
    rows = q2.shape[0]
    r = lax.broadcasted_iota(jnp.int32, (rows, 2 * tq), 0)
    c = lax.broadcasted_iota(jnp.int32, (rows, 2 * tq), 1)
    diff = (r & (tq - 1)) + tq - c
    ok = (diff >= 0) & (diff < SWA_WINDOW) & (c + (i - 1) * tq >= 0)
    return jnp.where(ok, _dot_nt(q2, kw) * ATTN_SCALE, NEG_INF), ok


def swa_fwd(q, kp, vp, sink, *, name):
    _, G, S, _ = q.shape
    tq = ATT_BLK
    qspec, kvspec, vec, sinkspec = _swa_specs(S, tq)

    def body(q_ref, k_ref, v_ref, s_ref, o_ref, lse_ref):
        i = pl.program_id(1)
        q2 = q_ref[...].reshape(G * tq, HEAD_DIM)
        ws = pl.multiple_of(i * tq, tq)
        logits, _ = _swa_logits(q2, k_ref[pl.ds(ws, 2 * tq), :], i, tq)
        sk = s_ref[...]
        m = jnp.maximum(jnp.max(logits, axis=1, keepdims=True), sk)
        e = jnp.exp(logits - m)
        den = jnp.sum(e, axis=1, keepdims=True) + jnp.exp(sk - m)
        o = _dot((e / den).astype(BF16), v_ref[pl.ds(ws, 2 * tq), :])
        o_ref[...] = o.reshape(G, tq, HEAD_DIM).astype(o_ref.dtype)
        lse_ref[...] = (m + jnp.log(den)).reshape(G, tq, 1)

    return pl.pallas_call(
        body, name=name, grid=(SWA_KV_HEADS, S // tq),
        in_specs=[qspec, kvspec, kvspec, sinkspec], out_specs=[qspec, vec],
        out_shape=[jax.ShapeDtypeStruct(q.shape, BF16), jax.ShapeDtypeStruct((SWA_KV_HEADS, G, S, 1), F32)],
        compiler_params=_cparams(("parallel", "arbitrary")),
    )(q, kp, vp, sink)


def swa_bwd(q, kp, vp, sink, o, lse, do, *, name):
    _, G, S, _ = q.shape
    tq = ATT_BLK
    qspec, kvspec, vec, sinkspec = _swa_specs(S, tq)

    def body(q_ref, k_ref, v_ref, s_ref, o_ref, lse_ref, do_ref, dq_ref, dk_ref, dv_ref, dsink_ref):
        i = pl.program_id(1)

        @pl.when(i == 0)
        def _():
            dk_ref[...] = jnp.zeros_like(dk_ref)
            dv_ref[...] = jnp.zeros_like(dv_ref)

        q2 = q_ref[...].reshape(G * tq, HEAD_DIM)
        do2 = do_ref[...].reshape(G * tq, HEAD_DIM)
        o2 = o_ref[...].reshape(G * tq, HEAD_DIM)
        lse2 = lse_ref[...].reshape(G * tq, 1)
        ws = pl.multiple_of(i * tq, tq)
        kw = k_ref[pl.ds(ws, 2 * tq), :]
        vw = v_ref[pl.ds(ws, 2 * tq), :]
        logits, ok = _swa_logits(q2, kw, i, tq)
        p = jnp.where(ok, jnp.exp(logits - lse2), 0.0)
        delta = jnp.sum(do2.astype(F32) * o2.astype(F32), axis=1, keepdims=True)
        ds = p * (_dot_nt(do2, vw) - delta)
        dsb = ds.astype(BF16)
        dq_ref[...] = (_dot(dsb, kw) * ATTN_SCALE).reshape(G, tq, HEAD_DIM)
        dk_ref[pl.ds(ws, 2 * tq), :] += _dot_tn(dsb, q2) * ATTN_SCALE
        dv_ref[pl.ds(ws, 2 * tq), :] += _dot_tn(p.astype(BF16), do2)
        dsink_ref[...] = (-jnp.exp(s_ref[...] - lse2) * delta).reshape(G, tq, 1)

    kvshape = jax.ShapeDtypeStruct(kp.shape, F32)
    return pl.pallas_call(
        body, name=name, grid=(SWA_KV_HEADS, S // tq),
        in_specs=[qspec, kvspec, kvspec, sinkspec, qspec, vec, qspec],
        out_specs=[qspec, kvspec, kvspec, vec],
        out_shape=[jax.ShapeDtypeStruct(q.shape, F32), kvshape, kvshape,
                   jax.ShapeDtypeStruct((SWA_KV_HEADS, G, S, 1), F32)],
        compiler_params=_cparams(("parallel", "arbitrary")),
    )(q, kp, vp, sink, o, lse, do)


def _adamw_tile(w, g, m, v):
    m = ADAM_B1 * m + (1.0 - ADAM_B1) * g
    v = ADAM_B2 * v + (1.0 - ADAM_B2) * (g * g)
    m_hat = m / (1.0 - ADAM_B1 ** ADAM_STEP)
    v_hat = v / (1.0 - ADAM_B2 ** ADAM_STEP)
    delta = -ADAM_LR * (m_hat / (jnp.sqrt(v_hat) + ADAM_EPS) + ADAM_WD * w)
    return g, delta, m, v


def adamw(gfull, t, w, m, v, *, name):
    off, K, ns, _ = t
    sb = off // K
    nat = pl.BlockSpec((K, SLAB), lambda q: (0, q))

    def body(g_ref, w_ref, m_ref, v_ref, *outs):
        for o, r in zip(outs, _adamw_tile(w_ref[...], g_ref[...], m_ref[...], v_ref[...])):
            o[...] = r

    return pl.pallas_call(
        body, name=name, grid=(ns,),
        in_specs=[pl.BlockSpec((K, SLAB), lambda q: (sb + q, 0)), nat, nat, nat],
        out_specs=[nat] * 4, out_shape=[jax.ShapeDtypeStruct(w.shape, F32)] * 4,
        compiler_params=_cparams(("parallel",)),
    )(gfull, w, m, v)


def adamw_small(g, w, m, v, *, name):
    def body(g_ref, w_ref, m_ref, v_ref, *outs):
        for o, r in zip(outs, _adamw_tile(w_ref[...], g_ref[...], m_ref[...], v_ref[...])):
            o[...] = r

    return pl.pallas_call(body, name=name, out_shape=[jax.ShapeDtypeStruct(w.shape, F32)] * 4)(g, w, m, v)


MESH = pl.DeviceIdType.MESH
HBM = pl.BlockSpec(memory_space=pl.ANY)


def _place():
    x, y, c = lax.axis_index("x"), lax.axis_index("y"), lax.axis_index("c")
    others = [(1 - x, y), (x, 1 - y), (1 - x, 1 - y)]
    return x, y, c, others


def _rcopy(src, dst, send_sems, recv_sems, k, to):
    return pltpu.make_async_remote_copy(src_ref=src, dst_ref=dst, send_sem=send_sems.at[k], recv_sem=recv_sems.at[k],
                                        device_id=to, device_id_type=MESH)


def _dma_sems(*counts):
    return [pltpu.SemaphoreType.DMA((n,)) for n in counts]


DMA_UNIT_ROWS = 128
DMA_PIECES = 4
DMA_PIECES_LOCAL = 8


def _row_pieces(h, n):
    units = h // DMA_UNIT_ROWS
    n = min(n, units)
    base, extra = divmod(units, n)
    sizes = [(base + (k < extra)) * DMA_UNIT_ROWS for k in range(n)]
    return [(sum(sizes[:k]), sizes[k]) for k in range(n)]


def _start_pieces(make, h, n):
    for s0, sz in _row_pieces(h, n):
        make(s0, sz).start()
    return make(0, h)


SEM = pl.BlockSpec(memory_space=pltpu.SEMAPHORE)
SPLIT_COPY = pltpu.CompilerParams(has_side_effects=pltpu.SideEffectType.DATAFLOW_SIDE_EFFECTING)
N_OTHERS = 3


def _hbm(a):
    return pltpu.with_memory_space_constraint(a, pltpu.HBM)


def _chip_rows(buf, chip, s0, sz):
    return buf.at[2 * chip[0] + chip[1], pl.ds(s0, sz)]


def allgather_start(bufs, *, name):
    n = len(bufs)

    def body(*refs):
        ins, send, recv, token = refs[:n], refs[n:2 * n], refs[2 * n:3 * n], refs[4 * n]
        x, y, c, others = _place()
        for i in range(n):
            h = bufs[i].shape[1] // 2
            for f, chip in enumerate(others):
                for s0, sz in _row_pieces(h, DMA_PIECES):
                    mine = _chip_rows(ins[i], (x, y), c * h + s0, sz)
                    _rcopy(mine, mine, send[i], recv[i], f, (*chip, c)).start()
        token[...] = jnp.zeros_like(token)

    res = pl.pallas_call(
        body, name=name, in_specs=[HBM] * n,
        out_specs=[SEM] * (2 * n) + [HBM] * n + [pl.BlockSpec(memory_space=pltpu.VMEM)],
        out_shape=[pltpu.SemaphoreType.DMA((N_OTHERS,))] * (2 * n) + [pltpu.HBM(b.shape, b.dtype) for b in bufs]
        + [jax.ShapeDtypeStruct((1, D_MODEL), F32)],
        input_output_aliases={i: 2 * n + i for i in range(n)},
        compiler_params=SPLIT_COPY,
    )(*[_hbm(b) for b in bufs])
    return res[:n], res[n:2 * n], res[2 * n:3 * n], res[3 * n]


def allgather_wait(buf, send, recv, after, *, name):
    h = buf.shape[1] // 2

    def body(buf_ref, send_sems, recv_sems, after_ref, out_ref):
        del after_ref, out_ref
        x, y, c, others = _place()
        for f, chip in enumerate(others):
            mine = _chip_rows(buf_ref, (x, y), c * h, h)
            theirs = _chip_rows(buf_ref, chip, c * h, h)
            cp = _rcopy(mine, theirs, send_sems, recv_sems, f, (*chip, c))
            cp.wait_send()
            cp.wait_recv()

    return pl.pallas_call(
        body, name=name, in_specs=[HBM, SEM, SEM, HBM], out_specs=HBM,
        out_shape=pltpu.HBM(buf.shape, buf.dtype), input_output_aliases={0: 0},
        compiler_params=SPLIT_COPY,
    )(buf, send, recv, after)


def allgather_forward(buf, *, name):
    h = buf.shape[1] // 2

    def body(in_ref, out_ref, send_sems, recv_sems):
        del in_ref
        x, y, c, others = _place()
        sibling = (x, y, 1 - c)
        sent = []
        for f, chip in enumerate(others):
            sent.append(_start_pieces(
                lambda s0, sz: _rcopy(_chip_rows(out_ref, chip, c * h + s0, sz), _chip_rows(out_ref, chip, c * h + s0, sz),
                                      send_sems, recv_sems, f, sibling), h, DMA_PIECES))
        for f, chip in enumerate(others):
            blk = _chip_rows(out_ref, chip, (1 - c) * h, h)
            _rcopy(blk, blk, send_sems, recv_sems, f, sibling).wait_recv()
        for cp in sent:
            cp.wait_send()

    return pl.pallas_call(
        body, name=name, in_specs=[HBM], out_specs=HBM,
        out_shape=jax.ShapeDtypeStruct(buf.shape, buf.dtype), input_output_aliases={0: 0},
        scratch_shapes=_dma_sems(N_OTHERS, N_OTHERS),
    )(buf)


def swap_halves(grads, *, name):
    n = len(grads)

    def body(*refs):
        ins, theirs = refs[:n], refs[n:2 * n]
        send_sems, recv_sems = refs[2 * n:]
        x, y, c, _ = _place()
        for i in range(n):
            h = grads[i].shape[1] // 2
            for k in range(N_CHIPS):
                for s0, sz in _row_pieces(h, DMA_PIECES):
                    _rcopy(ins[i].at[k, pl.ds((1 - c) * h + s0, sz)], theirs[i].at[k, pl.ds(s0, sz)],
                           send_sems, recv_sems, i, (x, y, 1 - c)).start()
        for i in range(n):
            h = grads[i].shape[1] // 2
            _rcopy(ins[i].at[:, pl.ds((1 - c) * h, h)], theirs[i], send_sems, recv_sems, i, (x, y, 1 - c)).wait()

    return pl.pallas_call(
        body, name=name, in_specs=[HBM] * n, out_specs=[HBM] * n,
        out_shape=[jax.ShapeDtypeStruct((N_CHIPS, g.shape[1] // 2, SLAB), g.dtype) for g in grads],
        scratch_shapes=_dma_sems(n, n))(*grads)


def scatter_start(part, *, name):
    h = part.shape[1]

    def body(part_ref, land_ref, send, recv, part_out, land_out, token):
        del part_out, land_out
        x, y, c, others = _place()
        for f, chip in enumerate(others):
            for s0, sz in _row_pieces(h, DMA_PIECES):
                _rcopy(_chip_rows(part_ref, chip, s0, sz), land_ref.at[f, pl.ds(s0, sz)], send, recv, f, (*chip, c)).start()
        token[...] = jnp.zeros_like(token)

    land = lax.empty((N_OTHERS,) + part.shape[1:], part.dtype)
    return pl.pallas_call(
        body, name=name, in_specs=[HBM, HBM],
        out_specs=[SEM, SEM, HBM, HBM, pl.BlockSpec(memory_space=pltpu.VMEM)],
        out_shape=[pltpu.SemaphoreType.DMA((N_OTHERS,))] * 2 + [pltpu.HBM(part.shape, part.dtype), pltpu.HBM(land.shape, land.dtype),
                                                                 jax.ShapeDtypeStruct((1, D_MODEL), F32)],
        input_output_aliases={0: 2, 1: 3},
        compiler_params=SPLIT_COPY,
    )(_hbm(part), _hbm(land))


def scatter_wait(part, land, send, recv, after, *, name):
    h = part.shape[1]

    def body(part_ref, land_ref, send_sems, recv_sems, after_ref, part_out, land_out):
        del after_ref, part_out, land_out
        x, y, c, others = _place()
        for f, chip in enumerate(others):
            cp = _rcopy(_chip_rows(part_ref, chip, 0, h), land_ref.at[f], send_sems, recv_sems, f, (*chip, c))
            cp.wait_send()
            cp.wait_recv()

    return pl.pallas_call(
        body, name=name, in_specs=[HBM, HBM, SEM, SEM, HBM], out_specs=[HBM, HBM],
        out_shape=[pltpu.HBM(part.shape, part.dtype), pltpu.HBM(land.shape, land.dtype)],
        input_output_aliases={0: 0, 1: 1},
        compiler_params=SPLIT_COPY,
    )(part, land, send, recv, after)


def join_halves(bufs, *, name):
    n = len(bufs)

    def body(*refs):
        outs = refs[n:2 * n]
        send_sems, recv_sems = refs[2 * n:]
        x, y, c, _ = _place()
        sibling = (x, y, 1 - c)
        cps = []
        for i in range(n):
            h = bufs[i].shape[0] // 2
            snd = _start_pieces(
                lambda s0, sz: _rcopy(outs[i].at[pl.ds(c * h + s0, sz)], outs[i].at[pl.ds(c * h + s0, sz)],
                                      send_sems, recv_sems, i, sibling), h, 2 * DMA_PIECES_LOCAL)
            theirs = outs[i].at[pl.ds((1 - c) * h, h)]
            cps.append((snd, _rcopy(theirs, theirs, send_sems, recv_sems, i, sibling)))
        for snd, rcv in cps:
            snd.wait_send()
            rcv.wait_recv()

    return pl.pallas_call(
        body, name=name, in_specs=[HBM] * n, out_specs=[HBM] * n,
        out_shape=[jax.ShapeDtypeStruct(b.shape, b.dtype) for b in bufs],
        input_output_aliases={i: i for i in range(n)},
        scratch_shapes=_dma_sems(n, n),
    )(*bufs)


def allreduce_small(v, *, name):
    rows, n = v.shape

    def body(x_ref, sum_ref, all_ref, send_sems, recv_sems, local_sem):
        x, y, c, others = _place()
        me, sibling = (x, y, c), (x, y, 1 - c)

        def blk(px, py, pc):
            return all_ref.at[pl.ds((4 * px + 2 * py + pc) * rows, rows), :]

        def copy(k, block, to, src=None):
            return _rcopy(blk(*block) if src is None else src, blk(*block), send_sems, recv_sems, k, to)

        mine = pltpu.make_async_copy(x_ref, blk(*me), local_sem)
        mine.start()
        first = [copy(0, me, sibling, src=x_ref)]
        first += [copy(1 + f, me, (*chip, c), src=x_ref) for f, chip in enumerate(others)]
        for cp in first:
            cp.start()
        passed = [copy(4 + f, (*chip, c), sibling) for f, chip in enumerate(others)]
        for f, chip in enumerate(others):
            copy(1 + f, (*chip, c), me).wait_recv()
            passed[f].start()
        copy(0, sibling, me).wait_recv()
        for f, chip in enumerate(others):
            copy(4 + f, (*chip, 1 - c), me).wait_recv()
        for cp in first + passed:
            cp.wait_send()
        mine.wait()
        acc = all_ref[pl.ds(0, rows), :]
        for d in range(1, N_DEVICES):
            acc = acc + all_ref[pl.ds(d * rows, rows), :]
        sum_ref[...] = acc

    vm = pl.BlockSpec(memory_space=pltpu.VMEM)
    return pl.pallas_call(
        body, name=name, in_specs=[vm], out_specs=[vm, vm],
        out_shape=[jax.ShapeDtypeStruct((rows, n), F32), jax.ShapeDtypeStruct((N_DEVICES * rows, n), F32)],
        scratch_shapes=_dma_sems(7, 7) + [pltpu.SemaphoreType.DMA],
    )(v)[0]


def add_pairs(grad, theirs, where, *, name):
    h = theirs.shape[1]
    spec = pl.BlockSpec((None, h, SLAB), lambda k, w: (k, 0, 0))

    def body(w_ref, a_ref, b_ref, o_ref):
        del w_ref
        o_ref[...] = (a_ref[...].astype(F32) + b_ref[...].astype(F32)).astype(o_ref.dtype)

    return pl.pallas_call(
        body, name=name,
        grid_spec=pltpu.PrefetchScalarGridSpec(
            num_scalar_prefetch=1, grid=(N_CHIPS,),
            in_specs=[pl.BlockSpec((None, h, SLAB), lambda k, w: (k, w[1], 0)), spec], out_specs=spec),
        out_shape=jax.ShapeDtypeStruct(theirs.shape, theirs.dtype),
        compiler_params=_cparams(("parallel",)))(where, grad, theirs)


def add_chips(pair, got, where, *, name):
    h = pair.shape[1]
    tr = h // 2

    def body(w_ref, a_ref, b_ref, o_ref):
        del w_ref
        acc = a_ref[...].astype(F32)
        for f in range(3):
            acc = acc + b_ref[f].astype(F32)
        o_ref[...] = acc

    return pl.pallas_call(
        body, name=name,
        grid_spec=pltpu.PrefetchScalarGridSpec(
            num_scalar_prefetch=1, grid=(2,),
            in_specs=[pl.BlockSpec((None, tr, SLAB), lambda i, w: (w[0], i, 0)),
                      pl.BlockSpec((3, tr, SLAB), lambda i, w: (0, i, 0))],
            out_specs=pl.BlockSpec((tr, SLAB), lambda i, w: (2 * w[1] + i, 0))),
        out_shape=jax.ShapeDtypeStruct((2 * h, SLAB), F32),
        compiler_params=_cparams(("parallel",)))(where, pair, got)


DEPTH = 4
MIXER = (0, 1, 2, 0)
W_IN_COLS = (768, 320, 772)
W_IN_PAD = (768, 512, 1024)
MATS = ("up", "down", "inp", "out", "gate", "proj")
MAT_ARG = dict(up="w_up", down="w_down", inp="w_in", out="w_out", gate="w_ple_gate", proj="w_ple_proj")
GAINS = ("attn_norm", "mlp_norm", "ple_norm")
N_SMALL = 16
KINDS = ("grad_", "delta_", "new_m_", "new_v_")


def _layout(kind):
    ns_in = W_IN_PAD[kind] // SLAB
    off = 8192 + 1024 * ns_in
    lay = dict(up=(0, 1024, 4, False), down=(4096, 1024, 4, True), inp=(8192, 1024, ns_in, False),
               out=(off, 256, 4, True), gate=(off + 1024, 256, 4, True), proj=(off + 2048, 256, 1, False))
    return lay, off + 2304


def _to_slabs(w):
    k, c = w.shape
    return w.reshape(k, c // SLAB, SLAB).transpose(1, 0, 2).reshape(-1, SLAB)


def _pad_cols(w, n):
    return jnp.pad(w, ((0, 0), (0, n - w.shape[1])))


def _heads(x2d, n):
    return x2d.reshape(x2d.shape[0], n, HEAD_DIM).transpose(1, 0, 2)


def _unheads(x3d):
    n, s, _ = x3d.shape
    return x3d.transpose(1, 0, 2).reshape(s, n * HEAD_DIM)


def _chip_cols(x2d, c, cpad):
    s = x2d.shape[0]
    return jnp.pad(x2d.reshape(s, N_CHIPS, c), ((0, 0), (0, 0), (0, cpad - c))).reshape(s, N_CHIPS * cpad)


def _unchip_cols(x2d, c, cpad):
    s = x2d.shape[0]
    return x2d.reshape(s, N_CHIPS, cpad)[:, :, :c].reshape(s, N_CHIPS * c)


def _add_res(acc, res):
    return (acc + res,)


def _relu2(acc):
    return acc, jnp.square(jnp.maximum(acc, 0.0))


def _relu2_bwd(acc, u):
    return (acc * (2.0 * jnp.maximum(u.astype(F32), 0.0)),)


def _ple_fwd(acc, x2, pp):
    return x2 + pp * _sigmoid(acc), acc


def _ple_bwd(dx, pp, gl):
    gate = _sigmoid(gl)
    return dx * gate, dx * pp * gate * (1.0 - gate)


def _layer_fwd(i, kind, x0, p_bf, wg, lay, gains, extra, tabs):
    s = x0.shape[0]
    an, mn, pn = gains
    sv = dict(x0=x0)
    h1 = rms_fwd(x0, an, name=f"attn_norm_{i}")
    if kind == 0:
        proj = mm_nn(h1, wg, lay["inp"], name=f"w_in_{i}")[0]
        qkv = proj.reshape(s, 3, N_HEADS, HEAD_DIM).transpose(1, 2, 0, 3)
        o, tot = sb_fwd(qkv[0], qkv[1], qkv[2], name=f"sb_fwd_{i}")
        sv.update(qkv=qkv, tot=tot)
    elif kind == 1:
        projp = mm_nn(h1, wg, lay["inp"], name=f"w_in_{i}", out_dtypes=(F32,))[0]
        proj = _unchip_cols(projp, W_IN_COLS[1], W_IN_PAD[1])
        nq = N_HEADS * HEAD_DIM
        nqk = nq + SWA_KV_HEADS * HEAD_DIM
        qk = rope_fwd(proj[:, :nqk], tabs, name=f"rope_{i}")
        q = _heads(qk[:, :nq], N_HEADS).reshape(SWA_KV_HEADS, SWA_GROUP, s, HEAD_DIM)
        front = ((0, 0), (SWA_WINDOW, 0), (0, 0))
        kp = jnp.pad(_heads(qk[:, nq:], SWA_KV_HEADS), front)
        vp = jnp.pad(_heads(proj[:, nqk:].astype(BF16), SWA_KV_HEADS), front)
        sink = jnp.repeat(extra.reshape(SWA_KV_HEADS, SWA_GROUP), ATT_BLK, axis=1)[:, :, None]
        o4, lse = swa_fwd(q, kp, vp, sink, name=f"swa_fwd_{i}")
        o = o4.reshape(N_HEADS, s, HEAD_DIM)
        sv.update(q=q, kp=kp, vp=vp, sink=sink, o4=o4, lse=lse)
    else:
        projp = mm_nn(h1, wg, lay["inp"], name=f"w_in_{i}", out_dtypes=(F32,))[0]
        proj = _unchip_cols(projp, W_IN_COLS[2], W_IN_PAD[2])
        nqkv = 3 * N_HEADS * HEAD_DIM
        qkv = proj[:, :nqkv].astype(BF16).reshape(s, 3, N_HEADS, HEAD_DIM).transpose(1, 2, 0, 3)
        fl = _pad_cols(proj[:, nqkv:], 128)
        bp = _pad_cols(extra[None], 128)
        cum_t = fox_gate_fwd(fl, bp, name=f"gate_fwd_{i}")[:, :N_HEADS].T
        cq = cum_t[:, :, None]
        ck = cum_t.reshape(N_HEADS, s // min(ATT_BK, s), 1, min(ATT_BK, s))
        o, lse = fox_fwd(qkv[0], qkv[1], qkv[2], cq, ck, name=f"fox_fwd_{i}")
        sv.update(qkv=qkv, fl=fl, bp=bp, cq=cq, ck=ck, o=o, lse=lse)
    a = _unheads(o)
    x1 = mm_nn(a, wg, lay["out"], name=f"w_out_{i}", epi=_add_res, extras=(x0,), out_dtypes=(F32,))[0]
    h2 = rms_fwd(x1, mn, name=f"mlp_norm_{i}")
    u, r = mm_nn(h2, wg, lay["up"], name=f"w_up_{i}", epi=_relu2, out_dtypes=(BF16, BF16))
    x2 = mm_nn(r, wg, lay["down"], name=f"w_down_{i}", epi=_add_res, extras=(x1,), out_dtypes=(F32,))[0]
    h3 = rms_fwd(x2, pn, name=f"ple_norm_{i}")
    pp = mm_nn(p_bf, wg, lay["proj"], name=f"w_ple_proj_{i}", out_dtypes=(F32,))[0]
    x3, gl = mm_nn(h3, wg, lay["gate"], name=f"w_ple_gate_{i}", epi=_ple_fwd, extras=(x2, pp), out_dtypes=(F32, F32))
    sv.update(h1=h1, a=a, x1=x1, h2=h2, u=u, r=r, x2=x2, h3=h3, pp=pp, gl=gl)
    return x3, sv


def _layer_bwd(i, kind, dx3, sv, p_bf, wg, lay, n_rows, gains, tabs):
    s = dx3.shape[0]
    an, mn, pn = gains
    g = lax.empty((N_CHIPS, n_rows, SLAB), BF16)
    d_pp, d_gl = ew(_ple_bwd, [dx3, sv["pp"], sv["gl"]], [BF16, BF16], name=f"ple_bwd_{i}")
    g = mm_tn(p_bf, d_pp, g, lay["proj"], name=f"dw_ple_proj_{i}")
    g = mm_tn(sv["h3"], d_gl, g, lay["gate"], name=f"dw_ple_gate_{i}")
    d_h3 = mm_nt(d_gl, wg, lay["gate"], name=f"dx_ple_gate_{i}", out_dtypes=(F32,))[0]
    dx2, dx2b, d_pn = rms_bwd(sv["x2"], pn, d_h3, dx3, name=f"ple_norm_bwd_{i}")
    g = mm_tn(sv["r"], dx2b, g, lay["down"], name=f"dw_down_{i}")
    d_u = mm_nt(dx2b, wg, lay["down"], name=f"dx_down_{i}", epi=_relu2_bwd, extras=(sv["u"],))[0]
    g = mm_tn(sv["h2"], d_u, g, lay["up"], name=f"dw_up_{i}")
    d_h2 = mm_nt(d_u, wg, lay["up"], name=f"dx_up_{i}", out_dtypes=(F32,))[0]
    dx1, dx1b, d_mn = rms_bwd(sv["x1"], mn, d_h2, dx2, name=f"mlp_norm_bwd_{i}")
    g = mm_tn(sv["a"], dx1b, g, lay["out"], name=f"dw_out_{i}")
    d_a = mm_nt(dx1b, wg, lay["out"], name=f"dx_out_{i}")[0]
    do = _heads(d_a, N_HEADS)
    d_extra = None
    if kind == 0:
        qkv = sv["qkv"]
        dq, dk, dv = sb_bwd(qkv[0], qkv[1], qkv[2], sv["tot"], do, name=f"sb_bwd_{i}")
        d_proj = jnp.stack([dq, dk, dv]).transpose(2, 0, 1, 3).reshape(s, 3 * N_HEADS * HEAD_DIM).astype(BF16)
    elif kind == 1:
        do4 = do.reshape(SWA_KV_HEADS, SWA_GROUP, s, HEAD_DIM)
        dq, dkp, dvp, dsr = swa_bwd(sv["q"], sv["kp"], sv["vp"], sv["sink"], sv["o4"], sv["lse"], do4, name=f"swa_bwd_{i}")
        dqk = jnp.concatenate([_unheads(dq.reshape(N_HEADS, s, HEAD_DIM)), _unheads(dkp[:, SWA_WINDOW:])], axis=1)
        dqk = rope_bwd(dqk, tabs, name=f"rope_bwd_{i}")
        d_proj = jnp.concatenate([dqk, _unheads(dvp[:, SWA_WINDOW:]).astype(BF16)], axis=1)
        d_proj = _chip_cols(d_proj, W_IN_COLS[1], W_IN_PAD[1])
        d_extra = jnp.sum(dsr[..., 0], axis=2).reshape(N_HEADS)
    else:
        qkv = sv["qkv"]
        dq, dk, dv, dcq, dck = fox_bwd(qkv[0], qkv[1], qkv[2], sv["o"], sv["lse"], sv["cq"], sv["ck"], do, name=f"fox_bwd_{i}")
        dcum = _pad_cols((dcq[:, :, 0] - dck.reshape(N_HEADS, s)).T, 128)
        dfl, dbp = fox_gate_bwd(dcum, sv["fl"], sv["bp"], name=f"gate_bwd_{i}")
        d_qkv = jnp.stack([dq, dk, dv]).transpose(2, 0, 1, 3).reshape(s, 3 * N_HEADS * HEAD_DIM)
        d_proj = jnp.concatenate([d_qkv, dfl[:, :N_HEADS]], axis=1).astype(BF16)
        d_proj = _chip_cols(d_proj, W_IN_COLS[2], W_IN_PAD[2])
        d_extra = dbp[0, :N_HEADS]
    g = mm_tn(sv["h1"], d_proj, g, lay["inp"], name=f"dw_in_{i}")
    d_h1 = mm_nt(d_proj, wg, lay["inp"], name=f"dx_in_{i}", out_dtypes=(F32,))[0]
    dx0, _, d_an = rms_bwd(sv["x0"], an, d_h1, dx1, name=f"attn_norm_bwd_{i}")
    return dx0, g, (d_an, d_mn, d_pn), d_extra


def _small_rows(a, prefix):
    rows = [a[f"{prefix}{n}_{i}"] for i in range(DEPTH) for n in GAINS] + [a[f"{prefix}final_norm"]]
    rows += [_pad_cols(a[f"{prefix}{n}"][None], D_MODEL)[0] for n in ("sinks_1", "b_forget_2")]
    return jnp.stack(rows + [jnp.zeros((D_MODEL,), F32)])


def _train_step(a):
    x = a["x"][0]
    tabs = rope_tables(x.shape[0], (N_HEADS + SWA_KV_HEADS) * HEAD_DIM)
    lays = [_layout(k) for k in MIXER]

    def natural(prefix, i, m):
        w = a[f"{prefix}{MAT_ARG[m]}_{i}"]
        return _pad_cols(w, W_IN_PAD[MIXER[i]]) if m == "inp" else w

    chip = 2 * lax.axis_index("x") + lax.axis_index("y")
    where = jnp.stack([chip, lax.axis_index("c")]).astype(jnp.int32)
    def own_block(i, zero):
        pk = jnp.concatenate([_to_slabs((natural("", i, m) + zero).astype(BF16)) for m in MATS], axis=0)
        return lax.dynamic_update_slice(lax.empty((N_CHIPS,) + pk.shape, BF16), pk[None], (chip, 0, 0))

    sends, recvs, bufs, token = allgather_start([own_block(0, 0.0)], name="allgather_start_0")
    more = allgather_start([own_block(i, token[0, 0]) for i in range(1, DEPTH)], name="allgather_start_1")
    sends, recvs, bufs = sends + more[0], recvs + more[1], bufs + more[2]

    gains = [tuple(a[f"{n}_{i}"][None] for n in GAINS) for i in range(DEPTH)]
    extras = [None, a["sinks_1"], a["b_forget_2"], None]
    p_bf = [a["p"][i, 0].astype(BF16) for i in range(DEPTH)]

    saved, wgs, after = [], [], token
    for i in range(DEPTH):
        landed = allgather_wait(bufs[i], sends[i], recvs[i], after, name=f"allgather_wait_{i}")
        wgs.append(allgather_forward(landed, name=f"allgather_forward_{i}"))
        x, sv = _layer_fwd(i, MIXER[i], x, p_bf[i], wgs[i], lays[i][0], gains[i], extras[i], tabs)
        saved.append(sv)
        after = x
    dx, d_final, loss = loss_head(x, a["final_norm"][None], a["loss_target"][0], name="loss_head")

    def finish(i, started, after):
        send, recv, part, land, _ = started
        part, got = scatter_wait(part, land, send, recv, after, name=f"scatter_wait_{i}")
        return join_halves([add_chips(part, got, where, name=f"add_chips_{i}")], name=f"join_halves_{i}")[0]

    small = [None] * N_SMALL
    small[12] = d_final[0]
    small[15] = _pad_cols(loss[:, :1], D_MODEL)[0]
    gfull = [None] * DEPTH
    started = None
    for i in reversed(range(DEPTH)):
        an, mn, pn = gains[i]
        if started is not None:
            pn = pn + started[4]
        dx, grad, d_gains, d_extra = _layer_bwd(i, MIXER[i], dx, saved[i], p_bf[i], wgs[i], lays[i][0], lays[i][1],
                                                (an, mn, pn), tabs)
        for j in range(3):
            small[3 * i + j] = d_gains[j][0]
        if d_extra is not None:
            small[12 + MIXER[i]] = _pad_cols(d_extra[None], D_MODEL)[0]
        if started is not None:
            gfull[i + 1] = finish(i + 1, started, dx)
        theirs = swap_halves([grad], name=f"swap_halves_{i}")[0]
        started = scatter_start(add_pairs(grad, theirs, where, name=f"add_pairs_{i}"), name=f"scatter_start_{i}")
    small = allreduce_small(jnp.stack(small), name="allreduce_small")

    out = {"loss": small[15, 0], "grad_x": dx[None]}
    res = adamw_small(small, _small_rows(a, ""), _small_rows(a, "m_"), _small_rows(a, "v_"), name="adamw_small")
    for i in reversed(range(DEPTH)):
        if i == 0:
            gfull[0] = finish(0, started, out[f"delta_{MAT_ARG[MATS[-1]]}_1"])
        for m in MATS:
            upd = adamw(gfull[i], lays[i][0][m], natural("", i, m), natural("m_", i, m), natural("v_", i, m),
                        name=f"adamw_{MAT_ARG[m]}_{i}")
            cols = a[f"{MAT_ARG[m]}_{i}"].shape[1]
            for kd, r in zip(KINDS, upd):
                out[f"{kd}{MAT_ARG[m]}_{i}"] = r[:, :cols]
    for kd, r in zip(KINDS, res):
        for i in range(DEPTH):
            for j, n in enumerate(GAINS):
                out[f"{kd}{n}_{i}"] = r[3 * i + j]
        out[f"{kd}final_norm"] = r[12]
        out[f"{kd}sinks_1"] = r[13, :N_HEADS]
        out[f"{kd}b_forget_2"] = r[14, :N_HEADS]
    return out


def _weight_names():
    names = []
    for i in range(DEPTH):
        names += [f"attn_norm_{i}", f"w_in_{i}", f"w_out_{i}"] + [[], ["sinks_1"], ["b_forget_2"]][MIXER[i]]
        names += [f"mlp_norm_{i}", f"w_up_{i}", f"w_down_{i}", f"ple_norm_{i}", f"w_ple_gate_{i}", f"w_ple_proj_{i}"]
    return names + ["final_norm"]


def kernel(x, p, attn_norm_0, w_in_0, w_out_0, mlp_norm_0, w_up_0, w_down_0, ple_norm_0, w_ple_gate_0, w_ple_proj_0, attn_norm_1, w_in_1, w_out_1, sinks_1, mlp_norm_1, w_up_1, w_down_1, ple_norm_1, w_ple_gate_1, w_ple_proj_1, attn_norm_2, w_in_2, w_out_2, b_forget_2, mlp_norm_2, w_up_2, w_down_2, ple_norm_2, w_ple_gate_2, w_ple_proj_2, attn_norm_3, w_in_3, w_out_3, mlp_norm_3, w_up_3, w_down_3, ple_norm_3, w_ple_gate_3, w_ple_proj_3, final_norm, loss_target, m_attn_norm_0, m_w_in_0, m_w_out_0, m_mlp_norm_0, m_w_up_0, m_w_down_0, m_ple_norm_0, m_w_ple_gate_0, m_w_ple_proj_0, m_attn_norm_1, m_w_in_1, m_w_out_1, m_sinks_1, m_mlp_norm_1, m_w_up_1, m_w_down_1, m_ple_norm_1, m_w_ple_gate_1, m_w_ple_proj_1, m_attn_norm_2, m_w_in_2, m_w_out_2, m_b_forget_2, m_mlp_norm_2, m_w_up_2, m_w_down_2, m_ple_norm_2, m_w_ple_gate_2, m_w_ple_proj_2, m_attn_norm_3, m_w_in_3, m_w_out_3, m_mlp_norm_3, m_w_up_3, m_w_down_3, m_ple_norm_3, m_w_ple_gate_3, m_w_ple_proj_3, m_final_norm, v_attn_norm_0, v_w_in_0, v_w_out_0, v_mlp_norm_0, v_w_up_0, v_w_down_0, v_ple_norm_0, v_w_ple_gate_0, v_w_ple_proj_0, v_attn_norm_1, v_w_in_1, v_w_out_1, v_sinks_1, v_mlp_norm_1, v_w_up_1, v_w_down_1, v_ple_norm_1, v_w_ple_gate_1, v_w_ple_proj_1, v_attn_norm_2, v_w_in_2, v_w_out_2, v_b_forget_2, v_mlp_norm_2, v_w_up_2, v_w_down_2, v_ple_norm_2, v_w_ple_gate_2, v_w_ple_proj_2, v_attn_norm_3, v_w_in_3, v_w_out_3, v_mlp_norm_3, v_w_up_3, v_w_down_3, v_ple_norm_3, v_w_ple_gate_3, v_w_ple_proj_3, v_final_norm):
    out = _train_step(dict(locals()))
    return (out["loss"], out["grad_x"], *[out[kd + n] for kd in KINDS for n in _weight_names()])
```

```python
import jax
import jax.numpy as jnp
from jax import lax
from jax.experimental import pallas as pl
from jax.experimental.pallas import tpu as pltpu

F32 = jnp.float32
BF16 = jnp.bfloat16

D_MODEL = 1024
N_HEADS = 16
HEAD_DIM = 64
SWA_KV_HEADS = 2
SWA_GROUP = 8
SWA_WINDOW = 128
ROPE_THETA = 500000.0
ROPE_DIM = 16
RMS_EPS = 1e-6
NEG_INF = -1e30
ATTN_SCALE = HEAD_DIM ** -0.5
N_CHIPS = 4
N_DEVICES = 8

SLAB = 256
ATT_BLK = 128
ATT_BQ = 512
ATT_BK = 512
ROW_TILE = 256
V7X_VMEM_LIMIT = 56 * 1024 * 1024

ADAM_LR, ADAM_B1, ADAM_B2, ADAM_EPS, ADAM_WD, ADAM_STEP = 0.001, 0.9, 0.999, 1e-08, 0.01, 10


def _cparams(sem=None):
    return pltpu.CompilerParams(dimension_semantics=sem, vmem_limit_bytes=V7X_VMEM_LIMIT)


def _dot(a, b):
    return jnp.dot(a, b, preferred_element_type=F32)


def _dot_nt(a, b):
    return lax.dot_general(a, b, (((1,), (1,)), ((), ())), preferred_element_type=F32)


def _dot_tn(a, b):
    return lax.dot_general(a, b, (((0,), (0,)), ((), ())), preferred_element_type=F32)


def _row_tile(M, K):
    return min(M, 1024) if K >= 1024 else M


def _finish(epi, acc, ex, outs):
    res = epi(acc, *[e[...] for e in ex]) if epi is not None else (acc,)
    for o, r in zip(outs, res):
        o[...] = r.astype(o.dtype)


def mm_nn(a, wg, t, *, name, epi=None, extras=(), out_dtypes=(BF16,)):
    off, K, ns, row = t
    M = a.shape[0]
    sb = off // K
    ne, no = len(extras), len(out_dtypes)
    if row:
        tm = _row_tile(M, K)
        nb = N_CHIPS
        grid = (M // tm, ns)
        a_spec = pl.BlockSpec((tm, N_CHIPS * K), lambda i, q: (i, 0))
        b_specs = [pl.BlockSpec((None, K, SLAB), lambda i, q, j=j: (j, sb + q, 0)) for j in range(nb)]
        tile = pl.BlockSpec((tm, SLAB), lambda i, q: (i, q))
        n_out = ns * SLAB
    else:
        nb = 1
        grid = (N_CHIPS, ns)
        a_spec = pl.BlockSpec((M, K), lambda j, q: (0, 0))
        b_specs = [pl.BlockSpec((None, K, SLAB), lambda j, q: (j, sb + q, 0))]
        tile = pl.BlockSpec((M, SLAB), lambda j, q: (0, j * ns + q))
        n_out = N_CHIPS * ns * SLAB

    def body(a_ref, *rest):
        bs, ex, outs = rest[:nb], rest[nb:nb + ne], rest[nb + ne:]
        acc = _dot(a_ref[:, pl.ds(0, K)], bs[0][...])
        for j in range(1, nb):
            acc = acc + _dot(a_ref[:, pl.ds(j * K, K)], bs[j][...])
        _finish(epi, acc, ex, outs)

    return pl.pallas_call(
        body, name=name, grid=grid,
        in_specs=[a_spec] + b_specs + [tile] * ne, out_specs=[tile] * no,
        out_shape=[jax.ShapeDtypeStruct((M, n_out), d) for d in out_dtypes],
        compiler_params=_cparams(("parallel", "parallel")),
    )(a, *([wg] * nb), *extras)


def mm_nt(dy, wg, t, *, name, epi=None, extras=(), out_dtypes=(BF16,)):
    off, K, ns, row = t
    M = dy.shape[0]
    tm = _row_tile(M, K)
    sb = off // K
    ne, no = len(extras), len(out_dtypes)
    grid = (M // tm, N_CHIPS)
    b_specs = [pl.BlockSpec((None, K, SLAB), lambda i, j, q=q: (j, sb + q, 0)) for q in range(ns)]
    if row:
        dy_spec = pl.BlockSpec((tm, ns * SLAB), lambda i, j: (i, 0))
        tile = pl.BlockSpec((tm, K), lambda i, j: (i, j))
        n_out = N_CHIPS * K
        sem = ("parallel", "parallel")
    else:
        dy_spec = pl.BlockSpec((tm, ns * SLAB), lambda i, j: (i, j))
        tile = pl.BlockSpec((tm, K), lambda i, j: (i, 0))
        n_out = K
        sem = ("parallel", "arbitrary")

    def body(dy_ref, *rest):
        bs, ex, outs = rest[:ns], rest[ns:ns + ne], rest[ns + ne:ns + ne + no]
        part = _dot_nt(dy_ref[:, pl.ds(0, SLAB)], bs[0][...])
        for q in range(1, ns):
            part = part + _dot_nt(dy_ref[:, pl.ds(q * SLAB, SLAB)], bs[q][...])
        if row:
            _finish(epi, part, ex, outs)
        else:
            acc_ref = rest[-1]
            j = pl.program_id(1)

            @pl.when(j == 0)
            def _():
                acc_ref[...] = part

            @pl.when(j > 0)
            def _():
                acc_ref[...] += part

            @pl.when(j == N_CHIPS - 1)
            def _():
                _finish(epi, acc_ref[...], ex, outs)

    return pl.pallas_call(
        body, name=name, grid=grid,
        in_specs=[dy_spec] + b_specs + [tile] * ne, out_specs=[tile] * no,
        out_shape=[jax.ShapeDtypeStruct((M, n_out), d) for d in out_dtypes],
        scratch_shapes=[] if row else [pltpu.VMEM((tm, K), F32)],
        compiler_params=_cparams(sem),
    )(dy, *([wg] * ns), *extras)


def mm_plain(a, b, *, name):
    M, K = a.shape
    N = b.shape[1]
    tm = min(M, 512)

    def body(a_ref, b_ref, o_ref):
        o_ref[...] = _dot(a_ref[...], b_ref[...])

    return pl.pallas_call(
        body, name=name, grid=(M // tm,),
        in_specs=[pl.BlockSpec((tm, K), lambda i: (i, 0)), pl.BlockSpec((K, N), lambda i: (0, 0))],
        out_specs=pl.BlockSpec((tm, N), lambda i: (i, 0)), out_shape=jax.ShapeDtypeStruct((M, N), F32),
        compiler_params=_cparams(("parallel",)),
    )(a, b)


def mm_tn(x, dy, g, t, *, name):
    off, K, ns, row = t
    S = x.shape[0]
    sb = off // K
    if row:
        x_map = lambda j, q: (0, j)
        dy_map = lambda j, q: (0, q)
    else:
        x_map = lambda j, q: (0, 0)
        dy_map = lambda j, q: (0, j * ns + q)

    def body(g_in, x_ref, dy_ref, o_ref):
        del g_in
        o_ref[...] = _dot_tn(x_ref[...], dy_ref[...]).astype(o_ref.dtype)

    return pl.pallas_call(
        body, name=name, grid=(N_CHIPS, ns),
        in_specs=[pl.BlockSpec(memory_space=pl.ANY), pl.BlockSpec((S, K), x_map), pl.BlockSpec((S, SLAB), dy_map)],
        out_specs=pl.BlockSpec((None, K, SLAB), lambda j, q: (j, sb + q, 0)),
        out_shape=jax.ShapeDtypeStruct(g.shape, g.dtype),
        input_output_aliases={0: 0},
        compiler_params=_cparams(("parallel", "parallel")),
    )(g, x, dy)


def ew(fn, ins, out_dtypes, *, name, bcast=()):
    S = ins[0].shape[0]
    tr = min(ROW_TILE, S)
    cols = ins[0].shape[1]
    ni, nb = len(ins), len(bcast)

    def body(*refs):
        res = fn(*[r[...] for r in refs[:ni + nb]])
        for o, r in zip(refs[ni + nb:], res):
            o[...] = r.astype(o.dtype)

    return pl.pallas_call(
        body, name=name, grid=(S // tr,),
        in_specs=[pl.BlockSpec((tr, a.shape[1]), lambda i: (i, 0)) for a in ins]
        + [pl.BlockSpec(b.shape, lambda i: (0, 0)) for b in bcast],
        out_specs=[pl.BlockSpec((tr, cols), lambda i: (i, 0)) for _ in out_dtypes],
        out_shape=[jax.ShapeDtypeStruct((S, cols), d) for d in out_dtypes],
        compiler_params=_cparams(("parallel",)),
    )(*ins, *bcast)


def _rstd(x):
    return lax.rsqrt(jnp.mean(x * x, axis=-1, keepdims=True) + RMS_EPS)


def _sigmoid(x):
    return 1.0 / (1.0 + jnp.exp(-x))


def _log_sigmoid(z):
    return jnp.minimum(z, 0.0) - jnp.log(1.0 + jnp.exp(-jnp.abs(z)))


def rms_fwd(x, g, *, name):
    return ew(lambda xv, gv: (xv * _rstd(xv) * gv,), [x], [BF16], name=name, bcast=[g])[0]


def _rms_bwd_tile(xv, gv, dh):
    rstd = _rstd(xv)
    xhat = xv * rstd
    gd = dh * gv
    dx = rstd * (gd - xhat * jnp.mean(xhat * gd, axis=-1, keepdims=True))
    return dx, jnp.sum(dh * xhat, axis=0, keepdims=True)


def rms_bwd(x, g, dh, dres, *, name):
    S, D = x.shape
    tr = min(ROW_TILE, S)

    def body(x_ref, g_ref, dh_ref, dres_ref, dx_ref, dxb_ref, dg_ref):
        i = pl.program_id(0)
        dx, dg = _rms_bwd_tile(x_ref[...], g_ref[...], dh_ref[...])
        dx = dx + dres_ref[...]
        dx_ref[...] = dx
        dxb_ref[...] = dx.astype(BF16)

        @pl.when(i == 0)
        def _():
            dg_ref[...] = dg

        @pl.when(i > 0)
        def _():
            dg_ref[...] += dg

    row = pl.BlockSpec((tr, D), lambda i: (i, 0))
    one = pl.BlockSpec((1, D), lambda i: (0, 0))
    return pl.pallas_call(
        body, name=name, grid=(S // tr,),
        in_specs=[row, one, row, row], out_specs=[row, row, one],
        out_shape=[jax.ShapeDtypeStruct((S, D), F32), jax.ShapeDtypeStruct((S, D), BF16),
                   jax.ShapeDtypeStruct((1, D), F32)],
        compiler_params=_cparams(("arbitrary",)),
    )(x, g, dh, dres)


def loss_head(x, g, target, *, name):
    S, D = x.shape
    tr = min(ROW_TILE, S)

    def body(x_ref, g_ref, t_ref, dx_ref, dg_ref, loss_ref):
        i = pl.program_id(0)
        xv, gv = x_ref[...], g_ref[...]
        err = xv * _rstd(xv) * gv - t_ref[...]
        part = 0.5 * jnp.sum(jnp.mean(err * err, axis=-1, keepdims=True), axis=0, keepdims=True)
        dx, dg = _rms_bwd_tile(xv, gv, err * (1.0 / D))
        dx_ref[...] = dx
        part = jnp.broadcast_to(part, loss_ref.shape)

        @pl.when(i == 0)
        def _():
            dg_ref[...] = dg
            loss_ref[...] = part

        @pl.when(i > 0)
        def _():
            dg_ref[...] += dg
            loss_ref[...] += part

    row = pl.BlockSpec((tr, D), lambda i: (i, 0))
    one = pl.BlockSpec((1, D), lambda i: (0, 0))
    return pl.pallas_call(
        body, name=name, grid=(S // tr,),
        in_specs=[row, one, row], out_specs=[row, one, pl.BlockSpec((1, 128), lambda i: (0, 0))],
        out_shape=[jax.ShapeDtypeStruct((S, D), F32), jax.ShapeDtypeStruct((1, D), F32),
                   jax.ShapeDtypeStruct((1, 128), F32)],
        compiler_params=_cparams(("arbitrary",)),
    )(x, g, target)


def rope_tables(S, n_cols):
    half = ROPE_DIM // 2
    inv_freq = ROPE_THETA ** (-jnp.arange(half, dtype=F32) / half)
    ang = jnp.arange(S, dtype=F32)[:, None] * inv_freq[None, :]
    cos, sin = jnp.cos(ang), jnp.sin(ang)
    z = jnp.zeros((S, HEAD_DIM - ROPE_DIM), F32)
    zh = jnp.zeros((S, half), F32)
    c = jnp.concatenate([cos, cos, jnp.ones_like(z)], axis=1)
    sa = jnp.concatenate([zh, sin, z], axis=1)
    sb = jnp.concatenate([-sin, zh, z], axis=1)
    return [jnp.tile(t, (1, n_cols // HEAD_DIM)) for t in (c, sa, sb)]


def rope_fwd(xqk, tables, *, name):
    n, half = xqk.shape[1], ROPE_DIM // 2

    def fn(x, c, sa, sb):
        return (x * c + pltpu.roll(x, half, 1) * sa + pltpu.roll(x, n - half, 1) * sb,)

    return ew(fn, [xqk] + list(tables), [BF16], name=name)[0]


def rope_bwd(dy, tables, *, name):
    n, half = dy.shape[1], ROPE_DIM // 2

    def fn(d, c, sa, sb):
        return (d * c + pltpu.roll(d * sa, n - half, 1) + pltpu.roll(d * sb, half, 1),)

    return ew(fn, [dy] + list(tables), [BF16], name=name)[0]


def _split3(x):
    h1 = x.astype(BF16)
    r1 = x - h1.astype(F32)
    h2 = r1.astype(BF16)
    return h1, h2, (r1 - h2.astype(F32)).astype(BF16)


def _split2(x):
    h1 = x.astype(BF16)
    return h1, (x - h1.astype(F32)).astype(BF16)


def _tri(n, cmp):
    r = lax.broadcasted_iota(jnp.int32, (n, n), 0)
    c = lax.broadcasted_iota(jnp.int32, (n, n), 1)
    return cmp(r, c).astype(BF16)


def fox_gate_fwd(fl, b, *, name):
    S, W = fl.shape
    tr = min(ROW_TILE, S)

    def body(fl_ref, b_ref, cum_ref, carry):
        i = pl.program_id(0)

        @pl.when(i == 0)
        def _():
            carry[...] = jnp.zeros_like(carry)

        lower = _tri(tr, lambda r, c: r >= c)
        cs = carry[...]
        for piece in _split3(_log_sigmoid(fl_ref[...] + b_ref[...])):
            cs = cs + _dot(lower, piece)
        cum_ref[...] = cs
        carry[...] = cs[tr - 1:tr, :]

    return pl.pallas_call(
        body, name=name, grid=(S // tr,),
        in_specs=[pl.BlockSpec((tr, W), lambda i: (i, 0)), pl.BlockSpec((1, W), lambda i: (0, 0))],
        out_specs=pl.BlockSpec((tr, W), lambda i: (i, 0)),
        out_shape=jax.ShapeDtypeStruct((S, W), F32),
        scratch_shapes=[pltpu.VMEM((1, W), F32)],
        compiler_params=_cparams(("arbitrary",)),
    )(fl, b)


def fox_gate_bwd(dcum, fl, b, *, name):
    S, W = fl.shape
    tr = min(ROW_TILE, S)
    nb = S // tr

    def body(dc_ref, fl_ref, b_ref, dfl_ref, db_ref, carry):
        i = pl.program_id(0)

        @pl.when(i == 0)
        def _():
            carry[...] = jnp.zeros_like(carry)

        upper = _tri(tr, lambda r, c: r <= c)
        cs = carry[...]
        for piece in _split3(dc_ref[...]):
            cs = cs + _dot(upper, piece)
        carry[...] = cs[0:1, :]
        dfl = cs * _sigmoid(-(fl_ref[...] + b_ref[...]))
        dfl_ref[...] = dfl
        db = jnp.sum(dfl, axis=0, keepdims=True)

        @pl.when(i == 0)
        def _():
            db_ref[...] = db

        @pl.when(i > 0)
        def _():
            db_ref[...] += db

    rev = pl.BlockSpec((tr, W), lambda i: (nb - 1 - i, 0))
    one = pl.BlockSpec((1, W), lambda i: (0, 0))
    return pl.pallas_call(
        body, name=name, grid=(nb,),
        in_specs=[rev, rev, one], out_specs=[rev, one],
        out_shape=[jax.ShapeDtypeStruct((S, W), F32), jax.ShapeDtypeStruct((1, W), F32)],
        scratch_shapes=[pltpu.VMEM((1, W), F32)],
        compiler_params=_cparams(("arbitrary",)),
    )(dcum, fl, b)


def _blk_iota(tq, tk):
    return (lax.broadcasted_iota(jnp.int32, (tq, tk), 0), lax.broadcasted_iota(jnp.int32, (tq, tk), 1))


def _cs(xb, tri):
    return _dot(xb, tri)


def _rowsum(xb):
    return jnp.sum(xb.astype(F32), axis=1, keepdims=True)


def _sb_block(qs, k, cmr, shift):
    z = _dot_nt(qs, k)
    strict = cmr < shift
    lb = jnp.minimum(z, 0.0) - jnp.log(1.0 + jnp.exp(-jnp.abs(z)))
    lom = jnp.where(strict, lb - z, 0.0).astype(BF16)
    return lb, lom, strict


def _att_tiles(S):
    return min(ATT_BQ, S), min(ATT_BK, S)


def _head_specs(S, tq):
    qspec = pl.BlockSpec((None, tq, HEAD_DIM), lambda h, i: (h, i, 0))
    kvspec = pl.BlockSpec((None, S, HEAD_DIM), lambda h, i: (h, 0, 0))
    vec = pl.BlockSpec((None, tq, 1), lambda h, i: (h, i, 0))
    return qspec, kvspec, vec


def sb_fwd(q, k, v, *, name):
    H, S, _ = q.shape
    tq, tk = _att_tiles(S)
    qspec, kvspec, vec = _head_specs(S, tq)

    def body(q_ref, k_ref, v_ref, o_ref, t_ref):
        i = pl.program_id(1)
        qs = q_ref[...] * ATTN_SCALE
        row, col = _blk_iota(tq, tk)
        cmr = col - row
        below = _tri(tk, lambda r, c: r > c)
        nkb = (i + 1) * (tq // tk)

        def step(n, carry):
            r_sum, acc = carry
            kb = nkb - 1 - n
            ks = pl.multiple_of(kb * tk, tk)
            lb, lom, strict = _sb_block(qs, k_ref[pl.ds(ks, tk), :], cmr, i * tq - kb * tk)
            w = jnp.where(strict, jnp.exp(lb + _cs(lom, below) + r_sum), 0.0)
            acc = acc + _dot(w.astype(BF16), v_ref[pl.ds(ks, tk), :])
            return r_sum + _rowsum(lom), acc

        r_sum, acc = lax.fori_loop(0, nkb, step, (jnp.zeros((tq, 1), F32), jnp.zeros((tq, HEAD_DIM), F32)))
        o_ref[...] = acc.astype(o_ref.dtype)
        t_ref[...] = r_sum

    return pl.pallas_call(
        body, name=name, grid=(H, S // tq),
        in_specs=[qspec, kvspec, kvspec], out_specs=[qspec, vec],
        out_shape=[jax.ShapeDtypeStruct((H, S, HEAD_DIM), BF16), jax.ShapeDtypeStruct((H, S, 1), F32)],
        compiler_params=_cparams(("parallel", "arbitrary")),
    )(q, k, v)


def sb_bwd(q, k, v, tot, do, *, name):
    H, S, _ = q.shape
    tq, tk = _att_tiles(S)
    qspec, kvspec, vec = _head_specs(S, tq)

    def body(q_ref, k_ref, v_ref, t_ref, do_ref, dq_ref, dk_out, dv_out, dk_ref, dv_ref):
        i = pl.program_id(1)

        @pl.when(i == 0)
        def _():
            dk_ref[...] = jnp.zeros_like(dk_ref)
            dv_ref[...] = jnp.zeros_like(dv_ref)

        qs, dov, t_all = q_ref[...] * ATTN_SCALE, do_ref[...], t_ref[...]
        row, col = _blk_iota(tq, tk)
        cmr = col - row
        upto = _tri(tk, lambda r, c: r <= c)
        before = _tri(tk, lambda r, c: r < c)

        def step(kb, carry):
            p_sum, e_sum, dq = carry
            ks = pl.multiple_of(kb * tk, tk)
            kv = k_ref[pl.ds(ks, tk), :]
            vv = v_ref[pl.ds(ks, tk), :]
            lb, lom, strict = _sb_block(qs, kv, cmr, i * tq - kb * tk)
            tail = t_all - p_sum - _cs(lom, upto)
            w = jnp.where(strict, jnp.exp(lb + tail), 0.0)
            e = _dot_nt(dov, vv) * w
            eb = e.astype(BF16)
            e_before = e_sum + _cs(eb, before)
            beta = jnp.exp(lb)
            dzb = jnp.where(strict, e - (e + e_before) * beta, 0.0).astype(BF16)
            dk_ref[pl.ds(ks, tk), :] += _dot_tn(dzb, qs)
            dv_ref[pl.ds(ks, tk), :] += _dot_tn(w.astype(BF16), dov)
            return (p_sum + _rowsum(lom), e_sum + _rowsum(eb),
                    dq + _dot(dzb, kv))

        zero = jnp.zeros((tq, 1), F32)
        _, _, dq = lax.fori_loop(0, (i + 1) * (tq // tk), step, (zero, zero, jnp.zeros((tq, HEAD_DIM), F32)))
        dq_ref[...] = (dq * ATTN_SCALE).astype(dq_ref.dtype)

        @pl.when(i == S // tq - 1)
        def _():
            dk_out[...] = dk_ref[...].astype(dk_out.dtype)
            dv_out[...] = dv_ref[...].astype(dv_out.dtype)

    full = jax.ShapeDtypeStruct((H, S, HEAD_DIM), BF16)
    return pl.pallas_call(
        body, name=name, grid=(H, S // tq),
        in_specs=[qspec, kvspec, kvspec, vec, qspec], out_specs=[qspec, kvspec, kvspec],
        out_shape=[full, full, full],
        scratch_shapes=[pltpu.VMEM((S, HEAD_DIM), F32)] * 2,
        compiler_params=_cparams(("parallel", "arbitrary")),
    )(q, k, v, tot, do)


def _fox_logits(qs, k, cq, ck, cmr, shift):
    causal = cmr <= shift
    return jnp.where(causal, _dot_nt(qs, k) + cq - ck, NEG_INF), causal


def fox_fwd(q, k, v, cq, ck, *, name):
    H, S, _ = q.shape
    tq, tk = _att_tiles(S)
    qspec, kvspec, vec = _head_specs(S, tq)
    ckspec = pl.BlockSpec((None, S // tk, 1, tk), lambda h, i: (h, 0, 0, 0))

    def body(q_ref, k_ref, v_ref, cq_ref, ck_ref, o_ref, lse_ref):
        i = pl.program_id(1)
        qs, cqv = q_ref[...] * ATTN_SCALE, cq_ref[...]
        row, col = _blk_iota(tq, tk)
        cmr = col - row

        def step(kb, carry):
            m, l, acc = carry
            ks = pl.multiple_of(kb * tk, tk)
            s, _ = _fox_logits(qs, k_ref[pl.ds(ks, tk), :], cqv, ck_ref[kb], cmr, i * tq - kb * tk)
            m_new = jnp.maximum(m, jnp.max(s, axis=1, keepdims=True))
            alpha = jnp.exp(m - m_new)
            p = jnp.exp(s - m_new)
            l = alpha * l + jnp.sum(p, axis=1, keepdims=True)
            acc = alpha * acc + _dot(p.astype(BF16), v_ref[pl.ds(ks, tk), :])
            return m_new, l, acc

        m, l, acc = lax.fori_loop(0, (i + 1) * (tq // tk), step,
                                  (jnp.full((tq, 1), NEG_INF, F32), jnp.zeros((tq, 1), F32), jnp.zeros((tq, HEAD_DIM), F32)))
        o_ref[...] = (acc / l).astype(o_ref.dtype)
        lse_ref[...] = m + jnp.log(l)

    return pl.pallas_call(
        body, name=name, grid=(H, S // tq),
        in_specs=[qspec, kvspec, kvspec, vec, ckspec], out_specs=[qspec, vec],
        out_shape=[jax.ShapeDtypeStruct((H, S, HEAD_DIM), BF16), jax.ShapeDtypeStruct((H, S, 1), F32)],
        compiler_params=_cparams(("parallel", "arbitrary")),
    )(q, k, v, cq, ck)


def fox_bwd(q, k, v, o, lse, cq, ck, do, *, name):
    H, S, _ = q.shape
    tq, tk = _att_tiles(S)
    qspec, kvspec, vec = _head_specs(S, tq)
    ckspec = pl.BlockSpec((None, S // tk, 1, tk), lambda h, i: (h, 0, 0, 0))

    def body(q_ref, k_ref, v_ref, o_ref, lse_ref, cq_ref, ck_ref, do_ref, dq_ref, dk_out, dv_out, dcq_ref, dck_ref,
             dk_ref, dv_ref):
        i = pl.program_id(1)

        @pl.when(i == 0)
        def _():
            dk_ref[...] = jnp.zeros_like(dk_ref)
            dv_ref[...] = jnp.zeros_like(dv_ref)
            dck_ref[...] = jnp.zeros_like(dck_ref)

        qs, dov, cqv, lsev = q_ref[...] * ATTN_SCALE, do_ref[...], cq_ref[...], lse_ref[...]
        delta = jnp.sum(dov.astype(F32) * o_ref[...].astype(F32), axis=1, keepdims=True)
        row, col = _blk_iota(tq, tk)
        cmr = col - row

        def step(kb, carry):
            dq, dcq = carry
            ks = pl.multiple_of(kb * tk, tk)
            kv = k_ref[pl.ds(ks, tk), :]
            vv = v_ref[pl.ds(ks, tk), :]
            s, causal = _fox_logits(qs, kv, cqv, ck_ref[kb], cmr, i * tq - kb * tk)
            p = jnp.where(causal, jnp.exp(s - lsev), 0.0)
            ds = p * (_dot_nt(dov, vv) - delta)
            dck_ref[kb] += jnp.sum(ds, axis=0, keepdims=True)
            dsb = ds.astype(BF16)
            dk_ref[pl.ds(ks, tk), :] += _dot_tn(dsb, qs)
            dv_ref[pl.ds(ks, tk), :] += _dot_tn(p.astype(BF16), dov)
            return dq + _dot(dsb, kv), dcq + jnp.sum(ds, axis=1, keepdims=True)

        dq, dcq = lax.fori_loop(0, (i + 1) * (tq // tk), step, (jnp.zeros((tq, HEAD_DIM), F32), jnp.zeros((tq, 1), F32)))
        dq_ref[...] = (dq * ATTN_SCALE).astype(dq_ref.dtype)
        dcq_ref[...] = dcq

        @pl.when(i == S // tq - 1)
        def _():
            dk_out[...] = dk_ref[...].astype(dk_out.dtype)
            dv_out[...] = dv_ref[...].astype(dv_out.dtype)

    full = jax.ShapeDtypeStruct((H, S, HEAD_DIM), BF16)
    return pl.pallas_call(
        body, name=name, grid=(H, S // tq),
        in_specs=[qspec, kvspec, kvspec, qspec, vec, vec, ckspec, qspec],
        out_specs=[qspec, kvspec, kvspec, vec, ckspec],
        out_shape=[full, full, full, jax.ShapeDtypeStruct((H, S, 1), F32),
                   jax.ShapeDtypeStruct((H, S // tk, 1, tk), F32)],
        scratch_shapes=[pltpu.VMEM((S, HEAD_DIM), F32)] * 2,
        compiler_params=_cparams(("parallel", "arbitrary")),
    )(q, k, v, o, lse, cq, ck, do)


def _swa_specs(S, tq):
    qspec = pl.BlockSpec((None, SWA_GROUP, tq, HEAD_DIM), lambda g, i: (g, 0, i, 0))
    kvspec = pl.BlockSpec((None, S + SWA_WINDOW, HEAD_DIM), lambda g, i: (g, 0, 0))
    vec = pl.BlockSpec((None, SWA_GROUP, tq, 1), lambda g, i: (g, 0, i, 0))
    sink = pl.BlockSpec((None, SWA_GROUP * tq, 1), lambda g, i: (g, 0, 0))
    return qspec, kvspec, vec, sink


def _swa_logits(q2, kw, i, tq):
    rows = q2.shape[0]
    r = lax.broadcasted_iota(jnp.int32, (rows, 2 * tq), 0)
    c = lax.broadcasted_iota(jnp.int32, (rows, 2 * tq), 1)
    diff = (r & (tq - 1)) + tq - c
    ok = (diff >= 0) & (diff < SWA_WINDOW) & (c + (i - 1) * tq >= 0)
    return jnp.where(ok, _dot_nt(q2, kw) * ATTN_SCALE, NEG_INF), ok


def swa_fwd(q, kp, vp, sink, *, name):
    _, G, S, _ = q.shape
    tq = ATT_BLK
    qspec, kvspec, vec, sinkspec = _swa_specs(S, tq)

    def body(q_ref, k_ref, v_ref, s_ref, o_ref, lse_ref):
        i = pl.program_id(1)
        q2 = q_ref[...].reshape(G * tq, HEAD_DIM)
        ws = pl.multiple_of(i * tq, tq)
        logits, _ = _swa_logits(q2, k_ref[pl.ds(ws, 2 * tq), :], i, tq)
        sk = s_ref[...]
        m = jnp.maximum(jnp.max(logits, axis=1, keepdims=True), sk)
        e = jnp.exp(logits - m)
        den = jnp.sum(e, axis=1, keepdims=True) + jnp.exp(sk - m)
        o = _dot((e / den).astype(BF16), v_ref[pl.ds(ws, 2 * tq), :])
        o_ref[...] = o.reshape(G, tq, HEAD_DIM).astype(o_ref.dtype)
        lse_ref[...] = (m + jnp.log(den)).reshape(G, tq, 1)

    return pl.pallas_call(
        body, name=name, grid=(SWA_KV_HEADS, S // tq),
        in_specs=[qspec, kvspec, kvspec, sinkspec], out_specs=[qspec, vec],
        out_shape=[jax.ShapeDtypeStruct(q.shape, BF16), jax.ShapeDtypeStruct((SWA_KV_HEADS, G, S, 1), F32)],
        compiler_params=_cparams(("parallel", "arbitrary")),
    )(q, kp, vp, sink)


def swa_bwd(q, kp, vp, sink, o, lse, do, *, name):
    _, G, S, _ = q.shape
    tq = ATT_BLK
    qspec, kvspec, vec, sinkspec = _swa_specs(S, tq)

    def body(q_ref, k_ref, v_ref, s_ref, o_ref, lse_ref, do_ref, dq_ref, dk_ref, dv_ref, dsink_ref):
        i = pl.program_id(1)

        @pl.when(i == 0)
        def _():
            dk_ref[...] = jnp.zeros_like(dk_ref)
            dv_ref[...] = jnp.zeros_like(dv_ref)

        q2 = q_ref[...].reshape(G * tq, HEAD_DIM)
        do2 = do_ref[...].reshape(G * tq, HEAD_DIM)
        o2 = o_ref[...].reshape(G * tq, HEAD_DIM)
        lse2 = lse_ref[...].reshape(G * tq, 1)
        ws = pl.multiple_of(i * tq, tq)
        kw = k_ref[pl.ds(ws, 2 * tq), :]
        vw = v_ref[pl.ds(ws, 2 * tq), :]
        logits, ok = _swa_logits(q2, kw, i, tq)
        p = jnp.where(ok, jnp.exp(logits - lse2), 0.0)
        delta = jnp.sum(do2.astype(F32) * o2.astype(F32), axis=1, keepdims=True)
        ds = p * (_dot_nt(do2, vw) - delta)
        dsb = ds.astype(BF16)
        dq_ref[...] = (_dot(dsb, kw) * ATTN_SCALE).reshape(G, tq, HEAD_DIM)
        dk_ref[pl.ds(ws, 2 * tq), :] += _dot_tn(dsb, q2) * ATTN_SCALE
        dv_ref[pl.ds(ws, 2 * tq), :] += _dot_tn(p.astype(BF16), do2)
        dsink_ref[...] = (-jnp.exp(s_ref[...] - lse2) * delta).reshape(G, tq, 1)

    kvshape = jax.ShapeDtypeStruct(kp.shape, F32)
    return pl.pallas_call(
        body, name=name, grid=(SWA_KV_HEADS, S // tq),
        in_specs=[qspec, kvspec, kvspec, sinkspec, qspec, vec, qspec],
        out_specs=[qspec, kvspec, kvspec, vec],
        out_shape=[jax.ShapeDtypeStruct(q.shape, F32), kvshape, kvshape,
                   jax.ShapeDtypeStruct((SWA_KV_HEADS, G, S, 1), F32)],
        compiler_params=_cparams(("parallel", "arbitrary")),
    )(q, kp, vp, sink, o, lse, do)


def _adamw_tile(w, g, m, v):
    m = ADAM_B1 * m + (1.0 - ADAM_B1) * g
    v = ADAM_B2 * v + (1.0 - ADAM_B2) * (g * g)
    m_hat = m / (1.0 - ADAM_B1 ** ADAM_STEP)
    v_hat = v / (1.0 - ADAM_B2 ** ADAM_STEP)
    delta = -ADAM_LR * (m_hat / (jnp.sqrt(v_hat) + ADAM_EPS) + ADAM_WD * w)
    return g, delta, m, v


def adamw(gfull, t, w, m, v, after, *, name):
    off, K, ns, _ = t
    sb = off // K
    nat = pl.BlockSpec((K, SLAB), lambda q: (0, q))

    def body(g_ref, w_ref, m_ref, v_ref, after_ref, *outs):
        del after_ref
        for o, r in zip(outs, _adamw_tile(w_ref[...], g_ref[...], m_ref[...], v_ref[...])):
            o[...] = r

    return pl.pallas_call(
        body, name=name, grid=(ns,),
        in_specs=[pl.BlockSpec((K, SLAB), lambda q: (sb + q, 0)), nat, nat, nat, HBM],
        out_specs=[nat] * 4, out_shape=[jax.ShapeDtypeStruct(w.shape, F32)] * 4,
        compiler_params=_cparams(("parallel",)),
    )(gfull, w, m, v, after)


def adamw_small(g, w, m, v, *, name):
    def body(g_ref, w_ref, m_ref, v_ref, *outs):
        for o, r in zip(outs, _adamw_tile(w_ref[...], g_ref[...], m_ref[...], v_ref[...])):
            o[...] = r

    return pl.pallas_call(body, name=name, out_shape=[jax.ShapeDtypeStruct(w.shape, F32)] * 4)(g, w, m, v)


MESH = pl.DeviceIdType.MESH
HBM = pl.BlockSpec(memory_space=pl.ANY)


def _place():
    x, y, c = lax.axis_index("x"), lax.axis_index("y"), lax.axis_index("c")
    others = [(1 - x, y), (x, 1 - y), (1 - x, 1 - y)]
    return x, y, c, others


def _rcopy(src, dst, send_sems, recv_sems, k, to):
    return pltpu.make_async_remote_copy(src_ref=src, dst_ref=dst, send_sem=send_sems.at[k], recv_sem=recv_sems.at[k],
                                        device_id=to, device_id_type=MESH)


def _dma_sems(*counts):
    return [pltpu.SemaphoreType.DMA((n,)) for n in counts]


DMA_UNIT_ROWS = 128
DMA_PIECES = 4
DMA_PIECES_LOCAL = 8


def _row_pieces(h, n):
    units = h // DMA_UNIT_ROWS
    n = min(n, units)
    base, extra = divmod(units, n)
    sizes = [(base + (k < extra)) * DMA_UNIT_ROWS for k in range(n)]
    return [(sum(sizes[:k]), sizes[k]) for k in range(n)]


def _start_pieces(make, h, n):
    for s0, sz in _row_pieces(h, n):
        make(s0, sz).start()
    return make(0, h)


SEM = pl.BlockSpec(memory_space=pltpu.SEMAPHORE)
SPLIT_COPY = pltpu.CompilerParams(has_side_effects=pltpu.SideEffectType.DATAFLOW_SIDE_EFFECTING)
N_OTHERS = 3


def _hbm(a):
    return pltpu.with_memory_space_constraint(a, pltpu.HBM)


def _chip_rows(buf, chip, s0, sz):
    return buf.at[2 * chip[0] + chip[1], pl.ds(s0, sz)]


def allgather_start(bufs, *, name):
    n = len(bufs)

    def body(*refs):
        ins, send, recv, token = refs[:n], refs[n:2 * n], refs[2 * n:3 * n], refs[4 * n]
        x, y, c, others = _place()
        for i in range(n):
            h = bufs[i].shape[1] // 2
            for f, chip in enumerate(others):
                for s0, sz in _row_pieces(h, DMA_PIECES):
                    mine = _chip_rows(ins[i], (x, y), c * h + s0, sz)
                    _rcopy(mine, mine, send[i], recv[i], f, (*chip, c)).start()
        token[...] = jnp.zeros_like(token)

    res = pl.pallas_call(
        body, name=name, in_specs=[HBM] * n,
        out_specs=[SEM] * (2 * n) + [HBM] * n + [pl.BlockSpec(memory_space=pltpu.VMEM)],
        out_shape=[pltpu.SemaphoreType.DMA((N_OTHERS,))] * (2 * n) + [pltpu.HBM(b.shape, b.dtype) for b in bufs]
        + [jax.ShapeDtypeStruct((1, D_MODEL), F32)],
        input_output_aliases={i: 2 * n + i for i in range(n)},
        compiler_params=SPLIT_COPY,
    )(*[_hbm(b) for b in bufs])
    return res[:n], res[n:2 * n], res[2 * n:3 * n], res[3 * n]


def allgather_wait(buf, send, recv, after, *, name):
    h = buf.shape[1] // 2

    def body(buf_ref, send_sems, recv_sems, after_ref, out_ref):
        del after_ref, out_ref
        x, y, c, others = _place()
        for f, chip in enumerate(others):
            mine = _chip_rows(buf_ref, (x, y), c * h, h)
            theirs = _chip_rows(buf_ref, chip, c * h, h)
            cp = _rcopy(mine, theirs, send_sems, recv_sems, f, (*chip, c))
            cp.wait_send()
            cp.wait_recv()

    return pl.pallas_call(
        body, name=name, in_specs=[HBM, SEM, SEM, HBM], out_specs=HBM,
        out_shape=pltpu.HBM(buf.shape, buf.dtype), input_output_aliases={0: 0},
        compiler_params=SPLIT_COPY,
    )(buf, send, recv, after)


def allgather_forward(buf, *, name):
    h = buf.shape[1] // 2

    def body(in_ref, out_ref, send_sems, recv_sems):
        del in_ref
        x, y, c, others = _place()
        sibling = (x, y, 1 - c)
        sent = []
        for f, chip in enumerate(others):
            sent.append(_start_pieces(
                lambda s0, sz: _rcopy(_chip_rows(out_ref, chip, c * h + s0, sz), _chip_rows(out_ref, chip, c * h + s0, sz),
                                      send_sems, recv_sems, f, sibling), h, DMA_PIECES))
        for f, chip in enumerate(others):
            blk = _chip_rows(out_ref, chip, (1 - c) * h, h)
            _rcopy(blk, blk, send_sems, recv_sems, f, sibling).wait_recv()
        for cp in sent:
            cp.wait_send()

    return pl.pallas_call(
        body, name=name, in_specs=[HBM], out_specs=HBM,
        out_shape=jax.ShapeDtypeStruct(buf.shape, buf.dtype), input_output_aliases={0: 0},
        scratch_shapes=_dma_sems(N_OTHERS, N_OTHERS),
    )(buf)


def swap_halves(grads, *, name):
    n = len(grads)

    def body(*refs):
        ins, theirs = refs[:n], refs[n:2 * n]
        send_sems, recv_sems = refs[2 * n:]
        x, y, c, _ = _place()
        for i in range(n):
            h = grads[i].shape[1] // 2
            for k in range(N_CHIPS):
                for s0, sz in _row_pieces(h, DMA_PIECES):
                    _rcopy(ins[i].at[k, pl.ds((1 - c) * h + s0, sz)], theirs[i].at[k, pl.ds(s0, sz)],
                           send_sems, recv_sems, i, (x, y, 1 - c)).start()
        for i in range(n):
            h = grads[i].shape[1] // 2
            _rcopy(ins[i].at[:, pl.ds((1 - c) * h, h)], theirs[i], send_sems, recv_sems, i, (x, y, 1 - c)).wait()

    return pl.pallas_call(
        body, name=name, in_specs=[HBM] * n, out_specs=[HBM] * n,
        out_shape=[jax.ShapeDtypeStruct((N_CHIPS, g.shape[1] // 2, SLAB), g.dtype) for g in grads],
        scratch_shapes=_dma_sems(n, n))(*grads)


def scatter_start(part, *, name):
    h = part.shape[1]

    def body(part_ref, land_ref, send, recv, part_out, land_out, token):
        del part_out, land_out
        x, y, c, others = _place()
        for f, chip in enumerate(others):
            for s0, sz in _row_pieces(h, DMA_PIECES):
                _rcopy(_chip_rows(part_ref, chip, s0, sz), land_ref.at[f, pl.ds(s0, sz)], send, recv, f, (*chip, c)).start()
        token[...] = jnp.zeros_like(token)

    land = lax.empty((N_OTHERS,) + part.shape[1:], part.dtype)
    return pl.pallas_call(
        body, name=name, in_specs=[HBM, HBM],
        out_specs=[SEM, SEM, HBM, HBM, pl.BlockSpec(memory_space=pltpu.VMEM)],
        out_shape=[pltpu.SemaphoreType.DMA((N_OTHERS,))] * 2 + [pltpu.HBM(part.shape, part.dtype), pltpu.HBM(land.shape, land.dtype),
                                                                 jax.ShapeDtypeStruct((1, D_MODEL), F32)],
        input_output_aliases={0: 2, 1: 3},
        compiler_params=SPLIT_COPY,
    )(_hbm(part), _hbm(land))


def scatter_wait(part, land, send, recv, after, *, name):
    h = part.shape[1]

    def body(part_ref, land_ref, send_sems, recv_sems, after_ref, part_out, land_out):
        del after_ref, part_out, land_out
        x, y, c, others = _place()
        for f, chip in enumerate(others):
            cp = _rcopy(_chip_rows(part_ref, chip, 0, h), land_ref.at[f], send_sems, recv_sems, f, (*chip, c))
            cp.wait_send()
            cp.wait_recv()

    return pl.pallas_call(
        body, name=name, in_specs=[HBM, HBM, SEM, SEM, HBM], out_specs=[HBM, HBM],
        out_shape=[pltpu.HBM(part.shape, part.dtype), pltpu.HBM(land.shape, land.dtype)],
        input_output_aliases={0: 0, 1: 1},
        compiler_params=SPLIT_COPY,
    )(part, land, send, recv, after)


def join_halves(bufs, *, name):
    n = len(bufs)

    def body(*refs):
        outs = refs[n:2 * n]
        send_sems, recv_sems = refs[2 * n:]
        x, y, c, _ = _place()
        sibling = (x, y, 1 - c)
        cps = []
        for i in range(n):
            h = bufs[i].shape[0] // 2
            snd = _start_pieces(
                lambda s0, sz: _rcopy(outs[i].at[pl.ds(c * h + s0, sz)], outs[i].at[pl.ds(c * h + s0, sz)],
                                      send_sems, recv_sems, i, sibling), h, 2 * DMA_PIECES_LOCAL)
            theirs = outs[i].at[pl.ds((1 - c) * h, h)]
            cps.append((snd, _rcopy(theirs, theirs, send_sems, recv_sems, i, sibling)))
        for snd, rcv in cps:
            snd.wait_send()
            rcv.wait_recv()

    return pl.pallas_call(
        body, name=name, in_specs=[HBM] * n, out_specs=[HBM] * n,
        out_shape=[jax.ShapeDtypeStruct(b.shape, b.dtype) for b in bufs],
        input_output_aliases={i: i for i in range(n)},
        scratch_shapes=_dma_sems(n, n),
    )(*bufs)


def allreduce_small(v, *, name):
    rows, n = v.shape

    def body(x_ref, sum_ref, all_ref, send_sems, recv_sems, local_sem):
        x, y, c, others = _place()
        me, sibling = (x, y, c), (x, y, 1 - c)

        def blk(px, py, pc):
            return all_ref.at[pl.ds((4 * px + 2 * py + pc) * rows, rows), :]

        def copy(k, block, to, src=None):
            return _rcopy(blk(*block) if src is None else src, blk(*block), send_sems, recv_sems, k, to)

        mine = pltpu.make_async_copy(x_ref, blk(*me), local_sem)
        mine.start()
        first = [copy(0, me, sibling, src=x_ref)]
        first += [copy(1 + f, me, (*chip, c), src=x_ref) for f, chip in enumerate(others)]
        for cp in first:
            cp.start()
        passed = [copy(4 + f, (*chip, c), sibling) for f, chip in enumerate(others)]
        for f, chip in enumerate(others):
            copy(1 + f, (*chip, c), me).wait_recv()
            passed[f].start()
        copy(0, sibling, me).wait_recv()
        for f, chip in enumerate(others):
            copy(4 + f, (*chip, 1 - c), me).wait_recv()
        for cp in first + passed:
            cp.wait_send()
        mine.wait()
        acc = all_ref[pl.ds(0, rows), :]
        for d in range(1, N_DEVICES):
            acc = acc + all_ref[pl.ds(d * rows, rows), :]
        sum_ref[...] = acc

    vm = pl.BlockSpec(memory_space=pltpu.VMEM)
    return pl.pallas_call(
        body, name=name, in_specs=[vm], out_specs=[vm, vm],
        out_shape=[jax.ShapeDtypeStruct((rows, n), F32), jax.ShapeDtypeStruct((N_DEVICES * rows, n), F32)],
        scratch_shapes=_dma_sems(7, 7) + [pltpu.SemaphoreType.DMA],
    )(v)[0]


def add_pairs(grad, theirs, where, *, name):
    h = theirs.shape[1]
    spec = pl.BlockSpec((None, h, SLAB), lambda k, w: (k, 0, 0))

    def body(w_ref, a_ref, b_ref, o_ref):
        del w_ref
        o_ref[...] = (a_ref[...].astype(F32) + b_ref[...].astype(F32)).astype(o_ref.dtype)

    return pl.pallas_call(
        body, name=name,
        grid_spec=pltpu.PrefetchScalarGridSpec(
            num_scalar_prefetch=1, grid=(N_CHIPS,),
            in_specs=[pl.BlockSpec((None, h, SLAB), lambda k, w: (k, w[1], 0)), spec], out_specs=spec),
        out_shape=jax.ShapeDtypeStruct(theirs.shape, theirs.dtype),
        compiler_params=_cparams(("parallel",)))(where, grad, theirs)


def add_chips(pair, got, where, *, name):
    h = pair.shape[1]
    tr = h // 2

    def body(w_ref, a_ref, b_ref, o_ref):
        del w_ref
        acc = a_ref[...].astype(F32)
        for f in range(3):
            acc = acc + b_ref[f].astype(F32)
        o_ref[...] = acc

    return pl.pallas_call(
        body, name=name,
        grid_spec=pltpu.PrefetchScalarGridSpec(
            num_scalar_prefetch=1, grid=(2,),
            in_specs=[pl.BlockSpec((None, tr, SLAB), lambda i, w: (w[0], i, 0)),
                      pl.BlockSpec((3, tr, SLAB), lambda i, w: (0, i, 0))],
            out_specs=pl.BlockSpec((tr, SLAB), lambda i, w: (2 * w[1] + i, 0))),
        out_shape=jax.ShapeDtypeStruct((2 * h, SLAB), F32),
        compiler_params=_cparams(("parallel",)))(where, pair, got)


DEPTH = 4
MIXER = (0, 1, 2, 0)
W_IN_COLS = (768, 320, 772)
W_IN_PAD = (768, 512, 1024)
MATS = ("up", "down", "inp", "out", "gate", "proj")
MAT_ARG = dict(up="w_up", down="w_down", inp="w_in", out="w_out", gate="w_ple_gate", proj="w_ple_proj")
GAINS = ("attn_norm", "mlp_norm", "ple_norm")
N_SMALL = 16
KINDS = ("grad_", "delta_", "new_m_", "new_v_")


def _layout(kind):
    ns_in = W_IN_PAD[kind] // SLAB
    off = 8192 + 1024 * ns_in
    lay = dict(up=(0, 1024, 4, False), down=(4096, 1024, 4, True), inp=(8192, 1024, ns_in, False),
               out=(off, 256, 4, True), gate=(off + 1024, 256, 4, True), proj=(off + 2048, 256, 1, False))
    return lay, off + 2304


def _to_slabs(w):
    k, c = w.shape
    return w.reshape(k, c // SLAB, SLAB).transpose(1, 0, 2).reshape(-1, SLAB)


def _pad_cols(w, n):
    return jnp.pad(w, ((0, 0), (0, n - w.shape[1])))


def _heads(x2d, n):
    return x2d.reshape(x2d.shape[0], n, HEAD_DIM).transpose(1, 0, 2)


def _unheads(x3d):
    n, s, _ = x3d.shape
    return x3d.transpose(1, 0, 2).reshape(s, n * HEAD_DIM)


def _chip_cols(x2d, c, cpad):
    s = x2d.shape[0]
    return jnp.pad(x2d.reshape(s, N_CHIPS, c), ((0, 0), (0, 0), (0, cpad - c))).reshape(s, N_CHIPS * cpad)


def _unchip_cols(x2d, c, cpad):
    s = x2d.shape[0]
    return x2d.reshape(s, N_CHIPS, cpad)[:, :, :c].reshape(s, N_CHIPS * c)


def _forget_cols(wg, t):
    off, K, _, _ = t
    cols = []
    for g in range(3 * N_HEADS * HEAD_DIM, 3 * N_HEADS * HEAD_DIM + N_HEADS):
        chip, local = divmod(g, W_IN_COLS[2])
        q, c = divmod(local, SLAB)
        cols.append(wg[chip, off + q * K:off + (q + 1) * K, c:c + 1])
    return jnp.concatenate(cols, axis=1)


def _add_res(acc, res):
    return (acc + res,)


def _relu2(acc):
    return acc, jnp.square(jnp.maximum(acc, 0.0))


def _relu2_bwd(acc, u):
    return (acc * (2.0 * jnp.maximum(u.astype(F32), 0.0)),)


def _ple_fwd(acc, x2, pp):
    return x2 + pp * _sigmoid(acc), acc


def _ple_bwd(dx, pp, gl):
    gate = _sigmoid(gl)
    return dx * gate, dx * pp * gate * (1.0 - gate)


def _layer_fwd(i, kind, x0, p_bf, wg, lay, gains, extra, tabs):
    s = x0.shape[0]
    an, mn, pn = gains
    sv = dict(x0=x0)
    h1 = rms_fwd(x0, an, name=f"attn_norm_{i}")
    if kind == 0:
        proj = mm_nn(h1, wg, lay["inp"], name=f"w_in_{i}")[0]
        qkv = proj.reshape(s, 3, N_HEADS, HEAD_DIM).transpose(1, 2, 0, 3)
        o, tot = sb_fwd(qkv[0], qkv[1], qkv[2], name=f"sb_fwd_{i}")
        sv.update(qkv=qkv, tot=tot)
    elif kind == 1:
        projp = mm_nn(h1, wg, lay["inp"], name=f"w_in_{i}", out_dtypes=(F32,))[0]
        proj = _unchip_cols(projp, W_IN_COLS[1], W_IN_PAD[1])
        nq = N_HEADS * HEAD_DIM
        nqk = nq + SWA_KV_HEADS * HEAD_DIM
        qk = rope_fwd(proj[:, :nqk], tabs, name=f"rope_{i}")
        q = _heads(qk[:, :nq], N_HEADS).reshape(SWA_KV_HEADS, SWA_GROUP, s, HEAD_DIM)
        front = ((0, 0), (SWA_WINDOW, 0), (0, 0))
        kp = jnp.pad(_heads(qk[:, nq:], SWA_KV_HEADS), front)
        vp = jnp.pad(_heads(proj[:, nqk:].astype(BF16), SWA_KV_HEADS), front)
        sink = jnp.repeat(extra.reshape(SWA_KV_HEADS, SWA_GROUP), ATT_BLK, axis=1)[:, :, None]
        o4, lse = swa_fwd(q, kp, vp, sink, name=f"swa_fwd_{i}")
        o = o4.reshape(N_HEADS, s, HEAD_DIM)
        sv.update(q=q, kp=kp, vp=vp, sink=sink, o4=o4, lse=lse)
    else:
        projp = mm_nn(h1, wg, lay["inp"], name=f"w_in_{i}")[0]
        nqkv = 3 * N_HEADS * HEAD_DIM
        qkv = _unchip_cols(projp, W_IN_COLS[2], W_IN_PAD[2])[:, :nqkv].reshape(s, 3, N_HEADS, HEAD_DIM).transpose(1, 2, 0, 3)
        fl = mm_plain(h1, _pad_cols(_forget_cols(wg, lay["inp"]), 128), name=f"w_forget_{i}")
        bp = _pad_cols(extra[None], 128)
        cum_t = fox_gate_fwd(fl, bp, name=f"gate_fwd_{i}")[:, :N_HEADS].T
        cq = cum_t[:, :, None]
        ck = cum_t.reshape(N_HEADS, s // min(ATT_BK, s), 1, min(ATT_BK, s))
        o, lse = fox_fwd(qkv[0], qkv[1], qkv[2], cq, ck, name=f"fox_fwd_{i}")
        sv.update(qkv=qkv, fl=fl, bp=bp, cq=cq, ck=ck, o=o, lse=lse)
    a = _unheads(o)
    x1 = mm_nn(a, wg, lay["out"], name=f"w_out_{i}", epi=_add_res, extras=(x0,), out_dtypes=(F32,))[0]
    h2 = rms_fwd(x1, mn, name=f"mlp_norm_{i}")
    u, r = mm_nn(h2, wg, lay["up"], name=f"w_up_{i}", epi=_relu2, out_dtypes=(BF16, BF16))
    x2 = mm_nn(r, wg, lay["down"], name=f"w_down_{i}", epi=_add_res, extras=(x1,), out_dtypes=(F32,))[0]
    h3 = rms_fwd(x2, pn, name=f"ple_norm_{i}")
    pp = mm_nn(p_bf, wg, lay["proj"], name=f"w_ple_proj_{i}", out_dtypes=(F32,))[0]
    x3, gl = mm_nn(h3, wg, lay["gate"], name=f"w_ple_gate_{i}", epi=_ple_fwd, extras=(x2, pp), out_dtypes=(F32, F32))
    sv.update(h1=h1, a=a, x1=x1, h2=h2, u=u, r=r, x2=x2, h3=h3, pp=pp, gl=gl)
    return x3, sv


def _layer_bwd(i, kind, dx3, sv, p_bf, wg, lay, n_rows, gains, tabs):
    s = dx3.shape[0]
    an, mn, pn = gains
    g = lax.empty((N_CHIPS, n_rows, SLAB), BF16)
    d_pp, d_gl = ew(_ple_bwd, [dx3, sv["pp"], sv["gl"]], [BF16, BF16], name=f"ple_bwd_{i}")
    g = mm_tn(p_bf, d_pp, g, lay["proj"], name=f"dw_ple_proj_{i}")
    g = mm_tn(sv["h3"], d_gl, g, lay["gate"], name=f"dw_ple_gate_{i}")
    d_h3 = mm_nt(d_gl, wg, lay["gate"], name=f"dx_ple_gate_{i}", out_dtypes=(F32,))[0]
    dx2, dx2b, d_pn = rms_bwd(sv["x2"], pn, d_h3, dx3, name=f"ple_norm_bwd_{i}")
    g = mm_tn(sv["r"], dx2b, g, lay["down"], name=f"dw_down_{i}")
    d_u = mm_nt(dx2b, wg, lay["down"], name=f"dx_down_{i}", epi=_relu2_bwd, extras=(sv["u"],))[0]
    g = mm_tn(sv["h2"], d_u, g, lay["up"], name=f"dw_up_{i}")
    d_h2 = mm_nt(d_u, wg, lay["up"], name=f"dx_up_{i}", out_dtypes=(F32,))[0]
    dx1, dx1b, d_mn = rms_bwd(sv["x1"], mn, d_h2, dx2, name=f"mlp_norm_bwd_{i}")
    g = mm_tn(sv["a"], dx1b, g, lay["out"], name=f"dw_out_{i}")
    d_a = mm_nt(dx1b, wg, lay["out"], name=f"dx_out_{i}")[0]
    do = _heads(d_a, N_HEADS)
    d_extra = None
    if kind == 0:
        qkv = sv["qkv"]
        dq, dk, dv = sb_bwd(qkv[0], qkv[1], qkv[2], sv["tot"], do, name=f"sb_bwd_{i}")
        d_proj = jnp.stack([dq, dk, dv]).transpose(2, 0, 1, 3).reshape(s, 3 * N_HEADS * HEAD_DIM)
    elif kind == 1:
        do4 = do.reshape(SWA_KV_HEADS, SWA_GROUP, s, HEAD_DIM)
        dq, dkp, dvp, dsr = swa_bwd(sv["q"], sv["kp"], sv["vp"], sv["sink"], sv["o4"], sv["lse"], do4, name=f"swa_bwd_{i}")
        dqk = jnp.concatenate([_unheads(dq.reshape(N_HEADS, s, HEAD_DIM)), _unheads(dkp[:, SWA_WINDOW:])], axis=1)
        dqk = rope_bwd(dqk, tabs, name=f"rope_bwd_{i}")
        d_proj = jnp.concatenate([dqk, _unheads(dvp[:, SWA_WINDOW:]).astype(BF16)], axis=1)
        d_proj = _chip_cols(d_proj, W_IN_COLS[1], W_IN_PAD[1])
        d_extra = jnp.sum(dsr[..., 0], axis=2).reshape(N_HEADS)
    else:
        qkv = sv["qkv"]
        dq, dk, dv, dcq, dck = fox_bwd(qkv[0], qkv[1], qkv[2], sv["o"], sv["lse"], sv["cq"], sv["ck"], do, name=f"fox_bwd_{i}")
        dcum = _pad_cols((dcq[:, :, 0] - dck.reshape(N_HEADS, s)).T, 128)
        dfl, dbp = fox_gate_bwd(dcum, sv["fl"], sv["bp"], name=f"gate_bwd_{i}")
        d_qkv = jnp.stack([dq, dk, dv]).transpose(2, 0, 1, 3).reshape(s, 3 * N_HEADS * HEAD_DIM)
        d_proj = jnp.concatenate([d_qkv, dfl[:, :N_HEADS].astype(BF16)], axis=1)
        d_proj = _chip_cols(d_proj, W_IN_COLS[2], W_IN_PAD[2])
        d_extra = dbp[0, :N_HEADS]
    g = mm_tn(sv["h1"], d_proj, g, lay["inp"], name=f"dw_in_{i}")
    d_h1 = mm_nt(d_proj, wg, lay["inp"], name=f"dx_in_{i}", out_dtypes=(F32,))[0]
    dx0, _, d_an = rms_bwd(sv["x0"], an, d_h1, dx1, name=f"attn_norm_bwd_{i}")
    return dx0, g, (d_an, d_mn, d_pn), d_extra


def _small_rows(a, prefix):
    rows = [a[f"{prefix}{n}_{i}"] for i in range(DEPTH) for n in GAINS] + [a[f"{prefix}final_norm"]]
    rows += [_pad_cols(a[f"{prefix}{n}"][None], D_MODEL)[0] for n in ("sinks_1", "b_forget_2")]
    return jnp.stack(rows + [jnp.zeros((D_MODEL,), F32)])


def _train_step(a):
    x = a["x"][0]
    tabs = rope_tables(x.shape[0], (N_HEADS + SWA_KV_HEADS) * HEAD_DIM)
    lays = [_layout(k) for k in MIXER]

    def natural(prefix, i, m):
        w = a[f"{prefix}{MAT_ARG[m]}_{i}"]
        return _pad_cols(w, W_IN_PAD[MIXER[i]]) if m == "inp" else w

    chip = 2 * lax.axis_index("x") + lax.axis_index("y")
    where = jnp.stack([chip, lax.axis_index("c")]).astype(jnp.int32)
    def own_block(i, zero):
        pk = jnp.concatenate([_to_slabs((natural("", i, m) + zero).astype(BF16)) for m in MATS], axis=0)
        return lax.dynamic_update_slice(lax.empty((N_CHIPS,) + pk.shape, BF16), pk[None], (chip, 0, 0))

    sends, recvs, bufs, token = allgather_start([own_block(0, 0.0)], name="allgather_start_0")
    more = allgather_start([own_block(i, token[0, 0]) for i in range(1, DEPTH)], name="allgather_start_1")
    sends, recvs, bufs, token = sends + more[0], recvs + more[1], bufs + more[2], more[3]

    gains = [tuple(a[f"{n}_{i}"][None] for n in GAINS) for i in range(DEPTH)]
    extras = [None, a["sinks_1"], a["b_forget_2"], None]
    p_bf = [a["p"][i, 0].astype(BF16) for i in range(DEPTH)]

    saved, wgs, after = [], [], token
    for i in range(DEPTH):
        landed = allgather_wait(bufs[i], sends[i], recvs[i], after, name=f"allgather_wait_{i}")
        wgs.append(allgather_forward(landed, name=f"allgather_forward_{i}"))
        x, sv = _layer_fwd(i, MIXER[i], x, p_bf[i], wgs[i], lays[i][0], gains[i], extras[i], tabs)
        saved.append(sv)
        after = x
    dx, d_final, loss = loss_head(x, a["final_norm"][None], a["loss_target"][0], name="loss_head")

    def finish(i, started, after):
        send, recv, part, land, _ = started
        part, got = scatter_wait(part, land, send, recv, after, name=f"scatter_wait_{i}")
        return join_halves([add_chips(part, got, where, name=f"add_chips_{i}")], name=f"join_halves_{i}")[0]

    small = [None] * N_SMALL
    small[12] = d_final[0]
    small[15] = _pad_cols(loss[:, :1], D_MODEL)[0]
    gfull = [None] * DEPTH
    started = None
    for i in reversed(range(DEPTH)):
        an, mn, pn = gains[i]
        if started is not None:
            pn = pn + started[4]
        dx, grad, d_gains, d_extra = _layer_bwd(i, MIXER[i], dx, saved[i], p_bf[i], wgs[i], lays[i][0], lays[i][1],
                                                (an, mn, pn), tabs)
        for j in range(3):
            small[3 * i + j] = d_gains[j][0]
        if d_extra is not None:
            small[12 + MIXER[i]] = _pad_cols(d_extra[None], D_MODEL)[0]
        if started is not None:
            gfull[i + 1] = finish(i + 1, started, dx)
        theirs = swap_halves([grad], name=f"swap_halves_{i}")[0]
        started = scatter_start(add_pairs(grad, theirs, where, name=f"add_pairs_{i}"), name=f"scatter_start_{i}")
    small = allreduce_small(jnp.stack(small), name="allreduce_small")

    out = {"loss": small[15, 0], "grad_x": dx[None]}
    res = adamw_small(small, _small_rows(a, ""), _small_rows(a, "m_"), _small_rows(a, "v_"), name="adamw_small")
    for i in reversed(range(DEPTH)):
        if i == 0:
            gfull[0] = finish(0, started, out[f"delta_{MAT_ARG[MATS[-1]]}_1"])
        for m in MATS:
            upd = adamw(gfull[i], lays[i][0][m], natural("", i, m), natural("m_", i, m), natural("v_", i, m), started[4],
                        name=f"adamw_{MAT_ARG[m]}_{i}")
            cols = a[f"{MAT_ARG[m]}_{i}"].shape[1]
            for kd, r in zip(KINDS, upd):
                out[f"{kd}{MAT_ARG[m]}_{i}"] = r[:, :cols]
    for kd, r in zip(KINDS, res):
        for i in range(DEPTH):
            for j, n in enumerate(GAINS):
                out[f"{kd}{n}_{i}"] = r[3 * i + j]
        out[f"{kd}final_norm"] = r[12]
        out[f"{kd}sinks_1"] = r[13, :N_HEADS]
        out[f"{kd}b_forget_2"] = r[14, :N_HEADS]
    return out


def _weight_names():
    names = []
    for i in range(DEPTH):
        names += [f"attn_norm_{i}", f"w_in_{i}", f"w_out_{i}"] + [[], ["sinks_1"], ["b_forget_2"]][MIXER[i]]
        names += [f"mlp_norm_{i}", f"w_up_{i}", f"w_down_{i}", f"ple_norm_{i}", f"w_ple_gate_{i}", f"w_ple_proj_{i}"]
    return names + ["final_norm"]


def kernel(x, p, attn_norm_0, w_in_0, w_out_0, mlp_norm_0, w_up_0, w_down_0, ple_norm_0, w_ple_gate_0, w_ple_proj_0, attn_norm_1, w_in_1, w_out_1, sinks_1, mlp_norm_1, w_up_1, w_down_1, ple_norm_1, w_ple_gate_1, w_ple_proj_1, attn_norm_2, w_in_2, w_out_2, b_forget_2, mlp_norm_2, w_up_2, w_down_2, ple_norm_2, w_ple_gate_2, w_ple_proj_2, attn_norm_3, w_in_3, w_out_3, mlp_norm_3, w_up_3, w_down_3, ple_norm_3, w_ple_gate_3, w_ple_proj_3, final_norm, loss_target, m_attn_norm_0, m_w_in_0, m_w_out_0, m_mlp_norm_0, m_w_up_0, m_w_down_0, m_ple_norm_0, m_w_ple_gate_0, m_w_ple_proj_0, m_attn_norm_1, m_w_in_1, m_w_out_1, m_sinks_1, m_mlp_norm_1, m_w_up_1, m_w_down_1, m_ple_norm_1, m_w_ple_gate_1, m_w_ple_proj_1, m_attn_norm_2, m_w_in_2, m_w_out_2, m_b_forget_2, m_mlp_norm_2, m_w_up_2, m_w_down_2, m_ple_norm_2, m_w_ple_gate_2, m_w_ple_proj_2, m_attn_norm_3, m_w_in_3, m_w_out_3, m_mlp_norm_3, m_w_up_3, m_w_down_3, m_ple_norm_3, m_w_ple_gate_3, m_w_ple_proj_3, m_final_norm, v_attn_norm_0, v_w_in_0, v_w_out_0, v_mlp_norm_0, v_w_up_0, v_w_down_0, v_ple_norm_0, v_w_ple_gate_0, v_w_ple_proj_0, v_attn_norm_1, v_w_in_1, v_w_out_1, v_sinks_1, v_mlp_norm_1, v_w_up_1, v_w_down_1, v_ple_norm_1, v_w_ple_gate_1, v_w_ple_proj_1, v_attn_norm_2, v_w_in_2, v_w_out_2, v_b_forget_2, v_mlp_norm_2, v_w_up_2, v_w_down_2, v_ple_norm_2, v_w_ple_gate_2, v_w_ple_proj_2, v_attn_norm_3, v_w_in_3, v_w_out_3, v_mlp_norm_3, v_w_up_3, v_w_down_3, v_ple_norm_3, v_w_ple_gate_3, v_w_ple_proj_3, v_final_norm):
    out = _train_step(dict(locals()))
    return (out["loss"], out["grad_x"], *[out[kd + n] for kd in KINDS for n in _weight_names()])
```

```python
import jax
import jax.numpy as jnp
from jax import lax
from jax.experimental import pallas as pl
from jax.experimental.pallas import tpu as pltpu

F32 = jnp.float32
BF16 = jnp.bfloat16

D_MODEL = 1024
N_HEADS = 16
HEAD_DIM = 64
SWA_KV_HEADS = 2
SWA_GROUP = 8
SWA_WINDOW = 128
ROPE_THETA = 500000.0
ROPE_DIM = 16
RMS_EPS = 1e-6
NEG_INF = -1e30
ATTN_SCALE = HEAD_DIM ** -0.5
N_CHIPS = 4
N_DEVICES = 8

SLAB = 256
ATT_BLK = 128
ATT_BQ = 512
ATT_BK = 512
ROW_TILE = 256
V7X_VMEM_LIMIT = 56 * 1024 * 1024

ADAM_LR, ADAM_B1, ADAM_B2, ADAM_EPS, ADAM_WD, ADAM_STEP = 0.001, 0.9, 0.999, 1e-08, 0.01, 10


def _cparams(sem=None):
    return pltpu.CompilerParams(dimension_semantics=sem, vmem_limit_bytes=V7X_VMEM_LIMIT)


def _dot(a, b):
    return jnp.dot(a, b, preferred_element_type=F32)


def _dot_nt(a, b):
    return lax.dot_general(a, b, (((1,), (1,)), ((), ())), preferred_element_type=F32)


def _dot_tn(a, b):
    return lax.dot_general(a, b, (((0,), (0,)), ((), ())), preferred_element_type=F32)


def _row_tile(M, K):
    return min(M, 1024) if K >= 1024 else M


def _finish(epi, acc, ex, outs):
    res = epi(acc, *[e[...] for e in ex]) if epi is not None else (acc,)
    for o, r in zip(outs, res):
        o[...] = r.astype(o.dtype)


def mm_nn(a, wg, t, *, name, epi=None, extras=(), out_dtypes=(BF16,)):
    off, K, ns, row = t
    M = a.shape[0]
    sb = off // K
    ne, no = len(extras), len(out_dtypes)
    if row:
        tm = _row_tile(M, K)
        nb = N_CHIPS
        grid = (M // tm, ns)
        a_spec = pl.BlockSpec((tm, N_CHIPS * K), lambda i, q: (i, 0))
        b_specs = [pl.BlockSpec((None, K, SLAB), lambda i, q, j=j: (j, sb + q, 0)) for j in range(nb)]
        tile = pl.BlockSpec((tm, SLAB), lambda i, q: (i, q))
        n_out = ns * SLAB
    else:
        nb = 1
        grid = (N_CHIPS, ns)
        a_spec = pl.BlockSpec((M, K), lambda j, q: (0, 0))
        b_specs = [pl.BlockSpec((None, K, SLAB), lambda j, q: (j, sb + q, 0))]
        tile = pl.BlockSpec((M, SLAB), lambda j, q: (0, j * ns + q))
        n_out = N_CHIPS * ns * SLAB

    def body(a_ref, *rest):
        bs, ex, outs = rest[:nb], rest[nb:nb + ne], rest[nb + ne:]
        acc = _dot(a_ref[:, pl.ds(0, K)], bs[0][...])
        for j in range(1, nb):
            acc = acc + _dot(a_ref[:, pl.ds(j * K, K)], bs[j][...])
        _finish(epi, acc, ex, outs)

    return pl.pallas_call(
        body, name=name, grid=grid,
        in_specs=[a_spec] + b_specs + [tile] * ne, out_specs=[tile] * no,
        out_shape=[jax.ShapeDtypeStruct((M, n_out), d) for d in out_dtypes],
        compiler_params=_cparams(("parallel", "parallel")),
    )(a, *([wg] * nb), *extras)


def mm_nt(dy, wg, t, *, name, epi=None, extras=(), out_dtypes=(BF16,)):
    off, K, ns, row = t
    M = dy.shape[0]
    tm = _row_tile(M, K)
    sb = off // K
    ne, no = len(extras), len(out_dtypes)
    grid = (M // tm, N_CHIPS)
    b_specs = [pl.BlockSpec((None, K, SLAB), lambda i, j, q=q: (j, sb + q, 0)) for q in range(ns)]
    if row:
        dy_spec = pl.BlockSpec((tm, ns * SLAB), lambda i, j: (i, 0))
        tile = pl.BlockSpec((tm, K), lambda i, j: (i, j))
        n_out = N_CHIPS * K
        sem = ("parallel", "parallel")
    else:
        dy_spec = pl.BlockSpec((tm, ns * SLAB), lambda i, j: (i, j))
        tile = pl.BlockSpec((tm, K), lambda i, j: (i, 0))
        n_out = K
        sem = ("parallel", "arbitrary")

    def body(dy_ref, *rest):
        bs, ex, outs = rest[:ns], rest[ns:ns + ne], rest[ns + ne:ns + ne + no]
        part = _dot_nt(dy_ref[:, pl.ds(0, SLAB)], bs[0][...])
        for q in range(1, ns):
            part = part + _dot_nt(dy_ref[:, pl.ds(q * SLAB, SLAB)], bs[q][...])
        if row:
            _finish(epi, part, ex, outs)
        else:
            acc_ref = rest[-1]
            j = pl.program_id(1)

            @pl.when(j == 0)
            def _():
                acc_ref[...] = part

            @pl.when(j > 0)
            def _():
                acc_ref[...] += part

            @pl.when(j == N_CHIPS - 1)
            def _():
                _finish(epi, acc_ref[...], ex, outs)

    return pl.pallas_call(
        body, name=name, grid=grid,
        in_specs=[dy_spec] + b_specs + [tile] * ne, out_specs=[tile] * no,
        out_shape=[jax.ShapeDtypeStruct((M, n_out), d) for d in out_dtypes],
        scratch_shapes=[] if row else [pltpu.VMEM((tm, K), F32)],
        compiler_params=_cparams(sem),
    )(dy, *([wg] * ns), *extras)


def mm_plain(a, b, *, name):
    M, K = a.shape
    N = b.shape[1]
    tm = min(M, 512)

    def body(a_ref, b_ref, o_ref):
        o_ref[...] = _dot(a_ref[...], b_ref[...])

    return pl.pallas_call(
        body, name=name, grid=(M // tm,),
        in_specs=[pl.BlockSpec((tm, K), lambda i: (i, 0)), pl.BlockSpec((K, N), lambda i: (0, 0))],
        out_specs=pl.BlockSpec((tm, N), lambda i: (i, 0)), out_shape=jax.ShapeDtypeStruct((M, N), F32),
        compiler_params=_cparams(("parallel",)),
    )(a, b)


def mm_tn(x, dy, g, t, *, name):
    off, K, ns, row = t
    S = x.shape[0]
    sb = off // K
    if row:
        x_map = lambda j, q: (0, j)
        dy_map = lambda j, q: (0, q)
    else:
        x_map = lambda j, q: (0, 0)
        dy_map = lambda j, q: (0, j * ns + q)

    def body(g_in, x_ref, dy_ref, o_ref):
        del g_in
        o_ref[...] = _dot_tn(x_ref[...], dy_ref[...]).astype(o_ref.dtype)

    return pl.pallas_call(
        body, name=name, grid=(N_CHIPS, ns),
        in_specs=[pl.BlockSpec(memory_space=pl.ANY), pl.BlockSpec((S, K), x_map), pl.BlockSpec((S, SLAB), dy_map)],
        out_specs=pl.BlockSpec((None, K, SLAB), lambda j, q: (j, sb + q, 0)),
        out_shape=jax.ShapeDtypeStruct(g.shape, g.dtype),
        input_output_aliases={0: 0},
        compiler_params=_cparams(("parallel", "parallel")),
    )(g, x, dy)


def ew(fn, ins, out_dtypes, *, name, bcast=()):
    S = ins[0].shape[0]
    tr = min(ROW_TILE, S)
    cols = ins[0].shape[1]
    ni, nb = len(ins), len(bcast)

    def body(*refs):
        res = fn(*[r[...] for r in refs[:ni + nb]])
        for o, r in zip(refs[ni + nb:], res):
            o[...] = r.astype(o.dtype)

    return pl.pallas_call(
        body, name=name, grid=(S // tr,),
        in_specs=[pl.BlockSpec((tr, a.shape[1]), lambda i: (i, 0)) for a in ins]
        + [pl.BlockSpec(b.shape, lambda i: (0, 0)) for b in bcast],
        out_specs=[pl.BlockSpec((tr, cols), lambda i: (i, 0)) for _ in out_dtypes],
        out_shape=[jax.ShapeDtypeStruct((S, cols), d) for d in out_dtypes],
        compiler_params=_cparams(("parallel",)),
    )(*ins, *bcast)


def _rstd(x):
    return lax.rsqrt(jnp.mean(x * x, axis=-1, keepdims=True) + RMS_EPS)


def _sigmoid(x):
    return 1.0 / (1.0 + jnp.exp(-x))


def _log_sigmoid(z):
    return jnp.minimum(z, 0.0) - jnp.log(1.0 + jnp.exp(-jnp.abs(z)))


def rms_fwd(x, g, *, name):
    return ew(lambda xv, gv: (xv * _rstd(xv) * gv,), [x], [BF16], name=name, bcast=[g])[0]


def _rms_bwd_tile(xv, gv, dh):
    rstd = _rstd(xv)
    xhat = xv * rstd
    gd = dh * gv
    dx = rstd * (gd - xhat * jnp.mean(xhat * gd, axis=-1, keepdims=True))
    return dx, jnp.sum(dh * xhat, axis=0, keepdims=True)


def rms_bwd(x, g, dh, dres, *, name):
    S, D = x.shape
    tr = min(ROW_TILE, S)

    def body(x_ref, g_ref, dh_ref, dres_ref, dx_ref, dxb_ref, dg_ref):
        i = pl.program_id(0)
        dx, dg = _rms_bwd_tile(x_ref[...], g_ref[...], dh_ref[...])
        dx = dx + dres_ref[...]
        dx_ref[...] = dx
        dxb_ref[...] = dx.astype(BF16)

        @pl.when(i == 0)
        def _():
            dg_ref[...] = dg

        @pl.when(i > 0)
        def _():
            dg_ref[...] += dg

    row = pl.BlockSpec((tr, D), lambda i: (i, 0))
    one = pl.BlockSpec((1, D), lambda i: (0, 0))
    return pl.pallas_call(
        body, name=name, grid=(S // tr,),
        in_specs=[row, one, row, row], out_specs=[row, row, one],
        out_shape=[jax.ShapeDtypeStruct((S, D), F32), jax.ShapeDtypeStruct((S, D), BF16),
                   jax.ShapeDtypeStruct((1, D), F32)],
        compiler_params=_cparams(("arbitrary",)),
    )(x, g, dh, dres)


def loss_head(x, g, target, *, name):
    S, D = x.shape
    tr = min(ROW_TILE, S)

    def body(x_ref, g_ref, t_ref, dx_ref, dg_ref, loss_ref):
        i = pl.program_id(0)
        xv, gv = x_ref[...], g_ref[...]
        err = xv * _rstd(xv) * gv - t_ref[...]
        part = 0.5 * jnp.sum(jnp.mean(err * err, axis=-1, keepdims=True), axis=0, keepdims=True)
        dx, dg = _rms_bwd_tile(xv, gv, err * (1.0 / D))
        dx_ref[...] = dx
        part = jnp.broadcast_to(part, loss_ref.shape)

        @pl.when(i == 0)
        def _():
            dg_ref[...] = dg
            loss_ref[...] = part

        @pl.when(i > 0)
        def _():
            dg_ref[...] += dg
            loss_ref[...] += part

    row = pl.BlockSpec((tr, D), lambda i: (i, 0))
    one = pl.BlockSpec((1, D), lambda i: (0, 0))
    return pl.pallas_call(
        body, name=name, grid=(S // tr,),
        in_specs=[row, one, row], out_specs=[row, one, pl.BlockSpec((1, 128), lambda i: (0, 0))],
        out_shape=[jax.ShapeDtypeStruct((S, D), F32), jax.ShapeDtypeStruct((1, D), F32),
                   jax.ShapeDtypeStruct((1, 128), F32)],
        compiler_params=_cparams(("arbitrary",)),
    )(x, g, target)


def rope_tables(S, n_cols):
    half = ROPE_DIM // 2
    inv_freq = ROPE_THETA ** (-jnp.arange(half, dtype=F32) / half)
    ang = jnp.arange(S, dtype=F32)[:, None] * inv_freq[None, :]
    cos, sin = jnp.cos(ang), jnp.sin(ang)
    z = jnp.zeros((S, HEAD_DIM - ROPE_DIM), F32)
    zh = jnp.zeros((S, half), F32)
    c = jnp.concatenate([cos, cos, jnp.ones_like(z)], axis=1)
    sa = jnp.concatenate([zh, sin, z], axis=1)
    sb = jnp.concatenate([-sin, zh, z], axis=1)
    return [jnp.tile(t, (1, n_cols // HEAD_DIM)) for t in (c, sa, sb)]


def rope_fwd(xqk, tables, *, name):
    n, half = xqk.shape[1], ROPE_DIM // 2

    def fn(x, c, sa, sb):
        return (x * c + pltpu.roll(x, half, 1) * sa + pltpu.roll(x, n - half, 1) * sb,)

    return ew(fn, [xqk] + list(tables), [BF16], name=name)[0]


def rope_bwd(dy, tables, *, name):
    n, half = dy.shape[1], ROPE_DIM // 2

    def fn(d, c, sa, sb):
        return (d * c + pltpu.roll(d * sa, n - half, 1) + pltpu.roll(d * sb, half, 1),)

    return ew(fn, [dy] + list(tables), [BF16], name=name)[0]


def _split3(x):
    h1 = x.astype(BF16)
    r1 = x - h1.astype(F32)
    h2 = r1.astype(BF16)
    return h1, h2, (r1 - h2.astype(F32)).astype(BF16)


def _split2(x):
    h1 = x.astype(BF16)
    return h1, (x - h1.astype(F32)).astype(BF16)


def _tri(n, cmp):
    r = lax.broadcasted_iota(jnp.int32, (n, n), 0)
    c = lax.broadcasted_iota(jnp.int32, (n, n), 1)
    return cmp(r, c).astype(BF16)


def fox_gate_fwd(fl, b, *, name):
    S, W = fl.shape
    tr = min(ROW_TILE, S)

    def body(fl_ref, b_ref, cum_ref, carry):
        i = pl.program_id(0)

        @pl.when(i == 0)
        def _():
            carry[...] = jnp.zeros_like(carry)

        lower = _tri(tr, lambda r, c: r >= c)
        cs = carry[...]
        for piece in _split3(_log_sigmoid(fl_ref[...] + b_ref[...])):
            cs = cs + _dot(lower, piece)
        cum_ref[...] = cs
        carry[...] = cs[tr - 1:tr, :]

    return pl.pallas_call(
        body, name=name, grid=(S // tr,),
        in_specs=[pl.BlockSpec((tr, W), lambda i: (i, 0)), pl.BlockSpec((1, W), lambda i: (0, 0))],
        out_specs=pl.BlockSpec((tr, W), lambda i: (i, 0)),
        out_shape=jax.ShapeDtypeStruct((S, W), F32),
        scratch_shapes=[pltpu.VMEM((1, W), F32)],
        compiler_params=_cparams(("arbitrary",)),
    )(fl, b)


def fox_gate_bwd(dcum, fl, b, *, name):
    S, W = fl.shape
    tr = min(ROW_TILE, S)
    nb = S // tr

    def body(dc_ref, fl_ref, b_ref, dfl_ref, db_ref, carry):
        i = pl.program_id(0)

        @pl.when(i == 0)
        def _():
            carry[...] = jnp.zeros_like(carry)

        upper = _tri(tr, lambda r, c: r <= c)
        cs = carry[...]
        for piece in _split3(dc_ref[...]):
            cs = cs + _dot(upper, piece)
        carry[...] = cs[0:1, :]
        dfl = cs * _sigmoid(-(fl_ref[...] + b_ref[...]))
        dfl_ref[...] = dfl
        db = jnp.sum(dfl, axis=0, keepdims=True)

        @pl.when(i == 0)
        def _():
            db_ref[...] = db

        @pl.when(i > 0)
        def _():
            db_ref[...] += db

    rev = pl.BlockSpec((tr, W), lambda i: (nb - 1 - i, 0))
    one = pl.BlockSpec((1, W), lambda i: (0, 0))
    return pl.pallas_call(
        body, name=name, grid=(nb,),
        in_specs=[rev, rev, one], out_specs=[rev, one],
        out_shape=[jax.ShapeDtypeStruct((S, W), F32), jax.ShapeDtypeStruct((1, W), F32)],
        scratch_shapes=[pltpu.VMEM((1, W), F32)],
        compiler_params=_cparams(("arbitrary",)),
    )(dcum, fl, b)


def _blk_iota(tq, tk):
    return (lax.broadcasted_iota(jnp.int32, (tq, tk), 0), lax.broadcasted_iota(jnp.int32, (tq, tk), 1))


def _cs(xb, tri):
    return _dot(xb, tri)


def _rowsum(xb):
    return jnp.sum(xb.astype(F32), axis=1, keepdims=True)


def _sb_block(qs, k, cmr, shift):
    z = _dot_nt(qs, k)
    strict = cmr < shift
    lb = jnp.minimum(z, 0.0) - jnp.log(1.0 + jnp.exp(-jnp.abs(z)))
    lom = jnp.where(strict, lb - z, 0.0).astype(BF16)
    return lb, lom, strict


def _att_tiles(S):
    return min(ATT_BQ, S), min(ATT_BK, S)


PAIR = 2 * HEAD_DIM
N_PAIRS = N_HEADS // 2


def _pair_specs(S, tq):
    cols = D_MODEL // PAIR
    qspec = pl.BlockSpec((tq, PAIR), lambda p, i: (i, p))
    kspec = pl.BlockSpec((S, PAIR), lambda p, i: (0, cols + p))
    vspec = pl.BlockSpec((S, PAIR), lambda p, i: (0, 2 * cols + p))
    kvout = pl.BlockSpec((S, PAIR), lambda p, i: (0, p))
    vec = pl.BlockSpec((2, tq, 1), lambda p, i: (p, i, 0))
    return qspec, kspec, vspec, kvout, vec


def _head_lanes(h):
    lane = lax.broadcasted_iota(jnp.int32, (1, PAIR), 1)
    return (lane >= h * HEAD_DIM) & (lane < (h + 1) * HEAD_DIM)


def _only(sel, x):
    return jnp.where(sel, x, jnp.zeros_like(x))


def sb_fwd(proj, *, name):
    S = proj.shape[0]
    tq, tk = _att_tiles(S)
    qspec, kspec, vspec, _, vec = _pair_specs(S, tq)

    def body(q_ref, k_ref, v_ref, o_ref, t_ref):
        i = pl.program_id(1)
        row, col = _blk_iota(tq, tk)
        cmr = col - row
        below = _tri(tk, lambda r, c: r > c)
        nkb = (i + 1) * (tq // tk)
        out = []
        for h in range(2):
            sel = _head_lanes(h)
            qs = _only(sel, q_ref[...] * ATTN_SCALE)

            def step(n, carry):
                r_sum, acc = carry
                kb = nkb - 1 - n
                ks = pl.multiple_of(kb * tk, tk)
                lb, lom, strict = _sb_block(qs, k_ref[pl.ds(ks, tk), :], cmr, i * tq - kb * tk)
                w = jnp.where(strict, jnp.exp(lb + _cs(lom, below) + r_sum), 0.0)
                acc = acc + _dot(w.astype(BF16), _only(sel, v_ref[pl.ds(ks, tk), :]))
                return r_sum + _rowsum(lom), acc

            r_sum, acc = lax.fori_loop(0, nkb, step, (jnp.zeros((tq, 1), F32), jnp.zeros((tq, PAIR), F32)))
            t_ref[h] = r_sum
            out.append(acc)
        o_ref[...] = (out[0] + out[1]).astype(o_ref.dtype)

    return pl.pallas_call(
        body, name=name, grid=(N_PAIRS, S // tq),
        in_specs=[qspec, kspec, vspec], out_specs=[qspec, vec],
        out_shape=[jax.ShapeDtypeStruct((S, D_MODEL), BF16), jax.ShapeDtypeStruct((N_HEADS, S, 1), F32)],
        compiler_params=_cparams(("parallel", "arbitrary")),
    )(proj, proj, proj)


def sb_bwd(proj, tot, do, *, name):
    S = proj.shape[0]
    tq, tk = _att_tiles(S)
    qspec, kspec, vspec, kvout, vec = _pair_specs(S, tq)

    def body(q_ref, k_ref, v_ref, t_ref, do_ref, dq_ref, dk_out, dv_out, dk_ref, dv_ref):
        i = pl.program_id(1)

        @pl.when(i == 0)
        def _():
            dk_ref[...] = jnp.zeros_like(dk_ref)
            dv_ref[...] = jnp.zeros_like(dv_ref)

        row, col = _blk_iota(tq, tk)
        cmr = col - row
        upto = _tri(tk, lambda r, c: r <= c)
        before = _tri(tk, lambda r, c: r < c)
        out = []
        for h in range(2):
            sel = _head_lanes(h)
            qs, dov, t_all = _only(sel, q_ref[...] * ATTN_SCALE), _only(sel, do_ref[...]), t_ref[h]

            def step(kb, carry):
                p_sum, e_sum, dq = carry
                ks = pl.multiple_of(kb * tk, tk)
                kv = k_ref[pl.ds(ks, tk), :]
                lb, lom, strict = _sb_block(qs, kv, cmr, i * tq - kb * tk)
                tail = t_all - p_sum - _cs(lom, upto)
                w = jnp.where(strict, jnp.exp(lb + tail), 0.0)
                e = _dot_nt(dov, v_ref[pl.ds(ks, tk), :]) * w
                eb = e.astype(BF16)
                e_before = e_sum + _cs(eb, before)
                beta = jnp.exp(lb)
                dzb = jnp.where(strict, e - (e + e_before) * beta, 0.0).astype(BF16)
                dk_ref[pl.ds(ks, tk), :] += _dot_tn(dzb, qs)
                dv_ref[pl.ds(ks, tk), :] += _dot_tn(w.astype(BF16), dov)
                return p_sum + _rowsum(lom), e_sum + _rowsum(eb), dq + _dot(dzb, _only(sel, kv))

            zero = jnp.zeros((tq, 1), F32)
            out.append(lax.fori_loop(0, (i + 1) * (tq // tk), step, (zero, zero, jnp.zeros((tq, PAIR), F32)))[2])
        dq_ref[...] = ((out[0] + out[1]) * ATTN_SCALE).astype(dq_ref.dtype)

        @pl.when(i == S // tq - 1)
        def _():
            dk_out[...] = dk_ref[...].astype(dk_out.dtype)
            dv_out[...] = dv_ref[...].astype(dv_out.dtype)

    full = jax.ShapeDtypeStruct((S, D_MODEL), BF16)
    return pl.pallas_call(
        body, name=name, grid=(N_PAIRS, S // tq),
        in_specs=[qspec, kspec, vspec, vec, qspec], out_specs=[qspec, kvout, kvout],
        out_shape=[full, full, full],
        scratch_shapes=[pltpu.VMEM((S, PAIR), F32)] * 2,
        compiler_params=_cparams(("parallel", "arbitrary")),
    )(proj, proj, proj, tot, do)


def _fox_logits(qs, k, cq, ck, cmr, shift):
    causal = cmr <= shift
    return jnp.where(causal, _dot_nt(qs, k) + cq - ck, NEG_INF), causal


def fox_fwd(proj, cq, ck, *, name):
    S = proj.shape[0]
    tq, tk = _att_tiles(S)
    qspec, kspec, vspec, _, vec = _pair_specs(S, tq)
    ckspec = pl.BlockSpec((2, S // tk, 1, tk), lambda p, i: (p, 0, 0, 0))

    def body(q_ref, k_ref, v_ref, cq_ref, ck_ref, o_ref, lse_ref):
        i = pl.program_id(1)
        row, col = _blk_iota(tq, tk)
        cmr = col - row
        out = []
        for h in range(2):
            sel = _head_lanes(h)
            qs, cqv = _only(sel, q_ref[...] * ATTN_SCALE), cq_ref[h]

            def step(kb, carry):
                m, l, acc = carry
                ks = pl.multiple_of(kb * tk, tk)
                s, _ = _fox_logits(qs, k_ref[pl.ds(ks, tk), :], cqv, ck_ref[h, kb], cmr, i * tq - kb * tk)
                m_new = jnp.maximum(m, jnp.max(s, axis=1, keepdims=True))
                alpha = jnp.exp(m - m_new)
                p = jnp.exp(s - m_new)
                l = alpha * l + jnp.sum(p, axis=1, keepdims=True)
                acc = alpha * acc + _dot(p.astype(BF16), _only(sel, v_ref[pl.ds(ks, tk), :]))
                return m_new, l, acc

            m, l, acc = lax.fori_loop(0, (i + 1) * (tq // tk), step,
                                      (jnp.full((tq, 1), NEG_INF, F32), jnp.zeros((tq, 1), F32), jnp.zeros((tq, PAIR), F32)))
            lse_ref[h] = m + jnp.log(l)
            out.append(acc / l)
        o_ref[...] = (out[0] + out[1]).astype(o_ref.dtype)

    return pl.pallas_call(
        body, name=name, grid=(N_PAIRS, S // tq),
        in_specs=[qspec, kspec, vspec, vec, ckspec], out_specs=[qspec, vec],
        out_shape=[jax.ShapeDtypeStruct((S, D_MODEL), BF16), jax.ShapeDtypeStruct((N_HEADS, S, 1), F32)],
        compiler_params=_cparams(("parallel", "arbitrary")),
    )(proj, proj, proj, cq, ck)


def fox_bwd(proj, o, lse, cq, ck, do, *, name):
    S = proj.shape[0]
    tq, tk = _att_tiles(S)
    qspec, kspec, vspec, kvout, vec = _pair_specs(S, tq)
    ckspec = pl.BlockSpec((2, S // tk, 1, tk), lambda p, i: (p, 0, 0, 0))

    def body(q_ref, k_ref, v_ref, o_ref, lse_ref, cq_ref, ck_ref, do_ref, dq_ref, dk_out, dv_out, dcq_ref, dck_ref,
             dk_ref, dv_ref):
        i = pl.program_id(1)

        @pl.when(i == 0)
        def _():
            dk_ref[...] = jnp.zeros_like(dk_ref)
            dv_ref[...] = jnp.zeros_like(dv_ref)
            dck_ref[...] = jnp.zeros_like(dck_ref)

        row, col = _blk_iota(tq, tk)
        cmr = col - row
        out = []
        for h in range(2):
            sel = _head_lanes(h)
            qs, dov, cqv, lsev = _only(sel, q_ref[...] * ATTN_SCALE), _only(sel, do_ref[...]), cq_ref[h], lse_ref[h]
            delta = jnp.sum(dov.astype(F32) * o_ref[...].astype(F32), axis=1, keepdims=True)

            def step(kb, carry):
                dq, dcq = carry
                ks = pl.multiple_of(kb * tk, tk)
                kv = k_ref[pl.ds(ks, tk), :]
                s, causal = _fox_logits(qs, kv, cqv, ck_ref[h, kb], cmr, i * tq - kb * tk)
                p = jnp.where(causal, jnp.exp(s - lsev), 0.0)
                ds = p * (_dot_nt(dov, v_ref[pl.ds(ks, tk), :]) - delta)
                dck_ref[h, kb] += jnp.sum(ds, axis=0, keepdims=True)
                dsb = ds.astype(BF16)
                dk_ref[pl.ds(ks, tk), :] += _dot_tn(dsb, qs)
                dv_ref[pl.ds(ks, tk), :] += _dot_tn(p.astype(BF16), dov)
                return dq + _dot(dsb, _only(sel, kv)), dcq + jnp.sum(ds, axis=1, keepdims=True)

            dq, dcq = lax.fori_loop(0, (i + 1) * (tq // tk), step, (jnp.zeros((tq, PAIR), F32), jnp.zeros((tq, 1), F32)))
            dcq_ref[h] = dcq
            out.append(dq)
        dq_ref[...] = ((out[0] + out[1]) * ATTN_SCALE).astype(dq_ref.dtype)

        @pl.when(i == S // tq - 1)
        def _():
            dk_out[...] = dk_ref[...].astype(dk_out.dtype)
            dv_out[...] = dv_ref[...].astype(dv_out.dtype)

    full = jax.ShapeDtypeStruct((S, D_MODEL), BF16)
    return pl.pallas_call(
        body, name=name, grid=(N_PAIRS, S // tq),
        in_specs=[qspec, kspec, vspec, qspec, vec, vec, ckspec, qspec],
        out_specs=[qspec, kvout, kvout, vec, ckspec],
        out_shape=[full, full, full, jax.ShapeDtypeStruct((N_HEADS, S, 1), F32),
                   jax.ShapeDtypeStruct((N_HEADS, S // tk, 1, tk), F32)],
        scratch_shapes=[pltpu.VMEM((S, PAIR), F32)] * 2,
        compiler_params=_cparams(("parallel", "arbitrary")),
    )(proj, proj, proj, o, lse, cq, ck, do)


def _swa_specs(S, tq):
    qspec = pl.BlockSpec((None, SWA_GROUP, tq, HEAD_DIM), lambda g, i: (g, 0, i, 0))
    kvspec = pl.BlockSpec((None, S + SWA_WINDOW, HEAD_DIM), lambda g, i: (g, 0, 0))
    vec = pl.BlockSpec((None, SWA_GROUP, tq, 1), lambda g, i: (g, 0, i, 0))
    sink = pl.BlockSpec((None, SWA_GROUP * tq, 1), lambda g, i: (g, 0, 0))
    return qspec, kvspec, vec, sink


def _swa_logits(q2, kw, i, tq):
    rows = q2.shape[0]
    r = lax.broadcasted_iota(jnp.int32, (rows, 2 * tq), 0)
    c = lax.broadcasted_iota(jnp.int32, (rows, 2 * tq), 1)
    diff = (r & (tq - 1)) + tq - c
    ok = (diff >= 0) & (diff < SWA_WINDOW) & (c + (i - 1) * tq >= 0)
    return jnp.where(ok, _dot_nt(q2, kw) * ATTN_SCALE, NEG_INF), ok


def swa_fwd(q, kp, vp, sink, *, name):
    _, G, S, _ = q.shape
    tq = ATT_BLK
    qspec, kvspec, vec, sinkspec = _swa_specs(S, tq)

    def body(q_ref, k_ref, v_ref, s_ref, o_ref, lse_ref):
        i = pl.program_id(1)
        q2 = q_ref[...].reshape(G * tq, HEAD_DIM)
        ws = pl.multiple_of(i * tq, tq)
        logits, _ = _swa_logits(q2, k_ref[pl.ds(ws, 2 * tq), :], i, tq)
        sk = s_ref[...]
        m = jnp.maximum(jnp.max(logits, axis=1, keepdims=True), sk)
        e = jnp.exp(logits - m)
        den = jnp.sum(e, axis=1, keepdims=True) + jnp.exp(sk - m)
        o = _dot((e / den).astype(BF16), v_ref[pl.ds(ws, 2 * tq), :])
        o_ref[...] = o.reshape(G, tq, HEAD_DIM).astype(o_ref.dtype)
        lse_ref[...] = (m + jnp.log(den)).reshape(G, tq, 1)

    return pl.pallas_call(
        body, name=name, grid=(SWA_KV_HEADS, S // tq),
        in_specs=[qspec, kvspec, kvspec, sinkspec], out_specs=[qspec, vec],
        out_shape=[jax.ShapeDtypeStruct(q.shape, BF16), jax.ShapeDtypeStruct((SWA_KV_HEADS, G, S, 1), F32)],
        compiler_params=_cparams(("parallel", "arbitrary")),
    )(q, kp, vp, sink)


def swa_bwd(q, kp, vp, sink, o, lse, do, *, name):
    _, G, S, _ = q.shape
    tq = ATT_BLK
    qspec, kvspec, vec, sinkspec = _swa_specs(S, tq)

    def body(q_ref, k_ref, v_ref, s_ref, o_ref, lse_ref, do_ref, dq_ref, dk_ref, dv_ref, dsink_ref):
        i = pl.program_id(1)

        @pl.when(i == 0)
        def _():
            dk_ref[...] = jnp.zeros_like(dk_ref)
            dv_ref[...] = jnp.zeros_like(dv_ref)

        q2 = q_ref[...].reshape(G * tq, HEAD_DIM)
        do2 = do_ref[...].reshape(G * tq, HEAD_DIM)
        o2 = o_ref[...].reshape(G * tq, HEAD_DIM)
        lse2 = lse_ref[...].reshape(G * tq, 1)
        ws = pl.multiple_of(i * tq, tq)
        kw = k_ref[pl.ds(ws, 2 * tq), :]
        vw = v_ref[pl.ds(ws, 2 * tq), :]
        logits, ok = _swa_logits(q2, kw, i, tq)
        p = jnp.where(ok, jnp.exp(logits - lse2), 0.0)
        delta = jnp.sum(do2.astype(F32) * o2.astype(F32), axis=1, keepdims=True)
        ds = p * (_dot_nt(do2, vw) - delta)
        dsb = ds.astype(BF16)
        dq_ref[...] = (_dot(dsb, kw) * ATTN_SCALE).reshape(G, tq, HEAD_DIM)
        dk_ref[pl.ds(ws, 2 * tq), :] += _dot_tn(dsb, q2) * ATTN_SCALE
        dv_ref[pl.ds(ws, 2 * tq), :] += _dot_tn(p.astype(BF16), do2)
        dsink_ref[...] = (-jnp.exp(s_ref[...] - lse2) * delta).reshape(G, tq, 1)

    kvshape = jax.ShapeDtypeStruct(kp.shape, F32)
    return pl.pallas_call(
        body, name=name, grid=(SWA_KV_HEADS, S // tq),
        in_specs=[qspec, kvspec, kvspec, sinkspec, qspec, vec, qspec],
        out_specs=[qspec, kvspec, kvspec, vec],
        out_shape=[jax.ShapeDtypeStruct(q.shape, F32), kvshape, kvshape,
                   jax.ShapeDtypeStruct((SWA_KV_HEADS, G, S, 1), F32)],
        compiler_params=_cparams(("parallel", "arbitrary")),
    )(q, kp, vp, sink, o, lse, do)


def _adamw_tile(w, g, m, v):
    m = ADAM_B1 * m + (1.0 - ADAM_B1) * g
    v = ADAM_B2 * v + (1.0 - ADAM_B2) * (g * g)
    m_hat = m / (1.0 - ADAM_B1 ** ADAM_STEP)
    v_hat = v / (1.0 - ADAM_B2 ** ADAM_STEP)
    delta = -ADAM_LR * (m_hat / (jnp.sqrt(v_hat) + ADAM_EPS) + ADAM_WD * w)
    return g, delta, m, v


def adamw(gfull, t, w, m, v, after, *, name):
    off, K, ns, _ = t
    sb = off // K
    nat = pl.BlockSpec((K, SLAB), lambda q: (0, q))

    def body(g_ref, w_ref, m_ref, v_ref, after_ref, *outs):
        del after_ref
        for o, r in zip(outs, _adamw_tile(w_ref[...], g_ref[...], m_ref[...], v_ref[...])):
            o[...] = r

    return pl.pallas_call(
        body, name=name, grid=(ns,),
        in_specs=[pl.BlockSpec((K, SLAB), lambda q: (sb + q, 0)), nat, nat, nat, HBM],
        out_specs=[nat] * 4, out_shape=[jax.ShapeDtypeStruct(w.shape, F32)] * 4,
        compiler_params=_cparams(("parallel",)),
    )(gfull, w, m, v, after)


def adamw_small(g, w, m, v, *, name):
    def body(g_ref, w_ref, m_ref, v_ref, *outs):
        for o, r in zip(outs, _adamw_tile(w_ref[...], g_ref[...], m_ref[...], v_ref[...])):
            o[...] = r

    return pl.pallas_call(body, name=name, out_shape=[jax.ShapeDtypeStruct(w.shape, F32)] * 4)(g, w, m, v)


MESH = pl.DeviceIdType.MESH
HBM = pl.BlockSpec(memory_space=pl.ANY)


def _place():
    x, y, c = lax.axis_index("x"), lax.axis_index("y"), lax.axis_index("c")
    others = [(1 - x, y), (x, 1 - y), (1 - x, 1 - y)]
    return x, y, c, others


def _rcopy(src, dst, send_sems, recv_sems, k, to):
    return pltpu.make_async_remote_copy(src_ref=src, dst_ref=dst, send_sem=send_sems.at[k], recv_sem=recv_sems.at[k],
                                        device_id=to, device_id_type=MESH)


def _dma_sems(*counts):
    return [pltpu.SemaphoreType.DMA((n,)) for n in counts]


DMA_UNIT_ROWS = 128
DMA_PIECES = 4
DMA_PIECES_LOCAL = 8


def _row_pieces(h, n):
    units = h // DMA_UNIT_ROWS
    n = min(n, units)
    base, extra = divmod(units, n)
    sizes = [(base + (k < extra)) * DMA_UNIT_ROWS for k in range(n)]
    return [(sum(sizes[:k]), sizes[k]) for k in range(n)]


def _start_pieces(make, h, n):
    for s0, sz in _row_pieces(h, n):
        make(s0, sz).start()
    return make(0, h)


SEM = pl.BlockSpec(memory_space=pltpu.SEMAPHORE)
SPLIT_COPY = pltpu.CompilerParams(has_side_effects=pltpu.SideEffectType.DATAFLOW_SIDE_EFFECTING)
N_OTHERS = 3


def _hbm(a):
    return pltpu.with_memory_space_constraint(a, pltpu.HBM)


def _chip_rows(buf, chip, s0, sz):
    return buf.at[2 * chip[0] + chip[1], pl.ds(s0, sz)]


def allgather_start(bufs, *, name):
    n = len(bufs)

    def body(*refs):
        ins, send, recv, token = refs[:n], refs[n:2 * n], refs[2 * n:3 * n], refs[4 * n]
        x, y, c, others = _place()
        for i in range(n):
            h = bufs[i].shape[1] // 2
            for f, chip in enumerate(others):
                for s0, sz in _row_pieces(h, DMA_PIECES):
                    mine = _chip_rows(ins[i], (x, y), c * h + s0, sz)
                    _rcopy(mine, mine, send[i], recv[i], f, (*chip, c)).start()
        token[...] = jnp.zeros_like(token)

    res = pl.pallas_call(
        body, name=name, in_specs=[HBM] * n,
        out_specs=[SEM] * (2 * n) + [HBM] * n + [pl.BlockSpec(memory_space=pltpu.VMEM)],
        out_shape=[pltpu.SemaphoreType.DMA((N_OTHERS,))] * (2 * n) + [pltpu.HBM(b.shape, b.dtype) for b in bufs]
        + [jax.ShapeDtypeStruct((1, D_MODEL), F32)],
        input_output_aliases={i: 2 * n + i for i in range(n)},
        compiler_params=SPLIT_COPY,
    )(*[_hbm(b) for b in bufs])
    return res[:n], res[n:2 * n], res[2 * n:3 * n], res[3 * n]


def allgather_wait(buf, send, recv, after, *, name):
    h = buf.shape[1] // 2

    def body(buf_ref, send_sems, recv_sems, after_ref, out_ref):
        del after_ref, out_ref
        x, y, c, others = _place()
        for f, chip in enumerate(others):
            mine = _chip_rows(buf_ref, (x, y), c * h, h)
            theirs = _chip_rows(buf_ref, chip, c * h, h)
            cp = _rcopy(mine, theirs, send_sems, recv_sems, f, (*chip, c))
            cp.wait_send()
            cp.wait_recv()

    return pl.pallas_call(
        body, name=name, in_specs=[HBM, SEM, SEM, HBM], out_specs=HBM,
        out_shape=pltpu.HBM(buf.shape, buf.dtype), input_output_aliases={0: 0},
        compiler_params=SPLIT_COPY,
    )(buf, send, recv, after)


def allgather_forward(buf, *, name):
    h = buf.shape[1] // 2

    def body(in_ref, out_ref, send_sems, recv_sems):
        del in_ref
        x, y, c, others = _place()
        sibling = (x, y, 1 - c)
        sent = []
        for f, chip in enumerate(others):
            sent.append(_start_pieces(
                lambda s0, sz: _rcopy(_chip_rows(out_ref, chip, c * h + s0, sz), _chip_rows(out_ref, chip, c * h + s0, sz),
                                      send_sems, recv_sems, f, sibling), h, DMA_PIECES))
        for f, chip in enumerate(others):
            blk = _chip_rows(out_ref, chip, (1 - c) * h, h)
            _rcopy(blk, blk, send_sems, recv_sems, f, sibling).wait_recv()
        for cp in sent:
            cp.wait_send()

    return pl.pallas_call(
        body, name=name, in_specs=[HBM], out_specs=HBM,
        out_shape=jax.ShapeDtypeStruct(buf.shape, buf.dtype), input_output_aliases={0: 0},
        scratch_shapes=_dma_sems(N_OTHERS, N_OTHERS),
    )(buf)


def swap_halves(grads, *, name):
    n = len(grads)

    def body(*refs):
        ins, theirs = refs[:n], refs[n:2 * n]
        send_sems, recv_sems = refs[2 * n:]
        x, y, c, _ = _place()
        for i in range(n):
            h = grads[i].shape[1] // 2
            for k in range(N_CHIPS):
                for s0, sz in _row_pieces(h, DMA_PIECES):
                    _rcopy(ins[i].at[k, pl.ds((1 - c) * h + s0, sz)], theirs[i].at[k, pl.ds(s0, sz)],
                           send_sems, recv_sems, i, (x, y, 1 - c)).start()
        for i in range(n):
            h = grads[i].shape[1] // 2
            _rcopy(ins[i].at[:, pl.ds((1 - c) * h, h)], theirs[i], send_sems, recv_sems, i, (x, y, 1 - c)).wait()

    return pl.pallas_call(
        body, name=name, in_specs=[HBM] * n, out_specs=[HBM] * n,
        out_shape=[jax.ShapeDtypeStruct((N_CHIPS, g.shape[1] // 2, SLAB), g.dtype) for g in grads],
        scratch_shapes=_dma_sems(n, n))(*grads)


def scatter_start(part, *, name):
    h = part.shape[1]

    def body(part_ref, land_ref, send, recv, part_out, land_out, token):
        del part_out, land_out
        x, y, c, others = _place()
        for f, chip in enumerate(others):
            for s0, sz in _row_pieces(h, DMA_PIECES):
                _rcopy(_chip_rows(part_ref, chip, s0, sz), land_ref.at[f, pl.ds(s0, sz)], send, recv, f, (*chip, c)).start()
        token[...] = jnp.zeros_like(token)

    land = lax.empty((N_OTHERS,) + part.shape[1:], part.dtype)
    return pl.pallas_call(
        body, name=name, in_specs=[HBM, HBM],
        out_specs=[SEM, SEM, HBM, HBM, pl.BlockSpec(memory_space=pltpu.VMEM)],
        out_shape=[pltpu.SemaphoreType.DMA((N_OTHERS,))] * 2 + [pltpu.HBM(part.shape, part.dtype), pltpu.HBM(land.shape, land.dtype),
                                                                 jax.ShapeDtypeStruct((1, D_MODEL), F32)],
        input_output_aliases={0: 2, 1: 3},
        compiler_params=SPLIT_COPY,
    )(_hbm(part), _hbm(land))


def scatter_wait(part, land, send, recv, after, *, name):
    h = part.shape[1]

    def body(part_ref, land_ref, send_sems, recv_sems, after_ref, part_out, land_out):
        del after_ref, part_out, land_out
        x, y, c, others = _place()
        for f, chip in enumerate(others):
            cp = _rcopy(_chip_rows(part_ref, chip, 0, h), land_ref.at[f], send_sems, recv_sems, f, (*chip, c))
            cp.wait_send()
            cp.wait_recv()

    return pl.pallas_call(
        body, name=name, in_specs=[HBM, HBM, SEM, SEM, HBM], out_specs=[HBM, HBM],
        out_shape=[pltpu.HBM(part.shape, part.dtype), pltpu.HBM(land.shape, land.dtype)],
        input_output_aliases={0: 0, 1: 1},
        compiler_params=SPLIT_COPY,
    )(part, land, send, recv, after)


def join_halves(bufs, *, name):
    n = len(bufs)

    def body(*refs):
        outs = refs[n:2 * n]
        send_sems, recv_sems = refs[2 * n:]
        x, y, c, _ = _place()
        sibling = (x, y, 1 - c)
        cps = []
        for i in range(n):
            h = bufs[i].shape[0] // 2
            snd = _start_pieces(
                lambda s0, sz: _rcopy(outs[i].at[pl.ds(c * h + s0, sz)], outs[i].at[pl.ds(c * h + s0, sz)],
                                      send_sems, recv_sems, i, sibling), h, 2 * DMA_PIECES_LOCAL)
            theirs = outs[i].at[pl.ds((1 - c) * h, h)]
            cps.append((snd, _rcopy(theirs, theirs, send_sems, recv_sems, i, sibling)))
        for snd, rcv in cps:
            snd.wait_send()
            rcv.wait_recv()

    return pl.pallas_call(
        body, name=name, in_specs=[HBM] * n, out_specs=[HBM] * n,
        out_shape=[jax.ShapeDtypeStruct(b.shape, b.dtype) for b in bufs],
        input_output_aliases={i: i for i in range(n)},
        scratch_shapes=_dma_sems(n, n),
    )(*bufs)


def allreduce_small(v, *, name):
    rows, n = v.shape

    def body(x_ref, sum_ref, all_ref, send_sems, recv_sems, local_sem):
        x, y, c, others = _place()
        me, sibling = (x, y, c), (x, y, 1 - c)

        def blk(px, py, pc):
            return all_ref.at[pl.ds((4 * px + 2 * py + pc) * rows, rows), :]

        def copy(k, block, to, src=None):
            return _rcopy(blk(*block) if src is None else src, blk(*block), send_sems, recv_sems, k, to)

        mine = pltpu.make_async_copy(x_ref, blk(*me), local_sem)
        mine.start()
        first = [copy(0, me, sibling, src=x_ref)]
        first += [copy(1 + f, me, (*chip, c), src=x_ref) for f, chip in enumerate(others)]
        for cp in first:
            cp.start()
        passed = [copy(4 + f, (*chip, c), sibling) for f, chip in enumerate(others)]
        for f, chip in enumerate(others):
            copy(1 + f, (*chip, c), me).wait_recv()
            passed[f].start()
        copy(0, sibling, me).wait_recv()
        for f, chip in enumerate(others):
            copy(4 + f, (*chip, 1 - c), me).wait_recv()
        for cp in first + passed:
            cp.wait_send()
        mine.wait()
        acc = all_ref[pl.ds(0, rows), :]
        for d in range(1, N_DEVICES):
            acc = acc + all_ref[pl.ds(d * rows, rows), :]
        sum_ref[...] = acc

    vm = pl.BlockSpec(memory_space=pltpu.VMEM)
    return pl.pallas_call(
        body, name=name, in_specs=[vm], out_specs=[vm, vm],
        out_shape=[jax.ShapeDtypeStruct((rows, n), F32), jax.ShapeDtypeStruct((N_DEVICES * rows, n), F32)],
        scratch_shapes=_dma_sems(7, 7) + [pltpu.SemaphoreType.DMA],
    )(v)[0]


def add_pairs(grad, theirs, where, *, name):
    h = theirs.shape[1]
    spec = pl.BlockSpec((None, h, SLAB), lambda k, w: (k, 0, 0))

    def body(w_ref, a_ref, b_ref, o_ref):
        del w_ref
        o_ref[...] = (a_ref[...].astype(F32) + b_ref[...].astype(F32)).astype(o_ref.dtype)

    return pl.pallas_call(
        body, name=name,
        grid_spec=pltpu.PrefetchScalarGridSpec(
            num_scalar_prefetch=1, grid=(N_CHIPS,),
            in_specs=[pl.BlockSpec((None, h, SLAB), lambda k, w: (k, w[1], 0)), spec], out_specs=spec),
        out_shape=jax.ShapeDtypeStruct(theirs.shape, theirs.dtype),
        compiler_params=_cparams(("parallel",)))(where, grad, theirs)


def add_chips(pair, got, where, *, name):
    h = pair.shape[1]
    tr = h // 2

    def body(w_ref, a_ref, b_ref, o_ref):
        del w_ref
        acc = a_ref[...].astype(F32)
        for f in range(3):
            acc = acc + b_ref[f].astype(F32)
        o_ref[...] = acc

    return pl.pallas_call(
        body, name=name,
        grid_spec=pltpu.PrefetchScalarGridSpec(
            num_scalar_prefetch=1, grid=(2,),
            in_specs=[pl.BlockSpec((None, tr, SLAB), lambda i, w: (w[0], i, 0)),
                      pl.BlockSpec((3, tr, SLAB), lambda i, w: (0, i, 0))],
            out_specs=pl.BlockSpec((tr, SLAB), lambda i, w: (2 * w[1] + i, 0))),
        out_shape=jax.ShapeDtypeStruct((2 * h, SLAB), F32),
        compiler_params=_cparams(("parallel",)))(where, pair, got)


DEPTH = 4
MIXER = (0, 1, 2, 0)
W_IN_COLS = (768, 320, 772)
W_IN_PAD = (768, 512, 1024)
MATS = ("up", "down", "inp", "out", "gate", "proj")
MAT_ARG = dict(up="w_up", down="w_down", inp="w_in", out="w_out", gate="w_ple_gate", proj="w_ple_proj")
GAINS = ("attn_norm", "mlp_norm", "ple_norm")
N_SMALL = 16
KINDS = ("grad_", "delta_", "new_m_", "new_v_")


def _layout(kind):
    ns_in = W_IN_PAD[kind] // SLAB
    off = 8192 + 1024 * ns_in
    lay = dict(up=(0, 1024, 4, False), down=(4096, 1024, 4, True), inp=(8192, 1024, ns_in, False),
               out=(off, 256, 4, True), gate=(off + 1024, 256, 4, True), proj=(off + 2048, 256, 1, False))
    return lay, off + 2304


def _to_slabs(w):
    k, c = w.shape
    return w.reshape(k, c // SLAB, SLAB).transpose(1, 0, 2).reshape(-1, SLAB)


def _pad_cols(w, n):
    return jnp.pad(w, ((0, 0), (0, n - w.shape[1])))


def _heads(x2d, n):
    return x2d.reshape(x2d.shape[0], n, HEAD_DIM).transpose(1, 0, 2)


def _unheads(x3d):
    n, s, _ = x3d.shape
    return x3d.transpose(1, 0, 2).reshape(s, n * HEAD_DIM)


def _chip_cols(x2d, c, cpad):
    s = x2d.shape[0]
    return jnp.pad(x2d.reshape(s, N_CHIPS, c), ((0, 0), (0, 0), (0, cpad - c))).reshape(s, N_CHIPS * cpad)


def _unchip_cols(x2d, c, cpad):
    s = x2d.shape[0]
    return x2d.reshape(s, N_CHIPS, cpad)[:, :, :c].reshape(s, N_CHIPS * c)


def _forget_cols(wg, t):
    off, K, _, _ = t
    cols = []
    for g in range(3 * N_HEADS * HEAD_DIM, 3 * N_HEADS * HEAD_DIM + N_HEADS):
        chip, local = divmod(g, W_IN_COLS[2])
        q, c = divmod(local, SLAB)
        cols.append(wg[chip, off + q * K:off + (q + 1) * K, c:c + 1])
    return jnp.concatenate(cols, axis=1)


def _add_res(acc, res):
    return (acc + res,)


def _relu2(acc):
    return acc, jnp.square(jnp.maximum(acc, 0.0))


def _relu2_bwd(acc, u):
    return (acc * (2.0 * jnp.maximum(u.astype(F32), 0.0)),)


def _ple_fwd(acc, x2, pp):
    return x2 + pp * _sigmoid(acc), acc


def _ple_bwd(dx, pp, gl):
    gate = _sigmoid(gl)
    return dx * gate, dx * pp * gate * (1.0 - gate)


def _layer_fwd(i, kind, x0, p_bf, wg, lay, gains, extra, tabs):
    s = x0.shape[0]
    an, mn, pn = gains
    sv = dict(x0=x0)
    h1 = rms_fwd(x0, an, name=f"attn_norm_{i}")
    if kind == 0:
        proj = mm_nn(h1, wg, lay["inp"], name=f"w_in_{i}")[0]
        a, tot = sb_fwd(proj, name=f"sb_fwd_{i}")
        sv.update(proj=proj, tot=tot)
    elif kind == 1:
        projp = mm_nn(h1, wg, lay["inp"], name=f"w_in_{i}", out_dtypes=(F32,))[0]
        proj = _unchip_cols(projp, W_IN_COLS[1], W_IN_PAD[1])
        nq = N_HEADS * HEAD_DIM
        nqk = nq + SWA_KV_HEADS * HEAD_DIM
        qk = rope_fwd(proj[:, :nqk], tabs, name=f"rope_{i}")
        q = _heads(qk[:, :nq], N_HEADS).reshape(SWA_KV_HEADS, SWA_GROUP, s, HEAD_DIM)
        front = ((0, 0), (SWA_WINDOW, 0), (0, 0))
        kp = jnp.pad(_heads(qk[:, nq:], SWA_KV_HEADS), front)
        vp = jnp.pad(_heads(proj[:, nqk:].astype(BF16), SWA_KV_HEADS), front)
        sink = jnp.repeat(extra.reshape(SWA_KV_HEADS, SWA_GROUP), ATT_BLK, axis=1)[:, :, None]
        o4, lse = swa_fwd(q, kp, vp, sink, name=f"swa_fwd_{i}")
        a = _unheads(o4.reshape(N_HEADS, s, HEAD_DIM))
        sv.update(q=q, kp=kp, vp=vp, sink=sink, o4=o4, lse=lse)
    else:
        projp = mm_nn(h1, wg, lay["inp"], name=f"w_in_{i}")[0]
        nqkv = 3 * N_HEADS * HEAD_DIM
        proj = _unchip_cols(projp, W_IN_COLS[2], W_IN_PAD[2])[:, :nqkv]
        fl = mm_plain(h1, _pad_cols(_forget_cols(wg, lay["inp"]), 128), name=f"w_forget_{i}")
        bp = _pad_cols(extra[None], 128)
        cum_t = fox_gate_fwd(fl, bp, name=f"gate_fwd_{i}")[:, :N_HEADS].T
        cq = cum_t[:, :, None]
        ck = cum_t.reshape(N_HEADS, s // min(ATT_BK, s), 1, min(ATT_BK, s))
        a, lse = fox_fwd(proj, cq, ck, name=f"fox_fwd_{i}")
        sv.update(proj=proj, fl=fl, bp=bp, cq=cq, ck=ck, lse=lse)
    x1 = mm_nn(a, wg, lay["out"], name=f"w_out_{i}", epi=_add_res, extras=(x0,), out_dtypes=(F32,))[0]
    h2 = rms_fwd(x1, mn, name=f"mlp_norm_{i}")
    u, r = mm_nn(h2, wg, lay["up"], name=f"w_up_{i}", epi=_relu2, out_dtypes=(BF16, BF16))
    x2 = mm_nn(r, wg, lay["down"], name=f"w_down_{i}", epi=_add_res, extras=(x1,), out_dtypes=(F32,))[0]
    h3 = rms_fwd(x2, pn, name=f"ple_norm_{i}")
    pp = mm_nn(p_bf, wg, lay["proj"], name=f"w_ple_proj_{i}", out_dtypes=(F32,))[0]
    x3, gl = mm_nn(h3, wg, lay["gate"], name=f"w_ple_gate_{i}", epi=_ple_fwd, extras=(x2, pp), out_dtypes=(F32, F32))
    sv.update(h1=h1, a=a, x1=x1, h2=h2, u=u, r=r, x2=x2, h3=h3, pp=pp, gl=gl)
    return x3, sv


def _layer_bwd(i, kind, dx3, sv, p_bf, wg, lay, n_rows, gains, tabs):
    s = dx3.shape[0]
    an, mn, pn = gains
    g = lax.empty((N_CHIPS, n_rows, SLAB), BF16)
    d_pp, d_gl = ew(_ple_bwd, [dx3, sv["pp"], sv["gl"]], [BF16, BF16], name=f"ple_bwd_{i}")
    g = mm_tn(p_bf, d_pp, g, lay["proj"], name=f"dw_ple_proj_{i}")
    g = mm_tn(sv["h3"], d_gl, g, lay["gate"], name=f"dw_ple_gate_{i}")
    d_h3 = mm_nt(d_gl, wg, lay["gate"], name=f"dx_ple_gate_{i}", out_dtypes=(F32,))[0]
    dx2, dx2b, d_pn = rms_bwd(sv["x2"], pn, d_h3, dx3, name=f"ple_norm_bwd_{i}")
    g = mm_tn(sv["r"], dx2b, g, lay["down"], name=f"dw_down_{i}")
    d_u = mm_nt(dx2b, wg, lay["down"], name=f"dx_down_{i}", epi=_relu2_bwd, extras=(sv["u"],))[0]
    g = mm_tn(sv["h2"], d_u, g, lay["up"], name=f"dw_up_{i}")
    d_h2 = mm_nt(d_u, wg, lay["up"], name=f"dx_up_{i}", out_dtypes=(F32,))[0]
    dx1, dx1b, d_mn = rms_bwd(sv["x1"], mn, d_h2, dx2, name=f"mlp_norm_bwd_{i}")
    g = mm_tn(sv["a"], dx1b, g, lay["out"], name=f"dw_out_{i}")
    d_a = mm_nt(dx1b, wg, lay["out"], name=f"dx_out_{i}")[0]
    d_extra = None
    if kind == 0:
        d_proj = jnp.concatenate(sb_bwd(sv["proj"], sv["tot"], d_a, name=f"sb_bwd_{i}"), axis=1)
    elif kind == 1:
        do4 = _heads(d_a, N_HEADS).reshape(SWA_KV_HEADS, SWA_GROUP, s, HEAD_DIM)
        dq, dkp, dvp, dsr = swa_bwd(sv["q"], sv["kp"], sv["vp"], sv["sink"], sv["o4"], sv["lse"], do4, name=f"swa_bwd_{i}")
        dqk = jnp.concatenate([_unheads(dq.reshape(N_HEADS, s, HEAD_DIM)), _unheads(dkp[:, SWA_WINDOW:])], axis=1)
        dqk = rope_bwd(dqk, tabs, name=f"rope_bwd_{i}")
        d_proj = jnp.concatenate([dqk, _unheads(dvp[:, SWA_WINDOW:]).astype(BF16)], axis=1)
        d_proj = _chip_cols(d_proj, W_IN_COLS[1], W_IN_PAD[1])
        d_extra = jnp.sum(dsr[..., 0], axis=2).reshape(N_HEADS)
    else:
        dq, dk, dv, dcq, dck = fox_bwd(sv["proj"], sv["a"], sv["lse"], sv["cq"], sv["ck"], d_a, name=f"fox_bwd_{i}")
        dcum = _pad_cols((dcq[:, :, 0] - dck.reshape(N_HEADS, s)).T, 128)
        dfl, dbp = fox_gate_bwd(dcum, sv["fl"], sv["bp"], name=f"gate_bwd_{i}")
        d_proj = jnp.concatenate([dq, dk, dv, dfl[:, :N_HEADS].astype(BF16)], axis=1)
        d_proj = _chip_cols(d_proj, W_IN_COLS[2], W_IN_PAD[2])
        d_extra = dbp[0, :N_HEADS]
    g = mm_tn(sv["h1"], d_proj, g, lay["inp"], name=f"dw_in_{i}")
    d_h1 = mm_nt(d_proj, wg, lay["inp"], name=f"dx_in_{i}", out_dtypes=(F32,))[0]
    dx0, _, d_an = rms_bwd(sv["x0"], an, d_h1, dx1, name=f"attn_norm_bwd_{i}")
    return dx0, g, (d_an, d_mn, d_pn), d_extra


def _small_rows(a, prefix):
    rows = [a[f"{prefix}{n}_{i}"] for i in range(DEPTH) for n in GAINS] + [a[f"{prefix}final_norm"]]
    rows += [_pad_cols(a[f"{prefix}{n}"][None], D_MODEL)[0] for n in ("sinks_1", "b_forget_2")]
    return jnp.stack(rows + [jnp.zeros((D_MODEL,), F32)])


def _train_step(a):
    x = a["x"][0]
    tabs = rope_tables(x.shape[0], (N_HEADS + SWA_KV_HEADS) * HEAD_DIM)
    lays = [_layout(k) for k in MIXER]

    def natural(prefix, i, m):
        w = a[f"{prefix}{MAT_ARG[m]}_{i}"]
        return _pad_cols(w, W_IN_PAD[MIXER[i]]) if m == "inp" else w

    chip = 2 * lax.axis_index("x") + lax.axis_index("y")
    where = jnp.stack([chip, lax.axis_index("c")]).astype(jnp.int32)
    def own_block(i, zero):
        pk = jnp.concatenate([_to_slabs((natural("", i, m) + zero).astype(BF16)) for m in MATS], axis=0)
        return lax.dynamic_update_slice(lax.empty((N_CHIPS,) + pk.shape, BF16), pk[None], (chip, 0, 0))

    sends, recvs, bufs, token = allgather_start([own_block(0, 0.0)], name="allgather_start_0")
    more = allgather_start([own_block(i, token[0, 0]) for i in range(1, DEPTH)], name="allgather_start_1")
    sends, recvs, bufs, token = sends + more[0], recvs + more[1], bufs + more[2], more[3]

    gains = [tuple(a[f"{n}_{i}"][None] for n in GAINS) for i in range(DEPTH)]
    extras = [None, a["sinks_1"], a["b_forget_2"], None]
    p_bf = [a["p"][i, 0].astype(BF16) for i in range(DEPTH)]

    saved, wgs, after = [], [], token
    for i in range(DEPTH):
        landed = allgather_wait(bufs[i], sends[i], recvs[i], after, name=f"allgather_wait_{i}")
        wgs.append(allgather_forward(landed, name=f"allgather_forward_{i}"))
        x, sv = _layer_fwd(i, MIXER[i], x, p_bf[i], wgs[i], lays[i][0], gains[i], extras[i], tabs)
        saved.append(sv)
        after = x
    dx, d_final, loss = loss_head(x, a["final_norm"][None], a["loss_target"][0], name="loss_head")

    def finish(i, started, after):
        send, recv, part, land, _ = started
        part, got = scatter_wait(part, land, send, recv, after, name=f"scatter_wait_{i}")
        return join_halves([add_chips(part, got, where, name=f"add_chips_{i}")], name=f"join_halves_{i}")[0]

    small = [None] * N_SMALL
    small[12] = d_final[0]
    small[15] = _pad_cols(loss[:, :1], D_MODEL)[0]
    gfull = [None] * DEPTH
    started = None
    for i in reversed(range(DEPTH)):
        an, mn, pn = gains[i]
        if started is not None:
            pn = pn + started[4]
        dx, grad, d_gains, d_extra = _layer_bwd(i, MIXER[i], dx, saved[i], p_bf[i], wgs[i], lays[i][0], lays[i][1],
                                                (an, mn, pn), tabs)
        for j in range(3):
            small[3 * i + j] = d_gains[j][0]
        if d_extra is not None:
            small[12 + MIXER[i]] = _pad_cols(d_extra[None], D_MODEL)[0]
        if started is not None:
            gfull[i + 1] = finish(i + 1, started, dx)
        theirs = swap_halves([grad], name=f"swap_halves_{i}")[0]
        started = scatter_start(add_pairs(grad, theirs, where, name=f"add_pairs_{i}"), name=f"scatter_start_{i}")
    small = allreduce_small(jnp.stack(small), name="allreduce_small")

    out = {"loss": small[15, 0], "grad_x": dx[None]}
    res = adamw_small(small, _small_rows(a, ""), _small_rows(a, "m_"), _small_rows(a, "v_"), name="adamw_small")
    for i in reversed(range(DEPTH)):
        if i == 0:
            gfull[0] = finish(0, started, out[f"delta_{MAT_ARG[MATS[-1]]}_1"])
        for m in MATS:
            upd = adamw(gfull[i], lays[i][0][m], natural("", i, m), natural("m_", i, m), natural("v_", i, m), started[4],
                        name=f"adamw_{MAT_ARG[m]}_{i}")
            cols = a[f"{MAT_ARG[m]}_{i}"].shape[1]
            for kd, r in zip(KINDS, upd):
                out[f"{kd}{MAT_ARG[m]}_{i}"] = r[:, :cols]
    for kd, r in zip(KINDS, res):
        for i in range(DEPTH):
            for j, n in enumerate(GAINS):
                out[f"{kd}{n}_{i}"] = r[3 * i + j]
        out[f"{kd}final_norm"] = r[12]
        out[f"{kd}sinks_1"] = r[13, :N_HEADS]
        out[f"{kd}b_forget_2"] = r[14, :N_HEADS]
    return out


def _weight_names():
    names = []
    for i in range(DEPTH):
        names += [f"attn_norm_{i}", f"w_in_{i}", f"w_out_{i}"] + [[], ["sinks_1"], ["b_forget_2"]][MIXER[i]]
        names += [f"mlp_norm_{i}", f"w_up_{i}", f"w_down_{i}", f"ple_norm_{i}", f"w_ple_gate_{i}", f"w_ple_proj_{i}"]
    return names + ["final_norm"]


def kernel(x, p, attn_norm_0, w_in_0, w_out_0, mlp_norm_0, w_up_0, w_down_0, ple_norm_0, w_ple_gate_0, w_ple_proj_0, attn_norm_1, w_in_1, w_out_1, sinks_1, mlp_norm_1, w_up_1, w_down_1, ple_norm_1, w_ple_gate_1, w_ple_proj_1, attn_norm_2, w_in_2, w_out_2, b_forget_2, mlp_norm_2, w_up_2, w_down_2, ple_norm_2, w_ple_gate_2, w_ple_proj_2, attn_norm_3, w_in_3, w_out_3, mlp_norm_3, w_up_3, w_down_3, ple_norm_3, w_ple_gate_3, w_ple_proj_3, final_norm, loss_target, m_attn_norm_0, m_w_in_0, m_w_out_0, m_mlp_norm_0, m_w_up_0, m_w_down_0, m_ple_norm_0, m_w_ple_gate_0, m_w_ple_proj_0, m_attn_norm_1, m_w_in_1, m_w_out_1, m_sinks_1, m_mlp_norm_1, m_w_up_1, m_w_down_1, m_ple_norm_1, m_w_ple_gate_1, m_w_ple_proj_1, m_attn_norm_2, m_w_in_2, m_w_out_2, m_b_forget_2, m_mlp_norm_2, m_w_up_2, m_w_down_2, m_ple_norm_2, m_w_ple_gate_2, m_w_ple_proj_2, m_attn_norm_3, m_w_in_3, m_w_out_3, m_mlp_norm_3, m_w_up_3, m_w_down_3, m_ple_norm_3, m_w_ple_gate_3, m_w_ple_proj_3, m_final_norm, v_attn_norm_0, v_w_in_0, v_w_out_0, v_mlp_norm_0, v_w_up_0, v_w_down_0, v_ple_norm_0, v_w_ple_gate_0, v_w_ple_proj_0, v_attn_norm_1, v_w_in_1, v_w_out_1, v_sinks_1, v_mlp_norm_1, v_w_up_1, v_w_down_1, v_ple_norm_1, v_w_ple_gate_1, v_w_ple_proj_1, v_attn_norm_2, v_w_in_2, v_w_out_2, v_b_forget_2, v_mlp_norm_2, v_w_up_2, v_w_down_2, v_ple_norm_2, v_w_ple_gate_2, v_w_ple_proj_2, v_attn_norm_3, v_w_in_3, v_w_out_3, v_mlp_norm_3, v_w_up_3, v_w_down_3, v_ple_norm_3, v_w_ple_gate_3, v_w_ple_proj_3, v_final_norm):
    out = _train_step(dict(locals()))
    return (out["loss"], out["grad_x"], *[out[kd + n] for kd in KINDS for n in _weight_names()])
```

```python
import jax
import jax.numpy as jnp
from jax import lax
from jax.experimental import pallas as pl
from jax.experimental.pallas import tpu as pltpu

F32 = jnp.float32
BF16 = jnp.bfloat16

D_MODEL = 1024
N_HEADS = 16
HEAD_DIM = 64
SWA_KV_HEADS = 2
SWA_GROUP = 8
SWA_WINDOW = 128
ROPE_THETA = 500000.0
ROPE_DIM = 16
RMS_EPS = 1e-6
NEG_INF = -1e30
ATTN_SCALE = HEAD_DIM ** -0.5
N_CHIPS = 4
N_DEVICES = 8

SLAB = 256
ATT_BLK = 128
ATT_BQ = 512
ATT_BK = 512
ROW_TILE = 256
V7X_VMEM_LIMIT = 56 * 1024 * 1024

ADAM_LR, ADAM_B1, ADAM_B2, ADAM_EPS, ADAM_WD, ADAM_STEP = 0.001, 0.9, 0.999, 1e-08, 0.01, 10


def _cparams(sem=None):
    return pltpu.CompilerParams(dimension_semantics=sem, vmem_limit_bytes=V7X_VMEM_LIMIT)


def _dot(a, b):
    return jnp.dot(a, b, preferred_element_type=F32)


def _dot_nt(a, b):
    return lax.dot_general(a, b, (((1,), (1,)), ((), ())), preferred_element_type=F32)


def _dot_tn(a, b):
    return lax.dot_general(a, b, (((0,), (0,)), ((), ())), preferred_element_type=F32)


def _row_tile(M, K):
    return min(M, 1024) if K >= 1024 else M


def _finish(epi, acc, ex, outs):
    res = epi(acc, *[e[...] for e in ex]) if epi is not None else (acc,)
    for o, r in zip(outs, res):
        o[...] = r.astype(o.dtype)


def mm_nn(a, wg, t, *, name, epi=None, extras=(), out_dtypes=(BF16,)):
    off, K, ns, row = t
    M = a.shape[0]
    sb = off // K
    ne, no = len(extras), len(out_dtypes)
    if row:
        tm = _row_tile(M, K)
        nb = N_CHIPS
        grid = (M // tm, ns)
        a_spec = pl.BlockSpec((tm, N_CHIPS * K), lambda i, q: (i, 0))
        b_specs = [pl.BlockSpec((None, K, SLAB), lambda i, q, j=j: (j, sb + q, 0)) for j in range(nb)]
        tile = pl.BlockSpec((tm, SLAB), lambda i, q: (i, q))
        n_out = ns * SLAB
    else:
        nb = ns
        grid = (N_CHIPS,)
        a_spec = pl.BlockSpec((M, K), lambda j: (0, 0))
        b_specs = [pl.BlockSpec((None, K, SLAB), lambda j, q=q: (j, sb + q, 0)) for q in range(ns)]
        tile = pl.BlockSpec((M, ns * SLAB), lambda j: (0, j))
        n_out = N_CHIPS * ns * SLAB

    def body(a_ref, *rest):
        bs, ex, outs = rest[:nb], rest[nb:nb + ne], rest[nb + ne:]
        if row:
            acc = _dot(a_ref[:, pl.ds(0, K)], bs[0][...])
            for j in range(1, nb):
                acc = acc + _dot(a_ref[:, pl.ds(j * K, K)], bs[j][...])
            _finish(epi, acc, ex, outs)
        else:
            av = a_ref[...]
            for q in range(ns):
                cols = pl.ds(q * SLAB, SLAB)
                _finish(epi, _dot(av, bs[q][...]), [e.at[:, cols] for e in ex], [o.at[:, cols] for o in outs])

    return pl.pallas_call(
        body, name=name, grid=grid,
        in_specs=[a_spec] + b_specs + [tile] * ne, out_specs=[tile] * no,
        out_shape=[jax.ShapeDtypeStruct((M, n_out), d) for d in out_dtypes],
        compiler_params=_cparams(("parallel",) * len(grid)),
    )(a, *([wg] * nb), *extras)


def mm_nt(dy, wg, t, *, name, epi=None, extras=(), out_dtypes=(BF16,)):
    off, K, ns, row = t
    M = dy.shape[0]
    tm = _row_tile(M, K)
    sb = off // K
    ne, no = len(extras), len(out_dtypes)
    grid = (M // tm, N_CHIPS)
    b_specs = [pl.BlockSpec((None, K, SLAB), lambda i, j, q=q: (j, sb + q, 0)) for q in range(ns)]
    if row:
        dy_spec = pl.BlockSpec((tm, ns * SLAB), lambda i, j: (i, 0))
        tile = pl.BlockSpec((tm, K), lambda i, j: (i, j))
        n_out = N_CHIPS * K
        sem = ("parallel", "parallel")
    else:
        dy_spec = pl.BlockSpec((tm, ns * SLAB), lambda i, j: (i, j))
        tile = pl.BlockSpec((tm, K), lambda i, j: (i, 0))
        n_out = K
        sem = ("parallel", "arbitrary")

    def body(dy_ref, *rest):
        bs, ex, outs = rest[:ns], rest[ns:ns + ne], rest[ns + ne:ns + ne + no]
        part = _dot_nt(dy_ref[:, pl.ds(0, SLAB)], bs[0][...])
        for q in range(1, ns):
            part = part + _dot_nt(dy_ref[:, pl.ds(q * SLAB, SLAB)], bs[q][...])
        if row:
            _finish(epi, part, ex, outs)
        else:
            acc_ref = rest[-1]
            j = pl.program_id(1)

            @pl.when(j == 0)
            def _():
                acc_ref[...] = part

            @pl.when(j > 0)
            def _():
                acc_ref[...] += part

            @pl.when(j == N_CHIPS - 1)
            def _():
                _finish(epi, acc_ref[...], ex, outs)

    return pl.pallas_call(
        body, name=name, grid=grid,
        in_specs=[dy_spec] + b_specs + [tile] * ne, out_specs=[tile] * no,
        out_shape=[jax.ShapeDtypeStruct((M, n_out), d) for d in out_dtypes],
        scratch_shapes=[] if row else [pltpu.VMEM((tm, K), F32)],
        compiler_params=_cparams(sem),
    )(dy, *([wg] * ns), *extras)


def mm_plain(a, b, *, name):
    M, K = a.shape
    N = b.shape[1]
    tm = min(M, 512)

    def body(a_ref, b_ref, o_ref):
        o_ref[...] = _dot(a_ref[...], b_ref[...])

    return pl.pallas_call(
        body, name=name, grid=(M // tm,),
        in_specs=[pl.BlockSpec((tm, K), lambda i: (i, 0)), pl.BlockSpec((K, N), lambda i: (0, 0))],
        out_specs=pl.BlockSpec((tm, N), lambda i: (i, 0)), out_shape=jax.ShapeDtypeStruct((M, N), F32),
        compiler_params=_cparams(("parallel",)),
    )(a, b)


def mm_tn(x, dy, g, t, *, name):
    off, K, ns, row = t
    S = x.shape[0]
    per = ns if off % (ns * K) == 0 else 1
    grid = (N_CHIPS, ns // per)
    if row:
        x_map = lambda j, q: (0, j)
        dy_map = lambda j, q: (0, q)
    else:
        x_map = lambda j, q: (0, 0)
        dy_map = lambda j, q: (0, j * (ns // per) + q)

    def body(g_in, x_ref, dy_ref, o_ref):
        del g_in
        xt = x_ref[...].T
        for q in range(per):
            o_ref[pl.ds(q * K, K), :] = _dot(xt, dy_ref[:, pl.ds(q * SLAB, SLAB)]).astype(o_ref.dtype)

    return pl.pallas_call(
        body, name=name, grid=grid,
        in_specs=[pl.BlockSpec(memory_space=pl.ANY), pl.BlockSpec((S, K), x_map), pl.BlockSpec((S, per * SLAB), dy_map)],
        out_specs=pl.BlockSpec((None, per * K, SLAB), lambda j, q: (j, off // (per * K) + q, 0)),
        out_shape=jax.ShapeDtypeStruct(g.shape, g.dtype),
        input_output_aliases={0: 0},
        compiler_params=_cparams(("parallel", "parallel")),
    )(g, x, dy)


def ew(fn, ins, out_dtypes, *, name, bcast=()):
    S = ins[0].shape[0]
    tr = min(ROW_TILE, S)
    cols = ins[0].shape[1]
    ni, nb = len(ins), len(bcast)

    def body(*refs):
        res = fn(*[r[...] for r in refs[:ni + nb]])
        for o, r in zip(refs[ni + nb:], res):
            o[...] = r.astype(o.dtype)

    return pl.pallas_call(
        body, name=name, grid=(S // tr,),
        in_specs=[pl.BlockSpec((tr, a.shape[1]), lambda i: (i, 0)) for a in ins]
        + [pl.BlockSpec(b.shape, lambda i: (0, 0)) for b in bcast],
        out_specs=[pl.BlockSpec((tr, cols), lambda i: (i, 0)) for _ in out_dtypes],
        out_shape=[jax.ShapeDtypeStruct((S, cols), d) for d in out_dtypes],
        compiler_params=_cparams(("parallel",)),
    )(*ins, *bcast)


def _rstd(x):
    return lax.rsqrt(jnp.mean(x * x, axis=-1, keepdims=True) + RMS_EPS)


def _sigmoid(x):
    return 1.0 / (1.0 + jnp.exp(-x))


def _log_sigmoid(z):
    return jnp.minimum(z, 0.0) - jnp.log(1.0 + jnp.exp(-jnp.abs(z)))


def rms_fwd(x, g, *, name):
    return ew(lambda xv, gv: (xv * _rstd(xv) * gv,), [x], [BF16], name=name, bcast=[g])[0]


def _rms_bwd_tile(xv, gv, dh):
    rstd = _rstd(xv)
    xhat = xv * rstd
    gd = dh * gv
    dx = rstd * (gd - xhat * jnp.mean(xhat * gd, axis=-1, keepdims=True))
    return dx, jnp.sum(dh * xhat, axis=0, keepdims=True)


def rms_bwd(x, g, dh, dres, *, name):
    S, D = x.shape
    tr = min(ROW_TILE, S)

    def body(x_ref, g_ref, dh_ref, dres_ref, dx_ref, dxb_ref, dg_ref):
        i = pl.program_id(0)
        dx, dg = _rms_bwd_tile(x_ref[...], g_ref[...], dh_ref[...])
        dx = dx + dres_ref[...]
        dx_ref[...] = dx
        dxb_ref[...] = dx.astype(BF16)

        @pl.when(i == 0)
        def _():
            dg_ref[...] = dg

        @pl.when(i > 0)
        def _():
            dg_ref[...] += dg

    row = pl.BlockSpec((tr, D), lambda i: (i, 0))
    one = pl.BlockSpec((1, D), lambda i: (0, 0))
    return pl.pallas_call(
        body, name=name, grid=(S // tr,),
        in_specs=[row, one, row, row], out_specs=[row, row, one],
        out_shape=[jax.ShapeDtypeStruct((S, D), F32), jax.ShapeDtypeStruct((S, D), BF16),
                   jax.ShapeDtypeStruct((1, D), F32)],
        compiler_params=_cparams(("arbitrary",)),
    )(x, g, dh, dres)


def loss_head(x, g, target, *, name):
    S, D = x.shape
    tr = min(ROW_TILE, S)

    def body(x_ref, g_ref, t_ref, dx_ref, dg_ref, loss_ref):
        i = pl.program_id(0)
        xv, gv = x_ref[...], g_ref[...]
        err = xv * _rstd(xv) * gv - t_ref[...]
        part = 0.5 * jnp.sum(jnp.mean(err * err, axis=-1, keepdims=True), axis=0, keepdims=True)
        dx, dg = _rms_bwd_tile(xv, gv, err * (1.0 / D))
        dx_ref[...] = dx
        part = jnp.broadcast_to(part, loss_ref.shape)

        @pl.when(i == 0)
        def _():
            dg_ref[...] = dg
            loss_ref[...] = part

        @pl.when(i > 0)
        def _():
            dg_ref[...] += dg
            loss_ref[...] += part

    row = pl.BlockSpec((tr, D), lambda i: (i, 0))
    one = pl.BlockSpec((1, D), lambda i: (0, 0))
    return pl.pallas_call(
        body, name=name, grid=(S // tr,),
        in_specs=[row, one, row], out_specs=[row, one, pl.BlockSpec((1, 128), lambda i: (0, 0))],
        out_shape=[jax.ShapeDtypeStruct((S, D), F32), jax.ShapeDtypeStruct((1, D), F32),
                   jax.ShapeDtypeStruct((1, 128), F32)],
        compiler_params=_cparams(("arbitrary",)),
    )(x, g, target)


def rope_tables(S, n_cols):
    half = ROPE_DIM // 2
    inv_freq = ROPE_THETA ** (-jnp.arange(half, dtype=F32) / half)
    ang = jnp.arange(S, dtype=F32)[:, None] * inv_freq[None, :]
    cos, sin = jnp.cos(ang), jnp.sin(ang)
    z = jnp.zeros((S, HEAD_DIM - ROPE_DIM), F32)
    zh = jnp.zeros((S, half), F32)
    c = jnp.concatenate([cos, cos, jnp.ones_like(z)], axis=1)
    sa = jnp.concatenate([zh, sin, z], axis=1)
    sb = jnp.concatenate([-sin, zh, z], axis=1)
    return [jnp.tile(t, (1, n_cols // HEAD_DIM)) for t in (c, sa, sb)]


def rope_fwd(xqk, tables, *, name):
    n, half = xqk.shape[1], ROPE_DIM // 2

    def fn(x, c, sa, sb):
        return (x * c + pltpu.roll(x, half, 1) * sa + pltpu.roll(x, n - half, 1) * sb,)

    return ew(fn, [xqk] + list(tables), [BF16], name=name)[0]


def rope_bwd(dy, tables, *, name):
    n, half = dy.shape[1], ROPE_DIM // 2

    def fn(d, c, sa, sb):
        return (d * c + pltpu.roll(d * sa, n - half, 1) + pltpu.roll(d * sb, half, 1),)

    return ew(fn, [dy] + list(tables), [BF16], name=name)[0]


def _split3(x):
    h1 = x.astype(BF16)
    r1 = x - h1.astype(F32)
    h2 = r1.astype(BF16)
    return h1, h2, (r1 - h2.astype(F32)).astype(BF16)


def _split2(x):
    h1 = x.astype(BF16)
    return h1, (x - h1.astype(F32)).astype(BF16)


def _tri(n, cmp):
    r = lax.broadcasted_iota(jnp.int32, (n, n), 0)
    c = lax.broadcasted_iota(jnp.int32, (n, n), 1)
    return cmp(r, c).astype(BF16)


def fox_gate_fwd(fl, b, *, name):
    S, W = fl.shape
    tr = min(ROW_TILE, S)

    def body(fl_ref, b_ref, cum_ref, carry):
        i = pl.program_id(0)

        @pl.when(i == 0)
        def _():
            carry[...] = jnp.zeros_like(carry)

        lower = _tri(tr, lambda r, c: r >= c)
        cs = carry[...]
        for piece in _split3(_log_sigmoid(fl_ref[...] + b_ref[...])):
            cs = cs + _dot(lower, piece)
        cum_ref[...] = cs
        carry[...] = cs[tr - 1:tr, :]

    return pl.pallas_call(
        body, name=name, grid=(S // tr,),
        in_specs=[pl.BlockSpec((tr, W), lambda i: (i, 0)), pl.BlockSpec((1, W), lambda i: (0, 0))],
        out_specs=pl.BlockSpec((tr, W), lambda i: (i, 0)),
        out_shape=jax.ShapeDtypeStruct((S, W), F32),
        scratch_shapes=[pltpu.VMEM((1, W), F32)],
        compiler_params=_cparams(("arbitrary",)),
    )(fl, b)


def fox_gate_bwd(dcum, fl, b, *, name):
    S, W = fl.shape
    tr = min(ROW_TILE, S)
    nb = S // tr

    def body(dc_ref, fl_ref, b_ref, dfl_ref, db_ref, carry):
        i = pl.program_id(0)

        @pl.when(i == 0)
        def _():
            carry[...] = jnp.zeros_like(carry)

        upper = _tri(tr, lambda r, c: r <= c)
        cs = carry[...]
        for piece in _split3(dc_ref[...]):
            cs = cs + _dot(upper, piece)
        carry[...] = cs[0:1, :]
        dfl = cs * _sigmoid(-(fl_ref[...] + b_ref[...]))
        dfl_ref[...] = dfl
        db = jnp.sum(dfl, axis=0, keepdims=True)

        @pl.when(i == 0)
        def _():
            db_ref[...] = db

        @pl.when(i > 0)
        def _():
            db_ref[...] += db

    rev = pl.BlockSpec((tr, W), lambda i: (nb - 1 - i, 0))
    one = pl.BlockSpec((1, W), lambda i: (0, 0))
    return pl.pallas_call(
        body, name=name, grid=(nb,),
        in_specs=[rev, rev, one], out_specs=[rev, one],
        out_shape=[jax.ShapeDtypeStruct((S, W), F32), jax.ShapeDtypeStruct((1, W), F32)],
        scratch_shapes=[pltpu.VMEM((1, W), F32)],
        compiler_params=_cparams(("arbitrary",)),
    )(dcum, fl, b)


def _blk_iota(tq, tk):
    return (lax.broadcasted_iota(jnp.int32, (tq, tk), 0), lax.broadcasted_iota(jnp.int32, (tq, tk), 1))


def _cs(xb, tri):
    return _dot(xb, tri)


def _rowsum(xb):
    return jnp.sum(xb.astype(F32), axis=1, keepdims=True)


def _sb_block(qs, k, cmr, shift):
    z = _dot_nt(qs, k)
    strict = cmr < shift
    lb = jnp.minimum(z, 0.0) - jnp.log(1.0 + jnp.exp(-jnp.abs(z)))
    lom = jnp.where(strict, lb - z, 0.0).astype(BF16)
    return lb, lom, strict


def _att_tiles(S):
    return min(ATT_BQ, S), min(ATT_BK, S)


PAIR = 2 * HEAD_DIM
N_PAIRS = N_HEADS // 2


def _pair_specs(S, tq):
    cols = D_MODEL // PAIR
    qspec = pl.BlockSpec((tq, PAIR), lambda p, i: (i, p))
    kspec = pl.BlockSpec((S, PAIR), lambda p, i: (0, cols + p))
    vspec = pl.BlockSpec((S, PAIR), lambda p, i: (0, 2 * cols + p))
    kvout = pl.BlockSpec((S, PAIR), lambda p, i: (0, p))
    vec = pl.BlockSpec((2, tq, 1), lambda p, i: (p, i, 0))
    return qspec, kspec, vspec, kvout, vec


def _head_lanes(h):
    lane = lax.broadcasted_iota(jnp.int32, (1, PAIR), 1)
    return (lane >= h * HEAD_DIM) & (lane < (h + 1) * HEAD_DIM)


def _only(sel, x):
    return jnp.where(sel, x, jnp.zeros_like(x))


def sb_fwd(proj, *, name):
    S = proj.shape[0]
    tq, tk = _att_tiles(S)
    qspec, kspec, vspec, _, vec = _pair_specs(S, tq)

    def body(q_ref, k_ref, v_ref, o_ref, t_ref):
        i = pl.program_id(1)
        row, col = _blk_iota(tq, tk)
        cmr = col - row
        below = _tri(tk, lambda r, c: r > c)
        nkb = (i + 1) * (tq // tk)
        out = []
        for h in range(2):
            sel = _head_lanes(h)
            qs = _only(sel, q_ref[...] * ATTN_SCALE)

            def step(n, carry):
                r_sum, acc = carry
                kb = nkb - 1 - n
                ks = pl.multiple_of(kb * tk, tk)
                lb, lom, strict = _sb_block(qs, k_ref[pl.ds(ks, tk), :], cmr, i * tq - kb * tk)
                w = jnp.where(strict, jnp.exp(lb + _cs(lom, below) + r_sum), 0.0)
                acc = acc + _dot(w.astype(BF16), _only(sel, v_ref[pl.ds(ks, tk), :]))
                return r_sum + _rowsum(lom), acc

            r_sum, acc = lax.fori_loop(0, nkb, step, (jnp.zeros((tq, 1), F32), jnp.zeros((tq, PAIR), F32)))
            t_ref[h] = r_sum
            out.append(acc)
        o_ref[...] = (out[0] + out[1]).astype(o_ref.dtype)

    return pl.pallas_call(
        body, name=name, grid=(N_PAIRS, S // tq),
        in_specs=[qspec, kspec, vspec], out_specs=[qspec, vec],
        out_shape=[jax.ShapeDtypeStruct((S, D_MODEL), BF16), jax.ShapeDtypeStruct((N_HEADS, S, 1), F32)],
        compiler_params=_cparams(("parallel", "arbitrary")),
    )(proj, proj, proj)


def sb_bwd(proj, tot, do, *, name):
    S = proj.shape[0]
    tq, tk = _att_tiles(S)
    qspec, kspec, vspec, kvout, vec = _pair_specs(S, tq)

    def body(q_ref, k_ref, v_ref, t_ref, do_ref, dq_ref, dk_out, dv_out, dk_ref, dv_ref):
        i = pl.program_id(1)

        @pl.when(i == 0)
        def _():
            dk_ref[...] = jnp.zeros_like(dk_ref)
            dv_ref[...] = jnp.zeros_like(dv_ref)

        row, col = _blk_iota(tq, tk)
        cmr = col - row
        upto = _tri(tk, lambda r, c: r <= c)
        before = _tri(tk, lambda r, c: r < c)
        out = []
        for h in range(2):
            sel = _head_lanes(h)
            qs, dov, t_all = _only(sel, q_ref[...] * ATTN_SCALE), _only(sel, do_ref[...]), t_ref[h]

            def step(kb, carry):
                p_sum, e_sum, dq = carry
                ks = pl.multiple_of(kb * tk, tk)
                kv = k_ref[pl.ds(ks, tk), :]
                lb, lom, strict = _sb_block(qs, kv, cmr, i * tq - kb * tk)
                tail = t_all - p_sum - _cs(lom, upto)
                w = jnp.where(strict, jnp.exp(lb + tail), 0.0)
                e = _dot_nt(dov, v_ref[pl.ds(ks, tk), :]) * w
                eb = e.astype(BF16)
                e_before = e_sum + _cs(eb, before)
                beta = jnp.exp(lb)
                dzb = jnp.where(strict, e - (e + e_before) * beta, 0.0).astype(BF16)
                dk_ref[pl.ds(ks, tk), :] += _dot_tn(dzb, qs)
                dv_ref[pl.ds(ks, tk), :] += _dot_tn(w.astype(BF16), dov)
                return p_sum + _rowsum(lom), e_sum + _rowsum(eb), dq + _dot(dzb, _only(sel, kv))

            zero = jnp.zeros((tq, 1), F32)
            out.append(lax.fori_loop(0, (i + 1) * (tq // tk), step, (zero, zero, jnp.zeros((tq, PAIR), F32)))[2])
        dq_ref[...] = ((out[0] + out[1]) * ATTN_SCALE).astype(dq_ref.dtype)

        @pl.when(i == S // tq - 1)
        def _():
            dk_out[...] = dk_ref[...].astype(dk_out.dtype)
            dv_out[...] = dv_ref[...].astype(dv_out.dtype)

    full = jax.ShapeDtypeStruct((S, D_MODEL), BF16)
    return pl.pallas_call(
        body, name=name, grid=(N_PAIRS, S // tq),
        in_specs=[qspec, kspec, vspec, vec, qspec], out_specs=[qspec, kvout, kvout],
        out_shape=[full, full, full],
        scratch_shapes=[pltpu.VMEM((S, PAIR), F32)] * 2,
        compiler_params=_cparams(("parallel", "arbitrary")),
    )(proj, proj, proj, tot, do)


def _fox_logits(qs, k, cq, ck, cmr, shift):
    causal = cmr <= shift
    return jnp.where(causal, _dot_nt(qs, k) + cq - ck, NEG_INF), causal


def fox_fwd(proj, cq, ck, *, name):
    S = proj.shape[0]
    tq, tk = _att_tiles(S)
    qspec, kspec, vspec, _, vec = _pair_specs(S, tq)
    ckspec = pl.BlockSpec((2, S // tk, 1, tk), lambda p, i: (p, 0, 0, 0))

    def body(q_ref, k_ref, v_ref, cq_ref, ck_ref, o_ref, lse_ref):
        i = pl.program_id(1)
        row, col = _blk_iota(tq, tk)
        cmr = col - row
        out = []
        for h in range(2):
            sel = _head_lanes(h)
            qs, cqv = _only(sel, q_ref[...] * ATTN_SCALE), cq_ref[h]

            def step(kb, carry):
                m, l, acc = carry
                ks = pl.multiple_of(kb * tk, tk)
                s, _ = _fox_logits(qs, k_ref[pl.ds(ks, tk), :], cqv, ck_ref[h, kb], cmr, i * tq - kb * tk)
                m_new = jnp.maximum(m, jnp.max(s, axis=1, keepdims=True))
                alpha = jnp.exp(m - m_new)
                p = jnp.exp(s - m_new)
                l = alpha * l + jnp.sum(p, axis=1, keepdims=True)
                acc = alpha * acc + _dot(p.astype(BF16), _only(sel, v_ref[pl.ds(ks, tk), :]))
                return m_new, l, acc

            m, l, acc = lax.fori_loop(0, (i + 1) * (tq // tk), step,
                                      (jnp.full((tq, 1), NEG_INF, F32), jnp.zeros((tq, 1), F32), jnp.zeros((tq, PAIR), F32)))
            lse_ref[h] = m + jnp.log(l)
            out.append(acc / l)
        o_ref[...] = (out[0] + out[1]).astype(o_ref.dtype)

    return pl.pallas_call(
        body, name=name, grid=(N_PAIRS, S // tq),
        in_specs=[qspec, kspec, vspec, vec, ckspec], out_specs=[qspec, vec],
        out_shape=[jax.ShapeDtypeStruct((S, D_MODEL), BF16), jax.ShapeDtypeStruct((N_HEADS, S, 1), F32)],
        compiler_params=_cparams(("parallel", "arbitrary")),
    )(proj, proj, proj, cq, ck)


def fox_bwd(proj, o, lse, cq, ck, do, *, name):
    S = proj.shape[0]
    tq, tk = _att_tiles(S)
    qspec, kspec, vspec, kvout, vec = _pair_specs(S, tq)
    ckspec = pl.BlockSpec((2, S // tk, 1, tk), lambda p, i: (p, 0, 0, 0))

    def body(q_ref, k_ref, v_ref, o_ref, lse_ref, cq_ref, ck_ref, do_ref, dq_ref, dk_out, dv_out, dcq_ref, dck_ref,
             dk_ref, dv_ref):
        i = pl.program_id(1)

        @pl.when(i == 0)
        def _():
            dk_ref[...] = jnp.zeros_like(dk_ref)
            dv_ref[...] = jnp.zeros_like(dv_ref)
            dck_ref[...] = jnp.zeros_like(dck_ref)

        row, col = _blk_iota(tq, tk)
        cmr = col - row
        out = []
        for h in range(2):
            sel = _head_lanes(h)
            qs, dov, cqv, lsev = _only(sel, q_ref[...] * ATTN_SCALE), _only(sel, do_ref[...]), cq_ref[h], lse_ref[h]
            delta = jnp.sum(dov.astype(F32) * o_ref[...].astype(F32), axis=1, keepdims=True)

            def step(kb, carry):
                dq, dcq = carry
                ks = pl.multiple_of(kb * tk, tk)
                kv = k_ref[pl.ds(ks, tk), :]
                s, causal = _fox_logits(qs, kv, cqv, ck_ref[h, kb], cmr, i * tq - kb * tk)
                p = jnp.where(causal, jnp.exp(s - lsev), 0.0)
                ds = p * (_dot_nt(dov, v_ref[pl.ds(ks, tk), :]) - delta)
                dck_ref[h, kb] += jnp.sum(ds, axis=0, keepdims=True)
                dsb = ds.astype(BF16)
                dk_ref[pl.ds(ks, tk), :] += _dot_tn(dsb, qs)
                dv_ref[pl.ds(ks, tk), :] += _dot_tn(p.astype(BF16), dov)
                return dq + _dot(dsb, _only(sel, kv)), dcq + jnp.sum(ds, axis=1, keepdims=True)

            dq, dcq = lax.fori_loop(0, (i + 1) * (tq // tk), step, (jnp.zeros((tq, PAIR), F32), jnp.zeros((tq, 1), F32)))
            dcq_ref[h] = dcq
            out.append(dq)
        dq_ref[...] = ((out[0] + out[1]) * ATTN_SCALE).astype(dq_ref.dtype)

        @pl.when(i == S // tq - 1)
        def _():
            dk_out[...] = dk_ref[...].astype(dk_out.dtype)
            dv_out[...] = dv_ref[...].astype(dv_out.dtype)

    full = jax.ShapeDtypeStruct((S, D_MODEL), BF16)
    return pl.pallas_call(
        body, name=name, grid=(N_PAIRS, S // tq),
        in_specs=[qspec, kspec, vspec, qspec, vec, vec, ckspec, qspec],
        out_specs=[qspec, kvout, kvout, vec, ckspec],
        out_shape=[full, full, full, jax.ShapeDtypeStruct((N_HEADS, S, 1), F32),
                   jax.ShapeDtypeStruct((N_HEADS, S // tk, 1, tk), F32)],
        scratch_shapes=[pltpu.VMEM((S, PAIR), F32)] * 2,
        compiler_params=_cparams(("parallel", "arbitrary")),
    )(proj, proj, proj, o, lse, cq, ck, do)


def _swa_specs(S, tq):
    qspec = pl.BlockSpec((None, SWA_GROUP, tq, HEAD_DIM), lambda g, i: (g, 0, i, 0))
    kvspec = pl.BlockSpec((None, S + SWA_WINDOW, HEAD_DIM), lambda g, i: (g, 0, 0))
    vec = pl.BlockSpec((None, SWA_GROUP, tq, 1), lambda g, i: (g, 0, i, 0))
    sink = pl.BlockSpec((None, SWA_GROUP * tq, 1), lambda g, i: (g, 0, 0))
    return qspec, kvspec, vec, sink


def _swa_logits(q2, kw, i, tq):
    rows = q2.shape[0]
    r = lax.broadcasted_iota(jnp.int32, (rows, 2 * tq), 0)
    c = lax.broadcasted_iota(jnp.int32, (rows, 2 * tq), 1)
    diff = (r & (tq - 1)) + tq - c
    ok = (diff >= 0) & (diff < SWA_WINDOW) & (c + (i - 1) * tq >= 0)
    return jnp.where(ok, _dot_nt(q2, kw) * ATTN_SCALE, NEG_INF), ok


def swa_fwd(q, kp, vp, sink, *, name):
    _, G, S, _ = q.shape
    tq = ATT_BLK
    qspec, kvspec, vec, sinkspec = _swa_specs(S, tq)

    def body(q_ref, k_ref, v_ref, s_ref, o_ref, lse_ref):
        i = pl.program_id(1)
        q2 = q_ref[...].reshape(G * tq, HEAD_DIM)
        ws = pl.multiple_of(i * tq, tq)
        logits, _ = _swa_logits(q2, k_ref[pl.ds(ws, 2 * tq), :], i, tq)
        sk = s_ref[...]
        m = jnp.maximum(jnp.max(logits, axis=1, keepdims=True), sk)
        e = jnp.exp(logits - m)
        den = jnp.sum(e, axis=1, keepdims=True) + jnp.exp(sk - m)
        o = _dot((e / den).astype(BF16), v_ref[pl.ds(ws, 2 * tq), :])
        o_ref[...] = o.reshape(G, tq, HEAD_DIM).astype(o_ref.dtype)
        lse_ref[...] = (m + jnp.log(den)).reshape(G, tq, 1)

    return pl.pallas_call(
        body, name=name, grid=(SWA_KV_HEADS, S // tq),
        in_specs=[qspec, kvspec, kvspec, sinkspec], out_specs=[qspec, vec],
        out_shape=[jax.ShapeDtypeStruct(q.shape, BF16), jax.ShapeDtypeStruct((SWA_KV_HEADS, G, S, 1), F32)],
        compiler_params=_cparams(("parallel", "arbitrary")),
    )(q, kp, vp, sink)


def swa_bwd(q, kp, vp, sink, o, lse, do, *, name):
    _, G, S, _ = q.shape
    tq = ATT_BLK
    qspec, kvspec, vec, sinkspec = _swa_specs(S, tq)

    def body(q_ref, k_ref, v_ref, s_ref, o_ref, lse_ref, do_ref, dq_ref, dk_ref, dv_ref, dsink_ref):
        i = pl.program_id(1)

        @pl.when(i == 0)
        def _():
            dk_ref[...] = jnp.zeros_like(dk_ref)
            dv_ref[...] = jnp.zeros_like(dv_ref)

        q2 = q_ref[...].reshape(G * tq, HEAD_DIM)
        do2 = do_ref[...].reshape(G * tq, HEAD_DIM)
        o2 = o_ref[...].reshape(G * tq, HEAD_DIM)
        lse2 = lse_ref[...].reshape(G * tq, 1)
        ws = pl.multiple_of(i * tq, tq)
        kw = k_ref[pl.ds(ws, 2 * tq), :]
        vw = v_ref[pl.ds(ws, 2 * tq), :]
        logits, ok = _swa_logits(q2, kw, i, tq)
        p = jnp.where(ok, jnp.exp(logits - lse2), 0.0)
        delta = jnp.sum(do2.astype(F32) * o2.astype(F32), axis=1, keepdims=True)
        ds = p * (_dot_nt(do2, vw) - delta)
        dsb = ds.astype(BF16)
        dq_ref[...] = (_dot(dsb, kw) * ATTN_SCALE).reshape(G, tq, HEAD_DIM)
        dk_ref[pl.ds(ws, 2 * tq), :] += _dot_tn(dsb, q2) * ATTN_SCALE
        dv_ref[pl.ds(ws, 2 * tq), :] += _dot_tn(p.astype(BF16), do2)
        dsink_ref[...] = (-jnp.exp(s_ref[...] - lse2) * delta).reshape(G, tq, 1)

    kvshape = jax.ShapeDtypeStruct(kp.shape, F32)
    return pl.pallas_call(
        body, name=name, grid=(SWA_KV_HEADS, S // tq),
        in_specs=[qspec, kvspec, kvspec, sinkspec, qspec, vec, qspec],
        out_specs=[qspec, kvspec, kvspec, vec],
        out_shape=[jax.ShapeDtypeStruct(q.shape, F32), kvshape, kvshape,
                   jax.ShapeDtypeStruct((SWA_KV_HEADS, G, S, 1), F32)],
        compiler_params=_cparams(("parallel", "arbitrary")),
    )(q, kp, vp, sink, o, lse, do)


def _adamw_tile(w, g, m, v):
    m = ADAM_B1 * m + (1.0 - ADAM_B1) * g
    v = ADAM_B2 * v + (1.0 - ADAM_B2) * (g * g)
    m_hat = m / (1.0 - ADAM_B1 ** ADAM_STEP)
    v_hat = v / (1.0 - ADAM_B2 ** ADAM_STEP)
    delta = -ADAM_LR * (m_hat / (jnp.sqrt(v_hat) + ADAM_EPS) + ADAM_WD * w)
    return g, delta, m, v


def adamw(gfull, t, w, m, v, after, *, name):
    off, K, ns, _ = t
    sb = off // K
    nat = pl.BlockSpec((K, SLAB), lambda q: (0, q))

    def body(g_ref, w_ref, m_ref, v_ref, after_ref, *outs):
        del after_ref
        for o, r in zip(outs, _adamw_tile(w_ref[...], g_ref[...], m_ref[...], v_ref[...])):
            o[...] = r

    return pl.pallas_call(
        body, name=name, grid=(ns,),
        in_specs=[pl.BlockSpec((K, SLAB), lambda q: (sb + q, 0)), nat, nat, nat, HBM],
        out_specs=[nat] * 4, out_shape=[jax.ShapeDtypeStruct(w.shape, F32)] * 4,
        compiler_params=_cparams(("parallel",)),
    )(gfull, w, m, v, after)


def adamw_small(g, w, m, v, *, name):
    def body(g_ref, w_ref, m_ref, v_ref, *outs):
        for o, r in zip(outs, _adamw_tile(w_ref[...], g_ref[...], m_ref[...], v_ref[...])):
            o[...] = r

    return pl.pallas_call(body, name=name, out_shape=[jax.ShapeDtypeStruct(w.shape, F32)] * 4)(g, w, m, v)


MESH = pl.DeviceIdType.MESH
HBM = pl.BlockSpec(memory_space=pl.ANY)


def _place():
    x, y, c = lax.axis_index("x"), lax.axis_index("y"), lax.axis_index("c")
    others = [(1 - x, y), (x, 1 - y), (1 - x, 1 - y)]
    return x, y, c, others


def _rcopy(src, dst, send_sems, recv_sems, k, to):
    return pltpu.make_async_remote_copy(src_ref=src, dst_ref=dst, send_sem=send_sems.at[k], recv_sem=recv_sems.at[k],
                                        device_id=to, device_id_type=MESH)


def _dma_sems(*counts):
    return [pltpu.SemaphoreType.DMA((n,)) for n in counts]


DMA_UNIT_ROWS = 128
DMA_PIECES = 4
DMA_PIECES_LOCAL = 8


def _row_pieces(h, n):
    units = h // DMA_UNIT_ROWS
    n = min(n, units)
    base, extra = divmod(units, n)
    sizes = [(base + (k < extra)) * DMA_UNIT_ROWS for k in range(n)]
    return [(sum(sizes[:k]), sizes[k]) for k in range(n)]


def _start_pieces(make, h, n):
    for s0, sz in _row_pieces(h, n):
        make(s0, sz).start()
    return make(0, h)


SEM = pl.BlockSpec(memory_space=pltpu.SEMAPHORE)
SPLIT_COPY = pltpu.CompilerParams(has_side_effects=pltpu.SideEffectType.DATAFLOW_SIDE_EFFECTING)
N_OTHERS = 3


def _hbm(a):
    return pltpu.with_memory_space_constraint(a, pltpu.HBM)


def _chip_rows(buf, chip, s0, sz):
    return buf.at[2 * chip[0] + chip[1], pl.ds(s0, sz)]


def allgather_start(bufs, *, name):
    n = len(bufs)

    def body(*refs):
        ins, send, recv, token = refs[:n], refs[n:2 * n], refs[2 * n:3 * n], refs[4 * n]
        x, y, c, others = _place()
        for i in range(n):
            h = bufs[i].shape[1] // 2
            for f, chip in enumerate(others):
                for s0, sz in _row_pieces(h, DMA_PIECES):
                    mine = _chip_rows(ins[i], (x, y), c * h + s0, sz)
                    _rcopy(mine, mine, send[i], recv[i], f, (*chip, c)).start()
        token[...] = jnp.zeros_like(token)

    res = pl.pallas_call(
        body, name=name, in_specs=[HBM] * n,
        out_specs=[SEM] * (2 * n) + [HBM] * n + [pl.BlockSpec(memory_space=pltpu.VMEM)],
        out_shape=[pltpu.SemaphoreType.DMA((N_OTHERS,))] * (2 * n) + [pltpu.HBM(b.shape, b.dtype) for b in bufs]
        + [jax.ShapeDtypeStruct((1, D_MODEL), F32)],
        input_output_aliases={i: 2 * n + i for i in range(n)},
        compiler_params=SPLIT_COPY,
    )(*[_hbm(b) for b in bufs])
    return res[:n], res[n:2 * n], res[2 * n:3 * n], res[3 * n]


def allgather_wait(buf, send, recv, after, *, name):
    h = buf.shape[1] // 2

    def body(buf_ref, send_sems, recv_sems, after_ref, out_ref):
        del after_ref, out_ref
        x, y, c, others = _place()
        for f, chip in enumerate(others):
            mine = _chip_rows(buf_ref, (x, y), c * h, h)
            theirs = _chip_rows(buf_ref, chip, c * h, h)
            cp = _rcopy(mine, theirs, send_sems, recv_sems, f, (*chip, c))
            cp.wait_send()
            cp.wait_recv()

    return pl.pallas_call(
        body, name=name, in_specs=[HBM, SEM, SEM, HBM], out_specs=HBM,
        out_shape=pltpu.HBM(buf.shape, buf.dtype), input_output_aliases={0: 0},
        compiler_params=SPLIT_COPY,
    )(buf, send, recv, after)


def allgather_forward(buf, *, name):
    h = buf.shape[1] // 2

    def body(in_ref, out_ref, send_sems, recv_sems):
        del in_ref
        x, y, c, others = _place()
        sibling = (x, y, 1 - c)
        sent = []
        for f, chip in enumerate(others):
            sent.append(_start_pieces(
                lambda s0, sz: _rcopy(_chip_rows(out_ref, chip, c * h + s0, sz), _chip_rows(out_ref, chip, c * h + s0, sz),
                                      send_sems, recv_sems, f, sibling), h, DMA_PIECES))
        for f, chip in enumerate(others):
            blk = _chip_rows(out_ref, chip, (1 - c) * h, h)
            _rcopy(blk, blk, send_sems, recv_sems, f, sibling).wait_recv()
        for cp in sent:
            cp.wait_send()

    return pl.pallas_call(
        body, name=name, in_specs=[HBM], out_specs=HBM,
        out_shape=jax.ShapeDtypeStruct(buf.shape, buf.dtype), input_output_aliases={0: 0},
        scratch_shapes=_dma_sems(N_OTHERS, N_OTHERS),
    )(buf)


def swap_halves(grads, *, name):
    n = len(grads)

    def body(*refs):
        ins, theirs = refs[:n], refs[n:2 * n]
        send_sems, recv_sems = refs[2 * n:]
        x, y, c, _ = _place()
        for i in range(n):
            h = grads[i].shape[1] // 2
            for k in range(N_CHIPS):
                for s0, sz in _row_pieces(h, DMA_PIECES):
                    _rcopy(ins[i].at[k, pl.ds((1 - c) * h + s0, sz)], theirs[i].at[k, pl.ds(s0, sz)],
                           send_sems, recv_sems, i, (x, y, 1 - c)).start()
        for i in range(n):
            h = grads[i].shape[1] // 2
            _rcopy(ins[i].at[:, pl.ds((1 - c) * h, h)], theirs[i], send_sems, recv_sems, i, (x, y, 1 - c)).wait()

    return pl.pallas_call(
        body, name=name, in_specs=[HBM] * n, out_specs=[HBM] * n,
        out_shape=[jax.ShapeDtypeStruct((N_CHIPS, g.shape[1] // 2, SLAB), g.dtype) for g in grads],
        scratch_shapes=_dma_sems(n, n))(*grads)


def scatter_start(part, *, name):
    h = part.shape[1]

    def body(part_ref, land_ref, send, recv, part_out, land_out, token):
        del part_out, land_out
        x, y, c, others = _place()
        for f, chip in enumerate(others):
            for s0, sz in _row_pieces(h, DMA_PIECES):
                _rcopy(_chip_rows(part_ref, chip, s0, sz), land_ref.at[f, pl.ds(s0, sz)], send, recv, f, (*chip, c)).start()
        token[...] = jnp.zeros_like(token)

    land = lax.empty((N_OTHERS,) + part.shape[1:], part.dtype)
    return pl.pallas_call(
        body, name=name, in_specs=[HBM, HBM],
        out_specs=[SEM, SEM, HBM, HBM, pl.BlockSpec(memory_space=pltpu.VMEM)],
        out_shape=[pltpu.SemaphoreType.DMA((N_OTHERS,))] * 2 + [pltpu.HBM(part.shape, part.dtype), pltpu.HBM(land.shape, land.dtype),
                                                                 jax.ShapeDtypeStruct((1, D_MODEL), F32)],
        input_output_aliases={0: 2, 1: 3},
        compiler_params=SPLIT_COPY,
    )(_hbm(part), _hbm(land))


def scatter_wait(part, land, send, recv, after, *, name):
    h = part.shape[1]

    def body(part_ref, land_ref, send_sems, recv_sems, after_ref, part_out, land_out):
        del after_ref, part_out, land_out
        x, y, c, others = _place()
        for f, chip in enumerate(others):
            cp = _rcopy(_chip_rows(part_ref, chip, 0, h), land_ref.at[f], send_sems, recv_sems, f, (*chip, c))
            cp.wait_send()
            cp.wait_recv()

    return pl.pallas_call(
        body, name=name, in_specs=[HBM, HBM, SEM, SEM, HBM], out_specs=[HBM, HBM],
        out_shape=[pltpu.HBM(part.shape, part.dtype), pltpu.HBM(land.shape, land.dtype)],
        input_output_aliases={0: 0, 1: 1},
        compiler_params=SPLIT_COPY,
    )(part, land, send, recv, after)


def join_halves(bufs, *, name):
    n = len(bufs)

    def body(*refs):
        outs = refs[n:2 * n]
        send_sems, recv_sems = refs[2 * n:]
        x, y, c, _ = _place()
        sibling = (x, y, 1 - c)
        cps = []
        for i in range(n):
            h = bufs[i].shape[0] // 2
            snd = _start_pieces(
                lambda s0, sz: _rcopy(outs[i].at[pl.ds(c * h + s0, sz)], outs[i].at[pl.ds(c * h + s0, sz)],
                                      send_sems, recv_sems, i, sibling), h, 2 * DMA_PIECES_LOCAL)
            theirs = outs[i].at[pl.ds((1 - c) * h, h)]
            cps.append((snd, _rcopy(theirs, theirs, send_sems, recv_sems, i, sibling)))
        for snd, rcv in cps:
            snd.wait_send()
            rcv.wait_recv()

    return pl.pallas_call(
        body, name=name, in_specs=[HBM] * n, out_specs=[HBM] * n,
        out_shape=[jax.ShapeDtypeStruct(b.shape, b.dtype) for b in bufs],
        input_output_aliases={i: i for i in range(n)},
        scratch_shapes=_dma_sems(n, n),
    )(*bufs)


def allreduce_small(v, *, name):
    rows, n = v.shape

    def body(x_ref, sum_ref, all_ref, send_sems, recv_sems, local_sem):
        x, y, c, others = _place()
        me, sibling = (x, y, c), (x, y, 1 - c)

        def blk(px, py, pc):
            return all_ref.at[pl.ds((4 * px + 2 * py + pc) * rows, rows), :]

        def copy(k, block, to, src=None):
            return _rcopy(blk(*block) if src is None else src, blk(*block), send_sems, recv_sems, k, to)

        mine = pltpu.make_async_copy(x_ref, blk(*me), local_sem)
        mine.start()
        first = [copy(0, me, sibling, src=x_ref)]
        first += [copy(1 + f, me, (*chip, c), src=x_ref) for f, chip in enumerate(others)]
        for cp in first:
            cp.start()
        passed = [copy(4 + f, (*chip, c), sibling) for f, chip in enumerate(others)]
        for f, chip in enumerate(others):
            copy(1 + f, (*chip, c), me).wait_recv()
            passed[f].start()
        copy(0, sibling, me).wait_recv()
        for f, chip in enumerate(others):
            copy(4 + f, (*chip, 1 - c), me).wait_recv()
        for cp in first + passed:
            cp.wait_send()
        mine.wait()
        acc = all_ref[pl.ds(0, rows), :]
        for d in range(1, N_DEVICES):
            acc = acc + all_ref[pl.ds(d * rows, rows), :]
        sum_ref[...] = acc

    vm = pl.BlockSpec(memory_space=pltpu.VMEM)
    return pl.pallas_call(
        body, name=name, in_specs=[vm], out_specs=[vm, vm],
        out_shape=[jax.ShapeDtypeStruct((rows, n), F32), jax.ShapeDtypeStruct((N_DEVICES * rows, n), F32)],
        scratch_shapes=_dma_sems(7, 7) + [pltpu.SemaphoreType.DMA],
    )(v)[0]


def add_pairs(grad, theirs, where, *, name):
    h = theirs.shape[1]
    spec = pl.BlockSpec((None, h, SLAB), lambda k, w: (k, 0, 0))

    def body(w_ref, a_ref, b_ref, o_ref):
        del w_ref
        o_ref[...] = (a_ref[...].astype(F32) + b_ref[...].astype(F32)).astype(o_ref.dtype)

    return pl.pallas_call(
        body, name=name,
        grid_spec=pltpu.PrefetchScalarGridSpec(
            num_scalar_prefetch=1, grid=(N_CHIPS,),
            in_specs=[pl.BlockSpec((None, h, SLAB), lambda k, w: (k, w[1], 0)), spec], out_specs=spec),
        out_shape=jax.ShapeDtypeStruct(theirs.shape, theirs.dtype),
        compiler_params=_cparams(("parallel",)))(where, grad, theirs)


def add_chips(pair, got, where, *, name):
    h = pair.shape[1]
    tr = h // 2

    def body(w_ref, a_ref, b_ref, o_ref):
        del w_ref
        acc = a_ref[...].astype(F32)
        for f in range(3):
            acc = acc + b_ref[f].astype(F32)
        o_ref[...] = acc

    return pl.pallas_call(
        body, name=name,
        grid_spec=pltpu.PrefetchScalarGridSpec(
            num_scalar_prefetch=1, grid=(2,),
            in_specs=[pl.BlockSpec((None, tr, SLAB), lambda i, w: (w[0], i, 0)),
                      pl.BlockSpec((3, tr, SLAB), lambda i, w: (0, i, 0))],
            out_specs=pl.BlockSpec((tr, SLAB), lambda i, w: (2 * w[1] + i, 0))),
        out_shape=jax.ShapeDtypeStruct((2 * h, SLAB), F32),
        compiler_params=_cparams(("parallel",)))(where, pair, got)


DEPTH = 4
MIXER = (0, 1, 2, 0)
W_IN_COLS = (768, 320, 772)
W_IN_PAD = (768, 512, 1024)
MATS = ("up", "down", "inp", "out", "gate", "proj")
MAT_ARG = dict(up="w_up", down="w_down", inp="w_in", out="w_out", gate="w_ple_gate", proj="w_ple_proj")
GAINS = ("attn_norm", "mlp_norm", "ple_norm")
N_SMALL = 16
KINDS = ("grad_", "delta_", "new_m_", "new_v_")


def _layout(kind):
    ns_in = W_IN_PAD[kind] // SLAB
    off = 8192 + 1024 * ns_in
    lay = dict(up=(0, 1024, 4, False), down=(4096, 1024, 4, True), inp=(8192, 1024, ns_in, False),
               out=(off, 256, 4, True), gate=(off + 1024, 256, 4, True), proj=(off + 2048, 256, 1, False))
    return lay, off + 2304


def _to_slabs(w):
    k, c = w.shape
    return w.reshape(k, c // SLAB, SLAB).transpose(1, 0, 2).reshape(-1, SLAB)


def _pad_cols(w, n):
    return jnp.pad(w, ((0, 0), (0, n - w.shape[1])))


def _heads(x2d, n):
    return x2d.reshape(x2d.shape[0], n, HEAD_DIM).transpose(1, 0, 2)


def _unheads(x3d):
    n, s, _ = x3d.shape
    return x3d.transpose(1, 0, 2).reshape(s, n * HEAD_DIM)


def _chip_cols(x2d, c, cpad):
    s = x2d.shape[0]
    return jnp.pad(x2d.reshape(s, N_CHIPS, c), ((0, 0), (0, 0), (0, cpad - c))).reshape(s, N_CHIPS * cpad)


def _unchip_cols(x2d, c, cpad):
    s = x2d.shape[0]
    return x2d.reshape(s, N_CHIPS, cpad)[:, :, :c].reshape(s, N_CHIPS * c)


def _forget_cols(wg, t):
    off, K, _, _ = t
    cols = []
    for g in range(3 * N_HEADS * HEAD_DIM, 3 * N_HEADS * HEAD_DIM + N_HEADS):
        chip, local = divmod(g, W_IN_COLS[2])
        q, c = divmod(local, SLAB)
        cols.append(wg[chip, off + q * K:off + (q + 1) * K, c:c + 1])
    return jnp.concatenate(cols, axis=1)


def _add_res(acc, res):
    return (acc + res,)


def _relu2(acc):
    return acc, jnp.square(jnp.maximum(acc, 0.0))


def _relu2_bwd(acc, u):
    return (acc * (2.0 * jnp.maximum(u.astype(F32), 0.0)),)


def _ple_fwd(acc, x2, pp):
    return x2 + pp * _sigmoid(acc), acc


def _ple_bwd(dx, pp, gl):
    gate = _sigmoid(gl)
    return dx * gate, dx * pp * gate * (1.0 - gate)


def _layer_fwd(i, kind, x0, p_bf, wg, lay, gains, extra, tabs):
    s = x0.shape[0]
    an, mn, pn = gains
    sv = dict(x0=x0)
    h1 = rms_fwd(x0, an, name=f"attn_norm_{i}")
    if kind == 0:
        proj = mm_nn(h1, wg, lay["inp"], name=f"w_in_{i}")[0]
        a, tot = sb_fwd(proj, name=f"sb_fwd_{i}")
        sv.update(proj=proj, tot=tot)
    elif kind == 1:
        projp = mm_nn(h1, wg, lay["inp"], name=f"w_in_{i}", out_dtypes=(F32,))[0]
        proj = _unchip_cols(projp, W_IN_COLS[1], W_IN_PAD[1])
        nq = N_HEADS * HEAD_DIM
        nqk = nq + SWA_KV_HEADS * HEAD_DIM
        qk = rope_fwd(proj[:, :nqk], tabs, name=f"rope_{i}")
        q = _heads(qk[:, :nq], N_HEADS).reshape(SWA_KV_HEADS, SWA_GROUP, s, HEAD_DIM)
        front = ((0, 0), (SWA_WINDOW, 0), (0, 0))
        kp = jnp.pad(_heads(qk[:, nq:], SWA_KV_HEADS), front)
        vp = jnp.pad(_heads(proj[:, nqk:].astype(BF16), SWA_KV_HEADS), front)
        sink = jnp.repeat(extra.reshape(SWA_KV_HEADS, SWA_GROUP), ATT_BLK, axis=1)[:, :, None]
        o4, lse = swa_fwd(q, kp, vp, sink, name=f"swa_fwd_{i}")
        a = _unheads(o4.reshape(N_HEADS, s, HEAD_DIM))
        sv.update(q=q, kp=kp, vp=vp, sink=sink, o4=o4, lse=lse)
    else:
        projp = mm_nn(h1, wg, lay["inp"], name=f"w_in_{i}")[0]
        nqkv = 3 * N_HEADS * HEAD_DIM
        proj = _unchip_cols(projp, W_IN_COLS[2], W_IN_PAD[2])[:, :nqkv]
        fl = mm_plain(h1, _pad_cols(_forget_cols(wg, lay["inp"]), 128), name=f"w_forget_{i}")
        bp = _pad_cols(extra[None], 128)
        cum_t = fox_gate_fwd(fl, bp, name=f"gate_fwd_{i}")[:, :N_HEADS].T
        cq = cum_t[:, :, None]
        ck = cum_t.reshape(N_HEADS, s // min(ATT_BK, s), 1, min(ATT_BK, s))
        a, lse = fox_fwd(proj, cq, ck, name=f"fox_fwd_{i}")
        sv.update(proj=proj, fl=fl, bp=bp, cq=cq, ck=ck, lse=lse)
    x1 = mm_nn(a, wg, lay["out"], name=f"w_out_{i}", epi=_add_res, extras=(x0,), out_dtypes=(F32,))[0]
    h2 = rms_fwd(x1, mn, name=f"mlp_norm_{i}")
    u, r = mm_nn(h2, wg, lay["up"], name=f"w_up_{i}", epi=_relu2, out_dtypes=(BF16, BF16))
    x2 = mm_nn(r, wg, lay["down"], name=f"w_down_{i}", epi=_add_res, extras=(x1,), out_dtypes=(F32,))[0]
    h3 = rms_fwd(x2, pn, name=f"ple_norm_{i}")
    pp = mm_nn(p_bf, wg, lay["proj"], name=f"w_ple_proj_{i}", out_dtypes=(F32,))[0]
    x3, gl = mm_nn(h3, wg, lay["gate"], name=f"w_ple_gate_{i}", epi=_ple_fwd, extras=(x2, pp), out_dtypes=(F32, F32))
    sv.update(h1=h1, a=a, x1=x1, h2=h2, u=u, r=r, x2=x2, h3=h3, pp=pp, gl=gl)
    return x3, sv


def _layer_bwd(i, kind, dx3, sv, p_bf, wg, lay, n_rows, gains, tabs):
    s = dx3.shape[0]
    an, mn, pn = gains
    g = lax.empty((N_CHIPS, n_rows, SLAB), BF16)
    d_pp, d_gl = ew(_ple_bwd, [dx3, sv["pp"], sv["gl"]], [BF16, BF16], name=f"ple_bwd_{i}")
    g = mm_tn(p_bf, d_pp, g, lay["proj"], name=f"dw_ple_proj_{i}")
    g = mm_tn(sv["h3"], d_gl, g, lay["gate"], name=f"dw_ple_gate_{i}")
    d_h3 = mm_nt(d_gl, wg, lay["gate"], name=f"dx_ple_gate_{i}", out_dtypes=(F32,))[0]
    dx2, dx2b, d_pn = rms_bwd(sv["x2"], pn, d_h3, dx3, name=f"ple_norm_bwd_{i}")
    g = mm_tn(sv["r"], dx2b, g, lay["down"], name=f"dw_down_{i}")
    d_u = mm_nt(dx2b, wg, lay["down"], name=f"dx_down_{i}", epi=_relu2_bwd, extras=(sv["u"],))[0]
    g = mm_tn(sv["h2"], d_u, g, lay["up"], name=f"dw_up_{i}")
    d_h2 = mm_nt(d_u, wg, lay["up"], name=f"dx_up_{i}", out_dtypes=(F32,))[0]
    dx1, dx1b, d_mn = rms_bwd(sv["x1"], mn, d_h2, dx2, name=f"mlp_norm_bwd_{i}")
    g = mm_tn(sv["a"], dx1b, g, lay["out"], name=f"dw_out_{i}")
    d_a = mm_nt(dx1b, wg, lay["out"], name=f"dx_out_{i}")[0]
    d_extra = None
    if kind == 0:
        d_proj = jnp.concatenate(sb_bwd(sv["proj"], sv["tot"], d_a, name=f"sb_bwd_{i}"), axis=1)
    elif kind == 1:
        do4 = _heads(d_a, N_HEADS).reshape(SWA_KV_HEADS, SWA_GROUP, s, HEAD_DIM)
        dq, dkp, dvp, dsr = swa_bwd(sv["q"], sv["kp"], sv["vp"], sv["sink"], sv["o4"], sv["lse"], do4, name=f"swa_bwd_{i}")
        dqk = jnp.concatenate([_unheads(dq.reshape(N_HEADS, s, HEAD_DIM)), _unheads(dkp[:, SWA_WINDOW:])], axis=1)
        dqk = rope_bwd(dqk, tabs, name=f"rope_bwd_{i}")
        d_proj = jnp.concatenate([dqk, _unheads(dvp[:, SWA_WINDOW:]).astype(BF16)], axis=1)
        d_proj = _chip_cols(d_proj, W_IN_COLS[1], W_IN_PAD[1])
        d_extra = jnp.sum(dsr[..., 0], axis=2).reshape(N_HEADS)
    else:
        dq, dk, dv, dcq, dck = fox_bwd(sv["proj"], sv["a"], sv["lse"], sv["cq"], sv["ck"], d_a, name=f"fox_bwd_{i}")
        dcum = _pad_cols((dcq[:, :, 0] - dck.reshape(N_HEADS, s)).T, 128)
        dfl, dbp = fox_gate_bwd(dcum, sv["fl"], sv["bp"], name=f"gate_bwd_{i}")
        d_proj = jnp.concatenate([dq, dk, dv, dfl[:, :N_HEADS].astype(BF16)], axis=1)
        d_proj = _chip_cols(d_proj, W_IN_COLS[2], W_IN_PAD[2])
        d_extra = dbp[0, :N_HEADS]
    g = mm_tn(sv["h1"], d_proj, g, lay["inp"], name=f"dw_in_{i}")
    d_h1 = mm_nt(d_proj, wg, lay["inp"], name=f"dx_in_{i}", out_dtypes=(F32,))[0]
    dx0, _, d_an = rms_bwd(sv["x0"], an, d_h1, dx1, name=f"attn_norm_bwd_{i}")
    return dx0, g, (d_an, d_mn, d_pn), d_extra


def _small_rows(a, prefix):
    rows = [a[f"{prefix}{n}_{i}"] for i in range(DEPTH) for n in GAINS] + [a[f"{prefix}final_norm"]]
    rows += [_pad_cols(a[f"{prefix}{n}"][None], D_MODEL)[0] for n in ("sinks_1", "b_forget_2")]
    return jnp.stack(rows + [jnp.zeros((D_MODEL,), F32)])


def _train_step(a):
    x = a["x"][0]
    tabs = rope_tables(x.shape[0], (N_HEADS + SWA_KV_HEADS) * HEAD_DIM)
    lays = [_layout(k) for k in MIXER]

    def natural(prefix, i, m):
        w = a[f"{prefix}{MAT_ARG[m]}_{i}"]
        return _pad_cols(w, W_IN_PAD[MIXER[i]]) if m == "inp" else w

    chip = 2 * lax.axis_index("x") + lax.axis_index("y")
    where = jnp.stack([chip, lax.axis_index("c")]).astype(jnp.int32)
    def own_block(i, zero):
        pk = jnp.concatenate([_to_slabs((natural("", i, m) + zero).astype(BF16)) for m in MATS], axis=0)
        return lax.dynamic_update_slice(lax.empty((N_CHIPS,) + pk.shape, BF16), pk[None], (chip, 0, 0))

    sends, recvs, bufs, token = allgather_start([own_block(0, 0.0)], name="allgather_start_0")
    more = allgather_start([own_block(i, token[0, 0]) for i in range(1, DEPTH)], name="allgather_start_1")
    sends, recvs, bufs, token = sends + more[0], recvs + more[1], bufs + more[2], more[3]

    gains = [tuple(a[f"{n}_{i}"][None] for n in GAINS) for i in range(DEPTH)]
    extras = [None, a["sinks_1"], a["b_forget_2"], None]
    p_bf = [a["p"][i, 0].astype(BF16) for i in range(DEPTH)]

    saved, wgs, after = [], [], token
    for i in range(DEPTH):
        landed = allgather_wait(bufs[i], sends[i], recvs[i], after, name=f"allgather_wait_{i}")
        wgs.append(allgather_forward(landed, name=f"allgather_forward_{i}"))
        x, sv = _layer_fwd(i, MIXER[i], x, p_bf[i], wgs[i], lays[i][0], gains[i], extras[i], tabs)
        saved.append(sv)
        after = x
    dx, d_final, loss = loss_head(x, a["final_norm"][None], a["loss_target"][0], name="loss_head")

    def finish(i, started, after):
        send, recv, part, land, _ = started
        part, got = scatter_wait(part, land, send, recv, after, name=f"scatter_wait_{i}")
        return join_halves([add_chips(part, got, where, name=f"add_chips_{i}")], name=f"join_halves_{i}")[0]

    small = [None] * N_SMALL
    small[12] = d_final[0]
    small[15] = _pad_cols(loss[:, :1], D_MODEL)[0]
    gfull = [None] * DEPTH
    started = None
    for i in reversed(range(DEPTH)):
        an, mn, pn = gains[i]
        if started is not None:
            pn = pn + started[4]
        dx, grad, d_gains, d_extra = _layer_bwd(i, MIXER[i], dx, saved[i], p_bf[i], wgs[i], lays[i][0], lays[i][1],
                                                (an, mn, pn), tabs)
        for j in range(3):
            small[3 * i + j] = d_gains[j][0]
        if d_extra is not None:
            small[12 + MIXER[i]] = _pad_cols(d_extra[None], D_MODEL)[0]
        if started is not None:
            gfull[i + 1] = finish(i + 1, started, dx)
        theirs = swap_halves([grad], name=f"swap_halves_{i}")[0]
        started = scatter_start(add_pairs(grad, theirs, where, name=f"add_pairs_{i}"), name=f"scatter_start_{i}")
    small = allreduce_small(jnp.stack(small), name="allreduce_small")

    out = {"loss": small[15, 0], "grad_x": dx[None]}
    res = adamw_small(small, _small_rows(a, ""), _small_rows(a, "m_"), _small_rows(a, "v_"), name="adamw_small")
    for i in reversed(range(DEPTH)):
        if i == 0:
            gfull[0] = finish(0, started, out[f"delta_{MAT_ARG[MATS[-1]]}_1"])
        for m in MATS:
            upd = adamw(gfull[i], lays[i][0][m], natural("", i, m), natural("m_", i, m), natural("v_", i, m), started[4],
                        name=f"adamw_{MAT_ARG[m]}_{i}")
            cols = a[f"{MAT_ARG[m]}_{i}"].shape[1]
            for kd, r in zip(KINDS, upd):
                out[f"{kd}{MAT_ARG[m]}_{i}"] = r[:, :cols]
    for kd, r in zip(KINDS, res):
        for i in range(DEPTH):
            for j, n in enumerate(GAINS):
                out[f"{kd}{n}_{i}"] = r[3 * i + j]
        out[f"{kd}final_norm"] = r[12]
        out[f"{kd}sinks_1"] = r[13, :N_HEADS]
        out[f"{kd}b_forget_2"] = r[14, :N_HEADS]
    return out


def _weight_names():
    names = []
    for i in range(DEPTH):
        names += [f"attn_norm_{i}", f"w_in_{i}", f"w_out_{i}"] + [[], ["sinks_1"], ["b_forget_2"]][MIXER[i]]
        names += [f"mlp_norm_{i}", f"w_up_{i}", f"w_down_{i}", f"ple_norm_{i}", f"w_ple_gate_{i}", f"w_ple_proj_{i}"]
    return names + ["final_norm"]


def kernel(x, p, attn_norm_0, w_in_0, w_out_0, mlp_norm_0, w_up_0, w_down_0, ple_norm_0, w_ple_gate_0, w_ple_proj_0, attn_norm_1, w_in_1, w_out_1, sinks_1, mlp_norm_1, w_up_1, w_down_1, ple_norm_1, w_ple_gate_1, w_ple_proj_1, attn_norm_2, w_in_2, w_out_2, b_forget_2, mlp_norm_2, w_up_2, w_down_2, ple_norm_2, w_ple_gate_2, w_ple_proj_2, attn_norm_3, w_in_3, w_out_3, mlp_norm_3, w_up_3, w_down_3, ple_norm_3, w_ple_gate_3, w_ple_proj_3, final_norm, loss_target, m_attn_norm_0, m_w_in_0, m_w_out_0, m_mlp_norm_0, m_w_up_0, m_w_down_0, m_ple_norm_0, m_w_ple_gate_0, m_w_ple_proj_0, m_attn_norm_1, m_w_in_1, m_w_out_1, m_sinks_1, m_mlp_norm_1, m_w_up_1, m_w_down_1, m_ple_norm_1, m_w_ple_gate_1, m_w_ple_proj_1, m_attn_norm_2, m_w_in_2, m_w_out_2, m_b_forget_2, m_mlp_norm_2, m_w_up_2, m_w_down_2, m_ple_norm_2, m_w_ple_gate_2, m_w_ple_proj_2, m_attn_norm_3, m_w_in_3, m_w_out_3, m_mlp_norm_3, m_w_up_3, m_w_down_3, m_ple_norm_3, m_w_ple_gate_3, m_w_ple_proj_3, m_final_norm, v_attn_norm_0, v_w_in_0, v_w_out_0, v_mlp_norm_0, v_w_up_0, v_w_down_0, v_ple_norm_0, v_w_ple_gate_0, v_w_ple_proj_0, v_attn_norm_1, v_w_in_1, v_w_out_1, v_sinks_1, v_mlp_norm_1, v_w_up_1, v_w_down_1, v_ple_norm_1, v_w_ple_gate_1, v_w_ple_proj_1, v_attn_norm_2, v_w_in_2, v_w_out_2, v_b_forget_2, v_mlp_norm_2, v_w_up_2, v_w_down_2, v_ple_norm_2, v_w_ple_gate_2, v_w_ple_proj_2, v_attn_norm_3, v_w_in_3, v_w_out_3, v_mlp_norm_3, v_w_up_3, v_w_down_3, v_ple_norm_3, v_w_ple_gate_3, v_w_ple_proj_3, v_final_norm):
    out = _train_step(dict(locals()))
    return (out["loss"], out["grad_x"], *[out[kd + n] for kd in KINDS for n in _weight_names()])
```

```python
import jax
import jax.numpy as jnp
from jax import lax
from jax.experimental import pallas as pl
from jax.experimental.pallas import tpu as pltpu

F32 = jnp.float32
BF16 = jnp.bfloat16

D_MODEL = 1024
N_HEADS = 16
HEAD_DIM = 64
SWA_KV_HEADS = 2
SWA_GROUP = 8
SWA_WINDOW = 128
ROPE_THETA = 500000.0
ROPE_DIM = 16
RMS_EPS = 1e-6
NEG_INF = -1e30
ATTN_SCALE = HEAD_DIM ** -0.5
N_CHIPS = 4
N_DEVICES = 8

SLAB = 256
ATT_BLK = 128
ATT_BQ = 512
ATT_BK = 512
ROW_TILE = 256
V7X_VMEM_LIMIT = 56 * 1024 * 1024

ADAM_LR, ADAM_B1, ADAM_B2, ADAM_EPS, ADAM_WD, ADAM_STEP = 0.001, 0.9, 0.999, 1e-08, 0.01, 10


def _cparams(sem=None):
    return pltpu.CompilerParams(dimension_semantics=sem, vmem_limit_bytes=V7X_VMEM_LIMIT)


def _dot(a, b):
    return jnp.dot(a, b, preferred_element_type=F32)


def _dot_nt(a, b):
    return lax.dot_general(a, b, (((1,), (1,)), ((), ())), preferred_element_type=F32)


def _dot_tn(a, b):
    return lax.dot_general(a, b, (((0,), (0,)), ((), ())), preferred_element_type=F32)


def _row_tile(M, K):
    return min(M, 1024) if K >= 1024 else M


def _finish(epi, acc, ex, outs):
    res = epi(acc, *[e[...] for e in ex]) if epi is not None else (acc,)
    for o, r in zip(outs, res):
        o[...] = r.astype(o.dtype)


def _once(shape, index_map):
    return pl.BlockSpec(shape, index_map, pipeline_mode=pl.Buffered(1))


def mm_nn(a, wg, t, *, name, epi=None, extras=(), out_dtypes=(BF16,), norm_gain=None):
    off, K, ns, row = t
    M = a.shape[0]
    sb = off // K
    ne, no = len(extras), len(out_dtypes)
    norm = norm_gain is not None
    if row:
        tm = M if norm else _row_tile(M, K)
        nb = N_CHIPS
        grid = (M // tm, ns)
        a_shape = (tm, N_CHIPS * K)
        a_spec = (_once if norm else pl.BlockSpec)(a_shape, lambda i, q: (i, 0))
        b_specs = [pl.BlockSpec((None, K, SLAB), lambda i, q, j=j: (j, sb + q, 0)) for j in range(nb)]
        tile = pl.BlockSpec((tm, SLAB), lambda i, q: (i, q))
        n_out = ns * SLAB
    else:
        nb = ns
        grid = (N_CHIPS,)
        a_shape = (M, K)
        a_spec = (_once if norm else pl.BlockSpec)(a_shape, lambda j: (0, 0))
        b_specs = [pl.BlockSpec((None, K, SLAB), lambda j, q=q: (j, sb + q, 0)) for q in range(ns)]
        tile = pl.BlockSpec((M, ns * SLAB), lambda j: (0, j))
        n_out = N_CHIPS * ns * SLAB

    def body(a_ref, *rest):
        if norm:
            g_ref, rest, h_out, h_ref = rest[0], rest[1:-2], rest[-2], rest[-1]

            @pl.when(pl.program_id(len(grid) - 1) == 0)
            def _():
                xv = a_ref[...]
                h_ref[...] = (xv * _rstd(xv) * g_ref[...]).astype(BF16)
                h_out[...] = h_ref[...]

            a_ref = h_ref
        bs, ex, outs = rest[:nb], rest[nb:nb + ne], rest[nb + ne:]
        if row:
            acc = _dot(a_ref[:, pl.ds(0, K)], bs[0][...])
            for j in range(1, nb):
                acc = acc + _dot(a_ref[:, pl.ds(j * K, K)], bs[j][...])
            _finish(epi, acc, ex, outs)
        else:
            av = a_ref[...]
            for q in range(ns):
                cols = pl.ds(q * SLAB, SLAB)
                _finish(epi, _dot(av, bs[q][...]), [e.at[:, cols] for e in ex], [o.at[:, cols] for o in outs])

    h_spec = _once(a_shape, (lambda i, q: (i, 0)) if row else (lambda j: (0, 0)))
    return pl.pallas_call(
        body, name=name, grid=grid,
        in_specs=[a_spec] + ([pl.BlockSpec(norm_gain.shape, lambda *_: (0, 0))] if norm else []) + b_specs + [tile] * ne,
        out_specs=[tile] * no + ([h_spec] if norm else []),
        out_shape=[jax.ShapeDtypeStruct((M, n_out), d) for d in out_dtypes]
        + ([jax.ShapeDtypeStruct(a.shape, BF16)] if norm else []),
        scratch_shapes=[pltpu.VMEM(a_shape, BF16)] if norm else [],
        compiler_params=_cparams((("arbitrary" if norm else "parallel"),) * len(grid)),
    )(a, *([norm_gain] if norm else []), *([wg] * nb), *extras)


def mm_nt(dy, wg, t, *, name, epi=None, extras=(), out_dtypes=(BF16,), rms=None):
    off, K, ns, row = t
    M = dy.shape[0]
    tm = _row_tile(M, K)
    sb = off // K
    if rms is not None:
        x, gain, dres = rms
        extras, out_dtypes = (x, dres), (F32, BF16)
    ne, no = len(extras), len(out_dtypes)
    grid = (M // tm, N_CHIPS)
    b_specs = [pl.BlockSpec((None, K, SLAB), lambda i, j, q=q: (j, sb + q, 0)) for q in range(ns)]
    if row:
        dy_spec = pl.BlockSpec((tm, ns * SLAB), lambda i, j: (i, 0))
        tile = pl.BlockSpec((tm, K), lambda i, j: (i, j))
        n_out = N_CHIPS * K
        sem = ("parallel", "parallel")
    else:
        dy_spec = pl.BlockSpec((tm, ns * SLAB), lambda i, j: (i, j))
        tile = pl.BlockSpec((tm, K), lambda i, j: (i, 0))
        n_out = K
        sem = ("arbitrary" if rms is not None else "parallel", "arbitrary")
    one = pl.BlockSpec((1, K), lambda i, j: (0, 0))

    def body(dy_ref, *rest):
        if rms is not None:
            g_ref, rest, dg_ref, acc_ref = rest[0], rest[1:-2], rest[-2], rest[-1]
            rest = rest + (acc_ref,)
        bs, ex, outs = rest[:ns], rest[ns:ns + ne], rest[ns + ne:ns + ne + no]
        part = _dot_nt(dy_ref[:, pl.ds(0, SLAB)], bs[0][...])
        for q in range(1, ns):
            part = part + _dot_nt(dy_ref[:, pl.ds(q * SLAB, SLAB)], bs[q][...])
        if row:
            _finish(epi, part, ex, outs)
        else:
            acc_ref = rest[-1]
            i, j = pl.program_id(0), pl.program_id(1)

            @pl.when(j == 0)
            def _():
                acc_ref[...] = part

            @pl.when(j > 0)
            def _():
                acc_ref[...] += part

            @pl.when(j == N_CHIPS - 1)
            def _():
                if rms is None:
                    _finish(epi, acc_ref[...], ex, outs)
                else:
                    dx, dg = _rms_bwd_tile(ex[0][...], g_ref[...], acc_ref[...])
                    dx = dx + ex[1][...]
                    outs[0][...] = dx
                    outs[1][...] = dx.astype(BF16)

                    @pl.when(i == 0)
                    def _():
                        dg_ref[...] = dg

                    @pl.when(i > 0)
                    def _():
                        dg_ref[...] += dg

    has = rms is not None
    return pl.pallas_call(
        body, name=name, grid=grid,
        in_specs=[dy_spec] + ([one] if has else []) + b_specs + [tile] * ne,
        out_specs=[tile] * no + ([one] if has else []),
        out_shape=[jax.ShapeDtypeStruct((M, n_out), d) for d in out_dtypes] + ([jax.ShapeDtypeStruct((1, K), F32)] if has else []),
        scratch_shapes=[] if row else [pltpu.VMEM((tm, K), F32)],
        compiler_params=_cparams(sem),
    )(dy, *([gain] if has else []), *([wg] * ns), *extras)


def mm_plain(a, b, *, name):
    M, K = a.shape
    N = b.shape[1]
    tm = min(M, 512)

    def body(a_ref, b_ref, o_ref):
        o_ref[...] = _dot(a_ref[...], b_ref[...])

    return pl.pallas_call(
        body, name=name, grid=(M // tm,),
        in_specs=[pl.BlockSpec((tm, K), lambda i: (i, 0)), pl.BlockSpec((K, N), lambda i: (0, 0))],
        out_specs=pl.BlockSpec((tm, N), lambda i: (i, 0)), out_shape=jax.ShapeDtypeStruct((M, N), F32),
        compiler_params=_cparams(("parallel",)),
    )(a, b)


def mm_tn(x, dy, g, t, *, name):
    off, K, ns, row = t
    S = x.shape[0]
    per = ns if off % (ns * K) == 0 else 1
    grid = (N_CHIPS, ns // per)
    if row:
        x_map = lambda j, q: (0, j)
        dy_map = lambda j, q: (0, q)
    else:
        x_map = lambda j, q: (0, 0)
        dy_map = lambda j, q: (0, j * (ns // per) + q)

    def body(g_in, x_ref, dy_ref, o_ref):
        del g_in
        xt = x_ref[...].T
        for q in range(per):
            o_ref[pl.ds(q * K, K), :] = _dot(xt, dy_ref[:, pl.ds(q * SLAB, SLAB)]).astype(o_ref.dtype)

    return pl.pallas_call(
        body, name=name, grid=grid,
        in_specs=[pl.BlockSpec(memory_space=pl.ANY), pl.BlockSpec((S, K), x_map), pl.BlockSpec((S, per * SLAB), dy_map)],
        out_specs=pl.BlockSpec((None, per * K, SLAB), lambda j, q: (j, off // (per * K) + q, 0)),
        out_shape=jax.ShapeDtypeStruct(g.shape, g.dtype),
        input_output_aliases={0: 0},
        compiler_params=_cparams(("parallel", "parallel")),
    )(g, x, dy)


def ew(fn, ins, out_dtypes, *, name, bcast=()):
    S = ins[0].shape[0]
    tr = min(ROW_TILE, S)
    cols = ins[0].shape[1]
    ni, nb = len(ins), len(bcast)

    def body(*refs):
        res = fn(*[r[...] for r in refs[:ni + nb]])
        for o, r in zip(refs[ni + nb:], res):
            o[...] = r.astype(o.dtype)

    return pl.pallas_call(
        body, name=name, grid=(S // tr,),
        in_specs=[pl.BlockSpec((tr, a.shape[1]), lambda i: (i, 0)) for a in ins]
        + [pl.BlockSpec(b.shape, lambda i: (0, 0)) for b in bcast],
        out_specs=[pl.BlockSpec((tr, cols), lambda i: (i, 0)) for _ in out_dtypes],
        out_shape=[jax.ShapeDtypeStruct((S, cols), d) for d in out_dtypes],
        compiler_params=_cparams(("parallel",)),
    )(*ins, *bcast)


def _rstd(x):
    return lax.rsqrt(jnp.mean(x * x, axis=-1, keepdims=True) + RMS_EPS)


def _sigmoid(x):
    return 1.0 / (1.0 + jnp.exp(-x))


def _log_sigmoid(z):
    return jnp.minimum(z, 0.0) - jnp.log(1.0 + jnp.exp(-jnp.abs(z)))


def _rms_bwd_tile(xv, gv, dh):
    rstd = _rstd(xv)
    xhat = xv * rstd
    gd = dh * gv
    dx = rstd * (gd - xhat * jnp.mean(xhat * gd, axis=-1, keepdims=True))
    return dx, jnp.sum(dh * xhat, axis=0, keepdims=True)


def rms_bwd(x, g, dh, dres, *, name):
    S, D = x.shape
    tr = min(ROW_TILE, S)

    def body(x_ref, g_ref, dh_ref, dres_ref, dx_ref, dxb_ref, dg_ref):
        i = pl.program_id(0)
        dx, dg = _rms_bwd_tile(x_ref[...], g_ref[...], dh_ref[...])
        dx = dx + dres_ref[...]
        dx_ref[...] = dx
        dxb_ref[...] = dx.astype(BF16)

        @pl.when(i == 0)
        def _():
            dg_ref[...] = dg

        @pl.when(i > 0)
        def _():
            dg_ref[...] += dg

    row = pl.BlockSpec((tr, D), lambda i: (i, 0))
    one = pl.BlockSpec((1, D), lambda i: (0, 0))
    return pl.pallas_call(
        body, name=name, grid=(S // tr,),
        in_specs=[row, one, row, row], out_specs=[row, row, one],
        out_shape=[jax.ShapeDtypeStruct((S, D), F32), jax.ShapeDtypeStruct((S, D), BF16),
                   jax.ShapeDtypeStruct((1, D), F32)],
        compiler_params=_cparams(("arbitrary",)),
    )(x, g, dh, dres)


def loss_head(x, g, target, *, name):
    S, D = x.shape
    tr = min(ROW_TILE, S)

    def body(x_ref, g_ref, t_ref, dx_ref, dg_ref, loss_ref):
        i = pl.program_id(0)
        xv, gv = x_ref[...], g_ref[...]
        err = xv * _rstd(xv) * gv - t_ref[...]
        part = 0.5 * jnp.sum(jnp.mean(err * err, axis=-1, keepdims=True), axis=0, keepdims=True)
        dx, dg = _rms_bwd_tile(xv, gv, err * (1.0 / D))
        dx_ref[...] = dx
        part = jnp.broadcast_to(part, loss_ref.shape)

        @pl.when(i == 0)
        def _():
            dg_ref[...] = dg
            loss_ref[...] = part

        @pl.when(i > 0)
        def _():
            dg_ref[...] += dg
            loss_ref[...] += part

    row = pl.BlockSpec((tr, D), lambda i: (i, 0))
    one = pl.BlockSpec((1, D), lambda i: (0, 0))
    return pl.pallas_call(
        body, name=name, grid=(S // tr,),
        in_specs=[row, one, row], out_specs=[row, one, pl.BlockSpec((1, 128), lambda i: (0, 0))],
        out_shape=[jax.ShapeDtypeStruct((S, D), F32), jax.ShapeDtypeStruct((1, D), F32),
                   jax.ShapeDtypeStruct((1, 128), F32)],
        compiler_params=_cparams(("arbitrary",)),
    )(x, g, target)


def rope_tables(S, n_cols):
    half = ROPE_DIM // 2
    inv_freq = ROPE_THETA ** (-jnp.arange(half, dtype=F32) / half)
    ang = jnp.arange(S, dtype=F32)[:, None] * inv_freq[None, :]
    cos, sin = jnp.cos(ang), jnp.sin(ang)
    z = jnp.zeros((S, HEAD_DIM - ROPE_DIM), F32)
    zh = jnp.zeros((S, half), F32)
    c = jnp.concatenate([cos, cos, jnp.ones_like(z)], axis=1)
    sa = jnp.concatenate([zh, sin, z], axis=1)
    sb = jnp.concatenate([-sin, zh, z], axis=1)
    return [jnp.tile(t, (1, n_cols // HEAD_DIM)) for t in (c, sa, sb)]


def rope_fwd(xqk, tables, *, name):
    n, half = xqk.shape[1], ROPE_DIM // 2

    def fn(x, c, sa, sb):
        return (x * c + pltpu.roll(x, half, 1) * sa + pltpu.roll(x, n - half, 1) * sb,)

    return ew(fn, [xqk] + list(tables), [BF16], name=name)[0]


def rope_bwd(dy, tables, *, name):
    n, half = dy.shape[1], ROPE_DIM // 2

    def fn(d, c, sa, sb):
        return (d * c + pltpu.roll(d * sa, n - half, 1) + pltpu.roll(d * sb, half, 1),)

    return ew(fn, [dy] + list(tables), [BF16], name=name)[0]


def _split3(x):
    h1 = x.astype(BF16)
    r1 = x - h1.astype(F32)
    h2 = r1.astype(BF16)
    return h1, h2, (r1 - h2.astype(F32)).astype(BF16)


def _tri(n, cmp):
    r = lax.broadcasted_iota(jnp.int32, (n, n), 0)
    c = lax.broadcasted_iota(jnp.int32, (n, n), 1)
    return cmp(r, c).astype(BF16)


def fox_gate_fwd(fl, b, *, name):
    S, W = fl.shape
    tr = min(ROW_TILE, S)

    def body(fl_ref, b_ref, cum_ref, carry):
        i = pl.program_id(0)

        @pl.when(i == 0)
        def _():
            carry[...] = jnp.zeros_like(carry)

        lower = _tri(tr, lambda r, c: r >= c)
        cs = carry[...]
        for piece in _split3(_log_sigmoid(fl_ref[...] + b_ref[...])):
            cs = cs + _dot(lower, piece)
        cum_ref[...] = cs
        carry[...] = cs[tr - 1:tr, :]

    return pl.pallas_call(
        body, name=name, grid=(S // tr,),
        in_specs=[pl.BlockSpec((tr, W), lambda i: (i, 0)), pl.BlockSpec((1, W), lambda i: (0, 0))],
        out_specs=pl.BlockSpec((tr, W), lambda i: (i, 0)),
        out_shape=jax.ShapeDtypeStruct((S, W), F32),
        scratch_shapes=[pltpu.VMEM((1, W), F32)],
        compiler_params=_cparams(("arbitrary",)),
    )(fl, b)


def fox_gate_bwd(dcum, fl, b, *, name):
    S, W = fl.shape
    tr = min(ROW_TILE, S)
    nb = S // tr

    def body(dc_ref, fl_ref, b_ref, dfl_ref, db_ref, carry):
        i = pl.program_id(0)

        @pl.when(i == 0)
        def _():
            carry[...] = jnp.zeros_like(carry)

        upper = _tri(tr, lambda r, c: r <= c)
        cs = carry[...]
        for piece in _split3(dc_ref[...]):
            cs = cs + _dot(upper, piece)
        carry[...] = cs[0:1, :]
        dfl = cs * _sigmoid(-(fl_ref[...] + b_ref[...]))
        dfl_ref[...] = dfl
        db = jnp.sum(dfl, axis=0, keepdims=True)

        @pl.when(i == 0)
        def _():
            db_ref[...] = db

        @pl.when(i > 0)
        def _():
            db_ref[...] += db

    rev = pl.BlockSpec((tr, W), lambda i: (nb - 1 - i, 0))
    one = pl.BlockSpec((1, W), lambda i: (0, 0))
    return pl.pallas_call(
        body, name=name, grid=(nb,),
        in_specs=[rev, rev, one], out_specs=[rev, one],
        out_shape=[jax.ShapeDtypeStruct((S, W), F32), jax.ShapeDtypeStruct((1, W), F32)],
        scratch_shapes=[pltpu.VMEM((1, W), F32)],
        compiler_params=_cparams(("arbitrary",)),
    )(dcum, fl, b)


def _blk_iota(tq, tk):
    return (lax.broadcasted_iota(jnp.int32, (tq, tk), 0), lax.broadcasted_iota(jnp.int32, (tq, tk), 1))


def _cs(xb, tri):
    return _dot(xb, tri)


def _rowsum(xb):
    return jnp.sum(xb.astype(F32), axis=1, keepdims=True)


def _sb_block(qs, k, cmr, shift):
    z = _dot_nt(qs, k)
    strict = cmr < shift
    lb = jnp.minimum(z, 0.0) - jnp.log(1.0 + jnp.exp(-jnp.abs(z)))
    lom = jnp.where(strict, lb - z, 0.0).astype(BF16)
    return lb, lom, strict


def _att_tiles(S):
    return min(ATT_BQ, S), min(ATT_BK, S)


PAIR = 2 * HEAD_DIM
N_PAIRS = N_HEADS // 2


def _pair_specs(S, tq):
    cols = D_MODEL // PAIR
    qspec = pl.BlockSpec((tq, PAIR), lambda p, i: (i, p))
    kspec = pl.BlockSpec((S, PAIR), lambda p, i: (0, cols + p))
    vspec = pl.BlockSpec((S, PAIR), lambda p, i: (0, 2 * cols + p))
    kvout = pl.BlockSpec((S, PAIR), lambda p, i: (0, p))
    vec = pl.BlockSpec((2, tq, 1), lambda p, i: (p, i, 0))
    return qspec, kspec, vspec, kvout, vec


def _head_lanes(h):
    lane = lax.broadcasted_iota(jnp.int32, (1, PAIR), 1)
    return (lane >= h * HEAD_DIM) & (lane < (h + 1) * HEAD_DIM)


def _only(sel, x):
    return jnp.where(sel, x, jnp.zeros_like(x))


def sb_fwd(proj, *, name):
    S = proj.shape[0]
    tq, tk = _att_tiles(S)
    qspec, kspec, vspec, _, vec = _pair_specs(S, tq)

    def body(q_ref, k_ref, v_ref, o_ref, t_ref):
        i = pl.program_id(1)
        row, col = _blk_iota(tq, tk)
        cmr = col - row
        below = _tri(tk, lambda r, c: r > c)
        nkb = (i + 1) * (tq // tk)
        out = []
        for h in range(2):
            sel = _head_lanes(h)
            qs = _only(sel, q_ref[...] * ATTN_SCALE)

            def step(n, carry):
                r_sum, acc = carry
                kb = nkb - 1 - n
                ks = pl.multiple_of(kb * tk, tk)
                lb, lom, strict = _sb_block(qs, k_ref[pl.ds(ks, tk), :], cmr, i * tq - kb * tk)
                w = jnp.where(strict, jnp.exp(lb + _cs(lom, below) + r_sum), 0.0)
                acc = acc + _dot(w.astype(BF16), _only(sel, v_ref[pl.ds(ks, tk), :]))
                return r_sum + _rowsum(lom), acc

            r_sum, acc = lax.fori_loop(0, nkb, step, (jnp.zeros((tq, 1), F32), jnp.zeros((tq, PAIR), F32)))
            t_ref[h] = r_sum
            out.append(acc)
        o_ref[...] = (out[0] + out[1]).astype(o_ref.dtype)

    return pl.pallas_call(
        body, name=name, grid=(N_PAIRS, S // tq),
        in_specs=[qspec, kspec, vspec], out_specs=[qspec, vec],
        out_shape=[jax.ShapeDtypeStruct((S, D_MODEL), BF16), jax.ShapeDtypeStruct((N_HEADS, S, 1), F32)],
        compiler_params=_cparams(("parallel", "arbitrary")),
    )(proj, proj, proj)


def sb_bwd(proj, tot, do, *, name):
    S = proj.shape[0]
    tq, tk = _att_tiles(S)
    qspec, kspec, vspec, kvout, vec = _pair_specs(S, tq)

    def body(q_ref, k_ref, v_ref, t_ref, do_ref, dq_ref, dk_out, dv_out, dk_ref, dv_ref):
        i = pl.program_id(1)

        @pl.when(i == 0)
        def _():
            dk_ref[...] = jnp.zeros_like(dk_ref)
            dv_ref[...] = jnp.zeros_like(dv_ref)

        row, col = _blk_iota(tq, tk)
        cmr = col - row
        upto = _tri(tk, lambda r, c: r <= c)
        before = _tri(tk, lambda r, c: r < c)
        out = []
        for h in range(2):
            sel = _head_lanes(h)
            qs, dov, t_all = _only(sel, q_ref[...] * ATTN_SCALE), _only(sel, do_ref[...]), t_ref[h]

            def step(kb, carry):
                p_sum, e_sum, dq = carry
                ks = pl.multiple_of(kb * tk, tk)
                kv = k_ref[pl.ds(ks, tk), :]
                lb, lom, strict = _sb_block(qs, kv, cmr, i * tq - kb * tk)
                tail = t_all - p_sum - _cs(lom, upto)
                w = jnp.where(strict, jnp.exp(lb + tail), 0.0)
                e = _dot_nt(dov, v_ref[pl.ds(ks, tk), :]) * w
                eb = e.astype(BF16)
                e_before = e_sum + _cs(eb, before)
                beta = jnp.exp(lb)
                dzb = jnp.where(strict, e - (e + e_before) * beta, 0.0).astype(BF16)
                dk_ref[pl.ds(ks, tk), :] += _dot_tn(dzb, qs)
                dv_ref[pl.ds(ks, tk), :] += _dot_tn(w.astype(BF16), dov)
                return p_sum + _rowsum(lom), e_sum + _rowsum(eb), dq + _dot(dzb, _only(sel, kv))

            zero = jnp.zeros((tq, 1), F32)
            out.append(lax.fori_loop(0, (i + 1) * (tq // tk), step, (zero, zero, jnp.zeros((tq, PAIR), F32)))[2])
        dq_ref[...] = ((out[0] + out[1]) * ATTN_SCALE).astype(dq_ref.dtype)

        @pl.when(i == S // tq - 1)
        def _():
            dk_out[...] = dk_ref[...].astype(dk_out.dtype)
            dv_out[...] = dv_ref[...].astype(dv_out.dtype)

    full = jax.ShapeDtypeStruct((S, D_MODEL), BF16)
    return pl.pallas_call(
        body, name=name, grid=(N_PAIRS, S // tq),
        in_specs=[qspec, kspec, vspec, vec, qspec], out_specs=[qspec, kvout, kvout],
        out_shape=[full, full, full],
        scratch_shapes=[pltpu.VMEM((S, PAIR), F32)] * 2,
        compiler_params=_cparams(("parallel", "arbitrary")),
    )(proj, proj, proj, tot, do)


def _fox_logits(qs, k, cq, ck, cmr, shift):
    causal = cmr <= shift
    return jnp.where(causal, _dot_nt(qs, k) + cq - ck, NEG_INF), causal


def fox_fwd(proj, cq, ck, *, name):
    S = proj.shape[0]
    tq, tk = _att_tiles(S)
    qspec, kspec, vspec, _, vec = _pair_specs(S, tq)
    ckspec = pl.BlockSpec((2, S // tk, 1, tk), lambda p, i: (p, 0, 0, 0))

    def body(q_ref, k_ref, v_ref, cq_ref, ck_ref, o_ref, lse_ref):
        i = pl.program_id(1)
        row, col = _blk_iota(tq, tk)
        cmr = col - row
        out = []
        for h in range(2):
            sel = _head_lanes(h)
            qs, cqv = _only(sel, q_ref[...] * ATTN_SCALE), cq_ref[h]

            def step(kb, carry):
                m, l, acc = carry
                ks = pl.multiple_of(kb * tk, tk)
                s, _ = _fox_logits(qs, k_ref[pl.ds(ks, tk), :], cqv, ck_ref[h, kb], cmr, i * tq - kb * tk)
                m_new = jnp.maximum(m, jnp.max(s, axis=1, keepdims=True))
                alpha = jnp.exp(m - m_new)
                p = jnp.exp(s - m_new)
                l = alpha * l + jnp.sum(p, axis=1, keepdims=True)
                acc = alpha * acc + _dot(p.astype(BF16), _only(sel, v_ref[pl.ds(ks, tk), :]))
                return m_new, l, acc

            m, l, acc = lax.fori_loop(0, (i + 1) * (tq // tk), step,
                                      (jnp.full((tq, 1), NEG_INF, F32), jnp.zeros((tq, 1), F32), jnp.zeros((tq, PAIR), F32)))
            lse_ref[h] = m + jnp.log(l)
            out.append(acc / l)
        o_ref[...] = (out[0] + out[1]).astype(o_ref.dtype)

    return pl.pallas_call(
        body, name=name, grid=(N_PAIRS, S // tq),
        in_specs=[qspec, kspec, vspec, vec, ckspec], out_specs=[qspec, vec],
        out_shape=[jax.ShapeDtypeStruct((S, D_MODEL), BF16), jax.ShapeDtypeStruct((N_HEADS, S, 1), F32)],
        compiler_params=_cparams(("parallel", "arbitrary")),
    )(proj, proj, proj, cq, ck)


def fox_bwd(proj, o, lse, cq, ck, do, *, name):
    S = proj.shape[0]
    tq, tk = _att_tiles(S)
    qspec, kspec, vspec, kvout, vec = _pair_specs(S, tq)
    ckspec = pl.BlockSpec((2, S // tk, 1, tk), lambda p, i: (p, 0, 0, 0))

    def body(q_ref, k_ref, v_ref, o_ref, lse_ref, cq_ref, ck_ref, do_ref, dq_ref, dk_out, dv_out, dcq_ref, dck_ref,
             dk_ref, dv_ref):
        i = pl.program_id(1)

        @pl.when(i == 0)
        def _():
            dk_ref[...] = jnp.zeros_like(dk_ref)
            dv_ref[...] = jnp.zeros_like(dv_ref)
            dck_ref[...] = jnp.zeros_like(dck_ref)

        row, col = _blk_iota(tq, tk)
        cmr = col - row
        out = []
        for h in range(2):
            sel = _head_lanes(h)
            qs, dov, cqv, lsev = _only(sel, q_ref[...] * ATTN_SCALE), _only(sel, do_ref[...]), cq_ref[h], lse_ref[h]
            delta = jnp.sum(dov.astype(F32) * o_ref[...].astype(F32), axis=1, keepdims=True)

            def step(kb, carry):
                dq, dcq = carry
                ks = pl.multiple_of(kb * tk, tk)
                kv = k_ref[pl.ds(ks, tk), :]
                s, causal = _fox_logits(qs, kv, cqv, ck_ref[h, kb], cmr, i * tq - kb * tk)
                p = jnp.where(causal, jnp.exp(s - lsev), 0.0)
                ds = p * (_dot_nt(dov, v_ref[pl.ds(ks, tk), :]) - delta)
                dck_ref[h, kb] += jnp.sum(ds, axis=0, keepdims=True)
                dsb = ds.astype(BF16)
                dk_ref[pl.ds(ks, tk), :] += _dot_tn(dsb, qs)
                dv_ref[pl.ds(ks, tk), :] += _dot_tn(p.astype(BF16), dov)
                return dq + _dot(dsb, _only(sel, kv)), dcq + jnp.sum(ds, axis=1, keepdims=True)

            dq, dcq = lax.fori_loop(0, (i + 1) * (tq // tk), step, (jnp.zeros((tq, PAIR), F32), jnp.zeros((tq, 1), F32)))
            dcq_ref[h] = dcq
            out.append(dq)
        dq_ref[...] = ((out[0] + out[1]) * ATTN_SCALE).astype(dq_ref.dtype)

        @pl.when(i == S // tq - 1)
        def _():
            dk_out[...] = dk_ref[...].astype(dk_out.dtype)
            dv_out[...] = dv_ref[...].astype(dv_out.dtype)

    full = jax.ShapeDtypeStruct((S, D_MODEL), BF16)
    return pl.pallas_call(
        body, name=name, grid=(N_PAIRS, S // tq),
        in_specs=[qspec, kspec, vspec, qspec, vec, vec, ckspec, qspec],
        out_specs=[qspec, kvout, kvout, vec, ckspec],
        out_shape=[full, full, full, jax.ShapeDtypeStruct((N_HEADS, S, 1), F32),
                   jax.ShapeDtypeStruct((N_HEADS, S // tk, 1, tk), F32)],
        scratch_shapes=[pltpu.VMEM((S, PAIR), F32)] * 2,
        compiler_params=_cparams(("parallel", "arbitrary")),
    )(proj, proj, proj, o, lse, cq, ck, do)


def _swa_specs(S, tq):
    qspec = pl.BlockSpec((None, SWA_GROUP, tq, HEAD_DIM), lambda g, i: (g, 0, i, 0))
    kvspec = pl.BlockSpec((None, S + SWA_WINDOW, HEAD_DIM), lambda g, i: (g, 0, 0))
    vec = pl.BlockSpec((None, SWA_GROUP, tq, 1), lambda g, i: (g, 0, i, 0))
    sink = pl.BlockSpec((None, SWA_GROUP * tq, 1), lambda g, i: (g, 0, 0))
    return qspec, kvspec, vec, sink


def _swa_logits(q2, kw, i, tq):
    rows = q2.shape[0]
    r = lax.broadcasted_iota(jnp.int32, (rows, 2 * tq), 0)
    c = lax.broadcasted_iota(jnp.int32, (rows, 2 * tq), 1)
    diff = (r & (tq - 1)) + tq - c
    ok = (diff >= 0) & (diff < SWA_WINDOW) & (c + (i - 1) * tq >= 0)
    return jnp.where(ok, _dot_nt(q2, kw) * ATTN_SCALE, NEG_INF), ok


def swa_fwd(q, kp, vp, sink, *, name):
    _, G, S, _ = q.shape
    tq = ATT_BLK
    qspec, kvspec, vec, sinkspec = _swa_specs(S, tq)

    def body(q_ref, k_ref, v_ref, s_ref, o_ref, lse_ref):
        i = pl.program_id(1)
        q2 = q_ref[...].reshape(G * tq, HEAD_DIM)
        ws = pl.multiple_of(i * tq, tq)
        logits, _ = _swa_logits(q2, k_ref[pl.ds(ws, 2 * tq), :], i, tq)
        sk = s_ref[...]
        m = jnp.maximum(jnp.max(logits, axis=1, keepdims=True), sk)
        e = jnp.exp(logits - m)
        den = jnp.sum(e, axis=1, keepdims=True) + jnp.exp(sk - m)
        o = _dot((e / den).astype(BF16), v_ref[pl.ds(ws, 2 * tq), :])
        o_ref[...] = o.reshape(G, tq, HEAD_DIM).astype(o_ref.dtype)
        lse_ref[...] = (m + jnp.log(den)).reshape(G, tq, 1)

    return pl.pallas_call(
        body, name=name, grid=(SWA_KV_HEADS, S // tq),
        in_specs=[qspec, kvspec, kvspec, sinkspec], out_specs=[qspec, vec],
        out_shape=[jax.ShapeDtypeStruct(q.shape, BF16), jax.ShapeDtypeStruct((SWA_KV_HEADS, G, S, 1), F32)],
        compiler_params=_cparams(("parallel", "arbitrary")),
    )(q, kp, vp, sink)


def swa_bwd(q, kp, vp, sink, o, lse, do, *, name):
    _, G, S, _ = q.shape
    tq = ATT_BLK
    qspec, kvspec, vec, sinkspec = _swa_specs(S, tq)

    def body(q_ref, k_ref, v_ref, s_ref, o_ref, lse_ref, do_ref, dq_ref, dk_ref, dv_ref, dsink_ref):
        i = pl.program_id(1)

        @pl.when(i == 0)
        def _():
            dk_ref[...] = jnp.zeros_like(dk_ref)
            dv_ref[...] = jnp.zeros_like(dv_ref)

        q2 = q_ref[...].reshape(G * tq, HEAD_DIM)
        do2 = do_ref[...].reshape(G * tq, HEAD_DIM)
        o2 = o_ref[...].reshape(G * tq, HEAD_DIM)
        lse2 = lse_ref[...].reshape(G * tq, 1)
        ws = pl.multiple_of(i * tq, tq)
        kw = k_ref[pl.ds(ws, 2 * tq), :]
        vw = v_ref[pl.ds(ws, 2 * tq), :]
        logits, ok = _swa_logits(q2, kw, i, tq)
        p = jnp.where(ok, jnp.exp(logits - lse2), 0.0)
        delta = jnp.sum(do2.astype(F32) * o2.astype(F32), axis=1, keepdims=True)
        ds = p * (_dot_nt(do2, vw) - delta)
        dsb = ds.astype(BF16)
        dq_ref[...] = (_dot(dsb, kw) * ATTN_SCALE).reshape(G, tq, HEAD_DIM)
        dk_ref[pl.ds(ws, 2 * tq), :] += _dot_tn(dsb, q2) * ATTN_SCALE
        dv_ref[pl.ds(ws, 2 * tq), :] += _dot_tn(p.astype(BF16), do2)
        dsink_ref[...] = (-jnp.exp(s_ref[...] - lse2) * delta).reshape(G, tq, 1)

    kvshape = jax.ShapeDtypeStruct(kp.shape, F32)
    return pl.pallas_call(
        body, name=name, grid=(SWA_KV_HEADS, S // tq),
        in_specs=[qspec, kvspec, kvspec, sinkspec, qspec, vec, qspec],
        out_specs=[qspec, kvspec, kvspec, vec],
        out_shape=[jax.ShapeDtypeStruct(q.shape, F32), kvshape, kvshape,
                   jax.ShapeDtypeStruct((SWA_KV_HEADS, G, S, 1), F32)],
        compiler_params=_cparams(("parallel", "arbitrary")),
    )(q, kp, vp, sink, o, lse, do)


def _adamw_tile(w, g, m, v):
    m = ADAM_B1 * m + (1.0 - ADAM_B1) * g
    v = ADAM_B2 * v + (1.0 - ADAM_B2) * (g * g)
    m_hat = m / (1.0 - ADAM_B1 ** ADAM_STEP)
    v_hat = v / (1.0 - ADAM_B2 ** ADAM_STEP)
    delta = -ADAM_LR * (m_hat / (jnp.sqrt(v_hat) + ADAM_EPS) + ADAM_WD * w)
    return g, delta, m, v


def adamw(gfull, t, w, m, v, after, *, name):
    off, K, ns, _ = t
    sb = off // K
    nat = pl.BlockSpec((K, SLAB), lambda q: (0, q))

    def body(g_ref, w_ref, m_ref, v_ref, after_ref, *outs):
        del after_ref
        for o, r in zip(outs, _adamw_tile(w_ref[...], g_ref[...], m_ref[...], v_ref[...])):
            o[...] = r

    return pl.pallas_call(
        body, name=name, grid=(ns,),
        in_specs=[pl.BlockSpec((K, SLAB), lambda q: (sb + q, 0)), nat, nat, nat, HBM],
        out_specs=[nat] * 4, out_shape=[jax.ShapeDtypeStruct(w.shape, F32)] * 4,
        compiler_params=_cparams(("parallel",)),
    )(gfull, w, m, v, after)


def adamw_small(g, w, m, v, *, name):
    def body(g_ref, w_ref, m_ref, v_ref, *outs):
        for o, r in zip(outs, _adamw_tile(w_ref[...], g_ref[...], m_ref[...], v_ref[...])):
            o[...] = r

    return pl.pallas_call(body, name=name, out_shape=[jax.ShapeDtypeStruct(w.shape, F32)] * 4)(g, w, m, v)


MESH = pl.DeviceIdType.MESH
HBM = pl.BlockSpec(memory_space=pl.ANY)


def _place():
    x, y, c = lax.axis_index("x"), lax.axis_index("y"), lax.axis_index("c")
    others = [(1 - x, y), (x, 1 - y), (1 - x, 1 - y)]
    return x, y, c, others


def _rcopy(src, dst, send_sems, recv_sems, k, to):
    return pltpu.make_async_remote_copy(src_ref=src, dst_ref=dst, send_sem=send_sems.at[k], recv_sem=recv_sems.at[k],
                                        device_id=to, device_id_type=MESH)


def _dma_sems(*counts):
    return [pltpu.SemaphoreType.DMA((n,)) for n in counts]


DMA_UNIT_ROWS = 128
DMA_PIECES = 4
DMA_PIECES_LOCAL = 8


def _row_pieces(h, n):
    units = h // DMA_UNIT_ROWS
    n = min(n, units)
    base, extra = divmod(units, n)
    sizes = [(base + (k < extra)) * DMA_UNIT_ROWS for k in range(n)]
    return [(sum(sizes[:k]), sizes[k]) for k in range(n)]


def _start_pieces(make, h, n):
    for s0, sz in _row_pieces(h, n):
        make(s0, sz).start()
    return make(0, h)


SEM = pl.BlockSpec(memory_space=pltpu.SEMAPHORE)
SPLIT_COPY = pltpu.CompilerParams(has_side_effects=pltpu.SideEffectType.DATAFLOW_SIDE_EFFECTING)
N_OTHERS = 3


def _hbm(a):
    return pltpu.with_memory_space_constraint(a, pltpu.HBM)


def _chip_rows(buf, chip, s0, sz):
    return buf.at[2 * chip[0] + chip[1], pl.ds(s0, sz)]


def allgather_start(bufs, *, name):
    n = len(bufs)

    def body(*refs):
        ins, send, recv, token = refs[:n], refs[n:2 * n], refs[2 * n:3 * n], refs[4 * n]
        x, y, c, others = _place()
        for i in range(n):
            h = bufs[i].shape[1] // 2
            for f, chip in enumerate(others):
                for s0, sz in _row_pieces(h, DMA_PIECES):
                    mine = _chip_rows(ins[i], (x, y), c * h + s0, sz)
                    _rcopy(mine, mine, send[i], recv[i], f, (*chip, c)).start()
        token[...] = jnp.zeros_like(token)

    res = pl.pallas_call(
        body, name=name, in_specs=[HBM] * n,
        out_specs=[SEM] * (2 * n) + [HBM] * n + [pl.BlockSpec(memory_space=pltpu.VMEM)],
        out_shape=[pltpu.SemaphoreType.DMA((N_OTHERS,))] * (2 * n) + [pltpu.HBM(b.shape, b.dtype) for b in bufs]
        + [jax.ShapeDtypeStruct((1, D_MODEL), F32)],
        input_output_aliases={i: 2 * n + i for i in range(n)},
        compiler_params=SPLIT_COPY,
    )(*[_hbm(b) for b in bufs])
    return res[:n], res[n:2 * n], res[2 * n:3 * n], res[3 * n]


def allgather_wait(buf, send, recv, after, *, name):
    h = buf.shape[1] // 2

    def body(buf_ref, send_sems, recv_sems, after_ref, out_ref):
        del after_ref, out_ref
        x, y, c, others = _place()
        for f, chip in enumerate(others):
            mine = _chip_rows(buf_ref, (x, y), c * h, h)
            theirs = _chip_rows(buf_ref, chip, c * h, h)
            cp = _rcopy(mine, theirs, send_sems, recv_sems, f, (*chip, c))
            cp.wait_send()
            cp.wait_recv()

    return pl.pallas_call(
        body, name=name, in_specs=[HBM, SEM, SEM, HBM], out_specs=HBM,
        out_shape=pltpu.HBM(buf.shape, buf.dtype), input_output_aliases={0: 0},
        compiler_params=SPLIT_COPY,
    )(buf, send, recv, after)


def allgather_forward(buf, *, name):
    h = buf.shape[1] // 2

    def body(in_ref, out_ref, send_sems, recv_sems):
        del in_ref
        x, y, c, others = _place()
        sibling = (x, y, 1 - c)
        sent = []
        for f, chip in enumerate(others):
            sent.append(_start_pieces(
                lambda s0, sz: _rcopy(_chip_rows(out_ref, chip, c * h + s0, sz), _chip_rows(out_ref, chip, c * h + s0, sz),
                                      send_sems, recv_sems, f, sibling), h, DMA_PIECES))
        for f, chip in enumerate(others):
            blk = _chip_rows(out_ref, chip, (1 - c) * h, h)
            _rcopy(blk, blk, send_sems, recv_sems, f, sibling).wait_recv()
        for cp in sent:
            cp.wait_send()

    return pl.pallas_call(
        body, name=name, in_specs=[HBM], out_specs=HBM,
        out_shape=jax.ShapeDtypeStruct(buf.shape, buf.dtype), input_output_aliases={0: 0},
        scratch_shapes=_dma_sems(N_OTHERS, N_OTHERS),
    )(buf)


def swap_halves(grads, *, name):
    n = len(grads)

    def body(*refs):
        ins, theirs = refs[:n], refs[n:2 * n]
        send_sems, recv_sems = refs[2 * n:]
        x, y, c, _ = _place()
        for i in range(n):
            h = grads[i].shape[1] // 2
            for k in range(N_CHIPS):
                for s0, sz in _row_pieces(h, DMA_PIECES):
                    _rcopy(ins[i].at[k, pl.ds((1 - c) * h + s0, sz)], theirs[i].at[k, pl.ds(s0, sz)],
                           send_sems, recv_sems, i, (x, y, 1 - c)).start()
        for i in range(n):
            h = grads[i].shape[1] // 2
            _rcopy(ins[i].at[:, pl.ds((1 - c) * h, h)], theirs[i], send_sems, recv_sems, i, (x, y, 1 - c)).wait()

    return pl.pallas_call(
        body, name=name, in_specs=[HBM] * n, out_specs=[HBM] * n,
        out_shape=[jax.ShapeDtypeStruct((N_CHIPS, g.shape[1] // 2, SLAB), g.dtype) for g in grads],
        scratch_shapes=_dma_sems(n, n))(*grads)


def scatter_start(part, *, name):
    h = part.shape[1]

    def body(part_ref, land_ref, send, recv, part_out, land_out, token):
        del part_out, land_out
        x, y, c, others = _place()
        for f, chip in enumerate(others):
            for s0, sz in _row_pieces(h, DMA_PIECES):
                _rcopy(_chip_rows(part_ref, chip, s0, sz), land_ref.at[f, pl.ds(s0, sz)], send, recv, f, (*chip, c)).start()
        token[...] = jnp.zeros_like(token)

    land = lax.empty((N_OTHERS,) + part.shape[1:], part.dtype)
    return pl.pallas_call(
        body, name=name, in_specs=[HBM, HBM],
        out_specs=[SEM, SEM, HBM, HBM, pl.BlockSpec(memory_space=pltpu.VMEM)],
        out_shape=[pltpu.SemaphoreType.DMA((N_OTHERS,))] * 2 + [pltpu.HBM(part.shape, part.dtype), pltpu.HBM(land.shape, land.dtype),
                                                                 jax.ShapeDtypeStruct((1, D_MODEL), F32)],
        input_output_aliases={0: 2, 1: 3},
        compiler_params=SPLIT_COPY,
    )(_hbm(part), _hbm(land))


def scatter_wait(part, land, send, recv, after, *, name):
    h = part.shape[1]

    def body(part_ref, land_ref, send_sems, recv_sems, after_ref, part_out, land_out):
        del after_ref, part_out, land_out
        x, y, c, others = _place()
        for f, chip in enumerate(others):
            cp = _rcopy(_chip_rows(part_ref, chip, 0, h), land_ref.at[f], send_sems, recv_sems, f, (*chip, c))
            cp.wait_send()
            cp.wait_recv()

    return pl.pallas_call(
        body, name=name, in_specs=[HBM, HBM, SEM, SEM, HBM], out_specs=[HBM, HBM],
        out_shape=[pltpu.HBM(part.shape, part.dtype), pltpu.HBM(land.shape, land.dtype)],
        input_output_aliases={0: 0, 1: 1},
        compiler_params=SPLIT_COPY,
    )(part, land, send, recv, after)


def join_halves(bufs, *, name):
    n = len(bufs)

    def body(*refs):
        outs = refs[n:2 * n]
        send_sems, recv_sems = refs[2 * n:]
        x, y, c, _ = _place()
        sibling = (x, y, 1 - c)
        cps = []
        for i in range(n):
            h = bufs[i].shape[0] // 2
            snd = _start_pieces(
                lambda s0, sz: _rcopy(outs[i].at[pl.ds(c * h + s0, sz)], outs[i].at[pl.ds(c * h + s0, sz)],
                                      send_sems, recv_sems, i, sibling), h, 2 * DMA_PIECES_LOCAL)
            theirs = outs[i].at[pl.ds((1 - c) * h, h)]
            cps.append((snd, _rcopy(theirs, theirs, send_sems, recv_sems, i, sibling)))
        for snd, rcv in cps:
            snd.wait_send()
            rcv.wait_recv()

    return pl.pallas_call(
        body, name=name, in_specs=[HBM] * n, out_specs=[HBM] * n,
        out_shape=[jax.ShapeDtypeStruct(b.shape, b.dtype) for b in bufs],
        input_output_aliases={i: i for i in range(n)},
        scratch_shapes=_dma_sems(n, n),
    )(*bufs)


def allreduce_small(v, *, name):
    rows, n = v.shape

    def body(x_ref, sum_ref, all_ref, send_sems, recv_sems, local_sem):
        x, y, c, others = _place()
        me, sibling = (x, y, c), (x, y, 1 - c)

        def blk(px, py, pc):
            return all_ref.at[pl.ds((4 * px + 2 * py + pc) * rows, rows), :]

        def copy(k, block, to, src=None):
            return _rcopy(blk(*block) if src is None else src, blk(*block), send_sems, recv_sems, k, to)

        mine = pltpu.make_async_copy(x_ref, blk(*me), local_sem)
        mine.start()
        first = [copy(0, me, sibling, src=x_ref)]
        first += [copy(1 + f, me, (*chip, c), src=x_ref) for f, chip in enumerate(others)]
        for cp in first:
            cp.start()
        passed = [copy(4 + f, (*chip, c), sibling) for f, chip in enumerate(others)]
        for f, chip in enumerate(others):
            copy(1 + f, (*chip, c), me).wait_recv()
            passed[f].start()
        copy(0, sibling, me).wait_recv()
        for f, chip in enumerate(others):
            copy(4 + f, (*chip, 1 - c), me).wait_recv()
        for cp in first + passed:
            cp.wait_send()
        mine.wait()
        acc = all_ref[pl.ds(0, rows), :]
        for d in range(1, N_DEVICES):
            acc = acc + all_ref[pl.ds(d * rows, rows), :]
        sum_ref[...] = acc

    vm = pl.BlockSpec(memory_space=pltpu.VMEM)
    return pl.pallas_call(
        body, name=name, in_specs=[vm], out_specs=[vm, vm],
        out_shape=[jax.ShapeDtypeStruct((rows, n), F32), jax.ShapeDtypeStruct((N_DEVICES * rows, n), F32)],
        scratch_shapes=_dma_sems(7, 7) + [pltpu.SemaphoreType.DMA],
    )(v)[0]


def add_pairs(grad, theirs, where, *, name):
    h = theirs.shape[1]
    spec = pl.BlockSpec((None, h, SLAB), lambda k, w: (k, 0, 0))

    def body(w_ref, a_ref, b_ref, o_ref):
        del w_ref
        o_ref[...] = (a_ref[...].astype(F32) + b_ref[...].astype(F32)).astype(o_ref.dtype)

    return pl.pallas_call(
        body, name=name,
        grid_spec=pltpu.PrefetchScalarGridSpec(
            num_scalar_prefetch=1, grid=(N_CHIPS,),
            in_specs=[pl.BlockSpec((None, h, SLAB), lambda k, w: (k, w[1], 0)), spec], out_specs=spec),
        out_shape=jax.ShapeDtypeStruct(theirs.shape, theirs.dtype),
        compiler_params=_cparams(("parallel",)))(where, grad, theirs)


def add_chips(pair, got, where, *, name):
    h = pair.shape[1]
    tr = h // 2

    def body(w_ref, a_ref, b_ref, o_ref):
        del w_ref
        acc = a_ref[...].astype(F32)
        for f in range(3):
            acc = acc + b_ref[f].astype(F32)
        o_ref[...] = acc

    return pl.pallas_call(
        body, name=name,
        grid_spec=pltpu.PrefetchScalarGridSpec(
            num_scalar_prefetch=1, grid=(2,),
            in_specs=[pl.BlockSpec((None, tr, SLAB), lambda i, w: (w[0], i, 0)),
                      pl.BlockSpec((3, tr, SLAB), lambda i, w: (0, i, 0))],
            out_specs=pl.BlockSpec((tr, SLAB), lambda i, w: (2 * w[1] + i, 0))),
        out_shape=jax.ShapeDtypeStruct((2 * h, SLAB), F32),
        compiler_params=_cparams(("parallel",)))(where, pair, got)


DEPTH = 4
MIXER = (0, 1, 2, 0)
W_IN_COLS = (768, 320, 772)
W_IN_PAD = (768, 512, 1024)
MATS = ("up", "down", "inp", "out", "gate", "proj")
MAT_ARG = dict(up="w_up", down="w_down", inp="w_in", out="w_out", gate="w_ple_gate", proj="w_ple_proj")
GAINS = ("attn_norm", "mlp_norm", "ple_norm")
N_SMALL = 16
KINDS = ("grad_", "delta_", "new_m_", "new_v_")


def _layout(kind):
    ns_in = W_IN_PAD[kind] // SLAB
    off = 8192 + 1024 * ns_in
    lay = dict(up=(0, 1024, 4, False), down=(4096, 1024, 4, True), inp=(8192, 1024, ns_in, False),
               out=(off, 256, 4, True), gate=(off + 1024, 256, 4, True), proj=(off + 2048, 256, 1, False))
    return lay, off + 2304


def _to_slabs(w):
    k, c = w.shape
    return w.reshape(k, c // SLAB, SLAB).transpose(1, 0, 2).reshape(-1, SLAB)


def _pad_cols(w, n):
    return jnp.pad(w, ((0, 0), (0, n - w.shape[1])))


def _heads(x2d, n):
    return x2d.reshape(x2d.shape[0], n, HEAD_DIM).transpose(1, 0, 2)


def _unheads(x3d):
    n, s, _ = x3d.shape
    return x3d.transpose(1, 0, 2).reshape(s, n * HEAD_DIM)


def _chip_cols(x2d, c, cpad):
    s = x2d.shape[0]
    return jnp.pad(x2d.reshape(s, N_CHIPS, c), ((0, 0), (0, 0), (0, cpad - c))).reshape(s, N_CHIPS * cpad)


def _unchip_cols(x2d, c, cpad):
    s = x2d.shape[0]
    return x2d.reshape(s, N_CHIPS, cpad)[:, :, :c].reshape(s, N_CHIPS * c)


def _forget_cols(wg, t):
    off, K, _, _ = t
    cols = []
    for g in range(3 * N_HEADS * HEAD_DIM, 3 * N_HEADS * HEAD_DIM + N_HEADS):
        chip, local = divmod(g, W_IN_COLS[2])
        q, c = divmod(local, SLAB)
        cols.append(wg[chip, off + q * K:off + (q + 1) * K, c:c + 1])
    return jnp.concatenate(cols, axis=1)


def _add_res(acc, res):
    return (acc + res,)


def _relu2(acc):
    return acc, jnp.square(jnp.maximum(acc, 0.0))


def _relu2_bwd(acc, u):
    return (acc * (2.0 * jnp.maximum(u.astype(F32), 0.0)),)


def _ple_fwd(acc, x2, pp):
    return x2 + pp * _sigmoid(acc), acc


def _ple_bwd(dx, pp, gl):
    gate = _sigmoid(gl)
    return dx * gate, dx * pp * gate * (1.0 - gate)


def _layer_fwd(i, kind, x0, p_bf, wg, lay, gains, extra, tabs):
    s = x0.shape[0]
    an, mn, pn = gains
    sv = dict(x0=x0)
    if kind == 0:
        proj, h1 = mm_nn(x0, wg, lay["inp"], name=f"w_in_{i}", norm_gain=an)
        a, tot = sb_fwd(proj, name=f"sb_fwd_{i}")
        sv.update(proj=proj, tot=tot)
    elif kind == 1:
        projp, h1 = mm_nn(x0, wg, lay["inp"], name=f"w_in_{i}", out_dtypes=(F32,), norm_gain=an)
        proj = _unchip_cols(projp, W_IN_COLS[1], W_IN_PAD[1])
        nq = N_HEADS * HEAD_DIM
        nqk = nq + SWA_KV_HEADS * HEAD_DIM
        qk = rope_fwd(proj[:, :nqk], tabs, name=f"rope_{i}")
        q = _heads(qk[:, :nq], N_HEADS).reshape(SWA_KV_HEADS, SWA_GROUP, s, HEAD_DIM)
        front = ((0, 0), (SWA_WINDOW, 0), (0, 0))
        kp = jnp.pad(_heads(qk[:, nq:], SWA_KV_HEADS), front)
        vp = jnp.pad(_heads(proj[:, nqk:].astype(BF16), SWA_KV_HEADS), front)
        sink = jnp.repeat(extra.reshape(SWA_KV_HEADS, SWA_GROUP), ATT_BLK, axis=1)[:, :, None]
        o4, lse = swa_fwd(q, kp, vp, sink, name=f"swa_fwd_{i}")
        a = _unheads(o4.reshape(N_HEADS, s, HEAD_DIM))
        sv.update(q=q, kp=kp, vp=vp, sink=sink, o4=o4, lse=lse)
    else:
        projp, h1 = mm_nn(x0, wg, lay["inp"], name=f"w_in_{i}", norm_gain=an)
        nqkv = 3 * N_HEADS * HEAD_DIM
        proj = _unchip_cols(projp, W_IN_COLS[2], W_IN_PAD[2])[:, :nqkv]
        fl = mm_plain(h1, _pad_cols(_forget_cols(wg, lay["inp"]), 128), name=f"w_forget_{i}")
        bp = _pad_cols(extra[None], 128)
        cum_t = fox_gate_fwd(fl, bp, name=f"gate_fwd_{i}")[:, :N_HEADS].T
        cq = cum_t[:, :, None]
        ck = cum_t.reshape(N_HEADS, s // min(ATT_BK, s), 1, min(ATT_BK, s))
        a, lse = fox_fwd(proj, cq, ck, name=f"fox_fwd_{i}")
        sv.update(proj=proj, fl=fl, bp=bp, cq=cq, ck=ck, lse=lse)
    x1 = mm_nn(a, wg, lay["out"], name=f"w_out_{i}", epi=_add_res, extras=(x0,), out_dtypes=(F32,))[0]
    u, r, h2 = mm_nn(x1, wg, lay["up"], name=f"w_up_{i}", epi=_relu2, out_dtypes=(BF16, BF16), norm_gain=mn)
    x2 = mm_nn(r, wg, lay["down"], name=f"w_down_{i}", epi=_add_res, extras=(x1,), out_dtypes=(F32,))[0]
    pp = mm_nn(p_bf, wg, lay["proj"], name=f"w_ple_proj_{i}", out_dtypes=(F32,))[0]
    x3, gl, h3 = mm_nn(x2, wg, lay["gate"], name=f"w_ple_gate_{i}", epi=_ple_fwd, extras=(x2, pp), out_dtypes=(F32, F32),
                       norm_gain=pn)
    sv.update(h1=h1, a=a, x1=x1, h2=h2, u=u, r=r, x2=x2, h3=h3, pp=pp, gl=gl)
    return x3, sv


def _layer_bwd(i, kind, dx3, sv, p_bf, wg, lay, n_rows, gains, tabs):
    s = dx3.shape[0]
    an, mn, pn = gains
    g = lax.empty((N_CHIPS, n_rows, SLAB), BF16)
    d_pp, d_gl = ew(_ple_bwd, [dx3, sv["pp"], sv["gl"]], [BF16, BF16], name=f"ple_bwd_{i}")
    g = mm_tn(p_bf, d_pp, g, lay["proj"], name=f"dw_ple_proj_{i}")
    g = mm_tn(sv["h3"], d_gl, g, lay["gate"], name=f"dw_ple_gate_{i}")
    d_h3 = mm_nt(d_gl, wg, lay["gate"], name=f"dx_ple_gate_{i}", out_dtypes=(F32,))[0]
    dx2, dx2b, d_pn = rms_bwd(sv["x2"], pn, d_h3, dx3, name=f"ple_norm_bwd_{i}")
    g = mm_tn(sv["r"], dx2b, g, lay["down"], name=f"dw_down_{i}")
    d_u = mm_nt(dx2b, wg, lay["down"], name=f"dx_down_{i}", epi=_relu2_bwd, extras=(sv["u"],))[0]
    g = mm_tn(sv["h2"], d_u, g, lay["up"], name=f"dw_up_{i}")
    dx1, dx1b, d_mn = mm_nt(d_u, wg, lay["up"], name=f"dx_up_{i}", rms=(sv["x1"], mn, dx2))
    g = mm_tn(sv["a"], dx1b, g, lay["out"], name=f"dw_out_{i}")
    d_a = mm_nt(dx1b, wg, lay["out"], name=f"dx_out_{i}")[0]
    d_extra = None
    if kind == 0:
        d_proj = jnp.concatenate(sb_bwd(sv["proj"], sv["tot"], d_a, name=f"sb_bwd_{i}"), axis=1)
    elif kind == 1:
        do4 = _heads(d_a, N_HEADS).reshape(SWA_KV_HEADS, SWA_GROUP, s, HEAD_DIM)
        dq, dkp, dvp, dsr = swa_bwd(sv["q"], sv["kp"], sv["vp"], sv["sink"], sv["o4"], sv["lse"], do4, name=f"swa_bwd_{i}")
        dqk = jnp.concatenate([_unheads(dq.reshape(N_HEADS, s, HEAD_DIM)), _unheads(dkp[:, SWA_WINDOW:])], axis=1)
        dqk = rope_bwd(dqk, tabs, name=f"rope_bwd_{i}")
        d_proj = jnp.concatenate([dqk, _unheads(dvp[:, SWA_WINDOW:]).astype(BF16)], axis=1)
        d_proj = _chip_cols(d_proj, W_IN_COLS[1], W_IN_PAD[1])
        d_extra = jnp.sum(dsr[..., 0], axis=2).reshape(N_HEADS)
    else:
        dq, dk, dv, dcq, dck = fox_bwd(sv["proj"], sv["a"], sv["lse"], sv["cq"], sv["ck"], d_a, name=f"fox_bwd_{i}")
        dcum = _pad_cols((dcq[:, :, 0] - dck.reshape(N_HEADS, s)).T, 128)
        dfl, dbp = fox_gate_bwd(dcum, sv["fl"], sv["bp"], name=f"gate_bwd_{i}")
        d_proj = jnp.concatenate([dq, dk, dv, dfl[:, :N_HEADS].astype(BF16)], axis=1)
        d_proj = _chip_cols(d_proj, W_IN_COLS[2], W_IN_PAD[2])
        d_extra = dbp[0, :N_HEADS]
    g = mm_tn(sv["h1"], d_proj, g, lay["inp"], name=f"dw_in_{i}")
    dx0, _, d_an = mm_nt(d_proj, wg, lay["inp"], name=f"dx_in_{i}", rms=(sv["x0"], an, dx1))
    return dx0, g, (d_an, d_mn, d_pn), d_extra


def _small_rows(a, prefix):
    rows = [a[f"{prefix}{n}_{i}"] for i in range(DEPTH) for n in GAINS] + [a[f"{prefix}final_norm"]]
    rows += [_pad_cols(a[f"{prefix}{n}"][None], D_MODEL)[0] for n in ("sinks_1", "b_forget_2")]
    return jnp.stack(rows + [jnp.zeros((D_MODEL,), F32)])


def _train_step(a):
    x = a["x"][0]
    tabs = rope_tables(x.shape[0], (N_HEADS + SWA_KV_HEADS) * HEAD_DIM)
    lays = [_layout(k) for k in MIXER]

    def natural(prefix, i, m):
        w = a[f"{prefix}{MAT_ARG[m]}_{i}"]
        return _pad_cols(w, W_IN_PAD[MIXER[i]]) if m == "inp" else w

    chip = 2 * lax.axis_index("x") + lax.axis_index("y")
    where = jnp.stack([chip, lax.axis_index("c")]).astype(jnp.int32)
    def own_block(i, zero):
        pk = jnp.concatenate([_to_slabs((natural("", i, m) + zero).astype(BF16)) for m in MATS], axis=0)
        return lax.dynamic_update_slice(lax.empty((N_CHIPS,) + pk.shape, BF16), pk[None], (chip, 0, 0))

    sends, recvs, bufs, token = allgather_start([own_block(0, 0.0)], name="allgather_start_0")
    more = allgather_start([own_block(i, token[0, 0]) for i in range(1, DEPTH)], name="allgather_start_1")
    sends, recvs, bufs, token = sends + more[0], recvs + more[1], bufs + more[2], more[3]

    gains = [tuple(a[f"{n}_{i}"][None] for n in GAINS) for i in range(DEPTH)]
    extras = [None, a["sinks_1"], a["b_forget_2"], None]
    p_bf = [a["p"][i, 0].astype(BF16) for i in range(DEPTH)]

    saved, wgs, after = [], [], token
    for i in range(DEPTH):
        landed = allgather_wait(bufs[i], sends[i], recvs[i], after, name=f"allgather_wait_{i}")
        wgs.append(allgather_forward(landed, name=f"allgather_forward_{i}"))
        x, sv = _layer_fwd(i, MIXER[i], x, p_bf[i], wgs[i], lays[i][0], gains[i], extras[i], tabs)
        saved.append(sv)
        after = x
    dx, d_final, loss = loss_head(x, a["final_norm"][None], a["loss_target"][0], name="loss_head")

    def finish(i, started, after):
        send, recv, part, land, _ = started
        part, got = scatter_wait(part, land, send, recv, after, name=f"scatter_wait_{i}")
        return join_halves([add_chips(part, got, where, name=f"add_chips_{i}")], name=f"join_halves_{i}")[0]

    small = [None] * N_SMALL
    small[12] = d_final[0]
    small[15] = _pad_cols(loss[:, :1], D_MODEL)[0]
    gfull = [None] * DEPTH
    started = None
    for i in reversed(range(DEPTH)):
        an, mn, pn = gains[i]
        if started is not None:
            pn = pn + started[4]
        dx, grad, d_gains, d_extra = _layer_bwd(i, MIXER[i], dx, saved[i], p_bf[i], wgs[i], lays[i][0], lays[i][1],
                                                (an, mn, pn), tabs)
        for j in range(3):
            small[3 * i + j] = d_gains[j][0]
        if d_extra is not None:
            small[12 + MIXER[i]] = _pad_cols(d_extra[None], D_MODEL)[0]
        if started is not None:
            gfull[i + 1] = finish(i + 1, started, dx)
        theirs = swap_halves([grad], name=f"swap_halves_{i}")[0]
        started = scatter_start(add_pairs(grad, theirs, where, name=f"add_pairs_{i}"), name=f"scatter_start_{i}")
    small = allreduce_small(jnp.stack(small), name="allreduce_small")

    out = {"loss": small[15, 0], "grad_x": dx[None]}
    res = adamw_small(small, _small_rows(a, ""), _small_rows(a, "m_"), _small_rows(a, "v_"), name="adamw_small")
    for i in reversed(range(DEPTH)):
        if i == 0:
            gfull[0] = finish(0, started, out[f"delta_{MAT_ARG[MATS[-1]]}_1"])
        for m in MATS:
            upd = adamw(gfull[i], lays[i][0][m], natural("", i, m), natural("m_", i, m), natural("v_", i, m), started[4],
                        name=f"adamw_{MAT_ARG[m]}_{i}")
            cols = a[f"{MAT_ARG[m]}_{i}"].shape[1]
            for kd, r in zip(KINDS, upd):
                out[f"{kd}{MAT_ARG[m]}_{i}"] = r[:, :cols]
    for kd, r in zip(KINDS, res):
        for i in range(DEPTH):
            for j, n in enumerate(GAINS):
                out[f"{kd}{n}_{i}"] = r[3 * i + j]
        out[f"{kd}final_norm"] = r[12]
        out[f"{kd}sinks_1"] = r[13, :N_HEADS]
        out[f"{kd}b_forget_2"] = r[14, :N_HEADS]
    return out


def _weight_names():
    names = []
    for i in range(DEPTH):
        names += [f"attn_norm_{i}", f"w_in_{i}", f"w_out_{i}"] + [[], ["sinks_1"], ["b_forget_2"]][MIXER[i]]
        names += [f"mlp_norm_{i}", f"w_up_{i}", f"w_down_{i}", f"ple_norm_{i}", f"w_ple_gate_{i}", f"w_ple_proj_{i}"]
    return names + ["final_norm"]


def kernel(x, p, attn_norm_0, w_in_0, w_out_0, mlp_norm_0, w_up_0, w_down_0, ple_norm_0, w_ple_gate_0, w_ple_proj_0, attn_norm_1, w_in_1, w_out_1, sinks_1, mlp_norm_1, w_up_1, w_down_1, ple_norm_1, w_ple_gate_1, w_ple_proj_1, attn_norm_2, w_in_2, w_out_2, b_forget_2, mlp_norm_2, w_up_2, w_down_2, ple_norm_2, w_ple_gate_2, w_ple_proj_2, attn_norm_3, w_in_3, w_out_3, mlp_norm_3, w_up_3, w_down_3, ple_norm_3, w_ple_gate_3, w_ple_proj_3, final_norm, loss_target, m_attn_norm_0, m_w_in_0, m_w_out_0, m_mlp_norm_0, m_w_up_0, m_w_down_0, m_ple_norm_0, m_w_ple_gate_0, m_w_ple_proj_0, m_attn_norm_1, m_w_in_1, m_w_out_1, m_sinks_1, m_mlp_norm_1, m_w_up_1, m_w_down_1, m_ple_norm_1, m_w_ple_gate_1, m_w_ple_proj_1, m_attn_norm_2, m_w_in_2, m_w_out_2, m_b_forget_2, m_mlp_norm_2, m_w_up_2, m_w_down_2, m_ple_norm_2, m_w_ple_gate_2, m_w_ple_proj_2, m_attn_norm_3, m_w_in_3, m_w_out_3, m_mlp_norm_3, m_w_up_3, m_w_down_3, m_ple_norm_3, m_w_ple_gate_3, m_w_ple_proj_3, m_final_norm, v_attn_norm_0, v_w_in_0, v_w_out_0, v_mlp_norm_0, v_w_up_0, v_w_down_0, v_ple_norm_0, v_w_ple_gate_0, v_w_ple_proj_0, v_attn_norm_1, v_w_in_1, v_w_out_1, v_sinks_1, v_mlp_norm_1, v_w_up_1, v_w_down_1, v_ple_norm_1, v_w_ple_gate_1, v_w_ple_proj_1, v_attn_norm_2, v_w_in_2, v_w_out_2, v_b_forget_2, v_mlp_norm_2, v_w_up_2, v_w_down_2, v_ple_norm_2, v_w_ple_gate_2, v_w_ple_proj_2, v_attn_norm_3, v_w_in_3, v_w_out_3, v_mlp_norm_3, v_w_up_3, v_w_down_3, v_ple_norm_3, v_w_ple_gate_3, v_w_ple_proj_3, v_final_norm):
    out = _train_step(dict(locals()))
    return (out["loss"], out["grad_x"], *[out[kd + n] for kd in KINDS for n in _weight_names()])
```

```python
import jax
import jax.numpy as jnp
from jax import lax
from jax.experimental import pallas as pl
from jax.experimental.pallas import tpu as pltpu

F32 = jnp.float32
BF16 = jnp.bfloat16

D_MODEL = 1024
N_HEADS = 16
HEAD_DIM = 64
SWA_KV_HEADS = 2
SWA_GROUP = 8
SWA_WINDOW = 128
ROPE_THETA = 500000.0
ROPE_DIM = 16
RMS_EPS = 1e-6
NEG_INF = -1e30
ATTN_SCALE = HEAD_DIM ** -0.5
N_CHIPS = 4
N_DEVICES = 8

SLAB = 256
ATT_BLK = 128
ATT_BQ = 512
ATT_BK = 512
ROW_TILE = 256
V7X_VMEM_LIMIT = 56 * 1024 * 1024

ADAM_LR, ADAM_B1, ADAM_B2, ADAM_EPS, ADAM_WD, ADAM_STEP = 0.001, 0.9, 0.999, 1e-08, 0.01, 10


def _cparams(sem=None):
    return pltpu.CompilerParams(dimension_semantics=sem, vmem_limit_bytes=V7X_VMEM_LIMIT)


def _dot(a, b):
    return jnp.dot(a, b, preferred_element_type=F32)


def _dot_nt(a, b):
    return lax.dot_general(a, b, (((1,), (1,)), ((), ())), preferred_element_type=F32)


def _dot_tn(a, b):
    return lax.dot_general(a, b, (((0,), (0,)), ((), ())), preferred_element_type=F32)


def _row_tile(M, K):
    return min(M, 1024) if K >= 1024 else M


def _finish(epi, acc, ex, outs):
    res = epi(acc, *[e[...] for e in ex]) if epi is not None else (acc,)
    for o, r in zip(outs, res):
        o[...] = r.astype(o.dtype)


def _once(shape, index_map):
    return pl.BlockSpec(shape, index_map, pipeline_mode=pl.Buffered(1))


def mm_nn(a, wg, t, *, name, epi=None, extras=(), out_dtypes=(BF16,), norm_gain=None):
    off, K, ns, row = t
    M = a.shape[0]
    sb = off // K
    ne, no = len(extras), len(out_dtypes)
    norm = norm_gain is not None
    if row:
        tm = M if norm else _row_tile(M, K)
        nb = N_CHIPS
        grid = (M // tm, ns)
        a_shape = (tm, N_CHIPS * K)
        a_spec = (_once if norm else pl.BlockSpec)(a_shape, lambda i, q: (i, 0))
        b_specs = [pl.BlockSpec((None, K, SLAB), lambda i, q, j=j: (j, sb + q, 0)) for j in range(nb)]
        tile = pl.BlockSpec((tm, SLAB), lambda i, q: (i, q))
        n_out = ns * SLAB
    else:
        nb = ns
        grid = (N_CHIPS,)
        a_shape = (M, K)
        a_spec = (_once if norm else pl.BlockSpec)(a_shape, lambda j: (0, 0))
        b_specs = [pl.BlockSpec((None, K, SLAB), lambda j, q=q: (j, sb + q, 0)) for q in range(ns)]
        tile = pl.BlockSpec((M, ns * SLAB), lambda j: (0, j))
        n_out = N_CHIPS * ns * SLAB

    def body(a_ref, *rest):
        if norm:
            g_ref, rest, h_out, h_ref = rest[0], rest[1:-2], rest[-2], rest[-1]

            @pl.when(pl.program_id(len(grid) - 1) == 0)
            def _():
                xv = a_ref[...]
                h_ref[...] = (xv * _rstd(xv) * g_ref[...]).astype(BF16)
                h_out[...] = h_ref[...]

            a_ref = h_ref
        bs, ex, outs = rest[:nb], rest[nb:nb + ne], rest[nb + ne:]
        if row:
            acc = _dot(a_ref[:, pl.ds(0, K)], bs[0][...])
            for j in range(1, nb):
                acc = acc + _dot(a_ref[:, pl.ds(j * K, K)], bs[j][...])
            _finish(epi, acc, ex, outs)
        else:
            av = a_ref[...]
            for q in range(ns):
                cols = pl.ds(q * SLAB, SLAB)
                _finish(epi, _dot(av, bs[q][...]), [e.at[:, cols] for e in ex], [o.at[:, cols] for o in outs])

    h_spec = _once(a_shape, (lambda i, q: (i, 0)) if row else (lambda j: (0, 0)))
    return pl.pallas_call(
        body, name=name, grid=grid,
        in_specs=[a_spec] + ([pl.BlockSpec(norm_gain.shape, lambda *_: (0, 0))] if norm else []) + b_specs + [tile] * ne,
        out_specs=[tile] * no + ([h_spec] if norm else []),
        out_shape=[jax.ShapeDtypeStruct((M, n_out), d) for d in out_dtypes]
        + ([jax.ShapeDtypeStruct(a.shape, BF16)] if norm else []),
        scratch_shapes=[pltpu.VMEM(a_shape, BF16)] if norm else [],
        compiler_params=_cparams((("arbitrary" if norm else "parallel"),) * len(grid)),
    )(a, *([norm_gain] if norm else []), *([wg] * nb), *extras)


def mm_nt(dy, wg, t, *, name, epi=None, extras=(), out_dtypes=(BF16,), rms=None):
    off, K, ns, row = t
    M = dy.shape[0]
    tm = _row_tile(M, K)
    sb = off // K
    if rms is not None:
        x, gain, dres = rms
        extras, out_dtypes = (x, dres), (F32, BF16)
    ne, no = len(extras), len(out_dtypes)
    grid = (M // tm, N_CHIPS)
    b_specs = [pl.BlockSpec((None, K, SLAB), lambda i, j, q=q: (j, sb + q, 0)) for q in range(ns)]
    if row:
        dy_spec = pl.BlockSpec((tm, ns * SLAB), lambda i, j: (i, 0))
        tile = pl.BlockSpec((tm, K), lambda i, j: (i, j))
        n_out = N_CHIPS * K
        sem = ("parallel", "parallel")
    else:
        dy_spec = pl.BlockSpec((tm, ns * SLAB), lambda i, j: (i, j))
        tile = pl.BlockSpec((tm, K), lambda i, j: (i, 0))
        n_out = K
        sem = ("arbitrary" if rms is not None else "parallel", "arbitrary")
    one = pl.BlockSpec((1, K), lambda i, j: (0, 0))

    def body(dy_ref, *rest):
        if rms is not None:
            g_ref, rest, dg_ref, acc_ref = rest[0], rest[1:-2], rest[-2], rest[-1]
            rest = rest + (acc_ref,)
        bs, ex, outs = rest[:ns], rest[ns:ns + ne], rest[ns + ne:ns + ne + no]
        part = _dot_nt(dy_ref[:, pl.ds(0, SLAB)], bs[0][...])
        for q in range(1, ns):
            part = part + _dot_nt(dy_ref[:, pl.ds(q * SLAB, SLAB)], bs[q][...])
        if row:
            _finish(epi, part, ex, outs)
        else:
            acc_ref = rest[-1]
            i, j = pl.program_id(0), pl.program_id(1)

            @pl.when(j == 0)
            def _():
                acc_ref[...] = part

            @pl.when(j > 0)
            def _():
                acc_ref[...] += part

            @pl.when(j == N_CHIPS - 1)
            def _():
                if rms is None:
                    _finish(epi, acc_ref[...], ex, outs)
                else:
                    dx, dg = _rms_bwd_tile(ex[0][...], g_ref[...], acc_ref[...])
                    dx = dx + ex[1][...]
                    outs[0][...] = dx
                    outs[1][...] = dx.astype(BF16)

                    @pl.when(i == 0)
                    def _():
                        dg_ref[...] = dg

                    @pl.when(i > 0)
                    def _():
                        dg_ref[...] += dg

    has = rms is not None
    return pl.pallas_call(
        body, name=name, grid=grid,
        in_specs=[dy_spec] + ([one] if has else []) + b_specs + [tile] * ne,
        out_specs=[tile] * no + ([one] if has else []),
        out_shape=[jax.ShapeDtypeStruct((M, n_out), d) for d in out_dtypes] + ([jax.ShapeDtypeStruct((1, K), F32)] if has else []),
        scratch_shapes=[] if row else [pltpu.VMEM((tm, K), F32)],
        compiler_params=_cparams(sem),
    )(dy, *([gain] if has else []), *([wg] * ns), *extras)


def mm_plain(a, b, *, name):
    M, K = a.shape
    N = b.shape[1]
    tm = min(M, 512)

    def body(a_ref, b_ref, o_ref):
        o_ref[...] = _dot(a_ref[...], b_ref[...])

    return pl.pallas_call(
        body, name=name, grid=(M // tm,),
        in_specs=[pl.BlockSpec((tm, K), lambda i: (i, 0)), pl.BlockSpec((K, N), lambda i: (0, 0))],
        out_specs=pl.BlockSpec((tm, N), lambda i: (i, 0)), out_shape=jax.ShapeDtypeStruct((M, N), F32),
        compiler_params=_cparams(("parallel",)),
    )(a, b)


def mm_tn(x, dy, g, t, *, name):
    off, K, ns, row = t
    S = x.shape[0]
    per = ns if off % (ns * K) == 0 else 1
    grid = (N_CHIPS, ns // per)
    if row:
        x_map = lambda j, q: (0, j)
        dy_map = lambda j, q: (0, q)
    else:
        x_map = lambda j, q: (0, 0)
        dy_map = lambda j, q: (0, j * (ns // per) + q)

    def body(g_in, x_ref, dy_ref, o_ref):
        del g_in
        xt = x_ref[...].T
        for q in range(per):
            o_ref[pl.ds(q * K, K), :] = _dot(xt, dy_ref[:, pl.ds(q * SLAB, SLAB)]).astype(o_ref.dtype)

    return pl.pallas_call(
        body, name=name, grid=grid,
        in_specs=[pl.BlockSpec(memory_space=pl.ANY), pl.BlockSpec((S, K), x_map), pl.BlockSpec((S, per * SLAB), dy_map)],
        out_specs=pl.BlockSpec((None, per * K, SLAB), lambda j, q: (j, off // (per * K) + q, 0)),
        out_shape=jax.ShapeDtypeStruct(g.shape, g.dtype),
        input_output_aliases={0: 0},
        compiler_params=_cparams(("parallel", "parallel")),
    )(g, x, dy)


def ew(fn, ins, out_dtypes, *, name, bcast=()):
    S = ins[0].shape[0]
    tr = min(ROW_TILE, S)
    cols = ins[0].shape[1]
    ni, nb = len(ins), len(bcast)

    def body(*refs):
        res = fn(*[r[...] for r in refs[:ni + nb]])
        for o, r in zip(refs[ni + nb:], res):
            o[...] = r.astype(o.dtype)

    return pl.pallas_call(
        body, name=name, grid=(S // tr,),
        in_specs=[pl.BlockSpec((tr, a.shape[1]), lambda i: (i, 0)) for a in ins]
        + [pl.BlockSpec(b.shape, lambda i: (0, 0)) for b in bcast],
        out_specs=[pl.BlockSpec((tr, cols), lambda i: (i, 0)) for _ in out_dtypes],
        out_shape=[jax.ShapeDtypeStruct((S, cols), d) for d in out_dtypes],
        compiler_params=_cparams(("parallel",)),
    )(*ins, *bcast)


def _rstd(x):
    return lax.rsqrt(jnp.mean(x * x, axis=-1, keepdims=True) + RMS_EPS)


def _sigmoid(x):
    return 1.0 / (1.0 + jnp.exp(-x))


def _log_sigmoid(z):
    return jnp.minimum(z, 0.0) - jnp.log(1.0 + jnp.exp(-jnp.abs(z)))


def _rms_bwd_tile(xv, gv, dh):
    rstd = _rstd(xv)
    xhat = xv * rstd
    gd = dh * gv
    dx = rstd * (gd - xhat * jnp.mean(xhat * gd, axis=-1, keepdims=True))
    return dx, jnp.sum(dh * xhat, axis=0, keepdims=True)


def rms_bwd(x, g, dh, dres, *, name):
    S, D = x.shape
    tr = min(ROW_TILE, S)

    def body(x_ref, g_ref, dh_ref, dres_ref, dx_ref, dxb_ref, dg_ref):
        i = pl.program_id(0)
        dx, dg = _rms_bwd_tile(x_ref[...], g_ref[...], dh_ref[...])
        dx = dx + dres_ref[...]
        dx_ref[...] = dx
        dxb_ref[...] = dx.astype(BF16)

        @pl.when(i == 0)
        def _():
            dg_ref[...] = dg

        @pl.when(i > 0)
        def _():
            dg_ref[...] += dg

    row = pl.BlockSpec((tr, D), lambda i: (i, 0))
    one = pl.BlockSpec((1, D), lambda i: (0, 0))
    return pl.pallas_call(
        body, name=name, grid=(S // tr,),
        in_specs=[row, one, row, row], out_specs=[row, row, one],
        out_shape=[jax.ShapeDtypeStruct((S, D), F32), jax.ShapeDtypeStruct((S, D), BF16),
                   jax.ShapeDtypeStruct((1, D), F32)],
        compiler_params=_cparams(("arbitrary",)),
    )(x, g, dh, dres)


def loss_head(x, g, target, *, name):
    S, D = x.shape
    tr = min(ROW_TILE, S)

    def body(x_ref, g_ref, t_ref, dx_ref, dg_ref, loss_ref):
        i = pl.program_id(0)
        xv, gv = x_ref[...], g_ref[...]
        err = xv * _rstd(xv) * gv - t_ref[...]
        part = 0.5 * jnp.sum(jnp.mean(err * err, axis=-1, keepdims=True), axis=0, keepdims=True)
        dx, dg = _rms_bwd_tile(xv, gv, err * (1.0 / D))
        dx_ref[...] = dx
        part = jnp.broadcast_to(part, loss_ref.shape)

        @pl.when(i == 0)
        def _():
            dg_ref[...] = dg
            loss_ref[...] = part

        @pl.when(i > 0)
        def _():
            dg_ref[...] += dg
            loss_ref[...] += part

    row = pl.BlockSpec((tr, D), lambda i: (i, 0))
    one = pl.BlockSpec((1, D), lambda i: (0, 0))
    return pl.pallas_call(
        body, name=name, grid=(S // tr,),
        in_specs=[row, one, row], out_specs=[row, one, pl.BlockSpec((1, 128), lambda i: (0, 0))],
        out_shape=[jax.ShapeDtypeStruct((S, D), F32), jax.ShapeDtypeStruct((1, D), F32),
                   jax.ShapeDtypeStruct((1, 128), F32)],
        compiler_params=_cparams(("arbitrary",)),
    )(x, g, target)


def rope_tables(S):
    half = ROPE_DIM // 2
    inv_freq = ROPE_THETA ** (-jnp.arange(half, dtype=F32) / half)
    ang = jnp.arange(S, dtype=F32)[:, None] * inv_freq[None, :]
    cos, sin = jnp.cos(ang), jnp.sin(ang)
    z = jnp.zeros((S, HEAD_DIM - ROPE_DIM), F32)
    zh = jnp.zeros((S, half), F32)
    c = jnp.concatenate([cos, cos, jnp.ones_like(z)], axis=1)
    sa = jnp.concatenate([zh, sin, z], axis=1)
    sb = jnp.concatenate([-sin, zh, z], axis=1)
    return [jnp.concatenate([t, t], axis=1) for t in (c, sa, sb)]


def _wide(t, n):
    return jnp.tile(t, (1, n // t.shape[1]))


def rope_fwd(xqk, tables, *, name):
    n, half = xqk.shape[1], ROPE_DIM // 2

    def fn(x, c, sa, sb):
        return (x * _wide(c, n) + pltpu.roll(x, half, 1) * _wide(sa, n) + pltpu.roll(x, n - half, 1) * _wide(sb, n),)

    return ew(fn, [xqk] + list(tables), [BF16], name=name)[0]


def rope_bwd(dy, tables, *, name):
    n, half = dy.shape[1], ROPE_DIM // 2

    def fn(d, c, sa, sb):
        return (d * _wide(c, n) + pltpu.roll(d * _wide(sa, n), n - half, 1) + pltpu.roll(d * _wide(sb, n), half, 1),)

    return ew(fn, [dy] + list(tables), [BF16], name=name)[0]


def _split3(x):
    h1 = x.astype(BF16)
    r1 = x - h1.astype(F32)
    h2 = r1.astype(BF16)
    return h1, h2, (r1 - h2.astype(F32)).astype(BF16)


def _tri(n, cmp):
    r = lax.broadcasted_iota(jnp.int32, (n, n), 0)
    c = lax.broadcasted_iota(jnp.int32, (n, n), 1)
    return cmp(r, c).astype(BF16)


def fox_gate_fwd(fl, b, *, name):
    S, W = fl.shape
    tr = min(ROW_TILE, S)

    def body(fl_ref, b_ref, cum_ref, carry):
        i = pl.program_id(0)

        @pl.when(i == 0)
        def _():
            carry[...] = jnp.zeros_like(carry)

        lower = _tri(tr, lambda r, c: r >= c)
        cs = carry[...]
        for piece in _split3(_log_sigmoid(fl_ref[...] + b_ref[...])):
            cs = cs + _dot(lower, piece)
        cum_ref[...] = cs
        carry[...] = cs[tr - 1:tr, :]

    return pl.pallas_call(
        body, name=name, grid=(S // tr,),
        in_specs=[pl.BlockSpec((tr, W), lambda i: (i, 0)), pl.BlockSpec((1, W), lambda i: (0, 0))],
        out_specs=pl.BlockSpec((tr, W), lambda i: (i, 0)),
        out_shape=jax.ShapeDtypeStruct((S, W), F32),
        scratch_shapes=[pltpu.VMEM((1, W), F32)],
        compiler_params=_cparams(("arbitrary",)),
    )(fl, b)


def fox_gate_bwd(dcum, fl, b, *, name):
    S, W = fl.shape
    tr = min(ROW_TILE, S)
    nb = S // tr

    def body(dc_ref, fl_ref, b_ref, dfl_ref, db_ref, carry):
        i = pl.program_id(0)

        @pl.when(i == 0)
        def _():
            carry[...] = jnp.zeros_like(carry)

        upper = _tri(tr, lambda r, c: r <= c)
        cs = carry[...]
        for piece in _split3(dc_ref[...]):
            cs = cs + _dot(upper, piece)
        carry[...] = cs[0:1, :]
        dfl = cs * _sigmoid(-(fl_ref[...] + b_ref[...]))
        dfl_ref[...] = dfl
        db = jnp.sum(dfl, axis=0, keepdims=True)

        @pl.when(i == 0)
        def _():
            db_ref[...] = db

        @pl.when(i > 0)
        def _():
            db_ref[...] += db

    rev = pl.BlockSpec((tr, W), lambda i: (nb - 1 - i, 0))
    one = pl.BlockSpec((1, W), lambda i: (0, 0))
    return pl.pallas_call(
        body, name=name, grid=(nb,),
        in_specs=[rev, rev, one], out_specs=[rev, one],
        out_shape=[jax.ShapeDtypeStruct((S, W), F32), jax.ShapeDtypeStruct((1, W), F32)],
        scratch_shapes=[pltpu.VMEM((1, W), F32)],
        compiler_params=_cparams(("arbitrary",)),
    )(dcum, fl, b)


def _blk_iota(tq, tk):
    return (lax.broadcasted_iota(jnp.int32, (tq, tk), 0), lax.broadcasted_iota(jnp.int32, (tq, tk), 1))


def _cs(xb, tri):
    return _dot(xb, tri)


def _rowsum(xb):
    return jnp.sum(xb.astype(F32), axis=1, keepdims=True)


def _sb_block(qs, k, cmr, shift):
    z = _dot_nt(qs, k)
    strict = cmr < shift
    lb = jnp.minimum(z, 0.0) - jnp.log(1.0 + jnp.exp(-jnp.abs(z)))
    lom = jnp.where(strict, lb - z, 0.0).astype(BF16)
    return lb, lom, strict


def _att_tiles(S):
    return min(ATT_BQ, S), min(ATT_BK, S)


PAIR = 2 * HEAD_DIM
N_PAIRS = N_HEADS // 2


def _pair_specs(S, tq):
    cols = D_MODEL // PAIR
    qspec = pl.BlockSpec((tq, PAIR), lambda p, i: (i, p))
    kspec = pl.BlockSpec((S, PAIR), lambda p, i: (0, cols + p))
    vspec = pl.BlockSpec((S, PAIR), lambda p, i: (0, 2 * cols + p))
    kvout = pl.BlockSpec((S, PAIR), lambda p, i: (0, p))
    vec = pl.BlockSpec((2, tq, 1), lambda p, i: (p, i, 0))
    return qspec, kspec, vspec, kvout, vec


def _head_lanes(h):
    lane = lax.broadcasted_iota(jnp.int32, (1, PAIR), 1)
    return (lane >= h * HEAD_DIM) & (lane < (h + 1) * HEAD_DIM)


def _only(sel, x):
    return jnp.where(sel, x, jnp.zeros_like(x))


def sb_fwd(proj, *, name):
    S = proj.shape[0]
    tq, tk = _att_tiles(S)
    qspec, kspec, vspec, _, vec = _pair_specs(S, tq)

    def body(q_ref, k_ref, v_ref, o_ref, t_ref):
        i = pl.program_id(1)
        row, col = _blk_iota(tq, tk)
        cmr = col - row
        below = _tri(tk, lambda r, c: r > c)
        nkb = (i + 1) * (tq // tk)
        out = []
        for h in range(2):
            sel = _head_lanes(h)
            qs = _only(sel, q_ref[...] * ATTN_SCALE)

            def step(n, carry):
                r_sum, acc = carry
                kb = nkb - 1 - n
                ks = pl.multiple_of(kb * tk, tk)
                lb, lom, strict = _sb_block(qs, k_ref[pl.ds(ks, tk), :], cmr, i * tq - kb * tk)
                w = jnp.where(strict, jnp.exp(lb + _cs(lom, below) + r_sum), 0.0)
                acc = acc + _dot(w.astype(BF16), _only(sel, v_ref[pl.ds(ks, tk), :]))
                return r_sum + _rowsum(lom), acc

            r_sum, acc = lax.fori_loop(0, nkb, step, (jnp.zeros((tq, 1), F32), jnp.zeros((tq, PAIR), F32)))
            t_ref[h] = r_sum
            out.append(acc)
        o_ref[...] = (out[0] + out[1]).astype(o_ref.dtype)

    return pl.pallas_call(
        body, name=name, grid=(N_PAIRS, S // tq),
        in_specs=[qspec, kspec, vspec], out_specs=[qspec, vec],
        out_shape=[jax.ShapeDtypeStruct((S, D_MODEL), BF16), jax.ShapeDtypeStruct((N_HEADS, S, 1), F32)],
        compiler_params=_cparams(("parallel", "arbitrary")),
    )(proj, proj, proj)


def sb_bwd(proj, tot, do, *, name):
    S = proj.shape[0]
    tq, tk = _att_tiles(S)
    qspec, kspec, vspec, kvout, vec = _pair_specs(S, tq)

    def body(q_ref, k_ref, v_ref, t_ref, do_ref, dq_ref, dk_out, dv_out, dk_ref, dv_ref):
        i = pl.program_id(1)

        @pl.when(i == 0)
        def _():
            dk_ref[...] = jnp.zeros_like(dk_ref)
            dv_ref[...] = jnp.zeros_like(dv_ref)

        row, col = _blk_iota(tq, tk)
        cmr = col - row
        upto = _tri(tk, lambda r, c: r <= c)
        before = _tri(tk, lambda r, c: r < c)
        out = []
        for h in range(2):
            sel = _head_lanes(h)
            qs, dov, t_all = _only(sel, q_ref[...] * ATTN_SCALE), _only(sel, do_ref[...]), t_ref[h]

            def step(kb, carry):
                p_sum, e_sum, dq = carry
                ks = pl.multiple_of(kb * tk, tk)
                kv = k_ref[pl.ds(ks, tk), :]
                lb, lom, strict = _sb_block(qs, kv, cmr, i * tq - kb * tk)
                tail = t_all - p_sum - _cs(lom, upto)
                w = jnp.where(strict, jnp.exp(lb + tail), 0.0)
                e = _dot_nt(dov, v_ref[pl.ds(ks, tk), :]) * w
                eb = e.astype(BF16)
                e_before = e_sum + _cs(eb, before)
                beta = jnp.exp(lb)
                dzb = jnp.where(strict, e - (e + e_before) * beta, 0.0).astype(BF16)
                dk_ref[pl.ds(ks, tk), :] += _dot_tn(dzb, qs)
                dv_ref[pl.ds(ks, tk), :] += _dot_tn(w.astype(BF16), dov)
                return p_sum + _rowsum(lom), e_sum + _rowsum(eb), dq + _dot(dzb, _only(sel, kv))

            zero = jnp.zeros((tq, 1), F32)
            out.append(lax.fori_loop(0, (i + 1) * (tq // tk), step, (zero, zero, jnp.zeros((tq, PAIR), F32)))[2])
        dq_ref[...] = ((out[0] + out[1]) * ATTN_SCALE).astype(dq_ref.dtype)

        @pl.when(i == S // tq - 1)
        def _():
            dk_out[...] = dk_ref[...].astype(dk_out.dtype)
            dv_out[...] = dv_ref[...].astype(dv_out.dtype)

    full = jax.ShapeDtypeStruct((S, D_MODEL), BF16)
    return pl.pallas_call(
        body, name=name, grid=(N_PAIRS, S // tq),
        in_specs=[qspec, kspec, vspec, vec, qspec], out_specs=[qspec, kvout, kvout],
        out_shape=[full, full, full],
        scratch_shapes=[pltpu.VMEM((S, PAIR), F32)] * 2,
        compiler_params=_cparams(("parallel", "arbitrary")),
    )(proj, proj, proj, tot, do)


def _fox_logits(qs, k, cq, ck, cmr, shift):
    causal = cmr <= shift
    return jnp.where(causal, _dot_nt(qs, k) + cq - ck, NEG_INF), causal


def fox_fwd(proj, cq, ck, *, name):
    S = proj.shape[0]
    tq, tk = _att_tiles(S)
    qspec, kspec, vspec, _, vec = _pair_specs(S, tq)
    ckspec = pl.BlockSpec((2, S // tk, 1, tk), lambda p, i: (p, 0, 0, 0))

    def body(q_ref, k_ref, v_ref, cq_ref, ck_ref, o_ref, lse_ref):
        i = pl.program_id(1)
        row, col = _blk_iota(tq, tk)
        cmr = col - row
        out = []
        for h in range(2):
            sel = _head_lanes(h)
            qs, cqv = _only(sel, q_ref[...] * ATTN_SCALE), cq_ref[h]

            def step(kb, carry):
                m, l, acc = carry
                ks = pl.multiple_of(kb * tk, tk)
                s, _ = _fox_logits(qs, k_ref[pl.ds(ks, tk), :], cqv, ck_ref[h, kb], cmr, i * tq - kb * tk)
                m_new = jnp.maximum(m, jnp.max(s, axis=1, keepdims=True))
                alpha = jnp.exp(m - m_new)
                p = jnp.exp(s - m_new)
                l = alpha * l + jnp.sum(p, axis=1, keepdims=True)
                acc = alpha * acc + _dot(p.astype(BF16), _only(sel, v_ref[pl.ds(ks, tk), :]))
                return m_new, l, acc

            m, l, acc = lax.fori_loop(0, (i + 1) * (tq // tk), step,
                                      (jnp.full((tq, 1), NEG_INF, F32), jnp.zeros((tq, 1), F32), jnp.zeros((tq, PAIR), F32)))
            lse_ref[h] = m + jnp.log(l)
            out.append(acc / l)
        o_ref[...] = (out[0] + out[1]).astype(o_ref.dtype)

    return pl.pallas_call(
        body, name=name, grid=(N_PAIRS, S // tq),
        in_specs=[qspec, kspec, vspec, vec, ckspec], out_specs=[qspec, vec],
        out_shape=[jax.ShapeDtypeStruct((S, D_MODEL), BF16), jax.ShapeDtypeStruct((N_HEADS, S, 1), F32)],
        compiler_params=_cparams(("parallel", "arbitrary")),
    )(proj, proj, proj, cq, ck)


def fox_bwd(proj, o, lse, cq, ck, do, *, name):
    S = proj.shape[0]
    tq, tk = _att_tiles(S)
    qspec, kspec, vspec, kvout, vec = _pair_specs(S, tq)
    ckspec = pl.BlockSpec((2, S // tk, 1, tk), lambda p, i: (p, 0, 0, 0))

    def body(q_ref, k_ref, v_ref, o_ref, lse_ref, cq_ref, ck_ref, do_ref, dq_ref, dk_out, dv_out, dcq_ref, dck_ref,
             dk_ref, dv_ref):
        i = pl.program_id(1)

        @pl.when(i == 0)
        def _():
            dk_ref[...] = jnp.zeros_like(dk_ref)
            dv_ref[...] = jnp.zeros_like(dv_ref)
            dck_ref[...] = jnp.zeros_like(dck_ref)

        row, col = _blk_iota(tq, tk)
        cmr = col - row
        out = []
        for h in range(2):
            sel = _head_lanes(h)
            qs, dov, cqv, lsev = _only(sel, q_ref[...] * ATTN_SCALE), _only(sel, do_ref[...]), cq_ref[h], lse_ref[h]
            delta = jnp.sum(dov.astype(F32) * o_ref[...].astype(F32), axis=1, keepdims=True)

            def step(kb, carry):
                dq, dcq = carry
                ks = pl.multiple_of(kb * tk, tk)
                kv = k_ref[pl.ds(ks, tk), :]
                s, causal = _fox_logits(qs, kv, cqv, ck_ref[h, kb], cmr, i * tq - kb * tk)
                p = jnp.where(causal, jnp.exp(s - lsev), 0.0)
                ds = p * (_dot_nt(dov, v_ref[pl.ds(ks, tk), :]) - delta)
                dck_ref[h, kb] += jnp.sum(ds, axis=0, keepdims=True)
                dsb = ds.astype(BF16)
                dk_ref[pl.ds(ks, tk), :] += _dot_tn(dsb, qs)
                dv_ref[pl.ds(ks, tk), :] += _dot_tn(p.astype(BF16), dov)
                return dq + _dot(dsb, _only(sel, kv)), dcq + jnp.sum(ds, axis=1, keepdims=True)

            dq, dcq = lax.fori_loop(0, (i + 1) * (tq // tk), step, (jnp.zeros((tq, PAIR), F32), jnp.zeros((tq, 1), F32)))
            dcq_ref[h] = dcq
            out.append(dq)
        dq_ref[...] = ((out[0] + out[1]) * ATTN_SCALE).astype(dq_ref.dtype)

        @pl.when(i == S // tq - 1)
        def _():
            dk_out[...] = dk_ref[...].astype(dk_out.dtype)
            dv_out[...] = dv_ref[...].astype(dv_out.dtype)

    full = jax.ShapeDtypeStruct((S, D_MODEL), BF16)
    return pl.pallas_call(
        body, name=name, grid=(N_PAIRS, S // tq),
        in_specs=[qspec, kspec, vspec, qspec, vec, vec, ckspec, qspec],
        out_specs=[qspec, kvout, kvout, vec, ckspec],
        out_shape=[full, full, full, jax.ShapeDtypeStruct((N_HEADS, S, 1), F32),
                   jax.ShapeDtypeStruct((N_HEADS, S // tk, 1, tk), F32)],
        scratch_shapes=[pltpu.VMEM((S, PAIR), F32)] * 2,
        compiler_params=_cparams(("parallel", "arbitrary")),
    )(proj, proj, proj, o, lse, cq, ck, do)


def _swa_specs(S, tq):
    qspec = pl.BlockSpec((None, SWA_GROUP, tq, HEAD_DIM), lambda g, i: (g, 0, i, 0))
    kvspec = pl.BlockSpec((None, S + SWA_WINDOW, HEAD_DIM), lambda g, i: (g, 0, 0))
    vec = pl.BlockSpec((None, SWA_GROUP, tq, 1), lambda g, i: (g, 0, i, 0))
    sink = pl.BlockSpec((None, SWA_GROUP * tq, 1), lambda g, i: (g, 0, 0))
    return qspec, kvspec, vec, sink


def _swa_logits(q2, kw, i, tq):
    rows = q2.shape[0]
    r = lax.broadcasted_iota(jnp.int32, (rows, 2 * tq), 0)
    c = lax.broadcasted_iota(jnp.int32, (rows, 2 * tq), 1)
    diff = (r & (tq - 1)) + tq - c
    ok = (diff >= 0) & (diff < SWA_WINDOW) & (c + (i - 1) * tq >= 0)
    return jnp.where(ok, _dot_nt(q2, kw) * ATTN_SCALE, NEG_INF), ok


def swa_fwd(q, kp, vp, sink, *, name):
    _, G, S, _ = q.shape
    tq = ATT_BLK
    qspec, kvspec, vec, sinkspec = _swa_specs(S, tq)

    def body(q_ref, k_ref, v_ref, s_ref, o_ref, lse_ref):
        i = pl.program_id(1)
        q2 = q_ref[...].reshape(G * tq, HEAD_DIM)
        ws = pl.multiple_of(i * tq, tq)
        logits, _ = _swa_logits(q2, k_ref[pl.ds(ws, 2 * tq), :], i, tq)
        sk = s_ref[...]
        m = jnp.maximum(jnp.max(logits, axis=1, keepdims=True), sk)
        e = jnp.exp(logits - m)
        den = jnp.sum(e, axis=1, keepdims=True) + jnp.exp(sk - m)
        o = _dot((e / den).astype(BF16), v_ref[pl.ds(ws, 2 * tq), :])
        o_ref[...] = o.reshape(G, tq, HEAD_DIM).astype(o_ref.dtype)
        lse_ref[...] = (m + jnp.log(den)).reshape(G, tq, 1)

    return pl.pallas_call(
        body, name=name, grid=(SWA_KV_HEADS, S // tq),
        in_specs=[qspec, kvspec, kvspec, sinkspec], out_specs=[qspec, vec],
        out_shape=[jax.ShapeDtypeStruct(q.shape, BF16), jax.ShapeDtypeStruct((SWA_KV_HEADS, G, S, 1), F32)],
        compiler_params=_cparams(("parallel", "arbitrary")),
    )(q, kp, vp, sink)


def swa_bwd(q, kp, vp, sink, o, lse, do, *, name):
    _, G, S, _ = q.shape
    tq = ATT_BLK
    qspec, kvspec, vec, sinkspec = _swa_specs(S, tq)

    def body(q_ref, k_ref, v_ref, s_ref, o_ref, lse_ref, do_ref, dq_ref, dk_ref, dv_ref, dsink_ref):
        i = pl.program_id(1)

        @pl.when(i == 0)
        def _():
            dk_ref[...] = jnp.zeros_like(dk_ref)
            dv_ref[...] = jnp.zeros_like(dv_ref)

        q2 = q_ref[...].reshape(G * tq, HEAD_DIM)
        do2 = do_ref[...].reshape(G * tq, HEAD_DIM)
        o2 = o_ref[...].reshape(G * tq, HEAD_DIM)
        lse2 = lse_ref[...].reshape(G * tq, 1)
        ws = pl.multiple_of(i * tq, tq)
        kw = k_ref[pl.ds(ws, 2 * tq), :]
        vw = v_ref[pl.ds(ws, 2 * tq), :]
        logits, ok = _swa_logits(q2, kw, i, tq)
        p = jnp.where(ok, jnp.exp(logits - lse2), 0.0)
        delta = jnp.sum(do2.astype(F32) * o2.astype(F32), axis=1, keepdims=True)
        ds = p * (_dot_nt(do2, vw) - delta)
        dsb = ds.astype(BF16)
        dq_ref[...] = (_dot(dsb, kw) * ATTN_SCALE).reshape(G, tq, HEAD_DIM)
        dk_ref[pl.ds(ws, 2 * tq), :] += _dot_tn(dsb, q2) * ATTN_SCALE
        dv_ref[pl.ds(ws, 2 * tq), :] += _dot_tn(p.astype(BF16), do2)
        dsink_ref[...] = (-jnp.exp(s_ref[...] - lse2) * delta).reshape(G, tq, 1)

    kvshape = jax.ShapeDtypeStruct(kp.shape, F32)
    return pl.pallas_call(
        body, name=name, grid=(SWA_KV_HEADS, S // tq),
        in_specs=[qspec, kvspec, kvspec, sinkspec, qspec, vec, qspec],
        out_specs=[qspec, kvspec, kvspec, vec],
        out_shape=[jax.ShapeDtypeStruct(q.shape, F32), kvshape, kvshape,
                   jax.ShapeDtypeStruct((SWA_KV_HEADS, G, S, 1), F32)],
        compiler_params=_cparams(("parallel", "arbitrary")),
    )(q, kp, vp, sink, o, lse, do)


def _adamw_tile(w, g, m, v):
    m = ADAM_B1 * m + (1.0 - ADAM_B1) * g
    v = ADAM_B2 * v + (1.0 - ADAM_B2) * (g * g)
    m_hat = m / (1.0 - ADAM_B1 ** ADAM_STEP)
    v_hat = v / (1.0 - ADAM_B2 ** ADAM_STEP)
    delta = -ADAM_LR * (m_hat / (jnp.sqrt(v_hat) + ADAM_EPS) + ADAM_WD * w)
    return g, delta, m, v


def adamw(gfull, t, w, m, v, after, *, name):
    off, K, ns, _ = t
    sb = off // K
    nat = pl.BlockSpec((K, SLAB), lambda q: (0, q))

    def body(g_ref, w_ref, m_ref, v_ref, after_ref, *outs):
        del after_ref
        for o, r in zip(outs, _adamw_tile(w_ref[...], g_ref[...], m_ref[...], v_ref[...])):
            o[...] = r

    return pl.pallas_call(
        body, name=name, grid=(ns,),
        in_specs=[pl.BlockSpec((K, SLAB), lambda q: (sb + q, 0)), nat, nat, nat, HBM],
        out_specs=[nat] * 4, out_shape=[jax.ShapeDtypeStruct(w.shape, F32)] * 4,
        compiler_params=_cparams(("parallel",)),
    )(gfull, w, m, v, after)


def adamw_small(g, w, m, v, *, name):
    def body(g_ref, w_ref, m_ref, v_ref, *outs):
        for o, r in zip(outs, _adamw_tile(w_ref[...], g_ref[...], m_ref[...], v_ref[...])):
            o[...] = r

    return pl.pallas_call(body, name=name, out_shape=[jax.ShapeDtypeStruct(w.shape, F32)] * 4)(g, w, m, v)


MESH = pl.DeviceIdType.MESH
HBM = pl.BlockSpec(memory_space=pl.ANY)


def _place():
    x, y, c = lax.axis_index("x"), lax.axis_index("y"), lax.axis_index("c")
    others = [(1 - x, y), (x, 1 - y), (1 - x, 1 - y)]
    return x, y, c, others


def _rcopy(src, dst, send_sems, recv_sems, k, to):
    return pltpu.make_async_remote_copy(src_ref=src, dst_ref=dst, send_sem=send_sems.at[k], recv_sem=recv_sems.at[k],
                                        device_id=to, device_id_type=MESH)


def _dma_sems(*counts):
    return [pltpu.SemaphoreType.DMA((n,)) for n in counts]


DMA_UNIT_ROWS = 128
DMA_PIECES = 4
DMA_PIECES_LOCAL = 8


def _row_pieces(h, n):
    units = h // DMA_UNIT_ROWS
    n = min(n, units)
    base, extra = divmod(units, n)
    sizes = [(base + (k < extra)) * DMA_UNIT_ROWS for k in range(n)]
    return [(sum(sizes[:k]), sizes[k]) for k in range(n)]


def _start_pieces(make, h, n):
    for s0, sz in _row_pieces(h, n):
        make(s0, sz).start()
    return make(0, h)


SEM = pl.BlockSpec(memory_space=pltpu.SEMAPHORE)
SPLIT_COPY = pltpu.CompilerParams(has_side_effects=pltpu.SideEffectType.DATAFLOW_SIDE_EFFECTING)
N_OTHERS = 3


def _hbm(a):
    return pltpu.with_memory_space_constraint(a, pltpu.HBM)


def _chip_rows(buf, chip, s0, sz):
    return buf.at[2 * chip[0] + chip[1], pl.ds(s0, sz)]


def allgather_start(bufs, *, name):
    n = len(bufs)

    def body(*refs):
        ins, send, recv, token = refs[:n], refs[n:2 * n], refs[2 * n:3 * n], refs[4 * n]
        x, y, c, others = _place()
        for i in range(n):
            h = bufs[i].shape[1] // 2
            for f, chip in enumerate(others):
                for s0, sz in _row_pieces(h, DMA_PIECES):
                    mine = _chip_rows(ins[i], (x, y), c * h + s0, sz)
                    _rcopy(mine, mine, send[i], recv[i], f, (*chip, c)).start()
        token[...] = jnp.zeros_like(token)

    res = pl.pallas_call(
        body, name=name, in_specs=[HBM] * n,
        out_specs=[SEM] * (2 * n) + [HBM] * n + [pl.BlockSpec(memory_space=pltpu.VMEM)],
        out_shape=[pltpu.SemaphoreType.DMA((N_OTHERS,))] * (2 * n) + [pltpu.HBM(b.shape, b.dtype) for b in bufs]
        + [jax.ShapeDtypeStruct((1, D_MODEL), F32)],
        input_output_aliases={i: 2 * n + i for i in range(n)},
        compiler_params=SPLIT_COPY,
    )(*[_hbm(b) for b in bufs])
    return res[:n], res[n:2 * n], res[2 * n:3 * n], res[3 * n]


def allgather_wait(buf, send, recv, after, *, name):
    h = buf.shape[1] // 2

    def body(buf_ref, send_sems, recv_sems, after_ref, out_ref):
        del after_ref, out_ref
        x, y, c, others = _place()
        for f, chip in enumerate(others):
            mine = _chip_rows(buf_ref, (x, y), c * h, h)
            theirs = _chip_rows(buf_ref, chip, c * h, h)
            cp = _rcopy(mine, theirs, send_sems, recv_sems, f, (*chip, c))
            cp.wait_send()
            cp.wait_recv()

    return pl.pallas_call(
        body, name=name, in_specs=[HBM, SEM, SEM, HBM], out_specs=HBM,
        out_shape=pltpu.HBM(buf.shape, buf.dtype), input_output_aliases={0: 0},
        compiler_params=SPLIT_COPY,
    )(buf, send, recv, after)


def allgather_forward(buf, *, name):
    h = buf.shape[1] // 2

    def body(in_ref, out_ref, send_sems, recv_sems):
        del in_ref
        x, y, c, others = _place()
        sibling = (x, y, 1 - c)
        sent = []
        for f, chip in enumerate(others):
            sent.append(_start_pieces(
                lambda s0, sz: _rcopy(_chip_rows(out_ref, chip, c * h + s0, sz), _chip_rows(out_ref, chip, c * h + s0, sz),
                                      send_sems, recv_sems, f, sibling), h, DMA_PIECES))
        for f, chip in enumerate(others):
            blk = _chip_rows(out_ref, chip, (1 - c) * h, h)
            _rcopy(blk, blk, send_sems, recv_sems, f, sibling).wait_recv()
        for cp in sent:
            cp.wait_send()

    return pl.pallas_call(
        body, name=name, in_specs=[HBM], out_specs=HBM,
        out_shape=jax.ShapeDtypeStruct(buf.shape, buf.dtype), input_output_aliases={0: 0},
        scratch_shapes=_dma_sems(N_OTHERS, N_OTHERS),
    )(buf)


def swap_halves(grads, *, name):
    n = len(grads)

    def body(*refs):
        ins, theirs = refs[:n], refs[n:2 * n]
        send_sems, recv_sems = refs[2 * n:]
        x, y, c, _ = _place()
        for i in range(n):
            h = grads[i].shape[1] // 2
            for k in range(N_CHIPS):
                for s0, sz in _row_pieces(h, DMA_PIECES):
                    _rcopy(ins[i].at[k, pl.ds((1 - c) * h + s0, sz)], theirs[i].at[k, pl.ds(s0, sz)],
                           send_sems, recv_sems, i, (x, y, 1 - c)).start()
        for i in range(n):
            h = grads[i].shape[1] // 2
            _rcopy(ins[i].at[:, pl.ds((1 - c) * h, h)], theirs[i], send_sems, recv_sems, i, (x, y, 1 - c)).wait()

    return pl.pallas_call(
        body, name=name, in_specs=[HBM] * n, out_specs=[HBM] * n,
        out_shape=[jax.ShapeDtypeStruct((N_CHIPS, g.shape[1] // 2, SLAB), g.dtype) for g in grads],
        scratch_shapes=_dma_sems(n, n))(*grads)


def scatter_start(part, *, name):
    h = part.shape[1]

    def body(part_ref, land_ref, send, recv, part_out, land_out, token):
        del part_out, land_out
        x, y, c, others = _place()
        for f, chip in enumerate(others):
            for s0, sz in _row_pieces(h, DMA_PIECES):
                _rcopy(_chip_rows(part_ref, chip, s0, sz), land_ref.at[f, pl.ds(s0, sz)], send, recv, f, (*chip, c)).start()
        token[...] = jnp.zeros_like(token)

    land = lax.empty((N_OTHERS,) + part.shape[1:], part.dtype)
    return pl.pallas_call(
        body, name=name, in_specs=[HBM, HBM],
        out_specs=[SEM, SEM, HBM, HBM, pl.BlockSpec(memory_space=pltpu.VMEM)],
        out_shape=[pltpu.SemaphoreType.DMA((N_OTHERS,))] * 2 + [pltpu.HBM(part.shape, part.dtype), pltpu.HBM(land.shape, land.dtype),
                                                                 jax.ShapeDtypeStruct((1, D_MODEL), F32)],
        input_output_aliases={0: 2, 1: 3},
        compiler_params=SPLIT_COPY,
    )(_hbm(part), _hbm(land))


def scatter_wait(part, land, send, recv, after, *, name):
    h = part.shape[1]

    def body(part_ref, land_ref, send_sems, recv_sems, after_ref, part_out, land_out):
        del after_ref, part_out, land_out
        x, y, c, others = _place()
        for f, chip in enumerate(others):
            cp = _rcopy(_chip_rows(part_ref, chip, 0, h), land_ref.at[f], send_sems, recv_sems, f, (*chip, c))
            cp.wait_send()
            cp.wait_recv()

    return pl.pallas_call(
        body, name=name, in_specs=[HBM, HBM, SEM, SEM, HBM], out_specs=[HBM, HBM],
        out_shape=[pltpu.HBM(part.shape, part.dtype), pltpu.HBM(land.shape, land.dtype)],
        input_output_aliases={0: 0, 1: 1},
        compiler_params=SPLIT_COPY,
    )(part, land, send, recv, after)


def join_halves(bufs, *, name):
    n = len(bufs)

    def body(*refs):
        outs = refs[n:2 * n]
        send_sems, recv_sems = refs[2 * n:]
        x, y, c, _ = _place()
        sibling = (x, y, 1 - c)
        cps = []
        for i in range(n):
            h = bufs[i].shape[0] // 2
            snd = _start_pieces(
                lambda s0, sz: _rcopy(outs[i].at[pl.ds(c * h + s0, sz)], outs[i].at[pl.ds(c * h + s0, sz)],
                                      send_sems, recv_sems, i, sibling), h, 2 * DMA_PIECES_LOCAL)
            theirs = outs[i].at[pl.ds((1 - c) * h, h)]
            cps.append((snd, _rcopy(theirs, theirs, send_sems, recv_sems, i, sibling)))
        for snd, rcv in cps:
            snd.wait_send()
            rcv.wait_recv()

    return pl.pallas_call(
        body, name=name, in_specs=[HBM] * n, out_specs=[HBM] * n,
        out_shape=[jax.ShapeDtypeStruct(b.shape, b.dtype) for b in bufs],
        input_output_aliases={i: i for i in range(n)},
        scratch_shapes=_dma_sems(n, n),
    )(*bufs)


def allreduce_small(v, *, name):
    rows, n = v.shape

    def body(x_ref, sum_ref, all_ref, send_sems, recv_sems, local_sem):
        x, y, c, others = _place()
        me, sibling = (x, y, c), (x, y, 1 - c)

        def blk(px, py, pc):
            return all_ref.at[pl.ds((4 * px + 2 * py + pc) * rows, rows), :]

        def copy(k, block, to, src=None):
            return _rcopy(blk(*block) if src is None else src, blk(*block), send_sems, recv_sems, k, to)

        mine = pltpu.make_async_copy(x_ref, blk(*me), local_sem)
        mine.start()
        first = [copy(0, me, sibling, src=x_ref)]
        first += [copy(1 + f, me, (*chip, c), src=x_ref) for f, chip in enumerate(others)]
        for cp in first:
            cp.start()
        passed = [copy(4 + f, (*chip, c), sibling) for f, chip in enumerate(others)]
        for f, chip in enumerate(others):
            copy(1 + f, (*chip, c), me).wait_recv()
            passed[f].start()
        copy(0, sibling, me).wait_recv()
        for f, chip in enumerate(others):
            copy(4 + f, (*chip, 1 - c), me).wait_recv()
        for cp in first + passed:
            cp.wait_send()
        mine.wait()
        acc = all_ref[pl.ds(0, rows), :]
        for d in range(1, N_DEVICES):
            acc = acc + all_ref[pl.ds(d * rows, rows), :]
        sum_ref[...] = acc

    vm = pl.BlockSpec(memory_space=pltpu.VMEM)
    return pl.pallas_call(
        body, name=name, in_specs=[vm], out_specs=[vm, vm],
        out_shape=[jax.ShapeDtypeStruct((rows, n), F32), jax.ShapeDtypeStruct((N_DEVICES * rows, n), F32)],
        scratch_shapes=_dma_sems(7, 7) + [pltpu.SemaphoreType.DMA],
    )(v)[0]


def add_pairs(grad, theirs, where, *, name):
    h = theirs.shape[1]
    spec = pl.BlockSpec((None, h, SLAB), lambda k, w: (k, 0, 0))

    def body(w_ref, a_ref, b_ref, o_ref):
        del w_ref
        o_ref[...] = (a_ref[...].astype(F32) + b_ref[...].astype(F32)).astype(o_ref.dtype)

    return pl.pallas_call(
        body, name=name,
        grid_spec=pltpu.PrefetchScalarGridSpec(
            num_scalar_prefetch=1, grid=(N_CHIPS,),
            in_specs=[pl.BlockSpec((None, h, SLAB), lambda k, w: (k, w[1], 0)), spec], out_specs=spec),
        out_shape=jax.ShapeDtypeStruct(theirs.shape, theirs.dtype),
        compiler_params=_cparams(("parallel",)))(where, grad, theirs)


def add_chips(pair, got, where, *, name):
    h = pair.shape[1]
    tr = h // 2

    def body(w_ref, a_ref, b_ref, o_ref):
        del w_ref
        acc = a_ref[...].astype(F32)
        for f in range(3):
            acc = acc + b_ref[f].astype(F32)
        o_ref[...] = acc

    return pl.pallas_call(
        body, name=name,
        grid_spec=pltpu.PrefetchScalarGridSpec(
            num_scalar_prefetch=1, grid=(2,),
            in_specs=[pl.BlockSpec((None, tr, SLAB), lambda i, w: (w[0], i, 0)),
                      pl.BlockSpec((3, tr, SLAB), lambda i, w: (0, i, 0))],
            out_specs=pl.BlockSpec((tr, SLAB), lambda i, w: (2 * w[1] + i, 0))),
        out_shape=jax.ShapeDtypeStruct((2 * h, SLAB), F32),
        compiler_params=_cparams(("parallel",)))(where, pair, got)


DEPTH = 4
MIXER = (0, 1, 2, 0)
W_IN_COLS = (768, 320, 772)
W_IN_PAD = (768, 512, 1024)
MATS = ("up", "down", "inp", "out", "gate", "proj")
MAT_ARG = dict(up="w_up", down="w_down", inp="w_in", out="w_out", gate="w_ple_gate", proj="w_ple_proj")
GAINS = ("attn_norm", "mlp_norm", "ple_norm")
N_SMALL = 16
KINDS = ("grad_", "delta_", "new_m_", "new_v_")


def _layout(kind):
    ns_in = W_IN_PAD[kind] // SLAB
    off = 8192 + 1024 * ns_in
    lay = dict(up=(0, 1024, 4, False), down=(4096, 1024, 4, True), inp=(8192, 1024, ns_in, False),
               out=(off, 256, 4, True), gate=(off + 1024, 256, 4, True), proj=(off + 2048, 256, 1, False))
    return lay, off + 2304


def _to_slabs(w):
    k, c = w.shape
    return w.reshape(k, c // SLAB, SLAB).transpose(1, 0, 2).reshape(-1, SLAB)


def _pad_cols(w, n):
    return jnp.pad(w, ((0, 0), (0, n - w.shape[1])))


def _heads(x2d, n):
    return x2d.reshape(x2d.shape[0], n, HEAD_DIM).transpose(1, 0, 2)


def _unheads(x3d):
    n, s, _ = x3d.shape
    return x3d.transpose(1, 0, 2).reshape(s, n * HEAD_DIM)


def _chip_cols(x2d, c, cpad):
    return jnp.concatenate([_pad_cols(x2d[:, j * c:(j + 1) * c], cpad) for j in range(N_CHIPS)], axis=1)


def _unchip_cols(x2d, c, cpad):
    return jnp.concatenate([x2d[:, j * cpad:j * cpad + c] for j in range(N_CHIPS)], axis=1)


def _forget_cols(wg, t):
    off, K, _, _ = t
    cols = []
    for g in range(3 * N_HEADS * HEAD_DIM, 3 * N_HEADS * HEAD_DIM + N_HEADS):
        chip, local = divmod(g, W_IN_COLS[2])
        q, c = divmod(local, SLAB)
        cols.append(wg[chip, off + q * K:off + (q + 1) * K, c:c + 1])
    return jnp.concatenate(cols, axis=1)


def _add_res(acc, res):
    return (acc + res,)


def _relu2(acc):
    return acc, jnp.square(jnp.maximum(acc, 0.0))


def _relu2_bwd(acc, u):
    return (acc * (2.0 * jnp.maximum(u.astype(F32), 0.0)),)


def _ple_fwd(acc, x2, pp):
    return x2 + pp * _sigmoid(acc), acc


def _ple_bwd(dx, pp, gl):
    gate = _sigmoid(gl)
    return dx * gate, dx * pp * gate * (1.0 - gate)


def _layer_fwd(i, kind, x0, p_bf, wg, lay, gains, extra, tabs):
    s = x0.shape[0]
    an, mn, pn = gains
    sv = dict(x0=x0)
    if kind == 0:
        proj, h1 = mm_nn(x0, wg, lay["inp"], name=f"w_in_{i}", norm_gain=an)
        a, tot = sb_fwd(proj, name=f"sb_fwd_{i}")
        sv.update(proj=proj, tot=tot)
    elif kind == 1:
        projp, h1 = mm_nn(x0, wg, lay["inp"], name=f"w_in_{i}", out_dtypes=(F32,), norm_gain=an)
        proj = _unchip_cols(projp, W_IN_COLS[1], W_IN_PAD[1])
        nq = N_HEADS * HEAD_DIM
        nqk = nq + SWA_KV_HEADS * HEAD_DIM
        qk = rope_fwd(proj[:, :nqk], tabs, name=f"rope_{i}")
        q = _heads(qk[:, :nq], N_HEADS).reshape(SWA_KV_HEADS, SWA_GROUP, s, HEAD_DIM)
        front = ((0, 0), (SWA_WINDOW, 0), (0, 0))
        kp = jnp.pad(_heads(qk[:, nq:], SWA_KV_HEADS), front)
        vp = jnp.pad(_heads(proj[:, nqk:].astype(BF16), SWA_KV_HEADS), front)
        sink = jnp.repeat(extra.reshape(SWA_KV_HEADS, SWA_GROUP), ATT_BLK, axis=1)[:, :, None]
        o4, lse = swa_fwd(q, kp, vp, sink, name=f"swa_fwd_{i}")
        a = _unheads(o4.reshape(N_HEADS, s, HEAD_DIM))
        sv.update(q=q, kp=kp, vp=vp, sink=sink, o4=o4, lse=lse)
    else:
        projp, h1 = mm_nn(x0, wg, lay["inp"], name=f"w_in_{i}", norm_gain=an)
        nqkv = 3 * N_HEADS * HEAD_DIM
        proj = _unchip_cols(projp, W_IN_COLS[2], W_IN_PAD[2])[:, :nqkv]
        fl = mm_plain(h1, _pad_cols(_forget_cols(wg, lay["inp"]), 128), name=f"w_forget_{i}")
        bp = _pad_cols(extra[None], 128)
        cum_t = fox_gate_fwd(fl, bp, name=f"gate_fwd_{i}")[:, :N_HEADS].T
        cq = cum_t[:, :, None]
        ck = cum_t.reshape(N_HEADS, s // min(ATT_BK, s), 1, min(ATT_BK, s))
        a, lse = fox_fwd(proj, cq, ck, name=f"fox_fwd_{i}")
        sv.update(proj=proj, fl=fl, bp=bp, cq=cq, ck=ck, lse=lse)
    x1 = mm_nn(a, wg, lay["out"], name=f"w_out_{i}", epi=_add_res, extras=(x0,), out_dtypes=(F32,))[0]
    u, r, h2 = mm_nn(x1, wg, lay["up"], name=f"w_up_{i}", epi=_relu2, out_dtypes=(BF16, BF16), norm_gain=mn)
    x2 = mm_nn(r, wg, lay["down"], name=f"w_down_{i}", epi=_add_res, extras=(x1,), out_dtypes=(F32,))[0]
    pp = mm_nn(p_bf, wg, lay["proj"], name=f"w_ple_proj_{i}", out_dtypes=(F32,))[0]
    x3, gl, h3 = mm_nn(x2, wg, lay["gate"], name=f"w_ple_gate_{i}", epi=_ple_fwd, extras=(x2, pp), out_dtypes=(F32, F32),
                       norm_gain=pn)
    sv.update(h1=h1, a=a, x1=x1, h2=h2, u=u, r=r, x2=x2, h3=h3, pp=pp, gl=gl)
    return x3, sv


def _layer_bwd(i, kind, dx3, sv, p_bf, wg, lay, n_rows, gains, tabs):
    s = dx3.shape[0]
    an, mn, pn = gains
    g = lax.empty((N_CHIPS, n_rows, SLAB), BF16)
    d_pp, d_gl = ew(_ple_bwd, [dx3, sv["pp"], sv["gl"]], [BF16, BF16], name=f"ple_bwd_{i}")
    g = mm_tn(p_bf, d_pp, g, lay["proj"], name=f"dw_ple_proj_{i}")
    g = mm_tn(sv["h3"], d_gl, g, lay["gate"], name=f"dw_ple_gate_{i}")
    d_h3 = mm_nt(d_gl, wg, lay["gate"], name=f"dx_ple_gate_{i}", out_dtypes=(F32,))[0]
    dx2, dx2b, d_pn = rms_bwd(sv["x2"], pn, d_h3, dx3, name=f"ple_norm_bwd_{i}")
    g = mm_tn(sv["r"], dx2b, g, lay["down"], name=f"dw_down_{i}")
    d_u = mm_nt(dx2b, wg, lay["down"], name=f"dx_down_{i}", epi=_relu2_bwd, extras=(sv["u"],))[0]
    g = mm_tn(sv["h2"], d_u, g, lay["up"], name=f"dw_up_{i}")
    dx1, dx1b, d_mn = mm_nt(d_u, wg, lay["up"], name=f"dx_up_{i}", rms=(sv["x1"], mn, dx2))
    g = mm_tn(sv["a"], dx1b, g, lay["out"], name=f"dw_out_{i}")
    d_a = mm_nt(dx1b, wg, lay["out"], name=f"dx_out_{i}")[0]
    d_extra = None
    if kind == 0:
        d_proj = jnp.concatenate(sb_bwd(sv["proj"], sv["tot"], d_a, name=f"sb_bwd_{i}"), axis=1)
    elif kind == 1:
        do4 = _heads(d_a, N_HEADS).reshape(SWA_KV_HEADS, SWA_GROUP, s, HEAD_DIM)
        dq, dkp, dvp, dsr = swa_bwd(sv["q"], sv["kp"], sv["vp"], sv["sink"], sv["o4"], sv["lse"], do4, name=f"swa_bwd_{i}")
        dqk = jnp.concatenate([_unheads(dq.reshape(N_HEADS, s, HEAD_DIM)), _unheads(dkp[:, SWA_WINDOW:])], axis=1)
        dqk = rope_bwd(dqk, tabs, name=f"rope_bwd_{i}")
        d_proj = jnp.concatenate([dqk, _unheads(dvp[:, SWA_WINDOW:]).astype(BF16)], axis=1)
        d_proj = _chip_cols(d_proj, W_IN_COLS[1], W_IN_PAD[1])
        d_extra = jnp.sum(dsr[..., 0], axis=2).reshape(N_HEADS)
    else:
        dq, dk, dv, dcq, dck = fox_bwd(sv["proj"], sv["a"], sv["lse"], sv["cq"], sv["ck"], d_a, name=f"fox_bwd_{i}")
        dcum = _pad_cols((dcq[:, :, 0] - dck.reshape(N_HEADS, s)).T, 128)
        dfl, dbp = fox_gate_bwd(dcum, sv["fl"], sv["bp"], name=f"gate_bwd_{i}")
        d_proj = jnp.concatenate([dq, dk, dv, dfl[:, :N_HEADS].astype(BF16)], axis=1)
        d_proj = _chip_cols(d_proj, W_IN_COLS[2], W_IN_PAD[2])
        d_extra = dbp[0, :N_HEADS]
    g = mm_tn(sv["h1"], d_proj, g, lay["inp"], name=f"dw_in_{i}")
    dx0, _, d_an = mm_nt(d_proj, wg, lay["inp"], name=f"dx_in_{i}", rms=(sv["x0"], an, dx1))
    return dx0, g, (d_an, d_mn, d_pn), d_extra


def _small_rows(a, prefix):
    rows = [a[f"{prefix}{n}_{i}"] for i in range(DEPTH) for n in GAINS] + [a[f"{prefix}final_norm"]]
    rows += [_pad_cols(a[f"{prefix}{n}"][None], D_MODEL)[0] for n in ("sinks_1", "b_forget_2")]
    return jnp.stack(rows + [jnp.zeros((D_MODEL,), F32)])


def _train_step(a):
    x = a["x"][0]
    tabs = rope_tables(x.shape[0])
    lays = [_layout(k) for k in MIXER]

    def natural(prefix, i, m):
        w = a[f"{prefix}{MAT_ARG[m]}_{i}"]
        return _pad_cols(w, W_IN_PAD[MIXER[i]]) if m == "inp" else w

    chip = 2 * lax.axis_index("x") + lax.axis_index("y")
    where = jnp.stack([chip, lax.axis_index("c")]).astype(jnp.int32)
    def own_block(i, zero):
        pk = jnp.concatenate([_to_slabs((natural("", i, m) + zero).astype(BF16)) for m in MATS], axis=0)
        return lax.dynamic_update_slice(lax.empty((N_CHIPS,) + pk.shape, BF16), pk[None], (chip, 0, 0))

    sends, recvs, bufs, token = allgather_start([own_block(0, 0.0)], name="allgather_start_0")
    more = allgather_start([own_block(i, token[0, 0]) for i in range(1, DEPTH)], name="allgather_start_1")
    sends, recvs, bufs, token = sends + more[0], recvs + more[1], bufs + more[2], more[3]

    gains = [tuple(a[f"{n}_{i}"][None] for n in GAINS) for i in range(DEPTH)]
    extras = [None, a["sinks_1"], a["b_forget_2"], None]
    p_bf = [a["p"][i, 0].astype(BF16) for i in range(DEPTH)]

    saved, wgs, after = [], [], token
    for i in range(DEPTH):
        landed = allgather_wait(bufs[i], sends[i], recvs[i], after, name=f"allgather_wait_{i}")
        wgs.append(allgather_forward(landed, name=f"allgather_forward_{i}"))
        x, sv = _layer_fwd(i, MIXER[i], x, p_bf[i], wgs[i], lays[i][0], gains[i], extras[i], tabs)
        saved.append(sv)
        after = x
    dx, d_final, loss = loss_head(x, a["final_norm"][None], a["loss_target"][0], name="loss_head")

    def finish(i, started, after):
        send, recv, part, land, _ = started
        part, got = scatter_wait(part, land, send, recv, after, name=f"scatter_wait_{i}")
        return join_halves([add_chips(part, got, where, name=f"add_chips_{i}")], name=f"join_halves_{i}")[0]

    small = [None] * N_SMALL
    small[12] = d_final[0]
    small[15] = _pad_cols(loss[:, :1], D_MODEL)[0]
    gfull = [None] * DEPTH
    started = None
    for i in reversed(range(DEPTH)):
        an, mn, pn = gains[i]
        if started is not None:
            pn = pn + started[4]
        dx, grad, d_gains, d_extra = _layer_bwd(i, MIXER[i], dx, saved[i], p_bf[i], wgs[i], lays[i][0], lays[i][1],
                                                (an, mn, pn), tabs)
        for j in range(3):
            small[3 * i + j] = d_gains[j][0]
        if d_extra is not None:
            small[12 + MIXER[i]] = _pad_cols(d_extra[None], D_MODEL)[0]
        if started is not None:
            gfull[i + 1] = finish(i + 1, started, dx)
        theirs = swap_halves([grad], name=f"swap_halves_{i}")[0]
        started = scatter_start(add_pairs(grad, theirs, where, name=f"add_pairs_{i}"), name=f"scatter_start_{i}")
    small = allreduce_small(jnp.stack(small), name="allreduce_small")

    out = {"loss": small[15, 0], "grad_x": dx[None]}
    res = adamw_small(small, _small_rows(a, ""), _small_rows(a, "m_"), _small_rows(a, "v_"), name="adamw_small")
    for i in reversed(range(DEPTH)):
        if i == 0:
            gfull[0] = finish(0, started, out[f"delta_{MAT_ARG[MATS[-1]]}_1"])
        for m in MATS:
            upd = adamw(gfull[i], lays[i][0][m], natural("", i, m), natural("m_", i, m), natural("v_", i, m), started[4],
                        name=f"adamw_{MAT_ARG[m]}_{i}")
            cols = a[f"{MAT_ARG[m]}_{i}"].shape[1]
            for kd, r in zip(KINDS, upd):
                out[f"{kd}{MAT_ARG[m]}_{i}"] = r[:, :cols]
    for kd, r in zip(KINDS, res):
        for i in range(DEPTH):
            for j, n in enumerate(GAINS):
                out[f"{kd}{n}_{i}"] = r[3 * i + j]
        out[f"{kd}final_norm"] = r[12]
        out[f"{kd}sinks_1"] = r[13, :N_HEADS]
        out[f"{kd}b_forget_2"] = r[14, :N_HEADS]
    return out


def _weight_names():
    names = []
    for i in range(DEPTH):
        names += [f"attn_norm_{i}", f"w_in_{i}", f"w_out_{i}"] + [[], ["sinks_1"], ["b_forget_2"]][MIXER[i]]
        names += [f"mlp_norm_{i}", f"w_up_{i}", f"w_down_{i}", f"ple_norm_{i}", f"w_ple_gate_{i}", f"w_ple_proj_{i}"]
    return names + ["final_norm"]


def kernel(x, p, attn_norm_0, w_in_0, w_out_0, mlp_norm_0, w_up_0, w_down_0, ple_norm_0, w_ple_gate_0, w_ple_proj_0, attn_norm_1, w_in_1, w_out_1, sinks_1, mlp_norm_1, w_up_1, w_down_1, ple_norm_1, w_ple_gate_1, w_ple_proj_1, attn_norm_2, w_in_2, w_out_2, b_forget_2, mlp_norm_2, w_up_2, w_down_2, ple_norm_2, w_ple_gate_2, w_ple_proj_2, attn_norm_3, w_in_3, w_out_3, mlp_norm_3, w_up_3, w_down_3, ple_norm_3, w_ple_gate_3, w_ple_proj_3, final_norm, loss_target, m_attn_norm_0, m_w_in_0, m_w_out_0, m_mlp_norm_0, m_w_up_0, m_w_down_0, m_ple_norm_0, m_w_ple_gate_0, m_w_ple_proj_0, m_attn_norm_1, m_w_in_1, m_w_out_1, m_sinks_1, m_mlp_norm_1, m_w_up_1, m_w_down_1, m_ple_norm_1, m_w_ple_gate_1, m_w_ple_proj_1, m_attn_norm_2, m_w_in_2, m_w_out_2, m_b_forget_2, m_mlp_norm_2, m_w_up_2, m_w_down_2, m_ple_norm_2, m_w_ple_gate_2, m_w_ple_proj_2, m_attn_norm_3, m_w_in_3, m_w_out_3, m_mlp_norm_3, m_w_up_3, m_w_down_3, m_ple_norm_3, m_w_ple_gate_3, m_w_ple_proj_3, m_final_norm, v_attn_norm_0, v_w_in_0, v_w_out_0, v_mlp_norm_0, v_w_up_0, v_w_down_0, v_ple_norm_0, v_w_ple_gate_0, v_w_ple_proj_0, v_attn_norm_1, v_w_in_1, v_w_out_1, v_sinks_1, v_mlp_norm_1, v_w_up_1, v_w_down_1, v_ple_norm_1, v_w_ple_gate_1, v_w_ple_proj_1, v_attn_norm_2, v_w_in_2, v_w_out_2, v_b_forget_2, v_mlp_norm_2, v_w_up_2, v_w_down_2, v_ple_norm_2, v_w_ple_gate_2, v_w_ple_proj_2, v_attn_norm_3, v_w_in_3, v_w_out_3, v_mlp_norm_3, v_w_up_3, v_w_down_3, v_ple_norm_3, v_w_ple_gate_3, v_w_ple_proj_3, v_final_norm):
    out = _train_step(dict(locals()))
    return (out["loss"], out["grad_x"], *[out[kd + n] for kd in KINDS for n in _weight_names()])
```

```python
import jax
import jax.numpy as jnp
from jax import lax
from jax.experimental import pallas as pl
from jax.experimental.pallas import tpu as pltpu

F32 = jnp.float32
BF16 = jnp.bfloat16

D_MODEL = 1024
N_HEADS = 16
HEAD_DIM = 64
SWA_KV_HEADS = 2
SWA_GROUP = 8
SWA_WINDOW = 128
ROPE_THETA = 500000.0
ROPE_DIM = 16
RMS_EPS = 1e-6
NEG_INF = -1e30
ATTN_SCALE = HEAD_DIM ** -0.5
N_CHIPS = 4
N_DEVICES = 8

SLAB = 256
ATT_BLK = 128
ATT_BQ = 512
ATT_BK = 512
ROW_TILE = 256
V7X_VMEM_LIMIT = 56 * 1024 * 1024

ADAM_LR, ADAM_B1, ADAM_B2, ADAM_EPS, ADAM_WD, ADAM_STEP = 0.001, 0.9, 0.999, 1e-08, 0.01, 10


def _cparams(sem=None):
    return pltpu.CompilerParams(dimension_semantics=sem, vmem_limit_bytes=V7X_VMEM_LIMIT)


def _dot(a, b):
    return jnp.dot(a, b, preferred_element_type=F32)


def _dot_nt(a, b):
    return lax.dot_general(a, b, (((1,), (1,)), ((), ())), preferred_element_type=F32)


def _dot_tn(a, b):
    return lax.dot_general(a, b, (((0,), (0,)), ((), ())), preferred_element_type=F32)


def _row_tile(M, K):
    return min(M, 1024) if K >= 1024 else M


def _finish(epi, acc, ex, outs):
    res = epi(acc, *[e[...] for e in ex]) if epi is not None else (acc,)
    for o, r in zip(outs, res):
        o[...] = r.astype(o.dtype)


def _once(shape, index_map):
    return pl.BlockSpec(shape, index_map, pipeline_mode=pl.Buffered(1))


def mm_nn(a, wg, t, *, name, epi=None, extras=(), out_dtypes=(BF16,), norm_gain=None):
    off, K, ns, row = t
    M = a.shape[0]
    sb = off // K
    ne, no = len(extras), len(out_dtypes)
    norm = norm_gain is not None
    if row:
        tm = M if norm else _row_tile(M, K)
        nb = N_CHIPS
        grid = (M // tm, ns)
        a_shape = (tm, N_CHIPS * K)
        a_spec = (_once if norm else pl.BlockSpec)(a_shape, lambda i, q: (i, 0))
        b_specs = [pl.BlockSpec((None, K, SLAB), lambda i, q, j=j: (j, sb + q, 0)) for j in range(nb)]
        tile = pl.BlockSpec((tm, SLAB), lambda i, q: (i, q))
        n_out = ns * SLAB
    else:
        nb = ns
        grid = (N_CHIPS,)
        a_shape = (M, K)
        a_spec = (_once if norm else pl.BlockSpec)(a_shape, lambda j: (0, 0))
        b_specs = [pl.BlockSpec((None, K, SLAB), lambda j, q=q: (j, sb + q, 0)) for q in range(ns)]
        tile = pl.BlockSpec((M, ns * SLAB), lambda j: (0, j))
        n_out = N_CHIPS * ns * SLAB

    def body(a_ref, *rest):
        if norm:
            g_ref, rest, h_out, h_ref = rest[0], rest[1:-2], rest[-2], rest[-1]

            @pl.when(pl.program_id(len(grid) - 1) == 0)
            def _():
                xv = a_ref[...]
                h_ref[...] = (xv * _rstd(xv) * g_ref[...]).astype(BF16)
                h_out[...] = h_ref[...]

            a_ref = h_ref
        bs, ex, outs = rest[:nb], rest[nb:nb + ne], rest[nb + ne:]
        if row:
            acc = _dot(a_ref[:, pl.ds(0, K)], bs[0][...])
            for j in range(1, nb):
                acc = acc + _dot(a_ref[:, pl.ds(j * K, K)], bs[j][...])
            _finish(epi, acc, ex, outs)
        else:
            av = a_ref[...]
            for q in range(ns):
                cols = pl.ds(q * SLAB, SLAB)
                _finish(epi, _dot(av, bs[q][...]), [e.at[:, cols] for e in ex], [o.at[:, cols] for o in outs])

    h_spec = _once(a_shape, (lambda i, q: (i, 0)) if row else (lambda j: (0, 0)))
    return pl.pallas_call(
        body, name=name, grid=grid,
        in_specs=[a_spec] + ([pl.BlockSpec(norm_gain.shape, lambda *_: (0, 0))] if norm else []) + b_specs + [tile] * ne,
        out_specs=[tile] * no + ([h_spec] if norm else []),
        out_shape=[jax.ShapeDtypeStruct((M, n_out), d) for d in out_dtypes]
        + ([jax.ShapeDtypeStruct(a.shape, BF16)] if norm else []),
        scratch_shapes=[pltpu.VMEM(a_shape, BF16)] if norm else [],
        compiler_params=_cparams((("arbitrary" if norm else "parallel"),) * len(grid)),
    )(a, *([norm_gain] if norm else []), *([wg] * nb), *extras)


def mm_nt(dy, wg, t, *, name, epi=None, extras=(), out_dtypes=(BF16,), rms=None):
    off, K, ns, row = t
    M = dy.shape[0]
    tm = _row_tile(M, K)
    sb = off // K
    if rms is not None:
        x, gain, dres = rms
        extras, out_dtypes = (x, dres), (F32, BF16)
    ne, no = len(extras), len(out_dtypes)
    grid = (M // tm, N_CHIPS)
    b_specs = [pl.BlockSpec((None, K, SLAB), lambda i, j, q=q: (j, sb + q, 0)) for q in range(ns)]
    if row:
        dy_spec = pl.BlockSpec((tm, ns * SLAB), lambda i, j: (i, 0))
        tile = pl.BlockSpec((tm, K), lambda i, j: (i, j))
        n_out = N_CHIPS * K
        sem = ("parallel", "parallel")
    else:
        dy_spec = pl.BlockSpec((tm, ns * SLAB), lambda i, j: (i, j))
        tile = pl.BlockSpec((tm, K), lambda i, j: (i, 0))
        n_out = K
        sem = ("arbitrary" if rms is not None else "parallel", "arbitrary")
    one = pl.BlockSpec((1, K), lambda i, j: (0, 0))

    def body(dy_ref, *rest):
        if rms is not None:
            g_ref, rest, dg_ref, acc_ref = rest[0], rest[1:-2], rest[-2], rest[-1]
            rest = rest + (acc_ref,)
        bs, ex, outs = rest[:ns], rest[ns:ns + ne], rest[ns + ne:ns + ne + no]
        part = _dot_nt(dy_ref[:, pl.ds(0, SLAB)], bs[0][...])
        for q in range(1, ns):
            part = part + _dot_nt(dy_ref[:, pl.ds(q * SLAB, SLAB)], bs[q][...])
        if row:
            _finish(epi, part, ex, outs)
        else:
            acc_ref = rest[-1]
            i, j = pl.program_id(0), pl.program_id(1)

            @pl.when(j == 0)
            def _():
                acc_ref[...] = part

            @pl.when(j > 0)
            def _():
                acc_ref[...] += part

            @pl.when(j == N_CHIPS - 1)
            def _():
                if rms is None:
                    _finish(epi, acc_ref[...], ex, outs)
                else:
                    dx, dg = _rms_bwd_tile(ex[0][...], g_ref[...], acc_ref[...])
                    dx = dx + ex[1][...]
                    outs[0][...] = dx
                    outs[1][...] = dx.astype(BF16)

                    @pl.when(i == 0)
                    def _():
                        dg_ref[...] = dg

                    @pl.when(i > 0)
                    def _():
                        dg_ref[...] += dg

    has = rms is not None
    return pl.pallas_call(
        body, name=name, grid=grid,
        in_specs=[dy_spec] + ([one] if has else []) + b_specs + [tile] * ne,
        out_specs=[tile] * no + ([one] if has else []),
        out_shape=[jax.ShapeDtypeStruct((M, n_out), d) for d in out_dtypes] + ([jax.ShapeDtypeStruct((1, K), F32)] if has else []),
        scratch_shapes=[] if row else [pltpu.VMEM((tm, K), F32)],
        compiler_params=_cparams(sem),
    )(dy, *([gain] if has else []), *([wg] * ns), *extras)


def mm_plain(a, b, *, name):
    M, K = a.shape
    N = b.shape[1]
    tm = min(M, 512)

    def body(a_ref, b_ref, o_ref):
        o_ref[...] = _dot(a_ref[...], b_ref[...])

    return pl.pallas_call(
        body, name=name, grid=(M // tm,),
        in_specs=[pl.BlockSpec((tm, K), lambda i: (i, 0)), pl.BlockSpec((K, N), lambda i: (0, 0))],
        out_specs=pl.BlockSpec((tm, N), lambda i: (i, 0)), out_shape=jax.ShapeDtypeStruct((M, N), F32),
        compiler_params=_cparams(("parallel",)),
    )(a, b)


def mm_tn(x, dy, g, t, *, name):
    off, K, ns, row = t
    S = x.shape[0]
    per = ns if off % (ns * K) == 0 else 1
    grid = (N_CHIPS, ns // per)
    if row:
        x_map = lambda j, q: (0, j)
        dy_map = lambda j, q: (0, q)
    else:
        x_map = lambda j, q: (0, 0)
        dy_map = lambda j, q: (0, j * (ns // per) + q)

    def body(g_in, x_ref, dy_ref, o_ref):
        del g_in
        xt = x_ref[...].T
        for q in range(per):
            o_ref[pl.ds(q * K, K), :] = _dot(xt, dy_ref[:, pl.ds(q * SLAB, SLAB)]).astype(o_ref.dtype)

    return pl.pallas_call(
        body, name=name, grid=grid,
        in_specs=[pl.BlockSpec(memory_space=pl.ANY), pl.BlockSpec((S, K), x_map), pl.BlockSpec((S, per * SLAB), dy_map)],
        out_specs=pl.BlockSpec((None, per * K, SLAB), lambda j, q: (j, off // (per * K) + q, 0)),
        out_shape=jax.ShapeDtypeStruct(g.shape, g.dtype),
        input_output_aliases={0: 0},
        compiler_params=_cparams(("parallel", "parallel")),
    )(g, x, dy)


def ew(fn, ins, out_dtypes, *, name, bcast=()):
    S = ins[0].shape[0]
    tr = min(ROW_TILE, S)
    cols = ins[0].shape[1]
    ni, nb = len(ins), len(bcast)

    def body(*refs):
        res = fn(*[r[...] for r in refs[:ni + nb]])
        for o, r in zip(refs[ni + nb:], res):
            o[...] = r.astype(o.dtype)

    return pl.pallas_call(
        body, name=name, grid=(S // tr,),
        in_specs=[pl.BlockSpec((tr, a.shape[1]), lambda i: (i, 0)) for a in ins]
        + [pl.BlockSpec(b.shape, lambda i: (0, 0)) for b in bcast],
        out_specs=[pl.BlockSpec((tr, cols), lambda i: (i, 0)) for _ in out_dtypes],
        out_shape=[jax.ShapeDtypeStruct((S, cols), d) for d in out_dtypes],
        compiler_params=_cparams(("parallel",)),
    )(*ins, *bcast)


def _rstd(x):
    return lax.rsqrt(jnp.mean(x * x, axis=-1, keepdims=True) + RMS_EPS)


def _sigmoid(x):
    return 1.0 / (1.0 + jnp.exp(-x))


def _log_sigmoid(z):
    return jnp.minimum(z, 0.0) - jnp.log(1.0 + jnp.exp(-jnp.abs(z)))


def _rms_bwd_tile(xv, gv, dh):
    rstd = _rstd(xv)
    xhat = xv * rstd
    gd = dh * gv
    dx = rstd * (gd - xhat * jnp.mean(xhat * gd, axis=-1, keepdims=True))
    return dx, jnp.sum(dh * xhat, axis=0, keepdims=True)


def rms_bwd(x, g, dh, dres, *, name):
    S, D = x.shape
    tr = min(ROW_TILE, S)

    def body(x_ref, g_ref, dh_ref, dres_ref, dx_ref, dxb_ref, dg_ref):
        i = pl.program_id(0)
        dx, dg = _rms_bwd_tile(x_ref[...], g_ref[...], dh_ref[...])
        dx = dx + dres_ref[...]
        dx_ref[...] = dx
        dxb_ref[...] = dx.astype(BF16)

        @pl.when(i == 0)
        def _():
            dg_ref[...] = dg

        @pl.when(i > 0)
        def _():
            dg_ref[...] += dg

    row = pl.BlockSpec((tr, D), lambda i: (i, 0))
    one = pl.BlockSpec((1, D), lambda i: (0, 0))
    return pl.pallas_call(
        body, name=name, grid=(S // tr,),
        in_specs=[row, one, row, row], out_specs=[row, row, one],
        out_shape=[jax.ShapeDtypeStruct((S, D), F32), jax.ShapeDtypeStruct((S, D), BF16),
                   jax.ShapeDtypeStruct((1, D), F32)],
        compiler_params=_cparams(("arbitrary",)),
    )(x, g, dh, dres)


def loss_head(x, g, target, *, name):
    S, D = x.shape
    tr = min(ROW_TILE, S)

    def body(x_ref, g_ref, t_ref, dx_ref, dg_ref, loss_ref):
        i = pl.program_id(0)
        xv, gv = x_ref[...], g_ref[...]
        err = xv * _rstd(xv) * gv - t_ref[...]
        part = 0.5 * jnp.sum(jnp.mean(err * err, axis=-1, keepdims=True), axis=0, keepdims=True)
        dx, dg = _rms_bwd_tile(xv, gv, err * (1.0 / D))
        dx_ref[...] = dx
        part = jnp.broadcast_to(part, loss_ref.shape)

        @pl.when(i == 0)
        def _():
            dg_ref[...] = dg
            loss_ref[...] = part

        @pl.when(i > 0)
        def _():
            dg_ref[...] += dg
            loss_ref[...] += part

    row = pl.BlockSpec((tr, D), lambda i: (i, 0))
    one = pl.BlockSpec((1, D), lambda i: (0, 0))
    return pl.pallas_call(
        body, name=name, grid=(S // tr,),
        in_specs=[row, one, row], out_specs=[row, one, pl.BlockSpec((1, 128), lambda i: (0, 0))],
        out_shape=[jax.ShapeDtypeStruct((S, D), F32), jax.ShapeDtypeStruct((1, D), F32),
                   jax.ShapeDtypeStruct((1, 128), F32)],
        compiler_params=_cparams(("arbitrary",)),
    )(x, g, target)


def rope_tables(S):
    half = ROPE_DIM // 2
    inv_freq = ROPE_THETA ** (-jnp.arange(half, dtype=F32) / half)
    ang = jnp.arange(S, dtype=F32)[:, None] * inv_freq[None, :]
    cos, sin = jnp.cos(ang), jnp.sin(ang)
    z = jnp.zeros((S, HEAD_DIM - ROPE_DIM), F32)
    zh = jnp.zeros((S, half), F32)
    c = jnp.concatenate([cos, cos, jnp.ones_like(z)], axis=1)
    sa = jnp.concatenate([zh, sin, z], axis=1)
    sb = jnp.concatenate([-sin, zh, z], axis=1)
    return [jnp.concatenate([t, t], axis=1) for t in (c, sa, sb)]


def _wide(t, n):
    return jnp.tile(t, (1, n // t.shape[1]))


def rope_fwd(xqk, tables, *, name):
    n, half = xqk.shape[1], ROPE_DIM // 2

    def fn(x, c, sa, sb):
        return (x * _wide(c, n) + pltpu.roll(x, half, 1) * _wide(sa, n) + pltpu.roll(x, n - half, 1) * _wide(sb, n),)

    return ew(fn, [xqk] + list(tables), [BF16], name=name)[0]


def rope_bwd(dy, tables, *, name):
    n, half = dy.shape[1], ROPE_DIM // 2

    def fn(d, c, sa, sb):
        return (d * _wide(c, n) + pltpu.roll(d * _wide(sa, n), n - half, 1) + pltpu.roll(d * _wide(sb, n), half, 1),)

    return ew(fn, [dy] + list(tables), [BF16], name=name)[0]


def _split3(x):
    h1 = x.astype(BF16)
    r1 = x - h1.astype(F32)
    h2 = r1.astype(BF16)
    return h1, h2, (r1 - h2.astype(F32)).astype(BF16)


def _tri(n, cmp):
    r = lax.broadcasted_iota(jnp.int32, (n, n), 0)
    c = lax.broadcasted_iota(jnp.int32, (n, n), 1)
    return cmp(r, c).astype(BF16)


def fox_gate_fwd(fl, b, *, name):
    S, W = fl.shape
    tr = min(ROW_TILE, S)

    def body(fl_ref, b_ref, cum_ref, carry):
        i = pl.program_id(0)

        @pl.when(i == 0)
        def _():
            carry[...] = jnp.zeros_like(carry)

        lower = _tri(tr, lambda r, c: r >= c)
        cs = carry[...]
        for piece in _split3(_log_sigmoid(fl_ref[...] + b_ref[...])):
            cs = cs + _dot(lower, piece)
        cum_ref[...] = cs
        carry[...] = cs[tr - 1:tr, :]

    return pl.pallas_call(
        body, name=name, grid=(S // tr,),
        in_specs=[pl.BlockSpec((tr, W), lambda i: (i, 0)), pl.BlockSpec((1, W), lambda i: (0, 0))],
        out_specs=pl.BlockSpec((tr, W), lambda i: (i, 0)),
        out_shape=jax.ShapeDtypeStruct((S, W), F32),
        scratch_shapes=[pltpu.VMEM((1, W), F32)],
        compiler_params=_cparams(("arbitrary",)),
    )(fl, b)


def fox_gate_bwd(dcum, fl, b, *, name):
    S, W = fl.shape
    tr = min(ROW_TILE, S)
    nb = S // tr

    def body(dc_ref, fl_ref, b_ref, dfl_ref, db_ref, carry):
        i = pl.program_id(0)

        @pl.when(i == 0)
        def _():
            carry[...] = jnp.zeros_like(carry)

        upper = _tri(tr, lambda r, c: r <= c)
        cs = carry[...]
        for piece in _split3(dc_ref[...]):
            cs = cs + _dot(upper, piece)
        carry[...] = cs[0:1, :]
        dfl = cs * _sigmoid(-(fl_ref[...] + b_ref[...]))
        dfl_ref[...] = dfl
        db = jnp.sum(dfl, axis=0, keepdims=True)

        @pl.when(i == 0)
        def _():
            db_ref[...] = db

        @pl.when(i > 0)
        def _():
            db_ref[...] += db

    rev = pl.BlockSpec((tr, W), lambda i: (nb - 1 - i, 0))
    one = pl.BlockSpec((1, W), lambda i: (0, 0))
    return pl.pallas_call(
        body, name=name, grid=(nb,),
        in_specs=[rev, rev, one], out_specs=[rev, one],
        out_shape=[jax.ShapeDtypeStruct((S, W), F32), jax.ShapeDtypeStruct((1, W), F32)],
        scratch_shapes=[pltpu.VMEM((1, W), F32)],
        compiler_params=_cparams(("arbitrary",)),
    )(dcum, fl, b)


def _blk_iota(tq, tk):
    return (lax.broadcasted_iota(jnp.int32, (tq, tk), 0), lax.broadcasted_iota(jnp.int32, (tq, tk), 1))


def _cs(xb, tri):
    return _dot(xb, tri)


def _rowsum(xb):
    return jnp.sum(xb.astype(F32), axis=1, keepdims=True)


def _sb_block(qs, k, cmr, shift):
    z = _dot_nt(qs, k)
    lb = jnp.minimum(z, 0.0) - jnp.log(1.0 + jnp.exp(-jnp.abs(z)))
    if cmr is None:
        return lb, (lb - z).astype(BF16), None
    strict = cmr < shift
    lom = jnp.where(strict, lb - z, 0.0).astype(BF16)
    return lb, lom, strict


def _keep(mask, x):
    return x if mask is None else jnp.where(mask, x, 0.0)


def _att_tiles(S):
    return min(ATT_BQ, S), min(ATT_BK, S)


PAIR = 2 * HEAD_DIM
N_PAIRS = N_HEADS // 2


def _pair_specs(S, tq):
    cols = D_MODEL // PAIR
    qspec = pl.BlockSpec((tq, PAIR), lambda p, i: (i, p))
    kspec = pl.BlockSpec((S, PAIR), lambda p, i: (0, cols + p))
    vspec = pl.BlockSpec((S, PAIR), lambda p, i: (0, 2 * cols + p))
    kvout = pl.BlockSpec((S, PAIR), lambda p, i: (0, p))
    vec = pl.BlockSpec((2, tq, 1), lambda p, i: (p, i, 0))
    return qspec, kspec, vspec, kvout, vec


def _head_lanes(h):
    lane = lax.broadcasted_iota(jnp.int32, (1, PAIR), 1)
    return (lane >= h * HEAD_DIM) & (lane < (h + 1) * HEAD_DIM)


def _only(sel, x):
    return jnp.where(sel, x, jnp.zeros_like(x))


def sb_fwd(proj, *, name):
    S = proj.shape[0]
    tq, tk = _att_tiles(S)
    qspec, kspec, vspec, _, vec = _pair_specs(S, tq)

    def body(q_ref, k_ref, v_ref, o_ref, t_ref):
        i = pl.program_id(1)
        row, col = _blk_iota(tq, tk)
        cmr = col - row
        below = _tri(tk, lambda r, c: r > c)
        nkb = (i + 1) * (tq // tk)
        out = []
        for h in range(2):
            sel = _head_lanes(h)
            qs = _only(sel, q_ref[...] * ATTN_SCALE)

            def step(n, carry, masked):
                r_sum, acc = carry
                kb = nkb - 1 - n
                ks = pl.multiple_of(kb * tk, tk)
                lb, lom, strict = _sb_block(qs, k_ref[pl.ds(ks, tk), :], cmr if masked else None, i * tq - kb * tk)
                w = _keep(strict, jnp.exp(lb + _cs(lom, below) + r_sum))
                acc = acc + _dot(w.astype(BF16), _only(sel, v_ref[pl.ds(ks, tk), :]))
                return r_sum + _rowsum(lom), acc

            nd = tq // tk
            carry = lax.fori_loop(0, nd, lambda n, c: step(n, c, True), (jnp.zeros((tq, 1), F32), jnp.zeros((tq, PAIR), F32)))
            r_sum, acc = lax.fori_loop(nd, nkb, lambda n, c: step(n, c, False), carry)
            t_ref[h] = r_sum
            out.append(acc)
        o_ref[...] = (out[0] + out[1]).astype(o_ref.dtype)

    return pl.pallas_call(
        body, name=name, grid=(N_PAIRS, S // tq),
        in_specs=[qspec, kspec, vspec], out_specs=[qspec, vec],
        out_shape=[jax.ShapeDtypeStruct((S, D_MODEL), BF16), jax.ShapeDtypeStruct((N_HEADS, S, 1), F32)],
        compiler_params=_cparams(("parallel", "arbitrary")),
    )(proj, proj, proj)


def sb_bwd(proj, tot, do, *, name):
    S = proj.shape[0]
    tq, tk = _att_tiles(S)
    qspec, kspec, vspec, kvout, vec = _pair_specs(S, tq)

    def body(q_ref, k_ref, v_ref, t_ref, do_ref, dq_ref, dk_out, dv_out, dk_ref, dv_ref):
        i = pl.program_id(1)

        @pl.when(i == 0)
        def _():
            dk_ref[...] = jnp.zeros_like(dk_ref)
            dv_ref[...] = jnp.zeros_like(dv_ref)

        row, col = _blk_iota(tq, tk)
        cmr = col - row
        upto = _tri(tk, lambda r, c: r <= c)
        before = _tri(tk, lambda r, c: r < c)
        out = []
        for h in range(2):
            sel = _head_lanes(h)
            qs, dov, t_all = _only(sel, q_ref[...] * ATTN_SCALE), _only(sel, do_ref[...]), t_ref[h]

            def step(kb, carry, masked):
                p_sum, e_sum, dq = carry
                ks = pl.multiple_of(kb * tk, tk)
                kv = k_ref[pl.ds(ks, tk), :]
                lb, lom, strict = _sb_block(qs, kv, cmr if masked else None, i * tq - kb * tk)
                tail = t_all - p_sum - _cs(lom, upto)
                w = _keep(strict, jnp.exp(lb + tail))
                e = _dot_nt(dov, v_ref[pl.ds(ks, tk), :]) * w
                eb = e.astype(BF16)
                e_before = e_sum + _cs(eb, before)
                beta = jnp.exp(lb)
                dzb = _keep(strict, e - (e + e_before) * beta).astype(BF16)
                dk_ref[pl.ds(ks, tk), :] += _dot_tn(dzb, qs)
                dv_ref[pl.ds(ks, tk), :] += _dot_tn(w.astype(BF16), dov)
                return p_sum + _rowsum(lom), e_sum + _rowsum(eb), dq + _dot(dzb, _only(sel, kv))

            zero = jnp.zeros((tq, 1), F32)
            nlow = i * (tq // tk)
            carry = lax.fori_loop(0, nlow, lambda kb, c: step(kb, c, False), (zero, zero, jnp.zeros((tq, PAIR), F32)))
            out.append(lax.fori_loop(nlow, nlow + tq // tk, lambda kb, c: step(kb, c, True), carry)[2])
        dq_ref[...] = ((out[0] + out[1]) * ATTN_SCALE).astype(dq_ref.dtype)

        @pl.when(i == S // tq - 1)
        def _():
            dk_out[...] = dk_ref[...].astype(dk_out.dtype)
            dv_out[...] = dv_ref[...].astype(dv_out.dtype)

    full = jax.ShapeDtypeStruct((S, D_MODEL), BF16)
    return pl.pallas_call(
        body, name=name, grid=(N_PAIRS, S // tq),
        in_specs=[qspec, kspec, vspec, vec, qspec], out_specs=[qspec, kvout, kvout],
        out_shape=[full, full, full],
        scratch_shapes=[pltpu.VMEM((S, PAIR), F32)] * 2,
        compiler_params=_cparams(("parallel", "arbitrary")),
    )(proj, proj, proj, tot, do)


def _fox_logits(qs, k, cq, ck, cmr, shift):
    s = _dot_nt(qs, k) + cq - ck
    if cmr is None:
        return s, None
    causal = cmr <= shift
    return jnp.where(causal, s, NEG_INF), causal


def fox_fwd(proj, cq, ck, *, name):
    S = proj.shape[0]
    tq, tk = _att_tiles(S)
    qspec, kspec, vspec, _, vec = _pair_specs(S, tq)
    ckspec = pl.BlockSpec((2, S // tk, 1, tk), lambda p, i: (p, 0, 0, 0))

    def body(q_ref, k_ref, v_ref, cq_ref, ck_ref, o_ref, lse_ref):
        i = pl.program_id(1)
        row, col = _blk_iota(tq, tk)
        cmr = col - row
        out = []
        for h in range(2):
            sel = _head_lanes(h)
            qs, cqv = _only(sel, q_ref[...] * ATTN_SCALE), cq_ref[h]

            def step(kb, carry, masked):
                m, l, acc = carry
                ks = pl.multiple_of(kb * tk, tk)
                s, _ = _fox_logits(qs, k_ref[pl.ds(ks, tk), :], cqv, ck_ref[h, kb], cmr if masked else None, i * tq - kb * tk)
                m_new = jnp.maximum(m, jnp.max(s, axis=1, keepdims=True))
                alpha = jnp.exp(m - m_new)
                p = jnp.exp(s - m_new)
                l = alpha * l + jnp.sum(p, axis=1, keepdims=True)
                acc = alpha * acc + _dot(p.astype(BF16), _only(sel, v_ref[pl.ds(ks, tk), :]))
                return m_new, l, acc

            nlow = i * (tq // tk)
            carry = lax.fori_loop(0, nlow, lambda kb, c: step(kb, c, False),
                                  (jnp.full((tq, 1), NEG_INF, F32), jnp.zeros((tq, 1), F32), jnp.zeros((tq, PAIR), F32)))
            m, l, acc = lax.fori_loop(nlow, nlow + tq // tk, lambda kb, c: step(kb, c, True), carry)
            lse_ref[h] = m + jnp.log(l)
            out.append(acc / l)
        o_ref[...] = (out[0] + out[1]).astype(o_ref.dtype)

    return pl.pallas_call(
        body, name=name, grid=(N_PAIRS, S // tq),
        in_specs=[qspec, kspec, vspec, vec, ckspec], out_specs=[qspec, vec],
        out_shape=[jax.ShapeDtypeStruct((S, D_MODEL), BF16), jax.ShapeDtypeStruct((N_HEADS, S, 1), F32)],
        compiler_params=_cparams(("parallel", "arbitrary")),
    )(proj, proj, proj, cq, ck)


def fox_bwd(proj, o, lse, cq, ck, do, *, name):
    S = proj.shape[0]
    tq, tk = _att_tiles(S)
    qspec, kspec, vspec, kvout, vec = _pair_specs(S, tq)
    ckspec = pl.BlockSpec((2, S // tk, 1, tk), lambda p, i: (p, 0, 0, 0))

    def body(q_ref, k_ref, v_ref, o_ref, lse_ref, cq_ref, ck_ref, do_ref, dq_ref, dk_out, dv_out, dcq_ref, dck_ref,
             dk_ref, dv_ref):
        i = pl.program_id(1)

        @pl.when(i == 0)
        def _():
            dk_ref[...] = jnp.zeros_like(dk_ref)
            dv_ref[...] = jnp.zeros_like(dv_ref)
            dck_ref[...] = jnp.zeros_like(dck_ref)

        row, col = _blk_iota(tq, tk)
        cmr = col - row
        out = []
        for h in range(2):
            sel = _head_lanes(h)
            qs, dov, cqv, lsev = _only(sel, q_ref[...] * ATTN_SCALE), _only(sel, do_ref[...]), cq_ref[h], lse_ref[h]
            delta = jnp.sum(dov.astype(F32) * o_ref[...].astype(F32), axis=1, keepdims=True)

            def step(kb, carry, masked):
                dq, dcq = carry
                ks = pl.multiple_of(kb * tk, tk)
                kv = k_ref[pl.ds(ks, tk), :]
                s, causal = _fox_logits(qs, kv, cqv, ck_ref[h, kb], cmr if masked else None, i * tq - kb * tk)
                p = _keep(causal, jnp.exp(s - lsev))
                ds = p * (_dot_nt(dov, v_ref[pl.ds(ks, tk), :]) - delta)
                dck_ref[h, kb] += jnp.sum(ds, axis=0, keepdims=True)
                dsb = ds.astype(BF16)
                dk_ref[pl.ds(ks, tk), :] += _dot_tn(dsb, qs)
                dv_ref[pl.ds(ks, tk), :] += _dot_tn(p.astype(BF16), dov)
                return dq + _dot(dsb, _only(sel, kv)), dcq + jnp.sum(ds, axis=1, keepdims=True)

            nlow = i * (tq // tk)
            carry = lax.fori_loop(0, nlow, lambda kb, c: step(kb, c, False),
                                  (jnp.zeros((tq, PAIR), F32), jnp.zeros((tq, 1), F32)))
            dq, dcq = lax.fori_loop(nlow, nlow + tq // tk, lambda kb, c: step(kb, c, True), carry)
            dcq_ref[h] = dcq
            out.append(dq)
        dq_ref[...] = ((out[0] + out[1]) * ATTN_SCALE).astype(dq_ref.dtype)

        @pl.when(i == S // tq - 1)
        def _():
            dk_out[...] = dk_ref[...].astype(dk_out.dtype)
            dv_out[...] = dv_ref[...].astype(dv_out.dtype)

    full = jax.ShapeDtypeStruct((S, D_MODEL), BF16)
    return pl.pallas_call(
        body, name=name, grid=(N_PAIRS, S // tq),
        in_specs=[qspec, kspec, vspec, qspec, vec, vec, ckspec, qspec],
        out_specs=[qspec, kvout, kvout, vec, ckspec],
        out_shape=[full, full, full, jax.ShapeDtypeStruct((N_HEADS, S, 1), F32),
                   jax.ShapeDtypeStruct((N_HEADS, S // tk, 1, tk), F32)],
        scratch_shapes=[pltpu.VMEM((S, PAIR), F32)] * 2,
        compiler_params=_cparams(("parallel", "arbitrary")),
    )(proj, proj, proj, o, lse, cq, ck, do)


def _swa_specs(S, tq):
    qspec = pl.BlockSpec((None, SWA_GROUP, tq, HEAD_DIM), lambda g, i: (g, 0, i, 0))
    kvspec = pl.BlockSpec((None, S + SWA_WINDOW, HEAD_DIM), lambda g, i: (g, 0, 0))
    vec = pl.BlockSpec((None, SWA_GROUP, tq, 1), lambda g, i: (g, 0, i, 0))
    sink = pl.BlockSpec((None, SWA_GROUP * tq, 1), lambda g, i: (g, 0, 0))
    return qspec, kvspec, vec, sink


def _swa_logits(q2, kw, i, tq):
    rows = q2.shape[0]
    r = lax.broadcasted_iota(jnp.int32, (rows, 2 * tq), 0)
    c = lax.broadcasted_iota(jnp.int32, (rows, 2 * tq), 1)
    diff = (r & (tq - 1)) + tq - c
    ok = (diff >= 0) & (diff < SWA_WINDOW) & (c + (i - 1) * tq >= 0)
    return jnp.where(ok, _dot_nt(q2, kw) * ATTN_SCALE, NEG_INF), ok


def swa_fwd(q, kp, vp, sink, *, name):
    _, G, S, _ = q.shape
    tq = ATT_BLK
    qspec, kvspec, vec, sinkspec = _swa_specs(S, tq)

    def body(q_ref, k_ref, v_ref, s_ref, o_ref, lse_ref):
        i = pl.program_id(1)
        q2 = q_ref[...].reshape(G * tq, HEAD_DIM)
        ws = pl.multiple_of(i * tq, tq)
        logits, _ = _swa_logits(q2, k_ref[pl.ds(ws, 2 * tq), :], i, tq)
        sk = s_ref[...]
        m = jnp.maximum(jnp.max(logits, axis=1, keepdims=True), sk)
        e = jnp.exp(logits - m)
        den = jnp.sum(e, axis=1, keepdims=True) + jnp.exp(sk - m)
        o = _dot((e / den).astype(BF16), v_ref[pl.ds(ws, 2 * tq), :])
        o_ref[...] = o.reshape(G, tq, HEAD_DIM).astype(o_ref.dtype)
        lse_ref[...] = (m + jnp.log(den)).reshape(G, tq, 1)

    return pl.pallas_call(
        body, name=name, grid=(SWA_KV_HEADS, S // tq),
        in_specs=[qspec, kvspec, kvspec, sinkspec], out_specs=[qspec, vec],
        out_shape=[jax.ShapeDtypeStruct(q.shape, BF16), jax.ShapeDtypeStruct((SWA_KV_HEADS, G, S, 1), F32)],
        compiler_params=_cparams(("parallel", "arbitrary")),
    )(q, kp, vp, sink)


def swa_bwd(q, kp, vp, sink, o, lse, do, *, name):
    _, G, S, _ = q.shape
    tq = ATT_BLK
    qspec, kvspec, vec, sinkspec = _swa_specs(S, tq)

    def body(q_ref, k_ref, v_ref, s_ref, o_ref, lse_ref, do_ref, dq_ref, dk_ref, dv_ref, dsink_ref):
        i = pl.program_id(1)

        @pl.when(i == 0)
        def _():
            dk_ref[...] = jnp.zeros_like(dk_ref)
            dv_ref[...] = jnp.zeros_like(dv_ref)

        q2 = q_ref[...].reshape(G * tq, HEAD_DIM)
        do2 = do_ref[...].reshape(G * tq, HEAD_DIM)
        o2 = o_ref[...].reshape(G * tq, HEAD_DIM)
        lse2 = lse_ref[...].reshape(G * tq, 1)
        ws = pl.multiple_of(i * tq, tq)
        kw = k_ref[pl.ds(ws, 2 * tq), :]
        vw = v_ref[pl.ds(ws, 2 * tq), :]
        logits, ok = _swa_logits(q2, kw, i, tq)
        p = jnp.where(ok, jnp.exp(logits - lse2), 0.0)
        delta = jnp.sum(do2.astype(F32) * o2.astype(F32), axis=1, keepdims=True)
        ds = p * (_dot_nt(do2, vw) - delta)
        dsb = ds.astype(BF16)
        dq_ref[...] = (_dot(dsb, kw) * ATTN_SCALE).reshape(G, tq, HEAD_DIM)
        dk_ref[pl.ds(ws, 2 * tq), :] += _dot_tn(dsb, q2) * ATTN_SCALE
        dv_ref[pl.ds(ws, 2 * tq), :] += _dot_tn(p.astype(BF16), do2)
        dsink_ref[...] = (-jnp.exp(s_ref[...] - lse2) * delta).reshape(G, tq, 1)

    kvshape = jax.ShapeDtypeStruct(kp.shape, F32)
    return pl.pallas_call(
        body, name=name, grid=(SWA_KV_HEADS, S // tq),
        in_specs=[qspec, kvspec, kvspec, sinkspec, qspec, vec, qspec],
        out_specs=[qspec, kvspec, kvspec, vec],
        out_shape=[jax.ShapeDtypeStruct(q.shape, F32), kvshape, kvshape,
                   jax.ShapeDtypeStruct((SWA_KV_HEADS, G, S, 1), F32)],
        compiler_params=_cparams(("parallel", "arbitrary")),
    )(q, kp, vp, sink, o, lse, do)


def _adamw_tile(w, g, m, v):
    m = ADAM_B1 * m + (1.0 - ADAM_B1) * g
    v = ADAM_B2 * v + (1.0 - ADAM_B2) * (g * g)
    m_hat = m / (1.0 - ADAM_B1 ** ADAM_STEP)
    v_hat = v / (1.0 - ADAM_B2 ** ADAM_STEP)
    delta = -ADAM_LR * (m_hat / (jnp.sqrt(v_hat) + ADAM_EPS) + ADAM_WD * w)
    return g, delta, m, v


def adamw(gfull, t, w, m, v, after, *, name):
    off, K, ns, _ = t
    sb = off // K
    nat = pl.BlockSpec((K, SLAB), lambda q: (0, q))

    def body(g_ref, w_ref, m_ref, v_ref, after_ref, *outs):
        del after_ref
        for o, r in zip(outs, _adamw_tile(w_ref[...], g_ref[...], m_ref[...], v_ref[...])):
            o[...] = r

    return pl.pallas_call(
        body, name=name, grid=(ns,),
        in_specs=[pl.BlockSpec((K, SLAB), lambda q: (sb + q, 0)), nat, nat, nat, HBM],
        out_specs=[nat] * 4, out_shape=[jax.ShapeDtypeStruct(w.shape, F32)] * 4,
        compiler_params=_cparams(("parallel",)),
    )(gfull, w, m, v, after)


def adamw_small(g, w, m, v, *, name):
    rows, d = w.shape

    def body(g_ref, w_ref, m_ref, v_ref, *outs):
        for j in range(rows):
            one = pl.ds(j, 1)
            for k, r in enumerate(_adamw_tile(w_ref[one, :], g_ref[one, :], m_ref[one, :], v_ref[one, :])):
                outs[k * rows + j][...] = r

    flat = pl.pallas_call(body, name=name, out_shape=[jax.ShapeDtypeStruct((1, d), F32)] * (4 * rows))(g, w, m, v)
    return [flat[k * rows:(k + 1) * rows] for k in range(4)]


MESH = pl.DeviceIdType.MESH
HBM = pl.BlockSpec(memory_space=pl.ANY)


def _place():
    x, y, c = lax.axis_index("x"), lax.axis_index("y"), lax.axis_index("c")
    others = [(1 - x, y), (x, 1 - y), (1 - x, 1 - y)]
    return x, y, c, others


def _rcopy(src, dst, send_sems, recv_sems, k, to):
    return pltpu.make_async_remote_copy(src_ref=src, dst_ref=dst, send_sem=send_sems.at[k], recv_sem=recv_sems.at[k],
                                        device_id=to, device_id_type=MESH)


def _dma_sems(*counts):
    return [pltpu.SemaphoreType.DMA((n,)) for n in counts]


DMA_UNIT_ROWS = 128
DMA_PIECES = 4
DMA_PIECES_LOCAL = 8


def _row_pieces(h, n):
    units = h // DMA_UNIT_ROWS
    n = min(n, units)
    base, extra = divmod(units, n)
    sizes = [(base + (k < extra)) * DMA_UNIT_ROWS for k in range(n)]
    return [(sum(sizes[:k]), sizes[k]) for k in range(n)]


def _start_pieces(make, h, n):
    for s0, sz in _row_pieces(h, n):
        make(s0, sz).start()
    return make(0, h)


SEM = pl.BlockSpec(memory_space=pltpu.SEMAPHORE)
SPLIT_COPY = pltpu.CompilerParams(has_side_effects=pltpu.SideEffectType.DATAFLOW_SIDE_EFFECTING)
N_OTHERS = 3


def _hbm(a):
    return pltpu.with_memory_space_constraint(a, pltpu.HBM)


def _chip_rows(buf, chip, s0, sz):
    return buf.at[2 * chip[0] + chip[1], pl.ds(s0, sz)]


def allgather_start(bufs, *, name):
    n = len(bufs)

    def body(*refs):
        ins, send, recv, token = refs[:n], refs[n:2 * n], refs[2 * n:3 * n], refs[4 * n]
        x, y, c, others = _place()
        for i in range(n):
            h = bufs[i].shape[1] // 2
            for f, chip in enumerate(others):
                for s0, sz in _row_pieces(h, DMA_PIECES):
                    mine = _chip_rows(ins[i], (x, y), c * h + s0, sz)
                    _rcopy(mine, mine, send[i], recv[i], f, (*chip, c)).start()
        token[...] = jnp.zeros_like(token)

    res = pl.pallas_call(
        body, name=name, in_specs=[HBM] * n,
        out_specs=[SEM] * (2 * n) + [HBM] * n + [pl.BlockSpec(memory_space=pltpu.VMEM)],
        out_shape=[pltpu.SemaphoreType.DMA((N_OTHERS,))] * (2 * n) + [pltpu.HBM(b.shape, b.dtype) for b in bufs]
        + [jax.ShapeDtypeStruct((1, D_MODEL), F32)],
        input_output_aliases={i: 2 * n + i for i in range(n)},
        compiler_params=SPLIT_COPY,
    )(*[_hbm(b) for b in bufs])
    return res[:n], res[n:2 * n], res[2 * n:3 * n], res[3 * n]


def allgather_wait(buf, send, recv, after, *, name):
    h = buf.shape[1] // 2

    def body(buf_ref, send_sems, recv_sems, after_ref, out_ref):
        del after_ref, out_ref
        x, y, c, others = _place()
        for f, chip in enumerate(others):
            mine = _chip_rows(buf_ref, (x, y), c * h, h)
            theirs = _chip_rows(buf_ref, chip, c * h, h)
            cp = _rcopy(mine, theirs, send_sems, recv_sems, f, (*chip, c))
            cp.wait_send()
            cp.wait_recv()

    return pl.pallas_call(
        body, name=name, in_specs=[HBM, SEM, SEM, HBM], out_specs=HBM,
        out_shape=pltpu.HBM(buf.shape, buf.dtype), input_output_aliases={0: 0},
        compiler_params=SPLIT_COPY,
    )(buf, send, recv, after)


def allgather_forward(buf, *, name):
    h = buf.shape[1] // 2

    def body(in_ref, out_ref, send_sems, recv_sems):
        del in_ref
        x, y, c, others = _place()
        sibling = (x, y, 1 - c)
        sent = []
        for f, chip in enumerate(others):
            sent.append(_start_pieces(
                lambda s0, sz: _rcopy(_chip_rows(out_ref, chip, c * h + s0, sz), _chip_rows(out_ref, chip, c * h + s0, sz),
                                      send_sems, recv_sems, f, sibling), h, DMA_PIECES))
        for f, chip in enumerate(others):
            blk = _chip_rows(out_ref, chip, (1 - c) * h, h)
            _rcopy(blk, blk, send_sems, recv_sems, f, sibling).wait_recv()
        for cp in sent:
            cp.wait_send()

    return pl.pallas_call(
        body, name=name, in_specs=[HBM], out_specs=HBM,
        out_shape=jax.ShapeDtypeStruct(buf.shape, buf.dtype), input_output_aliases={0: 0},
        scratch_shapes=_dma_sems(N_OTHERS, N_OTHERS),
    )(buf)


def swap_halves(grads, *, name):
    n = len(grads)

    def body(*refs):
        ins, theirs = refs[:n], refs[n:2 * n]
        send_sems, recv_sems = refs[2 * n:]
        x, y, c, _ = _place()
        for i in range(n):
            h = grads[i].shape[1] // 2
            for k in range(N_CHIPS):
                for s0, sz in _row_pieces(h, DMA_PIECES):
                    _rcopy(ins[i].at[k, pl.ds((1 - c) * h + s0, sz)], theirs[i].at[k, pl.ds(s0, sz)],
                           send_sems, recv_sems, i, (x, y, 1 - c)).start()
        for i in range(n):
            h = grads[i].shape[1] // 2
            _rcopy(ins[i].at[:, pl.ds((1 - c) * h, h)], theirs[i], send_sems, recv_sems, i, (x, y, 1 - c)).wait()

    return pl.pallas_call(
        body, name=name, in_specs=[HBM] * n, out_specs=[HBM] * n,
        out_shape=[jax.ShapeDtypeStruct((N_CHIPS, g.shape[1] // 2, SLAB), g.dtype) for g in grads],
        scratch_shapes=_dma_sems(n, n))(*grads)


def scatter_start(part, *, name):
    h = part.shape[1]

    def body(part_ref, land_ref, send, recv, part_out, land_out, token):
        del part_out, land_out
        x, y, c, others = _place()
        for f, chip in enumerate(others):
            for s0, sz in _row_pieces(h, DMA_PIECES):
                _rcopy(_chip_rows(part_ref, chip, s0, sz), land_ref.at[f, pl.ds(s0, sz)], send, recv, f, (*chip, c)).start()
        token[...] = jnp.zeros_like(token)

    land = lax.empty((N_OTHERS,) + part.shape[1:], part.dtype)
    return pl.pallas_call(
        body, name=name, in_specs=[HBM, HBM],
        out_specs=[SEM, SEM, HBM, HBM, pl.BlockSpec(memory_space=pltpu.VMEM)],
        out_shape=[pltpu.SemaphoreType.DMA((N_OTHERS,))] * 2 + [pltpu.HBM(part.shape, part.dtype), pltpu.HBM(land.shape, land.dtype),
                                                                 jax.ShapeDtypeStruct((1, D_MODEL), F32)],
        input_output_aliases={0: 2, 1: 3},
        compiler_params=SPLIT_COPY,
    )(_hbm(part), _hbm(land))


def scatter_wait(part, land, send, recv, after, *, name):
    h = part.shape[1]

    def body(part_ref, land_ref, send_sems, recv_sems, after_ref, part_out, land_out):
        del after_ref, part_out, land_out
        x, y, c, others = _place()
        for f, chip in enumerate(others):
            cp = _rcopy(_chip_rows(part_ref, chip, 0, h), land_ref.at[f], send_sems, recv_sems, f, (*chip, c))
            cp.wait_send()
            cp.wait_recv()

    return pl.pallas_call(
        body, name=name, in_specs=[HBM, HBM, SEM, SEM, HBM], out_specs=[HBM, HBM],
        out_shape=[pltpu.HBM(part.shape, part.dtype), pltpu.HBM(land.shape, land.dtype)],
        input_output_aliases={0: 0, 1: 1},
        compiler_params=SPLIT_COPY,
    )(part, land, send, recv, after)


def join_halves(bufs, *, name):
    n = len(bufs)

    def body(*refs):
        outs = refs[n:2 * n]
        send_sems, recv_sems = refs[2 * n:]
        x, y, c, _ = _place()
        sibling = (x, y, 1 - c)
        cps = []
        for i in range(n):
            h = bufs[i].shape[0] // 2
            snd = _start_pieces(
                lambda s0, sz: _rcopy(outs[i].at[pl.ds(c * h + s0, sz)], outs[i].at[pl.ds(c * h + s0, sz)],
                                      send_sems, recv_sems, i, sibling), h, 2 * DMA_PIECES_LOCAL)
            theirs = outs[i].at[pl.ds((1 - c) * h, h)]
            cps.append((snd, _rcopy(theirs, theirs, send_sems, recv_sems, i, sibling)))
        for snd, rcv in cps:
            snd.wait_send()
            rcv.wait_recv()

    return pl.pallas_call(
        body, name=name, in_specs=[HBM] * n, out_specs=[HBM] * n,
        out_shape=[jax.ShapeDtypeStruct(b.shape, b.dtype) for b in bufs],
        input_output_aliases={i: i for i in range(n)},
        scratch_shapes=_dma_sems(n, n),
    )(*bufs)


def allreduce_small(v, *, name):
    rows, n = v.shape

    def body(x_ref, sum_ref, all_ref, send_sems, recv_sems, local_sem):
        x, y, c, others = _place()
        me, sibling = (x, y, c), (x, y, 1 - c)

        def blk(px, py, pc):
            return all_ref.at[pl.ds((4 * px + 2 * py + pc) * rows, rows), :]

        def copy(k, block, to, src=None):
            return _rcopy(blk(*block) if src is None else src, blk(*block), send_sems, recv_sems, k, to)

        mine = pltpu.make_async_copy(x_ref, blk(*me), local_sem)
        mine.start()
        first = [copy(0, me, sibling, src=x_ref)]
        first += [copy(1 + f, me, (*chip, c), src=x_ref) for f, chip in enumerate(others)]
        for cp in first:
            cp.start()
        passed = [copy(4 + f, (*chip, c), sibling) for f, chip in enumerate(others)]
        for f, chip in enumerate(others):
            copy(1 + f, (*chip, c), me).wait_recv()
            passed[f].start()
        copy(0, sibling, me).wait_recv()
        for f, chip in enumerate(others):
            copy(4 + f, (*chip, 1 - c), me).wait_recv()
        for cp in first + passed:
            cp.wait_send()
        mine.wait()
        acc = all_ref[pl.ds(0, rows), :]
        for d in range(1, N_DEVICES):
            acc = acc + all_ref[pl.ds(d * rows, rows), :]
        sum_ref[...] = acc

    vm = pl.BlockSpec(memory_space=pltpu.VMEM)
    return pl.pallas_call(
        body, name=name, in_specs=[vm], out_specs=[vm, vm],
        out_shape=[jax.ShapeDtypeStruct((rows, n), F32), jax.ShapeDtypeStruct((N_DEVICES * rows, n), F32)],
        scratch_shapes=_dma_sems(7, 7) + [pltpu.SemaphoreType.DMA],
    )(v)[0]


def add_pairs(grad, theirs, where, *, name):
    h = theirs.shape[1]
    spec = pl.BlockSpec((None, h, SLAB), lambda k, w: (k, 0, 0))

    def body(w_ref, a_ref, b_ref, o_ref):
        del w_ref
        o_ref[...] = (a_ref[...].astype(F32) + b_ref[...].astype(F32)).astype(o_ref.dtype)

    return pl.pallas_call(
        body, name=name,
        grid_spec=pltpu.PrefetchScalarGridSpec(
            num_scalar_prefetch=1, grid=(N_CHIPS,),
            in_specs=[pl.BlockSpec((None, h, SLAB), lambda k, w: (k, w[1], 0)), spec], out_specs=spec),
        out_shape=jax.ShapeDtypeStruct(theirs.shape, theirs.dtype),
        compiler_params=_cparams(("parallel",)))(where, grad, theirs)


def add_chips(pair, got, where, *, name):
    h = pair.shape[1]
    tr = h // 2

    def body(w_ref, a_ref, b_ref, o_ref):
        del w_ref
        acc = a_ref[...].astype(F32)
        for f in range(3):
            acc = acc + b_ref[f].astype(F32)
        o_ref[...] = acc

    return pl.pallas_call(
        body, name=name,
        grid_spec=pltpu.PrefetchScalarGridSpec(
            num_scalar_prefetch=1, grid=(2,),
            in_specs=[pl.BlockSpec((None, tr, SLAB), lambda i, w: (w[0], i, 0)),
                      pl.BlockSpec((3, tr, SLAB), lambda i, w: (0, i, 0))],
            out_specs=pl.BlockSpec((tr, SLAB), lambda i, w: (2 * w[1] + i, 0))),
        out_shape=jax.ShapeDtypeStruct((2 * h, SLAB), F32),
        compiler_params=_cparams(("parallel",)))(where, pair, got)


DEPTH = 4
MIXER = (0, 1, 2, 0)
W_IN_COLS = (768, 320, 772)
W_IN_PAD = (768, 512, 1024)
MATS = ("up", "down", "inp", "out", "gate", "proj")
MAT_ARG = dict(up="w_up", down="w_down", inp="w_in", out="w_out", gate="w_ple_gate", proj="w_ple_proj")
GAINS = ("attn_norm", "mlp_norm", "ple_norm")
N_SMALL = 16
KINDS = ("grad_", "delta_", "new_m_", "new_v_")


def _layout(kind):
    ns_in = W_IN_PAD[kind] // SLAB
    off = 8192 + 1024 * ns_in
    lay = dict(up=(0, 1024, 4, False), down=(4096, 1024, 4, True), inp=(8192, 1024, ns_in, False),
               out=(off, 256, 4, True), gate=(off + 1024, 256, 4, True), proj=(off + 2048, 256, 1, False))
    return lay, off + 2304


def _to_slabs(w):
    k, c = w.shape
    return w.reshape(k, c // SLAB, SLAB).transpose(1, 0, 2).reshape(-1, SLAB)


def _pad_cols(w, n):
    return jnp.pad(w, ((0, 0), (0, n - w.shape[1])))


def _heads(x2d, n):
    return x2d.reshape(x2d.shape[0], n, HEAD_DIM).transpose(1, 0, 2)


def _unheads(x3d):
    n, s, _ = x3d.shape
    return x3d.transpose(1, 0, 2).reshape(s, n * HEAD_DIM)


def _chip_cols(x2d, c, cpad):
    return jnp.concatenate([_pad_cols(x2d[:, j * c:(j + 1) * c], cpad) for j in range(N_CHIPS)], axis=1)


def _unchip_cols(x2d, c, cpad):
    return jnp.concatenate([x2d[:, j * cpad:j * cpad + c] for j in range(N_CHIPS)], axis=1)


def _forget_cols(wg, t):
    off, K, _, _ = t
    cols = []
    for g in range(3 * N_HEADS * HEAD_DIM, 3 * N_HEADS * HEAD_DIM + N_HEADS):
        chip, local = divmod(g, W_IN_COLS[2])
        q, c = divmod(local, SLAB)
        cols.append(wg[chip, off + q * K:off + (q + 1) * K, c:c + 1])
    return jnp.concatenate(cols, axis=1)


def _add_res(acc, res):
    return (acc + res,)


def _relu2(acc):
    return acc, jnp.square(jnp.maximum(acc, 0.0))


def _relu2_bwd(acc, u):
    return (acc * (2.0 * jnp.maximum(u.astype(F32), 0.0)),)


def _ple_fwd(acc, x2, pp):
    return x2 + pp * _sigmoid(acc), acc


def _ple_bwd(dx, pp, gl):
    gate = _sigmoid(gl)
    return dx * gate, dx * pp * gate * (1.0 - gate)


def _layer_fwd(i, kind, x0, p_bf, wg, lay, gains, extra, tabs):
    s = x0.shape[0]
    an, mn, pn = gains
    sv = dict(x0=x0)
    if kind == 0:
        proj, h1 = mm_nn(x0, wg, lay["inp"], name=f"w_in_{i}", norm_gain=an)
        a, tot = sb_fwd(proj, name=f"sb_fwd_{i}")
        sv.update(proj=proj, tot=tot)
    elif kind == 1:
        projp, h1 = mm_nn(x0, wg, lay["inp"], name=f"w_in_{i}", out_dtypes=(F32,), norm_gain=an)
        proj = _unchip_cols(projp, W_IN_COLS[1], W_IN_PAD[1])
        nq = N_HEADS * HEAD_DIM
        nqk = nq + SWA_KV_HEADS * HEAD_DIM
        qk = rope_fwd(proj[:, :nqk], tabs, name=f"rope_{i}")
        q = _heads(qk[:, :nq], N_HEADS).reshape(SWA_KV_HEADS, SWA_GROUP, s, HEAD_DIM)
        front = ((0, 0), (SWA_WINDOW, 0), (0, 0))
        kp = jnp.pad(_heads(qk[:, nq:], SWA_KV_HEADS), front)
        vp = jnp.pad(_heads(proj[:, nqk:].astype(BF16), SWA_KV_HEADS), front)
        sink = jnp.repeat(extra.reshape(SWA_KV_HEADS, SWA_GROUP), ATT_BLK, axis=1)[:, :, None]
        o4, lse = swa_fwd(q, kp, vp, sink, name=f"swa_fwd_{i}")
        a = _unheads(o4.reshape(N_HEADS, s, HEAD_DIM))
        sv.update(q=q, kp=kp, vp=vp, sink=sink, o4=o4, lse=lse)
    else:
        projp, h1 = mm_nn(x0, wg, lay["inp"], name=f"w_in_{i}", norm_gain=an)
        nqkv = 3 * N_HEADS * HEAD_DIM
        proj = _unchip_cols(projp, W_IN_COLS[2], W_IN_PAD[2])[:, :nqkv]
        fl = mm_plain(h1, _pad_cols(_forget_cols(wg, lay["inp"]), 128), name=f"w_forget_{i}")
        bp = _pad_cols(extra[None], 128)
        cum_t = fox_gate_fwd(fl, bp, name=f"gate_fwd_{i}")[:, :N_HEADS].T
        cq = cum_t[:, :, None]
        ck = cum_t.reshape(N_HEADS, s // min(ATT_BK, s), 1, min(ATT_BK, s))
        a, lse = fox_fwd(proj, cq, ck, name=f"fox_fwd_{i}")
        sv.update(proj=proj, fl=fl, bp=bp, cq=cq, ck=ck, lse=lse)
    x1 = mm_nn(a, wg, lay["out"], name=f"w_out_{i}", epi=_add_res, extras=(x0,), out_dtypes=(F32,))[0]
    u, r, h2 = mm_nn(x1, wg, lay["up"], name=f"w_up_{i}", epi=_relu2, out_dtypes=(BF16, BF16), norm_gain=mn)
    x2 = mm_nn(r, wg, lay["down"], name=f"w_down_{i}", epi=_add_res, extras=(x1,), out_dtypes=(F32,))[0]
    pp = mm_nn(p_bf, wg, lay["proj"], name=f"w_ple_proj_{i}", out_dtypes=(F32,))[0]
    x3, gl, h3 = mm_nn(x2, wg, lay["gate"], name=f"w_ple_gate_{i}", epi=_ple_fwd, extras=(x2, pp), out_dtypes=(F32, F32),
                       norm_gain=pn)
    sv.update(h1=h1, a=a, x1=x1, h2=h2, u=u, r=r, x2=x2, h3=h3, pp=pp, gl=gl)
    return x3, sv


def _layer_bwd(i, kind, dx3, sv, p_bf, wg, lay, n_rows, gains, tabs):
    s = dx3.shape[0]
    an, mn, pn = gains
    g = lax.empty((N_CHIPS, n_rows, SLAB), BF16)
    d_pp, d_gl = ew(_ple_bwd, [dx3, sv["pp"], sv["gl"]], [BF16, BF16], name=f"ple_bwd_{i}")
    g = mm_tn(p_bf, d_pp, g, lay["proj"], name=f"dw_ple_proj_{i}")
    g = mm_tn(sv["h3"], d_gl, g, lay["gate"], name=f"dw_ple_gate_{i}")
    d_h3 = mm_nt(d_gl, wg, lay["gate"], name=f"dx_ple_gate_{i}", out_dtypes=(F32,))[0]
    dx2, dx2b, d_pn = rms_bwd(sv["x2"], pn, d_h3, dx3, name=f"ple_norm_bwd_{i}")
    g = mm_tn(sv["r"], dx2b, g, lay["down"], name=f"dw_down_{i}")
    d_u = mm_nt(dx2b, wg, lay["down"], name=f"dx_down_{i}", epi=_relu2_bwd, extras=(sv["u"],))[0]
    g = mm_tn(sv["h2"], d_u, g, lay["up"], name=f"dw_up_{i}")
    dx1, dx1b, d_mn = mm_nt(d_u, wg, lay["up"], name=f"dx_up_{i}", rms=(sv["x1"], mn, dx2))
    g = mm_tn(sv["a"], dx1b, g, lay["out"], name=f"dw_out_{i}")
    d_a = mm_nt(dx1b, wg, lay["out"], name=f"dx_out_{i}")[0]
    d_extra = None
    if kind == 0:
        d_proj = jnp.concatenate(sb_bwd(sv["proj"], sv["tot"], d_a, name=f"sb_bwd_{i}"), axis=1)
    elif kind == 1:
        do4 = _heads(d_a, N_HEADS).reshape(SWA_KV_HEADS, SWA_GROUP, s, HEAD_DIM)
        dq, dkp, dvp, dsr = swa_bwd(sv["q"], sv["kp"], sv["vp"], sv["sink"], sv["o4"], sv["lse"], do4, name=f"swa_bwd_{i}")
        dqk = jnp.concatenate([_unheads(dq.reshape(N_HEADS, s, HEAD_DIM)), _unheads(dkp[:, SWA_WINDOW:])], axis=1)
        dqk = rope_bwd(dqk, tabs, name=f"rope_bwd_{i}")
        d_proj = jnp.concatenate([dqk, _unheads(dvp[:, SWA_WINDOW:]).astype(BF16)], axis=1)
        d_proj = _chip_cols(d_proj, W_IN_COLS[1], W_IN_PAD[1])
        d_extra = jnp.sum(dsr[..., 0], axis=2).reshape(N_HEADS)
    else:
        dq, dk, dv, dcq, dck = fox_bwd(sv["proj"], sv["a"], sv["lse"], sv["cq"], sv["ck"], d_a, name=f"fox_bwd_{i}")
        dcum = _pad_cols((dcq[:, :, 0] - dck.reshape(N_HEADS, s)).T, 128)
        dfl, dbp = fox_gate_bwd(dcum, sv["fl"], sv["bp"], name=f"gate_bwd_{i}")
        d_proj = jnp.concatenate([dq, dk, dv, dfl[:, :N_HEADS].astype(BF16)], axis=1)
        d_proj = _chip_cols(d_proj, W_IN_COLS[2], W_IN_PAD[2])
        d_extra = dbp[0, :N_HEADS]
    g = mm_tn(sv["h1"], d_proj, g, lay["inp"], name=f"dw_in_{i}")
    dx0, _, d_an = mm_nt(d_proj, wg, lay["inp"], name=f"dx_in_{i}", rms=(sv["x0"], an, dx1))
    return dx0, g, (d_an, d_mn, d_pn), d_extra


def _small_rows(a, prefix):
    rows = [a[f"{prefix}{n}_{i}"] for i in range(DEPTH) for n in GAINS] + [a[f"{prefix}final_norm"]]
    rows += [_pad_cols(a[f"{prefix}{n}"][None], D_MODEL)[0] for n in ("sinks_1", "b_forget_2")]
    return jnp.stack(rows + [jnp.zeros((D_MODEL,), F32)])


def _train_step(a):
    x = a["x"][0]
    tabs = rope_tables(x.shape[0])
    lays = [_layout(k) for k in MIXER]

    def natural(prefix, i, m):
        w = a[f"{prefix}{MAT_ARG[m]}_{i}"]
        return _pad_cols(w, W_IN_PAD[MIXER[i]]) if m == "inp" else w

    chip = 2 * lax.axis_index("x") + lax.axis_index("y")
    where = jnp.stack([chip, lax.axis_index("c")]).astype(jnp.int32)
    def own_block(i, zero):
        pk = jnp.concatenate([_to_slabs((natural("", i, m) + zero).astype(BF16)) for m in MATS], axis=0)
        return lax.dynamic_update_slice(lax.empty((N_CHIPS,) + pk.shape, BF16), pk[None], (chip, 0, 0))

    sends, recvs, bufs, token = allgather_start([own_block(0, 0.0)], name="allgather_start_0")
    more = allgather_start([own_block(i, token[0, 0]) for i in range(1, DEPTH)], name="allgather_start_1")
    sends, recvs, bufs, token = sends + more[0], recvs + more[1], bufs + more[2], more[3]

    gains = [tuple(a[f"{n}_{i}"][None] for n in GAINS) for i in range(DEPTH)]
    extras = [None, a["sinks_1"], a["b_forget_2"], None]
    p_bf = [a["p"][i, 0].astype(BF16) for i in range(DEPTH)]

    saved, wgs, after = [], [], token
    for i in range(DEPTH):
        landed = allgather_wait(bufs[i], sends[i], recvs[i], after, name=f"allgather_wait_{i}")
        wgs.append(allgather_forward(landed, name=f"allgather_forward_{i}"))
        x, sv = _layer_fwd(i, MIXER[i], x, p_bf[i], wgs[i], lays[i][0], gains[i], extras[i], tabs)
        saved.append(sv)
        after = x
    dx, d_final, loss = loss_head(x, a["final_norm"][None], a["loss_target"][0], name="loss_head")

    def finish(i, started, after):
        send, recv, part, land, _ = started
        part, got = scatter_wait(part, land, send, recv, after, name=f"scatter_wait_{i}")
        return join_halves([add_chips(part, got, where, name=f"add_chips_{i}")], name=f"join_halves_{i}")[0]

    small = [None] * N_SMALL
    small[12] = d_final[0]
    small[15] = _pad_cols(loss[:, :1], D_MODEL)[0]
    gfull = [None] * DEPTH
    started = None
    for i in reversed(range(DEPTH)):
        an, mn, pn = gains[i]
        if started is not None:
            pn = pn + started[4]
        dx, grad, d_gains, d_extra = _layer_bwd(i, MIXER[i], dx, saved[i], p_bf[i], wgs[i], lays[i][0], lays[i][1],
                                                (an, mn, pn), tabs)
        for j in range(3):
            small[3 * i + j] = d_gains[j][0]
        if d_extra is not None:
            small[12 + MIXER[i]] = _pad_cols(d_extra[None], D_MODEL)[0]
        if started is not None:
            gfull[i + 1] = finish(i + 1, started, dx)
        theirs = swap_halves([grad], name=f"swap_halves_{i}")[0]
        started = scatter_start(add_pairs(grad, theirs, where, name=f"add_pairs_{i}"), name=f"scatter_start_{i}")
    small = allreduce_small(jnp.stack(small), name="allreduce_small")

    out = {"loss": small[15, 0], "grad_x": dx[None]}
    res = adamw_small(small, _small_rows(a, ""), _small_rows(a, "m_"), _small_rows(a, "v_"), name="adamw_small")
    for i in reversed(range(DEPTH)):
        if i == 0:
            gfull[0] = finish(0, started, out[f"delta_{MAT_ARG[MATS[-1]]}_1"])
        for m in MATS:
            upd = adamw(gfull[i], lays[i][0][m], natural("", i, m), natural("m_", i, m), natural("v_", i, m), started[4],
                        name=f"adamw_{MAT_ARG[m]}_{i}")
            cols = a[f"{MAT_ARG[m]}_{i}"].shape[1]
            for kd, r in zip(KINDS, upd):
                out[f"{kd}{MAT_ARG[m]}_{i}"] = r[:, :cols]
    for kd, r in zip(KINDS, res):
        for i in range(DEPTH):
            for j, n in enumerate(GAINS):
                out[f"{kd}{n}_{i}"] = r[3 * i + j][0]
        out[f"{kd}final_norm"] = r[12][0]
        out[f"{kd}sinks_1"] = r[13][0, :N_HEADS]
        out[f"{kd}b_forget_2"] = r[14][0, :N_HEADS]
    return out


def _weight_names():
    names = []
    for i in range(DEPTH):
        names += [f"attn_norm_{i}", f"w_in_{i}", f"w_out_{i}"] + [[], ["sinks_1"], ["b_forget_2"]][MIXER[i]]
        names += [f"mlp_norm_{i}", f"w_up_{i}", f"w_down_{i}", f"ple_norm_{i}", f"w_ple_gate_{i}", f"w_ple_proj_{i}"]
    return names + ["final_norm"]


def kernel(x, p, attn_norm_0, w_in_0, w_out_0, mlp_norm_0, w_up_0, w_down_0, ple_norm_0, w_ple_gate_0, w_ple_proj_0, attn_norm_1, w_in_1, w_out_1, sinks_1, mlp_norm_1, w_up_1, w_down_1, ple_norm_1, w_ple_gate_1, w_ple_proj_1, attn_norm_2, w_in_2, w_out_2, b_forget_2, mlp_norm_2, w_up_2, w_down_2, ple_norm_2, w_ple_gate_2, w_ple_proj_2, attn_norm_3, w_in_3, w_out_3, mlp_norm_3, w_up_3, w_down_3, ple_norm_3, w_ple_gate_3, w_ple_proj_3, final_norm, loss_target, m_attn_norm_0, m_w_in_0, m_w_out_0, m_mlp_norm_0, m_w_up_0, m_w_down_0, m_ple_norm_0, m_w_ple_gate_0, m_w_ple_proj_0, m_attn_norm_1, m_w_in_1, m_w_out_1, m_sinks_1, m_mlp_norm_1, m_w_up_1, m_w_down_1, m_ple_norm_1, m_w_ple_gate_1, m_w_ple_proj_1, m_attn_norm_2, m_w_in_2, m_w_out_2, m_b_forget_2, m_mlp_norm_2, m_w_up_2, m_w_down_2, m_ple_norm_2, m_w_ple_gate_2, m_w_ple_proj_2, m_attn_norm_3, m_w_in_3, m_w_out_3, m_mlp_norm_3, m_w_up_3, m_w_down_3, m_ple_norm_3, m_w_ple_gate_3, m_w_ple_proj_3, m_final_norm, v_attn_norm_0, v_w_in_0, v_w_out_0, v_mlp_norm_0, v_w_up_0, v_w_down_0, v_ple_norm_0, v_w_ple_gate_0, v_w_ple_proj_0, v_attn_norm_1, v_w_in_1, v_w_out_1, v_sinks_1, v_mlp_norm_1, v_w_up_1, v_w_down_1, v_ple_norm_1, v_w_ple_gate_1, v_w_ple_proj_1, v_attn_norm_2, v_w_in_2, v_w_out_2, v_b_forget_2, v_mlp_norm_2, v_w_up_2, v_w_down_2, v_ple_norm_2, v_w_ple_gate_2, v_w_ple_proj_2, v_attn_norm_3, v_w_in_3, v_w_out_3, v_mlp_norm_3, v_w_up_3, v_w_down_3, v_ple_norm_3, v_w_ple_gate_3, v_w_ple_proj_3, v_final_norm):
    out = _train_step(dict(locals()))
    return (out["loss"], out["grad_x"], *[out[kd + n] for kd in KINDS for n in _weight_names()])
```

```python
import jax
import jax.numpy as jnp
from jax import lax
from jax.experimental import pallas as pl
from jax.experimental.pallas import tpu as pltpu

F32 = jnp.float32
BF16 = jnp.bfloat16

D_MODEL = 1024
N_HEADS = 16
HEAD_DIM = 64
SWA_KV_HEADS = 2
SWA_GROUP = 8
SWA_WINDOW = 128
ROPE_THETA = 500000.0
ROPE_DIM = 16
RMS_EPS = 1e-6
NEG_INF = -1e30
ATTN_SCALE = HEAD_DIM ** -0.5
N_CHIPS = 4
N_DEVICES = 8

SLAB = 256
ATT_BLK = 128
ATT_BQ = 512
ATT_BK = 512
ROW_TILE = 256
V7X_VMEM_LIMIT = 56 * 1024 * 1024

ADAM_LR, ADAM_B1, ADAM_B2, ADAM_EPS, ADAM_WD, ADAM_STEP = 0.001, 0.9, 0.999, 1e-08, 0.01, 10


def _cparams(sem=None):
    return pltpu.CompilerParams(dimension_semantics=sem, vmem_limit_bytes=V7X_VMEM_LIMIT)


def _dot(a, b):
    return jnp.dot(a, b, preferred_element_type=F32)


def _dot_nt(a, b):
    return lax.dot_general(a, b, (((1,), (1,)), ((), ())), preferred_element_type=F32)


def _dot_tn(a, b):
    return lax.dot_general(a, b, (((0,), (0,)), ((), ())), preferred_element_type=F32)


def _row_tile(M, K):
    return min(M, 1024) if K >= 1024 else M


def _finish(epi, acc, ex, outs):
    res = epi(acc, *[e[...] for e in ex]) if epi is not None else (acc,)
    for o, r in zip(outs, res):
        o[...] = r.astype(o.dtype)


def _once(shape, index_map):
    return pl.BlockSpec(shape, index_map, pipeline_mode=pl.Buffered(1))


def mm_nn(a, wg, t, *, name, epi=None, extras=(), out_dtypes=(BF16,), norm_gain=None):
    off, K, ns, row = t
    M = a.shape[0]
    sb = off // K
    ne, no = len(extras), len(out_dtypes)
    norm = norm_gain is not None
    if row:
        tm = M if norm else _row_tile(M, K)
        nb = N_CHIPS
        grid = (M // tm, ns)
        a_shape = (tm, N_CHIPS * K)
        a_spec = (_once if norm else pl.BlockSpec)(a_shape, lambda i, q: (i, 0))
        b_specs = [pl.BlockSpec((None, K, SLAB), lambda i, q, j=j: (j, sb + q, 0)) for j in range(nb)]
        tile = pl.BlockSpec((tm, SLAB), lambda i, q: (i, q))
        n_out = ns * SLAB
    else:
        nb = ns
        grid = (N_CHIPS,)
        a_shape = (M, K)
        a_spec = (_once if norm else pl.BlockSpec)(a_shape, lambda j: (0, 0))
        b_specs = [pl.BlockSpec((None, K, SLAB), lambda j, q=q: (j, sb + q, 0)) for q in range(ns)]
        tile = pl.BlockSpec((M, ns * SLAB), lambda j: (0, j))
        n_out = N_CHIPS * ns * SLAB

    def body(a_ref, *rest):
        if norm:
            g_ref, rest, h_out, h_ref = rest[0], rest[1:-2], rest[-2], rest[-1]

            @pl.when(pl.program_id(len(grid) - 1) == 0)
            def _():
                xv = a_ref[...]
                h_ref[...] = (xv * _rstd(xv) * g_ref[...]).astype(BF16)
                h_out[...] = h_ref[...]

            a_ref = h_ref
        bs, ex, outs = rest[:nb], rest[nb:nb + ne], rest[nb + ne:]
        if row:
            acc = _dot(a_ref[:, pl.ds(0, K)], bs[0][...])
            for j in range(1, nb):
                acc = acc + _dot(a_ref[:, pl.ds(j * K, K)], bs[j][...])
            _finish(epi, acc, ex, outs)
        else:
            av = a_ref[...]
            for q in range(ns):
                cols = pl.ds(q * SLAB, SLAB)
                _finish(epi, _dot(av, bs[q][...]), [e.at[:, cols] for e in ex], [o.at[:, cols] for o in outs])

    h_spec = _once(a_shape, (lambda i, q: (i, 0)) if row else (lambda j: (0, 0)))
    return pl.pallas_call(
        body, name=name, grid=grid,
        in_specs=[a_spec] + ([pl.BlockSpec(norm_gain.shape, lambda *_: (0, 0))] if norm else []) + b_specs + [tile] * ne,
        out_specs=[tile] * no + ([h_spec] if norm else []),
        out_shape=[jax.ShapeDtypeStruct((M, n_out), d) for d in out_dtypes]
        + ([jax.ShapeDtypeStruct(a.shape, BF16)] if norm else []),
        scratch_shapes=[pltpu.VMEM(a_shape, BF16)] if norm else [],
        compiler_params=_cparams((("arbitrary" if norm else "parallel"),) * len(grid)),
    )(a, *([norm_gain] if norm else []), *([wg] * nb), *extras)


def mm_nt(dy, wg, t, *, name, epi=None, extras=(), out_dtypes=(BF16,), rms=None):
    off, K, ns, row = t
    M = dy.shape[0]
    tm = _row_tile(M, K)
    sb = off // K
    if rms is not None:
        x, gain, dres = rms
        extras, out_dtypes = (x, dres), (F32, BF16)
    ne, no = len(extras), len(out_dtypes)
    grid = (M // tm, N_CHIPS)
    b_specs = [pl.BlockSpec((None, K, SLAB), lambda i, j, q=q: (j, sb + q, 0)) for q in range(ns)]
    if row:
        dy_spec = pl.BlockSpec((tm, ns * SLAB), lambda i, j: (i, 0))
        tile = pl.BlockSpec((tm, K), lambda i, j: (i, j))
        n_out = N_CHIPS * K
        sem = ("parallel", "parallel")
    else:
        dy_spec = pl.BlockSpec((tm, ns * SLAB), lambda i, j: (i, j))
        tile = pl.BlockSpec((tm, K), lambda i, j: (i, 0))
        n_out = K
        sem = ("arbitrary" if rms is not None else "parallel", "arbitrary")
    one = pl.BlockSpec((1, K), lambda i, j: (0, 0))

    def body(dy_ref, *rest):
        if rms is not None:
            g_ref, rest, dg_ref, acc_ref = rest[0], rest[1:-2], rest[-2], rest[-1]
            rest = rest + (acc_ref,)
        bs, ex, outs = rest[:ns], rest[ns:ns + ne], rest[ns + ne:ns + ne + no]
        part = _dot_nt(dy_ref[:, pl.ds(0, SLAB)], bs[0][...])
        for q in range(1, ns):
            part = part + _dot_nt(dy_ref[:, pl.ds(q * SLAB, SLAB)], bs[q][...])
        if row:
            _finish(epi, part, ex, outs)
        else:
            acc_ref = rest[-1]
            i, j = pl.program_id(0), pl.program_id(1)

            @pl.when(j == 0)
            def _():
                acc_ref[...] = part

            @pl.when(j > 0)
            def _():
                acc_ref[...] += part

            @pl.when(j == N_CHIPS - 1)
            def _():
                if rms is None:
                    _finish(epi, acc_ref[...], ex, outs)
                else:
                    dx, dg = _rms_bwd_tile(ex[0][...], g_ref[...], acc_ref[...])
                    dx = dx + ex[1][...]
                    outs[0][...] = dx
                    outs[1][...] = dx.astype(BF16)

                    @pl.when(i == 0)
                    def _():
                        dg_ref[...] = dg

                    @pl.when(i > 0)
                    def _():
                        dg_ref[...] += dg

    has = rms is not None
    return pl.pallas_call(
        body, name=name, grid=grid,
        in_specs=[dy_spec] + ([one] if has else []) + b_specs + [tile] * ne,
        out_specs=[tile] * no + ([one] if has else []),
        out_shape=[jax.ShapeDtypeStruct((M, n_out), d) for d in out_dtypes] + ([jax.ShapeDtypeStruct((1, K), F32)] if has else []),
        scratch_shapes=[] if row else [pltpu.VMEM((tm, K), F32)],
        compiler_params=_cparams(sem),
    )(dy, *([gain] if has else []), *([wg] * ns), *extras)


def mm_plain(a, b, *, name):
    M, K = a.shape
    N = b.shape[1]
    tm = min(M, 512)

    def body(a_ref, b_ref, o_ref):
        o_ref[...] = _dot(a_ref[...], b_ref[...])

    return pl.pallas_call(
        body, name=name, grid=(M // tm,),
        in_specs=[pl.BlockSpec((tm, K), lambda i: (i, 0)), pl.BlockSpec((K, N), lambda i: (0, 0))],
        out_specs=pl.BlockSpec((tm, N), lambda i: (i, 0)), out_shape=jax.ShapeDtypeStruct((M, N), F32),
        compiler_params=_cparams(("parallel",)),
    )(a, b)


def mm_tn(x, dy, g, t, *, name):
    off, K, ns, row = t
    S = x.shape[0]
    per = ns if off % (ns * K) == 0 else 1
    grid = (N_CHIPS, ns // per)
    if row:
        x_map = lambda j, q: (0, j)
        dy_map = lambda j, q: (0, q)
    else:
        x_map = lambda j, q: (0, 0)
        dy_map = lambda j, q: (0, j * (ns // per) + q)

    def body(g_in, x_ref, dy_ref, o_ref):
        del g_in
        xt = x_ref[...].T
        for q in range(per):
            o_ref[pl.ds(q * K, K), :] = _dot(xt, dy_ref[:, pl.ds(q * SLAB, SLAB)]).astype(o_ref.dtype)

    return pl.pallas_call(
        body, name=name, grid=grid,
        in_specs=[pl.BlockSpec(memory_space=pl.ANY), pl.BlockSpec((S, K), x_map), pl.BlockSpec((S, per * SLAB), dy_map)],
        out_specs=pl.BlockSpec((None, per * K, SLAB), lambda j, q: (j, off // (per * K) + q, 0)),
        out_shape=jax.ShapeDtypeStruct(g.shape, g.dtype),
        input_output_aliases={0: 0},
        compiler_params=_cparams(("parallel", "parallel")),
    )(g, x, dy)


def ew(fn, ins, out_dtypes, *, name, bcast=()):
    S = ins[0].shape[0]
    tr = min(ROW_TILE, S)
    cols = ins[0].shape[1]
    ni, nb = len(ins), len(bcast)

    def body(*refs):
        res = fn(*[r[...] for r in refs[:ni + nb]])
        for o, r in zip(refs[ni + nb:], res):
            o[...] = r.astype(o.dtype)

    return pl.pallas_call(
        body, name=name, grid=(S // tr,),
        in_specs=[pl.BlockSpec((tr, a.shape[1]), lambda i: (i, 0)) for a in ins]
        + [pl.BlockSpec(b.shape, lambda i: (0, 0)) for b in bcast],
        out_specs=[pl.BlockSpec((tr, cols), lambda i: (i, 0)) for _ in out_dtypes],
        out_shape=[jax.ShapeDtypeStruct((S, cols), d) for d in out_dtypes],
        compiler_params=_cparams(("parallel",)),
    )(*ins, *bcast)


def _rstd(x):
    return lax.rsqrt(jnp.mean(x * x, axis=-1, keepdims=True) + RMS_EPS)


def _sigmoid(x):
    return 1.0 / (1.0 + jnp.exp(-x))


def _log_sigmoid(z):
    return jnp.minimum(z, 0.0) - jnp.log(1.0 + jnp.exp(-jnp.abs(z)))


def _rms_bwd_tile(xv, gv, dh):
    rstd = _rstd(xv)
    xhat = xv * rstd
    gd = dh * gv
    dx = rstd * (gd - xhat * jnp.mean(xhat * gd, axis=-1, keepdims=True))
    return dx, jnp.sum(dh * xhat, axis=0, keepdims=True)


def rms_bwd(x, g, dh, dres, *, name):
    S, D = x.shape
    tr = min(ROW_TILE, S)

    def body(x_ref, g_ref, dh_ref, dres_ref, dx_ref, dxb_ref, dg_ref):
        i = pl.program_id(0)
        dx, dg = _rms_bwd_tile(x_ref[...], g_ref[...], dh_ref[...])
        dx = dx + dres_ref[...]
        dx_ref[...] = dx
        dxb_ref[...] = dx.astype(BF16)

        @pl.when(i == 0)
        def _():
            dg_ref[...] = dg

        @pl.when(i > 0)
        def _():
            dg_ref[...] += dg

    row = pl.BlockSpec((tr, D), lambda i: (i, 0))
    one = pl.BlockSpec((1, D), lambda i: (0, 0))
    return pl.pallas_call(
        body, name=name, grid=(S // tr,),
        in_specs=[row, one, row, row], out_specs=[row, row, one],
        out_shape=[jax.ShapeDtypeStruct((S, D), F32), jax.ShapeDtypeStruct((S, D), BF16),
                   jax.ShapeDtypeStruct((1, D), F32)],
        compiler_params=_cparams(("arbitrary",)),
    )(x, g, dh, dres)


def loss_head(x, g, target, *, name):
    S, D = x.shape
    tr = min(ROW_TILE, S)

    def body(x_ref, g_ref, t_ref, dx_ref, dg_ref, loss_ref):
        i = pl.program_id(0)
        xv, gv = x_ref[...], g_ref[...]
        err = xv * _rstd(xv) * gv - t_ref[...]
        part = 0.5 * jnp.sum(jnp.mean(err * err, axis=-1, keepdims=True), axis=0, keepdims=True)
        dx, dg = _rms_bwd_tile(xv, gv, err * (1.0 / D))
        dx_ref[...] = dx
        part = jnp.broadcast_to(part, loss_ref.shape)

        @pl.when(i == 0)
        def _():
            dg_ref[...] = dg
            loss_ref[...] = part

        @pl.when(i > 0)
        def _():
            dg_ref[...] += dg
            loss_ref[...] += part

    row = pl.BlockSpec((tr, D), lambda i: (i, 0))
    one = pl.BlockSpec((1, D), lambda i: (0, 0))
    return pl.pallas_call(
        body, name=name, grid=(S // tr,),
        in_specs=[row, one, row], out_specs=[row, one, pl.BlockSpec((1, 128), lambda i: (0, 0))],
        out_shape=[jax.ShapeDtypeStruct((S, D), F32), jax.ShapeDtypeStruct((1, D), F32),
                   jax.ShapeDtypeStruct((1, 128), F32)],
        compiler_params=_cparams(("arbitrary",)),
    )(x, g, target)


def rope_tables(S):
    half = ROPE_DIM // 2
    inv_freq = ROPE_THETA ** (-jnp.arange(half, dtype=F32) / half)
    ang = jnp.arange(S, dtype=F32)[:, None] * inv_freq[None, :]
    cos, sin = jnp.cos(ang), jnp.sin(ang)
    z = jnp.zeros((S, HEAD_DIM - ROPE_DIM), F32)
    zh = jnp.zeros((S, half), F32)
    c = jnp.concatenate([cos, cos, jnp.ones_like(z)], axis=1)
    sa = jnp.concatenate([zh, sin, z], axis=1)
    sb = jnp.concatenate([-sin, zh, z], axis=1)
    return [jnp.concatenate([t, t], axis=1) for t in (c, sa, sb)]


def _wide(t, n):
    return jnp.tile(t, (1, n // t.shape[1]))


def rope_fwd(xqk, tables, *, name):
    n, half = xqk.shape[1], ROPE_DIM // 2

    def fn(x, c, sa, sb):
        return (x * _wide(c, n) + pltpu.roll(x, half, 1) * _wide(sa, n) + pltpu.roll(x, n - half, 1) * _wide(sb, n),)

    return ew(fn, [xqk] + list(tables), [BF16], name=name)[0]


def rope_bwd(dy, tables, *, name):
    n, half = dy.shape[1], ROPE_DIM // 2

    def fn(d, c, sa, sb):
        return (d * _wide(c, n) + pltpu.roll(d * _wide(sa, n), n - half, 1) + pltpu.roll(d * _wide(sb, n), half, 1),)

    return ew(fn, [dy] + list(tables), [BF16], name=name)[0]


def _split3(x):
    h1 = x.astype(BF16)
    r1 = x - h1.astype(F32)
    h2 = r1.astype(BF16)
    return h1, h2, (r1 - h2.astype(F32)).astype(BF16)


def _tri(n, cmp):
    r = lax.broadcasted_iota(jnp.int32, (n, n), 0)
    c = lax.broadcasted_iota(jnp.int32, (n, n), 1)
    return cmp(r, c).astype(BF16)


def fox_gate_fwd(fl, b, *, name):
    S, W = fl.shape
    tr = min(ROW_TILE, S)

    def body(fl_ref, b_ref, cum_ref, carry):
        i = pl.program_id(0)

        @pl.when(i == 0)
        def _():
            carry[...] = jnp.zeros_like(carry)

        lower = _tri(tr, lambda r, c: r >= c)
        cs = carry[...]
        for piece in _split3(_log_sigmoid(fl_ref[...] + b_ref[...])):
            cs = cs + _dot(lower, piece)
        cum_ref[...] = cs
        carry[...] = cs[tr - 1:tr, :]

    return pl.pallas_call(
        body, name=name, grid=(S // tr,),
        in_specs=[pl.BlockSpec((tr, W), lambda i: (i, 0)), pl.BlockSpec((1, W), lambda i: (0, 0))],
        out_specs=pl.BlockSpec((tr, W), lambda i: (i, 0)),
        out_shape=jax.ShapeDtypeStruct((S, W), F32),
        scratch_shapes=[pltpu.VMEM((1, W), F32)],
        compiler_params=_cparams(("arbitrary",)),
    )(fl, b)


def fox_gate_bwd(dcum, fl, b, *, name):
    S, W = fl.shape
    tr = min(ROW_TILE, S)
    nb = S // tr

    def body(dc_ref, fl_ref, b_ref, dfl_ref, db_ref, carry):
        i = pl.program_id(0)

        @pl.when(i == 0)
        def _():
            carry[...] = jnp.zeros_like(carry)

        upper = _tri(tr, lambda r, c: r <= c)
        cs = carry[...]
        for piece in _split3(dc_ref[...]):
            cs = cs + _dot(upper, piece)
        carry[...] = cs[0:1, :]
        dfl = cs * _sigmoid(-(fl_ref[...] + b_ref[...]))
        dfl_ref[...] = dfl
        db = jnp.sum(dfl, axis=0, keepdims=True)

        @pl.when(i == 0)
        def _():
            db_ref[...] = db

        @pl.when(i > 0)
        def _():
            db_ref[...] += db

    rev = pl.BlockSpec((tr, W), lambda i: (nb - 1 - i, 0))
    one = pl.BlockSpec((1, W), lambda i: (0, 0))
    return pl.pallas_call(
        body, name=name, grid=(nb,),
        in_specs=[rev, rev, one], out_specs=[rev, one],
        out_shape=[jax.ShapeDtypeStruct((S, W), F32), jax.ShapeDtypeStruct((1, W), F32)],
        scratch_shapes=[pltpu.VMEM((1, W), F32)],
        compiler_params=_cparams(("arbitrary",)),
    )(dcum, fl, b)


def _blk_iota(tq, tk):
    return (lax.broadcasted_iota(jnp.int32, (tq, tk), 0), lax.broadcasted_iota(jnp.int32, (tq, tk), 1))


def _cs(xb, tri):
    return _dot(xb, tri)


def _rowsum(xb):
    return jnp.sum(xb.astype(F32), axis=1, keepdims=True)


def _sb_block(qs, k, cmr, shift):
    z = _dot_nt(qs, k)
    lb = jnp.minimum(z, 0.0) - jnp.log(1.0 + jnp.exp(-jnp.abs(z)))
    if cmr is None:
        return lb, (lb - z).astype(BF16), None
    strict = cmr < shift
    lom = jnp.where(strict, lb - z, 0.0).astype(BF16)
    return lb, lom, strict


def _keep(mask, x):
    return x if mask is None else jnp.where(mask, x, 0.0)


def _att_tiles(S):
    return min(ATT_BQ, S), min(ATT_BK, S)


PAIR = 2 * HEAD_DIM
N_PAIRS = N_HEADS // 2


def _pair_specs(S, tq):
    cols = D_MODEL // PAIR
    qspec = pl.BlockSpec((tq, PAIR), lambda p, i: (i, p))
    kspec = pl.BlockSpec((S, PAIR), lambda p, i: (0, cols + p))
    vspec = pl.BlockSpec((S, PAIR), lambda p, i: (0, 2 * cols + p))
    kvout = pl.BlockSpec((S, PAIR), lambda p, i: (0, p))
    vec = pl.BlockSpec((2, tq, 1), lambda p, i: (p, i, 0))
    return qspec, kspec, vspec, kvout, vec


def _head_lanes(h):
    lane = lax.broadcasted_iota(jnp.int32, (1, PAIR), 1)
    return (lane >= h * HEAD_DIM) & (lane < (h + 1) * HEAD_DIM)


def _only(sel, x):
    return jnp.where(sel, x, jnp.zeros_like(x))


def sb_fwd(proj, *, name):
    S = proj.shape[0]
    tq, tk = _att_tiles(S)
    qspec, kspec, vspec, _, vec = _pair_specs(S, tq)

    def body(q_ref, k_ref, v_ref, o_ref, t_ref):
        i = pl.program_id(1)
        row, col = _blk_iota(tq, tk)
        cmr = col - row
        below = _tri(tk, lambda r, c: r > c)
        nkb = (i + 1) * (tq // tk)
        out = []
        for h in range(2):
            sel = _head_lanes(h)
            qs = _only(sel, q_ref[...] * ATTN_SCALE)

            def step(n, carry, masked):
                r_sum, acc = carry
                kb = nkb - 1 - n
                ks = pl.multiple_of(kb * tk, tk)
                lb, lom, strict = _sb_block(qs, k_ref[pl.ds(ks, tk), :], cmr if masked else None, i * tq - kb * tk)
                w = _keep(strict, jnp.exp(lb + _cs(lom, below) + r_sum))
                acc = acc + _dot(w.astype(BF16), _only(sel, v_ref[pl.ds(ks, tk), :]))
                return r_sum + _rowsum(lom), acc

            nd = tq // tk
            carry = lax.fori_loop(0, nd, lambda n, c: step(n, c, True), (jnp.zeros((tq, 1), F32), jnp.zeros((tq, PAIR), F32)))
            r_sum, acc = lax.fori_loop(nd, nkb, lambda n, c: step(n, c, False), carry)
            t_ref[h] = r_sum
            out.append(acc)
        o_ref[...] = (out[0] + out[1]).astype(o_ref.dtype)

    return pl.pallas_call(
        body, name=name, grid=(N_PAIRS, S // tq),
        in_specs=[qspec, kspec, vspec], out_specs=[qspec, vec],
        out_shape=[jax.ShapeDtypeStruct((S, D_MODEL), BF16), jax.ShapeDtypeStruct((N_HEADS, S, 1), F32)],
        compiler_params=_cparams(("parallel", "arbitrary")),
    )(proj, proj, proj)


def sb_bwd(proj, tot, do, *, name):
    S = proj.shape[0]
    tq, tk = _att_tiles(S)
    qspec, kspec, vspec, kvout, vec = _pair_specs(S, tq)

    def body(q_ref, k_ref, v_ref, t_ref, do_ref, dq_ref, dk_out, dv_out, dk_ref, dv_ref):
        i = pl.program_id(1)

        @pl.when(i == 0)
        def _():
            dk_ref[...] = jnp.zeros_like(dk_ref)
            dv_ref[...] = jnp.zeros_like(dv_ref)

        row, col = _blk_iota(tq, tk)
        cmr = col - row
        upto = _tri(tk, lambda r, c: r <= c)
        before = _tri(tk, lambda r, c: r < c)
        out = []
        for h in range(2):
            sel = _head_lanes(h)
            qs, dov, t_all = _only(sel, q_ref[...] * ATTN_SCALE), _only(sel, do_ref[...]), t_ref[h]

            def step(kb, carry, masked):
                p_sum, e_sum, dq = carry
                ks = pl.multiple_of(kb * tk, tk)
                kv = k_ref[pl.ds(ks, tk), :]
                lb, lom, strict = _sb_block(qs, kv, cmr if masked else None, i * tq - kb * tk)
                tail = t_all - p_sum - _cs(lom, upto)
                w = _keep(strict, jnp.exp(lb + tail))
                e = _dot_nt(dov, v_ref[pl.ds(ks, tk), :]) * w
                eb = e.astype(BF16)
                e_before = e_sum + _cs(eb, before)
                beta = jnp.exp(lb)
                dzb = _keep(strict, e - (e + e_before) * beta).astype(BF16)
                dk_ref[pl.ds(ks, tk), :] += _dot_tn(dzb, qs)
                dv_ref[pl.ds(ks, tk), :] += _dot_tn(w.astype(BF16), dov)
                return p_sum + _rowsum(lom), e_sum + _rowsum(eb), dq + _dot(dzb, _only(sel, kv))

            zero = jnp.zeros((tq, 1), F32)
            nlow = i * (tq // tk)
            carry = lax.fori_loop(0, nlow, lambda kb, c: step(kb, c, False), (zero, zero, jnp.zeros((tq, PAIR), F32)))
            out.append(lax.fori_loop(nlow, nlow + tq // tk, lambda kb, c: step(kb, c, True), carry)[2])
        dq_ref[...] = ((out[0] + out[1]) * ATTN_SCALE).astype(dq_ref.dtype)

        @pl.when(i == S // tq - 1)
        def _():
            dk_out[...] = dk_ref[...].astype(dk_out.dtype)
            dv_out[...] = dv_ref[...].astype(dv_out.dtype)

    full = jax.ShapeDtypeStruct((S, D_MODEL), BF16)
    return pl.pallas_call(
        body, name=name, grid=(N_PAIRS, S // tq),
        in_specs=[qspec, kspec, vspec, vec, qspec], out_specs=[qspec, kvout, kvout],
        out_shape=[full, full, full],
        scratch_shapes=[pltpu.VMEM((S, PAIR), F32)] * 2,
        compiler_params=_cparams(("parallel", "arbitrary")),
    )(proj, proj, proj, tot, do)


def _fox_logits(qs, k, cq, ck, cmr, shift):
    s = _dot_nt(qs, k) + cq - ck
    if cmr is None:
        return s, None
    causal = cmr <= shift
    return jnp.where(causal, s, NEG_INF), causal


def fox_fwd(proj, cq, ck, *, name):
    S = proj.shape[0]
    tq, tk = _att_tiles(S)
    qspec, kspec, vspec, _, vec = _pair_specs(S, tq)
    ckspec = pl.BlockSpec((2, S // tk, 1, tk), lambda p, i: (p, 0, 0, 0))

    def body(q_ref, k_ref, v_ref, cq_ref, ck_ref, o_ref, lse_ref):
        i = pl.program_id(1)
        row, col = _blk_iota(tq, tk)
        cmr = col - row
        out = []
        for h in range(2):
            sel = _head_lanes(h)
            qs, cqv = _only(sel, q_ref[...] * ATTN_SCALE), cq_ref[h]

            def step(kb, carry, masked):
                m, l, acc = carry
                ks = pl.multiple_of(kb * tk, tk)
                s, _ = _fox_logits(qs, k_ref[pl.ds(ks, tk), :], cqv, ck_ref[h, kb], cmr if masked else None, i * tq - kb * tk)
                m_new = jnp.maximum(m, jnp.max(s, axis=1, keepdims=True))
                alpha = jnp.exp(m - m_new)
                p = jnp.exp(s - m_new)
                l = alpha * l + jnp.sum(p, axis=1, keepdims=True)
                acc = alpha * acc + _dot(p.astype(BF16), _only(sel, v_ref[pl.ds(ks, tk), :]))
                return m_new, l, acc

            nlow = i * (tq // tk)
            carry = lax.fori_loop(0, nlow, lambda kb, c: step(kb, c, False),
                                  (jnp.full((tq, 1), NEG_INF, F32), jnp.zeros((tq, 1), F32), jnp.zeros((tq, PAIR), F32)))
            m, l, acc = lax.fori_loop(nlow, nlow + tq // tk, lambda kb, c: step(kb, c, True), carry)
            lse_ref[h] = m + jnp.log(l)
            out.append(acc / l)
        o_ref[...] = (out[0] + out[1]).astype(o_ref.dtype)

    return pl.pallas_call(
        body, name=name, grid=(N_PAIRS, S // tq),
        in_specs=[qspec, kspec, vspec, vec, ckspec], out_specs=[qspec, vec],
        out_shape=[jax.ShapeDtypeStruct((S, D_MODEL), BF16), jax.ShapeDtypeStruct((N_HEADS, S, 1), F32)],
        compiler_params=_cparams(("parallel", "arbitrary")),
    )(proj, proj, proj, cq, ck)


def fox_bwd(proj, o, lse, cq, ck, do, *, name):
    S = proj.shape[0]
    tq, tk = _att_tiles(S)
    qspec, kspec, vspec, kvout, vec = _pair_specs(S, tq)
    ckspec = pl.BlockSpec((2, S // tk, 1, tk), lambda p, i: (p, 0, 0, 0))

    def body(q_ref, k_ref, v_ref, o_ref, lse_ref, cq_ref, ck_ref, do_ref, dq_ref, dk_out, dv_out, dcq_ref, dck_ref,
             dk_ref, dv_ref):
        i = pl.program_id(1)

        @pl.when(i == 0)
        def _():
            dk_ref[...] = jnp.zeros_like(dk_ref)
            dv_ref[...] = jnp.zeros_like(dv_ref)
            dck_ref[...] = jnp.zeros_like(dck_ref)

        row, col = _blk_iota(tq, tk)
        cmr = col - row
        out = []
        for h in range(2):
            sel = _head_lanes(h)
            qs, dov, cqv, lsev = _only(sel, q_ref[...] * ATTN_SCALE), _only(sel, do_ref[...]), cq_ref[h], lse_ref[h]
            delta = jnp.sum(dov.astype(F32) * o_ref[...].astype(F32), axis=1, keepdims=True)

            def step(kb, carry, masked):
                dq, dcq = carry
                ks = pl.multiple_of(kb * tk, tk)
                kv = k_ref[pl.ds(ks, tk), :]
                s, causal = _fox_logits(qs, kv, cqv, ck_ref[h, kb], cmr if masked else None, i * tq - kb * tk)
                p = _keep(causal, jnp.exp(s - lsev))
                ds = p * (_dot_nt(dov, v_ref[pl.ds(ks, tk), :]) - delta)
                dck_ref[h, kb] += jnp.sum(ds, axis=0, keepdims=True)
                dsb = ds.astype(BF16)
                dk_ref[pl.ds(ks, tk), :] += _dot_tn(dsb, qs)
                dv_ref[pl.ds(ks, tk), :] += _dot_tn(p.astype(BF16), dov)
                return dq + _dot(dsb, _only(sel, kv)), dcq + jnp.sum(ds, axis=1, keepdims=True)

            nlow = i * (tq // tk)
            carry = lax.fori_loop(0, nlow, lambda kb, c: step(kb, c, False),
                                  (jnp.zeros((tq, PAIR), F32), jnp.zeros((tq, 1), F32)))
            dq, dcq = lax.fori_loop(nlow, nlow + tq // tk, lambda kb, c: step(kb, c, True), carry)
            dcq_ref[h] = dcq
            out.append(dq)
        dq_ref[...] = ((out[0] + out[1]) * ATTN_SCALE).astype(dq_ref.dtype)

        @pl.when(i == S // tq - 1)
        def _():
            dk_out[...] = dk_ref[...].astype(dk_out.dtype)
            dv_out[...] = dv_ref[...].astype(dv_out.dtype)

    full = jax.ShapeDtypeStruct((S, D_MODEL), BF16)
    return pl.pallas_call(
        body, name=name, grid=(N_PAIRS, S // tq),
        in_specs=[qspec, kspec, vspec, qspec, vec, vec, ckspec, qspec],
        out_specs=[qspec, kvout, kvout, vec, ckspec],
        out_shape=[full, full, full, jax.ShapeDtypeStruct((N_HEADS, S, 1), F32),
                   jax.ShapeDtypeStruct((N_HEADS, S // tk, 1, tk), F32)],
        scratch_shapes=[pltpu.VMEM((S, PAIR), F32)] * 2,
        compiler_params=_cparams(("parallel", "arbitrary")),
    )(proj, proj, proj, o, lse, cq, ck, do)


def _swa_specs(S, tq):
    qspec = pl.BlockSpec((None, SWA_GROUP, tq, HEAD_DIM), lambda g, i: (g, 0, i, 0))
    kvspec = pl.BlockSpec((None, S + SWA_WINDOW, HEAD_DIM), lambda g, i: (g, 0, 0))
    vec = pl.BlockSpec((None, SWA_GROUP, tq, 1), lambda g, i: (g, 0, i, 0))
    sink = pl.BlockSpec((None, SWA_GROUP * tq, 1), lambda g, i: (g, 0, 0))
    return qspec, kvspec, vec, sink


def _swa_logits(q2, kw, i, tq):
    rows = q2.shape[0]
    r = lax.broadcasted_iota(jnp.int32, (rows, 2 * tq), 0)
    c = lax.broadcasted_iota(jnp.int32, (rows, 2 * tq), 1)
    diff = (r & (tq - 1)) + tq - c
    ok = (diff >= 0) & (diff < SWA_WINDOW) & (c + (i - 1) * tq >= 0)
    return jnp.where(ok, _dot_nt(q2, kw) * ATTN_SCALE, NEG_INF), ok


def swa_fwd(q, kp, vp, sink, *, name):
    _, G, S, _ = q.shape
    tq = ATT_BLK
    qspec, kvspec, vec, sinkspec = _swa_specs(S, tq)

    def body(q_ref, k_ref, v_ref, s_ref, o_ref, lse_ref):
        i = pl.program_id(1)
        q2 = q_ref[...].reshape(G * tq, HEAD_DIM)
        ws = pl.multiple_of(i * tq, tq)
        logits, _ = _swa_logits(q2, k_ref[pl.ds(ws, 2 * tq), :], i, tq)
        sk = s_ref[...]
        m = jnp.maximum(jnp.max(logits, axis=1, keepdims=True), sk)
        e = jnp.exp(logits - m)
        den = jnp.sum(e, axis=1, keepdims=True) + jnp.exp(sk - m)
        o = _dot((e / den).astype(BF16), v_ref[pl.ds(ws, 2 * tq), :])
        o_ref[...] = o.reshape(G, tq, HEAD_DIM).astype(o_ref.dtype)
        lse_ref[...] = (m + jnp.log(den)).reshape(G, tq, 1)

    return pl.pallas_call(
        body, name=name, grid=(SWA_KV_HEADS, S // tq),
        in_specs=[qspec, kvspec, kvspec, sinkspec], out_specs=[qspec, vec],
        out_shape=[jax.ShapeDtypeStruct(q.shape, BF16), jax.ShapeDtypeStruct((SWA_KV_HEADS, G, S, 1), F32)],
        compiler_params=_cparams(("parallel", "arbitrary")),
    )(q, kp, vp, sink)


def swa_bwd(q, kp, vp, sink, o, lse, do, *, name):
    _, G, S, _ = q.shape
    tq = ATT_BLK
    qspec, kvspec, vec, sinkspec = _swa_specs(S, tq)

    def body(q_ref, k_ref, v_ref, s_ref, o_ref, lse_ref, do_ref, dq_ref, dk_ref, dv_ref, dsink_ref):
        i = pl.program_id(1)

        @pl.when(i == 0)
        def _():
            dk_ref[...] = jnp.zeros_like(dk_ref)
            dv_ref[...] = jnp.zeros_like(dv_ref)

        q2 = q_ref[...].reshape(G * tq, HEAD_DIM)
        do2 = do_ref[...].reshape(G * tq, HEAD_DIM)
        o2 = o_ref[...].reshape(G * tq, HEAD_DIM)
        lse2 = lse_ref[...].reshape(G * tq, 1)
        ws = pl.multiple_of(i * tq, tq)
        kw = k_ref[pl.ds(ws, 2 * tq), :]
        vw = v_ref[pl.ds(ws, 2 * tq), :]
        logits, ok = _swa_logits(q2, kw, i, tq)
        p = jnp.where(ok, jnp.exp(logits - lse2), 0.0)
        delta = jnp.sum(do2.astype(F32) * o2.astype(F32), axis=1, keepdims=True)
        ds = p * (_dot_nt(do2, vw) - delta)
        dsb = ds.astype(BF16)
        dq_ref[...] = (_dot(dsb, kw) * ATTN_SCALE).reshape(G, tq, HEAD_DIM)
        dk_ref[pl.ds(ws, 2 * tq), :] += _dot_tn(dsb, q2) * ATTN_SCALE
        dv_ref[pl.ds(ws, 2 * tq), :] += _dot_tn(p.astype(BF16), do2)
        dsink_ref[...] = (-jnp.exp(s_ref[...] - lse2) * delta).reshape(G, tq, 1)

    kvshape = jax.ShapeDtypeStruct(kp.shape, F32)
    return pl.pallas_call(
        body, name=name, grid=(SWA_KV_HEADS, S // tq),
        in_specs=[qspec, kvspec, kvspec, sinkspec, qspec, vec, qspec],
        out_specs=[qspec, kvspec, kvspec, vec],
        out_shape=[jax.ShapeDtypeStruct(q.shape, F32), kvshape, kvshape,
                   jax.ShapeDtypeStruct((SWA_KV_HEADS, G, S, 1), F32)],
        compiler_params=_cparams(("parallel", "arbitrary")),
    )(q, kp, vp, sink, o, lse, do)


def _adamw_tile(w, g, m, v):
    m = ADAM_B1 * m + (1.0 - ADAM_B1) * g
    v = ADAM_B2 * v + (1.0 - ADAM_B2) * (g * g)
    m_hat = m / (1.0 - ADAM_B1 ** ADAM_STEP)
    v_hat = v / (1.0 - ADAM_B2 ** ADAM_STEP)
    delta = -ADAM_LR * (m_hat / (jnp.sqrt(v_hat) + ADAM_EPS) + ADAM_WD * w)
    return g, delta, m, v


def adamw(gfull, t, w, m, v, after, *, name):
    off, K, ns, _ = t
    sb = off // K
    nat = pl.BlockSpec((K, SLAB), lambda q: (0, q))

    def body(g_ref, w_ref, m_ref, v_ref, after_ref, *outs):
        del after_ref
        for o, r in zip(outs, _adamw_tile(w_ref[...], g_ref[...], m_ref[...], v_ref[...])):
            o[...] = r

    return pl.pallas_call(
        body, name=name, grid=(ns,),
        in_specs=[pl.BlockSpec((K, SLAB), lambda q: (sb + q, 0)), nat, nat, nat, HBM],
        out_specs=[nat] * 4, out_shape=[jax.ShapeDtypeStruct(w.shape, F32)] * 4,
        compiler_params=_cparams(("parallel",)),
    )(gfull, w, m, v, after)


def adamw_small(g, w, m, v, *, name):
    rows, d = w.shape

    def body(g_ref, w_ref, m_ref, v_ref, *outs):
        for j in range(rows):
            one = pl.ds(j, 1)
            for k, r in enumerate(_adamw_tile(w_ref[one, :], g_ref[one, :], m_ref[one, :], v_ref[one, :])):
                outs[k * rows + j][...] = r

    flat = pl.pallas_call(body, name=name, out_shape=[jax.ShapeDtypeStruct((1, d), F32)] * (4 * rows))(g, w, m, v)
    return [flat[k * rows:(k + 1) * rows] for k in range(4)]


MESH = pl.DeviceIdType.MESH
HBM = pl.BlockSpec(memory_space=pl.ANY)


def _place():
    x, y, c = lax.axis_index("x"), lax.axis_index("y"), lax.axis_index("c")
    others = [(1 - x, y), (x, 1 - y), (1 - x, 1 - y)]
    return x, y, c, others


def _rcopy(src, dst, send_sems, recv_sems, k, to):
    return pltpu.make_async_remote_copy(src_ref=src, dst_ref=dst, send_sem=send_sems.at[k], recv_sem=recv_sems.at[k],
                                        device_id=to, device_id_type=MESH)


def _dma_sems(*counts):
    return [pltpu.SemaphoreType.DMA((n,)) for n in counts]


DMA_UNIT_ROWS = 128
DMA_PIECES = 4


def _row_pieces(h, n):
    units = h // DMA_UNIT_ROWS
    n = min(n, units)
    base, extra = divmod(units, n)
    sizes = [(base + (k < extra)) * DMA_UNIT_ROWS for k in range(n)]
    return [(sum(sizes[:k]), sizes[k]) for k in range(n)]


def _start_pieces(make, h, n):
    for s0, sz in _row_pieces(h, n):
        make(s0, sz).start()
    return make(0, h)


SEM = pl.BlockSpec(memory_space=pltpu.SEMAPHORE)
SPLIT_COPY = pltpu.CompilerParams(has_side_effects=pltpu.SideEffectType.DATAFLOW_SIDE_EFFECTING)
N_OTHERS = 3


def _hbm(a):
    return pltpu.with_memory_space_constraint(a, pltpu.HBM)


def _chip_rows(buf, chip, s0, sz):
    return buf.at[2 * chip[0] + chip[1], pl.ds(s0, sz)]


def allgather_start(bufs, *, name):
    n = len(bufs)

    def body(*refs):
        ins, send, recv, token = refs[:n], refs[n:2 * n], refs[2 * n:3 * n], refs[4 * n]
        x, y, c, others = _place()
        for i in range(n):
            h = bufs[i].shape[1] // 2
            for f, chip in enumerate(others):
                for s0, sz in _row_pieces(h, DMA_PIECES):
                    mine = _chip_rows(ins[i], (x, y), c * h + s0, sz)
                    _rcopy(mine, mine, send[i], recv[i], f, (*chip, c)).start()
        token[...] = jnp.zeros_like(token)

    res = pl.pallas_call(
        body, name=name, in_specs=[HBM] * n,
        out_specs=[SEM] * (2 * n) + [HBM] * n + [pl.BlockSpec(memory_space=pltpu.VMEM)],
        out_shape=[pltpu.SemaphoreType.DMA((N_OTHERS,))] * (2 * n) + [pltpu.HBM(b.shape, b.dtype) for b in bufs]
        + [jax.ShapeDtypeStruct((1, D_MODEL), F32)],
        input_output_aliases={i: 2 * n + i for i in range(n)},
        compiler_params=SPLIT_COPY,
    )(*[_hbm(b) for b in bufs])
    return res[:n], res[n:2 * n], res[2 * n:3 * n], res[3 * n]


def allgather_wait(buf, send, recv, after, *, name):
    h = buf.shape[1] // 2

    def body(buf_ref, send_sems, recv_sems, after_ref, out_ref):
        del after_ref, out_ref
        x, y, c, others = _place()
        for f, chip in enumerate(others):
            mine = _chip_rows(buf_ref, (x, y), c * h, h)
            theirs = _chip_rows(buf_ref, chip, c * h, h)
            cp = _rcopy(mine, theirs, send_sems, recv_sems, f, (*chip, c))
            cp.wait_send()
            cp.wait_recv()

    return pl.pallas_call(
        body, name=name, in_specs=[HBM, SEM, SEM, HBM], out_specs=HBM,
        out_shape=pltpu.HBM(buf.shape, buf.dtype), input_output_aliases={0: 0},
        compiler_params=SPLIT_COPY,
    )(buf, send, recv, after)


def allgather_forward(buf, *, name):
    h = buf.shape[1] // 2

    def body(in_ref, out_ref, send_sems, recv_sems):
        del in_ref
        x, y, c, others = _place()
        sibling = (x, y, 1 - c)
        sent = []
        for f, chip in enumerate(others):
            sent.append(_start_pieces(
                lambda s0, sz: _rcopy(_chip_rows(out_ref, chip, c * h + s0, sz), _chip_rows(out_ref, chip, c * h + s0, sz),
                                      send_sems, recv_sems, f, sibling), h, DMA_PIECES))
        for f, chip in enumerate(others):
            blk = _chip_rows(out_ref, chip, (1 - c) * h, h)
            _rcopy(blk, blk, send_sems, recv_sems, f, sibling).wait_recv()
        for cp in sent:
            cp.wait_send()

    return pl.pallas_call(
        body, name=name, in_specs=[HBM], out_specs=HBM,
        out_shape=jax.ShapeDtypeStruct(buf.shape, buf.dtype), input_output_aliases={0: 0},
        scratch_shapes=_dma_sems(N_OTHERS, N_OTHERS),
    )(buf)


def _sem1():
    return pltpu.SemaphoreType.DMA((1,))


TOKEN = jax.ShapeDtypeStruct((1, D_MODEL), F32)
VMEM_SPEC = pl.BlockSpec(memory_space=pltpu.VMEM)


def swap_start(grad, *, name):
    h = grad.shape[1] // 2

    def body(g_ref, land_ref, send, recv, g_out, land_out, token):
        del g_out, land_out
        x, y, c, _ = _place()
        for k in range(N_CHIPS):
            for s0, sz in _row_pieces(h, DMA_PIECES):
                _rcopy(g_ref.at[k, pl.ds((1 - c) * h + s0, sz)], land_ref.at[k, pl.ds(s0, sz)], send, recv, 0, (x, y, 1 - c)).start()
        token[...] = jnp.zeros_like(token)

    land = lax.empty((N_CHIPS, h, SLAB), grad.dtype)
    return pl.pallas_call(
        body, name=name, in_specs=[HBM, HBM], out_specs=[SEM, SEM, HBM, HBM, VMEM_SPEC],
        out_shape=[_sem1(), _sem1(), pltpu.HBM(grad.shape, grad.dtype), pltpu.HBM(land.shape, land.dtype), TOKEN],
        input_output_aliases={0: 2, 1: 3}, compiler_params=SPLIT_COPY,
    )(_hbm(grad), _hbm(land))


def swap_wait(grad, land, send, recv, after, *, name):
    h = land.shape[1]

    def body(g_ref, land_ref, send_sems, recv_sems, after_ref, g_out, land_out):
        del after_ref, g_out, land_out
        x, y, c, _ = _place()
        cp = _rcopy(g_ref.at[:, pl.ds((1 - c) * h, h)], land_ref, send_sems, recv_sems, 0, (x, y, 1 - c))
        cp.wait_send()
        cp.wait_recv()

    return pl.pallas_call(
        body, name=name, in_specs=[HBM, HBM, SEM, SEM, HBM], out_specs=[HBM, HBM],
        out_shape=[pltpu.HBM(grad.shape, grad.dtype), pltpu.HBM(land.shape, land.dtype)],
        input_output_aliases={0: 0, 1: 1}, compiler_params=SPLIT_COPY,
    )(grad, land, send, recv, after)


def scatter_start(part, *, name):
    h = part.shape[1]

    def body(part_ref, land_ref, send, recv, part_out, land_out, token):
        del part_out, land_out
        x, y, c, others = _place()
        for f, chip in enumerate(others):
            for s0, sz in _row_pieces(h, DMA_PIECES):
                _rcopy(_chip_rows(part_ref, chip, s0, sz), land_ref.at[f, pl.ds(s0, sz)], send, recv, f, (*chip, c)).start()
        token[...] = jnp.zeros_like(token)

    land = lax.empty((N_OTHERS,) + part.shape[1:], part.dtype)
    return pl.pallas_call(
        body, name=name, in_specs=[HBM, HBM],
        out_specs=[SEM, SEM, HBM, HBM, pl.BlockSpec(memory_space=pltpu.VMEM)],
        out_shape=[pltpu.SemaphoreType.DMA((N_OTHERS,))] * 2 + [pltpu.HBM(part.shape, part.dtype), pltpu.HBM(land.shape, land.dtype),
                                                                 jax.ShapeDtypeStruct((1, D_MODEL), F32)],
        input_output_aliases={0: 2, 1: 3},
        compiler_params=SPLIT_COPY,
    )(_hbm(part), _hbm(land))


def scatter_wait(part, land, send, recv, after, *, name):
    h = part.shape[1]

    def body(part_ref, land_ref, send_sems, recv_sems, after_ref, part_out, land_out):
        del after_ref, part_out, land_out
        x, y, c, others = _place()
        for f, chip in enumerate(others):
            cp = _rcopy(_chip_rows(part_ref, chip, 0, h), land_ref.at[f], send_sems, recv_sems, f, (*chip, c))
            cp.wait_send()
            cp.wait_recv()

    return pl.pallas_call(
        body, name=name, in_specs=[HBM, HBM, SEM, SEM, HBM], out_specs=[HBM, HBM],
        out_shape=[pltpu.HBM(part.shape, part.dtype), pltpu.HBM(land.shape, land.dtype)],
        input_output_aliases={0: 0, 1: 1},
        compiler_params=SPLIT_COPY,
    )(part, land, send, recv, after)


def join_start(buf, *, name):
    h = buf.shape[0] // 2

    def body(b_ref, send, recv, b_out, token):
        del b_out
        x, y, c, _ = _place()
        for s0, sz in _row_pieces(h, 2 * DMA_PIECES):
            rows = b_ref.at[pl.ds(c * h + s0, sz)]
            _rcopy(rows, rows, send, recv, 0, (x, y, 1 - c)).start()
        token[...] = jnp.zeros_like(token)

    return pl.pallas_call(
        body, name=name, in_specs=[HBM], out_specs=[SEM, SEM, HBM, VMEM_SPEC],
        out_shape=[_sem1(), _sem1(), pltpu.HBM(buf.shape, buf.dtype), TOKEN],
        input_output_aliases={0: 2}, compiler_params=SPLIT_COPY,
    )(_hbm(buf))


def join_wait(buf, send, recv, after, *, name):
    h = buf.shape[0] // 2

    def body(b_ref, send_sems, recv_sems, after_ref, b_out):
        del after_ref, b_out
        x, y, c, _ = _place()
        cp = _rcopy(b_ref.at[pl.ds(c * h, h)], b_ref.at[pl.ds((1 - c) * h, h)], send_sems, recv_sems, 0, (x, y, 1 - c))
        cp.wait_send()
        cp.wait_recv()

    return pl.pallas_call(
        body, name=name, in_specs=[HBM, SEM, SEM, HBM], out_specs=HBM,
        out_shape=pltpu.HBM(buf.shape, buf.dtype), input_output_aliases={0: 0}, compiler_params=SPLIT_COPY,
    )(buf, send, recv, after)


def allreduce_small(v, *, name):
    rows, n = v.shape

    def body(x_ref, sum_ref, all_ref, send_sems, recv_sems, local_sem):
        x, y, c, others = _place()
        me, sibling = (x, y, c), (x, y, 1 - c)

        def blk(px, py, pc):
            return all_ref.at[pl.ds((4 * px + 2 * py + pc) * rows, rows), :]

        def copy(k, block, to, src=None):
            return _rcopy(blk(*block) if src is None else src, blk(*block), send_sems, recv_sems, k, to)

        mine = pltpu.make_async_copy(x_ref, blk(*me), local_sem)
        mine.start()
        first = [copy(0, me, sibling, src=x_ref)]
        first += [copy(1 + f, me, (*chip, c), src=x_ref) for f, chip in enumerate(others)]
        for cp in first:
            cp.start()
        passed = [copy(4 + f, (*chip, c), sibling) for f, chip in enumerate(others)]
        for f, chip in enumerate(others):
            copy(1 + f, (*chip, c), me).wait_recv()
            passed[f].start()
        copy(0, sibling, me).wait_recv()
        for f, chip in enumerate(others):
            copy(4 + f, (*chip, 1 - c), me).wait_recv()
        for cp in first + passed:
            cp.wait_send()
        mine.wait()
        acc = all_ref[pl.ds(0, rows), :]
        for d in range(1, N_DEVICES):
            acc = acc + all_ref[pl.ds(d * rows, rows), :]
        sum_ref[...] = acc

    vm = pl.BlockSpec(memory_space=pltpu.VMEM)
    return pl.pallas_call(
        body, name=name, in_specs=[vm], out_specs=[vm, vm],
        out_shape=[jax.ShapeDtypeStruct((rows, n), F32), jax.ShapeDtypeStruct((N_DEVICES * rows, n), F32)],
        scratch_shapes=_dma_sems(7, 7) + [pltpu.SemaphoreType.DMA],
    )(v)[0]


def add_pairs(grad, theirs, where, *, name):
    h = theirs.shape[1]
    spec = pl.BlockSpec((None, h, SLAB), lambda k, w: (k, 0, 0))

    def body(w_ref, a_ref, b_ref, o_ref):
        del w_ref
        o_ref[...] = (a_ref[...].astype(F32) + b_ref[...].astype(F32)).astype(o_ref.dtype)

    return pl.pallas_call(
        body, name=name,
        grid_spec=pltpu.PrefetchScalarGridSpec(
            num_scalar_prefetch=1, grid=(N_CHIPS,),
            in_specs=[pl.BlockSpec((None, h, SLAB), lambda k, w: (k, w[1], 0)), spec], out_specs=spec),
        out_shape=jax.ShapeDtypeStruct(theirs.shape, theirs.dtype),
        compiler_params=_cparams(("parallel",)))(where, grad, theirs)


def add_chips(pair, got, where, *, name):
    h = pair.shape[1]
    tr = h // 2

    def body(w_ref, a_ref, b_ref, o_ref):
        del w_ref
        acc = a_ref[...].astype(F32)
        for f in range(3):
            acc = acc + b_ref[f].astype(F32)
        o_ref[...] = acc

    return pl.pallas_call(
        body, name=name,
        grid_spec=pltpu.PrefetchScalarGridSpec(
            num_scalar_prefetch=1, grid=(2,),
            in_specs=[pl.BlockSpec((None, tr, SLAB), lambda i, w: (w[0], i, 0)),
                      pl.BlockSpec((3, tr, SLAB), lambda i, w: (0, i, 0))],
            out_specs=pl.BlockSpec((tr, SLAB), lambda i, w: (2 * w[1] + i, 0))),
        out_shape=jax.ShapeDtypeStruct((2 * h, SLAB), F32),
        compiler_params=_cparams(("parallel",)))(where, pair, got)


DEPTH = 4
MIXER = (0, 1, 2, 0)
W_IN_COLS = (768, 320, 772)
W_IN_PAD = (768, 512, 1024)
MATS = ("up", "down", "inp", "out", "gate", "proj")
MAT_ARG = dict(up="w_up", down="w_down", inp="w_in", out="w_out", gate="w_ple_gate", proj="w_ple_proj")
GAINS = ("attn_norm", "mlp_norm", "ple_norm")
N_SMALL = 16
KINDS = ("grad_", "delta_", "new_m_", "new_v_")


def _layout(kind):
    ns_in = W_IN_PAD[kind] // SLAB
    off = 8192 + 1024 * ns_in
    lay = dict(up=(0, 1024, 4, False), down=(4096, 1024, 4, True), inp=(8192, 1024, ns_in, False),
               out=(off, 256, 4, True), gate=(off + 1024, 256, 4, True), proj=(off + 2048, 256, 1, False))
    return lay, off + 2304


def _to_slabs(w):
    k, c = w.shape
    return w.reshape(k, c // SLAB, SLAB).transpose(1, 0, 2).reshape(-1, SLAB)


def _pad_cols(w, n):
    return jnp.pad(w, ((0, 0), (0, n - w.shape[1])))


def _heads(x2d, n):
    return x2d.reshape(x2d.shape[0], n, HEAD_DIM).transpose(1, 0, 2)


def _unheads(x3d):
    n, s, _ = x3d.shape
    return x3d.transpose(1, 0, 2).reshape(s, n * HEAD_DIM)


def _chip_cols(x2d, c, cpad):
    return jnp.concatenate([_pad_cols(x2d[:, j * c:(j + 1) * c], cpad) for j in range(N_CHIPS)], axis=1)


def _unchip_cols(x2d, c, cpad):
    return jnp.concatenate([x2d[:, j * cpad:j * cpad + c] for j in range(N_CHIPS)], axis=1)


def _forget_cols(wg, t):
    off, K, _, _ = t
    cols = []
    for g in range(3 * N_HEADS * HEAD_DIM, 3 * N_HEADS * HEAD_DIM + N_HEADS):
        chip, local = divmod(g, W_IN_COLS[2])
        q, c = divmod(local, SLAB)
        cols.append(wg[chip, off + q * K:off + (q + 1) * K, c:c + 1])
    return jnp.concatenate(cols, axis=1)


def _add_res(acc, res):
    return (acc + res,)


def _relu2(acc):
    return acc, jnp.square(jnp.maximum(acc, 0.0))


def _relu2_bwd(acc, u):
    return (acc * (2.0 * jnp.maximum(u.astype(F32), 0.0)),)


def _ple_fwd(acc, x2, pp):
    return x2 + pp * _sigmoid(acc), acc


def _ple_bwd(dx, pp, gl):
    gate = _sigmoid(gl)
    return dx * gate, dx * pp * gate * (1.0 - gate)


def _layer_fwd(i, kind, x0, p_bf, wg, lay, gains, extra, tabs):
    s = x0.shape[0]
    an, mn, pn = gains
    sv = dict(x0=x0)
    if kind == 0:
        proj, h1 = mm_nn(x0, wg, lay["inp"], name=f"w_in_{i}", norm_gain=an)
        a, tot = sb_fwd(proj, name=f"sb_fwd_{i}")
        sv.update(proj=proj, tot=tot)
    elif kind == 1:
        projp, h1 = mm_nn(x0, wg, lay["inp"], name=f"w_in_{i}", out_dtypes=(F32,), norm_gain=an)
        proj = _unchip_cols(projp, W_IN_COLS[1], W_IN_PAD[1])
        nq = N_HEADS * HEAD_DIM
        nqk = nq + SWA_KV_HEADS * HEAD_DIM
        qk = rope_fwd(proj[:, :nqk], tabs, name=f"rope_{i}")
        q = _heads(qk[:, :nq], N_HEADS).reshape(SWA_KV_HEADS, SWA_GROUP, s, HEAD_DIM)
        front = ((0, 0), (SWA_WINDOW, 0), (0, 0))
        kp = jnp.pad(_heads(qk[:, nq:], SWA_KV_HEADS), front)
        vp = jnp.pad(_heads(proj[:, nqk:].astype(BF16), SWA_KV_HEADS), front)
        sink = jnp.repeat(extra.reshape(SWA_KV_HEADS, SWA_GROUP), ATT_BLK, axis=1)[:, :, None]
        o4, lse = swa_fwd(q, kp, vp, sink, name=f"swa_fwd_{i}")
        a = _unheads(o4.reshape(N_HEADS, s, HEAD_DIM))
        sv.update(q=q, kp=kp, vp=vp, sink=sink, o4=o4, lse=lse)
    else:
        projp, h1 = mm_nn(x0, wg, lay["inp"], name=f"w_in_{i}", norm_gain=an)
        nqkv = 3 * N_HEADS * HEAD_DIM
        proj = _unchip_cols(projp, W_IN_COLS[2], W_IN_PAD[2])[:, :nqkv]
        fl = mm_plain(h1, _pad_cols(_forget_cols(wg, lay["inp"]), 128), name=f"w_forget_{i}")
        bp = _pad_cols(extra[None], 128)
        cum_t = fox_gate_fwd(fl, bp, name=f"gate_fwd_{i}")[:, :N_HEADS].T
        cq = cum_t[:, :, None]
        ck = cum_t.reshape(N_HEADS, s // min(ATT_BK, s), 1, min(ATT_BK, s))
        a, lse = fox_fwd(proj, cq, ck, name=f"fox_fwd_{i}")
        sv.update(proj=proj, fl=fl, bp=bp, cq=cq, ck=ck, lse=lse)
    x1 = mm_nn(a, wg, lay["out"], name=f"w_out_{i}", epi=_add_res, extras=(x0,), out_dtypes=(F32,))[0]
    u, r, h2 = mm_nn(x1, wg, lay["up"], name=f"w_up_{i}", epi=_relu2, out_dtypes=(BF16, BF16), norm_gain=mn)
    x2 = mm_nn(r, wg, lay["down"], name=f"w_down_{i}", epi=_add_res, extras=(x1,), out_dtypes=(F32,))[0]
    pp = mm_nn(p_bf, wg, lay["proj"], name=f"w_ple_proj_{i}", out_dtypes=(F32,))[0]
    x3, gl, h3 = mm_nn(x2, wg, lay["gate"], name=f"w_ple_gate_{i}", epi=_ple_fwd, extras=(x2, pp), out_dtypes=(F32, F32),
                       norm_gain=pn)
    sv.update(h1=h1, a=a, x1=x1, h2=h2, u=u, r=r, x2=x2, h3=h3, pp=pp, gl=gl)
    return x3, sv


def _layer_bwd(i, kind, dx3, sv, p_bf, wg, lay, n_rows, gains, tabs, mid):
    s = dx3.shape[0]
    an, mn, pn = gains
    g = lax.empty((N_CHIPS, n_rows, SLAB), BF16)
    d_pp, d_gl = ew(_ple_bwd, [dx3, sv["pp"], sv["gl"]], [BF16, BF16], name=f"ple_bwd_{i}")
    g = mm_tn(p_bf, d_pp, g, lay["proj"], name=f"dw_ple_proj_{i}")
    g = mm_tn(sv["h3"], d_gl, g, lay["gate"], name=f"dw_ple_gate_{i}")
    d_h3 = mm_nt(d_gl, wg, lay["gate"], name=f"dx_ple_gate_{i}", out_dtypes=(F32,))[0]
    dx2, dx2b, d_pn = rms_bwd(sv["x2"], pn, d_h3, dx3, name=f"ple_norm_bwd_{i}")
    zero = mid(dx2)
    if zero is not None:
        mn = mn + zero
    g = mm_tn(sv["r"], dx2b, g, lay["down"], name=f"dw_down_{i}")
    d_u = mm_nt(dx2b, wg, lay["down"], name=f"dx_down_{i}", epi=_relu2_bwd, extras=(sv["u"],))[0]
    g = mm_tn(sv["h2"], d_u, g, lay["up"], name=f"dw_up_{i}")
    dx1, dx1b, d_mn = mm_nt(d_u, wg, lay["up"], name=f"dx_up_{i}", rms=(sv["x1"], mn, dx2))
    g = mm_tn(sv["a"], dx1b, g, lay["out"], name=f"dw_out_{i}")
    d_a = mm_nt(dx1b, wg, lay["out"], name=f"dx_out_{i}")[0]
    d_extra = None
    if kind == 0:
        d_proj = jnp.concatenate(sb_bwd(sv["proj"], sv["tot"], d_a, name=f"sb_bwd_{i}"), axis=1)
    elif kind == 1:
        do4 = _heads(d_a, N_HEADS).reshape(SWA_KV_HEADS, SWA_GROUP, s, HEAD_DIM)
        dq, dkp, dvp, dsr = swa_bwd(sv["q"], sv["kp"], sv["vp"], sv["sink"], sv["o4"], sv["lse"], do4, name=f"swa_bwd_{i}")
        dqk = jnp.concatenate([_unheads(dq.reshape(N_HEADS, s, HEAD_DIM)), _unheads(dkp[:, SWA_WINDOW:])], axis=1)
        dqk = rope_bwd(dqk, tabs, name=f"rope_bwd_{i}")
        d_proj = jnp.concatenate([dqk, _unheads(dvp[:, SWA_WINDOW:]).astype(BF16)], axis=1)
        d_proj = _chip_cols(d_proj, W_IN_COLS[1], W_IN_PAD[1])
        d_extra = jnp.sum(dsr[..., 0], axis=2).reshape(N_HEADS)
    else:
        dq, dk, dv, dcq, dck = fox_bwd(sv["proj"], sv["a"], sv["lse"], sv["cq"], sv["ck"], d_a, name=f"fox_bwd_{i}")
        dcum = _pad_cols((dcq[:, :, 0] - dck.reshape(N_HEADS, s)).T, 128)
        dfl, dbp = fox_gate_bwd(dcum, sv["fl"], sv["bp"], name=f"gate_bwd_{i}")
        d_proj = jnp.concatenate([dq, dk, dv, dfl[:, :N_HEADS].astype(BF16)], axis=1)
        d_proj = _chip_cols(d_proj, W_IN_COLS[2], W_IN_PAD[2])
        d_extra = dbp[0, :N_HEADS]
    g = mm_tn(sv["h1"], d_proj, g, lay["inp"], name=f"dw_in_{i}")
    dx0, _, d_an = mm_nt(d_proj, wg, lay["inp"], name=f"dx_in_{i}", rms=(sv["x0"], an, dx1))
    return dx0, g, (d_an, d_mn, d_pn), d_extra


def _small_rows(a, prefix):
    rows = [a[f"{prefix}{n}_{i}"] for i in range(DEPTH) for n in GAINS] + [a[f"{prefix}final_norm"]]
    rows += [_pad_cols(a[f"{prefix}{n}"][None], D_MODEL)[0] for n in ("sinks_1", "b_forget_2")]
    return jnp.stack(rows + [jnp.zeros((D_MODEL,), F32)])


def _train_step(a):
    x = a["x"][0]
    tabs = rope_tables(x.shape[0])
    lays = [_layout(k) for k in MIXER]

    def natural(prefix, i, m):
        w = a[f"{prefix}{MAT_ARG[m]}_{i}"]
        return _pad_cols(w, W_IN_PAD[MIXER[i]]) if m == "inp" else w

    chip = 2 * lax.axis_index("x") + lax.axis_index("y")
    where = jnp.stack([chip, lax.axis_index("c")]).astype(jnp.int32)
    def own_block(i, zero):
        pk = jnp.concatenate([_to_slabs((natural("", i, m) + zero).astype(BF16)) for m in MATS], axis=0)
        return lax.dynamic_update_slice(lax.empty((N_CHIPS,) + pk.shape, BF16), pk[None], (chip, 0, 0))

    sends, recvs, bufs, token = allgather_start([own_block(0, 0.0)], name="allgather_start_0")
    more = allgather_start([own_block(i, token[0, 0]) for i in range(1, DEPTH)], name="allgather_start_1")
    sends, recvs, bufs, token = sends + more[0], recvs + more[1], bufs + more[2], more[3]

    gains = [tuple(a[f"{n}_{i}"][None] for n in GAINS) for i in range(DEPTH)]
    extras = [None, a["sinks_1"], a["b_forget_2"], None]
    p_bf = [a["p"][i, 0].astype(BF16) for i in range(DEPTH)]

    saved, wgs, after = [], [], token
    for i in range(DEPTH):
        landed = allgather_wait(bufs[i], sends[i], recvs[i], after, name=f"allgather_wait_{i}")
        wgs.append(allgather_forward(landed, name=f"allgather_forward_{i}"))
        x, sv = _layer_fwd(i, MIXER[i], x, p_bf[i], wgs[i], lays[i][0], gains[i], extras[i], tabs)
        saved.append(sv)
        after = x
    dx, d_final, loss = loss_head(x, a["final_norm"][None], a["loss_target"][0], name="loss_head")

    def pair_and_scatter(j, swapped, after):
        send, recv, grad, land, _ = swapped
        grad, theirs = swap_wait(grad, land, send, recv, after, name=f"swap_wait_{j}")
        return scatter_start(add_pairs(grad, theirs, where, name=f"add_pairs_{j}"), name=f"scatter_start_{j}")

    def sum_and_join(j, scattered, after):
        send, recv, part, land, _ = scattered
        part, got = scatter_wait(part, land, send, recv, after, name=f"scatter_wait_{j}")
        return join_start(add_chips(part, got, where, name=f"add_chips_{j}"), name=f"join_start_{j}")

    small = [None] * N_SMALL
    small[12] = d_final[0]
    small[15] = _pad_cols(loss[:, :1], D_MODEL)[0]
    joined = [None] * DEPTH
    state = dict(swapped=None, scattered=None)
    for i in reversed(range(DEPTH)):
        an, mn, pn = gains[i]
        if state["swapped"] is not None:
            pn = pn + state["swapped"][4]

        def mid(dx2):
            if state["swapped"] is None:
                return None
            state["scattered"] = pair_and_scatter(i + 1, state["swapped"], dx2)
            return state["scattered"][4]

        dx, grad, d_gains, d_extra = _layer_bwd(i, MIXER[i], dx, saved[i], p_bf[i], wgs[i], lays[i][0], lays[i][1],
                                                (an, mn, pn), tabs, mid)
        for j in range(3):
            small[3 * i + j] = d_gains[j][0]
        if d_extra is not None:
            small[12 + MIXER[i]] = _pad_cols(d_extra[None], D_MODEL)[0]
        if state["scattered"] is not None:
            joined[i + 1] = sum_and_join(i + 1, state["scattered"], dx)
        state["swapped"] = swap_start(grad, name=f"swap_start_{i}")
    started = pair_and_scatter(0, state["swapped"], dx)
    small = allreduce_small(jnp.stack(small), name="allreduce_small")

    out = {"loss": small[15, 0], "grad_x": dx[None]}
    res = adamw_small(small, _small_rows(a, ""), _small_rows(a, "m_"), _small_rows(a, "v_"), name="adamw_small")
    for i in reversed(range(DEPTH)):
        if i == 0:
            joined[0] = sum_and_join(0, started, out[f"delta_{MAT_ARG[MATS[-1]]}_1"])
        send, recv, buf, zero = joined[i]
        gfull = join_wait(buf, send, recv, started[4] if i else zero, name=f"join_wait_{i}")
        for m in MATS:
            upd = adamw(gfull, lays[i][0][m], natural("", i, m), natural("m_", i, m), natural("v_", i, m), started[4],
                        name=f"adamw_{MAT_ARG[m]}_{i}")
            cols = a[f"{MAT_ARG[m]}_{i}"].shape[1]
            for kd, r in zip(KINDS, upd):
                out[f"{kd}{MAT_ARG[m]}_{i}"] = r[:, :cols]
    for kd, r in zip(KINDS, res):
        for i in range(DEPTH):
            for j, n in enumerate(GAINS):
                out[f"{kd}{n}_{i}"] = r[3 * i + j][0]
        out[f"{kd}final_norm"] = r[12][0]
        out[f"{kd}sinks_1"] = r[13][0, :N_HEADS]
        out[f"{kd}b_forget_2"] = r[14][0, :N_HEADS]
    return out


def _weight_names():
    names = []
    for i in range(DEPTH):
        names += [f"attn_norm_{i}", f"w_in_{i}", f"w_out_{i}"] + [[], ["sinks_1"], ["b_forget_2"]][MIXER[i]]
        names += [f"mlp_norm_{i}", f"w_up_{i}", f"w_down_{i}", f"ple_norm_{i}", f"w_ple_gate_{i}", f"w_ple_proj_{i}"]
    return names + ["final_norm"]


def kernel(x, p, attn_norm_0, w_in_0, w_out_0, mlp_norm_0, w_up_0, w_down_0, ple_norm_0, w_ple_gate_0, w_ple_proj_0, attn_norm_1, w_in_1, w_out_1, sinks_1, mlp_norm_1, w_up_1, w_down_1, ple_norm_1, w_ple_gate_1, w_ple_proj_1, attn_norm_2, w_in_2, w_out_2, b_forget_2, mlp_norm_2, w_up_2, w_down_2, ple_norm_2, w_ple_gate_2, w_ple_proj_2, attn_norm_3, w_in_3, w_out_3, mlp_norm_3, w_up_3, w_down_3, ple_norm_3, w_ple_gate_3, w_ple_proj_3, final_norm, loss_target, m_attn_norm_0, m_w_in_0, m_w_out_0, m_mlp_norm_0, m_w_up_0, m_w_down_0, m_ple_norm_0, m_w_ple_gate_0, m_w_ple_proj_0, m_attn_norm_1, m_w_in_1, m_w_out_1, m_sinks_1, m_mlp_norm_1, m_w_up_1, m_w_down_1, m_ple_norm_1, m_w_ple_gate_1, m_w_ple_proj_1, m_attn_norm_2, m_w_in_2, m_w_out_2, m_b_forget_2, m_mlp_norm_2, m_w_up_2, m_w_down_2, m_ple_norm_2, m_w_ple_gate_2, m_w_ple_proj_2, m_attn_norm_3, m_w_in_3, m_w_out_3, m_mlp_norm_3, m_w_up_3, m_w_down_3, m_ple_norm_3, m_w_ple_gate_3, m_w_ple_proj_3, m_final_norm, v_attn_norm_0, v_w_in_0, v_w_out_0, v_mlp_norm_0, v_w_up_0, v_w_down_0, v_ple_norm_0, v_w_ple_gate_0, v_w_ple_proj_0, v_attn_norm_1, v_w_in_1, v_w_out_1, v_sinks_1, v_mlp_norm_1, v_w_up_1, v_w_down_1, v_ple_norm_1, v_w_ple_gate_1, v_w_ple_proj_1, v_attn_norm_2, v_w_in_2, v_w_out_2, v_b_forget_2, v_mlp_norm_2, v_w_up_2, v_w_down_2, v_ple_norm_2, v_w_ple_gate_2, v_w_ple_proj_2, v_attn_norm_3, v_w_in_3, v_w_out_3, v_mlp_norm_3, v_w_up_3, v_w_down_3, v_ple_norm_3, v_w_ple_gate_3, v_w_ple_proj_3, v_final_norm):
    out = _train_step(dict(locals()))
    return (out["loss"], out["grad_x"], *[out[kd + n] for kd in KINDS for n in _weight_names()])
```

```python
import jax
import jax.numpy as jnp
from jax import lax
from jax.experimental import pallas as pl
from jax.experimental.pallas import tpu as pltpu

F32 = jnp.float32
BF16 = jnp.bfloat16

D_MODEL = 1024
N_HEADS = 16
HEAD_DIM = 64
SWA_KV_HEADS = 2
SWA_GROUP = 8
SWA_WINDOW = 128
ROPE_THETA = 500000.0
ROPE_DIM = 16
RMS_EPS = 1e-6
NEG_INF = -1e30
ATTN_SCALE = HEAD_DIM ** -0.5
N_CHIPS = 4
N_DEVICES = 8

SLAB = 256
ATT_BLK = 128
ATT_BQ = 512
ATT_BK = 512
ROW_TILE = 256
V7X_VMEM_LIMIT = 56 * 1024 * 1024

ADAM_LR, ADAM_B1, ADAM_B2, ADAM_EPS, ADAM_WD, ADAM_STEP = 0.001, 0.9, 0.999, 1e-08, 0.01, 10


def _cparams(sem=None):
    return pltpu.CompilerParams(dimension_semantics=sem, vmem_limit_bytes=V7X_VMEM_LIMIT)


def _dot(a, b):
    return jnp.dot(a, b, preferred_element_type=F32)


def _dot_nt(a, b):
    return lax.dot_general(a, b, (((1,), (1,)), ((), ())), preferred_element_type=F32)


def _dot_tn(a, b):
    return lax.dot_general(a, b, (((0,), (0,)), ((), ())), preferred_element_type=F32)


def _row_tile(M, K):
    return min(M, 1024) if K >= 1024 else M


def _finish(epi, acc, ex, outs):
    res = epi(acc, *[e[...] for e in ex]) if epi is not None else (acc,)
    for o, r in zip(outs, res):
        o[...] = r.astype(o.dtype)


def _once(shape, index_map):
    return pl.BlockSpec(shape, index_map, pipeline_mode=pl.Buffered(1))


def mm_nn(a, wg, t, *, name, epi=None, extras=(), out_dtypes=(BF16,), norm_gain=None):
    off, K, ns, row = t
    M = a.shape[0]
    sb = off // K
    ne, no = len(extras), len(out_dtypes)
    norm = norm_gain is not None
    if row:
        tm = M if norm else _row_tile(M, K)
        nb = N_CHIPS
        grid = (M // tm, ns)
        a_shape = (tm, N_CHIPS * K)
        a_spec = (_once if norm else pl.BlockSpec)(a_shape, lambda i, q: (i, 0))
        b_specs = [pl.BlockSpec((None, K, SLAB), lambda i, q, j=j: (j, sb + q, 0)) for j in range(nb)]
        tile = pl.BlockSpec((tm, SLAB), lambda i, q: (i, q))
        n_out = ns * SLAB
    else:
        nb = ns
        grid = (N_CHIPS,)
        a_shape = (M, K)
        a_spec = (_once if norm else pl.BlockSpec)(a_shape, lambda j: (0, 0))
        b_specs = [pl.BlockSpec((None, K, SLAB), lambda j, q=q: (j, sb + q, 0)) for q in range(ns)]
        tile = pl.BlockSpec((M, ns * SLAB), lambda j: (0, j))
        n_out = N_CHIPS * ns * SLAB

    def body(a_ref, *rest):
        if norm:
            g_ref, rest, h_out, h_ref = rest[0], rest[1:-2], rest[-2], rest[-1]

            @pl.when(pl.program_id(len(grid) - 1) == 0)
            def _():
                xv = a_ref[...]
                h_ref[...] = (xv * _rstd(xv) * g_ref[...]).astype(BF16)
                h_out[...] = h_ref[...]

            a_ref = h_ref
        bs, ex, outs = rest[:nb], rest[nb:nb + ne], rest[nb + ne:]
        if row:
            acc = _dot(a_ref[:, pl.ds(0, K)], bs[0][...])
            for j in range(1, nb):
                acc = acc + _dot(a_ref[:, pl.ds(j * K, K)], bs[j][...])
            _finish(epi, acc, ex, outs)
        else:
            av = a_ref[...]
            for q in range(ns):
                cols = pl.ds(q * SLAB, SLAB)
                _finish(epi, _dot(av, bs[q][...]), [e.at[:, cols] for e in ex], [o.at[:, cols] for o in outs])

    h_spec = _once(a_shape, (lambda i, q: (i, 0)) if row else (lambda j: (0, 0)))
    return pl.pallas_call(
        body, name=name, grid=grid,
        in_specs=[a_spec] + ([pl.BlockSpec(norm_gain.shape, lambda *_: (0, 0))] if norm else []) + b_specs + [tile] * ne,
        out_specs=[tile] * no + ([h_spec] if norm else []),
        out_shape=[jax.ShapeDtypeStruct((M, n_out), d) for d in out_dtypes]
        + ([jax.ShapeDtypeStruct(a.shape, BF16)] if norm else []),
        scratch_shapes=[pltpu.VMEM(a_shape, BF16)] if norm else [],
        compiler_params=_cparams((("arbitrary" if norm else "parallel"),) * len(grid)),
    )(a, *([norm_gain] if norm else []), *([wg] * nb), *extras)


def mm_nt(dy, wg, t, *, name, epi=None, extras=(), out_dtypes=(BF16,), rms=None):
    off, K, ns, row = t
    M = dy.shape[0]
    tm = _row_tile(M, K)
    sb = off // K
    if rms is not None:
        x, gain, dres = rms
        extras, out_dtypes = (x, dres), (F32, BF16)
    ne, no = len(extras), len(out_dtypes)
    grid = (M // tm, N_CHIPS)
    b_specs = [pl.BlockSpec((None, K, SLAB), lambda i, j, q=q: (j, sb + q, 0)) for q in range(ns)]
    if row:
        dy_spec = pl.BlockSpec((tm, ns * SLAB), lambda i, j: (i, 0))
        tile = pl.BlockSpec((tm, K), lambda i, j: (i, j))
        n_out = N_CHIPS * K
        sem = ("parallel", "parallel")
    else:
        dy_spec = pl.BlockSpec((tm, ns * SLAB), lambda i, j: (i, j))
        tile = pl.BlockSpec((tm, K), lambda i, j: (i, 0))
        n_out = K
        sem = ("arbitrary" if rms is not None else "parallel", "arbitrary")
    one = pl.BlockSpec((1, K), lambda i, j: (0, 0))

    def body(dy_ref, *rest):
        if rms is not None:
            g_ref, rest, dg_ref, acc_ref = rest[0], rest[1:-2], rest[-2], rest[-1]
            rest = rest + (acc_ref,)
        bs, ex, outs = rest[:ns], rest[ns:ns + ne], rest[ns + ne:ns + ne + no]
        part = _dot_nt(dy_ref[:, pl.ds(0, SLAB)], bs[0][...])
        for q in range(1, ns):
            part = part + _dot_nt(dy_ref[:, pl.ds(q * SLAB, SLAB)], bs[q][...])
        if row:
            _finish(epi, part, ex, outs)
        else:
            acc_ref = rest[-1]
            i, j = pl.program_id(0), pl.program_id(1)

            @pl.when(j == 0)
            def _():
                acc_ref[...] = part

            @pl.when(j > 0)
            def _():
                acc_ref[...] += part

            @pl.when(j == N_CHIPS - 1)
            def _():
                if rms is None:
                    _finish(epi, acc_ref[...], ex, outs)
                else:
                    dx, dg = _rms_bwd_tile(ex[0][...], g_ref[...], acc_ref[...])
                    dx = dx + ex[1][...]
                    outs[0][...] = dx
                    outs[1][...] = dx.astype(BF16)

                    @pl.when(i == 0)
                    def _():
                        dg_ref[...] = dg

                    @pl.when(i > 0)
                    def _():
                        dg_ref[...] += dg

    has = rms is not None
    return pl.pallas_call(
        body, name=name, grid=grid,
        in_specs=[dy_spec] + ([one] if has else []) + b_specs + [tile] * ne,
        out_specs=[tile] * no + ([one] if has else []),
        out_shape=[jax.ShapeDtypeStruct((M, n_out), d) for d in out_dtypes] + ([jax.ShapeDtypeStruct((1, K), F32)] if has else []),
        scratch_shapes=[] if row else [pltpu.VMEM((tm, K), F32)],
        compiler_params=_cparams(sem),
    )(dy, *([gain] if has else []), *([wg] * ns), *extras)


def mm_plain(a, b, *, name):
    M, K = a.shape
    N = b.shape[1]
    tm = min(M, 512)

    def body(a_ref, b_ref, o_ref):
        o_ref[...] = _dot(a_ref[...], b_ref[...])

    return pl.pallas_call(
        body, name=name, grid=(M // tm,),
        in_specs=[pl.BlockSpec((tm, K), lambda i: (i, 0)), pl.BlockSpec((K, N), lambda i: (0, 0))],
        out_specs=pl.BlockSpec((tm, N), lambda i: (i, 0)), out_shape=jax.ShapeDtypeStruct((M, N), F32),
        compiler_params=_cparams(("parallel",)),
    )(a, b)


def mm_tn(x, dy, g, t, *, name):
    off, K, ns, row = t
    S = x.shape[0]
    per = ns if off % (ns * K) == 0 else 1
    grid = (N_CHIPS, ns // per)
    if row:
        x_map = lambda j, q: (0, j)
        dy_map = lambda j, q: (0, q)
    else:
        x_map = lambda j, q: (0, 0)
        dy_map = lambda j, q: (0, j * (ns // per) + q)

    def body(g_in, x_ref, dy_ref, o_ref):
        del g_in
        xt = x_ref[...].T
        for q in range(per):
            o_ref[pl.ds(q * K, K), :] = _dot(xt, dy_ref[:, pl.ds(q * SLAB, SLAB)]).astype(o_ref.dtype)

    return pl.pallas_call(
        body, name=name, grid=grid,
        in_specs=[pl.BlockSpec(memory_space=pl.ANY), pl.BlockSpec((S, K), x_map), pl.BlockSpec((S, per * SLAB), dy_map)],
        out_specs=pl.BlockSpec((None, per * K, SLAB), lambda j, q: (j, off // (per * K) + q, 0)),
        out_shape=jax.ShapeDtypeStruct(g.shape, g.dtype),
        input_output_aliases={0: 0},
        compiler_params=_cparams(("parallel", "parallel")),
    )(g, x, dy)


def ew(fn, ins, out_dtypes, *, name, bcast=()):
    S = ins[0].shape[0]
    tr = min(ROW_TILE, S)
    cols = ins[0].shape[1]
    ni, nb = len(ins), len(bcast)

    def body(*refs):
        res = fn(*[r[...] for r in refs[:ni + nb]])
        for o, r in zip(refs[ni + nb:], res):
            o[...] = r.astype(o.dtype)

    return pl.pallas_call(
        body, name=name, grid=(S // tr,),
        in_specs=[pl.BlockSpec((tr, a.shape[1]), lambda i: (i, 0)) for a in ins]
        + [pl.BlockSpec(b.shape, lambda i: (0, 0)) for b in bcast],
        out_specs=[pl.BlockSpec((tr, cols), lambda i: (i, 0)) for _ in out_dtypes],
        out_shape=[jax.ShapeDtypeStruct((S, cols), d) for d in out_dtypes],
        compiler_params=_cparams(("parallel",)),
    )(*ins, *bcast)


def _rstd(x):
    return lax.rsqrt(jnp.mean(x * x, axis=-1, keepdims=True) + RMS_EPS)


def _sigmoid(x):
    return 1.0 / (1.0 + jnp.exp(-x))


def _log_sigmoid(z):
    return jnp.minimum(z, 0.0) - jnp.log(1.0 + jnp.exp(-jnp.abs(z)))


def _rms_bwd_tile(xv, gv, dh):
    rstd = _rstd(xv)
    xhat = xv * rstd
    gd = dh * gv
    dx = rstd * (gd - xhat * jnp.mean(xhat * gd, axis=-1, keepdims=True))
    return dx, jnp.sum(dh * xhat, axis=0, keepdims=True)


def rms_bwd(x, g, dh, dres, *, name):
    S, D = x.shape
    tr = min(ROW_TILE, S)

    def body(x_ref, g_ref, dh_ref, dres_ref, dx_ref, dxb_ref, dg_ref):
        i = pl.program_id(0)
        dx, dg = _rms_bwd_tile(x_ref[...], g_ref[...], dh_ref[...])
        dx = dx + dres_ref[...]
        dx_ref[...] = dx
        dxb_ref[...] = dx.astype(BF16)

        @pl.when(i == 0)
        def _():
            dg_ref[...] = dg

        @pl.when(i > 0)
        def _():
            dg_ref[...] += dg

    row = pl.BlockSpec((tr, D), lambda i: (i, 0))
    one = pl.BlockSpec((1, D), lambda i: (0, 0))
    return pl.pallas_call(
        body, name=name, grid=(S // tr,),
        in_specs=[row, one, row, row], out_specs=[row, row, one],
        out_shape=[jax.ShapeDtypeStruct((S, D), F32), jax.ShapeDtypeStruct((S, D), BF16),
                   jax.ShapeDtypeStruct((1, D), F32)],
        compiler_params=_cparams(("arbitrary",)),
    )(x, g, dh, dres)


def loss_head(x, g, target, *, name):
    S, D = x.shape
    tr = min(ROW_TILE, S)

    def body(x_ref, g_ref, t_ref, dx_ref, dg_ref, loss_ref):
        i = pl.program_id(0)
        xv, gv = x_ref[...], g_ref[...]
        err = xv * _rstd(xv) * gv - t_ref[...]
        part = 0.5 * jnp.sum(jnp.mean(err * err, axis=-1, keepdims=True), axis=0, keepdims=True)
        dx, dg = _rms_bwd_tile(xv, gv, err * (1.0 / D))
        dx_ref[...] = dx
        part = jnp.broadcast_to(part, loss_ref.shape)

        @pl.when(i == 0)
        def _():
            dg_ref[...] = dg
            loss_ref[...] = part

        @pl.when(i > 0)
        def _():
            dg_ref[...] += dg
            loss_ref[...] += part

    row = pl.BlockSpec((tr, D), lambda i: (i, 0))
    one = pl.BlockSpec((1, D), lambda i: (0, 0))
    return pl.pallas_call(
        body, name=name, grid=(S // tr,),
        in_specs=[row, one, row], out_specs=[row, one, pl.BlockSpec((1, 128), lambda i: (0, 0))],
        out_shape=[jax.ShapeDtypeStruct((S, D), F32), jax.ShapeDtypeStruct((1, D), F32),
                   jax.ShapeDtypeStruct((1, 128), F32)],
        compiler_params=_cparams(("arbitrary",)),
    )(x, g, target)


def rope_tables(S):
    half = ROPE_DIM // 2
    inv_freq = ROPE_THETA ** (-jnp.arange(half, dtype=F32) / half)
    ang = jnp.arange(S, dtype=F32)[:, None] * inv_freq[None, :]
    cos, sin = jnp.cos(ang), jnp.sin(ang)
    z = jnp.zeros((S, HEAD_DIM - ROPE_DIM), F32)
    zh = jnp.zeros((S, half), F32)
    c = jnp.concatenate([cos, cos, jnp.ones_like(z)], axis=1)
    sa = jnp.concatenate([zh, sin, z], axis=1)
    sb = jnp.concatenate([-sin, zh, z], axis=1)
    return [jnp.concatenate([t, t], axis=1) for t in (c, sa, sb)]


def _wide(t, n):
    return jnp.tile(t, (1, n // t.shape[1]))


def rope_fwd(xqk, tables, *, name):
    n, half = xqk.shape[1], ROPE_DIM // 2

    def fn(x, c, sa, sb):
        return (x * _wide(c, n) + pltpu.roll(x, half, 1) * _wide(sa, n) + pltpu.roll(x, n - half, 1) * _wide(sb, n),)

    return ew(fn, [xqk] + list(tables), [BF16], name=name)[0]


def rope_bwd(dy, tables, *, name):
    n, half = dy.shape[1], ROPE_DIM // 2

    def fn(d, c, sa, sb):
        return (d * _wide(c, n) + pltpu.roll(d * _wide(sa, n), n - half, 1) + pltpu.roll(d * _wide(sb, n), half, 1),)

    return ew(fn, [dy] + list(tables), [BF16], name=name)[0]


def _split3(x):
    h1 = x.astype(BF16)
    r1 = x - h1.astype(F32)
    h2 = r1.astype(BF16)
    return h1, h2, (r1 - h2.astype(F32)).astype(BF16)


def _tri(n, cmp):
    r = lax.broadcasted_iota(jnp.int32, (n, n), 0)
    c = lax.broadcasted_iota(jnp.int32, (n, n), 1)
    return cmp(r, c).astype(BF16)


def fox_gate_fwd(fl, b, *, name):
    S, W = fl.shape
    tr = min(ROW_TILE, S)

    def body(fl_ref, b_ref, cum_ref, carry):
        i = pl.program_id(0)

        @pl.when(i == 0)
        def _():
            carry[...] = jnp.zeros_like(carry)

        lower = _tri(tr, lambda r, c: r >= c)
        cs = carry[...]
        for piece in _split3(_log_sigmoid(fl_ref[...] + b_ref[...])):
            cs = cs + _dot(lower, piece)
        cum_ref[...] = cs
        carry[...] = cs[tr - 1:tr, :]

    return pl.pallas_call(
        body, name=name, grid=(S // tr,),
        in_specs=[pl.BlockSpec((tr, W), lambda i: (i, 0)), pl.BlockSpec((1, W), lambda i: (0, 0))],
        out_specs=pl.BlockSpec((tr, W), lambda i: (i, 0)),
        out_shape=jax.ShapeDtypeStruct((S, W), F32),
        scratch_shapes=[pltpu.VMEM((1, W), F32)],
        compiler_params=_cparams(("arbitrary",)),
    )(fl, b)


def fox_gate_bwd(dcum, fl, b, *, name):
    S, W = fl.shape
    tr = min(ROW_TILE, S)
    nb = S // tr

    def body(dc_ref, fl_ref, b_ref, dfl_ref, db_ref, carry):
        i = pl.program_id(0)

        @pl.when(i == 0)
        def _():
            carry[...] = jnp.zeros_like(carry)

        upper = _tri(tr, lambda r, c: r <= c)
        cs = carry[...]
        for piece in _split3(dc_ref[...]):
            cs = cs + _dot(upper, piece)
        carry[...] = cs[0:1, :]
        dfl = cs * _sigmoid(-(fl_ref[...] + b_ref[...]))
        dfl_ref[...] = dfl
        db = jnp.sum(dfl, axis=0, keepdims=True)

        @pl.when(i == 0)
        def _():
            db_ref[...] = db

        @pl.when(i > 0)
        def _():
            db_ref[...] += db

    rev = pl.BlockSpec((tr, W), lambda i: (nb - 1 - i, 0))
    one = pl.BlockSpec((1, W), lambda i: (0, 0))
    return pl.pallas_call(
        body, name=name, grid=(nb,),
        in_specs=[rev, rev, one], out_specs=[rev, one],
        out_shape=[jax.ShapeDtypeStruct((S, W), F32), jax.ShapeDtypeStruct((1, W), F32)],
        scratch_shapes=[pltpu.VMEM((1, W), F32)],
        compiler_params=_cparams(("arbitrary",)),
    )(dcum, fl, b)


def _blk_iota(tq, tk):
    return (lax.broadcasted_iota(jnp.int32, (tq, tk), 0), lax.broadcasted_iota(jnp.int32, (tq, tk), 1))


def _cs(xb, tri):
    return _dot(xb, tri)


def _rowsum(xb):
    return jnp.sum(xb.astype(F32), axis=1, keepdims=True)


def _sb_block(qs, k, cmr, shift):
    z = _dot_nt(qs, k)
    lb = jnp.minimum(z, 0.0) - jnp.log(1.0 + jnp.exp(-jnp.abs(z)))
    if cmr is None:
        return lb, (lb - z).astype(BF16), None
    strict = cmr < shift
    lom = jnp.where(strict, lb - z, 0.0).astype(BF16)
    return lb, lom, strict


def _keep(mask, x):
    return x if mask is None else jnp.where(mask, x, 0.0)


def _att_tiles(S):
    return min(ATT_BQ, S), min(ATT_BK, S)


PAIR = 2 * HEAD_DIM
N_PAIRS = N_HEADS // 2


def _pair_specs(S, tq):
    cols = D_MODEL // PAIR
    qspec = pl.BlockSpec((tq, PAIR), lambda p, i: (i, p))
    kspec = pl.BlockSpec((S, PAIR), lambda p, i: (0, cols + p))
    vspec = pl.BlockSpec((S, PAIR), lambda p, i: (0, 2 * cols + p))
    kvout = pl.BlockSpec((S, PAIR), lambda p, i: (0, p))
    vec = pl.BlockSpec((2, tq, 1), lambda p, i: (p, i, 0))
    return qspec, kspec, vspec, kvout, vec


def _head_lanes(h):
    lane = lax.broadcasted_iota(jnp.int32, (1, PAIR), 1)
    return (lane >= h * HEAD_DIM) & (lane < (h + 1) * HEAD_DIM)


def _only(sel, x):
    return jnp.where(sel, x, jnp.zeros_like(x))


def sb_fwd(proj, *, name):
    S = proj.shape[0]
    tq, tk = _att_tiles(S)
    qspec, kspec, vspec, _, vec = _pair_specs(S, tq)

    def body(q_ref, k_ref, v_ref, o_ref, t_ref):
        i = pl.program_id(1)
        row, col = _blk_iota(tq, tk)
        cmr = col - row
        below = _tri(tk, lambda r, c: r > c)
        nkb = (i + 1) * (tq // tk)
        out = []
        for h in range(2):
            sel = _head_lanes(h)
            qs = _only(sel, q_ref[...] * ATTN_SCALE)

            def step(n, carry, masked):
                r_sum, acc = carry
                kb = nkb - 1 - n
                ks = pl.multiple_of(kb * tk, tk)
                lb, lom, strict = _sb_block(qs, k_ref[pl.ds(ks, tk), :], cmr if masked else None, i * tq - kb * tk)
                w = _keep(strict, jnp.exp(lb + _cs(lom, below) + r_sum))
                acc = acc + _dot(w.astype(BF16), _only(sel, v_ref[pl.ds(ks, tk), :]))
                return r_sum + _rowsum(lom), acc

            nd = tq // tk
            carry = lax.fori_loop(0, nd, lambda n, c: step(n, c, True), (jnp.zeros((tq, 1), F32), jnp.zeros((tq, PAIR), F32)))
            r_sum, acc = lax.fori_loop(nd, nkb, lambda n, c: step(n, c, False), carry)
            t_ref[h] = r_sum
            out.append(acc)
        o_ref[...] = (out[0] + out[1]).astype(o_ref.dtype)

    return pl.pallas_call(
        body, name=name, grid=(N_PAIRS, S // tq),
        in_specs=[qspec, kspec, vspec], out_specs=[qspec, vec],
        out_shape=[jax.ShapeDtypeStruct((S, D_MODEL), BF16), jax.ShapeDtypeStruct((N_HEADS, S, 1), F32)],
        compiler_params=_cparams(("parallel", "arbitrary")),
    )(proj, proj, proj)


def sb_bwd(proj, tot, do, *, name):
    S = proj.shape[0]
    tq, tk = _att_tiles(S)
    qspec, kspec, vspec, kvout, vec = _pair_specs(S, tq)

    def body(q_ref, k_ref, v_ref, t_ref, do_ref, dq_ref, dk_out, dv_out, dk_ref, dv_ref):
        i = pl.program_id(1)

        @pl.when(i == 0)
        def _():
            dk_ref[...] = jnp.zeros_like(dk_ref)
            dv_ref[...] = jnp.zeros_like(dv_ref)

        row, col = _blk_iota(tq, tk)
        cmr = col - row
        upto = _tri(tk, lambda r, c: r <= c)
        before = _tri(tk, lambda r, c: r < c)
        out = []
        for h in range(2):
            sel = _head_lanes(h)
            qs, dov, t_all = _only(sel, q_ref[...] * ATTN_SCALE), _only(sel, do_ref[...]), t_ref[h]

            def step(kb, carry, masked):
                p_sum, e_sum, dq = carry
                ks = pl.multiple_of(kb * tk, tk)
                kv = k_ref[pl.ds(ks, tk), :]
                lb, lom, strict = _sb_block(qs, kv, cmr if masked else None, i * tq - kb * tk)
                tail = t_all - p_sum - _cs(lom, upto)
                w = _keep(strict, jnp.exp(lb + tail))
                e = _dot_nt(dov, v_ref[pl.ds(ks, tk), :]) * w
                eb = e.astype(BF16)
                e_before = e_sum + _cs(eb, before)
                beta = jnp.exp(lb)
                dzb = _keep(strict, e - (e + e_before) * beta).astype(BF16)
                dk_ref[pl.ds(ks, tk), :] += _dot_tn(dzb, qs)
                dv_ref[pl.ds(ks, tk), :] += _dot_tn(w.astype(BF16), dov)
                return p_sum + _rowsum(lom), e_sum + _rowsum(eb), dq + _dot(dzb, _only(sel, kv))

            zero = jnp.zeros((tq, 1), F32)
            nlow = i * (tq // tk)
            carry = lax.fori_loop(0, nlow, lambda kb, c: step(kb, c, False), (zero, zero, jnp.zeros((tq, PAIR), F32)))
            out.append(lax.fori_loop(nlow, nlow + tq // tk, lambda kb, c: step(kb, c, True), carry)[2])
        dq_ref[...] = ((out[0] + out[1]) * ATTN_SCALE).astype(dq_ref.dtype)

        @pl.when(i == S // tq - 1)
        def _():
            dk_out[...] = dk_ref[...].astype(dk_out.dtype)
            dv_out[...] = dv_ref[...].astype(dv_out.dtype)

    full = jax.ShapeDtypeStruct((S, D_MODEL), BF16)
    return pl.pallas_call(
        body, name=name, grid=(N_PAIRS, S // tq),
        in_specs=[qspec, kspec, vspec, vec, qspec], out_specs=[qspec, kvout, kvout],
        out_shape=[full, full, full],
        scratch_shapes=[pltpu.VMEM((S, PAIR), F32)] * 2,
        compiler_params=_cparams(("parallel", "arbitrary")),
    )(proj, proj, proj, tot, do)


def _fox_logits(qs, k, cq, ck, cmr, shift):
    s = _dot_nt(qs, k) + cq - ck
    if cmr is None:
        return s, None
    causal = cmr <= shift
    return jnp.where(causal, s, NEG_INF), causal


def fox_fwd(proj, cq, ck, *, name):
    S = proj.shape[0]
    tq, tk = _att_tiles(S)
    qspec, kspec, vspec, _, vec = _pair_specs(S, tq)
    ckspec = pl.BlockSpec((2, S // tk, 1, tk), lambda p, i: (p, 0, 0, 0))

    def body(q_ref, k_ref, v_ref, cq_ref, ck_ref, o_ref, lse_ref):
        i = pl.program_id(1)
        row, col = _blk_iota(tq, tk)
        cmr = col - row
        out = []
        for h in range(2):
            sel = _head_lanes(h)
            qs, cqv = _only(sel, q_ref[...] * ATTN_SCALE), cq_ref[h]

            def step(kb, carry, masked):
                m, l, acc = carry
                ks = pl.multiple_of(kb * tk, tk)
                s, _ = _fox_logits(qs, k_ref[pl.ds(ks, tk), :], cqv, ck_ref[h, kb], cmr if masked else None, i * tq - kb * tk)
                m_new = jnp.maximum(m, jnp.max(s, axis=1, keepdims=True))
                alpha = jnp.exp(m - m_new)
                p = jnp.exp(s - m_new)
                l = alpha * l + jnp.sum(p, axis=1, keepdims=True)
                acc = alpha * acc + _dot(p.astype(BF16), _only(sel, v_ref[pl.ds(ks, tk), :]))
                return m_new, l, acc

            nlow = i * (tq // tk)
            carry = lax.fori_loop(0, nlow, lambda kb, c: step(kb, c, False),
                                  (jnp.full((tq, 1), NEG_INF, F32), jnp.zeros((tq, 1), F32), jnp.zeros((tq, PAIR), F32)))
            m, l, acc = lax.fori_loop(nlow, nlow + tq // tk, lambda kb, c: step(kb, c, True), carry)
            lse_ref[h] = m + jnp.log(l)
            out.append(acc / l)
        o_ref[...] = (out[0] + out[1]).astype(o_ref.dtype)

    return pl.pallas_call(
        body, name=name, grid=(N_PAIRS, S // tq),
        in_specs=[qspec, kspec, vspec, vec, ckspec], out_specs=[qspec, vec],
        out_shape=[jax.ShapeDtypeStruct((S, D_MODEL), BF16), jax.ShapeDtypeStruct((N_HEADS, S, 1), F32)],
        compiler_params=_cparams(("parallel", "arbitrary")),
    )(proj, proj, proj, cq, ck)


def fox_bwd(proj, o, lse, cq, ck, do, *, name):
    S = proj.shape[0]
    tq, tk = _att_tiles(S)
    qspec, kspec, vspec, kvout, vec = _pair_specs(S, tq)
    ckspec = pl.BlockSpec((2, S // tk, 1, tk), lambda p, i: (p, 0, 0, 0))

    def body(q_ref, k_ref, v_ref, o_ref, lse_ref, cq_ref, ck_ref, do_ref, dq_ref, dk_out, dv_out, dcq_ref, dck_ref,
             dk_ref, dv_ref):
        i = pl.program_id(1)

        @pl.when(i == 0)
        def _():
            dk_ref[...] = jnp.zeros_like(dk_ref)
            dv_ref[...] = jnp.zeros_like(dv_ref)
            dck_ref[...] = jnp.zeros_like(dck_ref)

        row, col = _blk_iota(tq, tk)
        cmr = col - row
        out = []
        for h in range(2):
            sel = _head_lanes(h)
            qs, dov, cqv, lsev = _only(sel, q_ref[...] * ATTN_SCALE), _only(sel, do_ref[...]), cq_ref[h], lse_ref[h]
            delta = jnp.sum(dov.astype(F32) * o_ref[...].astype(F32), axis=1, keepdims=True)

            def step(kb, carry, masked):
                dq, dcq = carry
                ks = pl.multiple_of(kb * tk, tk)
                kv = k_ref[pl.ds(ks, tk), :]
                s, causal = _fox_logits(qs, kv, cqv, ck_ref[h, kb], cmr if masked else None, i * tq - kb * tk)
                p = _keep(causal, jnp.exp(s - lsev))
                ds = p * (_dot_nt(dov, v_ref[pl.ds(ks, tk), :]) - delta)
                dck_ref[h, kb] += jnp.sum(ds, axis=0, keepdims=True)
                dsb = ds.astype(BF16)
                dk_ref[pl.ds(ks, tk), :] += _dot_tn(dsb, qs)
                dv_ref[pl.ds(ks, tk), :] += _dot_tn(p.astype(BF16), dov)
                return dq + _dot(dsb, _only(sel, kv)), dcq + jnp.sum(ds, axis=1, keepdims=True)

            nlow = i * (tq // tk)
            carry = lax.fori_loop(0, nlow, lambda kb, c: step(kb, c, False),
                                  (jnp.zeros((tq, PAIR), F32), jnp.zeros((tq, 1), F32)))
            dq, dcq = lax.fori_loop(nlow, nlow + tq // tk, lambda kb, c: step(kb, c, True), carry)
            dcq_ref[h] = dcq
            out.append(dq)
        dq_ref[...] = ((out[0] + out[1]) * ATTN_SCALE).astype(dq_ref.dtype)

        @pl.when(i == S // tq - 1)
        def _():
            dk_out[...] = dk_ref[...].astype(dk_out.dtype)
            dv_out[...] = dv_ref[...].astype(dv_out.dtype)

    full = jax.ShapeDtypeStruct((S, D_MODEL), BF16)
    return pl.pallas_call(
        body, name=name, grid=(N_PAIRS, S // tq),
        in_specs=[qspec, kspec, vspec, qspec, vec, vec, ckspec, qspec],
        out_specs=[qspec, kvout, kvout, vec, ckspec],
        out_shape=[full, full, full, jax.ShapeDtypeStruct((N_HEADS, S, 1), F32),
                   jax.ShapeDtypeStruct((N_HEADS, S // tk, 1, tk), F32)],
        scratch_shapes=[pltpu.VMEM((S, PAIR), F32)] * 2,
        compiler_params=_cparams(("parallel", "arbitrary")),
    )(proj, proj, proj, o, lse, cq, ck, do)


def _swa_specs(S, tq):
    qspec = pl.BlockSpec((None, SWA_GROUP, tq, HEAD_DIM), lambda g, i: (g, 0, i, 0))
    kvspec = pl.BlockSpec((None, S + SWA_WINDOW, HEAD_DIM), lambda g, i: (g, 0, 0))
    vec = pl.BlockSpec((None, SWA_GROUP, tq, 1), lambda g, i: (g, 0, i, 0))
    sink = pl.BlockSpec((None, SWA_GROUP * tq, 1), lambda g, i: (g, 0, 0))
    return qspec, kvspec, vec, sink


def _swa_logits(q2, kw, i, tq):
    rows = q2.shape[0]
    r = lax.broadcasted_iota(jnp.int32, (rows, 2 * tq), 0)
    c = lax.broadcasted_iota(jnp.int32, (rows, 2 * tq), 1)
    diff = (r & (tq - 1)) + tq - c
    ok = (diff >= 0) & (diff < SWA_WINDOW) & (c + (i - 1) * tq >= 0)
    return jnp.where(ok, _dot_nt(q2, kw) * ATTN_SCALE, NEG_INF), ok


def swa_fwd(q, kp, vp, sink, *, name):
    _, G, S, _ = q.shape
    tq = ATT_BLK
    qspec, kvspec, vec, sinkspec = _swa_specs(S, tq)

    def body(q_ref, k_ref, v_ref, s_ref, o_ref, lse_ref):
        i = pl.program_id(1)
        q2 = q_ref[...].reshape(G * tq, HEAD_DIM)
        ws = pl.multiple_of(i * tq, tq)
        logits, _ = _swa_logits(q2, k_ref[pl.ds(ws, 2 * tq), :], i, tq)
        sk = s_ref[...]
        m = jnp.maximum(jnp.max(logits, axis=1, keepdims=True), sk)
        e = jnp.exp(logits - m)
        den = jnp.sum(e, axis=1, keepdims=True) + jnp.exp(sk - m)
        o = _dot((e / den).astype(BF16), v_ref[pl.ds(ws, 2 * tq), :])
        o_ref[...] = o.reshape(G, tq, HEAD_DIM).astype(o_ref.dtype)
        lse_ref[...] = (m + jnp.log(den)).reshape(G, tq, 1)

    return pl.pallas_call(
        body, name=name, grid=(SWA_KV_HEADS, S // tq),
        in_specs=[qspec, kvspec, kvspec, sinkspec], out_specs=[qspec, vec],
        out_shape=[jax.ShapeDtypeStruct(q.shape, BF16), jax.ShapeDtypeStruct((SWA_KV_HEADS, G, S, 1), F32)],
        compiler_params=_cparams(("parallel", "arbitrary")),
    )(q, kp, vp, sink)


def swa_bwd(q, kp, vp, sink, o, lse, do, *, name):
    _, G, S, _ = q.shape
    tq = ATT_BLK
    qspec, kvspec, vec, sinkspec = _swa_specs(S, tq)

    def body(q_ref, k_ref, v_ref, s_ref, o_ref, lse_ref, do_ref, dq_ref, dk_ref, dv_ref, dsink_ref):
        i = pl.program_id(1)

        @pl.when(i == 0)
        def _():
            dk_ref[...] = jnp.zeros_like(dk_ref)
            dv_ref[...] = jnp.zeros_like(dv_ref)

        q2 = q_ref[...].reshape(G * tq, HEAD_DIM)
        do2 = do_ref[...].reshape(G * tq, HEAD_DIM)
        o2 = o_ref[...].reshape(G * tq, HEAD_DIM)
        lse2 = lse_ref[...].reshape(G * tq, 1)
        ws = pl.multiple_of(i * tq, tq)
        kw = k_ref[pl.ds(ws, 2 * tq), :]
        vw = v_ref[pl.ds(ws, 2 * tq), :]
        logits, ok = _swa_logits(q2, kw, i, tq)
        p = jnp.where(ok, jnp.exp(logits - lse2), 0.0)
        delta = jnp.sum(do2.astype(F32) * o2.astype(F32), axis=1, keepdims=True)
        ds = p * (_dot_nt(do2, vw) - delta)
        dsb = ds.astype(BF16)
        dq_ref[...] = (_dot(dsb, kw) * ATTN_SCALE).reshape(G, tq, HEAD_DIM)
        dk_ref[pl.ds(ws, 2 * tq), :] += _dot_tn(dsb, q2) * ATTN_SCALE
        dv_ref[pl.ds(ws, 2 * tq), :] += _dot_tn(p.astype(BF16), do2)
        dsink_ref[...] = (-jnp.exp(s_ref[...] - lse2) * delta).reshape(G, tq, 1)

    kvshape = jax.ShapeDtypeStruct(kp.shape, F32)
    return pl.pallas_call(
        body, name=name, grid=(SWA_KV_HEADS, S // tq),
        in_specs=[qspec, kvspec, kvspec, sinkspec, qspec, vec, qspec],
        out_specs=[qspec, kvspec, kvspec, vec],
        out_shape=[jax.ShapeDtypeStruct(q.shape, F32), kvshape, kvshape,
                   jax.ShapeDtypeStruct((SWA_KV_HEADS, G, S, 1), F32)],
        compiler_params=_cparams(("parallel", "arbitrary")),
    )(q, kp, vp, sink, o, lse, do)


def _adamw_tile(w, g, m, v):
    m = ADAM_B1 * m + (1.0 - ADAM_B1) * g
    v = ADAM_B2 * v + (1.0 - ADAM_B2) * (g * g)
    m_hat = m / (1.0 - ADAM_B1 ** ADAM_STEP)
    v_hat = v / (1.0 - ADAM_B2 ** ADAM_STEP)
    delta = -ADAM_LR * (m_hat / (jnp.sqrt(v_hat) + ADAM_EPS) + ADAM_WD * w)
    return g, delta, m, v


def adamw(gfull, t, w, m, v, after, *, name):
    off, K, ns, _ = t
    sb = off // K
    nat = pl.BlockSpec((K, SLAB), lambda q: (0, q))

    def body(g_ref, w_ref, m_ref, v_ref, after_ref, *outs):
        del after_ref
        for o, r in zip(outs, _adamw_tile(w_ref[...], g_ref[...], m_ref[...], v_ref[...])):
            o[...] = r

    return pl.pallas_call(
        body, name=name, grid=(ns,),
        in_specs=[pl.BlockSpec((K, SLAB), lambda q: (sb + q, 0)), nat, nat, nat, HBM],
        out_specs=[nat] * 4, out_shape=[jax.ShapeDtypeStruct(w.shape, F32)] * 4,
        compiler_params=_cparams(("parallel",)),
    )(gfull, w, m, v, after)


def adamw_small(g, w, m, v, *, name):
    rows, d = w.shape

    def body(g_ref, w_ref, m_ref, v_ref, *outs):
        for j in range(rows):
            one = pl.ds(j, 1)
            for k, r in enumerate(_adamw_tile(w_ref[one, :], g_ref[one, :], m_ref[one, :], v_ref[one, :])):
                outs[k * rows + j][...] = r

    flat = pl.pallas_call(body, name=name, out_shape=[jax.ShapeDtypeStruct((1, d), F32)] * (4 * rows))(g, w, m, v)
    return [flat[k * rows:(k + 1) * rows] for k in range(4)]


MESH = pl.DeviceIdType.MESH
HBM = pl.BlockSpec(memory_space=pl.ANY)


def _place():
    x, y, c = lax.axis_index("x"), lax.axis_index("y"), lax.axis_index("c")
    others = [(1 - x, y), (x, 1 - y), (1 - x, 1 - y)]
    return x, y, c, others


def _rcopy(src, dst, send_sems, recv_sems, k, to):
    return pltpu.make_async_remote_copy(src_ref=src, dst_ref=dst, send_sem=send_sems.at[k], recv_sem=recv_sems.at[k],
                                        device_id=to, device_id_type=MESH)


def _dma_sems(*counts):
    return [pltpu.SemaphoreType.DMA((n,)) for n in counts]


DMA_UNIT_ROWS = 128
DMA_PIECES = 4


def _row_pieces(h, n):
    units = h // DMA_UNIT_ROWS
    n = min(n, units)
    base, extra = divmod(units, n)
    sizes = [(base + (k < extra)) * DMA_UNIT_ROWS for k in range(n)]
    return [(sum(sizes[:k]), sizes[k]) for k in range(n)]


def _start_pieces(make, h, n):
    for s0, sz in _row_pieces(h, n):
        make(s0, sz).start()
    return make(0, h)


SEM = pl.BlockSpec(memory_space=pltpu.SEMAPHORE)
SPLIT_COPY = pltpu.CompilerParams(has_side_effects=pltpu.SideEffectType.DATAFLOW_SIDE_EFFECTING)
N_OTHERS = 3


def _hbm(a):
    return pltpu.with_memory_space_constraint(a, pltpu.HBM)


def _chip_rows(buf, chip, s0, sz):
    return buf.at[2 * chip[0] + chip[1], pl.ds(s0, sz)]


def allgather_start(bufs, *, name):
    n = len(bufs)

    def body(*refs):
        ins, send, recv, token = refs[:n], refs[n:2 * n], refs[2 * n:3 * n], refs[4 * n]
        x, y, c, others = _place()
        for i in range(n):
            h = bufs[i].shape[1] // 2
            for f, chip in enumerate(others):
                for s0, sz in _row_pieces(h, DMA_PIECES):
                    mine = _chip_rows(ins[i], (x, y), c * h + s0, sz)
                    _rcopy(mine, mine, send[i], recv[i], f, (*chip, c)).start()
        token[...] = jnp.zeros_like(token)

    res = pl.pallas_call(
        body, name=name, in_specs=[HBM] * n,
        out_specs=[SEM] * (2 * n) + [HBM] * n + [pl.BlockSpec(memory_space=pltpu.VMEM)],
        out_shape=[pltpu.SemaphoreType.DMA((N_OTHERS,))] * (2 * n) + [pltpu.HBM(b.shape, b.dtype) for b in bufs]
        + [jax.ShapeDtypeStruct((1, D_MODEL), F32)],
        input_output_aliases={i: 2 * n + i for i in range(n)},
        compiler_params=SPLIT_COPY,
    )(*[_hbm(b) for b in bufs])
    return res[:n], res[n:2 * n], res[2 * n:3 * n], res[3 * n]


def allgather_wait(buf, send, recv, after, *, name):
    h = buf.shape[1] // 2

    def body(buf_ref, send_sems, recv_sems, after_ref, out_ref):
        del after_ref, out_ref
        x, y, c, others = _place()
        for f, chip in enumerate(others):
            mine = _chip_rows(buf_ref, (x, y), c * h, h)
            theirs = _chip_rows(buf_ref, chip, c * h, h)
            cp = _rcopy(mine, theirs, send_sems, recv_sems, f, (*chip, c))
            cp.wait_send()
            cp.wait_recv()

    return pl.pallas_call(
        body, name=name, in_specs=[HBM, SEM, SEM, HBM], out_specs=HBM,
        out_shape=pltpu.HBM(buf.shape, buf.dtype), input_output_aliases={0: 0},
        compiler_params=SPLIT_COPY,
    )(buf, send, recv, after)


def allgather_forward(buf, *, name):
    h = buf.shape[1] // 2

    def body(in_ref, out_ref, send_sems, recv_sems):
        del in_ref
        x, y, c, others = _place()
        sibling = (x, y, 1 - c)
        sent = []
        for f, chip in enumerate(others):
            sent.append(_start_pieces(
                lambda s0, sz: _rcopy(_chip_rows(out_ref, chip, c * h + s0, sz), _chip_rows(out_ref, chip, c * h + s0, sz),
                                      send_sems, recv_sems, f, sibling), h, DMA_PIECES))
        for f, chip in enumerate(others):
            blk = _chip_rows(out_ref, chip, (1 - c) * h, h)
            _rcopy(blk, blk, send_sems, recv_sems, f, sibling).wait_recv()
        for cp in sent:
            cp.wait_send()

    return pl.pallas_call(
        body, name=name, in_specs=[HBM], out_specs=HBM,
        out_shape=jax.ShapeDtypeStruct(buf.shape, buf.dtype), input_output_aliases={0: 0},
        scratch_shapes=_dma_sems(N_OTHERS, N_OTHERS),
    )(buf)


def _sem1():
    return pltpu.SemaphoreType.DMA((1,))


TOKEN = jax.ShapeDtypeStruct((1, D_MODEL), F32)
VMEM_SPEC = pl.BlockSpec(memory_space=pltpu.VMEM)


def swap_start(grad, *, name):
    h = grad.shape[1] // 2

    def body(g_ref, land_ref, send, recv, g_out, land_out, token):
        del g_out, land_out
        x, y, c, _ = _place()
        for k in range(N_CHIPS):
            for s0, sz in _row_pieces(h, DMA_PIECES):
                _rcopy(g_ref.at[k, pl.ds((1 - c) * h + s0, sz)], land_ref.at[k, pl.ds(s0, sz)], send, recv, 0, (x, y, 1 - c)).start()
        token[...] = jnp.zeros_like(token)

    land = lax.empty((N_CHIPS, h, SLAB), grad.dtype)
    return pl.pallas_call(
        body, name=name, in_specs=[HBM, HBM], out_specs=[SEM, SEM, HBM, HBM, VMEM_SPEC],
        out_shape=[_sem1(), _sem1(), pltpu.HBM(grad.shape, grad.dtype), pltpu.HBM(land.shape, land.dtype), TOKEN],
        input_output_aliases={0: 2, 1: 3}, compiler_params=SPLIT_COPY,
    )(_hbm(grad), _hbm(land))


def swap_wait(grad, land, send, recv, after, *, name):
    h = land.shape[1]

    def body(g_ref, land_ref, send_sems, recv_sems, after_ref, g_out, land_out):
        del after_ref, g_out, land_out
        x, y, c, _ = _place()
        cp = _rcopy(g_ref.at[:, pl.ds((1 - c) * h, h)], land_ref, send_sems, recv_sems, 0, (x, y, 1 - c))
        cp.wait_send()
        cp.wait_recv()

    return pl.pallas_call(
        body, name=name, in_specs=[HBM, HBM, SEM, SEM, HBM], out_specs=[HBM, HBM],
        out_shape=[pltpu.HBM(grad.shape, grad.dtype), pltpu.HBM(land.shape, land.dtype)],
        input_output_aliases={0: 0, 1: 1}, compiler_params=SPLIT_COPY,
    )(grad, land, send, recv, after)


def scatter_start(part, *, name):
    h = part.shape[1]

    def body(part_ref, land_ref, send, recv, part_out, land_out, token):
        del part_out, land_out
        x, y, c, others = _place()
        for f, chip in enumerate(others):
            for s0, sz in _row_pieces(h, DMA_PIECES):
                _rcopy(_chip_rows(part_ref, chip, s0, sz), land_ref.at[f, pl.ds(s0, sz)], send, recv, f, (*chip, c)).start()
        token[...] = jnp.zeros_like(token)

    land = lax.empty((N_OTHERS,) + part.shape[1:], part.dtype)
    return pl.pallas_call(
        body, name=name, in_specs=[HBM, HBM],
        out_specs=[SEM, SEM, HBM, HBM, pl.BlockSpec(memory_space=pltpu.VMEM)],
        out_shape=[pltpu.SemaphoreType.DMA((N_OTHERS,))] * 2 + [pltpu.HBM(part.shape, part.dtype), pltpu.HBM(land.shape, land.dtype),
                                                                 jax.ShapeDtypeStruct((1, D_MODEL), F32)],
        input_output_aliases={0: 2, 1: 3},
        compiler_params=SPLIT_COPY,
    )(_hbm(part), _hbm(land))


def scatter_wait(part, land, send, recv, after, *, name):
    h = part.shape[1]

    def body(part_ref, land_ref, send_sems, recv_sems, after_ref, part_out, land_out):
        del after_ref, part_out, land_out
        x, y, c, others = _place()
        for f, chip in enumerate(others):
            cp = _rcopy(_chip_rows(part_ref, chip, 0, h), land_ref.at[f], send_sems, recv_sems, f, (*chip, c))
            cp.wait_send()
            cp.wait_recv()

    return pl.pallas_call(
        body, name=name, in_specs=[HBM, HBM, SEM, SEM, HBM], out_specs=[HBM, HBM],
        out_shape=[pltpu.HBM(part.shape, part.dtype), pltpu.HBM(land.shape, land.dtype)],
        input_output_aliases={0: 0, 1: 1},
        compiler_params=SPLIT_COPY,
    )(part, land, send, recv, after)


def join_start(buf, *, name):
    h = buf.shape[0] // 2

    def body(b_ref, send, recv, b_out, token):
        del b_out
        x, y, c, _ = _place()
        for s0, sz in _row_pieces(h, 2 * DMA_PIECES):
            rows = b_ref.at[pl.ds(c * h + s0, sz)]
            _rcopy(rows, rows, send, recv, 0, (x, y, 1 - c)).start()
        token[...] = jnp.zeros_like(token)

    return pl.pallas_call(
        body, name=name, in_specs=[HBM], out_specs=[SEM, SEM, HBM, VMEM_SPEC],
        out_shape=[_sem1(), _sem1(), pltpu.HBM(buf.shape, buf.dtype), TOKEN],
        input_output_aliases={0: 2}, compiler_params=SPLIT_COPY,
    )(_hbm(buf))


def join_wait(buf, send, recv, after, *, name):
    h = buf.shape[0] // 2

    def body(b_ref, send_sems, recv_sems, after_ref, b_out):
        del after_ref, b_out
        x, y, c, _ = _place()
        cp = _rcopy(b_ref.at[pl.ds(c * h, h)], b_ref.at[pl.ds((1 - c) * h, h)], send_sems, recv_sems, 0, (x, y, 1 - c))
        cp.wait_send()
        cp.wait_recv()

    return pl.pallas_call(
        body, name=name, in_specs=[HBM, SEM, SEM, HBM], out_specs=HBM,
        out_shape=pltpu.HBM(buf.shape, buf.dtype), input_output_aliases={0: 0}, compiler_params=SPLIT_COPY,
    )(buf, send, recv, after)


def allreduce_small(v, *, name):
    rows, n = v.shape

    def body(x_ref, sum_ref, all_ref, send_sems, recv_sems, local_sem):
        x, y, c, others = _place()
        me, sibling = (x, y, c), (x, y, 1 - c)

        def blk(px, py, pc):
            return all_ref.at[pl.ds((4 * px + 2 * py + pc) * rows, rows), :]

        def copy(k, block, to, src=None):
            return _rcopy(blk(*block) if src is None else src, blk(*block), send_sems, recv_sems, k, to)

        mine = pltpu.make_async_copy(x_ref, blk(*me), local_sem)
        mine.start()
        first = [copy(0, me, sibling, src=x_ref)]
        first += [copy(1 + f, me, (*chip, c), src=x_ref) for f, chip in enumerate(others)]
        for cp in first:
            cp.start()
        passed = [copy(4 + f, (*chip, c), sibling) for f, chip in enumerate(others)]
        for f, chip in enumerate(others):
            copy(1 + f, (*chip, c), me).wait_recv()
            passed[f].start()
        copy(0, sibling, me).wait_recv()
        for f, chip in enumerate(others):
            copy(4 + f, (*chip, 1 - c), me).wait_recv()
        for cp in first + passed:
            cp.wait_send()
        mine.wait()
        acc = all_ref[pl.ds(0, rows), :]
        for d in range(1, N_DEVICES):
            acc = acc + all_ref[pl.ds(d * rows, rows), :]
        sum_ref[...] = acc

    vm = pl.BlockSpec(memory_space=pltpu.VMEM)
    return pl.pallas_call(
        body, name=name, in_specs=[vm], out_specs=[vm, vm],
        out_shape=[jax.ShapeDtypeStruct((rows, n), F32), jax.ShapeDtypeStruct((N_DEVICES * rows, n), F32)],
        scratch_shapes=_dma_sems(7, 7) + [pltpu.SemaphoreType.DMA],
    )(v)[0]


def add_pairs(grad, theirs, where, *, name):
    h = theirs.shape[1]
    spec = pl.BlockSpec((None, h, SLAB), lambda k, w: (k, 0, 0))

    def body(w_ref, a_ref, b_ref, o_ref):
        del w_ref
        o_ref[...] = (a_ref[...].astype(F32) + b_ref[...].astype(F32)).astype(o_ref.dtype)

    return pl.pallas_call(
        body, name=name,
        grid_spec=pltpu.PrefetchScalarGridSpec(
            num_scalar_prefetch=1, grid=(N_CHIPS,),
            in_specs=[pl.BlockSpec((None, h, SLAB), lambda k, w: (k, w[1], 0)), spec], out_specs=spec),
        out_shape=jax.ShapeDtypeStruct(theirs.shape, theirs.dtype),
        compiler_params=_cparams(("parallel",)))(where, grad, theirs)


def add_chips(pair, got, where, *, name):
    h = pair.shape[1]
    tr = h // 2

    def body(w_ref, a_ref, b_ref, o_ref):
        del w_ref
        acc = a_ref[...].astype(F32)
        for f in range(3):
            acc = acc + b_ref[f].astype(F32)
        o_ref[...] = acc

    return pl.pallas_call(
        body, name=name,
        grid_spec=pltpu.PrefetchScalarGridSpec(
            num_scalar_prefetch=1, grid=(2,),
            in_specs=[pl.BlockSpec((None, tr, SLAB), lambda i, w: (w[0], i, 0)),
                      pl.BlockSpec((3, tr, SLAB), lambda i, w: (0, i, 0))],
            out_specs=pl.BlockSpec((tr, SLAB), lambda i, w: (2 * w[1] + i, 0))),
        out_shape=jax.ShapeDtypeStruct((2 * h, SLAB), F32),
        compiler_params=_cparams(("parallel",)))(where, pair, got)


DEPTH = 4
MIXER = (0, 1, 2, 0)
W_IN_COLS = (768, 320, 772)
W_IN_PAD = (768, 512, 1024)
MATS = ("up", "down", "inp", "out", "gate", "proj")
MAT_ARG = dict(up="w_up", down="w_down", inp="w_in", out="w_out", gate="w_ple_gate", proj="w_ple_proj")
GAINS = ("attn_norm", "mlp_norm", "ple_norm")
N_SMALL = 16
KINDS = ("grad_", "delta_", "new_m_", "new_v_")


def _layout(kind):
    ns_in = W_IN_PAD[kind] // SLAB
    off = 8192 + 1024 * ns_in
    lay = dict(up=(0, 1024, 4, False), down=(4096, 1024, 4, True), inp=(8192, 1024, ns_in, False),
               out=(off, 256, 4, True), gate=(off + 1024, 256, 4, True), proj=(off + 2048, 256, 1, False))
    return lay, off + 2304


def _to_slabs(w):
    k, c = w.shape
    return w.reshape(k, c // SLAB, SLAB).transpose(1, 0, 2).reshape(-1, SLAB)


def _pad_cols(w, n):
    return jnp.pad(w, ((0, 0), (0, n - w.shape[1])))


def _heads(x2d, n):
    return x2d.reshape(x2d.shape[0], n, HEAD_DIM).transpose(1, 0, 2)


def _unheads(x3d):
    n, s, _ = x3d.shape
    return x3d.transpose(1, 0, 2).reshape(s, n * HEAD_DIM)


def _chip_cols(x2d, c, cpad):
    return jnp.concatenate([_pad_cols(x2d[:, j * c:(j + 1) * c], cpad) for j in range(N_CHIPS)], axis=1)


def _unchip_cols(x2d, c, cpad):
    return jnp.concatenate([x2d[:, j * cpad:j * cpad + c] for j in range(N_CHIPS)], axis=1)


def _forget_cols(wg, t):
    off, K, _, _ = t
    cols = []
    for g in range(3 * N_HEADS * HEAD_DIM, 3 * N_HEADS * HEAD_DIM + N_HEADS):
        chip, local = divmod(g, W_IN_COLS[2])
        q, c = divmod(local, SLAB)
        cols.append(wg[chip, off + q * K:off + (q + 1) * K, c:c + 1])
    return jnp.concatenate(cols, axis=1)


def _add_res(acc, res):
    return (acc + res,)


def _relu2(acc):
    return acc, jnp.square(jnp.maximum(acc, 0.0))


def _relu2_bwd(acc, u):
    return (acc * (2.0 * jnp.maximum(u.astype(F32), 0.0)),)


def _ple_fwd(acc, x2, pp):
    return x2 + pp * _sigmoid(acc), acc


def _ple_bwd(dx, pp, gl):
    gate = _sigmoid(gl)
    return dx * gate, dx * pp * gate * (1.0 - gate)


def _layer_fwd(i, kind, x0, p_bf, wg, lay, gains, extra, tabs):
    s = x0.shape[0]
    an, mn, pn = gains
    sv = dict(x0=x0)
    if kind == 0:
        proj, h1 = mm_nn(x0, wg, lay["inp"], name=f"w_in_{i}", norm_gain=an)
        a, tot = sb_fwd(proj, name=f"sb_fwd_{i}")
        sv.update(proj=proj, tot=tot)
    elif kind == 1:
        projp, h1 = mm_nn(x0, wg, lay["inp"], name=f"w_in_{i}", out_dtypes=(F32,), norm_gain=an)
        proj = _unchip_cols(projp, W_IN_COLS[1], W_IN_PAD[1])
        nq = N_HEADS * HEAD_DIM
        nqk = nq + SWA_KV_HEADS * HEAD_DIM
        qk = rope_fwd(proj[:, :nqk], tabs, name=f"rope_{i}")
        q = _heads(qk[:, :nq], N_HEADS).reshape(SWA_KV_HEADS, SWA_GROUP, s, HEAD_DIM)
        front = ((0, 0), (SWA_WINDOW, 0), (0, 0))
        kp = jnp.pad(_heads(qk[:, nq:], SWA_KV_HEADS), front)
        vp = jnp.pad(_heads(proj[:, nqk:].astype(BF16), SWA_KV_HEADS), front)
        sink = jnp.repeat(extra.reshape(SWA_KV_HEADS, SWA_GROUP), ATT_BLK, axis=1)[:, :, None]
        o4, lse = swa_fwd(q, kp, vp, sink, name=f"swa_fwd_{i}")
        a = _unheads(o4.reshape(N_HEADS, s, HEAD_DIM))
        sv.update(q=q, kp=kp, vp=vp, sink=sink, o4=o4, lse=lse)
    else:
        projp, h1 = mm_nn(x0, wg, lay["inp"], name=f"w_in_{i}", norm_gain=an)
        nqkv = 3 * N_HEADS * HEAD_DIM
        proj = _unchip_cols(projp, W_IN_COLS[2], W_IN_PAD[2])[:, :nqkv]
        fl = mm_plain(h1, _pad_cols(_forget_cols(wg, lay["inp"]), 128), name=f"w_forget_{i}")
        bp = _pad_cols(extra[None], 128)
        cum_t = fox_gate_fwd(fl, bp, name=f"gate_fwd_{i}")[:, :N_HEADS].T
        cq = cum_t[:, :, None]
        ck = cum_t.reshape(N_HEADS, s // min(ATT_BK, s), 1, min(ATT_BK, s))
        a, lse = fox_fwd(proj, cq, ck, name=f"fox_fwd_{i}")
        sv.update(proj=proj, fl=fl, bp=bp, cq=cq, ck=ck, lse=lse)
    x1 = mm_nn(a, wg, lay["out"], name=f"w_out_{i}", epi=_add_res, extras=(x0,), out_dtypes=(F32,))[0]
    u, r, h2 = mm_nn(x1, wg, lay["up"], name=f"w_up_{i}", epi=_relu2, out_dtypes=(BF16, BF16), norm_gain=mn)
    x2 = mm_nn(r, wg, lay["down"], name=f"w_down_{i}", epi=_add_res, extras=(x1,), out_dtypes=(F32,))[0]
    pp = mm_nn(p_bf, wg, lay["proj"], name=f"w_ple_proj_{i}", out_dtypes=(F32,))[0]
    x3, gl, h3 = mm_nn(x2, wg, lay["gate"], name=f"w_ple_gate_{i}", epi=_ple_fwd, extras=(x2, pp), out_dtypes=(F32, F32),
                       norm_gain=pn)
    sv.update(h1=h1, a=a, x1=x1, h2=h2, u=u, r=r, x2=x2, h3=h3, pp=pp, gl=gl)
    return x3, sv


def _layer_bwd(i, kind, dx3, sv, p_bf, wg, lay, n_rows, gains, tabs, mid):
    s = dx3.shape[0]
    an, mn, pn = gains
    g = lax.empty((N_CHIPS, n_rows, SLAB), BF16)
    d_pp, d_gl = ew(_ple_bwd, [dx3, sv["pp"], sv["gl"]], [BF16, BF16], name=f"ple_bwd_{i}")
    g = mm_tn(p_bf, d_pp, g, lay["proj"], name=f"dw_ple_proj_{i}")
    g = mm_tn(sv["h3"], d_gl, g, lay["gate"], name=f"dw_ple_gate_{i}")
    d_h3 = mm_nt(d_gl, wg, lay["gate"], name=f"dx_ple_gate_{i}", out_dtypes=(F32,))[0]
    dx2, dx2b, d_pn = rms_bwd(sv["x2"], pn, d_h3, dx3, name=f"ple_norm_bwd_{i}")
    zero = mid(dx2)
    if zero is not None:
        mn = mn + zero
    g = mm_tn(sv["r"], dx2b, g, lay["down"], name=f"dw_down_{i}")
    d_u = mm_nt(dx2b, wg, lay["down"], name=f"dx_down_{i}", epi=_relu2_bwd, extras=(sv["u"],))[0]
    g = mm_tn(sv["h2"], d_u, g, lay["up"], name=f"dw_up_{i}")
    dx1, dx1b, d_mn = mm_nt(d_u, wg, lay["up"], name=f"dx_up_{i}", rms=(sv["x1"], mn, dx2))
    g = mm_tn(sv["a"], dx1b, g, lay["out"], name=f"dw_out_{i}")
    d_a = mm_nt(dx1b, wg, lay["out"], name=f"dx_out_{i}")[0]
    d_extra = None
    if kind == 0:
        d_proj = jnp.concatenate(sb_bwd(sv["proj"], sv["tot"], d_a, name=f"sb_bwd_{i}"), axis=1)
    elif kind == 1:
        do4 = _heads(d_a, N_HEADS).reshape(SWA_KV_HEADS, SWA_GROUP, s, HEAD_DIM)
        dq, dkp, dvp, dsr = swa_bwd(sv["q"], sv["kp"], sv["vp"], sv["sink"], sv["o4"], sv["lse"], do4, name=f"swa_bwd_{i}")
        dqk = jnp.concatenate([_unheads(dq.reshape(N_HEADS, s, HEAD_DIM)), _unheads(dkp[:, SWA_WINDOW:])], axis=1)
        dqk = rope_bwd(dqk, tabs, name=f"rope_bwd_{i}")
        d_proj = jnp.concatenate([dqk, _unheads(dvp[:, SWA_WINDOW:]).astype(BF16)], axis=1)
        d_proj = _chip_cols(d_proj, W_IN_COLS[1], W_IN_PAD[1])
        d_extra = jnp.sum(dsr[..., 0], axis=2).reshape(N_HEADS)
    else:
        dq, dk, dv, dcq, dck = fox_bwd(sv["proj"], sv["a"], sv["lse"], sv["cq"], sv["ck"], d_a, name=f"fox_bwd_{i}")
        dcum = _pad_cols((dcq[:, :, 0] - dck.reshape(N_HEADS, s)).T, 128)
        dfl, dbp = fox_gate_bwd(dcum, sv["fl"], sv["bp"], name=f"gate_bwd_{i}")
        d_proj = jnp.concatenate([dq, dk, dv, dfl[:, :N_HEADS].astype(BF16)], axis=1)
        d_proj = _chip_cols(d_proj, W_IN_COLS[2], W_IN_PAD[2])
        d_extra = dbp[0, :N_HEADS]
    g = mm_tn(sv["h1"], d_proj, g, lay["inp"], name=f"dw_in_{i}")
    dx0, _, d_an = mm_nt(d_proj, wg, lay["inp"], name=f"dx_in_{i}", rms=(sv["x0"], an, dx1))
    return dx0, g, (d_an, d_mn, d_pn), d_extra


def _small_rows(a, prefix):
    rows = [a[f"{prefix}{n}_{i}"] for i in range(DEPTH) for n in GAINS] + [a[f"{prefix}final_norm"]]
    rows += [_pad_cols(a[f"{prefix}{n}"][None], D_MODEL)[0] for n in ("sinks_1", "b_forget_2")]
    return jnp.stack(rows + [jnp.zeros((D_MODEL,), F32)])


def _train_step(a):
    x = a["x"][0]
    tabs = rope_tables(x.shape[0])
    lays = [_layout(k) for k in MIXER]

    def natural(prefix, i, m):
        w = a[f"{prefix}{MAT_ARG[m]}_{i}"]
        return _pad_cols(w, W_IN_PAD[MIXER[i]]) if m == "inp" else w

    chip = 2 * lax.axis_index("x") + lax.axis_index("y")
    where = jnp.stack([chip, lax.axis_index("c")]).astype(jnp.int32)
    def own_block(i, zero):
        pk = jnp.concatenate([_to_slabs((natural("", i, m) + zero).astype(BF16)) for m in MATS], axis=0)
        return lax.dynamic_update_slice(lax.empty((N_CHIPS,) + pk.shape, BF16), pk[None], (chip, 0, 0))

    sends, recvs, bufs, token = allgather_start([own_block(0, 0.0)], name="allgather_start_0")
    more = allgather_start([own_block(i, token[0, 0]) for i in range(1, DEPTH)], name="allgather_start_1")
    sends, recvs, bufs, token = sends + more[0], recvs + more[1], bufs + more[2], more[3]

    gains = [tuple(a[f"{n}_{i}"][None] for n in GAINS) for i in range(DEPTH)]
    extras = [None, a["sinks_1"], a["b_forget_2"], None]
    p_bf = [a["p"][i, 0].astype(BF16) for i in range(DEPTH)]

    saved, wgs, after = [], [], token
    for i in range(DEPTH):
        landed = allgather_wait(bufs[i], sends[i], recvs[i], after, name=f"allgather_wait_{i}")
        wgs.append(allgather_forward(landed, name=f"allgather_forward_{i}"))
        x, sv = _layer_fwd(i, MIXER[i], x, p_bf[i], wgs[i], lays[i][0], gains[i], extras[i], tabs)
        saved.append(sv)
        after = x
    dx, d_final, loss = loss_head(x, a["final_norm"][None], a["loss_target"][0], name="loss_head")

    def pair_and_scatter(j, swapped, after):
        send, recv, grad, land, _ = swapped
        grad, theirs = swap_wait(grad, land, send, recv, after, name=f"swap_wait_{j}")
        return scatter_start(add_pairs(grad, theirs, where, name=f"add_pairs_{j}"), name=f"scatter_start_{j}")

    def sum_and_join(j, scattered, after):
        send, recv, part, land, _ = scattered
        part, got = scatter_wait(part, land, send, recv, after, name=f"scatter_wait_{j}")
        return join_start(add_chips(part, got, where, name=f"add_chips_{j}"), name=f"join_start_{j}")

    small = [None] * N_SMALL
    small[12] = d_final[0]
    small[15] = _pad_cols(loss[:, :1], D_MODEL)[0]
    joined = [None] * DEPTH
    state = dict(swapped=None, scattered=None)
    for i in reversed(range(DEPTH)):
        an, mn, pn = gains[i]
        if state["swapped"] is not None:
            pn = pn + state["swapped"][4]

        def mid(dx2):
            if state["swapped"] is None:
                return None
            state["scattered"] = pair_and_scatter(i + 1, state["swapped"], dx2)
            return state["scattered"][4]

        dx, grad, d_gains, d_extra = _layer_bwd(i, MIXER[i], dx, saved[i], p_bf[i], wgs[i], lays[i][0], lays[i][1],
                                                (an, mn, pn), tabs, mid)
        for j in range(3):
            small[3 * i + j] = d_gains[j][0]
        if d_extra is not None:
            small[12 + MIXER[i]] = _pad_cols(d_extra[None], D_MODEL)[0]
        if state["scattered"] is not None:
            joined[i + 1] = sum_and_join(i + 1, state["scattered"], dx)
        state["swapped"] = swap_start(grad, name=f"swap_start_{i}")
    started = pair_and_scatter(0, state["swapped"], dx)
    small = allreduce_small(jnp.stack(small), name="allreduce_small")

    out = {"loss": small[15, 0], "grad_x": dx[None]}
    res = adamw_small(small, _small_rows(a, ""), _small_rows(a, "m_"), _small_rows(a, "v_"), name="adamw_small")
    prev = started[4]
    for i in reversed(range(DEPTH)):
        if i == 0:
            joined[0] = sum_and_join(0, started, prev)
        send, recv, buf, zero = joined[i]
        gfull = join_wait(buf, send, recv, prev if i else zero, name=f"join_wait_{i}")
        for m in MATS:
            upd = adamw(gfull, lays[i][0][m], natural("", i, m), natural("m_", i, m), natural("v_", i, m), prev,
                        name=f"adamw_{MAT_ARG[m]}_{i}")
            prev = upd[1]
            cols = a[f"{MAT_ARG[m]}_{i}"].shape[1]
            for kd, r in zip(KINDS, upd):
                out[f"{kd}{MAT_ARG[m]}_{i}"] = r[:, :cols]
    for kd, r in zip(KINDS, res):
        for i in range(DEPTH):
            for j, n in enumerate(GAINS):
                out[f"{kd}{n}_{i}"] = r[3 * i + j][0]
        out[f"{kd}final_norm"] = r[12][0]
        out[f"{kd}sinks_1"] = r[13][0, :N_HEADS]
        out[f"{kd}b_forget_2"] = r[14][0, :N_HEADS]
    return out


def _weight_names():
    names = []
    for i in range(DEPTH):
        names += [f"attn_norm_{i}", f"w_in_{i}", f"w_out_{i}"] + [[], ["sinks_1"], ["b_forget_2"]][MIXER[i]]
        names += [f"mlp_norm_{i}", f"w_up_{i}", f"w_down_{i}", f"ple_norm_{i}", f"w_ple_gate_{i}", f"w_ple_proj_{i}"]
    return names + ["final_norm"]


def kernel(x, p, attn_norm_0, w_in_0, w_out_0, mlp_norm_0, w_up_0, w_down_0, ple_norm_0, w_ple_gate_0, w_ple_proj_0, attn_norm_1, w_in_1, w_out_1, sinks_1, mlp_norm_1, w_up_1, w_down_1, ple_norm_1, w_ple_gate_1, w_ple_proj_1, attn_norm_2, w_in_2, w_out_2, b_forget_2, mlp_norm_2, w_up_2, w_down_2, ple_norm_2, w_ple_gate_2, w_ple_proj_2, attn_norm_3, w_in_3, w_out_3, mlp_norm_3, w_up_3, w_down_3, ple_norm_3, w_ple_gate_3, w_ple_proj_3, final_norm, loss_target, m_attn_norm_0, m_w_in_0, m_w_out_0, m_mlp_norm_0, m_w_up_0, m_w_down_0, m_ple_norm_0, m_w_ple_gate_0, m_w_ple_proj_0, m_attn_norm_1, m_w_in_1, m_w_out_1, m_sinks_1, m_mlp_norm_1, m_w_up_1, m_w_down_1, m_ple_norm_1, m_w_ple_gate_1, m_w_ple_proj_1, m_attn_norm_2, m_w_in_2, m_w_out_2, m_b_forget_2, m_mlp_norm_2, m_w_up_2, m_w_down_2, m_ple_norm_2, m_w_ple_gate_2, m_w_ple_proj_2, m_attn_norm_3, m_w_in_3, m_w_out_3, m_mlp_norm_3, m_w_up_3, m_w_down_3, m_ple_norm_3, m_w_ple_gate_3, m_w_ple_proj_3, m_final_norm, v_attn_norm_0, v_w_in_0, v_w_out_0, v_mlp_norm_0, v_w_up_0, v_w_down_0, v_ple_norm_0, v_w_ple_gate_0, v_w_ple_proj_0, v_attn_norm_1, v_w_in_1, v_w_out_1, v_sinks_1, v_mlp_norm_1, v_w_up_1, v_w_down_1, v_ple_norm_1, v_w_ple_gate_1, v_w_ple_proj_1, v_attn_norm_2, v_w_in_2, v_w_out_2, v_b_forget_2, v_mlp_norm_2, v_w_up_2, v_w_down_2, v_ple_norm_2, v_w_ple_gate_2, v_w_ple_proj_2, v_attn_norm_3, v_w_in_3, v_w_out_3, v_mlp_norm_3, v_w_up_3, v_w_down_3, v_ple_norm_3, v_w_ple_gate_3, v_w_ple_proj_3, v_final_norm):
    out = _train_step(dict(locals()))
    return (out["loss"], out["grad_x"], *[out[kd + n] for kd in KINDS for n in _weight_names()])
```

```python
import jax
import jax.numpy as jnp
from jax import lax
from jax.experimental import pallas as pl
from jax.experimental.pallas import tpu as pltpu

F32 = jnp.float32
BF16 = jnp.bfloat16

D_MODEL = 1024
N_HEADS = 16
HEAD_DIM = 64
SWA_KV_HEADS = 2
SWA_GROUP = 8
SWA_WINDOW = 128
ROPE_THETA = 500000.0
ROPE_DIM = 16
RMS_EPS = 1e-6
NEG_INF = -1e30
ATTN_SCALE = HEAD_DIM ** -0.5
N_CHIPS = 4
N_DEVICES = 8

SLAB = 256
ATT_BLK = 128
ATT_BQ = 512
ATT_BK = 512
ROW_TILE = 256
V7X_VMEM_LIMIT = 56 * 1024 * 1024

ADAM_LR, ADAM_B1, ADAM_B2, ADAM_EPS, ADAM_WD, ADAM_STEP = 0.001, 0.9, 0.999, 1e-08, 0.01, 10


def _cparams(sem=None):
    return pltpu.CompilerParams(dimension_semantics=sem, vmem_limit_bytes=V7X_VMEM_LIMIT)


def _dot(a, b):
    return jnp.dot(a, b, preferred_element_type=F32)


def _dot_nt(a, b):
    return lax.dot_general(a, b, (((1,), (1,)), ((), ())), preferred_element_type=F32)


def _dot_tn(a, b):
    return lax.dot_general(a, b, (((0,), (0,)), ((), ())), preferred_element_type=F32)


def _row_tile(M, K):
    return min(M, 1024) if K >= 1024 else M


def _finish(epi, acc, ex, outs):
    res = epi(acc, *[e[...] for e in ex]) if epi is not None else (acc,)
    for o, r in zip(outs, res):
        o[...] = r.astype(o.dtype)


def _once(shape, index_map):
    return pl.BlockSpec(shape, index_map, pipeline_mode=pl.Buffered(1))


def mm_nn(a, wg, t, *, name, epi=None, extras=(), out_dtypes=(BF16,), norm_gain=None):
    off, K, ns, row = t
    M = a.shape[0]
    sb = off // K
    ne, no = len(extras), len(out_dtypes)
    norm = norm_gain is not None
    if row:
        tm = M if norm else _row_tile(M, K)
        nb = N_CHIPS
        grid = (M // tm, ns)
        a_shape = (tm, N_CHIPS * K)
        a_spec = (_once if norm else pl.BlockSpec)(a_shape, lambda i, q: (i, 0))
        b_specs = [pl.BlockSpec((None, K, SLAB), lambda i, q, j=j: (j, sb + q, 0)) for j in range(nb)]
        tile = pl.BlockSpec((tm, SLAB), lambda i, q: (i, q))
        n_out = ns * SLAB
    else:
        nb = ns
        grid = (N_CHIPS,)
        a_shape = (M, K)
        a_spec = (_once if norm else pl.BlockSpec)(a_shape, lambda j: (0, 0))
        b_specs = [pl.BlockSpec((None, K, SLAB), lambda j, q=q: (j, sb + q, 0)) for q in range(ns)]
        tile = pl.BlockSpec((M, ns * SLAB), lambda j: (0, j))
        n_out = N_CHIPS * ns * SLAB

    def body(a_ref, *rest):
        if norm:
            g_ref, rest, h_out, h_ref = rest[0], rest[1:-2], rest[-2], rest[-1]

            @pl.when(pl.program_id(len(grid) - 1) == 0)
            def _():
                xv = a_ref[...]
                h_ref[...] = (xv * _rstd(xv) * g_ref[...]).astype(BF16)
                h_out[...] = h_ref[...]

            a_ref = h_ref
        bs, ex, outs = rest[:nb], rest[nb:nb + ne], rest[nb + ne:]
        if row:
            acc = _dot(a_ref[:, pl.ds(0, K)], bs[0][...])
            for j in range(1, nb):
                acc = acc + _dot(a_ref[:, pl.ds(j * K, K)], bs[j][...])
            _finish(epi, acc, ex, outs)
        else:
            av = a_ref[...]
            for q in range(ns):
                cols = pl.ds(q * SLAB, SLAB)
                _finish(epi, _dot(av, bs[q][...]), [e.at[:, cols] for e in ex], [o.at[:, cols] for o in outs])

    h_spec = _once(a_shape, (lambda i, q: (i, 0)) if row else (lambda j: (0, 0)))
    return pl.pallas_call(
        body, name=name, grid=grid,
        in_specs=[a_spec] + ([pl.BlockSpec(norm_gain.shape, lambda *_: (0, 0))] if norm else []) + b_specs + [tile] * ne,
        out_specs=[tile] * no + ([h_spec] if norm else []),
        out_shape=[jax.ShapeDtypeStruct((M, n_out), d) for d in out_dtypes]
        + ([jax.ShapeDtypeStruct(a.shape, BF16)] if norm else []),
        scratch_shapes=[pltpu.VMEM(a_shape, BF16)] if norm else [],
        compiler_params=_cparams((("arbitrary" if norm else "parallel"),) * len(grid)),
    )(a, *([norm_gain] if norm else []), *([wg] * nb), *extras)


def mm_nt(dy, wg, t, *, name, epi=None, extras=(), out_dtypes=(BF16,), rms=None):
    off, K, ns, row = t
    M = dy.shape[0]
    tm = _row_tile(M, K)
    sb = off // K
    if rms is not None:
        x, gain, dres = rms
        extras, out_dtypes = (x, dres), (F32, BF16)
    ne, no = len(extras), len(out_dtypes)
    grid = (M // tm, N_CHIPS)
    b_specs = [pl.BlockSpec((None, K, SLAB), lambda i, j, q=q: (j, sb + q, 0)) for q in range(ns)]
    if row:
        dy_spec = pl.BlockSpec((tm, ns * SLAB), lambda i, j: (i, 0))
        tile = pl.BlockSpec((tm, K), lambda i, j: (i, j))
        n_out = N_CHIPS * K
        sem = ("parallel", "parallel")
    else:
        dy_spec = pl.BlockSpec((tm, ns * SLAB), lambda i, j: (i, j))
        tile = pl.BlockSpec((tm, K), lambda i, j: (i, 0))
        n_out = K
        sem = ("arbitrary" if rms is not None else "parallel", "arbitrary")
    one = pl.BlockSpec((1, K), lambda i, j: (0, 0))

    def body(dy_ref, *rest):
        if rms is not None:
            g_ref, rest, dg_ref, acc_ref = rest[0], rest[1:-2], rest[-2], rest[-1]
            rest = rest + (acc_ref,)
        bs, ex, outs = rest[:ns], rest[ns:ns + ne], rest[ns + ne:ns + ne + no]
        part = _dot_nt(dy_ref[:, pl.ds(0, SLAB)], bs[0][...])
        for q in range(1, ns):
            part = part + _dot_nt(dy_ref[:, pl.ds(q * SLAB, SLAB)], bs[q][...])
        if row:
            _finish(epi, part, ex, outs)
        else:
            acc_ref = rest[-1]
            i, j = pl.program_id(0), pl.program_id(1)

            @pl.when(j == 0)
            def _():
                acc_ref[...] = part

            @pl.when(j > 0)
            def _():
                acc_ref[...] += part

            @pl.when(j == N_CHIPS - 1)
            def _():
                if rms is None:
                    _finish(epi, acc_ref[...], ex, outs)
                else:
                    dx, dg = _rms_bwd_tile(ex[0][...], g_ref[...], acc_ref[...])
                    dx = dx + ex[1][...]
                    outs[0][...] = dx
                    outs[1][...] = dx.astype(BF16)

                    @pl.when(i == 0)
                    def _():
                        dg_ref[...] = dg

                    @pl.when(i > 0)
                    def _():
                        dg_ref[...] += dg

    has = rms is not None
    return pl.pallas_call(
        body, name=name, grid=grid,
        in_specs=[dy_spec] + ([one] if has else []) + b_specs + [tile] * ne,
        out_specs=[tile] * no + ([one] if has else []),
        out_shape=[jax.ShapeDtypeStruct((M, n_out), d) for d in out_dtypes] + ([jax.ShapeDtypeStruct((1, K), F32)] if has else []),
        scratch_shapes=[] if row else [pltpu.VMEM((tm, K), F32)],
        compiler_params=_cparams(sem),
    )(dy, *([gain] if has else []), *([wg] * ns), *extras)


def mm_plain(a, b, *, name):
    M, K = a.shape
    N = b.shape[1]
    tm = min(M, 512)

    def body(a_ref, b_ref, o_ref):
        o_ref[...] = _dot(a_ref[...], b_ref[...])

    return pl.pallas_call(
        body, name=name, grid=(M // tm,),
        in_specs=[pl.BlockSpec((tm, K), lambda i: (i, 0)), pl.BlockSpec((K, N), lambda i: (0, 0))],
        out_specs=pl.BlockSpec((tm, N), lambda i: (i, 0)), out_shape=jax.ShapeDtypeStruct((M, N), F32),
        compiler_params=_cparams(("parallel",)),
    )(a, b)


def mm_tn(x, dy, g, t, *, name):
    off, K, ns, row = t
    S = x.shape[0]
    per = ns if off % (ns * K) == 0 else 1
    grid = (N_CHIPS, ns // per)
    if row:
        x_map = lambda j, q: (0, j)
        dy_map = lambda j, q: (0, q)
    else:
        x_map = lambda j, q: (0, 0)
        dy_map = lambda j, q: (0, j * (ns // per) + q)

    def body(g_in, x_ref, dy_ref, o_ref):
        del g_in
        xt = x_ref[...].T
        for q in range(per):
            o_ref[pl.ds(q * K, K), :] = _dot(xt, dy_ref[:, pl.ds(q * SLAB, SLAB)]).astype(o_ref.dtype)

    return pl.pallas_call(
        body, name=name, grid=grid,
        in_specs=[pl.BlockSpec(memory_space=pl.ANY), pl.BlockSpec((S, K), x_map), pl.BlockSpec((S, per * SLAB), dy_map)],
        out_specs=pl.BlockSpec((None, per * K, SLAB), lambda j, q: (j, off // (per * K) + q, 0)),
        out_shape=jax.ShapeDtypeStruct(g.shape, g.dtype),
        input_output_aliases={0: 0},
        compiler_params=_cparams(("parallel", "parallel")),
    )(g, x, dy)


def ew(fn, ins, out_dtypes, *, name, bcast=()):
    S = ins[0].shape[0]
    tr = min(ROW_TILE, S)
    cols = ins[0].shape[1]
    ni, nb = len(ins), len(bcast)

    def body(*refs):
        res = fn(*[r[...] for r in refs[:ni + nb]])
        for o, r in zip(refs[ni + nb:], res):
            o[...] = r.astype(o.dtype)

    return pl.pallas_call(
        body, name=name, grid=(S // tr,),
        in_specs=[pl.BlockSpec((tr, a.shape[1]), lambda i: (i, 0)) for a in ins]
        + [pl.BlockSpec(b.shape, lambda i: (0, 0)) for b in bcast],
        out_specs=[pl.BlockSpec((tr, cols), lambda i: (i, 0)) for _ in out_dtypes],
        out_shape=[jax.ShapeDtypeStruct((S, cols), d) for d in out_dtypes],
        compiler_params=_cparams(("parallel",)),
    )(*ins, *bcast)


def _rstd(x):
    return lax.rsqrt(jnp.mean(x * x, axis=-1, keepdims=True) + RMS_EPS)


def _sigmoid(x):
    return 1.0 / (1.0 + jnp.exp(-x))


def _log_sigmoid(z):
    return jnp.minimum(z, 0.0) - jnp.log(1.0 + jnp.exp(-jnp.abs(z)))


def _rms_bwd_tile(xv, gv, dh):
    rstd = _rstd(xv)
    xhat = xv * rstd
    gd = dh * gv
    dx = rstd * (gd - xhat * jnp.mean(xhat * gd, axis=-1, keepdims=True))
    return dx, jnp.sum(dh * xhat, axis=0, keepdims=True)


def rms_bwd(x, g, dh, dres, *, name):
    S, D = x.shape
    tr = min(ROW_TILE, S)

    def body(x_ref, g_ref, dh_ref, dres_ref, dx_ref, dxb_ref, dg_ref):
        i = pl.program_id(0)
        dx, dg = _rms_bwd_tile(x_ref[...], g_ref[...], dh_ref[...])
        dx = dx + dres_ref[...]
        dx_ref[...] = dx
        dxb_ref[...] = dx.astype(BF16)

        @pl.when(i == 0)
        def _():
            dg_ref[...] = dg

        @pl.when(i > 0)
        def _():
            dg_ref[...] += dg

    row = pl.BlockSpec((tr, D), lambda i: (i, 0))
    one = pl.BlockSpec((1, D), lambda i: (0, 0))
    return pl.pallas_call(
        body, name=name, grid=(S // tr,),
        in_specs=[row, one, row, row], out_specs=[row, row, one],
        out_shape=[jax.ShapeDtypeStruct((S, D), F32), jax.ShapeDtypeStruct((S, D), BF16),
                   jax.ShapeDtypeStruct((1, D), F32)],
        compiler_params=_cparams(("arbitrary",)),
    )(x, g, dh, dres)


def loss_head(x, g, target, *, name):
    S, D = x.shape
    tr = min(ROW_TILE, S)

    def body(x_ref, g_ref, t_ref, dx_ref, dg_ref, loss_ref):
        i = pl.program_id(0)
        xv, gv = x_ref[...], g_ref[...]
        err = xv * _rstd(xv) * gv - t_ref[...]
        part = 0.5 * jnp.sum(jnp.mean(err * err, axis=-1, keepdims=True), axis=0, keepdims=True)
        dx, dg = _rms_bwd_tile(xv, gv, err * (1.0 / D))
        dx_ref[...] = dx
        part = jnp.broadcast_to(part, loss_ref.shape)

        @pl.when(i == 0)
        def _():
            dg_ref[...] = dg
            loss_ref[...] = part

        @pl.when(i > 0)
        def _():
            dg_ref[...] += dg
            loss_ref[...] += part

    row = pl.BlockSpec((tr, D), lambda i: (i, 0))
    one = pl.BlockSpec((1, D), lambda i: (0, 0))
    return pl.pallas_call(
        body, name=name, grid=(S // tr,),
        in_specs=[row, one, row], out_specs=[row, one, pl.BlockSpec((1, 128), lambda i: (0, 0))],
        out_shape=[jax.ShapeDtypeStruct((S, D), F32), jax.ShapeDtypeStruct((1, D), F32),
                   jax.ShapeDtypeStruct((1, 128), F32)],
        compiler_params=_cparams(("arbitrary",)),
    )(x, g, target)


def rope_tables(S):
    half = ROPE_DIM // 2
    inv_freq = ROPE_THETA ** (-jnp.arange(half, dtype=F32) / half)
    ang = jnp.arange(S, dtype=F32)[:, None] * inv_freq[None, :]
    cos, sin = jnp.cos(ang), jnp.sin(ang)
    z = jnp.zeros((S, HEAD_DIM - ROPE_DIM), F32)
    zh = jnp.zeros((S, half), F32)
    c = jnp.concatenate([cos, cos, jnp.ones_like(z)], axis=1)
    sa = jnp.concatenate([zh, sin, z], axis=1)
    sb = jnp.concatenate([-sin, zh, z], axis=1)
    return [jnp.concatenate([t, t], axis=1) for t in (c, sa, sb)]


def _wide(t, n):
    return jnp.tile(t, (1, n // t.shape[1]))


def rope_fwd(xqk, tables, *, name):
    n, half = xqk.shape[1], ROPE_DIM // 2

    def fn(x, c, sa, sb):
        return (x * _wide(c, n) + pltpu.roll(x, half, 1) * _wide(sa, n) + pltpu.roll(x, n - half, 1) * _wide(sb, n),)

    return ew(fn, [xqk] + list(tables), [BF16], name=name)[0]


def rope_bwd(dy, tables, *, name):
    n, half = dy.shape[1], ROPE_DIM // 2

    def fn(d, c, sa, sb):
        return (d * _wide(c, n) + pltpu.roll(d * _wide(sa, n), n - half, 1) + pltpu.roll(d * _wide(sb, n), half, 1),)

    return ew(fn, [dy] + list(tables), [BF16], name=name)[0]


def _split3(x):
    h1 = x.astype(BF16)
    r1 = x - h1.astype(F32)
    h2 = r1.astype(BF16)
    return h1, h2, (r1 - h2.astype(F32)).astype(BF16)


def _tri(n, cmp):
    r = lax.broadcasted_iota(jnp.int32, (n, n), 0)
    c = lax.broadcasted_iota(jnp.int32, (n, n), 1)
    return cmp(r, c).astype(BF16)


def fox_gate_fwd(fl, b, *, name):
    S, W = fl.shape
    tr = min(ROW_TILE, S)

    def body(fl_ref, b_ref, cum_ref, carry):
        i = pl.program_id(0)

        @pl.when(i == 0)
        def _():
            carry[...] = jnp.zeros_like(carry)

        lower = _tri(tr, lambda r, c: r >= c)
        cs = carry[...]
        for piece in _split3(_log_sigmoid(fl_ref[...] + b_ref[...])):
            cs = cs + _dot(lower, piece)
        cum_ref[...] = cs
        carry[...] = cs[tr - 1:tr, :]

    return pl.pallas_call(
        body, name=name, grid=(S // tr,),
        in_specs=[pl.BlockSpec((tr, W), lambda i: (i, 0)), pl.BlockSpec((1, W), lambda i: (0, 0))],
        out_specs=pl.BlockSpec((tr, W), lambda i: (i, 0)),
        out_shape=jax.ShapeDtypeStruct((S, W), F32),
        scratch_shapes=[pltpu.VMEM((1, W), F32)],
        compiler_params=_cparams(("arbitrary",)),
    )(fl, b)


def fox_gate_bwd(dcum, fl, b, *, name):
    S, W = fl.shape
    tr = min(ROW_TILE, S)
    nb = S // tr

    def body(dc_ref, fl_ref, b_ref, dfl_ref, db_ref, carry):
        i = pl.program_id(0)

        @pl.when(i == 0)
        def _():
            carry[...] = jnp.zeros_like(carry)

        upper = _tri(tr, lambda r, c: r <= c)
        cs = carry[...]
        for piece in _split3(dc_ref[...]):
            cs = cs + _dot(upper, piece)
        carry[...] = cs[0:1, :]
        dfl = cs * _sigmoid(-(fl_ref[...] + b_ref[...]))
        dfl_ref[...] = dfl
        db = jnp.sum(dfl, axis=0, keepdims=True)

        @pl.when(i == 0)
        def _():
            db_ref[...] = db

        @pl.when(i > 0)
        def _():
            db_ref[...] += db

    rev = pl.BlockSpec((tr, W), lambda i: (nb - 1 - i, 0))
    one = pl.BlockSpec((1, W), lambda i: (0, 0))
    return pl.pallas_call(
        body, name=name, grid=(nb,),
        in_specs=[rev, rev, one], out_specs=[rev, one],
        out_shape=[jax.ShapeDtypeStruct((S, W), F32), jax.ShapeDtypeStruct((1, W), F32)],
        scratch_shapes=[pltpu.VMEM((1, W), F32)],
        compiler_params=_cparams(("arbitrary",)),
    )(dcum, fl, b)


def _blk_iota(tq, tk):
    return (lax.broadcasted_iota(jnp.int32, (tq, tk), 0), lax.broadcasted_iota(jnp.int32, (tq, tk), 1))


def _cs(xb, tri):
    return _dot(xb, tri)


def _rowsum(xb):
    return jnp.sum(xb.astype(F32), axis=1, keepdims=True)


def _sb_block(qs, k, cmr, shift):
    z = _dot_nt(qs, k)
    lb = jnp.minimum(z, 0.0) - jnp.log(1.0 + jnp.exp(-jnp.abs(z)))
    if cmr is None:
        return lb, (lb - z).astype(BF16), None
    strict = cmr < shift
    lom = jnp.where(strict, lb - z, 0.0).astype(BF16)
    return lb, lom, strict


def _keep(mask, x):
    return x if mask is None else jnp.where(mask, x, 0.0)


def _att_tiles(S):
    return min(ATT_BQ, S), min(ATT_BK, S)


PAIR = 2 * HEAD_DIM
N_PAIRS = N_HEADS // 2


def _pair_specs(S, tq):
    cols = D_MODEL // PAIR
    qspec = pl.BlockSpec((tq, PAIR), lambda p, i: (i, p))
    kspec = pl.BlockSpec((S, PAIR), lambda p, i: (0, cols + p))
    vspec = pl.BlockSpec((S, PAIR), lambda p, i: (0, 2 * cols + p))
    kvout = pl.BlockSpec((S, PAIR), lambda p, i: (0, p))
    vec = pl.BlockSpec((2, tq, 1), lambda p, i: (p, i, 0))
    return qspec, kspec, vspec, kvout, vec


def _head_lanes(h):
    lane = lax.broadcasted_iota(jnp.int32, (1, PAIR), 1)
    return (lane >= h * HEAD_DIM) & (lane < (h + 1) * HEAD_DIM)


def _only(sel, x):
    return jnp.where(sel, x, jnp.zeros_like(x))


def sb_fwd(proj, *, name):
    S = proj.shape[0]
    tq, tk = _att_tiles(S)
    qspec, kspec, vspec, _, vec = _pair_specs(S, tq)

    def body(q_ref, k_ref, v_ref, o_ref, t_ref):
        i = pl.program_id(1)
        row, col = _blk_iota(tq, tk)
        cmr = col - row
        below = _tri(tk, lambda r, c: r > c)
        nkb = (i + 1) * (tq // tk)
        out = []
        for h in range(2):
            sel = _head_lanes(h)
            qs = _only(sel, q_ref[...] * ATTN_SCALE)

            def step(n, carry, masked):
                r_sum, acc = carry
                kb = nkb - 1 - n
                ks = pl.multiple_of(kb * tk, tk)
                lb, lom, strict = _sb_block(qs, k_ref[pl.ds(ks, tk), :], cmr if masked else None, i * tq - kb * tk)
                w = _keep(strict, jnp.exp(lb + _cs(lom, below) + r_sum))
                acc = acc + _dot(w.astype(BF16), _only(sel, v_ref[pl.ds(ks, tk), :]))
                return r_sum + _rowsum(lom), acc

            nd = tq // tk
            carry = lax.fori_loop(0, nd, lambda n, c: step(n, c, True), (jnp.zeros((tq, 1), F32), jnp.zeros((tq, PAIR), F32)))
            r_sum, acc = lax.fori_loop(nd, nkb, lambda n, c: step(n, c, False), carry)
            t_ref[h] = r_sum
            out.append(acc)
        o_ref[...] = (out[0] + out[1]).astype(o_ref.dtype)

    return pl.pallas_call(
        body, name=name, grid=(N_PAIRS, S // tq),
        in_specs=[qspec, kspec, vspec], out_specs=[qspec, vec],
        out_shape=[jax.ShapeDtypeStruct((S, D_MODEL), BF16), jax.ShapeDtypeStruct((N_HEADS, S, 1), F32)],
        compiler_params=_cparams(("parallel", "arbitrary")),
    )(proj, proj, proj)


def sb_bwd(proj, tot, do, *, name):
    S = proj.shape[0]
    tq, tk = _att_tiles(S)
    qspec, kspec, vspec, kvout, vec = _pair_specs(S, tq)

    def body(q_ref, k_ref, v_ref, t_ref, do_ref, dq_ref, dk_out, dv_out, dk_ref, dv_ref):
        i = pl.program_id(1)

        @pl.when(i == 0)
        def _():
            dk_ref[...] = jnp.zeros_like(dk_ref)
            dv_ref[...] = jnp.zeros_like(dv_ref)

        row, col = _blk_iota(tq, tk)
        cmr = col - row
        upto = _tri(tk, lambda r, c: r <= c)
        before = _tri(tk, lambda r, c: r < c)
        out = []
        for h in range(2):
            sel = _head_lanes(h)
            qs, dov, t_all = _only(sel, q_ref[...] * ATTN_SCALE), _only(sel, do_ref[...]), t_ref[h]

            def step(kb, carry, masked):
                p_sum, e_sum, dq = carry
                ks = pl.multiple_of(kb * tk, tk)
                kv = k_ref[pl.ds(ks, tk), :]
                lb, lom, strict = _sb_block(qs, kv, cmr if masked else None, i * tq - kb * tk)
                tail = t_all - p_sum - _cs(lom, upto)
                w = _keep(strict, jnp.exp(lb + tail))
                e = _dot_nt(dov, v_ref[pl.ds(ks, tk), :]) * w
                eb = e.astype(BF16)
                e_before = e_sum + _cs(eb, before)
                beta = jnp.exp(lb)
                dzb = _keep(strict, e - (e + e_before) * beta).astype(BF16)
                dk_ref[pl.ds(ks, tk), :] += _dot_tn(dzb, qs)
                dv_ref[pl.ds(ks, tk), :] += _dot_tn(w.astype(BF16), dov)
                return p_sum + _rowsum(lom), e_sum + _rowsum(eb), dq + _dot(dzb, _only(sel, kv))

            zero = jnp.zeros((tq, 1), F32)
            nlow = i * (tq // tk)
            carry = lax.fori_loop(0, nlow, lambda kb, c: step(kb, c, False), (zero, zero, jnp.zeros((tq, PAIR), F32)))
            out.append(lax.fori_loop(nlow, nlow + tq // tk, lambda kb, c: step(kb, c, True), carry)[2])
        dq_ref[...] = ((out[0] + out[1]) * ATTN_SCALE).astype(dq_ref.dtype)

        @pl.when(i == S // tq - 1)
        def _():
            dk_out[...] = dk_ref[...].astype(dk_out.dtype)
            dv_out[...] = dv_ref[...].astype(dv_out.dtype)

    full = jax.ShapeDtypeStruct((S, D_MODEL), BF16)
    return pl.pallas_call(
        body, name=name, grid=(N_PAIRS, S // tq),
        in_specs=[qspec, kspec, vspec, vec, qspec], out_specs=[qspec, kvout, kvout],
        out_shape=[full, full, full],
        scratch_shapes=[pltpu.VMEM((S, PAIR), F32)] * 2,
        compiler_params=_cparams(("parallel", "arbitrary")),
    )(proj, proj, proj, tot, do)


def _fox_logits(qs, k, cq, ck, cmr, shift):
    s = _dot_nt(qs, k) + cq - ck
    if cmr is None:
        return s, None
    causal = cmr <= shift
    return jnp.where(causal, s, NEG_INF), causal


def fox_fwd(proj, cq, ck, *, name):
    S = proj.shape[0]
    tq, tk = _att_tiles(S)
    qspec, kspec, vspec, _, vec = _pair_specs(S, tq)
    ckspec = pl.BlockSpec((2, S // tk, 1, tk), lambda p, i: (p, 0, 0, 0))

    def body(q_ref, k_ref, v_ref, cq_ref, ck_ref, o_ref, lse_ref):
        i = pl.program_id(1)
        row, col = _blk_iota(tq, tk)
        cmr = col - row
        out = []
        for h in range(2):
            sel = _head_lanes(h)
            qs, cqv = _only(sel, q_ref[...] * ATTN_SCALE), cq_ref[h]

            def step(kb, carry, masked):
                m, l, acc = carry
                ks = pl.multiple_of(kb * tk, tk)
                s, _ = _fox_logits(qs, k_ref[pl.ds(ks, tk), :], cqv, ck_ref[h, kb], cmr if masked else None, i * tq - kb * tk)
                m_new = jnp.maximum(m, jnp.max(s, axis=1, keepdims=True))
                alpha = jnp.exp(m - m_new)
                p = jnp.exp(s - m_new)
                l = alpha * l + jnp.sum(p, axis=1, keepdims=True)
                acc = alpha * acc + _dot(p.astype(BF16), _only(sel, v_ref[pl.ds(ks, tk), :]))
                return m_new, l, acc

            nlow = i * (tq // tk)
            carry = lax.fori_loop(0, nlow, lambda kb, c: step(kb, c, False),
                                  (jnp.full((tq, 1), NEG_INF, F32), jnp.zeros((tq, 1), F32), jnp.zeros((tq, PAIR), F32)))
            m, l, acc = lax.fori_loop(nlow, nlow + tq // tk, lambda kb, c: step(kb, c, True), carry)
            lse_ref[h] = m + jnp.log(l)
            out.append(acc / l)
        o_ref[...] = (out[0] + out[1]).astype(o_ref.dtype)

    return pl.pallas_call(
        body, name=name, grid=(N_PAIRS, S // tq),
        in_specs=[qspec, kspec, vspec, vec, ckspec], out_specs=[qspec, vec],
        out_shape=[jax.ShapeDtypeStruct((S, D_MODEL), BF16), jax.ShapeDtypeStruct((N_HEADS, S, 1), F32)],
        compiler_params=_cparams(("parallel", "arbitrary")),
    )(proj, proj, proj, cq, ck)


def fox_bwd(proj, o, lse, cq, ck, do, *, name):
    S = proj.shape[0]
    tq, tk = _att_tiles(S)
    qspec, kspec, vspec, kvout, vec = _pair_specs(S, tq)
    ckspec = pl.BlockSpec((2, S // tk, 1, tk), lambda p, i: (p, 0, 0, 0))

    def body(q_ref, k_ref, v_ref, o_ref, lse_ref, cq_ref, ck_ref, do_ref, dq_ref, dk_out, dv_out, dcq_ref, dck_ref,
             dk_ref, dv_ref):
        i = pl.program_id(1)

        @pl.when(i == 0)
        def _():
            dk_ref[...] = jnp.zeros_like(dk_ref)
            dv_ref[...] = jnp.zeros_like(dv_ref)
            dck_ref[...] = jnp.zeros_like(dck_ref)

        row, col = _blk_iota(tq, tk)
        cmr = col - row
        out = []
        for h in range(2):
            sel = _head_lanes(h)
            qs, dov, cqv, lsev = _only(sel, q_ref[...] * ATTN_SCALE), _only(sel, do_ref[...]), cq_ref[h], lse_ref[h]
            delta = jnp.sum(dov.astype(F32) * o_ref[...].astype(F32), axis=1, keepdims=True)

            def step(kb, carry, masked):
                dq, dcq = carry
                ks = pl.multiple_of(kb * tk, tk)
                kv = k_ref[pl.ds(ks, tk), :]
                s, causal = _fox_logits(qs, kv, cqv, ck_ref[h, kb], cmr if masked else None, i * tq - kb * tk)
                p = _keep(causal, jnp.exp(s - lsev))
                ds = p * (_dot_nt(dov, v_ref[pl.ds(ks, tk), :]) - delta)
                dck_ref[h, kb] += jnp.sum(ds, axis=0, keepdims=True)
                dsb = ds.astype(BF16)
                dk_ref[pl.ds(ks, tk), :] += _dot_tn(dsb, qs)
                dv_ref[pl.ds(ks, tk), :] += _dot_tn(p.astype(BF16), dov)
                return dq + _dot(dsb, _only(sel, kv)), dcq + jnp.sum(ds, axis=1, keepdims=True)

            nlow = i * (tq // tk)
            carry = lax.fori_loop(0, nlow, lambda kb, c: step(kb, c, False),
                                  (jnp.zeros((tq, PAIR), F32), jnp.zeros((tq, 1), F32)))
            dq, dcq = lax.fori_loop(nlow, nlow + tq // tk, lambda kb, c: step(kb, c, True), carry)
            dcq_ref[h] = dcq
            out.append(dq)
        dq_ref[...] = ((out[0] + out[1]) * ATTN_SCALE).astype(dq_ref.dtype)

        @pl.when(i == S // tq - 1)
        def _():
            dk_out[...] = dk_ref[...].astype(dk_out.dtype)
            dv_out[...] = dv_ref[...].astype(dv_out.dtype)

    full = jax.ShapeDtypeStruct((S, D_MODEL), BF16)
    return pl.pallas_call(
        body, name=name, grid=(N_PAIRS, S // tq),
        in_specs=[qspec, kspec, vspec, qspec, vec, vec, ckspec, qspec],
        out_specs=[qspec, kvout, kvout, vec, ckspec],
        out_shape=[full, full, full, jax.ShapeDtypeStruct((N_HEADS, S, 1), F32),
                   jax.ShapeDtypeStruct((N_HEADS, S // tk, 1, tk), F32)],
        scratch_shapes=[pltpu.VMEM((S, PAIR), F32)] * 2,
        compiler_params=_cparams(("parallel", "arbitrary")),
    )(proj, proj, proj, o, lse, cq, ck, do)


def _swa_specs(S, tq):
    qspec = pl.BlockSpec((None, SWA_GROUP, tq, HEAD_DIM), lambda g, i: (g, 0, i, 0))
    kvspec = pl.BlockSpec((None, S + SWA_WINDOW, HEAD_DIM), lambda g, i: (g, 0, 0))
    vec = pl.BlockSpec((None, SWA_GROUP, tq, 1), lambda g, i: (g, 0, i, 0))
    sink = pl.BlockSpec((None, SWA_GROUP * tq, 1), lambda g, i: (g, 0, 0))
    return qspec, kvspec, vec, sink


def _swa_logits(q2, kw, i, tq):
    rows = q2.shape[0]
    r = lax.broadcasted_iota(jnp.int32, (rows, 2 * tq), 0)
    c = lax.broadcasted_iota(jnp.int32, (rows, 2 * tq), 1)
    diff = (r & (tq - 1)) + tq - c
    ok = (diff >= 0) & (diff < SWA_WINDOW) & (c + (i - 1) * tq >= 0)
    return jnp.where(ok, _dot_nt(q2, kw) * ATTN_SCALE, NEG_INF), ok


def swa_fwd(q, kp, vp, sink, *, name):
    _, G, S, _ = q.shape
    tq = ATT_BLK
    qspec, kvspec, vec, sinkspec = _swa_specs(S, tq)

    def body(q_ref, k_ref, v_ref, s_ref, o_ref, lse_ref):
        i = pl.program_id(1)
        q2 = q_ref[...].reshape(G * tq, HEAD_DIM)
        ws = pl.multiple_of(i * tq, tq)
        logits, _ = _swa_logits(q2, k_ref[pl.ds(ws, 2 * tq), :], i, tq)
        sk = s_ref[...]
        m = jnp.maximum(jnp.max(logits, axis=1, keepdims=True), sk)
        e = jnp.exp(logits - m)
        den = jnp.sum(e, axis=1, keepdims=True) + jnp.exp(sk - m)
        o = _dot((e / den).astype(BF16), v_ref[pl.ds(ws, 2 * tq), :])
        o_ref[...] = o.reshape(G, tq, HEAD_DIM).astype(o_ref.dtype)
        lse_ref[...] = (m + jnp.log(den)).reshape(G, tq, 1)

    return pl.pallas_call(
        body, name=name, grid=(SWA_KV_HEADS, S // tq),
        in_specs=[qspec, kvspec, kvspec, sinkspec], out_specs=[qspec, vec],
        out_shape=[jax.ShapeDtypeStruct(q.shape, BF16), jax.ShapeDtypeStruct((SWA_KV_HEADS, G, S, 1), F32)],
        compiler_params=_cparams(("parallel", "arbitrary")),
    )(q, kp, vp, sink)


def swa_bwd(q, kp, vp, sink, o, lse, do, *, name):
    _, G, S, _ = q.shape
    tq = ATT_BLK
    qspec, kvspec, vec, sinkspec = _swa_specs(S, tq)

    def body(q_ref, k_ref, v_ref, s_ref, o_ref, lse_ref, do_ref, dq_ref, dk_ref, dv_ref, dsink_ref):
        i = pl.program_id(1)

        @pl.when(i == 0)
        def _():
            dk_ref[...] = jnp.zeros_like(dk_ref)
            dv_ref[...] = jnp.zeros_like(dv_ref)

        q2 = q_ref[...].reshape(G * tq, HEAD_DIM)
        do2 = do_ref[...].reshape(G * tq, HEAD_DIM)
        o2 = o_ref[...].reshape(G * tq, HEAD_DIM)
        lse2 = lse_ref[...].reshape(G * tq, 1)
        ws = pl.multiple_of(i * tq, tq)
        kw = k_ref[pl.ds(ws, 2 * tq), :]
        vw = v_ref[pl.ds(ws, 2 * tq), :]
        logits, ok = _swa_logits(q2, kw, i, tq)
        p = jnp.where(ok, jnp.exp(logits - lse2), 0.0)
        delta = jnp.sum(do2.astype(F32) * o2.astype(F32), axis=1, keepdims=True)
        ds = p * (_dot_nt(do2, vw) - delta)
        dsb = ds.astype(BF16)
        dq_ref[...] = (_dot(dsb, kw) * ATTN_SCALE).reshape(G, tq, HEAD_DIM)
        dk_ref[pl.ds(ws, 2 * tq), :] += _dot_tn(dsb, q2) * ATTN_SCALE
        dv_ref[pl.ds(ws, 2 * tq), :] += _dot_tn(p.astype(BF16), do2)
        dsink_ref[...] = (-jnp.exp(s_ref[...] - lse2) * delta).reshape(G, tq, 1)

    kvshape = jax.ShapeDtypeStruct(kp.shape, F32)
    return pl.pallas_call(
        body, name=name, grid=(SWA_KV_HEADS, S // tq),
        in_specs=[qspec, kvspec, kvspec, sinkspec, qspec, vec, qspec],
        out_specs=[qspec, kvspec, kvspec, vec],
        out_shape=[jax.ShapeDtypeStruct(q.shape, F32), kvshape, kvshape,
                   jax.ShapeDtypeStruct((SWA_KV_HEADS, G, S, 1), F32)],
        compiler_params=_cparams(("parallel", "arbitrary")),
    )(q, kp, vp, sink, o, lse, do)


def _adamw_tile(w, g, m, v):
    m = ADAM_B1 * m + (1.0 - ADAM_B1) * g
    v = ADAM_B2 * v + (1.0 - ADAM_B2) * (g * g)
    m_hat = m / (1.0 - ADAM_B1 ** ADAM_STEP)
    v_hat = v / (1.0 - ADAM_B2 ** ADAM_STEP)
    delta = -ADAM_LR * (m_hat / (jnp.sqrt(v_hat) + ADAM_EPS) + ADAM_WD * w)
    return g, delta, m, v


def adamw(gfull, t, w, m, v, after, *, name):
    off, K, ns, _ = t
    sb = off // K
    nat = pl.BlockSpec((K, SLAB), lambda q: (0, q))

    def body(g_ref, w_ref, m_ref, v_ref, after_ref, *outs):
        del after_ref
        for o, r in zip(outs, _adamw_tile(w_ref[...], g_ref[...], m_ref[...], v_ref[...])):
            o[...] = r

    return pl.pallas_call(
        body, name=name, grid=(ns,),
        in_specs=[pl.BlockSpec((K, SLAB), lambda q: (sb + q, 0)), nat, nat, nat, HBM],
        out_specs=[nat] * 4, out_shape=[jax.ShapeDtypeStruct(w.shape, F32)] * 4,
        compiler_params=_cparams(("parallel",)),
    )(gfull, w, m, v, after)


def adamw_small(g, w, m, v, *, name):
    rows, d = w.shape

    def body(g_ref, w_ref, m_ref, v_ref, *outs):
        for j in range(rows):
            one = pl.ds(j, 1)
            for k, r in enumerate(_adamw_tile(w_ref[one, :], g_ref[one, :], m_ref[one, :], v_ref[one, :])):
                outs[k * rows + j][...] = r

    flat = pl.pallas_call(body, name=name, out_shape=[jax.ShapeDtypeStruct((1, d), F32)] * (4 * rows))(g, w, m, v)
    return [flat[k * rows:(k + 1) * rows] for k in range(4)]


MESH = pl.DeviceIdType.MESH
HBM = pl.BlockSpec(memory_space=pl.ANY)


def _place():
    x, y, c = lax.axis_index("x"), lax.axis_index("y"), lax.axis_index("c")
    others = [(1 - x, y), (x, 1 - y), (1 - x, 1 - y)]
    return x, y, c, others


def _rcopy(src, dst, send_sems, recv_sems, k, to):
    return pltpu.make_async_remote_copy(src_ref=src, dst_ref=dst, send_sem=send_sems.at[k], recv_sem=recv_sems.at[k],
                                        device_id=to, device_id_type=MESH)


def _dma_sems(*counts):
    return [pltpu.SemaphoreType.DMA((n,)) for n in counts]


DMA_UNIT_ROWS = 128
DMA_PIECES = 4


def _row_pieces(h, n):
    units = h // DMA_UNIT_ROWS
    n = min(n, units)
    base, extra = divmod(units, n)
    sizes = [(base + (k < extra)) * DMA_UNIT_ROWS for k in range(n)]
    return [(sum(sizes[:k]), sizes[k]) for k in range(n)]


def _start_pieces(make, h, n):
    for s0, sz in _row_pieces(h, n):
        make(s0, sz).start()
    return make(0, h)


SEM = pl.BlockSpec(memory_space=pltpu.SEMAPHORE)
SPLIT_COPY = pltpu.CompilerParams(has_side_effects=pltpu.SideEffectType.DATAFLOW_SIDE_EFFECTING)
N_OTHERS = 3


def _hbm(a):
    return pltpu.with_memory_space_constraint(a, pltpu.HBM)


def _chip_rows(buf, chip, s0, sz):
    return buf.at[2 * chip[0] + chip[1], pl.ds(s0, sz)]


def allgather_start(bufs, *, name):
    n = len(bufs)

    def body(*refs):
        ins, send, recv, token = refs[:n], refs[n:2 * n], refs[2 * n:3 * n], refs[4 * n]
        x, y, c, others = _place()
        for i in range(n):
            h = bufs[i].shape[1] // 2
            for f, chip in enumerate(others):
                for s0, sz in _row_pieces(h, DMA_PIECES):
                    mine = _chip_rows(ins[i], (x, y), c * h + s0, sz)
                    _rcopy(mine, mine, send[i], recv[i], f, (*chip, c)).start()
        token[...] = jnp.zeros_like(token)

    res = pl.pallas_call(
        body, name=name, in_specs=[HBM] * n,
        out_specs=[SEM] * (2 * n) + [HBM] * n + [pl.BlockSpec(memory_space=pltpu.VMEM)],
        out_shape=[pltpu.SemaphoreType.DMA((N_OTHERS,))] * (2 * n) + [pltpu.HBM(b.shape, b.dtype) for b in bufs]
        + [jax.ShapeDtypeStruct((1, D_MODEL), F32)],
        input_output_aliases={i: 2 * n + i for i in range(n)},
        compiler_params=SPLIT_COPY,
    )(*[_hbm(b) for b in bufs])
    return res[:n], res[n:2 * n], res[2 * n:3 * n], res[3 * n]


def allgather_wait(buf, send, recv, after, *, name):
    h = buf.shape[1] // 2
    after = list(after)

    def body(buf_ref, send_sems, recv_sems, *rest):
        del rest
        x, y, c, others = _place()
        for f, chip in enumerate(others):
            mine = _chip_rows(buf_ref, (x, y), c * h, h)
            theirs = _chip_rows(buf_ref, chip, c * h, h)
            cp = _rcopy(mine, theirs, send_sems, recv_sems, f, (*chip, c))
            cp.wait_send()
            cp.wait_recv()

    return pl.pallas_call(
        body, name=name, in_specs=[HBM, SEM, SEM] + [HBM] * len(after), out_specs=HBM,
        out_shape=pltpu.HBM(buf.shape, buf.dtype), input_output_aliases={0: 0},
        compiler_params=SPLIT_COPY,
    )(buf, send, recv, *after)


def allgather_forward(buf, *, name):
    h = buf.shape[1] // 2

    def body(in_ref, out_ref, send_sems, recv_sems):
        del in_ref
        x, y, c, others = _place()
        sibling = (x, y, 1 - c)
        sent = []
        for f, chip in enumerate(others):
            sent.append(_start_pieces(
                lambda s0, sz: _rcopy(_chip_rows(out_ref, chip, c * h + s0, sz), _chip_rows(out_ref, chip, c * h + s0, sz),
                                      send_sems, recv_sems, f, sibling), h, DMA_PIECES))
        for f, chip in enumerate(others):
            blk = _chip_rows(out_ref, chip, (1 - c) * h, h)
            _rcopy(blk, blk, send_sems, recv_sems, f, sibling).wait_recv()
        for cp in sent:
            cp.wait_send()

    return pl.pallas_call(
        body, name=name, in_specs=[HBM], out_specs=HBM,
        out_shape=jax.ShapeDtypeStruct(buf.shape, buf.dtype), input_output_aliases={0: 0},
        scratch_shapes=_dma_sems(N_OTHERS, N_OTHERS),
    )(buf)


def _sem1():
    return pltpu.SemaphoreType.DMA((1,))


TOKEN = jax.ShapeDtypeStruct((1, D_MODEL), F32)
VMEM_SPEC = pl.BlockSpec(memory_space=pltpu.VMEM)


def swap_start(grad, *, name):
    h = grad.shape[1] // 2

    def body(g_ref, land_ref, send, recv, g_out, land_out, token):
        del g_out, land_out
        x, y, c, _ = _place()
        for k in range(N_CHIPS):
            for s0, sz in _row_pieces(h, DMA_PIECES):
                _rcopy(g_ref.at[k, pl.ds((1 - c) * h + s0, sz)], land_ref.at[k, pl.ds(s0, sz)], send, recv, 0, (x, y, 1 - c)).start()
        token[...] = jnp.zeros_like(token)

    land = lax.empty((N_CHIPS, h, SLAB), grad.dtype)
    return pl.pallas_call(
        body, name=name, in_specs=[HBM, HBM], out_specs=[SEM, SEM, HBM, HBM, VMEM_SPEC],
        out_shape=[_sem1(), _sem1(), pltpu.HBM(grad.shape, grad.dtype), pltpu.HBM(land.shape, land.dtype), TOKEN],
        input_output_aliases={0: 2, 1: 3}, compiler_params=SPLIT_COPY,
    )(_hbm(grad), _hbm(land))


def swap_wait(grad, land, send, recv, after, *, name):
    h = land.shape[1]

    def body(g_ref, land_ref, send_sems, recv_sems, after_ref, g_out, land_out):
        del after_ref, g_out, land_out
        x, y, c, _ = _place()
        cp = _rcopy(g_ref.at[:, pl.ds((1 - c) * h, h)], land_ref, send_sems, recv_sems, 0, (x, y, 1 - c))
        cp.wait_send()
        cp.wait_recv()

    return pl.pallas_call(
        body, name=name, in_specs=[HBM, HBM, SEM, SEM, HBM], out_specs=[HBM, HBM],
        out_shape=[pltpu.HBM(grad.shape, grad.dtype), pltpu.HBM(land.shape, land.dtype)],
        input_output_aliases={0: 0, 1: 1}, compiler_params=SPLIT_COPY,
    )(grad, land, send, recv, after)


def scatter_start(part, *, name):
    h = part.shape[1]

    def body(part_ref, land_ref, send, recv, part_out, land_out, token):
        del part_out, land_out
        x, y, c, others = _place()
        for f, chip in enumerate(others):
            for s0, sz in _row_pieces(h, DMA_PIECES):
                _rcopy(_chip_rows(part_ref, chip, s0, sz), land_ref.at[f, pl.ds(s0, sz)], send, recv, f, (*chip, c)).start()
        token[...] = jnp.zeros_like(token)

    land = lax.empty((N_OTHERS,) + part.shape[1:], part.dtype)
    return pl.pallas_call(
        body, name=name, in_specs=[HBM, HBM],
        out_specs=[SEM, SEM, HBM, HBM, pl.BlockSpec(memory_space=pltpu.VMEM)],
        out_shape=[pltpu.SemaphoreType.DMA((N_OTHERS,))] * 2 + [pltpu.HBM(part.shape, part.dtype), pltpu.HBM(land.shape, land.dtype),
                                                                 jax.ShapeDtypeStruct((1, D_MODEL), F32)],
        input_output_aliases={0: 2, 1: 3},
        compiler_params=SPLIT_COPY,
    )(_hbm(part), _hbm(land))


def scatter_wait(part, land, send, recv, after, *, name):
    h = part.shape[1]

    def body(part_ref, land_ref, send_sems, recv_sems, after_ref, part_out, land_out):
        del after_ref, part_out, land_out
        x, y, c, others = _place()
        for f, chip in enumerate(others):
            cp = _rcopy(_chip_rows(part_ref, chip, 0, h), land_ref.at[f], send_sems, recv_sems, f, (*chip, c))
            cp.wait_send()
            cp.wait_recv()

    return pl.pallas_call(
        body, name=name, in_specs=[HBM, HBM, SEM, SEM, HBM], out_specs=[HBM, HBM],
        out_shape=[pltpu.HBM(part.shape, part.dtype), pltpu.HBM(land.shape, land.dtype)],
        input_output_aliases={0: 0, 1: 1},
        compiler_params=SPLIT_COPY,
    )(part, land, send, recv, after)


def join_start(buf, *, name):
    h = buf.shape[0] // 2

    def body(b_ref, send, recv, b_out, token):
        del b_out
        x, y, c, _ = _place()
        for s0, sz in _row_pieces(h, 2 * DMA_PIECES):
            rows = b_ref.at[pl.ds(c * h + s0, sz)]
            _rcopy(rows, rows, send, recv, 0, (x, y, 1 - c)).start()
        token[...] = jnp.zeros_like(token)

    return pl.pallas_call(
        body, name=name, in_specs=[HBM], out_specs=[SEM, SEM, HBM, VMEM_SPEC],
        out_shape=[_sem1(), _sem1(), pltpu.HBM(buf.shape, buf.dtype), TOKEN],
        input_output_aliases={0: 2}, compiler_params=SPLIT_COPY,
    )(_hbm(buf))


def join_wait(buf, send, recv, after, *, name):
    h = buf.shape[0] // 2

    def body(b_ref, send_sems, recv_sems, after_ref, b_out):
        del after_ref, b_out
        x, y, c, _ = _place()
        cp = _rcopy(b_ref.at[pl.ds(c * h, h)], b_ref.at[pl.ds((1 - c) * h, h)], send_sems, recv_sems, 0, (x, y, 1 - c))
        cp.wait_send()
        cp.wait_recv()

    return pl.pallas_call(
        body, name=name, in_specs=[HBM, SEM, SEM, HBM], out_specs=HBM,
        out_shape=pltpu.HBM(buf.shape, buf.dtype), input_output_aliases={0: 0}, compiler_params=SPLIT_COPY,
    )(buf, send, recv, after)


def allreduce_small(v, *, name):
    rows, n = v.shape

    def body(x_ref, sum_ref, all_ref, send_sems, recv_sems, local_sem):
        x, y, c, others = _place()
        me, sibling = (x, y, c), (x, y, 1 - c)

        def blk(px, py, pc):
            return all_ref.at[pl.ds((4 * px + 2 * py + pc) * rows, rows), :]

        def copy(k, block, to, src=None):
            return _rcopy(blk(*block) if src is None else src, blk(*block), send_sems, recv_sems, k, to)

        mine = pltpu.make_async_copy(x_ref, blk(*me), local_sem)
        mine.start()
        first = [copy(0, me, sibling, src=x_ref)]
        first += [copy(1 + f, me, (*chip, c), src=x_ref) for f, chip in enumerate(others)]
        for cp in first:
            cp.start()
        passed = [copy(4 + f, (*chip, c), sibling) for f, chip in enumerate(others)]
        for f, chip in enumerate(others):
            copy(1 + f, (*chip, c), me).wait_recv()
            passed[f].start()
        copy(0, sibling, me).wait_recv()
        for f, chip in enumerate(others):
            copy(4 + f, (*chip, 1 - c), me).wait_recv()
        for cp in first + passed:
            cp.wait_send()
        mine.wait()
        acc = all_ref[pl.ds(0, rows), :]
        for d in range(1, N_DEVICES):
            acc = acc + all_ref[pl.ds(d * rows, rows), :]
        sum_ref[...] = acc

    vm = pl.BlockSpec(memory_space=pltpu.VMEM)
    return pl.pallas_call(
        body, name=name, in_specs=[vm], out_specs=[vm, vm],
        out_shape=[jax.ShapeDtypeStruct((rows, n), F32), jax.ShapeDtypeStruct((N_DEVICES * rows, n), F32)],
        scratch_shapes=_dma_sems(7, 7) + [pltpu.SemaphoreType.DMA],
    )(v)[0]


def add_pairs(grad, theirs, where, *, name):
    h = theirs.shape[1]
    spec = pl.BlockSpec((None, h, SLAB), lambda k, w: (k, 0, 0))

    def body(w_ref, a_ref, b_ref, o_ref):
        del w_ref
        o_ref[...] = (a_ref[...].astype(F32) + b_ref[...].astype(F32)).astype(o_ref.dtype)

    return pl.pallas_call(
        body, name=name,
        grid_spec=pltpu.PrefetchScalarGridSpec(
            num_scalar_prefetch=1, grid=(N_CHIPS,),
            in_specs=[pl.BlockSpec((None, h, SLAB), lambda k, w: (k, w[1], 0)), spec], out_specs=spec),
        out_shape=jax.ShapeDtypeStruct(theirs.shape, theirs.dtype),
        compiler_params=_cparams(("parallel",)))(where, grad, theirs)


def add_chips(pair, got, where, *, name):
    h = pair.shape[1]
    tr = h // 2

    def body(w_ref, a_ref, b_ref, o_ref):
        del w_ref
        acc = a_ref[...].astype(F32)
        for f in range(3):
            acc = acc + b_ref[f].astype(F32)
        o_ref[...] = acc

    return pl.pallas_call(
        body, name=name,
        grid_spec=pltpu.PrefetchScalarGridSpec(
            num_scalar_prefetch=1, grid=(2,),
            in_specs=[pl.BlockSpec((None, tr, SLAB), lambda i, w: (w[0], i, 0)),
                      pl.BlockSpec((3, tr, SLAB), lambda i, w: (0, i, 0))],
            out_specs=pl.BlockSpec((tr, SLAB), lambda i, w: (2 * w[1] + i, 0))),
        out_shape=jax.ShapeDtypeStruct((2 * h, SLAB), F32),
        compiler_params=_cparams(("parallel",)))(where, pair, got)


DEPTH = 4
MIXER = (0, 1, 2, 0)
W_IN_COLS = (768, 320, 772)
W_IN_PAD = (768, 512, 1024)
MATS = ("up", "down", "inp", "out", "gate", "proj")
MAT_ARG = dict(up="w_up", down="w_down", inp="w_in", out="w_out", gate="w_ple_gate", proj="w_ple_proj")
GAINS = ("attn_norm", "mlp_norm", "ple_norm")
N_SMALL = 16
KINDS = ("grad_", "delta_", "new_m_", "new_v_")


def _layout(kind):
    ns_in = W_IN_PAD[kind] // SLAB
    off = 8192 + 1024 * ns_in
    lay = dict(up=(0, 1024, 4, False), down=(4096, 1024, 4, True), inp=(8192, 1024, ns_in, False),
               out=(off, 256, 4, True), gate=(off + 1024, 256, 4, True), proj=(off + 2048, 256, 1, False))
    return lay, off + 2304


def _to_slabs(w):
    k, c = w.shape
    return w.reshape(k, c // SLAB, SLAB).transpose(1, 0, 2).reshape(-1, SLAB)


def _pad_cols(w, n):
    return jnp.pad(w, ((0, 0), (0, n - w.shape[1])))


def _heads(x2d, n):
    return x2d.reshape(x2d.shape[0], n, HEAD_DIM).transpose(1, 0, 2)


def _unheads(x3d):
    n, s, _ = x3d.shape
    return x3d.transpose(1, 0, 2).reshape(s, n * HEAD_DIM)


def _chip_cols(x2d, c, cpad):
    return jnp.concatenate([_pad_cols(x2d[:, j * c:(j + 1) * c], cpad) for j in range(N_CHIPS)], axis=1)


def _unchip_cols(x2d, c, cpad):
    return jnp.concatenate([x2d[:, j * cpad:j * cpad + c] for j in range(N_CHIPS)], axis=1)


def _forget_cols(wg, t):
    off, K, _, _ = t
    cols = []
    for g in range(3 * N_HEADS * HEAD_DIM, 3 * N_HEADS * HEAD_DIM + N_HEADS):
        chip, local = divmod(g, W_IN_COLS[2])
        q, c = divmod(local, SLAB)
        cols.append(wg[chip, off + q * K:off + (q + 1) * K, c:c + 1])
    return jnp.concatenate(cols, axis=1)


def _add_res(acc, res):
    return (acc + res,)


def _relu2(acc):
    return acc, jnp.square(jnp.maximum(acc, 0.0))


def _relu2_bwd(acc, u):
    return (acc * (2.0 * jnp.maximum(u.astype(F32), 0.0)),)


def _ple_fwd(acc, x2, pp):
    return x2 + pp * _sigmoid(acc), acc


def _ple_bwd(dx, pp, gl):
    gate = _sigmoid(gl)
    return dx * gate, dx * pp * gate * (1.0 - gate)


def _layer_fwd(i, kind, x0, p_bf, wg, lay, gains, extra, tabs):
    s = x0.shape[0]
    an, mn, pn = gains
    sv = dict(x0=x0)
    if kind == 0:
        proj, h1 = mm_nn(x0, wg, lay["inp"], name=f"w_in_{i}", norm_gain=an)
        a, tot = sb_fwd(proj, name=f"sb_fwd_{i}")
        sv.update(proj=proj, tot=tot)
    elif kind == 1:
        projp, h1 = mm_nn(x0, wg, lay["inp"], name=f"w_in_{i}", out_dtypes=(F32,), norm_gain=an)
        proj = _unchip_cols(projp, W_IN_COLS[1], W_IN_PAD[1])
        nq = N_HEADS * HEAD_DIM
        nqk = nq + SWA_KV_HEADS * HEAD_DIM
        qk = rope_fwd(proj[:, :nqk], tabs, name=f"rope_{i}")
        q = _heads(qk[:, :nq], N_HEADS).reshape(SWA_KV_HEADS, SWA_GROUP, s, HEAD_DIM)
        front = ((0, 0), (SWA_WINDOW, 0), (0, 0))
        kp = jnp.pad(_heads(qk[:, nq:], SWA_KV_HEADS), front)
        vp = jnp.pad(_heads(proj[:, nqk:].astype(BF16), SWA_KV_HEADS), front)
        sink = jnp.repeat(extra.reshape(SWA_KV_HEADS, SWA_GROUP), ATT_BLK, axis=1)[:, :, None]
        o4, lse = swa_fwd(q, kp, vp, sink, name=f"swa_fwd_{i}")
        a = _unheads(o4.reshape(N_HEADS, s, HEAD_DIM))
        sv.update(q=q, kp=kp, vp=vp, sink=sink, o4=o4, lse=lse)
    else:
        projp, h1 = mm_nn(x0, wg, lay["inp"], name=f"w_in_{i}", norm_gain=an)
        nqkv = 3 * N_HEADS * HEAD_DIM
        proj = _unchip_cols(projp, W_IN_COLS[2], W_IN_PAD[2])[:, :nqkv]
        fl = mm_plain(h1, _pad_cols(_forget_cols(wg, lay["inp"]), 128), name=f"w_forget_{i}")
        bp = _pad_cols(extra[None], 128)
        cum_t = fox_gate_fwd(fl, bp, name=f"gate_fwd_{i}")[:, :N_HEADS].T
        cq = cum_t[:, :, None]
        ck = cum_t.reshape(N_HEADS, s // min(ATT_BK, s), 1, min(ATT_BK, s))
        a, lse = fox_fwd(proj, cq, ck, name=f"fox_fwd_{i}")
        sv.update(proj=proj, fl=fl, bp=bp, cq=cq, ck=ck, lse=lse)
    x1 = mm_nn(a, wg, lay["out"], name=f"w_out_{i}", epi=_add_res, extras=(x0,), out_dtypes=(F32,))[0]
    u, r, h2 = mm_nn(x1, wg, lay["up"], name=f"w_up_{i}", epi=_relu2, out_dtypes=(BF16, BF16), norm_gain=mn)
    x2 = mm_nn(r, wg, lay["down"], name=f"w_down_{i}", epi=_add_res, extras=(x1,), out_dtypes=(F32,))[0]
    pp = mm_nn(p_bf, wg, lay["proj"], name=f"w_ple_proj_{i}", out_dtypes=(F32,))[0]
    x3, gl, h3 = mm_nn(x2, wg, lay["gate"], name=f"w_ple_gate_{i}", epi=_ple_fwd, extras=(x2, pp), out_dtypes=(F32, F32),
                       norm_gain=pn)
    sv.update(h1=h1, a=a, x1=x1, h2=h2, u=u, r=r, x2=x2, h3=h3, pp=pp, gl=gl)
    return x3, sv


def _layer_bwd(i, kind, dx3, sv, p_bf, wg, lay, n_rows, gains, tabs, mid, end):
    s = dx3.shape[0]
    an, mn, pn = gains
    g = lax.empty((N_CHIPS, n_rows, SLAB), BF16)
    d_pp, d_gl = ew(_ple_bwd, [dx3, sv["pp"], sv["gl"]], [BF16, BF16], name=f"ple_bwd_{i}")
    g = mm_tn(p_bf, d_pp, g, lay["proj"], name=f"dw_ple_proj_{i}")
    g = mm_tn(sv["h3"], d_gl, g, lay["gate"], name=f"dw_ple_gate_{i}")
    d_h3 = mm_nt(d_gl, wg, lay["gate"], name=f"dx_ple_gate_{i}", out_dtypes=(F32,))[0]
    dx2, dx2b, d_pn = rms_bwd(sv["x2"], pn, d_h3, dx3, name=f"ple_norm_bwd_{i}")
    zero = mid(dx2)
    if zero is not None:
        mn = mn + zero
    g = mm_tn(sv["r"], dx2b, g, lay["down"], name=f"dw_down_{i}")
    d_u = mm_nt(dx2b, wg, lay["down"], name=f"dx_down_{i}", epi=_relu2_bwd, extras=(sv["u"],))[0]
    g = mm_tn(sv["h2"], d_u, g, lay["up"], name=f"dw_up_{i}")
    dx1, dx1b, d_mn = mm_nt(d_u, wg, lay["up"], name=f"dx_up_{i}", rms=(sv["x1"], mn, dx2))
    g = mm_tn(sv["a"], dx1b, g, lay["out"], name=f"dw_out_{i}")
    d_a = mm_nt(dx1b, wg, lay["out"], name=f"dx_out_{i}")[0]
    d_extra = None
    if kind == 0:
        d_proj = jnp.concatenate(sb_bwd(sv["proj"], sv["tot"], d_a, name=f"sb_bwd_{i}"), axis=1)
    elif kind == 1:
        do4 = _heads(d_a, N_HEADS).reshape(SWA_KV_HEADS, SWA_GROUP, s, HEAD_DIM)
        dq, dkp, dvp, dsr = swa_bwd(sv["q"], sv["kp"], sv["vp"], sv["sink"], sv["o4"], sv["lse"], do4, name=f"swa_bwd_{i}")
        dqk = jnp.concatenate([_unheads(dq.reshape(N_HEADS, s, HEAD_DIM)), _unheads(dkp[:, SWA_WINDOW:])], axis=1)
        dqk = rope_bwd(dqk, tabs, name=f"rope_bwd_{i}")
        d_proj = jnp.concatenate([dqk, _unheads(dvp[:, SWA_WINDOW:]).astype(BF16)], axis=1)
        d_proj = _chip_cols(d_proj, W_IN_COLS[1], W_IN_PAD[1])
        d_extra = jnp.sum(dsr[..., 0], axis=2).reshape(N_HEADS)
    else:
        dq, dk, dv, dcq, dck = fox_bwd(sv["proj"], sv["a"], sv["lse"], sv["cq"], sv["ck"], d_a, name=f"fox_bwd_{i}")
        dcum = _pad_cols((dcq[:, :, 0] - dck.reshape(N_HEADS, s)).T, 128)
        dfl, dbp = fox_gate_bwd(dcum, sv["fl"], sv["bp"], name=f"gate_bwd_{i}")
        d_proj = jnp.concatenate([dq, dk, dv, dfl[:, :N_HEADS].astype(BF16)], axis=1)
        d_proj = _chip_cols(d_proj, W_IN_COLS[2], W_IN_PAD[2])
        d_extra = dbp[0, :N_HEADS]
    g = mm_tn(sv["h1"], d_proj, g, lay["inp"], name=f"dw_in_{i}")
    zero = end(g)
    if zero is not None:
        an = an + zero
    dx0, _, d_an = mm_nt(d_proj, wg, lay["inp"], name=f"dx_in_{i}", rms=(sv["x0"], an, dx1))
    return dx0, g, (d_an, d_mn, d_pn), d_extra


def _small_rows(a, prefix):
    rows = [a[f"{prefix}{n}_{i}"] for i in range(DEPTH) for n in GAINS] + [a[f"{prefix}final_norm"]]
    rows += [_pad_cols(a[f"{prefix}{n}"][None], D_MODEL)[0] for n in ("sinks_1", "b_forget_2")]
    return jnp.stack(rows + [jnp.zeros((D_MODEL,), F32)])


def _train_step(a):
    x = a["x"][0]
    tabs = rope_tables(x.shape[0])
    lays = [_layout(k) for k in MIXER]

    padded = {}

    def natural(prefix, i, m):
        w = a[f"{prefix}{MAT_ARG[m]}_{i}"]
        if m != "inp" or w.shape[1] == W_IN_PAD[MIXER[i]]:
            return w
        if (prefix, i) not in padded:
            padded[prefix, i] = _pad_cols(w, W_IN_PAD[MIXER[i]])
        return padded[prefix, i]

    chip = 2 * lax.axis_index("x") + lax.axis_index("y")
    where = jnp.stack([chip, lax.axis_index("c")]).astype(jnp.int32)
    def own_block(i, zero):
        pk = jnp.concatenate([_to_slabs((natural("", i, m) + zero).astype(BF16)) for m in MATS], axis=0)
        return lax.dynamic_update_slice(lax.empty((N_CHIPS,) + pk.shape, BF16), pk[None], (chip, 0, 0))

    sends, recvs, bufs, token = allgather_start([own_block(0, 0.0)], name="allgather_start_0")
    more = allgather_start([own_block(i, token[0, 0]) for i in range(1, DEPTH)], name="allgather_start_1")
    sends, recvs, bufs, token = sends + more[0], recvs + more[1], bufs + more[2], more[3]

    gains = [tuple(a[f"{n}_{i}"][None] for n in GAINS) for i in range(DEPTH)]
    extras = [None, a["sinks_1"], a["b_forget_2"], None]
    p_bf = [a["p"][i, 0].astype(BF16) for i in range(DEPTH)]

    small_state = [_small_rows(a, prefix) for prefix in ("", "m_", "v_")]
    after = [token, *tabs, *p_bf, *small_state] + [natural(prefix, i, "inp") for prefix in ("m_", "v_") for i in (1, 2)]
    saved, wgs = [], []
    for i in range(DEPTH):
        landed = allgather_wait(bufs[i], sends[i], recvs[i], after, name=f"allgather_wait_{i}")
        wgs.append(allgather_forward(landed, name=f"allgather_forward_{i}"))
        x, sv = _layer_fwd(i, MIXER[i], x, p_bf[i], wgs[i], lays[i][0], gains[i], extras[i], tabs)
        saved.append(sv)
        after = [x]
    dx, d_final, loss = loss_head(x, a["final_norm"][None], a["loss_target"][0], name="loss_head")

    def pair_and_scatter(j, swapped, after):
        send, recv, grad, land, _ = swapped
        grad, theirs = swap_wait(grad, land, send, recv, after, name=f"swap_wait_{j}")
        return scatter_start(add_pairs(grad, theirs, where, name=f"add_pairs_{j}"), name=f"scatter_start_{j}")

    def sum_and_join(j, scattered, after):
        send, recv, part, land, _ = scattered
        part, got = scatter_wait(part, land, send, recv, after, name=f"scatter_wait_{j}")
        return join_start(add_chips(part, got, where, name=f"add_chips_{j}"), name=f"join_start_{j}")

    small = [None] * N_SMALL
    small[12] = d_final[0]
    small[15] = _pad_cols(loss[:, :1], D_MODEL)[0]
    joined = [None] * DEPTH
    state = dict(swapped=None, scattered=None)
    for i in reversed(range(DEPTH)):
        an, mn, pn = gains[i]
        if state["swapped"] is not None:
            pn = pn + state["swapped"][4]

        def mid(dx2):
            if state["swapped"] is None:
                return None
            state["scattered"] = pair_and_scatter(i + 1, state["swapped"], dx2)
            return state["scattered"][4]

        def end(grad):
            state["swapped"] = swap_start(grad, name=f"swap_start_{i}")
            return state["swapped"][4]

        dx, _, d_gains, d_extra = _layer_bwd(i, MIXER[i], dx, saved[i], p_bf[i], wgs[i], lays[i][0], lays[i][1],
                                             (an, mn, pn), tabs, mid, end)
        for j in range(3):
            small[3 * i + j] = d_gains[j][0]
        if d_extra is not None:
            small[12 + MIXER[i]] = _pad_cols(d_extra[None], D_MODEL)[0]
        if state["scattered"] is not None:
            joined[i + 1] = sum_and_join(i + 1, state["scattered"], dx)
    started = pair_and_scatter(0, state["swapped"], dx)
    small = allreduce_small(jnp.stack(small), name="allreduce_small")

    out = {"loss": small[15, 0], "grad_x": dx[None]}
    res = adamw_small(small, *small_state, name="adamw_small")
    prev = started[4]
    for i in reversed(range(DEPTH)):
        if i == 0:
            joined[0] = sum_and_join(0, started, prev)
        send, recv, buf, zero = joined[i]
        gfull = join_wait(buf, send, recv, prev if i else zero, name=f"join_wait_{i}")
        for m in MATS:
            upd = adamw(gfull, lays[i][0][m], natural("", i, m), natural("m_", i, m), natural("v_", i, m), prev,
                        name=f"adamw_{MAT_ARG[m]}_{i}")
            prev = upd[1]
            cols = a[f"{MAT_ARG[m]}_{i}"].shape[1]
            for kd, r in zip(KINDS, upd):
                out[f"{kd}{MAT_ARG[m]}_{i}"] = r[:, :cols]
    for kd, r in zip(KINDS, res):
        for i in range(DEPTH):
            for j, n in enumerate(GAINS):
                out[f"{kd}{n}_{i}"] = r[3 * i + j][0]
        out[f"{kd}final_norm"] = r[12][0]
        out[f"{kd}sinks_1"] = r[13][0, :N_HEADS]
        out[f"{kd}b_forget_2"] = r[14][0, :N_HEADS]
    return out


def _weight_names():
    names = []
    for i in range(DEPTH):
        names += [f"attn_norm_{i}", f"w_in_{i}", f"w_out_{i}"] + [[], ["sinks_1"], ["b_forget_2"]][MIXER[i]]
        names += [f"mlp_norm_{i}", f"w_up_{i}", f"w_down_{i}", f"ple_norm_{i}", f"w_ple_gate_{i}", f"w_ple_proj_{i}"]
    return names + ["final_norm"]


def kernel(x, p, attn_norm_0, w_in_0, w_out_0, mlp_norm_0, w_up_0, w_down_0, ple_norm_0, w_ple_gate_0, w_ple_proj_0, attn_norm_1, w_in_1, w_out_1, sinks_1, mlp_norm_1, w_up_1, w_down_1, ple_norm_1, w_ple_gate_1, w_ple_proj_1, attn_norm_2, w_in_2, w_out_2, b_forget_2, mlp_norm_2, w_up_2, w_down_2, ple_norm_2, w_ple_gate_2, w_ple_proj_2, attn_norm_3, w_in_3, w_out_3, mlp_norm_3, w_up_3, w_down_3, ple_norm_3, w_ple_gate_3, w_ple_proj_3, final_norm, loss_target, m_attn_norm_0, m_w_in_0, m_w_out_0, m_mlp_norm_0, m_w_up_0, m_w_down_0, m_ple_norm_0, m_w_ple_gate_0, m_w_ple_proj_0, m_attn_norm_1, m_w_in_1, m_w_out_1, m_sinks_1, m_mlp_norm_1, m_w_up_1, m_w_down_1, m_ple_norm_1, m_w_ple_gate_1, m_w_ple_proj_1, m_attn_norm_2, m_w_in_2, m_w_out_2, m_b_forget_2, m_mlp_norm_2, m_w_up_2, m_w_down_2, m_ple_norm_2, m_w_ple_gate_2, m_w_ple_proj_2, m_attn_norm_3, m_w_in_3, m_w_out_3, m_mlp_norm_3, m_w_up_3, m_w_down_3, m_ple_norm_3, m_w_ple_gate_3, m_w_ple_proj_3, m_final_norm, v_attn_norm_0, v_w_in_0, v_w_out_0, v_mlp_norm_0, v_w_up_0, v_w_down_0, v_ple_norm_0, v_w_ple_gate_0, v_w_ple_proj_0, v_attn_norm_1, v_w_in_1, v_w_out_1, v_sinks_1, v_mlp_norm_1, v_w_up_1, v_w_down_1, v_ple_norm_1, v_w_ple_gate_1, v_w_ple_proj_1, v_attn_norm_2, v_w_in_2, v_w_out_2, v_b_forget_2, v_mlp_norm_2, v_w_up_2, v_w_down_2, v_ple_norm_2, v_w_ple_gate_2, v_w_ple_proj_2, v_attn_norm_3, v_w_in_3, v_w_out_3, v_mlp_norm_3, v_w_up_3, v_w_down_3, v_ple_norm_3, v_w_ple_gate_3, v_w_ple_proj_3, v_final_norm):
    out = _train_step(dict(locals()))
    return (out["loss"], out["grad_x"], *[out[kd + n] for kd in KINDS for n in _weight_names()])
```

```python
import jax
import jax.numpy as jnp
from jax import lax
from jax.experimental import pallas as pl
from jax.experimental.pallas import tpu as pltpu

F32 = jnp.float32
BF16 = jnp.bfloat16

D_MODEL = 1024
N_HEADS = 16
HEAD_DIM = 64
SWA_KV_HEADS = 2
SWA_GROUP = 8
SWA_WINDOW = 128
ROPE_THETA = 500000.0
ROPE_DIM = 16
RMS_EPS = 1e-6
NEG_INF = -1e30
ATTN_SCALE = HEAD_DIM ** -0.5
N_CHIPS = 4
N_DEVICES = 8

SLAB = 256
ATT_BLK = 128
ATT_BQ = 512
ATT_BK = 512
ROW_TILE = 256
V7X_VMEM_LIMIT = 56 * 1024 * 1024

ADAM_LR, ADAM_B1, ADAM_B2, ADAM_EPS, ADAM_WD, ADAM_STEP = 0.001, 0.9, 0.999, 1e-08, 0.01, 10


def _cparams(sem=None):
    return pltpu.CompilerParams(dimension_semantics=sem, vmem_limit_bytes=V7X_VMEM_LIMIT)


def _dot(a, b):
    return jnp.dot(a, b, preferred_element_type=F32)


def _dot_nt(a, b):
    return lax.dot_general(a, b, (((1,), (1,)), ((), ())), preferred_element_type=F32)


def _dot_tn(a, b):
    return lax.dot_general(a, b, (((0,), (0,)), ((), ())), preferred_element_type=F32)


def _row_tile(M, K):
    return min(M, 1024) if K >= 1024 else M


def _finish(epi, acc, ex, outs):
    res = epi(acc, *[e[...] for e in ex]) if epi is not None else (acc,)
    for o, r in zip(outs, res):
        o[...] = r.astype(o.dtype)


def _once(shape, index_map):
    return pl.BlockSpec(shape, index_map, pipeline_mode=pl.Buffered(1))


def mm_nn(a, wg, t, *, name, epi=None, extras=(), out_dtypes=(BF16,), norm_gain=None):
    off, K, ns, row = t
    M = a.shape[0]
    sb = off // K
    ne, no = len(extras), len(out_dtypes)
    norm = norm_gain is not None
    if row:
        tm = M if norm else _row_tile(M, K)
        nb = N_CHIPS
        grid = (M // tm, ns)
        a_shape = (tm, N_CHIPS * K)
        a_spec = (_once if norm else pl.BlockSpec)(a_shape, lambda i, q: (i, 0))
        b_specs = [pl.BlockSpec((None, K, SLAB), lambda i, q, j=j: (j, sb + q, 0)) for j in range(nb)]
        tile = pl.BlockSpec((tm, SLAB), lambda i, q: (i, q))
        n_out = ns * SLAB
    else:
        nb = ns
        grid = (N_CHIPS,)
        a_shape = (M, K)
        a_spec = (_once if norm else pl.BlockSpec)(a_shape, lambda j: (0, 0))
        b_specs = [pl.BlockSpec((None, K, SLAB), lambda j, q=q: (j, sb + q, 0)) for q in range(ns)]
        tile = pl.BlockSpec((M, ns * SLAB), lambda j: (0, j))
        n_out = N_CHIPS * ns * SLAB

    def body(a_ref, *rest):
        if norm:
            g_ref, rest, h_out, h_ref = rest[0], rest[1:-2], rest[-2], rest[-1]

            @pl.when(pl.program_id(len(grid) - 1) == 0)
            def _():
                xv = a_ref[...]
                h_ref[...] = (xv * _rstd(xv) * g_ref[...]).astype(BF16)
                h_out[...] = h_ref[...]

            a_ref = h_ref
        bs, ex, outs = rest[:nb], rest[nb:nb + ne], rest[nb + ne:]
        if row:
            acc = _dot(a_ref[:, pl.ds(0, K)], bs[0][...])
            for j in range(1, nb):
                acc = acc + _dot(a_ref[:, pl.ds(j * K, K)], bs[j][...])
            _finish(epi, acc, ex, outs)
        else:
            av = a_ref[...]
            for q in range(ns):
                cols = pl.ds(q * SLAB, SLAB)
                _finish(epi, _dot(av, bs[q][...]), [e.at[:, cols] for e in ex], [o.at[:, cols] for o in outs])

    h_spec = _once(a_shape, (lambda i, q: (i, 0)) if row else (lambda j: (0, 0)))
    return pl.pallas_call(
        body, name=name, grid=grid,
        in_specs=[a_spec] + ([pl.BlockSpec(norm_gain.shape, lambda *_: (0, 0))] if norm else []) + b_specs + [tile] * ne,
        out_specs=[tile] * no + ([h_spec] if norm else []),
        out_shape=[jax.ShapeDtypeStruct((M, n_out), d) for d in out_dtypes]
        + ([jax.ShapeDtypeStruct(a.shape, BF16)] if norm else []),
        scratch_shapes=[pltpu.VMEM(a_shape, BF16)] if norm else [],
        compiler_params=_cparams((("arbitrary" if norm else "parallel"),) * len(grid)),
    )(a, *([norm_gain] if norm else []), *([wg] * nb), *extras)


def mm_nt(dy, wg, t, *, name, epi=None, extras=(), out_dtypes=(BF16,), rms=None):
    off, K, ns, row = t
    M = dy.shape[0]
    tm = _row_tile(M, K)
    sb = off // K
    if rms is not None:
        x, gain, dres = rms
        extras, out_dtypes = (x, dres), (F32, BF16)
    ne, no = len(extras), len(out_dtypes)
    grid = (M // tm, N_CHIPS)
    b_specs = [pl.BlockSpec((None, K, SLAB), lambda i, j, q=q: (j, sb + q, 0)) for q in range(ns)]
    if row:
        dy_spec = pl.BlockSpec((tm, ns * SLAB), lambda i, j: (i, 0))
        tile = pl.BlockSpec((tm, K), lambda i, j: (i, j))
        n_out = N_CHIPS * K
        sem = ("parallel", "parallel")
    else:
        dy_spec = pl.BlockSpec((tm, ns * SLAB), lambda i, j: (i, j))
        tile = pl.BlockSpec((tm, K), lambda i, j: (i, 0))
        n_out = K
        sem = ("arbitrary" if rms is not None else "parallel", "arbitrary")
    one = pl.BlockSpec((1, K), lambda i, j: (0, 0))

    def body(dy_ref, *rest):
        if rms is not None:
            g_ref, rest, dg_ref, acc_ref = rest[0], rest[1:-2], rest[-2], rest[-1]
            rest = rest + (acc_ref,)
        bs, ex, outs = rest[:ns], rest[ns:ns + ne], rest[ns + ne:ns + ne + no]
        part = _dot_nt(dy_ref[:, pl.ds(0, SLAB)], bs[0][...])
        for q in range(1, ns):
            part = part + _dot_nt(dy_ref[:, pl.ds(q * SLAB, SLAB)], bs[q][...])
        if row:
            _finish(epi, part, ex, outs)
        else:
            acc_ref = rest[-1]
            i, j = pl.program_id(0), pl.program_id(1)

            @pl.when(j == 0)
            def _():
                acc_ref[...] = part

            @pl.when(j > 0)
            def _():
                acc_ref[...] += part

            @pl.when(j == N_CHIPS - 1)
            def _():
                if rms is None:
                    _finish(epi, acc_ref[...], ex, outs)
                else:
                    dx, dg = _rms_bwd_tile(ex[0][...], g_ref[...], acc_ref[...])
                    dx = dx + ex[1][...]
                    outs[0][...] = dx
                    outs[1][...] = dx.astype(BF16)

                    @pl.when(i == 0)
                    def _():
                        dg_ref[...] = dg

                    @pl.when(i > 0)
                    def _():
                        dg_ref[...] += dg

    has = rms is not None
    return pl.pallas_call(
        body, name=name, grid=grid,
        in_specs=[dy_spec] + ([one] if has else []) + b_specs + [tile] * ne,
        out_specs=[tile] * no + ([one] if has else []),
        out_shape=[jax.ShapeDtypeStruct((M, n_out), d) for d in out_dtypes] + ([jax.ShapeDtypeStruct((1, K), F32)] if has else []),
        scratch_shapes=[] if row else [pltpu.VMEM((tm, K), F32)],
        compiler_params=_cparams(sem),
    )(dy, *([gain] if has else []), *([wg] * ns), *extras)


def mm_plain(a, b, *, name):
    M, K = a.shape
    N = b.shape[1]
    tm = min(M, 512)

    def body(a_ref, b_ref, o_ref):
        o_ref[...] = _dot(a_ref[...], b_ref[...])

    return pl.pallas_call(
        body, name=name, grid=(M // tm,),
        in_specs=[pl.BlockSpec((tm, K), lambda i: (i, 0)), pl.BlockSpec((K, N), lambda i: (0, 0))],
        out_specs=pl.BlockSpec((tm, N), lambda i: (i, 0)), out_shape=jax.ShapeDtypeStruct((M, N), F32),
        compiler_params=_cparams(("parallel",)),
    )(a, b)


def mm_tn(x, dy, g, t, *, name):
    off, K, ns, row = t
    S = x.shape[0]
    per = ns if off % (ns * K) == 0 else 1
    grid = (N_CHIPS, ns // per)
    if row:
        x_map = lambda j, q: (0, j)
        dy_map = lambda j, q: (0, q)
    else:
        x_map = lambda j, q: (0, 0)
        dy_map = lambda j, q: (0, j * (ns // per) + q)

    def body(g_in, x_ref, dy_ref, o_ref):
        del g_in
        xt = x_ref[...].T
        for q in range(per):
            o_ref[pl.ds(q * K, K), :] = _dot(xt, dy_ref[:, pl.ds(q * SLAB, SLAB)]).astype(o_ref.dtype)

    return pl.pallas_call(
        body, name=name, grid=grid,
        in_specs=[pl.BlockSpec(memory_space=pl.ANY), pl.BlockSpec((S, K), x_map), pl.BlockSpec((S, per * SLAB), dy_map)],
        out_specs=pl.BlockSpec((None, per * K, SLAB), lambda j, q: (j, off // (per * K) + q, 0)),
        out_shape=jax.ShapeDtypeStruct(g.shape, g.dtype),
        input_output_aliases={0: 0},
        compiler_params=_cparams(("parallel", "parallel")),
    )(g, x, dy)


def ew(fn, ins, out_dtypes, *, name, bcast=()):
    S = ins[0].shape[0]
    tr = min(ROW_TILE, S)
    cols = ins[0].shape[1]
    ni, nb = len(ins), len(bcast)

    def body(*refs):
        res = fn(*[r[...] for r in refs[:ni + nb]])
        for o, r in zip(refs[ni + nb:], res):
            o[...] = r.astype(o.dtype)

    return pl.pallas_call(
        body, name=name, grid=(S // tr,),
        in_specs=[pl.BlockSpec((tr, a.shape[1]), lambda i: (i, 0)) for a in ins]
        + [pl.BlockSpec(b.shape, lambda i: (0, 0)) for b in bcast],
        out_specs=[pl.BlockSpec((tr, cols), lambda i: (i, 0)) for _ in out_dtypes],
        out_shape=[jax.ShapeDtypeStruct((S, cols), d) for d in out_dtypes],
        compiler_params=_cparams(("parallel",)),
    )(*ins, *bcast)


def _rstd(x):
    return lax.rsqrt(jnp.mean(x * x, axis=-1, keepdims=True) + RMS_EPS)


def _sigmoid(x):
    return 1.0 / (1.0 + jnp.exp(-x))


def _log_sigmoid(z):
    return jnp.minimum(z, 0.0) - jnp.log(1.0 + jnp.exp(-jnp.abs(z)))


def _rms_bwd_tile(xv, gv, dh):
    rstd = _rstd(xv)
    xhat = xv * rstd
    gd = dh * gv
    dx = rstd * (gd - xhat * jnp.mean(xhat * gd, axis=-1, keepdims=True))
    return dx, jnp.sum(dh * xhat, axis=0, keepdims=True)


def rms_bwd(x, g, dh, dres, *, name):
    S, D = x.shape
    tr = min(ROW_TILE, S)

    def body(x_ref, g_ref, dh_ref, dres_ref, dx_ref, dxb_ref, dg_ref):
        i = pl.program_id(0)
        dx, dg = _rms_bwd_tile(x_ref[...], g_ref[...], dh_ref[...])
        dx = dx + dres_ref[...]
        dx_ref[...] = dx
        dxb_ref[...] = dx.astype(BF16)

        @pl.when(i == 0)
        def _():
            dg_ref[...] = dg

        @pl.when(i > 0)
        def _():
            dg_ref[...] += dg

    row = pl.BlockSpec((tr, D), lambda i: (i, 0))
    one = pl.BlockSpec((1, D), lambda i: (0, 0))
    return pl.pallas_call(
        body, name=name, grid=(S // tr,),
        in_specs=[row, one, row, row], out_specs=[row, row, one],
        out_shape=[jax.ShapeDtypeStruct((S, D), F32), jax.ShapeDtypeStruct((S, D), BF16),
                   jax.ShapeDtypeStruct((1, D), F32)],
        compiler_params=_cparams(("arbitrary",)),
    )(x, g, dh, dres)


def loss_head(x, g, target, *, name):
    S, D = x.shape
    tr = min(ROW_TILE, S)

    def body(x_ref, g_ref, t_ref, dx_ref, dg_ref, loss_ref):
        i = pl.program_id(0)
        xv, gv = x_ref[...], g_ref[...]
        err = xv * _rstd(xv) * gv - t_ref[...]
        part = 0.5 * jnp.sum(jnp.mean(err * err, axis=-1, keepdims=True), axis=0, keepdims=True)
        dx, dg = _rms_bwd_tile(xv, gv, err * (1.0 / D))
        dx_ref[...] = dx
        part = jnp.broadcast_to(part, loss_ref.shape)

        @pl.when(i == 0)
        def _():
            dg_ref[...] = dg
            loss_ref[...] = part

        @pl.when(i > 0)
        def _():
            dg_ref[...] += dg
            loss_ref[...] += part

    row = pl.BlockSpec((tr, D), lambda i: (i, 0))
    one = pl.BlockSpec((1, D), lambda i: (0, 0))
    return pl.pallas_call(
        body, name=name, grid=(S // tr,),
        in_specs=[row, one, row], out_specs=[row, one, pl.BlockSpec((1, 128), lambda i: (0, 0))],
        out_shape=[jax.ShapeDtypeStruct((S, D), F32), jax.ShapeDtypeStruct((1, D), F32),
                   jax.ShapeDtypeStruct((1, 128), F32)],
        compiler_params=_cparams(("arbitrary",)),
    )(x, g, target)


def rope_tables(S):
    half = ROPE_DIM // 2
    inv_freq = ROPE_THETA ** (-jnp.arange(half, dtype=F32) / half)
    ang = jnp.arange(S, dtype=F32)[:, None] * inv_freq[None, :]
    cos, sin = jnp.cos(ang), jnp.sin(ang)
    z = jnp.zeros((S, HEAD_DIM - ROPE_DIM), F32)
    zh = jnp.zeros((S, half), F32)
    c = jnp.concatenate([cos, cos, jnp.ones_like(z)], axis=1)
    sa = jnp.concatenate([zh, sin, z], axis=1)
    sb = jnp.concatenate([-sin, zh, z], axis=1)
    return [jnp.concatenate([t, t], axis=1) for t in (c, sa, sb)]


def _wide(t, n):
    return jnp.tile(t, (1, n // t.shape[1]))


def rope_fwd(xqk, tables, *, name):
    n, half = xqk.shape[1], ROPE_DIM // 2

    def fn(x, c, sa, sb):
        return (x * _wide(c, n) + pltpu.roll(x, half, 1) * _wide(sa, n) + pltpu.roll(x, n - half, 1) * _wide(sb, n),)

    return ew(fn, [xqk] + list(tables), [BF16], name=name)[0]


def rope_bwd(dy, tables, *, name):
    n, half = dy.shape[1], ROPE_DIM // 2

    def fn(d, c, sa, sb):
        return (d * _wide(c, n) + pltpu.roll(d * _wide(sa, n), n - half, 1) + pltpu.roll(d * _wide(sb, n), half, 1),)

    return ew(fn, [dy] + list(tables), [BF16], name=name)[0]


def _split3(x):
    h1 = x.astype(BF16)
    r1 = x - h1.astype(F32)
    h2 = r1.astype(BF16)
    return h1, h2, (r1 - h2.astype(F32)).astype(BF16)


def _tri(n, cmp):
    r = lax.broadcasted_iota(jnp.int32, (n, n), 0)
    c = lax.broadcasted_iota(jnp.int32, (n, n), 1)
    return cmp(r, c).astype(BF16)


def fox_gate_fwd(fl, b, *, name):
    S, W = fl.shape
    tr = min(ROW_TILE, S)

    def body(fl_ref, b_ref, cum_ref, carry):
        i = pl.program_id(0)

        @pl.when(i == 0)
        def _():
            carry[...] = jnp.zeros_like(carry)

        lower = _tri(tr, lambda r, c: r >= c)
        cs = carry[...]
        for piece in _split3(_log_sigmoid(fl_ref[...] + b_ref[...])):
            cs = cs + _dot(lower, piece)
        cum_ref[...] = cs
        carry[...] = cs[tr - 1:tr, :]

    return pl.pallas_call(
        body, name=name, grid=(S // tr,),
        in_specs=[pl.BlockSpec((tr, W), lambda i: (i, 0)), pl.BlockSpec((1, W), lambda i: (0, 0))],
        out_specs=pl.BlockSpec((tr, W), lambda i: (i, 0)),
        out_shape=jax.ShapeDtypeStruct((S, W), F32),
        scratch_shapes=[pltpu.VMEM((1, W), F32)],
        compiler_params=_cparams(("arbitrary",)),
    )(fl, b)


def fox_gate_bwd(dcum, fl, b, *, name):
    S, W = fl.shape
    tr = min(ROW_TILE, S)
    nb = S // tr

    def body(dc_ref, fl_ref, b_ref, dfl_ref, db_ref, carry):
        i = pl.program_id(0)

        @pl.when(i == 0)
        def _():
            carry[...] = jnp.zeros_like(carry)

        upper = _tri(tr, lambda r, c: r <= c)
        cs = carry[...]
        for piece in _split3(dc_ref[...]):
            cs = cs + _dot(upper, piece)
        carry[...] = cs[0:1, :]
        dfl = cs * _sigmoid(-(fl_ref[...] + b_ref[...]))
        dfl_ref[...] = dfl
        db = jnp.sum(dfl, axis=0, keepdims=True)

        @pl.when(i == 0)
        def _():
            db_ref[...] = db

        @pl.when(i > 0)
        def _():
            db_ref[...] += db

    rev = pl.BlockSpec((tr, W), lambda i: (nb - 1 - i, 0))
    one = pl.BlockSpec((1, W), lambda i: (0, 0))
    return pl.pallas_call(
        body, name=name, grid=(nb,),
        in_specs=[rev, rev, one], out_specs=[rev, one],
        out_shape=[jax.ShapeDtypeStruct((S, W), F32), jax.ShapeDtypeStruct((1, W), F32)],
        scratch_shapes=[pltpu.VMEM((1, W), F32)],
        compiler_params=_cparams(("arbitrary",)),
    )(dcum, fl, b)


def _blk_iota(tq, tk):
    return (lax.broadcasted_iota(jnp.int32, (tq, tk), 0), lax.broadcasted_iota(jnp.int32, (tq, tk), 1))


def _cs(xb, tri):
    return _dot(xb, tri)


def _rowsum(xb):
    return jnp.sum(xb.astype(F32), axis=1, keepdims=True)


def _sb_block(qs, k, cmr, shift):
    z = _dot_nt(qs, k)
    lb = jnp.minimum(z, 0.0) - jnp.log(1.0 + jnp.exp(-jnp.abs(z)))
    if cmr is None:
        return lb, (lb - z).astype(BF16), None
    strict = cmr < shift
    lom = jnp.where(strict, lb - z, 0.0).astype(BF16)
    return lb, lom, strict


def _keep(mask, x):
    return x if mask is None else jnp.where(mask, x, 0.0)


def _att_tiles(S):
    return min(ATT_BQ, S), min(ATT_BK, S)


PAIR = 2 * HEAD_DIM
N_PAIRS = N_HEADS // 2


def _pair_specs(S, tq):
    cols = D_MODEL // PAIR
    qspec = pl.BlockSpec((tq, PAIR), lambda p, i: (i, p))
    kspec = pl.BlockSpec((S, PAIR), lambda p, i: (0, cols + p))
    vspec = pl.BlockSpec((S, PAIR), lambda p, i: (0, 2 * cols + p))
    kvout = pl.BlockSpec((S, PAIR), lambda p, i: (0, p))
    vec = pl.BlockSpec((2, tq, 1), lambda p, i: (p, i, 0))
    return qspec, kspec, vspec, kvout, vec


def _head_lanes(h):
    lane = lax.broadcasted_iota(jnp.int32, (1, PAIR), 1)
    return (lane >= h * HEAD_DIM) & (lane < (h + 1) * HEAD_DIM)


def _only(sel, x):
    return jnp.where(sel, x, jnp.zeros_like(x))


def sb_fwd(proj, *, name):
    S = proj.shape[0]
    tq, tk = _att_tiles(S)
    qspec, kspec, vspec, _, vec = _pair_specs(S, tq)

    def body(q_ref, k_ref, v_ref, o_ref, t_ref):
        i = pl.program_id(1)
        row, col = _blk_iota(tq, tk)
        cmr = col - row
        below = _tri(tk, lambda r, c: r > c)
        nkb = (i + 1) * (tq // tk)
        out = []
        for h in range(2):
            sel = _head_lanes(h)
            qs = _only(sel, q_ref[...] * ATTN_SCALE)

            def step(n, carry, masked):
                r_sum, acc = carry
                kb = nkb - 1 - n
                ks = pl.multiple_of(kb * tk, tk)
                lb, lom, strict = _sb_block(qs, k_ref[pl.ds(ks, tk), :], cmr if masked else None, i * tq - kb * tk)
                w = _keep(strict, jnp.exp(lb + _cs(lom, below) + r_sum))
                acc = acc + _dot(w.astype(BF16), _only(sel, v_ref[pl.ds(ks, tk), :]))
                return r_sum + _rowsum(lom), acc

            nd = tq // tk
            carry = lax.fori_loop(0, nd, lambda n, c: step(n, c, True), (jnp.zeros((tq, 1), F32), jnp.zeros((tq, PAIR), F32)))
            r_sum, acc = lax.fori_loop(nd, nkb, lambda n, c: step(n, c, False), carry)
            t_ref[h] = r_sum
            out.append(acc)
        o_ref[...] = (out[0] + out[1]).astype(o_ref.dtype)

    return pl.pallas_call(
        body, name=name, grid=(N_PAIRS, S // tq),
        in_specs=[qspec, kspec, vspec], out_specs=[qspec, vec],
        out_shape=[jax.ShapeDtypeStruct((S, D_MODEL), BF16), jax.ShapeDtypeStruct((N_HEADS, S, 1), F32)],
        compiler_params=_cparams(("parallel", "arbitrary")),
    )(proj, proj, proj)


def sb_bwd(proj, tot, do, *, name):
    S = proj.shape[0]
    tq, tk = _att_tiles(S)
    qspec, kspec, vspec, kvout, vec = _pair_specs(S, tq)

    def body(q_ref, k_ref, v_ref, t_ref, do_ref, dq_ref, dk_out, dv_out, dk_ref, dv_ref):
        i = pl.program_id(1)

        @pl.when(i == 0)
        def _():
            dk_ref[...] = jnp.zeros_like(dk_ref)
            dv_ref[...] = jnp.zeros_like(dv_ref)

        row, col = _blk_iota(tq, tk)
        cmr = col - row
        upto = _tri(tk, lambda r, c: r <= c)
        before = _tri(tk, lambda r, c: r < c)
        out = []
        for h in range(2):
            sel = _head_lanes(h)
            qs, dov, t_all = _only(sel, q_ref[...] * ATTN_SCALE), _only(sel, do_ref[...]), t_ref[h]

            def step(kb, carry, masked):
                p_sum, e_sum, dq = carry
                ks = pl.multiple_of(kb * tk, tk)
                kv = k_ref[pl.ds(ks, tk), :]
                lb, lom, strict = _sb_block(qs, kv, cmr if masked else None, i * tq - kb * tk)
                tail = t_all - p_sum - _cs(lom, upto)
                w = _keep(strict, jnp.exp(lb + tail))
                e = _dot_nt(dov, v_ref[pl.ds(ks, tk), :]) * w
                eb = e.astype(BF16)
                e_before = e_sum + _cs(eb, before)
                beta = jnp.exp(lb)
                dzb = _keep(strict, e - (e + e_before) * beta).astype(BF16)
                dk_ref[pl.ds(ks, tk), :] += _dot_tn(dzb, qs)
                dv_ref[pl.ds(ks, tk), :] += _dot_tn(w.astype(BF16), dov)
                return p_sum + _rowsum(lom), e_sum + _rowsum(eb), dq + _dot(dzb, _only(sel, kv))

            zero = jnp.zeros((tq, 1), F32)
            nlow = i * (tq // tk)
            carry = lax.fori_loop(0, nlow, lambda kb, c: step(kb, c, False), (zero, zero, jnp.zeros((tq, PAIR), F32)))
            out.append(lax.fori_loop(nlow, nlow + tq // tk, lambda kb, c: step(kb, c, True), carry)[2])
        dq_ref[...] = ((out[0] + out[1]) * ATTN_SCALE).astype(dq_ref.dtype)

        @pl.when(i == S // tq - 1)
        def _():
            dk_out[...] = dk_ref[...].astype(dk_out.dtype)
            dv_out[...] = dv_ref[...].astype(dv_out.dtype)

    full = jax.ShapeDtypeStruct((S, D_MODEL), BF16)
    return pl.pallas_call(
        body, name=name, grid=(N_PAIRS, S // tq),
        in_specs=[qspec, kspec, vspec, vec, qspec], out_specs=[qspec, kvout, kvout],
        out_shape=[full, full, full],
        scratch_shapes=[pltpu.VMEM((S, PAIR), F32)] * 2,
        compiler_params=_cparams(("parallel", "arbitrary")),
    )(proj, proj, proj, tot, do)


def _fox_logits(qs, k, cq, ck, cmr, shift):
    s = _dot_nt(qs, k) + cq - ck
    if cmr is None:
        return s, None
    causal = cmr <= shift
    return jnp.where(causal, s, NEG_INF), causal


def fox_fwd(proj, cq, ck, *, name):
    S = proj.shape[0]
    tq, tk = _att_tiles(S)
    qspec, kspec, vspec, _, vec = _pair_specs(S, tq)
    ckspec = pl.BlockSpec((2, S // tk, 1, tk), lambda p, i: (p, 0, 0, 0))

    def body(q_ref, k_ref, v_ref, cq_ref, ck_ref, o_ref, lse_ref):
        i = pl.program_id(1)
        row, col = _blk_iota(tq, tk)
        cmr = col - row
        out = []
        for h in range(2):
            sel = _head_lanes(h)
            qs, cqv = _only(sel, q_ref[...] * ATTN_SCALE), cq_ref[h]

            def step(kb, carry, masked):
                m, l, acc = carry
                ks = pl.multiple_of(kb * tk, tk)
                s, _ = _fox_logits(qs, k_ref[pl.ds(ks, tk), :], cqv, ck_ref[h, kb], cmr if masked else None, i * tq - kb * tk)
                m_new = jnp.maximum(m, jnp.max(s, axis=1, keepdims=True))
                alpha = jnp.exp(m - m_new)
                p = jnp.exp(s - m_new)
                l = alpha * l + jnp.sum(p, axis=1, keepdims=True)
                acc = alpha * acc + _dot(p.astype(BF16), _only(sel, v_ref[pl.ds(ks, tk), :]))
                return m_new, l, acc

            nlow = i * (tq // tk)
            carry = lax.fori_loop(0, nlow, lambda kb, c: step(kb, c, False),
                                  (jnp.full((tq, 1), NEG_INF, F32), jnp.zeros((tq, 1), F32), jnp.zeros((tq, PAIR), F32)))
            m, l, acc = lax.fori_loop(nlow, nlow + tq // tk, lambda kb, c: step(kb, c, True), carry)
            lse_ref[h] = m + jnp.log(l)
            out.append(acc / l)
        o_ref[...] = (out[0] + out[1]).astype(o_ref.dtype)

    return pl.pallas_call(
        body, name=name, grid=(N_PAIRS, S // tq),
        in_specs=[qspec, kspec, vspec, vec, ckspec], out_specs=[qspec, vec],
        out_shape=[jax.ShapeDtypeStruct((S, D_MODEL), BF16), jax.ShapeDtypeStruct((N_HEADS, S, 1), F32)],
        compiler_params=_cparams(("parallel", "arbitrary")),
    )(proj, proj, proj, cq, ck)


def fox_bwd(proj, o, lse, cq, ck, do, *, name):
    S = proj.shape[0]
    tq, tk = _att_tiles(S)
    qspec, kspec, vspec, kvout, vec = _pair_specs(S, tq)
    ckspec = pl.BlockSpec((2, S // tk, 1, tk), lambda p, i: (p, 0, 0, 0))

    def body(q_ref, k_ref, v_ref, o_ref, lse_ref, cq_ref, ck_ref, do_ref, dq_ref, dk_out, dv_out, dcq_ref, dck_ref,
             dk_ref, dv_ref):
        i = pl.program_id(1)

        @pl.when(i == 0)
        def _():
            dk_ref[...] = jnp.zeros_like(dk_ref)
            dv_ref[...] = jnp.zeros_like(dv_ref)
            dck_ref[...] = jnp.zeros_like(dck_ref)

        row, col = _blk_iota(tq, tk)
        cmr = col - row
        out = []
        for h in range(2):
            sel = _head_lanes(h)
            qs, dov, cqv, lsev = _only(sel, q_ref[...] * ATTN_SCALE), _only(sel, do_ref[...]), cq_ref[h], lse_ref[h]
            delta = jnp.sum(dov.astype(F32) * o_ref[...].astype(F32), axis=1, keepdims=True)

            def step(kb, carry, masked):
                dq, dcq = carry
                ks = pl.multiple_of(kb * tk, tk)
                kv = k_ref[pl.ds(ks, tk), :]
                s, causal = _fox_logits(qs, kv, cqv, ck_ref[h, kb], cmr if masked else None, i * tq - kb * tk)
                p = _keep(causal, jnp.exp(s - lsev))
                ds = p * (_dot_nt(dov, v_ref[pl.ds(ks, tk), :]) - delta)
                dck_ref[h, kb] += jnp.sum(ds, axis=0, keepdims=True)
                dsb = ds.astype(BF16)
                dk_ref[pl.ds(ks, tk), :] += _dot_tn(dsb, qs)
                dv_ref[pl.ds(ks, tk), :] += _dot_tn(p.astype(BF16), dov)
                return dq + _dot(dsb, _only(sel, kv)), dcq + jnp.sum(ds, axis=1, keepdims=True)

            nlow = i * (tq // tk)
            carry = lax.fori_loop(0, nlow, lambda kb, c: step(kb, c, False),
                                  (jnp.zeros((tq, PAIR), F32), jnp.zeros((tq, 1), F32)))
            dq, dcq = lax.fori_loop(nlow, nlow + tq // tk, lambda kb, c: step(kb, c, True), carry)
            dcq_ref[h] = dcq
            out.append(dq)
        dq_ref[...] = ((out[0] + out[1]) * ATTN_SCALE).astype(dq_ref.dtype)

        @pl.when(i == S // tq - 1)
        def _():
            dk_out[...] = dk_ref[...].astype(dk_out.dtype)
            dv_out[...] = dv_ref[...].astype(dv_out.dtype)

    full = jax.ShapeDtypeStruct((S, D_MODEL), BF16)
    return pl.pallas_call(
        body, name=name, grid=(N_PAIRS, S // tq),
        in_specs=[qspec, kspec, vspec, qspec, vec, vec, ckspec, qspec],
        out_specs=[qspec, kvout, kvout, vec, ckspec],
        out_shape=[full, full, full, jax.ShapeDtypeStruct((N_HEADS, S, 1), F32),
                   jax.ShapeDtypeStruct((N_HEADS, S // tk, 1, tk), F32)],
        scratch_shapes=[pltpu.VMEM((S, PAIR), F32)] * 2,
        compiler_params=_cparams(("parallel", "arbitrary")),
    )(proj, proj, proj, o, lse, cq, ck, do)


def _swa_specs(S, tq):
    qspec = pl.BlockSpec((None, SWA_GROUP, tq, HEAD_DIM), lambda g, i: (g, 0, i, 0))
    kvspec = pl.BlockSpec((None, S + SWA_WINDOW, HEAD_DIM), lambda g, i: (g, 0, 0))
    vec = pl.BlockSpec((None, SWA_GROUP, tq, 1), lambda g, i: (g, 0, i, 0))
    sink = pl.BlockSpec((None, SWA_GROUP * tq, 1), lambda g, i: (g, 0, 0))
    return qspec, kvspec, vec, sink


def _swa_logits(q2, kw, i, tq):
    rows = q2.shape[0]
    r = lax.broadcasted_iota(jnp.int32, (rows, 2 * tq), 0)
    c = lax.broadcasted_iota(jnp.int32, (rows, 2 * tq), 1)
    diff = (r & (tq - 1)) + tq - c
    ok = (diff >= 0) & (diff < SWA_WINDOW) & (c + (i - 1) * tq >= 0)
    return jnp.where(ok, _dot_nt(q2, kw) * ATTN_SCALE, NEG_INF), ok


def swa_fwd(q, kp, vp, sink, *, name):
    _, G, S, _ = q.shape
    tq = ATT_BLK
    qspec, kvspec, vec, sinkspec = _swa_specs(S, tq)

    def body(q_ref, k_ref, v_ref, s_ref, o_ref, lse_ref):
        i = pl.program_id(1)
        q2 = q_ref[...].reshape(G * tq, HEAD_DIM)
        ws = pl.multiple_of(i * tq, tq)
        logits, _ = _swa_logits(q2, k_ref[pl.ds(ws, 2 * tq), :], i, tq)
        sk = s_ref[...]
        m = jnp.maximum(jnp.max(logits, axis=1, keepdims=True), sk)
        e = jnp.exp(logits - m)
        den = jnp.sum(e, axis=1, keepdims=True) + jnp.exp(sk - m)
        o = _dot((e / den).astype(BF16), v_ref[pl.ds(ws, 2 * tq), :])
        o_ref[...] = o.reshape(G, tq, HEAD_DIM).astype(o_ref.dtype)
        lse_ref[...] = (m + jnp.log(den)).reshape(G, tq, 1)

    return pl.pallas_call(
        body, name=name, grid=(SWA_KV_HEADS, S // tq),
        in_specs=[qspec, kvspec, kvspec, sinkspec], out_specs=[qspec, vec],
        out_shape=[jax.ShapeDtypeStruct(q.shape, BF16), jax.ShapeDtypeStruct((SWA_KV_HEADS, G, S, 1), F32)],
        compiler_params=_cparams(("parallel", "arbitrary")),
    )(q, kp, vp, sink)


def swa_bwd(q, kp, vp, sink, o, lse, do, *, name):
    _, G, S, _ = q.shape
    tq = ATT_BLK
    qspec, kvspec, vec, sinkspec = _swa_specs(S, tq)

    def body(q_ref, k_ref, v_ref, s_ref, o_ref, lse_ref, do_ref, dq_ref, dk_ref, dv_ref, dsink_ref):
        i = pl.program_id(1)

        @pl.when(i == 0)
        def _():
            dk_ref[...] = jnp.zeros_like(dk_ref)
            dv_ref[...] = jnp.zeros_like(dv_ref)

        q2 = q_ref[...].reshape(G * tq, HEAD_DIM)
        do2 = do_ref[...].reshape(G * tq, HEAD_DIM)
        o2 = o_ref[...].reshape(G * tq, HEAD_DIM)
        lse2 = lse_ref[...].reshape(G * tq, 1)
        ws = pl.multiple_of(i * tq, tq)
        kw = k_ref[pl.ds(ws, 2 * tq), :]
        vw = v_ref[pl.ds(ws, 2 * tq), :]
        logits, ok = _swa_logits(q2, kw, i, tq)
        p = jnp.where(ok, jnp.exp(logits - lse2), 0.0)
        delta = jnp.sum(do2.astype(F32) * o2.astype(F32), axis=1, keepdims=True)
        ds = p * (_dot_nt(do2, vw) - delta)
        dsb = ds.astype(BF16)
        dq_ref[...] = (_dot(dsb, kw) * ATTN_SCALE).reshape(G, tq, HEAD_DIM)
        dk_ref[pl.ds(ws, 2 * tq), :] += _dot_tn(dsb, q2) * ATTN_SCALE
        dv_ref[pl.ds(ws, 2 * tq), :] += _dot_tn(p.astype(BF16), do2)
        dsink_ref[...] = (-jnp.exp(s_ref[...] - lse2) * delta).reshape(G, tq, 1)

    kvshape = jax.ShapeDtypeStruct(kp.shape, F32)
    return pl.pallas_call(
        body, name=name, grid=(SWA_KV_HEADS, S // tq),
        in_specs=[qspec, kvspec, kvspec, sinkspec, qspec, vec, qspec],
        out_specs=[qspec, kvspec, kvspec, vec],
        out_shape=[jax.ShapeDtypeStruct(q.shape, F32), kvshape, kvshape,
                   jax.ShapeDtypeStruct((SWA_KV_HEADS, G, S, 1), F32)],
        compiler_params=_cparams(("parallel", "arbitrary")),
    )(q, kp, vp, sink, o, lse, do)


def _adamw_tile(w, g, m, v):
    m = ADAM_B1 * m + (1.0 - ADAM_B1) * g
    v = ADAM_B2 * v + (1.0 - ADAM_B2) * (g * g)
    m_hat = m / (1.0 - ADAM_B1 ** ADAM_STEP)
    v_hat = v / (1.0 - ADAM_B2 ** ADAM_STEP)
    delta = -ADAM_LR * (m_hat / (jnp.sqrt(v_hat) + ADAM_EPS) + ADAM_WD * w)
    return g, delta, m, v


def adamw(gfull, t, w, m, v, after, *, name):
    off, K, ns, _ = t
    sb = off // K
    nat = pl.BlockSpec((K, SLAB), lambda q: (0, q))

    def body(g_ref, w_ref, m_ref, v_ref, after_ref, *outs):
        del after_ref
        for o, r in zip(outs, _adamw_tile(w_ref[...], g_ref[...], m_ref[...], v_ref[...])):
            o[...] = r

    return pl.pallas_call(
        body, name=name, grid=(ns,),
        in_specs=[pl.BlockSpec((K, SLAB), lambda q: (sb + q, 0)), nat, nat, nat, HBM],
        out_specs=[nat] * 4, out_shape=[jax.ShapeDtypeStruct(w.shape, F32)] * 4,
        compiler_params=_cparams(("parallel",)),
    )(gfull, w, m, v, after)


def adamw_small(g, w, m, v, *, name):
    rows, d = w.shape

    def body(g_ref, w_ref, m_ref, v_ref, *outs):
        for j in range(rows):
            one = pl.ds(j, 1)
            for k, r in enumerate(_adamw_tile(w_ref[one, :], g_ref[one, :], m_ref[one, :], v_ref[one, :])):
                outs[k * rows + j][...] = r

    flat = pl.pallas_call(body, name=name, out_shape=[jax.ShapeDtypeStruct((1, d), F32)] * (4 * rows))(g, w, m, v)
    return [flat[k * rows:(k + 1) * rows] for k in range(4)]


MESH = pl.DeviceIdType.MESH
HBM = pl.BlockSpec(memory_space=pl.ANY)


def _place():
    x, y, c = lax.axis_index("x"), lax.axis_index("y"), lax.axis_index("c")
    others = [(1 - x, y), (x, 1 - y), (1 - x, 1 - y)]
    return x, y, c, others


def _rcopy(src, dst, send_sems, recv_sems, k, to):
    return pltpu.make_async_remote_copy(src_ref=src, dst_ref=dst, send_sem=send_sems.at[k], recv_sem=recv_sems.at[k],
                                        device_id=to, device_id_type=MESH)


def _dma_sems(*counts):
    return [pltpu.SemaphoreType.DMA((n,)) for n in counts]


DMA_UNIT_ROWS = 128
DMA_PIECES = 4


def _row_pieces(h, n):
    units = h // DMA_UNIT_ROWS
    n = min(n, units)
    base, extra = divmod(units, n)
    sizes = [(base + (k < extra)) * DMA_UNIT_ROWS for k in range(n)]
    return [(sum(sizes[:k]), sizes[k]) for k in range(n)]


SEM = pl.BlockSpec(memory_space=pltpu.SEMAPHORE)
SPLIT_COPY = pltpu.CompilerParams(has_side_effects=pltpu.SideEffectType.DATAFLOW_SIDE_EFFECTING)
N_OTHERS = 3


def _hbm(a):
    return pltpu.with_memory_space_constraint(a, pltpu.HBM)


def _chip_rows(buf, chip, s0, sz):
    return buf.at[2 * chip[0] + chip[1], pl.ds(s0, sz)]


def allgather_start(bufs, *, name):
    n = len(bufs)

    def body(*refs):
        ins, send, recv, token = refs[:n], refs[n:2 * n], refs[2 * n:3 * n], refs[4 * n]
        x, y, c, others = _place()
        for i in range(n):
            h = bufs[i].shape[1] // 2
            for f, chip in enumerate(others):
                for s0, sz in _row_pieces(h, DMA_PIECES):
                    mine = _chip_rows(ins[i], (x, y), c * h + s0, sz)
                    _rcopy(mine, mine, send[i], recv[i], f, (*chip, c)).start()
        token[...] = jnp.zeros_like(token)

    res = pl.pallas_call(
        body, name=name, in_specs=[HBM] * n,
        out_specs=[SEM] * (2 * n) + [HBM] * n + [pl.BlockSpec(memory_space=pltpu.VMEM)],
        out_shape=[pltpu.SemaphoreType.DMA((N_OTHERS,))] * (2 * n) + [pltpu.HBM(b.shape, b.dtype) for b in bufs]
        + [jax.ShapeDtypeStruct((1, D_MODEL), F32)],
        input_output_aliases={i: 2 * n + i for i in range(n)},
        compiler_params=SPLIT_COPY,
    )(*[_hbm(b) for b in bufs])
    return res[:n], res[n:2 * n], res[2 * n:3 * n], res[3 * n]


def allgather_wait(buf, send, recv, after, *, name):
    h = buf.shape[1] // 2
    after = list(after)

    def body(buf_ref, send_sems, recv_sems, *rest):
        del rest
        x, y, c, others = _place()
        for f, chip in enumerate(others):
            mine = _chip_rows(buf_ref, (x, y), c * h, h)
            theirs = _chip_rows(buf_ref, chip, c * h, h)
            cp = _rcopy(mine, theirs, send_sems, recv_sems, f, (*chip, c))
            cp.wait_send()
            cp.wait_recv()

    return pl.pallas_call(
        body, name=name, in_specs=[HBM, SEM, SEM] + [HBM] * len(after), out_specs=HBM,
        out_shape=pltpu.HBM(buf.shape, buf.dtype), input_output_aliases={0: 0},
        compiler_params=SPLIT_COPY,
    )(buf, send, recv, *after)


def forward_start(buf, *, name):
    h = buf.shape[1] // 2

    def body(b_ref, send, recv, b_out, token):
        del b_out
        x, y, c, others = _place()
        for f, chip in enumerate(others):
            for s0, sz in _row_pieces(h, DMA_PIECES):
                rows = _chip_rows(b_ref, chip, c * h + s0, sz)
                _rcopy(rows, rows, send, recv, f, (x, y, 1 - c)).start()
        token[...] = jnp.zeros_like(token)

    return pl.pallas_call(
        body, name=name, in_specs=[HBM],
        out_specs=[SEM, SEM, HBM, pl.BlockSpec(memory_space=pltpu.VMEM)],
        out_shape=[pltpu.SemaphoreType.DMA((N_OTHERS,))] * 2 + [pltpu.HBM(buf.shape, buf.dtype),
                                                                 jax.ShapeDtypeStruct((1, D_MODEL), F32)],
        input_output_aliases={0: 2}, compiler_params=SPLIT_COPY,
    )(_hbm(buf))


def forward_wait(buf, send, recv, after, *, name):
    h = buf.shape[1] // 2

    def body(b_ref, send_sems, recv_sems, after_ref, b_out):
        del after_ref, b_out
        x, y, c, others = _place()
        for f, chip in enumerate(others):
            cp = _rcopy(_chip_rows(b_ref, chip, c * h, h), _chip_rows(b_ref, chip, (1 - c) * h, h),
                        send_sems, recv_sems, f, (x, y, 1 - c))
            cp.wait_send()
            cp.wait_recv()

    return pl.pallas_call(
        body, name=name, in_specs=[HBM, SEM, SEM, HBM], out_specs=HBM,
        out_shape=pltpu.HBM(buf.shape, buf.dtype), input_output_aliases={0: 0}, compiler_params=SPLIT_COPY,
    )(buf, send, recv, after)


def _sem1():
    return pltpu.SemaphoreType.DMA((1,))


TOKEN = jax.ShapeDtypeStruct((1, D_MODEL), F32)
VMEM_SPEC = pl.BlockSpec(memory_space=pltpu.VMEM)


def swap_start(grad, *, name):
    h = grad.shape[1] // 2

    def body(g_ref, land_ref, send, recv, g_out, land_out, token):
        del g_out, land_out
        x, y, c, _ = _place()
        for k in range(N_CHIPS):
            for s0, sz in _row_pieces(h, DMA_PIECES):
                _rcopy(g_ref.at[k, pl.ds((1 - c) * h + s0, sz)], land_ref.at[k, pl.ds(s0, sz)], send, recv, 0, (x, y, 1 - c)).start()
        token[...] = jnp.zeros_like(token)

    land = lax.empty((N_CHIPS, h, SLAB), grad.dtype)
    return pl.pallas_call(
        body, name=name, in_specs=[HBM, HBM], out_specs=[SEM, SEM, HBM, HBM, VMEM_SPEC],
        out_shape=[_sem1(), _sem1(), pltpu.HBM(grad.shape, grad.dtype), pltpu.HBM(land.shape, land.dtype), TOKEN],
        input_output_aliases={0: 2, 1: 3}, compiler_params=SPLIT_COPY,
    )(_hbm(grad), _hbm(land))


def swap_wait(grad, land, send, recv, after, *, name):
    h = land.shape[1]

    def body(g_ref, land_ref, send_sems, recv_sems, after_ref, g_out, land_out):
        del after_ref, g_out, land_out
        x, y, c, _ = _place()
        cp = _rcopy(g_ref.at[:, pl.ds((1 - c) * h, h)], land_ref, send_sems, recv_sems, 0, (x, y, 1 - c))
        cp.wait_send()
        cp.wait_recv()

    return pl.pallas_call(
        body, name=name, in_specs=[HBM, HBM, SEM, SEM, HBM], out_specs=[HBM, HBM],
        out_shape=[pltpu.HBM(grad.shape, grad.dtype), pltpu.HBM(land.shape, land.dtype)],
        input_output_aliases={0: 0, 1: 1}, compiler_params=SPLIT_COPY,
    )(grad, land, send, recv, after)


def scatter_start(part, *, name):
    h = part.shape[1]

    def body(part_ref, land_ref, send, recv, part_out, land_out, token):
        del part_out, land_out
        x, y, c, others = _place()
        for f, chip in enumerate(others):
            for s0, sz in _row_pieces(h, DMA_PIECES):
                _rcopy(_chip_rows(part_ref, chip, s0, sz), land_ref.at[f, pl.ds(s0, sz)], send, recv, f, (*chip, c)).start()
        token[...] = jnp.zeros_like(token)

    land = lax.empty((N_OTHERS,) + part.shape[1:], part.dtype)
    return pl.pallas_call(
        body, name=name, in_specs=[HBM, HBM],
        out_specs=[SEM, SEM, HBM, HBM, pl.BlockSpec(memory_space=pltpu.VMEM)],
        out_shape=[pltpu.SemaphoreType.DMA((N_OTHERS,))] * 2 + [pltpu.HBM(part.shape, part.dtype), pltpu.HBM(land.shape, land.dtype),
                                                                 jax.ShapeDtypeStruct((1, D_MODEL), F32)],
        input_output_aliases={0: 2, 1: 3},
        compiler_params=SPLIT_COPY,
    )(_hbm(part), _hbm(land))


def scatter_wait(part, land, send, recv, after, *, name):
    h = part.shape[1]

    def body(part_ref, land_ref, send_sems, recv_sems, after_ref, part_out, land_out):
        del after_ref, part_out, land_out
        x, y, c, others = _place()
        for f, chip in enumerate(others):
            cp = _rcopy(_chip_rows(part_ref, chip, 0, h), land_ref.at[f], send_sems, recv_sems, f, (*chip, c))
            cp.wait_send()
            cp.wait_recv()

    return pl.pallas_call(
        body, name=name, in_specs=[HBM, HBM, SEM, SEM, HBM], out_specs=[HBM, HBM],
        out_shape=[pltpu.HBM(part.shape, part.dtype), pltpu.HBM(land.shape, land.dtype)],
        input_output_aliases={0: 0, 1: 1},
        compiler_params=SPLIT_COPY,
    )(part, land, send, recv, after)


def join_start(buf, *, name):
    h = buf.shape[0] // 2

    def body(b_ref, send, recv, b_out, token):
        del b_out
        x, y, c, _ = _place()
        for s0, sz in _row_pieces(h, 2 * DMA_PIECES):
            rows = b_ref.at[pl.ds(c * h + s0, sz)]
            _rcopy(rows, rows, send, recv, 0, (x, y, 1 - c)).start()
        token[...] = jnp.zeros_like(token)

    return pl.pallas_call(
        body, name=name, in_specs=[HBM], out_specs=[SEM, SEM, HBM, VMEM_SPEC],
        out_shape=[_sem1(), _sem1(), pltpu.HBM(buf.shape, buf.dtype), TOKEN],
        input_output_aliases={0: 2}, compiler_params=SPLIT_COPY,
    )(_hbm(buf))


def join_wait(buf, send, recv, after, *, name):
    h = buf.shape[0] // 2

    def body(b_ref, send_sems, recv_sems, after_ref, b_out):
        del after_ref, b_out
        x, y, c, _ = _place()
        cp = _rcopy(b_ref.at[pl.ds(c * h, h)], b_ref.at[pl.ds((1 - c) * h, h)], send_sems, recv_sems, 0, (x, y, 1 - c))
        cp.wait_send()
        cp.wait_recv()

    return pl.pallas_call(
        body, name=name, in_specs=[HBM, SEM, SEM, HBM], out_specs=HBM,
        out_shape=pltpu.HBM(buf.shape, buf.dtype), input_output_aliases={0: 0}, compiler_params=SPLIT_COPY,
    )(buf, send, recv, after)


def allreduce_small(v, *, name):
    rows, n = v.shape

    def body(x_ref, sum_ref, all_ref, send_sems, recv_sems, local_sem):
        x, y, c, others = _place()
        me, sibling = (x, y, c), (x, y, 1 - c)

        def blk(px, py, pc):
            return all_ref.at[pl.ds((4 * px + 2 * py + pc) * rows, rows), :]

        def copy(k, block, to, src=None):
            return _rcopy(blk(*block) if src is None else src, blk(*block), send_sems, recv_sems, k, to)

        mine = pltpu.make_async_copy(x_ref, blk(*me), local_sem)
        mine.start()
        first = [copy(0, me, sibling, src=x_ref)]
        first += [copy(1 + f, me, (*chip, c), src=x_ref) for f, chip in enumerate(others)]
        for cp in first:
            cp.start()
        passed = [copy(4 + f, (*chip, c), sibling) for f, chip in enumerate(others)]
        for f, chip in enumerate(others):
            copy(1 + f, (*chip, c), me).wait_recv()
            passed[f].start()
        copy(0, sibling, me).wait_recv()
        for f, chip in enumerate(others):
            copy(4 + f, (*chip, 1 - c), me).wait_recv()
        for cp in first + passed:
            cp.wait_send()
        mine.wait()
        acc = all_ref[pl.ds(0, rows), :]
        for d in range(1, N_DEVICES):
            acc = acc + all_ref[pl.ds(d * rows, rows), :]
        sum_ref[...] = acc

    vm = pl.BlockSpec(memory_space=pltpu.VMEM)
    return pl.pallas_call(
        body, name=name, in_specs=[vm], out_specs=[vm, vm],
        out_shape=[jax.ShapeDtypeStruct((rows, n), F32), jax.ShapeDtypeStruct((N_DEVICES * rows, n), F32)],
        scratch_shapes=_dma_sems(7, 7) + [pltpu.SemaphoreType.DMA],
    )(v)[0]


def add_pairs(grad, theirs, where, *, name):
    h = theirs.shape[1]
    spec = pl.BlockSpec((None, h, SLAB), lambda k, w: (k, 0, 0))

    def body(w_ref, a_ref, b_ref, o_ref):
        del w_ref
        o_ref[...] = (a_ref[...].astype(F32) + b_ref[...].astype(F32)).astype(o_ref.dtype)

    return pl.pallas_call(
        body, name=name,
        grid_spec=pltpu.PrefetchScalarGridSpec(
            num_scalar_prefetch=1, grid=(N_CHIPS,),
            in_specs=[pl.BlockSpec((None, h, SLAB), lambda k, w: (k, w[1], 0)), spec], out_specs=spec),
        out_shape=jax.ShapeDtypeStruct(theirs.shape, theirs.dtype),
        compiler_params=_cparams(("parallel",)))(where, grad, theirs)


def add_chips(pair, got, where, *, name):
    h = pair.shape[1]
    tr = h // 2

    def body(w_ref, a_ref, b_ref, o_ref):
        del w_ref
        acc = a_ref[...].astype(F32)
        for f in range(3):
            acc = acc + b_ref[f].astype(F32)
        o_ref[...] = acc

    return pl.pallas_call(
        body, name=name,
        grid_spec=pltpu.PrefetchScalarGridSpec(
            num_scalar_prefetch=1, grid=(2,),
            in_specs=[pl.BlockSpec((None, tr, SLAB), lambda i, w: (w[0], i, 0)),
                      pl.BlockSpec((3, tr, SLAB), lambda i, w: (0, i, 0))],
            out_specs=pl.BlockSpec((tr, SLAB), lambda i, w: (2 * w[1] + i, 0))),
        out_shape=jax.ShapeDtypeStruct((2 * h, SLAB), F32),
        compiler_params=_cparams(("parallel",)))(where, pair, got)


DEPTH = 4
MIXER = (0, 1, 2, 0)
W_IN_COLS = (768, 320, 772)
W_IN_PAD = (768, 512, 1024)
MATS = ("up", "down", "inp", "out", "gate", "proj")
MAT_ARG = dict(up="w_up", down="w_down", inp="w_in", out="w_out", gate="w_ple_gate", proj="w_ple_proj")
GAINS = ("attn_norm", "mlp_norm", "ple_norm")
N_SMALL = 16
KINDS = ("grad_", "delta_", "new_m_", "new_v_")


def _layout(kind):
    ns_in = W_IN_PAD[kind] // SLAB
    off = 8192 + 1024 * ns_in
    lay = dict(up=(0, 1024, 4, False), down=(4096, 1024, 4, True), inp=(8192, 1024, ns_in, False),
               out=(off, 256, 4, True), gate=(off + 1024, 256, 4, True), proj=(off + 2048, 256, 1, False))
    return lay, off + 2304


def _to_slabs(w):
    k, c = w.shape
    return w.reshape(k, c // SLAB, SLAB).transpose(1, 0, 2).reshape(-1, SLAB)


def _pad_cols(w, n):
    return jnp.pad(w, ((0, 0), (0, n - w.shape[1])))


def _heads(x2d, n):
    return x2d.reshape(x2d.shape[0], n, HEAD_DIM).transpose(1, 0, 2)


def _unheads(x3d):
    n, s, _ = x3d.shape
    return x3d.transpose(1, 0, 2).reshape(s, n * HEAD_DIM)


def _chip_cols(x2d, c, cpad):
    return jnp.concatenate([_pad_cols(x2d[:, j * c:(j + 1) * c], cpad) for j in range(N_CHIPS)], axis=1)


def _unchip_cols(x2d, c, cpad):
    return jnp.concatenate([x2d[:, j * cpad:j * cpad + c] for j in range(N_CHIPS)], axis=1)


def _forget_cols(wg, t):
    off, K, _, _ = t
    cols = []
    for g in range(3 * N_HEADS * HEAD_DIM, 3 * N_HEADS * HEAD_DIM + N_HEADS):
        chip, local = divmod(g, W_IN_COLS[2])
        q, c = divmod(local, SLAB)
        cols.append(wg[chip, off + q * K:off + (q + 1) * K, c:c + 1])
    return jnp.concatenate(cols, axis=1)


def _add_res(acc, res):
    return (acc + res,)


def _relu2(acc):
    return acc, jnp.square(jnp.maximum(acc, 0.0))


def _relu2_bwd(acc, u):
    return (acc * (2.0 * jnp.maximum(u.astype(F32), 0.0)),)


def _ple_fwd(acc, x2, pp):
    return x2 + pp * _sigmoid(acc), acc


def _ple_bwd(dx, pp, gl):
    gate = _sigmoid(gl)
    return dx * gate, dx * pp * gate * (1.0 - gate)


def _layer_fwd(i, kind, x0, p_bf, wg, lay, gains, extra, tabs, mid):
    s = x0.shape[0]
    an, mn, pn = gains
    sv = dict(x0=x0)
    if kind == 0:
        proj, h1 = mm_nn(x0, wg, lay["inp"], name=f"w_in_{i}", norm_gain=an)
        a, tot = sb_fwd(proj, name=f"sb_fwd_{i}")
        sv.update(proj=proj, tot=tot)
    elif kind == 1:
        projp, h1 = mm_nn(x0, wg, lay["inp"], name=f"w_in_{i}", out_dtypes=(F32,), norm_gain=an)
        proj = _unchip_cols(projp, W_IN_COLS[1], W_IN_PAD[1])
        nq = N_HEADS * HEAD_DIM
        nqk = nq + SWA_KV_HEADS * HEAD_DIM
        qk = rope_fwd(proj[:, :nqk], tabs, name=f"rope_{i}")
        q = _heads(qk[:, :nq], N_HEADS).reshape(SWA_KV_HEADS, SWA_GROUP, s, HEAD_DIM)
        front = ((0, 0), (SWA_WINDOW, 0), (0, 0))
        kp = jnp.pad(_heads(qk[:, nq:], SWA_KV_HEADS), front)
        vp = jnp.pad(_heads(proj[:, nqk:].astype(BF16), SWA_KV_HEADS), front)
        sink = jnp.repeat(extra.reshape(SWA_KV_HEADS, SWA_GROUP), ATT_BLK, axis=1)[:, :, None]
        o4, lse = swa_fwd(q, kp, vp, sink, name=f"swa_fwd_{i}")
        a = _unheads(o4.reshape(N_HEADS, s, HEAD_DIM))
        sv.update(q=q, kp=kp, vp=vp, sink=sink, o4=o4, lse=lse)
    else:
        projp, h1 = mm_nn(x0, wg, lay["inp"], name=f"w_in_{i}", norm_gain=an)
        nqkv = 3 * N_HEADS * HEAD_DIM
        proj = _unchip_cols(projp, W_IN_COLS[2], W_IN_PAD[2])[:, :nqkv]
        fl = mm_plain(h1, _pad_cols(_forget_cols(wg, lay["inp"]), 128), name=f"w_forget_{i}")
        bp = _pad_cols(extra[None], 128)
        cum_t = fox_gate_fwd(fl, bp, name=f"gate_fwd_{i}")[:, :N_HEADS].T
        cq = cum_t[:, :, None]
        ck = cum_t.reshape(N_HEADS, s // min(ATT_BK, s), 1, min(ATT_BK, s))
        a, lse = fox_fwd(proj, cq, ck, name=f"fox_fwd_{i}")
        sv.update(proj=proj, fl=fl, bp=bp, cq=cq, ck=ck, lse=lse)
    zero = mid(a)
    if zero is not None:
        mn = mn + zero
    x1 = mm_nn(a, wg, lay["out"], name=f"w_out_{i}", epi=_add_res, extras=(x0,), out_dtypes=(F32,))[0]
    u, r, h2 = mm_nn(x1, wg, lay["up"], name=f"w_up_{i}", epi=_relu2, out_dtypes=(BF16, BF16), norm_gain=mn)
    x2 = mm_nn(r, wg, lay["down"], name=f"w_down_{i}", epi=_add_res, extras=(x1,), out_dtypes=(F32,))[0]
    pp = mm_nn(p_bf, wg, lay["proj"], name=f"w_ple_proj_{i}", out_dtypes=(F32,))[0]
    x3, gl, h3 = mm_nn(x2, wg, lay["gate"], name=f"w_ple_gate_{i}", epi=_ple_fwd, extras=(x2, pp), out_dtypes=(F32, F32),
                       norm_gain=pn)
    sv.update(h1=h1, a=a, x1=x1, h2=h2, u=u, r=r, x2=x2, h3=h3, pp=pp, gl=gl)
    return x3, sv


def _layer_bwd(i, kind, dx3, sv, p_bf, wg, lay, n_rows, gains, tabs, mid, end):
    s = dx3.shape[0]
    an, mn, pn = gains
    g = lax.empty((N_CHIPS, n_rows, SLAB), BF16)
    d_pp, d_gl = ew(_ple_bwd, [dx3, sv["pp"], sv["gl"]], [BF16, BF16], name=f"ple_bwd_{i}")
    g = mm_tn(p_bf, d_pp, g, lay["proj"], name=f"dw_ple_proj_{i}")
    g = mm_tn(sv["h3"], d_gl, g, lay["gate"], name=f"dw_ple_gate_{i}")
    d_h3 = mm_nt(d_gl, wg, lay["gate"], name=f"dx_ple_gate_{i}", out_dtypes=(F32,))[0]
    dx2, dx2b, d_pn = rms_bwd(sv["x2"], pn, d_h3, dx3, name=f"ple_norm_bwd_{i}")
    zero = mid(dx2)
    if zero is not None:
        mn = mn + zero
    g = mm_tn(sv["r"], dx2b, g, lay["down"], name=f"dw_down_{i}")
    d_u = mm_nt(dx2b, wg, lay["down"], name=f"dx_down_{i}", epi=_relu2_bwd, extras=(sv["u"],))[0]
    g = mm_tn(sv["h2"], d_u, g, lay["up"], name=f"dw_up_{i}")
    dx1, dx1b, d_mn = mm_nt(d_u, wg, lay["up"], name=f"dx_up_{i}", rms=(sv["x1"], mn, dx2))
    g = mm_tn(sv["a"], dx1b, g, lay["out"], name=f"dw_out_{i}")
    d_a = mm_nt(dx1b, wg, lay["out"], name=f"dx_out_{i}")[0]
    d_extra = None
    if kind == 0:
        d_proj = jnp.concatenate(sb_bwd(sv["proj"], sv["tot"], d_a, name=f"sb_bwd_{i}"), axis=1)
    elif kind == 1:
        do4 = _heads(d_a, N_HEADS).reshape(SWA_KV_HEADS, SWA_GROUP, s, HEAD_DIM)
        dq, dkp, dvp, dsr = swa_bwd(sv["q"], sv["kp"], sv["vp"], sv["sink"], sv["o4"], sv["lse"], do4, name=f"swa_bwd_{i}")
        dqk = jnp.concatenate([_unheads(dq.reshape(N_HEADS, s, HEAD_DIM)), _unheads(dkp[:, SWA_WINDOW:])], axis=1)
        dqk = rope_bwd(dqk, tabs, name=f"rope_bwd_{i}")
        d_proj = jnp.concatenate([dqk, _unheads(dvp[:, SWA_WINDOW:]).astype(BF16)], axis=1)
        d_proj = _chip_cols(d_proj, W_IN_COLS[1], W_IN_PAD[1])
        d_extra = jnp.sum(dsr[..., 0], axis=2).reshape(N_HEADS)
    else:
        dq, dk, dv, dcq, dck = fox_bwd(sv["proj"], sv["a"], sv["lse"], sv["cq"], sv["ck"], d_a, name=f"fox_bwd_{i}")
        dcum = _pad_cols((dcq[:, :, 0] - dck.reshape(N_HEADS, s)).T, 128)
        dfl, dbp = fox_gate_bwd(dcum, sv["fl"], sv["bp"], name=f"gate_bwd_{i}")
        d_proj = jnp.concatenate([dq, dk, dv, dfl[:, :N_HEADS].astype(BF16)], axis=1)
        d_proj = _chip_cols(d_proj, W_IN_COLS[2], W_IN_PAD[2])
        d_extra = dbp[0, :N_HEADS]
    g = mm_tn(sv["h1"], d_proj, g, lay["inp"], name=f"dw_in_{i}")
    zero = end(g)
    if zero is not None:
        an = an + zero
    dx0, _, d_an = mm_nt(d_proj, wg, lay["inp"], name=f"dx_in_{i}", rms=(sv["x0"], an, dx1))
    return dx0, g, (d_an, d_mn, d_pn), d_extra


def _small_rows(a, prefix):
    rows = [a[f"{prefix}{n}_{i}"] for i in range(DEPTH) for n in GAINS] + [a[f"{prefix}final_norm"]]
    rows += [_pad_cols(a[f"{prefix}{n}"][None], D_MODEL)[0] for n in ("sinks_1", "b_forget_2")]
    return jnp.stack(rows + [jnp.zeros((D_MODEL,), F32)])


def _train_step(a):
    x = a["x"][0]
    tabs = rope_tables(x.shape[0])
    lays = [_layout(k) for k in MIXER]

    padded = {}

    def natural(prefix, i, m):
        w = a[f"{prefix}{MAT_ARG[m]}_{i}"]
        if m != "inp" or w.shape[1] == W_IN_PAD[MIXER[i]]:
            return w
        if (prefix, i) not in padded:
            padded[prefix, i] = _pad_cols(w, W_IN_PAD[MIXER[i]])
        return padded[prefix, i]

    chip = 2 * lax.axis_index("x") + lax.axis_index("y")
    where = jnp.stack([chip, lax.axis_index("c")]).astype(jnp.int32)
    def own_block(i, zero):
        pk = jnp.concatenate([_to_slabs((natural("", i, m) + zero).astype(BF16)) for m in MATS], axis=0)
        return lax.dynamic_update_slice(lax.empty((N_CHIPS,) + pk.shape, BF16), pk[None], (chip, 0, 0))

    sends, recvs, bufs, token = allgather_start([own_block(0, 0.0)], name="allgather_start_0")
    more = allgather_start([own_block(i, token[0, 0]) for i in range(1, DEPTH)], name="allgather_start_1")
    sends, recvs, bufs, token = sends + more[0], recvs + more[1], bufs + more[2], more[3]

    gains = [tuple(a[f"{n}_{i}"][None] for n in GAINS) for i in range(DEPTH)]
    extras = [None, a["sinks_1"], a["b_forget_2"], None]
    p_bf = [a["p"][i, 0].astype(BF16) for i in range(DEPTH)]

    small_state = [_small_rows(a, prefix) for prefix in ("", "m_", "v_")]
    after = [token, *tabs, *p_bf, *small_state] + [natural(prefix, i, "inp") for prefix in ("m_", "v_") for i in (1, 2)]
    saved, wgs = [], []
    landed = allgather_wait(bufs[0], sends[0], recvs[0], after, name="allgather_wait_0")
    moving = dict(next=forward_start(landed, name="forward_start_0"))
    for i in range(DEPTH):
        send, recv, buf, zero = moving["next"]
        wgs.append(forward_wait(buf, send, recv, x if i else zero, name=f"forward_wait_{i}"))

        def mid(attn_out):
            if i + 1 == DEPTH:
                return None
            arrived = allgather_wait(bufs[i + 1], sends[i + 1], recvs[i + 1], [attn_out], name=f"allgather_wait_{i + 1}")
            moving["next"] = forward_start(arrived, name=f"forward_start_{i + 1}")
            return moving["next"][3]

        x, sv = _layer_fwd(i, MIXER[i], x, p_bf[i], wgs[i], lays[i][0], gains[i], extras[i], tabs, mid)
        saved.append(sv)
    dx, d_final, loss = loss_head(x, a["final_norm"][None], a["loss_target"][0], name="loss_head")

    def pair_and_scatter(j, swapped, after):
        send, recv, grad, land, _ = swapped
        grad, theirs = swap_wait(grad, land, send, recv, after, name=f"swap_wait_{j}")
        return scatter_start(add_pairs(grad, theirs, where, name=f"add_pairs_{j}"), name=f"scatter_start_{j}")

    def sum_and_join(j, scattered, after):
        send, recv, part, land, _ = scattered
        part, got = scatter_wait(part, land, send, recv, after, name=f"scatter_wait_{j}")
        return join_start(add_chips(part, got, where, name=f"add_chips_{j}"), name=f"join_start_{j}")

    small = [None] * N_SMALL
    small[12] = d_final[0]
    small[15] = _pad_cols(loss[:, :1], D_MODEL)[0]
    joined = [None] * DEPTH
    state = dict(swapped=None, scattered=None)
    for i in reversed(range(DEPTH)):
        an, mn, pn = gains[i]
        if state["swapped"] is not None:
            pn = pn + state["swapped"][4]

        def mid(dx2):
            if state["swapped"] is None:
                return None
            state["scattered"] = pair_and_scatter(i + 1, state["swapped"], dx2)
            return state["scattered"][4]

        def end(grad):
            state["swapped"] = swap_start(grad, name=f"swap_start_{i}")
            return state["swapped"][4]

        dx, _, d_gains, d_extra = _layer_bwd(i, MIXER[i], dx, saved[i], p_bf[i], wgs[i], lays[i][0], lays[i][1],
                                             (an, mn, pn), tabs, mid, end)
        for j in range(3):
            small[3 * i + j] = d_gains[j][0]
        if d_extra is not None:
            small[12 + MIXER[i]] = _pad_cols(d_extra[None], D_MODEL)[0]
        if state["scattered"] is not None:
            joined[i + 1] = sum_and_join(i + 1, state["scattered"], dx)
    started = pair_and_scatter(0, state["swapped"], dx)
    small = allreduce_small(jnp.stack(small), name="allreduce_small")

    out = {"loss": small[15, 0], "grad_x": dx[None]}
    res = adamw_small(small, *small_state, name="adamw_small")
    prev = started[4]
    for i in reversed(range(DEPTH)):
        if i == 0:
            joined[0] = sum_and_join(0, started, prev)
        send, recv, buf, zero = joined[i]
        gfull = join_wait(buf, send, recv, prev if i else zero, name=f"join_wait_{i}")
        for m in MATS:
            upd = adamw(gfull, lays[i][0][m], natural("", i, m), natural("m_", i, m), natural("v_", i, m), prev,
                        name=f"adamw_{MAT_ARG[m]}_{i}")
            prev = upd[1]
            cols = a[f"{MAT_ARG[m]}_{i}"].shape[1]
            for kd, r in zip(KINDS, upd):
                out[f"{kd}{MAT_ARG[m]}_{i}"] = r[:, :cols]
    for kd, r in zip(KINDS, res):
        for i in range(DEPTH):
            for j, n in enumerate(GAINS):
                out[f"{kd}{n}_{i}"] = r[3 * i + j][0]
        out[f"{kd}final_norm"] = r[12][0]
        out[f"{kd}sinks_1"] = r[13][0, :N_HEADS]
        out[f"{kd}b_forget_2"] = r[14][0, :N_HEADS]
    return out


def _weight_names():
    names = []
    for i in range(DEPTH):
        names += [f"attn_norm_{i}", f"w_in_{i}", f"w_out_{i}"] + [[], ["sinks_1"], ["b_forget_2"]][MIXER[i]]
        names += [f"mlp_norm_{i}", f"w_up_{i}", f"w_down_{i}", f"ple_norm_{i}", f"w_ple_gate_{i}", f"w_ple_proj_{i}"]
    return names + ["final_norm"]


def kernel(x, p, attn_norm_0, w_in_0, w_out_0, mlp_norm_0, w_up_0, w_down_0, ple_norm_0, w_ple_gate_0, w_ple_proj_0, attn_norm_1, w_in_1, w_out_1, sinks_1, mlp_norm_1, w_up_1, w_down_1, ple_norm_1, w_ple_gate_1, w_ple_proj_1, attn_norm_2, w_in_2, w_out_2, b_forget_2, mlp_norm_2, w_up_2, w_down_2, ple_norm_2, w_ple_gate_2, w_ple_proj_2, attn_norm_3, w_in_3, w_out_3, mlp_norm_3, w_up_3, w_down_3, ple_norm_3, w_ple_gate_3, w_ple_proj_3, final_norm, loss_target, m_attn_norm_0, m_w_in_0, m_w_out_0, m_mlp_norm_0, m_w_up_0, m_w_down_0, m_ple_norm_0, m_w_ple_gate_0, m_w_ple_proj_0, m_attn_norm_1, m_w_in_1, m_w_out_1, m_sinks_1, m_mlp_norm_1, m_w_up_1, m_w_down_1, m_ple_norm_1, m_w_ple_gate_1, m_w_ple_proj_1, m_attn_norm_2, m_w_in_2, m_w_out_2, m_b_forget_2, m_mlp_norm_2, m_w_up_2, m_w_down_2, m_ple_norm_2, m_w_ple_gate_2, m_w_ple_proj_2, m_attn_norm_3, m_w_in_3, m_w_out_3, m_mlp_norm_3, m_w_up_3, m_w_down_3, m_ple_norm_3, m_w_ple_gate_3, m_w_ple_proj_3, m_final_norm, v_attn_norm_0, v_w_in_0, v_w_out_0, v_mlp_norm_0, v_w_up_0, v_w_down_0, v_ple_norm_0, v_w_ple_gate_0, v_w_ple_proj_0, v_attn_norm_1, v_w_in_1, v_w_out_1, v_sinks_1, v_mlp_norm_1, v_w_up_1, v_w_down_1, v_ple_norm_1, v_w_ple_gate_1, v_w_ple_proj_1, v_attn_norm_2, v_w_in_2, v_w_out_2, v_b_forget_2, v_mlp_norm_2, v_w_up_2, v_w_down_2, v_ple_norm_2, v_w_ple_gate_2, v_w_ple_proj_2, v_attn_norm_3, v_w_in_3, v_w_out_3, v_mlp_norm_3, v_w_up_3, v_w_down_3, v_ple_norm_3, v_w_ple_gate_3, v_w_ple_proj_3, v_final_norm):
    out = _train_step(dict(locals()))
    return (out["loss"], out["grad_x"], *[out[kd + n] for kd in KINDS for n in _weight_names()])
```

```python
import jax
import jax.numpy as jnp
from jax import lax
from jax.experimental import pallas as pl
from jax.experimental.pallas import tpu as pltpu

F32 = jnp.float32
BF16 = jnp.bfloat16

D_MODEL = 1024
N_HEADS = 16
HEAD_DIM = 64
SWA_KV_HEADS = 2
SWA_GROUP = 8
SWA_WINDOW = 128
ROPE_THETA = 500000.0
ROPE_DIM = 16
RMS_EPS = 1e-6
NEG_INF = -1e30
ATTN_SCALE = HEAD_DIM ** -0.5
N_CHIPS = 4
N_DEVICES = 8

SLAB = 256
ATT_BLK = 128
ATT_BQ = 512
ATT_BK = 512
ROW_TILE = 256
V7X_VMEM_LIMIT = 56 * 1024 * 1024

ADAM_LR, ADAM_B1, ADAM_B2, ADAM_EPS, ADAM_WD, ADAM_STEP = 0.001, 0.9, 0.999, 1e-08, 0.01, 10


def _cparams(sem=None):
    return pltpu.CompilerParams(dimension_semantics=sem, vmem_limit_bytes=V7X_VMEM_LIMIT)


def _dot(a, b):
    return jnp.dot(a, b, preferred_element_type=F32)


def _dot_nt(a, b):
    return lax.dot_general(a, b, (((1,), (1,)), ((), ())), preferred_element_type=F32)


def _dot_tn(a, b):
    return lax.dot_general(a, b, (((0,), (0,)), ((), ())), preferred_element_type=F32)


def _row_tile(M, K):
    return min(M, 1024) if K >= 1024 else M


def _finish(epi, acc, ex, outs):
    res = epi(acc, *[e[...] for e in ex]) if epi is not None else (acc,)
    for o, r in zip(outs, res):
        o[...] = r.astype(o.dtype)


def _once(shape, index_map):
    return pl.BlockSpec(shape, index_map, pipeline_mode=pl.Buffered(1))


def mm_nn(a, wg, t, *, name, epi=None, extras=(), out_dtypes=(BF16,), norm_gain=None):
    off, K, ns, row = t
    M = a.shape[0]
    sb = off // K
    ne, no = len(extras), len(out_dtypes)
    norm = norm_gain is not None
    if row:
        tm = M if norm else _row_tile(M, K)
        nb = N_CHIPS
        grid = (M // tm, ns)
        a_shape = (tm, N_CHIPS * K)
        a_spec = (_once if norm else pl.BlockSpec)(a_shape, lambda i, q: (i, 0))
        b_specs = [pl.BlockSpec((None, K, SLAB), lambda i, q, j=j: (j, sb + q, 0)) for j in range(nb)]
        tile = pl.BlockSpec((tm, SLAB), lambda i, q: (i, q))
        n_out = ns * SLAB
    else:
        nb = ns
        grid = (N_CHIPS,)
        a_shape = (M, K)
        a_spec = (_once if norm else pl.BlockSpec)(a_shape, lambda j: (0, 0))
        b_specs = [pl.BlockSpec((None, K, SLAB), lambda j, q=q: (j, sb + q, 0)) for q in range(ns)]
        tile = pl.BlockSpec((M, ns * SLAB), lambda j: (0, j))
        n_out = N_CHIPS * ns * SLAB

    def body(a_ref, *rest):
        if norm:
            g_ref, rest, h_out, h_ref = rest[0], rest[1:-2], rest[-2], rest[-1]

            @pl.when(pl.program_id(len(grid) - 1) == 0)
            def _():
                xv = a_ref[...]
                h_ref[...] = (xv * _rstd(xv) * g_ref[...]).astype(BF16)
                h_out[...] = h_ref[...]

            a_ref = h_ref
        bs, ex, outs = rest[:nb], rest[nb:nb + ne], rest[nb + ne:]
        if row:
            acc = _dot(a_ref[:, pl.ds(0, K)], bs[0][...])
            for j in range(1, nb):
                acc = acc + _dot(a_ref[:, pl.ds(j * K, K)], bs[j][...])
            _finish(epi, acc, ex, outs)
        else:
            av = a_ref[...]
            for q in range(ns):
                cols = pl.ds(q * SLAB, SLAB)
                _finish(epi, _dot(av, bs[q][...]), [e.at[:, cols] for e in ex], [o.at[:, cols] for o in outs])

    h_spec = _once(a_shape, (lambda i, q: (i, 0)) if row else (lambda j: (0, 0)))
    return pl.pallas_call(
        body, name=name, grid=grid,
        in_specs=[a_spec] + ([pl.BlockSpec(norm_gain.shape, lambda *_: (0, 0))] if norm else []) + b_specs + [tile] * ne,
        out_specs=[tile] * no + ([h_spec] if norm else []),
        out_shape=[jax.ShapeDtypeStruct((M, n_out), d) for d in out_dtypes]
        + ([jax.ShapeDtypeStruct(a.shape, BF16)] if norm else []),
        scratch_shapes=[pltpu.VMEM(a_shape, BF16)] if norm else [],
        compiler_params=_cparams((("arbitrary" if norm else "parallel"),) * len(grid)),
    )(a, *([norm_gain] if norm else []), *([wg] * nb), *extras)


def mm_nt(dy, wg, t, *, name, epi=None, extras=(), out_dtypes=(BF16,), rms=None):
    off, K, ns, row = t
    M = dy.shape[0]
    tm = _row_tile(M, K)
    sb = off // K
    if rms is not None:
        x, gain, dres = rms
        extras, out_dtypes = (x, dres), (F32, BF16)
    ne, no = len(extras), len(out_dtypes)
    grid = (M // tm, N_CHIPS)
    b_specs = [pl.BlockSpec((None, K, SLAB), lambda i, j, q=q: (j, sb + q, 0)) for q in range(ns)]
    if row:
        dy_spec = pl.BlockSpec((tm, ns * SLAB), lambda i, j: (i, 0))
        tile = pl.BlockSpec((tm, K), lambda i, j: (i, j))
        n_out = N_CHIPS * K
        sem = ("parallel", "parallel")
    else:
        dy_spec = pl.BlockSpec((tm, ns * SLAB), lambda i, j: (i, j))
        tile = pl.BlockSpec((tm, K), lambda i, j: (i, 0))
        n_out = K
        sem = ("arbitrary" if rms is not None else "parallel", "arbitrary")
    one = pl.BlockSpec((1, K), lambda i, j: (0, 0))

    def body(dy_ref, *rest):
        if rms is not None:
            g_ref, rest, dg_ref, acc_ref = rest[0], rest[1:-2], rest[-2], rest[-1]
            rest = rest + (acc_ref,)
        bs, ex, outs = rest[:ns], rest[ns:ns + ne], rest[ns + ne:ns + ne + no]
        part = _dot_nt(dy_ref[:, pl.ds(0, SLAB)], bs[0][...])
        for q in range(1, ns):
            part = part + _dot_nt(dy_ref[:, pl.ds(q * SLAB, SLAB)], bs[q][...])
        if row:
            _finish(epi, part, ex, outs)
        else:
            acc_ref = rest[-1]
            i, j = pl.program_id(0), pl.program_id(1)

            @pl.when(j == 0)
            def _():
                acc_ref[...] = part

            @pl.when(j > 0)
            def _():
                acc_ref[...] += part

            @pl.when(j == N_CHIPS - 1)
            def _():
                if rms is None:
                    _finish(epi, acc_ref[...], ex, outs)
                else:
                    dx, dg = _rms_bwd_tile(ex[0][...], g_ref[...], acc_ref[...])
                    dx = dx + ex[1][...]
                    outs[0][...] = dx
                    outs[1][...] = dx.astype(BF16)

                    @pl.when(i == 0)
                    def _():
                        dg_ref[...] = dg

                    @pl.when(i > 0)
                    def _():
                        dg_ref[...] += dg

    has = rms is not None
    return pl.pallas_call(
        body, name=name, grid=grid,
        in_specs=[dy_spec] + ([one] if has else []) + b_specs + [tile] * ne,
        out_specs=[tile] * no + ([one] if has else []),
        out_shape=[jax.ShapeDtypeStruct((M, n_out), d) for d in out_dtypes] + ([jax.ShapeDtypeStruct((1, K), F32)] if has else []),
        scratch_shapes=[] if row else [pltpu.VMEM((tm, K), F32)],
        compiler_params=_cparams(sem),
    )(dy, *([gain] if has else []), *([wg] * ns), *extras)


def mm_plain(a, b, *, name):
    M, K = a.shape
    N = b.shape[1]
    tm = min(M, 512)

    def body(a_ref, b_ref, o_ref):
        o_ref[...] = _dot(a_ref[...], b_ref[...])

    return pl.pallas_call(
        body, name=name, grid=(M // tm,),
        in_specs=[pl.BlockSpec((tm, K), lambda i: (i, 0)), pl.BlockSpec((K, N), lambda i: (0, 0))],
        out_specs=pl.BlockSpec((tm, N), lambda i: (i, 0)), out_shape=jax.ShapeDtypeStruct((M, N), F32),
        compiler_params=_cparams(("parallel",)),
    )(a, b)


def mm_tn(x, dy, g, t, *, name):
    off, K, ns, row = t
    S = x.shape[0]
    per = ns if off % (ns * K) == 0 else 1
    grid = (N_CHIPS, ns // per)
    if row:
        x_map = lambda j, q: (0, j)
        dy_map = lambda j, q: (0, q)
    else:
        x_map = lambda j, q: (0, 0)
        dy_map = lambda j, q: (0, j * (ns // per) + q)

    def body(g_in, x_ref, dy_ref, o_ref):
        del g_in
        xt = x_ref[...].T
        for q in range(per):
            o_ref[pl.ds(q * K, K), :] = _dot(xt, dy_ref[:, pl.ds(q * SLAB, SLAB)]).astype(o_ref.dtype)

    return pl.pallas_call(
        body, name=name, grid=grid,
        in_specs=[pl.BlockSpec(memory_space=pl.ANY), pl.BlockSpec((S, K), x_map), pl.BlockSpec((S, per * SLAB), dy_map)],
        out_specs=pl.BlockSpec((None, per * K, SLAB), lambda j, q: (j, off // (per * K) + q, 0)),
        out_shape=jax.ShapeDtypeStruct(g.shape, g.dtype),
        input_output_aliases={0: 0},
        compiler_params=_cparams(("parallel", "parallel")),
    )(g, x, dy)


def ew(fn, ins, out_dtypes, *, name, bcast=()):
    S = ins[0].shape[0]
    tr = min(ROW_TILE, S)
    cols = ins[0].shape[1]
    ni, nb = len(ins), len(bcast)

    def body(*refs):
        res = fn(*[r[...] for r in refs[:ni + nb]])
        for o, r in zip(refs[ni + nb:], res):
            o[...] = r.astype(o.dtype)

    return pl.pallas_call(
        body, name=name, grid=(S // tr,),
        in_specs=[pl.BlockSpec((tr, a.shape[1]), lambda i: (i, 0)) for a in ins]
        + [pl.BlockSpec(b.shape, lambda i: (0, 0)) for b in bcast],
        out_specs=[pl.BlockSpec((tr, cols), lambda i: (i, 0)) for _ in out_dtypes],
        out_shape=[jax.ShapeDtypeStruct((S, cols), d) for d in out_dtypes],
        compiler_params=_cparams(("parallel",)),
    )(*ins, *bcast)


def _rstd(x):
    return lax.rsqrt(jnp.mean(x * x, axis=-1, keepdims=True) + RMS_EPS)


def _sigmoid(x):
    return 1.0 / (1.0 + jnp.exp(-x))


def _log_sigmoid(z):
    return jnp.minimum(z, 0.0) - jnp.log(1.0 + jnp.exp(-jnp.abs(z)))


def _rms_bwd_tile(xv, gv, dh):
    rstd = _rstd(xv)
    xhat = xv * rstd
    gd = dh * gv
    dx = rstd * (gd - xhat * jnp.mean(xhat * gd, axis=-1, keepdims=True))
    return dx, jnp.sum(dh * xhat, axis=0, keepdims=True)


def rms_bwd(x, g, dh, dres, *, name):
    S, D = x.shape
    tr = min(ROW_TILE, S)

    def body(x_ref, g_ref, dh_ref, dres_ref, dx_ref, dxb_ref, dg_ref):
        i = pl.program_id(0)
        dx, dg = _rms_bwd_tile(x_ref[...], g_ref[...], dh_ref[...])
        dx = dx + dres_ref[...]
        dx_ref[...] = dx
        dxb_ref[...] = dx.astype(BF16)

        @pl.when(i == 0)
        def _():
            dg_ref[...] = dg

        @pl.when(i > 0)
        def _():
            dg_ref[...] += dg

    row = pl.BlockSpec((tr, D), lambda i: (i, 0))
    one = pl.BlockSpec((1, D), lambda i: (0, 0))
    return pl.pallas_call(
        body, name=name, grid=(S // tr,),
        in_specs=[row, one, row, row], out_specs=[row, row, one],
        out_shape=[jax.ShapeDtypeStruct((S, D), F32), jax.ShapeDtypeStruct((S, D), BF16),
                   jax.ShapeDtypeStruct((1, D), F32)],
        compiler_params=_cparams(("arbitrary",)),
    )(x, g, dh, dres)


def loss_head(x, g, target, *, name):
    S, D = x.shape
    tr = min(ROW_TILE, S)

    def body(x_ref, g_ref, t_ref, dx_ref, dg_ref, loss_ref):
        i = pl.program_id(0)
        xv, gv = x_ref[...], g_ref[...]
        err = xv * _rstd(xv) * gv - t_ref[...]
        part = 0.5 * jnp.sum(jnp.mean(err * err, axis=-1, keepdims=True), axis=0, keepdims=True)
        dx, dg = _rms_bwd_tile(xv, gv, err * (1.0 / D))
        dx_ref[...] = dx
        part = jnp.broadcast_to(part, loss_ref.shape)

        @pl.when(i == 0)
        def _():
            dg_ref[...] = dg
            loss_ref[...] = part

        @pl.when(i > 0)
        def _():
            dg_ref[...] += dg
            loss_ref[...] += part

    row = pl.BlockSpec((tr, D), lambda i: (i, 0))
    one = pl.BlockSpec((1, D), lambda i: (0, 0))
    return pl.pallas_call(
        body, name=name, grid=(S // tr,),
        in_specs=[row, one, row], out_specs=[row, one, pl.BlockSpec((1, 128), lambda i: (0, 0))],
        out_shape=[jax.ShapeDtypeStruct((S, D), F32), jax.ShapeDtypeStruct((1, D), F32),
                   jax.ShapeDtypeStruct((1, 128), F32)],
        compiler_params=_cparams(("arbitrary",)),
    )(x, g, target)


def rope_tables(S):
    half = ROPE_DIM // 2
    inv_freq = ROPE_THETA ** (-jnp.arange(half, dtype=F32) / half)
    ang = jnp.arange(S, dtype=F32)[:, None] * inv_freq[None, :]
    cos, sin = jnp.cos(ang), jnp.sin(ang)
    z = jnp.zeros((S, HEAD_DIM - ROPE_DIM), F32)
    zh = jnp.zeros((S, half), F32)
    c = jnp.concatenate([cos, cos, jnp.ones_like(z)], axis=1)
    sa = jnp.concatenate([zh, sin, z], axis=1)
    sb = jnp.concatenate([-sin, zh, z], axis=1)
    return [jnp.concatenate([t, t], axis=1) for t in (c, sa, sb)]


def _wide(t, n):
    return jnp.tile(t, (1, n // t.shape[1]))


def rope_fwd(xqk, tables, *, name):
    n, half = xqk.shape[1], ROPE_DIM // 2

    def fn(x, c, sa, sb):
        return (x * _wide(c, n) + pltpu.roll(x, half, 1) * _wide(sa, n) + pltpu.roll(x, n - half, 1) * _wide(sb, n),)

    return ew(fn, [xqk] + list(tables), [BF16], name=name)[0]


def rope_bwd(dy, tables, *, name):
    n, half = dy.shape[1], ROPE_DIM // 2

    def fn(d, c, sa, sb):
        return (d * _wide(c, n) + pltpu.roll(d * _wide(sa, n), n - half, 1) + pltpu.roll(d * _wide(sb, n), half, 1),)

    return ew(fn, [dy] + list(tables), [BF16], name=name)[0]


def _split3(x):
    h1 = x.astype(BF16)
    r1 = x - h1.astype(F32)
    h2 = r1.astype(BF16)
    return h1, h2, (r1 - h2.astype(F32)).astype(BF16)


def _tri(n, cmp):
    r = lax.broadcasted_iota(jnp.int32, (n, n), 0)
    c = lax.broadcasted_iota(jnp.int32, (n, n), 1)
    return cmp(r, c).astype(BF16)


def fox_gate_fwd(fl, b, *, name):
    S, W = fl.shape
    tr = min(ROW_TILE, S)

    def body(fl_ref, b_ref, cum_ref, carry):
        i = pl.program_id(0)

        @pl.when(i == 0)
        def _():
            carry[...] = jnp.zeros_like(carry)

        lower = _tri(tr, lambda r, c: r >= c)
        cs = carry[...]
        for piece in _split3(_log_sigmoid(fl_ref[...] + b_ref[...])):
            cs = cs + _dot(lower, piece)
        cum_ref[...] = cs
        carry[...] = cs[tr - 1:tr, :]

    return pl.pallas_call(
        body, name=name, grid=(S // tr,),
        in_specs=[pl.BlockSpec((tr, W), lambda i: (i, 0)), pl.BlockSpec((1, W), lambda i: (0, 0))],
        out_specs=pl.BlockSpec((tr, W), lambda i: (i, 0)),
        out_shape=jax.ShapeDtypeStruct((S, W), F32),
        scratch_shapes=[pltpu.VMEM((1, W), F32)],
        compiler_params=_cparams(("arbitrary",)),
    )(fl, b)


def fox_gate_bwd(dcum, fl, b, *, name):
    S, W = fl.shape
    tr = min(ROW_TILE, S)
    nb = S // tr

    def body(dc_ref, fl_ref, b_ref, dfl_ref, db_ref, carry):
        i = pl.program_id(0)

        @pl.when(i == 0)
        def _():
            carry[...] = jnp.zeros_like(carry)

        upper = _tri(tr, lambda r, c: r <= c)
        cs = carry[...]
        for piece in _split3(dc_ref[...]):
            cs = cs + _dot(upper, piece)
        carry[...] = cs[0:1, :]
        dfl = cs * _sigmoid(-(fl_ref[...] + b_ref[...]))
        dfl_ref[...] = dfl
        db = jnp.sum(dfl, axis=0, keepdims=True)

        @pl.when(i == 0)
        def _():
            db_ref[...] = db

        @pl.when(i > 0)
        def _():
            db_ref[...] += db

    rev = pl.BlockSpec((tr, W), lambda i: (nb - 1 - i, 0))
    one = pl.BlockSpec((1, W), lambda i: (0, 0))
    return pl.pallas_call(
        body, name=name, grid=(nb,),
        in_specs=[rev, rev, one], out_specs=[rev, one],
        out_shape=[jax.ShapeDtypeStruct((S, W), F32), jax.ShapeDtypeStruct((1, W), F32)],
        scratch_shapes=[pltpu.VMEM((1, W), F32)],
        compiler_params=_cparams(("arbitrary",)),
    )(dcum, fl, b)


def _blk_iota(tq, tk):
    return (lax.broadcasted_iota(jnp.int32, (tq, tk), 0), lax.broadcasted_iota(jnp.int32, (tq, tk), 1))


def _cs(xb, tri):
    return _dot(xb, tri)


def _rowsum(xb):
    return jnp.sum(xb.astype(F32), axis=1, keepdims=True)


def _sb_block(qs, k, cmr, shift):
    z = _dot_nt(qs, k)
    lb = jnp.minimum(z, 0.0) - jnp.log(1.0 + jnp.exp(-jnp.abs(z)))
    if cmr is None:
        return lb, (lb - z).astype(BF16), None
    strict = cmr < shift
    lom = jnp.where(strict, lb - z, 0.0).astype(BF16)
    return lb, lom, strict


def _keep(mask, x):
    return x if mask is None else jnp.where(mask, x, 0.0)


def _att_tiles(S):
    return min(ATT_BQ, S), min(ATT_BK, S)


PAIR = 2 * HEAD_DIM
N_PAIRS = N_HEADS // 2


def _pair_specs(S, tq):
    cols = D_MODEL // PAIR
    qspec = pl.BlockSpec((tq, PAIR), lambda p, i: (i, p))
    kspec = pl.BlockSpec((S, PAIR), lambda p, i: (0, cols + p))
    vspec = pl.BlockSpec((S, PAIR), lambda p, i: (0, 2 * cols + p))
    kvout = pl.BlockSpec((S, PAIR), lambda p, i: (0, p))
    vec = pl.BlockSpec((2, tq, 1), lambda p, i: (p, i, 0))
    return qspec, kspec, vspec, kvout, vec


def _head_lanes(h):
    lane = lax.broadcasted_iota(jnp.int32, (1, PAIR), 1)
    return (lane >= h * HEAD_DIM) & (lane < (h + 1) * HEAD_DIM)


def _only(sel, x):
    return jnp.where(sel, x, jnp.zeros_like(x))


def sb_fwd(proj, *, name):
    S = proj.shape[0]
    tq, tk = _att_tiles(S)
    qspec, kspec, vspec, _, vec = _pair_specs(S, tq)

    def body(q_ref, k_ref, v_ref, o_ref, t_ref):
        i = pl.program_id(1)
        row, col = _blk_iota(tq, tk)
        cmr = col - row
        below = _tri(tk, lambda r, c: r > c)
        nkb = (i + 1) * (tq // tk)
        out = []
        for h in range(2):
            sel = _head_lanes(h)
            qs = _only(sel, q_ref[...] * ATTN_SCALE)

            def step(n, carry, masked):
                r_sum, acc = carry
                kb = nkb - 1 - n
                ks = pl.multiple_of(kb * tk, tk)
                lb, lom, strict = _sb_block(qs, k_ref[pl.ds(ks, tk), :], cmr if masked else None, i * tq - kb * tk)
                w = _keep(strict, jnp.exp(lb + _cs(lom, below) + r_sum))
                acc = acc + _dot(w.astype(BF16), _only(sel, v_ref[pl.ds(ks, tk), :]))
                return r_sum + _rowsum(lom), acc

            nd = tq // tk
            carry = lax.fori_loop(0, nd, lambda n, c: step(n, c, True), (jnp.zeros((tq, 1), F32), jnp.zeros((tq, PAIR), F32)))
            r_sum, acc = lax.fori_loop(nd, nkb, lambda n, c: step(n, c, False), carry)
            t_ref[h] = r_sum
            out.append(acc)
        o_ref[...] = (out[0] + out[1]).astype(o_ref.dtype)

    return pl.pallas_call(
        body, name=name, grid=(N_PAIRS, S // tq),
        in_specs=[qspec, kspec, vspec], out_specs=[qspec, vec],
        out_shape=[jax.ShapeDtypeStruct((S, D_MODEL), BF16), jax.ShapeDtypeStruct((N_HEADS, S, 1), F32)],
        compiler_params=_cparams(("parallel", "arbitrary")),
    )(proj, proj, proj)


def sb_bwd(proj, tot, do, *, name):
    S = proj.shape[0]
    tq, tk = _att_tiles(S)
    qspec, kspec, vspec, kvout, vec = _pair_specs(S, tq)

    def body(q_ref, k_ref, v_ref, t_ref, do_ref, dq_ref, dk_out, dv_out, dk_ref, dv_ref):
        i = pl.program_id(1)

        @pl.when(i == 0)
        def _():
            dk_ref[...] = jnp.zeros_like(dk_ref)
            dv_ref[...] = jnp.zeros_like(dv_ref)

        row, col = _blk_iota(tq, tk)
        cmr = col - row
        upto = _tri(tk, lambda r, c: r <= c)
        before = _tri(tk, lambda r, c: r < c)
        out = []
        for h in range(2):
            sel = _head_lanes(h)
            qs, dov, t_all = _only(sel, q_ref[...] * ATTN_SCALE), _only(sel, do_ref[...]), t_ref[h]

            def step(kb, carry, masked):
                p_sum, e_sum, dq = carry
                ks = pl.multiple_of(kb * tk, tk)
                kv = k_ref[pl.ds(ks, tk), :]
                lb, lom, strict = _sb_block(qs, kv, cmr if masked else None, i * tq - kb * tk)
                tail = t_all - p_sum - _cs(lom, upto)
                w = _keep(strict, jnp.exp(lb + tail))
                e = _dot_nt(dov, v_ref[pl.ds(ks, tk), :]) * w
                eb = e.astype(BF16)
                e_before = e_sum + _cs(eb, before)
                beta = jnp.exp(lb)
                dzb = _keep(strict, e - (e + e_before) * beta).astype(BF16)
                dk_ref[pl.ds(ks, tk), :] += _dot_tn(dzb, qs)
                dv_ref[pl.ds(ks, tk), :] += _dot_tn(w.astype(BF16), dov)
                return p_sum + _rowsum(lom), e_sum + _rowsum(eb), dq + _dot(dzb, _only(sel, kv))

            zero = jnp.zeros((tq, 1), F32)
            nlow = i * (tq // tk)
            carry = lax.fori_loop(0, nlow, lambda kb, c: step(kb, c, False), (zero, zero, jnp.zeros((tq, PAIR), F32)))
            out.append(lax.fori_loop(nlow, nlow + tq // tk, lambda kb, c: step(kb, c, True), carry)[2])
        dq_ref[...] = ((out[0] + out[1]) * ATTN_SCALE).astype(dq_ref.dtype)

        @pl.when(i == S // tq - 1)
        def _():
            dk_out[...] = dk_ref[...].astype(dk_out.dtype)
            dv_out[...] = dv_ref[...].astype(dv_out.dtype)

    full = jax.ShapeDtypeStruct((S, D_MODEL), BF16)
    return pl.pallas_call(
        body, name=name, grid=(N_PAIRS, S // tq),
        in_specs=[qspec, kspec, vspec, vec, qspec], out_specs=[qspec, kvout, kvout],
        out_shape=[full, full, full],
        scratch_shapes=[pltpu.VMEM((S, PAIR), F32)] * 2,
        compiler_params=_cparams(("parallel", "arbitrary")),
    )(proj, proj, proj, tot, do)


def _fox_logits(qs, k, cq, ck, cmr, shift):
    s = _dot_nt(qs, k) + cq - ck
    if cmr is None:
        return s, None
    causal = cmr <= shift
    return jnp.where(causal, s, NEG_INF), causal


def fox_fwd(proj, cq, ck, *, name):
    S = proj.shape[0]
    tq, tk = _att_tiles(S)
    qspec, kspec, vspec, _, vec = _pair_specs(S, tq)
    ckspec = pl.BlockSpec((2, S // tk, 1, tk), lambda p, i: (p, 0, 0, 0))

    def body(q_ref, k_ref, v_ref, cq_ref, ck_ref, o_ref, lse_ref):
        i = pl.program_id(1)
        row, col = _blk_iota(tq, tk)
        cmr = col - row
        out = []
        for h in range(2):
            sel = _head_lanes(h)
            qs, cqv = _only(sel, q_ref[...] * ATTN_SCALE), cq_ref[h]

            def step(kb, carry, masked):
                m, l, acc = carry
                ks = pl.multiple_of(kb * tk, tk)
                s, _ = _fox_logits(qs, k_ref[pl.ds(ks, tk), :], cqv, ck_ref[h, kb], cmr if masked else None, i * tq - kb * tk)
                m_new = jnp.maximum(m, jnp.max(s, axis=1, keepdims=True))
                alpha = jnp.exp(m - m_new)
                p = jnp.exp(s - m_new)
                l = alpha * l + jnp.sum(p, axis=1, keepdims=True)
                acc = alpha * acc + _dot(p.astype(BF16), _only(sel, v_ref[pl.ds(ks, tk), :]))
                return m_new, l, acc

            m, l, acc = lax.fori_loop(0, (i + 1) * (tq // tk), lambda kb, c: step(kb, c, True),
                                      (jnp.full((tq, 1), NEG_INF, F32), jnp.zeros((tq, 1), F32), jnp.zeros((tq, PAIR), F32)))
            lse_ref[h] = m + jnp.log(l)
            out.append(acc / l)
        o_ref[...] = (out[0] + out[1]).astype(o_ref.dtype)

    return pl.pallas_call(
        body, name=name, grid=(N_PAIRS, S // tq),
        in_specs=[qspec, kspec, vspec, vec, ckspec], out_specs=[qspec, vec],
        out_shape=[jax.ShapeDtypeStruct((S, D_MODEL), BF16), jax.ShapeDtypeStruct((N_HEADS, S, 1), F32)],
        compiler_params=_cparams(("parallel", "arbitrary")),
    )(proj, proj, proj, cq, ck)


def fox_bwd(proj, o, lse, cq, ck, do, *, name):
    S = proj.shape[0]
    tq, tk = _att_tiles(S)
    qspec, kspec, vspec, kvout, vec = _pair_specs(S, tq)
    ckspec = pl.BlockSpec((2, S // tk, 1, tk), lambda p, i: (p, 0, 0, 0))

    def body(q_ref, k_ref, v_ref, o_ref, lse_ref, cq_ref, ck_ref, do_ref, dq_ref, dk_out, dv_out, dcq_ref, dck_ref,
             dk_ref, dv_ref):
        i = pl.program_id(1)

        @pl.when(i == 0)
        def _():
            dk_ref[...] = jnp.zeros_like(dk_ref)
            dv_ref[...] = jnp.zeros_like(dv_ref)
            dck_ref[...] = jnp.zeros_like(dck_ref)

        row, col = _blk_iota(tq, tk)
        cmr = col - row
        out = []
        for h in range(2):
            sel = _head_lanes(h)
            qs, dov, cqv, lsev = _only(sel, q_ref[...] * ATTN_SCALE), _only(sel, do_ref[...]), cq_ref[h], lse_ref[h]
            delta = jnp.sum(dov.astype(F32) * o_ref[...].astype(F32), axis=1, keepdims=True)

            def step(kb, carry, masked):
                dq, dcq = carry
                ks = pl.multiple_of(kb * tk, tk)
                kv = k_ref[pl.ds(ks, tk), :]
                s, causal = _fox_logits(qs, kv, cqv, ck_ref[h, kb], cmr if masked else None, i * tq - kb * tk)
                p = _keep(causal, jnp.exp(s - lsev))
                ds = p * (_dot_nt(dov, v_ref[pl.ds(ks, tk), :]) - delta)
                dck_ref[h, kb] += jnp.sum(ds, axis=0, keepdims=True)
                dsb = ds.astype(BF16)
                dk_ref[pl.ds(ks, tk), :] += _dot_tn(dsb, qs)
                dv_ref[pl.ds(ks, tk), :] += _dot_tn(p.astype(BF16), dov)
                return dq + _dot(dsb, _only(sel, kv)), dcq + jnp.sum(ds, axis=1, keepdims=True)

            nlow = i * (tq // tk)
            carry = lax.fori_loop(0, nlow, lambda kb, c: step(kb, c, False),
                                  (jnp.zeros((tq, PAIR), F32), jnp.zeros((tq, 1), F32)))
            dq, dcq = lax.fori_loop(nlow, nlow + tq // tk, lambda kb, c: step(kb, c, True), carry)
            dcq_ref[h] = dcq
            out.append(dq)
        dq_ref[...] = ((out[0] + out[1]) * ATTN_SCALE).astype(dq_ref.dtype)

        @pl.when(i == S // tq - 1)
        def _():
            dk_out[...] = dk_ref[...].astype(dk_out.dtype)
            dv_out[...] = dv_ref[...].astype(dv_out.dtype)

    full = jax.ShapeDtypeStruct((S, D_MODEL), BF16)
    return pl.pallas_call(
        body, name=name, grid=(N_PAIRS, S // tq),
        in_specs=[qspec, kspec, vspec, qspec, vec, vec, ckspec, qspec],
        out_specs=[qspec, kvout, kvout, vec, ckspec],
        out_shape=[full, full, full, jax.ShapeDtypeStruct((N_HEADS, S, 1), F32),
                   jax.ShapeDtypeStruct((N_HEADS, S // tk, 1, tk), F32)],
        scratch_shapes=[pltpu.VMEM((S, PAIR), F32)] * 2,
        compiler_params=_cparams(("parallel", "arbitrary")),
    )(proj, proj, proj, o, lse, cq, ck, do)


def _swa_specs(S, tq):
    qspec = pl.BlockSpec((None, SWA_GROUP, tq, HEAD_DIM), lambda g, i: (g, 0, i, 0))
    kvspec = pl.BlockSpec((None, S + SWA_WINDOW, HEAD_DIM), lambda g, i: (g, 0, 0))
    vec = pl.BlockSpec((None, SWA_GROUP, tq, 1), lambda g, i: (g, 0, i, 0))
    sink = pl.BlockSpec((None, SWA_GROUP * tq, 1), lambda g, i: (g, 0, 0))
    return qspec, kvspec, vec, sink


def _swa_logits(q2, kw, i, tq):
    rows = q2.shape[0]
    r = lax.broadcasted_iota(jnp.int32, (rows, 2 * tq), 0)
    c = lax.broadcasted_iota(jnp.int32, (rows, 2 * tq), 1)
    diff = (r & (tq - 1)) + tq - c
    ok = (diff >= 0) & (diff < SWA_WINDOW) & (c + (i - 1) * tq >= 0)
    return jnp.where(ok, _dot_nt(q2, kw) * ATTN_SCALE, NEG_INF), ok


def swa_fwd(q, kp, vp, sink, *, name):
    _, G, S, _ = q.shape
    tq = ATT_BLK
    qspec, kvspec, vec, sinkspec = _swa_specs(S, tq)

    def body(q_ref, k_ref, v_ref, s_ref, o_ref, lse_ref):
        i = pl.program_id(1)
        q2 = q_ref[...].reshape(G * tq, HEAD_DIM)
        ws = pl.multiple_of(i * tq, tq)
        logits, _ = _swa_logits(q2, k_ref[pl.ds(ws, 2 * tq), :], i, tq)
        sk = s_ref[...]
        m = jnp.maximum(jnp.max(logits, axis=1, keepdims=True), sk)
        e = jnp.exp(logits - m)
        den = jnp.sum(e, axis=1, keepdims=True) + jnp.exp(sk - m)
        o = _dot((e / den).astype(BF16), v_ref[pl.ds(ws, 2 * tq), :])
        o_ref[...] = o.reshape(G, tq, HEAD_DIM).astype(o_ref.dtype)
        lse_ref[...] = (m + jnp.log(den)).reshape(G, tq, 1)

    return pl.pallas_call(
        body, name=name, grid=(SWA_KV_HEADS, S // tq),
        in_specs=[qspec, kvspec, kvspec, sinkspec], out_specs=[qspec, vec],
        out_shape=[jax.ShapeDtypeStruct(q.shape, BF16), jax.ShapeDtypeStruct((SWA_KV_HEADS, G, S, 1), F32)],
        compiler_params=_cparams(("parallel", "arbitrary")),
    )(q, kp, vp, sink)


def swa_bwd(q, kp, vp, sink, o, lse, do, *, name):
    _, G, S, _ = q.shape
    tq = ATT_BLK
    qspec, kvspec, vec, sinkspec = _swa_specs(S, tq)

    def body(q_ref, k_ref, v_ref, s_ref, o_ref, lse_ref, do_ref, dq_ref, dk_ref, dv_ref, dsink_ref):
        i = pl.program_id(1)

        @pl.when(i == 0)
        def _():
            dk_ref[...] = jnp.zeros_like(dk_ref)
            dv_ref[...] = jnp.zeros_like(dv_ref)

        q2 = q_ref[...].reshape(G * tq, HEAD_DIM)
        do2 = do_ref[...].reshape(G * tq, HEAD_DIM)
        o2 = o_ref[...].reshape(G * tq, HEAD_DIM)
        lse2 = lse_ref[...].reshape(G * tq, 1)
        ws = pl.multiple_of(i * tq, tq)
        kw = k_ref[pl.ds(ws, 2 * tq), :]
        vw = v_ref[pl.ds(ws, 2 * tq), :]
        logits, ok = _swa_logits(q2, kw, i, tq)
        p = jnp.where(ok, jnp.exp(logits - lse2), 0.0)
        delta = jnp.sum(do2.astype(F32) * o2.astype(F32), axis=1, keepdims=True)
        ds = p * (_dot_nt(do2, vw) - delta)
        dsb = ds.astype(BF16)
        dq_ref[...] = (_dot(dsb, kw) * ATTN_SCALE).reshape(G, tq, HEAD_DIM)
        dk_ref[pl.ds(ws, 2 * tq), :] += _dot_tn(dsb, q2) * ATTN_SCALE
        dv_ref[pl.ds(ws, 2 * tq), :] += _dot_tn(p.astype(BF16), do2)
        dsink_ref[...] = (-jnp.exp(s_ref[...] - lse2) * delta).reshape(G, tq, 1)

    kvshape = jax.ShapeDtypeStruct(kp.shape, F32)
    return pl.pallas_call(
        body, name=name, grid=(SWA_KV_HEADS, S // tq),
        in_specs=[qspec, kvspec, kvspec, sinkspec, qspec, vec, qspec],
        out_specs=[qspec, kvspec, kvspec, vec],
        out_shape=[jax.ShapeDtypeStruct(q.shape, F32), kvshape, kvshape,
                   jax.ShapeDtypeStruct((SWA_KV_HEADS, G, S, 1), F32)],
        compiler_params=_cparams(("parallel", "arbitrary")),
    )(q, kp, vp, sink, o, lse, do)


def _adamw_tile(w, g, m, v):
    m = ADAM_B1 * m + (1.0 - ADAM_B1) * g
    v = ADAM_B2 * v + (1.0 - ADAM_B2) * (g * g)
    m_hat = m / (1.0 - ADAM_B1 ** ADAM_STEP)
    v_hat = v / (1.0 - ADAM_B2 ** ADAM_STEP)
    delta = -ADAM_LR * (m_hat / (jnp.sqrt(v_hat) + ADAM_EPS) + ADAM_WD * w)
    return g, delta, m, v


def adamw(gfull, t, w, m, v, after, *, name):
    off, K, ns, _ = t
    sb = off // K
    nat = pl.BlockSpec((K, SLAB), lambda q: (0, q))

    def body(g_ref, w_ref, m_ref, v_ref, after_ref, *outs):
        del after_ref
        for o, r in zip(outs, _adamw_tile(w_ref[...], g_ref[...], m_ref[...], v_ref[...])):
            o[...] = r

    return pl.pallas_call(
        body, name=name, grid=(ns,),
        in_specs=[pl.BlockSpec((K, SLAB), lambda q: (sb + q, 0)), nat, nat, nat, HBM],
        out_specs=[nat] * 4, out_shape=[jax.ShapeDtypeStruct(w.shape, F32)] * 4,
        compiler_params=_cparams(("parallel",)),
    )(gfull, w, m, v, after)


def adamw_small(g, w, m, v, *, name):
    rows, d = w.shape

    def body(g_ref, w_ref, m_ref, v_ref, *outs):
        for j in range(rows):
            one = pl.ds(j, 1)
            for k, r in enumerate(_adamw_tile(w_ref[one, :], g_ref[one, :], m_ref[one, :], v_ref[one, :])):
                outs[k * rows + j][...] = r

    flat = pl.pallas_call(body, name=name, out_shape=[jax.ShapeDtypeStruct((1, d), F32)] * (4 * rows))(g, w, m, v)
    return [flat[k * rows:(k + 1) * rows] for k in range(4)]


MESH = pl.DeviceIdType.MESH
HBM = pl.BlockSpec(memory_space=pl.ANY)


def _place():
    x, y, c = lax.axis_index("x"), lax.axis_index("y"), lax.axis_index("c")
    others = [(1 - x, y), (x, 1 - y), (1 - x, 1 - y)]
    return x, y, c, others


def _rcopy(src, dst, send_sems, recv_sems, k, to):
    return pltpu.make_async_remote_copy(src_ref=src, dst_ref=dst, send_sem=send_sems.at[k], recv_sem=recv_sems.at[k],
                                        device_id=to, device_id_type=MESH)


def _dma_sems(*counts):
    return [pltpu.SemaphoreType.DMA((n,)) for n in counts]


DMA_UNIT_ROWS = 128
DMA_PIECES = 4


def _row_pieces(h, n):
    units = h // DMA_UNIT_ROWS
    n = min(n, units)
    base, extra = divmod(units, n)
    sizes = [(base + (k < extra)) * DMA_UNIT_ROWS for k in range(n)]
    return [(sum(sizes[:k]), sizes[k]) for k in range(n)]


SEM = pl.BlockSpec(memory_space=pltpu.SEMAPHORE)
SPLIT_COPY = pltpu.CompilerParams(has_side_effects=pltpu.SideEffectType.DATAFLOW_SIDE_EFFECTING)
N_OTHERS = 3


def _hbm(a):
    return pltpu.with_memory_space_constraint(a, pltpu.HBM)


def _chip_rows(buf, chip, s0, sz):
    return buf.at[2 * chip[0] + chip[1], pl.ds(s0, sz)]


def allgather_start(bufs, *, name):
    n = len(bufs)

    def body(*refs):
        ins, send, recv, token = refs[:n], refs[n:2 * n], refs[2 * n:3 * n], refs[4 * n]
        x, y, c, others = _place()
        for i in range(n):
            h = bufs[i].shape[1] // 2
            for f, chip in enumerate(others):
                for s0, sz in _row_pieces(h, DMA_PIECES):
                    mine = _chip_rows(ins[i], (x, y), c * h + s0, sz)
                    _rcopy(mine, mine, send[i], recv[i], f, (*chip, c)).start()
        token[...] = jnp.zeros_like(token)

    res = pl.pallas_call(
        body, name=name, in_specs=[HBM] * n,
        out_specs=[SEM] * (2 * n) + [HBM] * n + [pl.BlockSpec(memory_space=pltpu.VMEM)],
        out_shape=[pltpu.SemaphoreType.DMA((N_OTHERS,))] * (2 * n) + [pltpu.HBM(b.shape, b.dtype) for b in bufs]
        + [jax.ShapeDtypeStruct((1, D_MODEL), F32)],
        input_output_aliases={i: 2 * n + i for i in range(n)},
        compiler_params=SPLIT_COPY,
    )(*[_hbm(b) for b in bufs])
    return res[:n], res[n:2 * n], res[2 * n:3 * n], res[3 * n]


def allgather_wait(buf, send, recv, after, *, name):
    h = buf.shape[1] // 2
    after = list(after)

    def body(buf_ref, send_sems, recv_sems, *rest):
        del rest
        x, y, c, others = _place()
        for f, chip in enumerate(others):
            mine = _chip_rows(buf_ref, (x, y), c * h, h)
            theirs = _chip_rows(buf_ref, chip, c * h, h)
            cp = _rcopy(mine, theirs, send_sems, recv_sems, f, (*chip, c))
            cp.wait_send()
            cp.wait_recv()

    return pl.pallas_call(
        body, name=name, in_specs=[HBM, SEM, SEM] + [HBM] * len(after), out_specs=HBM,
        out_shape=pltpu.HBM(buf.shape, buf.dtype), input_output_aliases={0: 0},
        compiler_params=SPLIT_COPY,
    )(buf, send, recv, *after)


def forward_start(buf, *, name):
    h = buf.shape[1] // 2

    def body(b_ref, send, recv, b_out, token):
        del b_out
        x, y, c, others = _place()
        for f, chip in enumerate(others):
            for s0, sz in _row_pieces(h, DMA_PIECES):
                rows = _chip_rows(b_ref, chip, c * h + s0, sz)
                _rcopy(rows, rows, send, recv, f, (x, y, 1 - c)).start()
        token[...] = jnp.zeros_like(token)

    return pl.pallas_call(
        body, name=name, in_specs=[HBM],
        out_specs=[SEM, SEM, HBM, pl.BlockSpec(memory_space=pltpu.VMEM)],
        out_shape=[pltpu.SemaphoreType.DMA((N_OTHERS,))] * 2 + [pltpu.HBM(buf.shape, buf.dtype),
                                                                 jax.ShapeDtypeStruct((1, D_MODEL), F32)],
        input_output_aliases={0: 2}, compiler_params=SPLIT_COPY,
    )(_hbm(buf))


def forward_wait(buf, send, recv, after, *, name):
    h = buf.shape[1] // 2

    def body(b_ref, send_sems, recv_sems, after_ref, b_out):
        del after_ref, b_out
        x, y, c, others = _place()
        for f, chip in enumerate(others):
            cp = _rcopy(_chip_rows(b_ref, chip, c * h, h), _chip_rows(b_ref, chip, (1 - c) * h, h),
                        send_sems, recv_sems, f, (x, y, 1 - c))
            cp.wait_send()
            cp.wait_recv()

    return pl.pallas_call(
        body, name=name, in_specs=[HBM, SEM, SEM, HBM], out_specs=HBM,
        out_shape=pltpu.HBM(buf.shape, buf.dtype), input_output_aliases={0: 0}, compiler_params=SPLIT_COPY,
    )(buf, send, recv, after)


def _sem1():
    return pltpu.SemaphoreType.DMA((1,))


TOKEN = jax.ShapeDtypeStruct((1, D_MODEL), F32)
VMEM_SPEC = pl.BlockSpec(memory_space=pltpu.VMEM)


def swap_start(grad, *, name):
    h = grad.shape[1] // 2

    def body(g_ref, land_ref, send, recv, g_out, land_out, token):
        del g_out, land_out
        x, y, c, _ = _place()
        for k in range(N_CHIPS):
            for s0, sz in _row_pieces(h, DMA_PIECES):
                _rcopy(g_ref.at[k, pl.ds((1 - c) * h + s0, sz)], land_ref.at[k, pl.ds(s0, sz)], send, recv, 0, (x, y, 1 - c)).start()
        token[...] = jnp.zeros_like(token)

    land = lax.empty((N_CHIPS, h, SLAB), grad.dtype)
    return pl.pallas_call(
        body, name=name, in_specs=[HBM, HBM], out_specs=[SEM, SEM, HBM, HBM, VMEM_SPEC],
        out_shape=[_sem1(), _sem1(), pltpu.HBM(grad.shape, grad.dtype), pltpu.HBM(land.shape, land.dtype), TOKEN],
        input_output_aliases={0: 2, 1: 3}, compiler_params=SPLIT_COPY,
    )(_hbm(grad), _hbm(land))


def swap_wait(grad, land, send, recv, after, *, name):
    h = land.shape[1]

    def body(g_ref, land_ref, send_sems, recv_sems, after_ref, g_out, land_out):
        del after_ref, g_out, land_out
        x, y, c, _ = _place()
        cp = _rcopy(g_ref.at[:, pl.ds((1 - c) * h, h)], land_ref, send_sems, recv_sems, 0, (x, y, 1 - c))
        cp.wait_send()
        cp.wait_recv()

    return pl.pallas_call(
        body, name=name, in_specs=[HBM, HBM, SEM, SEM, HBM], out_specs=[HBM, HBM],
        out_shape=[pltpu.HBM(grad.shape, grad.dtype), pltpu.HBM(land.shape, land.dtype)],
        input_output_aliases={0: 0, 1: 1}, compiler_params=SPLIT_COPY,
    )(grad, land, send, recv, after)


def scatter_start(part, *, name):
    h = part.shape[1]

    def body(part_ref, land_ref, send, recv, part_out, land_out, token):
        del part_out, land_out
        x, y, c, others = _place()
        for f, chip in enumerate(others):
            for s0, sz in _row_pieces(h, DMA_PIECES):
                _rcopy(_chip_rows(part_ref, chip, s0, sz), land_ref.at[f, pl.ds(s0, sz)], send, recv, f, (*chip, c)).start()
        token[...] = jnp.zeros_like(token)

    land = lax.empty((N_OTHERS,) + part.shape[1:], part.dtype)
    return pl.pallas_call(
        body, name=name, in_specs=[HBM, HBM],
        out_specs=[SEM, SEM, HBM, HBM, pl.BlockSpec(memory_space=pltpu.VMEM)],
        out_shape=[pltpu.SemaphoreType.DMA((N_OTHERS,))] * 2 + [pltpu.HBM(part.shape, part.dtype), pltpu.HBM(land.shape, land.dtype),
                                                                 jax.ShapeDtypeStruct((1, D_MODEL), F32)],
        input_output_aliases={0: 2, 1: 3},
        compiler_params=SPLIT_COPY,
    )(_hbm(part), _hbm(land))


def scatter_wait(part, land, send, recv, after, *, name):
    h = part.shape[1]

    def body(part_ref, land_ref, send_sems, recv_sems, after_ref, part_out, land_out):
        del after_ref, part_out, land_out
        x, y, c, others = _place()
        for f, chip in enumerate(others):
            cp = _rcopy(_chip_rows(part_ref, chip, 0, h), land_ref.at[f], send_sems, recv_sems, f, (*chip, c))
            cp.wait_send()
            cp.wait_recv()

    return pl.pallas_call(
        body, name=name, in_specs=[HBM, HBM, SEM, SEM, HBM], out_specs=[HBM, HBM],
        out_shape=[pltpu.HBM(part.shape, part.dtype), pltpu.HBM(land.shape, land.dtype)],
        input_output_aliases={0: 0, 1: 1},
        compiler_params=SPLIT_COPY,
    )(part, land, send, recv, after)


def join_start(buf, *, name):
    h = buf.shape[0] // 2

    def body(b_ref, send, recv, b_out, token):
        del b_out
        x, y, c, _ = _place()
        for s0, sz in _row_pieces(h, 2 * DMA_PIECES):
            rows = b_ref.at[pl.ds(c * h + s0, sz)]
            _rcopy(rows, rows, send, recv, 0, (x, y, 1 - c)).start()
        token[...] = jnp.zeros_like(token)

    return pl.pallas_call(
        body, name=name, in_specs=[HBM], out_specs=[SEM, SEM, HBM, VMEM_SPEC],
        out_shape=[_sem1(), _sem1(), pltpu.HBM(buf.shape, buf.dtype), TOKEN],
        input_output_aliases={0: 2}, compiler_params=SPLIT_COPY,
    )(_hbm(buf))


def join_wait(buf, send, recv, after, *, name):
    h = buf.shape[0] // 2

    def body(b_ref, send_sems, recv_sems, after_ref, b_out):
        del after_ref, b_out
        x, y, c, _ = _place()
        cp = _rcopy(b_ref.at[pl.ds(c * h, h)], b_ref.at[pl.ds((1 - c) * h, h)], send_sems, recv_sems, 0, (x, y, 1 - c))
        cp.wait_send()
        cp.wait_recv()

    return pl.pallas_call(
        body, name=name, in_specs=[HBM, SEM, SEM, HBM], out_specs=HBM,
        out_shape=pltpu.HBM(buf.shape, buf.dtype), input_output_aliases={0: 0}, compiler_params=SPLIT_COPY,
    )(buf, send, recv, after)


def allreduce_small(v, *, name):
    rows, n = v.shape

    def body(x_ref, sum_ref, all_ref, send_sems, recv_sems, local_sem):
        x, y, c, others = _place()
        me, sibling = (x, y, c), (x, y, 1 - c)

        def blk(px, py, pc):
            return all_ref.at[pl.ds((4 * px + 2 * py + pc) * rows, rows), :]

        def copy(k, block, to, src=None):
            return _rcopy(blk(*block) if src is None else src, blk(*block), send_sems, recv_sems, k, to)

        mine = pltpu.make_async_copy(x_ref, blk(*me), local_sem)
        mine.start()
        first = [copy(0, me, sibling, src=x_ref)]
        first += [copy(1 + f, me, (*chip, c), src=x_ref) for f, chip in enumerate(others)]
        for cp in first:
            cp.start()
        passed = [copy(4 + f, (*chip, c), sibling) for f, chip in enumerate(others)]
        for f, chip in enumerate(others):
            copy(1 + f, (*chip, c), me).wait_recv()
            passed[f].start()
        copy(0, sibling, me).wait_recv()
        for f, chip in enumerate(others):
            copy(4 + f, (*chip, 1 - c), me).wait_recv()
        for cp in first + passed:
            cp.wait_send()
        mine.wait()
        acc = all_ref[pl.ds(0, rows), :]
        for d in range(1, N_DEVICES):
            acc = acc + all_ref[pl.ds(d * rows, rows), :]
        sum_ref[...] = acc

    vm = pl.BlockSpec(memory_space=pltpu.VMEM)
    return pl.pallas_call(
        body, name=name, in_specs=[vm], out_specs=[vm, vm],
        out_shape=[jax.ShapeDtypeStruct((rows, n), F32), jax.ShapeDtypeStruct((N_DEVICES * rows, n), F32)],
        scratch_shapes=_dma_sems(7, 7) + [pltpu.SemaphoreType.DMA],
    )(v)[0]


def add_pairs(grad, theirs, where, *, name):
    h = theirs.shape[1]
    spec = pl.BlockSpec((None, h, SLAB), lambda k, w: (k, 0, 0))

    def body(w_ref, a_ref, b_ref, o_ref):
        del w_ref
        o_ref[...] = (a_ref[...].astype(F32) + b_ref[...].astype(F32)).astype(o_ref.dtype)

    return pl.pallas_call(
        body, name=name,
        grid_spec=pltpu.PrefetchScalarGridSpec(
            num_scalar_prefetch=1, grid=(N_CHIPS,),
            in_specs=[pl.BlockSpec((None, h, SLAB), lambda k, w: (k, w[1], 0)), spec], out_specs=spec),
        out_shape=jax.ShapeDtypeStruct(theirs.shape, theirs.dtype),
        compiler_params=_cparams(("parallel",)))(where, grad, theirs)


def add_chips(pair, got, where, *, name):
    h = pair.shape[1]
    tr = h // 2

    def body(w_ref, a_ref, b_ref, o_ref):
        del w_ref
        acc = a_ref[...].astype(F32)
        for f in range(3):
            acc = acc + b_ref[f].astype(F32)
        o_ref[...] = acc

    return pl.pallas_call(
        body, name=name,
        grid_spec=pltpu.PrefetchScalarGridSpec(
            num_scalar_prefetch=1, grid=(2,),
            in_specs=[pl.BlockSpec((None, tr, SLAB), lambda i, w: (w[0], i, 0)),
                      pl.BlockSpec((3, tr, SLAB), lambda i, w: (0, i, 0))],
            out_specs=pl.BlockSpec((tr, SLAB), lambda i, w: (2 * w[1] + i, 0))),
        out_shape=jax.ShapeDtypeStruct((2 * h, SLAB), F32),
        compiler_params=_cparams(("parallel",)))(where, pair, got)


DEPTH = 4
MIXER = (0, 1, 2, 0)
W_IN_COLS = (768, 320, 772)
W_IN_PAD = (768, 512, 1024)
MATS = ("up", "down", "inp", "out", "gate", "proj")
MAT_ARG = dict(up="w_up", down="w_down", inp="w_in", out="w_out", gate="w_ple_gate", proj="w_ple_proj")
GAINS = ("attn_norm", "mlp_norm", "ple_norm")
N_SMALL = 16
KINDS = ("grad_", "delta_", "new_m_", "new_v_")


def _layout(kind):
    ns_in = W_IN_PAD[kind] // SLAB
    off = 8192 + 1024 * ns_in
    lay = dict(up=(0, 1024, 4, False), down=(4096, 1024, 4, True), inp=(8192, 1024, ns_in, False),
               out=(off, 256, 4, True), gate=(off + 1024, 256, 4, True), proj=(off + 2048, 256, 1, False))
    return lay, off + 2304


def _to_slabs(w):
    k, c = w.shape
    return w.reshape(k, c // SLAB, SLAB).transpose(1, 0, 2).reshape(-1, SLAB)


def _pad_cols(w, n):
    return jnp.pad(w, ((0, 0), (0, n - w.shape[1])))


def _heads(x2d, n):
    return x2d.reshape(x2d.shape[0], n, HEAD_DIM).transpose(1, 0, 2)


def _unheads(x3d):
    n, s, _ = x3d.shape
    return x3d.transpose(1, 0, 2).reshape(s, n * HEAD_DIM)


def _chip_cols(x2d, c, cpad):
    return jnp.concatenate([_pad_cols(x2d[:, j * c:(j + 1) * c], cpad) for j in range(N_CHIPS)], axis=1)


def _unchip_cols(x2d, c, cpad):
    return jnp.concatenate([x2d[:, j * cpad:j * cpad + c] for j in range(N_CHIPS)], axis=1)


def _forget_cols(wg, t):
    off, K, _, _ = t
    cols = []
    for g in range(3 * N_HEADS * HEAD_DIM, 3 * N_HEADS * HEAD_DIM + N_HEADS):
        chip, local = divmod(g, W_IN_COLS[2])
        q, c = divmod(local, SLAB)
        cols.append(wg[chip, off + q * K:off + (q + 1) * K, c:c + 1])
    return jnp.concatenate(cols, axis=1)


def _add_res(acc, res):
    return (acc + res,)


def _relu2(acc):
    return acc, jnp.square(jnp.maximum(acc, 0.0))


def _relu2_bwd(acc, u):
    return (acc * (2.0 * jnp.maximum(u.astype(F32), 0.0)),)


def _ple_fwd(acc, x2, pp):
    return x2 + pp * _sigmoid(acc), acc


def _ple_bwd(dx, pp, gl):
    gate = _sigmoid(gl)
    return dx * gate, dx * pp * gate * (1.0 - gate)


def _layer_fwd(i, kind, x0, p_bf, wg, lay, gains, extra, tabs, mid):
    s = x0.shape[0]
    an, mn, pn = gains
    sv = dict(x0=x0)
    if kind == 0:
        proj, h1 = mm_nn(x0, wg, lay["inp"], name=f"w_in_{i}", norm_gain=an)
        a, tot = sb_fwd(proj, name=f"sb_fwd_{i}")
        sv.update(proj=proj, tot=tot)
    elif kind == 1:
        projp, h1 = mm_nn(x0, wg, lay["inp"], name=f"w_in_{i}", out_dtypes=(F32,), norm_gain=an)
        proj = _unchip_cols(projp, W_IN_COLS[1], W_IN_PAD[1])
        nq = N_HEADS * HEAD_DIM
        nqk = nq + SWA_KV_HEADS * HEAD_DIM
        qk = rope_fwd(proj[:, :nqk], tabs, name=f"rope_{i}")
        q = _heads(qk[:, :nq], N_HEADS).reshape(SWA_KV_HEADS, SWA_GROUP, s, HEAD_DIM)
        front = ((0, 0), (SWA_WINDOW, 0), (0, 0))
        kp = jnp.pad(_heads(qk[:, nq:], SWA_KV_HEADS), front)
        vp = jnp.pad(_heads(proj[:, nqk:].astype(BF16), SWA_KV_HEADS), front)
        sink = jnp.repeat(extra.reshape(SWA_KV_HEADS, SWA_GROUP), ATT_BLK, axis=1)[:, :, None]
        o4, lse = swa_fwd(q, kp, vp, sink, name=f"swa_fwd_{i}")
        a = _unheads(o4.reshape(N_HEADS, s, HEAD_DIM))
        sv.update(q=q, kp=kp, vp=vp, sink=sink, o4=o4, lse=lse)
    else:
        projp, h1 = mm_nn(x0, wg, lay["inp"], name=f"w_in_{i}", norm_gain=an)
        nqkv = 3 * N_HEADS * HEAD_DIM
        proj = _unchip_cols(projp, W_IN_COLS[2], W_IN_PAD[2])[:, :nqkv]
        fl = mm_plain(h1, _pad_cols(_forget_cols(wg, lay["inp"]), 128), name=f"w_forget_{i}")
        bp = _pad_cols(extra[None], 128)
        cum_t = fox_gate_fwd(fl, bp, name=f"gate_fwd_{i}")[:, :N_HEADS].T
        cq = cum_t[:, :, None]
        ck = cum_t.reshape(N_HEADS, s // min(ATT_BK, s), 1, min(ATT_BK, s))
        a, lse = fox_fwd(proj, cq, ck, name=f"fox_fwd_{i}")
        sv.update(proj=proj, fl=fl, bp=bp, cq=cq, ck=ck, lse=lse)
    zero = mid(a)
    if zero is not None:
        mn = mn + zero
    x1 = mm_nn(a, wg, lay["out"], name=f"w_out_{i}", epi=_add_res, extras=(x0,), out_dtypes=(F32,))[0]
    u, r, h2 = mm_nn(x1, wg, lay["up"], name=f"w_up_{i}", epi=_relu2, out_dtypes=(BF16, BF16), norm_gain=mn)
    x2 = mm_nn(r, wg, lay["down"], name=f"w_down_{i}", epi=_add_res, extras=(x1,), out_dtypes=(F32,))[0]
    pp = mm_nn(p_bf, wg, lay["proj"], name=f"w_ple_proj_{i}", out_dtypes=(F32,))[0]
    x3, gl, h3 = mm_nn(x2, wg, lay["gate"], name=f"w_ple_gate_{i}", epi=_ple_fwd, extras=(x2, pp), out_dtypes=(F32, F32),
                       norm_gain=pn)
    sv.update(h1=h1, a=a, x1=x1, h2=h2, u=u, r=r, x2=x2, h3=h3, pp=pp, gl=gl)
    return x3, sv


def _layer_bwd(i, kind, dx3, sv, p_bf, wg, lay, n_rows, gains, tabs, mid, end):
    s = dx3.shape[0]
    an, mn, pn = gains
    g = lax.empty((N_CHIPS, n_rows, SLAB), BF16)
    d_pp, d_gl = ew(_ple_bwd, [dx3, sv["pp"], sv["gl"]], [BF16, BF16], name=f"ple_bwd_{i}")
    g = mm_tn(p_bf, d_pp, g, lay["proj"], name=f"dw_ple_proj_{i}")
    g = mm_tn(sv["h3"], d_gl, g, lay["gate"], name=f"dw_ple_gate_{i}")
    d_h3 = mm_nt(d_gl, wg, lay["gate"], name=f"dx_ple_gate_{i}", out_dtypes=(F32,))[0]
    dx2, dx2b, d_pn = rms_bwd(sv["x2"], pn, d_h3, dx3, name=f"ple_norm_bwd_{i}")
    zero = mid(dx2)
    if zero is not None:
        mn = mn + zero
    g = mm_tn(sv["r"], dx2b, g, lay["down"], name=f"dw_down_{i}")
    d_u = mm_nt(dx2b, wg, lay["down"], name=f"dx_down_{i}", epi=_relu2_bwd, extras=(sv["u"],))[0]
    g = mm_tn(sv["h2"], d_u, g, lay["up"], name=f"dw_up_{i}")
    dx1, dx1b, d_mn = mm_nt(d_u, wg, lay["up"], name=f"dx_up_{i}", rms=(sv["x1"], mn, dx2))
    g = mm_tn(sv["a"], dx1b, g, lay["out"], name=f"dw_out_{i}")
    d_a = mm_nt(dx1b, wg, lay["out"], name=f"dx_out_{i}")[0]
    d_extra = None
    if kind == 0:
        d_proj = jnp.concatenate(sb_bwd(sv["proj"], sv["tot"], d_a, name=f"sb_bwd_{i}"), axis=1)
    elif kind == 1:
        do4 = _heads(d_a, N_HEADS).reshape(SWA_KV_HEADS, SWA_GROUP, s, HEAD_DIM)
        dq, dkp, dvp, dsr = swa_bwd(sv["q"], sv["kp"], sv["vp"], sv["sink"], sv["o4"], sv["lse"], do4, name=f"swa_bwd_{i}")
        dqk = jnp.concatenate([_unheads(dq.reshape(N_HEADS, s, HEAD_DIM)), _unheads(dkp[:, SWA_WINDOW:])], axis=1)
        dqk = rope_bwd(dqk, tabs, name=f"rope_bwd_{i}")
        d_proj = jnp.concatenate([dqk, _unheads(dvp[:, SWA_WINDOW:]).astype(BF16)], axis=1)
        d_proj = _chip_cols(d_proj, W_IN_COLS[1], W_IN_PAD[1])
        d_extra = jnp.sum(dsr[..., 0], axis=2).reshape(N_HEADS)
    else:
        dq, dk, dv, dcq, dck = fox_bwd(sv["proj"], sv["a"], sv["lse"], sv["cq"], sv["ck"], d_a, name=f"fox_bwd_{i}")
        dcum = _pad_cols((dcq[:, :, 0] - dck.reshape(N_HEADS, s)).T, 128)
        dfl, dbp = fox_gate_bwd(dcum, sv["fl"], sv["bp"], name=f"gate_bwd_{i}")
        d_proj = jnp.concatenate([dq, dk, dv, dfl[:, :N_HEADS].astype(BF16)], axis=1)
        d_proj = _chip_cols(d_proj, W_IN_COLS[2], W_IN_PAD[2])
        d_extra = dbp[0, :N_HEADS]
    g = mm_tn(sv["h1"], d_proj, g, lay["inp"], name=f"dw_in_{i}")
    zero = end(g)
    if zero is not None:
        an = an + zero
    dx0, _, d_an = mm_nt(d_proj, wg, lay["inp"], name=f"dx_in_{i}", rms=(sv["x0"], an, dx1))
    return dx0, g, (d_an, d_mn, d_pn), d_extra


def _small_rows(a, prefix):
    rows = [a[f"{prefix}{n}_{i}"] for i in range(DEPTH) for n in GAINS] + [a[f"{prefix}final_norm"]]
    rows += [_pad_cols(a[f"{prefix}{n}"][None], D_MODEL)[0] for n in ("sinks_1", "b_forget_2")]
    return jnp.stack(rows + [jnp.zeros((D_MODEL,), F32)])


def _train_step(a):
    x = a["x"][0]
    tabs = rope_tables(x.shape[0])
    lays = [_layout(k) for k in MIXER]

    padded = {}

    def natural(prefix, i, m):
        w = a[f"{prefix}{MAT_ARG[m]}_{i}"]
        if m != "inp" or w.shape[1] == W_IN_PAD[MIXER[i]]:
            return w
        if (prefix, i) not in padded:
            padded[prefix, i] = _pad_cols(w, W_IN_PAD[MIXER[i]])
        return padded[prefix, i]

    chip = 2 * lax.axis_index("x") + lax.axis_index("y")
    where = jnp.stack([chip, lax.axis_index("c")]).astype(jnp.int32)
    def own_block(i, zero):
        pk = jnp.concatenate([_to_slabs((natural("", i, m) + zero).astype(BF16)) for m in MATS], axis=0)
        return lax.dynamic_update_slice(lax.empty((N_CHIPS,) + pk.shape, BF16), pk[None], (chip, 0, 0))

    sends, recvs, bufs, token = allgather_start([own_block(0, 0.0)], name="allgather_start_0")
    more = allgather_start([own_block(i, token[0, 0]) for i in range(1, DEPTH)], name="allgather_start_1")
    sends, recvs, bufs, token = sends + more[0], recvs + more[1], bufs + more[2], more[3]

    gains = [tuple(a[f"{n}_{i}"][None] for n in GAINS) for i in range(DEPTH)]
    extras = [None, a["sinks_1"], a["b_forget_2"], None]
    p_bf = [a["p"][i, 0].astype(BF16) for i in range(DEPTH)]

    small_state = [_small_rows(a, prefix) for prefix in ("", "m_", "v_")]
    after = [token, *tabs, *p_bf, *small_state] + [natural(prefix, i, "inp") for prefix in ("m_", "v_") for i in (1, 2)]
    saved, wgs = [], []
    landed = allgather_wait(bufs[0], sends[0], recvs[0], after, name="allgather_wait_0")
    moving = dict(next=forward_start(landed, name="forward_start_0"))
    for i in range(DEPTH):
        send, recv, buf, zero = moving["next"]
        wgs.append(forward_wait(buf, send, recv, x if i else zero, name=f"forward_wait_{i}"))

        def mid(attn_out):
            if i + 1 == DEPTH:
                return None
            arrived = allgather_wait(bufs[i + 1], sends[i + 1], recvs[i + 1], [attn_out], name=f"allgather_wait_{i + 1}")
            moving["next"] = forward_start(arrived, name=f"forward_start_{i + 1}")
            return moving["next"][3]

        x, sv = _layer_fwd(i, MIXER[i], x, p_bf[i], wgs[i], lays[i][0], gains[i], extras[i], tabs, mid)
        saved.append(sv)
    dx, d_final, loss = loss_head(x, a["final_norm"][None], a["loss_target"][0], name="loss_head")

    def pair_and_scatter(j, swapped, after):
        send, recv, grad, land, _ = swapped
        grad, theirs = swap_wait(grad, land, send, recv, after, name=f"swap_wait_{j}")
        return scatter_start(add_pairs(grad, theirs, where, name=f"add_pairs_{j}"), name=f"scatter_start_{j}")

    def sum_and_join(j, scattered, after):
        send, recv, part, land, _ = scattered
        part, got = scatter_wait(part, land, send, recv, after, name=f"scatter_wait_{j}")
        return join_start(add_chips(part, got, where, name=f"add_chips_{j}"), name=f"join_start_{j}")

    small = [None] * N_SMALL
    small[12] = d_final[0]
    small[15] = _pad_cols(loss[:, :1], D_MODEL)[0]
    joined = [None] * DEPTH
    state = dict(swapped=None, scattered=None)
    for i in reversed(range(DEPTH)):
        an, mn, pn = gains[i]
        if state["swapped"] is not None:
            pn = pn + state["swapped"][4]

        def mid(dx2):
            if state["swapped"] is None:
                return None
            state["scattered"] = pair_and_scatter(i + 1, state["swapped"], dx2)
            return state["scattered"][4]

        def end(grad):
            state["swapped"] = swap_start(grad, name=f"swap_start_{i}")
            return state["swapped"][4]

        dx, _, d_gains, d_extra = _layer_bwd(i, MIXER[i], dx, saved[i], p_bf[i], wgs[i], lays[i][0], lays[i][1],
                                             (an, mn, pn), tabs, mid, end)
        for j in range(3):
            small[3 * i + j] = d_gains[j][0]
        if d_extra is not None:
            small[12 + MIXER[i]] = _pad_cols(d_extra[None], D_MODEL)[0]
        if state["scattered"] is not None:
            joined[i + 1] = sum_and_join(i + 1, state["scattered"], dx)
    started = pair_and_scatter(0, state["swapped"], dx)
    small = allreduce_small(jnp.stack(small), name="allreduce_small")

    out = {"loss": small[15, 0], "grad_x": dx[None]}
    res = adamw_small(small, *small_state, name="adamw_small")
    prev = started[4]
    for i in reversed(range(DEPTH)):
        if i == 0:
            joined[0] = sum_and_join(0, started, prev)
        send, recv, buf, zero = joined[i]
        gfull = join_wait(buf, send, recv, prev if i else zero, name=f"join_wait_{i}")
        for m in MATS:
            upd = adamw(gfull, lays[i][0][m], natural("", i, m), natural("m_", i, m), natural("v_", i, m), prev,
                        name=f"adamw_{MAT_ARG[m]}_{i}")
            prev = upd[1]
            cols = a[f"{MAT_ARG[m]}_{i}"].shape[1]
            for kd, r in zip(KINDS, upd):
                out[f"{kd}{MAT_ARG[m]}_{i}"] = r[:, :cols]
    for kd, r in zip(KINDS, res):
        for i in range(DEPTH):
            for j, n in enumerate(GAINS):
                out[f"{kd}{n}_{i}"] = r[3 * i + j][0]
        out[f"{kd}final_norm"] = r[12][0]
        out[f"{kd}sinks_1"] = r[13][0, :N_HEADS]
        out[f"{kd}b_forget_2"] = r[14][0, :N_HEADS]
    return out


def _weight_names():
    names = []
    for i in range(DEPTH):
        names += [f"attn_norm_{i}", f"w_in_{i}", f"w_out_{i}"] + [[], ["sinks_1"], ["b_forget_2"]][MIXER[i]]
        names += [f"mlp_norm_{i}", f"w_up_{i}", f"w_down_{i}", f"ple_norm_{i}", f"w_ple_gate_{i}", f"w_ple_proj_{i}"]
    return names + ["final_norm"]


def kernel(x, p, attn_norm_0, w_in_0, w_out_0, mlp_norm_0, w_up_0, w_down_0, ple_norm_0, w_ple_gate_0, w_ple_proj_0, attn_norm_1, w_in_1, w_out_1, sinks_1, mlp_norm_1, w_up_1, w_down_1, ple_norm_1, w_ple_gate_1, w_ple_proj_1, attn_norm_2, w_in_2, w_out_2, b_forget_2, mlp_norm_2, w_up_2, w_down_2, ple_norm_2, w_ple_gate_2, w_ple_proj_2, attn_norm_3, w_in_3, w_out_3, mlp_norm_3, w_up_3, w_down_3, ple_norm_3, w_ple_gate_3, w_ple_proj_3, final_norm, loss_target, m_attn_norm_0, m_w_in_0, m_w_out_0, m_mlp_norm_0, m_w_up_0, m_w_down_0, m_ple_norm_0, m_w_ple_gate_0, m_w_ple_proj_0, m_attn_norm_1, m_w_in_1, m_w_out_1, m_sinks_1, m_mlp_norm_1, m_w_up_1, m_w_down_1, m_ple_norm_1, m_w_ple_gate_1, m_w_ple_proj_1, m_attn_norm_2, m_w_in_2, m_w_out_2, m_b_forget_2, m_mlp_norm_2, m_w_up_2, m_w_down_2, m_ple_norm_2, m_w_ple_gate_2, m_w_ple_proj_2, m_attn_norm_3, m_w_in_3, m_w_out_3, m_mlp_norm_3, m_w_up_3, m_w_down_3, m_ple_norm_3, m_w_ple_gate_3, m_w_ple_proj_3, m_final_norm, v_attn_norm_0, v_w_in_0, v_w_out_0, v_mlp_norm_0, v_w_up_0, v_w_down_0, v_ple_norm_0, v_w_ple_gate_0, v_w_ple_proj_0, v_attn_norm_1, v_w_in_1, v_w_out_1, v_sinks_1, v_mlp_norm_1, v_w_up_1, v_w_down_1, v_ple_norm_1, v_w_ple_gate_1, v_w_ple_proj_1, v_attn_norm_2, v_w_in_2, v_w_out_2, v_b_forget_2, v_mlp_norm_2, v_w_up_2, v_w_down_2, v_ple_norm_2, v_w_ple_gate_2, v_w_ple_proj_2, v_attn_norm_3, v_w_in_3, v_w_out_3, v_mlp_norm_3, v_w_up_3, v_w_down_3, v_ple_norm_3, v_w_ple_gate_3, v_w_ple_proj_3, v_final_norm):
    out = _train_step(dict(locals()))
    return (out["loss"], out["grad_x"], *[out[kd + n] for kd in KINDS for n in _weight_names()])
```

```python
import jax
import jax.numpy as jnp
from jax import lax
from jax.experimental import pallas as pl
from jax.experimental.pallas import tpu as pltpu

F32 = jnp.float32
BF16 = jnp.bfloat16

D_MODEL = 1024
N_HEADS = 16
HEAD_DIM = 64
SWA_KV_HEADS = 2
SWA_GROUP = 8
SWA_WINDOW = 128
ROPE_THETA = 500000.0
ROPE_DIM = 16
RMS_EPS = 1e-6
NEG_INF = -1e30
ATTN_SCALE = HEAD_DIM ** -0.5
N_CHIPS = 4
N_DEVICES = 8

SLAB = 256
ATT_BLK = 128
ATT_BQ = 512
ATT_BK = 512
ROW_TILE = 256
V7X_VMEM_LIMIT = 56 * 1024 * 1024

ADAM_LR, ADAM_B1, ADAM_B2, ADAM_EPS, ADAM_WD, ADAM_STEP = 0.001, 0.9, 0.999, 1e-08, 0.01, 10


def _cparams(sem=None):
    return pltpu.CompilerParams(dimension_semantics=sem, vmem_limit_bytes=V7X_VMEM_LIMIT)


def _dot(a, b):
    return jnp.dot(a, b, preferred_element_type=F32)


def _dot_nt(a, b):
    return lax.dot_general(a, b, (((1,), (1,)), ((), ())), preferred_element_type=F32)


def _dot_tn(a, b):
    return lax.dot_general(a, b, (((0,), (0,)), ((), ())), preferred_element_type=F32)


def _row_tile(M, K):
    return min(M, 1024) if K >= 1024 else M


def _finish(epi, acc, ex, outs):
    res = epi(acc, *[e[...] for e in ex]) if epi is not None else (acc,)
    for o, r in zip(outs, res):
        o[...] = r.astype(o.dtype)


def _once(shape, index_map):
    return pl.BlockSpec(shape, index_map, pipeline_mode=pl.Buffered(1))


def mm_nn(a, wg, t, *, name, epi=None, extras=(), out_dtypes=(BF16,), norm_gain=None):
    off, K, ns, row = t
    M = a.shape[0]
    sb = off // K
    ne, no = len(extras), len(out_dtypes)
    norm = norm_gain is not None
    if row:
        tm = M if norm else _row_tile(M, K)
        nb = N_CHIPS
        grid = (M // tm, ns)
        a_shape = (tm, N_CHIPS * K)
        a_spec = (_once if norm else pl.BlockSpec)(a_shape, lambda i, q: (i, 0))
        b_specs = [pl.BlockSpec((None, K, SLAB), lambda i, q, j=j: (j, sb + q, 0)) for j in range(nb)]
        tile = pl.BlockSpec((tm, SLAB), lambda i, q: (i, q))
        n_out = ns * SLAB
    else:
        nb = ns
        grid = (N_CHIPS,)
        a_shape = (M, K)
        a_spec = (_once if norm else pl.BlockSpec)(a_shape, lambda j: (0, 0))
        b_specs = [pl.BlockSpec((None, K, SLAB), lambda j, q=q: (j, sb + q, 0)) for q in range(ns)]
        tile = pl.BlockSpec((M, ns * SLAB), lambda j: (0, j))
        n_out = N_CHIPS * ns * SLAB

    def body(a_ref, *rest):
        if norm:
            g_ref, rest, h_out, h_ref = rest[0], rest[1:-2], rest[-2], rest[-1]

            @pl.when(pl.program_id(len(grid) - 1) == 0)
            def _():
                xv = a_ref[...]
                h_ref[...] = (xv * _rstd(xv) * g_ref[...]).astype(BF16)
                h_out[...] = h_ref[...]

            a_ref = h_ref
        bs, ex, outs = rest[:nb], rest[nb:nb + ne], rest[nb + ne:]
        if row:
            acc = _dot(a_ref[:, pl.ds(0, K)], bs[0][...])
            for j in range(1, nb):
                acc = acc + _dot(a_ref[:, pl.ds(j * K, K)], bs[j][...])
            _finish(epi, acc, ex, outs)
        else:
            av = a_ref[...]
            for q in range(ns):
                cols = pl.ds(q * SLAB, SLAB)
                _finish(epi, _dot(av, bs[q][...]), [e.at[:, cols] for e in ex], [o.at[:, cols] for o in outs])

    h_spec = _once(a_shape, (lambda i, q: (i, 0)) if row else (lambda j: (0, 0)))
    return pl.pallas_call(
        body, name=name, grid=grid,
        in_specs=[a_spec] + ([pl.BlockSpec(norm_gain.shape, lambda *_: (0, 0))] if norm else []) + b_specs + [tile] * ne,
        out_specs=[tile] * no + ([h_spec] if norm else []),
        out_shape=[jax.ShapeDtypeStruct((M, n_out), d) for d in out_dtypes]
        + ([jax.ShapeDtypeStruct(a.shape, BF16)] if norm else []),
        scratch_shapes=[pltpu.VMEM(a_shape, BF16)] if norm else [],
        compiler_params=_cparams((("arbitrary" if norm else "parallel"),) * len(grid)),
    )(a, *([norm_gain] if norm else []), *([wg] * nb), *extras)


def mm_nt(dy, wg, t, *, name, epi=None, extras=(), out_dtypes=(BF16,), rms=None):
    off, K, ns, row = t
    M = dy.shape[0]
    tm = _row_tile(M, K)
    sb = off // K
    if rms is not None:
        x, gain, dres = rms
        extras, out_dtypes = (x, dres), (F32, BF16)
    ne, no = len(extras), len(out_dtypes)
    grid = (M // tm, N_CHIPS)
    b_specs = [pl.BlockSpec((None, K, SLAB), lambda i, j, q=q: (j, sb + q, 0)) for q in range(ns)]
    if row:
        dy_spec = pl.BlockSpec((tm, ns * SLAB), lambda i, j: (i, 0))
        tile = pl.BlockSpec((tm, K), lambda i, j: (i, j))
        n_out = N_CHIPS * K
        sem = ("parallel", "parallel")
    else:
        dy_spec = pl.BlockSpec((tm, ns * SLAB), lambda i, j: (i, j))
        tile = pl.BlockSpec((tm, K), lambda i, j: (i, 0))
        n_out = K
        sem = ("arbitrary" if rms is not None else "parallel", "arbitrary")
    one = pl.BlockSpec((1, K), lambda i, j: (0, 0))

    def body(dy_ref, *rest):
        if rms is not None:
            g_ref, rest, dg_ref, acc_ref = rest[0], rest[1:-2], rest[-2], rest[-1]
            rest = rest + (acc_ref,)
        bs, ex, outs = rest[:ns], rest[ns:ns + ne], rest[ns + ne:ns + ne + no]
        part = _dot_nt(dy_ref[:, pl.ds(0, SLAB)], bs[0][...])
        for q in range(1, ns):
            part = part + _dot_nt(dy_ref[:, pl.ds(q * SLAB, SLAB)], bs[q][...])
        if row:
            _finish(epi, part, ex, outs)
        else:
            acc_ref = rest[-1]
            i, j = pl.program_id(0), pl.program_id(1)

            @pl.when(j == 0)
            def _():
                acc_ref[...] = part

            @pl.when(j > 0)
            def _():
                acc_ref[...] += part

            @pl.when(j == N_CHIPS - 1)
            def _():
                if rms is None:
                    _finish(epi, acc_ref[...], ex, outs)
                else:
                    dx, dg = _rms_bwd_tile(ex[0][...], g_ref[...], acc_ref[...])
                    dx = dx + ex[1][...]
                    outs[0][...] = dx
                    outs[1][...] = dx.astype(BF16)

                    @pl.when(i == 0)
                    def _():
                        dg_ref[...] = dg

                    @pl.when(i > 0)
                    def _():
                        dg_ref[...] += dg

    has = rms is not None
    return pl.pallas_call(
        body, name=name, grid=grid,
        in_specs=[dy_spec] + ([one] if has else []) + b_specs + [tile] * ne,
        out_specs=[tile] * no + ([one] if has else []),
        out_shape=[jax.ShapeDtypeStruct((M, n_out), d) for d in out_dtypes] + ([jax.ShapeDtypeStruct((1, K), F32)] if has else []),
        scratch_shapes=[] if row else [pltpu.VMEM((tm, K), F32)],
        compiler_params=_cparams(sem),
    )(dy, *([gain] if has else []), *([wg] * ns), *extras)


def mm_plain(a, b, *, name):
    M, K = a.shape
    N = b.shape[1]
    tm = min(M, 512)

    def body(a_ref, b_ref, o_ref):
        o_ref[...] = _dot(a_ref[...], b_ref[...])

    return pl.pallas_call(
        body, name=name, grid=(M // tm,),
        in_specs=[pl.BlockSpec((tm, K), lambda i: (i, 0)), pl.BlockSpec((K, N), lambda i: (0, 0))],
        out_specs=pl.BlockSpec((tm, N), lambda i: (i, 0)), out_shape=jax.ShapeDtypeStruct((M, N), F32),
        compiler_params=_cparams(("parallel",)),
    )(a, b)


def mm_tn(x, dy, g, t, *, name):
    off, K, ns, row = t
    S = x.shape[0]
    per = ns if off % (ns * K) == 0 else 1
    grid = (N_CHIPS, ns // per)
    if row:
        x_map = lambda j, q: (0, j)
        dy_map = lambda j, q: (0, q)
    else:
        x_map = lambda j, q: (0, 0)
        dy_map = lambda j, q: (0, j * (ns // per) + q)

    def body(g_in, x_ref, dy_ref, o_ref):
        del g_in
        xt = x_ref[...].T
        for q in range(per):
            o_ref[pl.ds(q * K, K), :] = _dot(xt, dy_ref[:, pl.ds(q * SLAB, SLAB)]).astype(o_ref.dtype)

    return pl.pallas_call(
        body, name=name, grid=grid,
        in_specs=[pl.BlockSpec(memory_space=pl.ANY), pl.BlockSpec((S, K), x_map), pl.BlockSpec((S, per * SLAB), dy_map)],
        out_specs=pl.BlockSpec((None, per * K, SLAB), lambda j, q: (j, off // (per * K) + q, 0)),
        out_shape=jax.ShapeDtypeStruct(g.shape, g.dtype),
        input_output_aliases={0: 0},
        compiler_params=_cparams(("parallel", "parallel")),
    )(g, x, dy)


def ew(fn, ins, out_dtypes, *, name, bcast=()):
    S = ins[0].shape[0]
    cols = ins[0].shape[1]
    tr = min(ROW_TILE if cols > 1280 else 2 * ROW_TILE, S)
    ni, nb = len(ins), len(bcast)

    def body(*refs):
        res = fn(*[r[...] for r in refs[:ni + nb]])
        for o, r in zip(refs[ni + nb:], res):
            o[...] = r.astype(o.dtype)

    return pl.pallas_call(
        body, name=name, grid=(S // tr,),
        in_specs=[pl.BlockSpec((tr, a.shape[1]), lambda i: (i, 0)) for a in ins]
        + [pl.BlockSpec(b.shape, lambda i: (0, 0)) for b in bcast],
        out_specs=[pl.BlockSpec((tr, cols), lambda i: (i, 0)) for _ in out_dtypes],
        out_shape=[jax.ShapeDtypeStruct((S, cols), d) for d in out_dtypes],
        compiler_params=_cparams(("parallel",)),
    )(*ins, *bcast)


def _rstd(x):
    return lax.rsqrt(jnp.mean(x * x, axis=-1, keepdims=True) + RMS_EPS)


def _sigmoid(x):
    return 1.0 / (1.0 + jnp.exp(-x))


def _log_sigmoid(z):
    return jnp.minimum(z, 0.0) - jnp.log(1.0 + jnp.exp(-jnp.abs(z)))


def _rms_bwd_tile(xv, gv, dh):
    rstd = _rstd(xv)
    xhat = xv * rstd
    gd = dh * gv
    dx = rstd * (gd - xhat * jnp.mean(xhat * gd, axis=-1, keepdims=True))
    return dx, jnp.sum(dh * xhat, axis=0, keepdims=True)


def rms_bwd(x, g, dh, dres, *, name):
    S, D = x.shape
    tr = min(ROW_TILE, S)

    def body(x_ref, g_ref, dh_ref, dres_ref, dx_ref, dxb_ref, dg_ref):
        i = pl.program_id(0)
        dx, dg = _rms_bwd_tile(x_ref[...], g_ref[...], dh_ref[...])
        dx = dx + dres_ref[...]
        dx_ref[...] = dx
        dxb_ref[...] = dx.astype(BF16)

        @pl.when(i == 0)
        def _():
            dg_ref[...] = dg

        @pl.when(i > 0)
        def _():
            dg_ref[...] += dg

    row = pl.BlockSpec((tr, D), lambda i: (i, 0))
    one = pl.BlockSpec((1, D), lambda i: (0, 0))
    return pl.pallas_call(
        body, name=name, grid=(S // tr,),
        in_specs=[row, one, row, row], out_specs=[row, row, one],
        out_shape=[jax.ShapeDtypeStruct((S, D), F32), jax.ShapeDtypeStruct((S, D), BF16),
                   jax.ShapeDtypeStruct((1, D), F32)],
        compiler_params=_cparams(("arbitrary",)),
    )(x, g, dh, dres)


def loss_head(x, g, target, *, name):
    S, D = x.shape
    tr = min(ROW_TILE, S)

    def body(x_ref, g_ref, t_ref, dx_ref, dg_ref, loss_ref):
        i = pl.program_id(0)
        xv, gv = x_ref[...], g_ref[...]
        err = xv * _rstd(xv) * gv - t_ref[...]
        part = 0.5 * jnp.sum(jnp.mean(err * err, axis=-1, keepdims=True), axis=0, keepdims=True)
        dx, dg = _rms_bwd_tile(xv, gv, err * (1.0 / D))
        dx_ref[...] = dx
        part = jnp.broadcast_to(part, loss_ref.shape)

        @pl.when(i == 0)
        def _():
            dg_ref[...] = dg
            loss_ref[...] = part

        @pl.when(i > 0)
        def _():
            dg_ref[...] += dg
            loss_ref[...] += part

    row = pl.BlockSpec((tr, D), lambda i: (i, 0))
    one = pl.BlockSpec((1, D), lambda i: (0, 0))
    return pl.pallas_call(
        body, name=name, grid=(S // tr,),
        in_specs=[row, one, row], out_specs=[row, one, pl.BlockSpec((1, 128), lambda i: (0, 0))],
        out_shape=[jax.ShapeDtypeStruct((S, D), F32), jax.ShapeDtypeStruct((1, D), F32),
                   jax.ShapeDtypeStruct((1, 128), F32)],
        compiler_params=_cparams(("arbitrary",)),
    )(x, g, target)


def rope_tables(S):
    half = ROPE_DIM // 2
    inv_freq = ROPE_THETA ** (-jnp.arange(half, dtype=F32) / half)
    ang = jnp.arange(S, dtype=F32)[:, None] * inv_freq[None, :]
    cos, sin = jnp.cos(ang), jnp.sin(ang)
    z = jnp.zeros((S, HEAD_DIM - ROPE_DIM), F32)
    zh = jnp.zeros((S, half), F32)
    c = jnp.concatenate([cos, cos, jnp.ones_like(z)], axis=1)
    sa = jnp.concatenate([zh, sin, z], axis=1)
    sb = jnp.concatenate([-sin, zh, z], axis=1)
    return [jnp.concatenate([t, t], axis=1) for t in (c, sa, sb)]


def _wide(t, n):
    return jnp.tile(t, (1, n // t.shape[1]))


def rope_fwd(xqk, tables, *, name):
    n, half = xqk.shape[1], ROPE_DIM // 2

    def fn(x, c, sa, sb):
        return (x * _wide(c, n) + pltpu.roll(x, half, 1) * _wide(sa, n) + pltpu.roll(x, n - half, 1) * _wide(sb, n),)

    return ew(fn, [xqk] + list(tables), [BF16], name=name)[0]


def rope_bwd(dy, tables, *, name):
    n, half = dy.shape[1], ROPE_DIM // 2

    def fn(d, c, sa, sb):
        return (d * _wide(c, n) + pltpu.roll(d * _wide(sa, n), n - half, 1) + pltpu.roll(d * _wide(sb, n), half, 1),)

    return ew(fn, [dy] + list(tables), [BF16], name=name)[0]


def _split3(x):
    h1 = x.astype(BF16)
    r1 = x - h1.astype(F32)
    h2 = r1.astype(BF16)
    return h1, h2, (r1 - h2.astype(F32)).astype(BF16)


def _tri(n, cmp):
    r = lax.broadcasted_iota(jnp.int32, (n, n), 0)
    c = lax.broadcasted_iota(jnp.int32, (n, n), 1)
    return cmp(r, c).astype(BF16)


def fox_gate_fwd(fl, b, *, name):
    S, W = fl.shape
    tr = min(ROW_TILE, S)

    def body(fl_ref, b_ref, cum_ref, carry):
        i = pl.program_id(0)

        @pl.when(i == 0)
        def _():
            carry[...] = jnp.zeros_like(carry)

        lower = _tri(tr, lambda r, c: r >= c)
        cs = carry[...]
        for piece in _split3(_log_sigmoid(fl_ref[...] + b_ref[...])):
            cs = cs + _dot(lower, piece)
        cum_ref[...] = cs
        carry[...] = cs[tr - 1:tr, :]

    return pl.pallas_call(
        body, name=name, grid=(S // tr,),
        in_specs=[pl.BlockSpec((tr, W), lambda i: (i, 0)), pl.BlockSpec((1, W), lambda i: (0, 0))],
        out_specs=pl.BlockSpec((tr, W), lambda i: (i, 0)),
        out_shape=jax.ShapeDtypeStruct((S, W), F32),
        scratch_shapes=[pltpu.VMEM((1, W), F32)],
        compiler_params=_cparams(("arbitrary",)),
    )(fl, b)


def fox_gate_bwd(dcum, fl, b, *, name):
    S, W = fl.shape
    tr = min(ROW_TILE, S)
    nb = S // tr

    def body(dc_ref, fl_ref, b_ref, dfl_ref, db_ref, carry):
        i = pl.program_id(0)

        @pl.when(i == 0)
        def _():
            carry[...] = jnp.zeros_like(carry)

        upper = _tri(tr, lambda r, c: r <= c)
        cs = carry[...]
        for piece in _split3(dc_ref[...]):
            cs = cs + _dot(upper, piece)
        carry[...] = cs[0:1, :]
        dfl = cs * _sigmoid(-(fl_ref[...] + b_ref[...]))
        dfl_ref[...] = dfl
        db = jnp.sum(dfl, axis=0, keepdims=True)

        @pl.when(i == 0)
        def _():
            db_ref[...] = db

        @pl.when(i > 0)
        def _():
            db_ref[...] += db

    rev = pl.BlockSpec((tr, W), lambda i: (nb - 1 - i, 0))
    one = pl.BlockSpec((1, W), lambda i: (0, 0))
    return pl.pallas_call(
        body, name=name, grid=(nb,),
        in_specs=[rev, rev, one], out_specs=[rev, one],
        out_shape=[jax.ShapeDtypeStruct((S, W), F32), jax.ShapeDtypeStruct((1, W), F32)],
        scratch_shapes=[pltpu.VMEM((1, W), F32)],
        compiler_params=_cparams(("arbitrary",)),
    )(dcum, fl, b)


def _blk_iota(tq, tk):
    return (lax.broadcasted_iota(jnp.int32, (tq, tk), 0), lax.broadcasted_iota(jnp.int32, (tq, tk), 1))


def _cs(xb, tri):
    return _dot(xb, tri)


def _rowsum(xb):
    return jnp.sum(xb.astype(F32), axis=1, keepdims=True)


def _sb_block(qs, k, cmr, shift):
    z = _dot_nt(qs, k)
    lb = jnp.minimum(z, 0.0) - jnp.log(1.0 + jnp.exp(-jnp.abs(z)))
    if cmr is None:
        return lb, (lb - z).astype(BF16), None
    strict = cmr < shift
    lom = jnp.where(strict, lb - z, 0.0).astype(BF16)
    return lb, lom, strict


def _keep(mask, x):
    return x if mask is None else jnp.where(mask, x, 0.0)


def _att_tiles(S):
    return min(ATT_BQ, S), min(ATT_BK, S)


PAIR = 2 * HEAD_DIM
N_PAIRS = N_HEADS // 2


def _pair_specs(S, tq):
    cols = D_MODEL // PAIR
    qspec = pl.BlockSpec((tq, PAIR), lambda p, i: (i, p))
    kspec = pl.BlockSpec((S, PAIR), lambda p, i: (0, cols + p))
    vspec = pl.BlockSpec((S, PAIR), lambda p, i: (0, 2 * cols + p))
    kvout = pl.BlockSpec((S, PAIR), lambda p, i: (0, p))
    vec = pl.BlockSpec((2, tq, 1), lambda p, i: (p, i, 0))
    return qspec, kspec, vspec, kvout, vec


def _head_lanes(h):
    lane = lax.broadcasted_iota(jnp.int32, (1, PAIR), 1)
    return (lane >= h * HEAD_DIM) & (lane < (h + 1) * HEAD_DIM)


def _only(sel, x):
    return jnp.where(sel, x, jnp.zeros_like(x))


def sb_fwd(proj, *, name):
    S = proj.shape[0]
    tq, tk = _att_tiles(S)
    qspec, kspec, vspec, _, vec = _pair_specs(S, tq)

    def body(q_ref, k_ref, v_ref, o_ref, t_ref):
        i = pl.program_id(1)
        row, col = _blk_iota(tq, tk)
        cmr = col - row
        below = _tri(tk, lambda r, c: r > c)
        nkb = (i + 1) * (tq // tk)
        out = []
        for h in range(2):
            sel = _head_lanes(h)
            qs = _only(sel, q_ref[...] * ATTN_SCALE)

            def step(n, carry, masked):
                r_sum, acc = carry
                kb = nkb - 1 - n
                ks = pl.multiple_of(kb * tk, tk)
                lb, lom, strict = _sb_block(qs, k_ref[pl.ds(ks, tk), :], cmr if masked else None, i * tq - kb * tk)
                w = _keep(strict, jnp.exp(lb + _cs(lom, below) + r_sum))
                acc = acc + _dot(w.astype(BF16), _only(sel, v_ref[pl.ds(ks, tk), :]))
                return r_sum + _rowsum(lom), acc

            nd = tq // tk
            carry = lax.fori_loop(0, nd, lambda n, c: step(n, c, True), (jnp.zeros((tq, 1), F32), jnp.zeros((tq, PAIR), F32)))
            r_sum, acc = lax.fori_loop(nd, nkb, lambda n, c: step(n, c, False), carry)
            t_ref[h] = r_sum
            out.append(acc)
        o_ref[...] = (out[0] + out[1]).astype(o_ref.dtype)

    return pl.pallas_call(
        body, name=name, grid=(N_PAIRS, S // tq),
        in_specs=[qspec, kspec, vspec], out_specs=[qspec, vec],
        out_shape=[jax.ShapeDtypeStruct((S, D_MODEL), BF16), jax.ShapeDtypeStruct((N_HEADS, S, 1), F32)],
        compiler_params=_cparams(("parallel", "arbitrary")),
    )(proj, proj, proj)


def sb_bwd(proj, tot, do, *, name):
    S = proj.shape[0]
    tq, tk = _att_tiles(S)
    qspec, kspec, vspec, kvout, vec = _pair_specs(S, tq)

    def body(q_ref, k_ref, v_ref, t_ref, do_ref, dq_ref, dk_out, dv_out, dk_ref, dv_ref):
        i = pl.program_id(1)

        @pl.when(i == 0)
        def _():
            dk_ref[...] = jnp.zeros_like(dk_ref)
            dv_ref[...] = jnp.zeros_like(dv_ref)

        row, col = _blk_iota(tq, tk)
        cmr = col - row
        upto = _tri(tk, lambda r, c: r <= c)
        before = _tri(tk, lambda r, c: r < c)
        out = []
        for h in range(2):
            sel = _head_lanes(h)
            qs, dov, t_all = _only(sel, q_ref[...] * ATTN_SCALE), _only(sel, do_ref[...]), t_ref[h]

            def step(kb, carry, masked):
                p_sum, e_sum, dq = carry
                ks = pl.multiple_of(kb * tk, tk)
                kv = k_ref[pl.ds(ks, tk), :]
                lb, lom, strict = _sb_block(qs, kv, cmr if masked else None, i * tq - kb * tk)
                tail = t_all - p_sum - _cs(lom, upto)
                w = _keep(strict, jnp.exp(lb + tail))
                e = _dot_nt(dov, v_ref[pl.ds(ks, tk), :]) * w
                eb = e.astype(BF16)
                e_before = e_sum + _cs(eb, before)
                beta = jnp.exp(lb)
                dzb = _keep(strict, e - (e + e_before) * beta).astype(BF16)
                dk_ref[pl.ds(ks, tk), :] += _dot_tn(dzb, qs)
                dv_ref[pl.ds(ks, tk), :] += _dot_tn(w.astype(BF16), dov)
                return p_sum + _rowsum(lom), e_sum + _rowsum(eb), dq + _dot(dzb, _only(sel, kv))

            zero = jnp.zeros((tq, 1), F32)
            nlow = i * (tq // tk)
            carry = lax.fori_loop(0, nlow, lambda kb, c: step(kb, c, False), (zero, zero, jnp.zeros((tq, PAIR), F32)))
            out.append(lax.fori_loop(nlow, nlow + tq // tk, lambda kb, c: step(kb, c, True), carry)[2])
        dq_ref[...] = ((out[0] + out[1]) * ATTN_SCALE).astype(dq_ref.dtype)

        @pl.when(i == S // tq - 1)
        def _():
            dk_out[...] = dk_ref[...].astype(dk_out.dtype)
            dv_out[...] = dv_ref[...].astype(dv_out.dtype)

    full = jax.ShapeDtypeStruct((S, D_MODEL), BF16)
    return pl.pallas_call(
        body, name=name, grid=(N_PAIRS, S // tq),
        in_specs=[qspec, kspec, vspec, vec, qspec], out_specs=[qspec, kvout, kvout],
        out_shape=[full, full, full],
        scratch_shapes=[pltpu.VMEM((S, PAIR), F32)] * 2,
        compiler_params=_cparams(("parallel", "arbitrary")),
    )(proj, proj, proj, tot, do)


def _fox_logits(qs, k, cq, ck, cmr, shift):
    s = _dot_nt(qs, k) + cq - ck
    if cmr is None:
        return s, None
    causal = cmr <= shift
    return jnp.where(causal, s, NEG_INF), causal


def fox_fwd(proj, cq, ck, *, name):
    S = proj.shape[0]
    tq, tk = _att_tiles(S)
    qspec, kspec, vspec, _, vec = _pair_specs(S, tq)
    ckspec = pl.BlockSpec((2, S // tk, 1, tk), lambda p, i: (p, 0, 0, 0))

    def body(q_ref, k_ref, v_ref, cq_ref, ck_ref, o_ref, lse_ref):
        i = pl.program_id(1)
        row, col = _blk_iota(tq, tk)
        cmr = col - row
        out = []
        for h in range(2):
            sel = _head_lanes(h)
            qs, cqv = _only(sel, q_ref[...] * ATTN_SCALE), cq_ref[h]

            def step(kb, carry, masked):
                m, l, acc = carry
                ks = pl.multiple_of(kb * tk, tk)
                s, _ = _fox_logits(qs, k_ref[pl.ds(ks, tk), :], cqv, ck_ref[h, kb], cmr if masked else None, i * tq - kb * tk)
                m_new = jnp.maximum(m, jnp.max(s, axis=1, keepdims=True))
                alpha = jnp.exp(m - m_new)
                p = jnp.exp(s - m_new)
                l = alpha * l + jnp.sum(p, axis=1, keepdims=True)
                acc = alpha * acc + _dot(p.astype(BF16), _only(sel, v_ref[pl.ds(ks, tk), :]))
                return m_new, l, acc

            m, l, acc = lax.fori_loop(0, (i + 1) * (tq // tk), lambda kb, c: step(kb, c, True),
                                      (jnp.full((tq, 1), NEG_INF, F32), jnp.zeros((tq, 1), F32), jnp.zeros((tq, PAIR), F32)))
            lse_ref[h] = m + jnp.log(l)
            out.append(acc / l)
        o_ref[...] = (out[0] + out[1]).astype(o_ref.dtype)

    return pl.pallas_call(
        body, name=name, grid=(N_PAIRS, S // tq),
        in_specs=[qspec, kspec, vspec, vec, ckspec], out_specs=[qspec, vec],
        out_shape=[jax.ShapeDtypeStruct((S, D_MODEL), BF16), jax.ShapeDtypeStruct((N_HEADS, S, 1), F32)],
        compiler_params=_cparams(("parallel", "arbitrary")),
    )(proj, proj, proj, cq, ck)


def fox_bwd(proj, o, lse, cq, ck, do, *, name):
    S = proj.shape[0]
    tq, tk = _att_tiles(S)
    qspec, kspec, vspec, kvout, vec = _pair_specs(S, tq)
    ckspec = pl.BlockSpec((2, S // tk, 1, tk), lambda p, i: (p, 0, 0, 0))

    def body(q_ref, k_ref, v_ref, o_ref, lse_ref, cq_ref, ck_ref, do_ref, dq_ref, dk_out, dv_out, dcq_ref, dck_ref,
             dk_ref, dv_ref):
        i = pl.program_id(1)

        @pl.when(i == 0)
        def _():
            dk_ref[...] = jnp.zeros_like(dk_ref)
            dv_ref[...] = jnp.zeros_like(dv_ref)
            dck_ref[...] = jnp.zeros_like(dck_ref)

        row, col = _blk_iota(tq, tk)
        cmr = col - row
        out = []
        for h in range(2):
            sel = _head_lanes(h)
            qs, dov, cqv, lsev = _only(sel, q_ref[...] * ATTN_SCALE), _only(sel, do_ref[...]), cq_ref[h], lse_ref[h]
            delta = jnp.sum(dov.astype(F32) * o_ref[...].astype(F32), axis=1, keepdims=True)

            def step(kb, carry, masked):
                dq, dcq = carry
                ks = pl.multiple_of(kb * tk, tk)
                kv = k_ref[pl.ds(ks, tk), :]
                s, causal = _fox_logits(qs, kv, cqv, ck_ref[h, kb], cmr if masked else None, i * tq - kb * tk)
                p = _keep(causal, jnp.exp(s - lsev))
                ds = p * (_dot_nt(dov, v_ref[pl.ds(ks, tk), :]) - delta)
                dck_ref[h, kb] += jnp.sum(ds, axis=0, keepdims=True)
                dsb = ds.astype(BF16)
                dk_ref[pl.ds(ks, tk), :] += _dot_tn(dsb, qs)
                dv_ref[pl.ds(ks, tk), :] += _dot_tn(p.astype(BF16), dov)
                return dq + _dot(dsb, _only(sel, kv)), dcq + jnp.sum(ds, axis=1, keepdims=True)

            nlow = i * (tq // tk)
            carry = lax.fori_loop(0, nlow, lambda kb, c: step(kb, c, False),
                                  (jnp.zeros((tq, PAIR), F32), jnp.zeros((tq, 1), F32)))
            dq, dcq = lax.fori_loop(nlow, nlow + tq // tk, lambda kb, c: step(kb, c, True), carry)
            dcq_ref[h] = dcq
            out.append(dq)
        dq_ref[...] = ((out[0] + out[1]) * ATTN_SCALE).astype(dq_ref.dtype)

        @pl.when(i == S // tq - 1)
        def _():
            dk_out[...] = dk_ref[...].astype(dk_out.dtype)
            dv_out[...] = dv_ref[...].astype(dv_out.dtype)

    full = jax.ShapeDtypeStruct((S, D_MODEL), BF16)
    return pl.pallas_call(
        body, name=name, grid=(N_PAIRS, S // tq),
        in_specs=[qspec, kspec, vspec, qspec, vec, vec, ckspec, qspec],
        out_specs=[qspec, kvout, kvout, vec, ckspec],
        out_shape=[full, full, full, jax.ShapeDtypeStruct((N_HEADS, S, 1), F32),
                   jax.ShapeDtypeStruct((N_HEADS, S // tk, 1, tk), F32)],
        scratch_shapes=[pltpu.VMEM((S, PAIR), F32)] * 2,
        compiler_params=_cparams(("parallel", "arbitrary")),
    )(proj, proj, proj, o, lse, cq, ck, do)


def _swa_specs(S, tq):
    qspec = pl.BlockSpec((None, SWA_GROUP, tq, HEAD_DIM), lambda g, i: (g, 0, i, 0))
    kvspec = pl.BlockSpec((None, S + SWA_WINDOW, HEAD_DIM), lambda g, i: (g, 0, 0))
    vec = pl.BlockSpec((None, SWA_GROUP, tq, 1), lambda g, i: (g, 0, i, 0))
    sink = pl.BlockSpec((None, SWA_GROUP * tq, 1), lambda g, i: (g, 0, 0))
    return qspec, kvspec, vec, sink


def _swa_logits(q2, kw, i, tq):
    rows = q2.shape[0]
    r = lax.broadcasted_iota(jnp.int32, (rows, 2 * tq), 0)
    c = lax.broadcasted_iota(jnp.int32, (rows, 2 * tq), 1)
    diff = (r & (tq - 1)) + tq - c
    ok = (diff >= 0) & (diff < SWA_WINDOW) & (c + (i - 1) * tq >= 0)
    return jnp.where(ok, _dot_nt(q2, kw) * ATTN_SCALE, NEG_INF), ok


def swa_fwd(q, kp, vp, sink, *, name):
    _, G, S, _ = q.shape
    tq = ATT_BLK
    qspec, kvspec, vec, sinkspec = _swa_specs(S, tq)

    def body(q_ref, k_ref, v_ref, s_ref, o_ref, lse_ref):
        i = pl.program_id(1)
        q2 = q_ref[...].reshape(G * tq, HEAD_DIM)
        ws = pl.multiple_of(i * tq, tq)
        logits, _ = _swa_logits(q2, k_ref[pl.ds(ws, 2 * tq), :], i, tq)
        sk = s_ref[...]
        m = jnp.maximum(jnp.max(logits, axis=1, keepdims=True), sk)
        e = jnp.exp(logits - m)
        den = jnp.sum(e, axis=1, keepdims=True) + jnp.exp(sk - m)
        o = _dot((e / den).astype(BF16), v_ref[pl.ds(ws, 2 * tq), :])
        o_ref[...] = o.reshape(G, tq, HEAD_DIM).astype(o_ref.dtype)
        lse_ref[...] = (m + jnp.log(den)).reshape(G, tq, 1)

    return pl.pallas_call(
        body, name=name, grid=(SWA_KV_HEADS, S // tq),
        in_specs=[qspec, kvspec, kvspec, sinkspec], out_specs=[qspec, vec],
        out_shape=[jax.ShapeDtypeStruct(q.shape, BF16), jax.ShapeDtypeStruct((SWA_KV_HEADS, G, S, 1), F32)],
        compiler_params=_cparams(("parallel", "arbitrary")),
    )(q, kp, vp, sink)


def swa_bwd(q, kp, vp, sink, o, lse, do, *, name):
    _, G, S, _ = q.shape
    tq = ATT_BLK
    qspec, kvspec, vec, sinkspec = _swa_specs(S, tq)

    def body(q_ref, k_ref, v_ref, s_ref, o_ref, lse_ref, do_ref, dq_ref, dk_ref, dv_ref, dsink_ref):
        i = pl.program_id(1)

        @pl.when(i == 0)
        def _():
            dk_ref[...] = jnp.zeros_like(dk_ref)
            dv_ref[...] = jnp.zeros_like(dv_ref)

        q2 = q_ref[...].reshape(G * tq, HEAD_DIM)
        do2 = do_ref[...].reshape(G * tq, HEAD_DIM)
        o2 = o_ref[...].reshape(G * tq, HEAD_DIM)
        lse2 = lse_ref[...].reshape(G * tq, 1)
        ws = pl.multiple_of(i * tq, tq)
        kw = k_ref[pl.ds(ws, 2 * tq), :]
        vw = v_ref[pl.ds(ws, 2 * tq), :]
        logits, ok = _swa_logits(q2, kw, i, tq)
        p = jnp.where(ok, jnp.exp(logits - lse2), 0.0)
        delta = jnp.sum(do2.astype(F32) * o2.astype(F32), axis=1, keepdims=True)
        ds = p * (_dot_nt(do2, vw) - delta)
        dsb = ds.astype(BF16)
        dq_ref[...] = (_dot(dsb, kw) * ATTN_SCALE).reshape(G, tq, HEAD_DIM)
        dk_ref[pl.ds(ws, 2 * tq), :] += _dot_tn(dsb, q2) * ATTN_SCALE
        dv_ref[pl.ds(ws, 2 * tq), :] += _dot_tn(p.astype(BF16), do2)
        dsink_ref[...] = (-jnp.exp(s_ref[...] - lse2) * delta).reshape(G, tq, 1)

    kvshape = jax.ShapeDtypeStruct(kp.shape, F32)
    return pl.pallas_call(
        body, name=name, grid=(SWA_KV_HEADS, S // tq),
        in_specs=[qspec, kvspec, kvspec, sinkspec, qspec, vec, qspec],
        out_specs=[qspec, kvspec, kvspec, vec],
        out_shape=[jax.ShapeDtypeStruct(q.shape, F32), kvshape, kvshape,
                   jax.ShapeDtypeStruct((SWA_KV_HEADS, G, S, 1), F32)],
        compiler_params=_cparams(("parallel", "arbitrary")),
    )(q, kp, vp, sink, o, lse, do)


def _adamw_tile(w, g, m, v):
    m = ADAM_B1 * m + (1.0 - ADAM_B1) * g
    v = ADAM_B2 * v + (1.0 - ADAM_B2) * (g * g)
    m_hat = m / (1.0 - ADAM_B1 ** ADAM_STEP)
    v_hat = v / (1.0 - ADAM_B2 ** ADAM_STEP)
    delta = -ADAM_LR * (m_hat / (jnp.sqrt(v_hat) + ADAM_EPS) + ADAM_WD * w)
    return g, delta, m, v


def adamw(gfull, t, w, m, v, after, *, name):
    off, K, ns, _ = t
    sb = off // K
    nat = pl.BlockSpec((K, SLAB), lambda q: (0, q))

    def body(g_ref, w_ref, m_ref, v_ref, after_ref, *outs):
        del after_ref
        for o, r in zip(outs, _adamw_tile(w_ref[...], g_ref[...], m_ref[...], v_ref[...])):
            o[...] = r

    return pl.pallas_call(
        body, name=name, grid=(ns,),
        in_specs=[pl.BlockSpec((K, SLAB), lambda q: (sb + q, 0)), nat, nat, nat, HBM],
        out_specs=[nat] * 4, out_shape=[jax.ShapeDtypeStruct(w.shape, F32)] * 4,
        compiler_params=_cparams(("parallel",)),
    )(gfull, w, m, v, after)


def adamw_small(g, w, m, v, *, name):
    rows, d = w.shape

    def body(g_ref, w_ref, m_ref, v_ref, *outs):
        for j in range(rows):
            one = pl.ds(j, 1)
            for k, r in enumerate(_adamw_tile(w_ref[one, :], g_ref[one, :], m_ref[one, :], v_ref[one, :])):
                outs[k * rows + j][...] = r

    flat = pl.pallas_call(body, name=name, out_shape=[jax.ShapeDtypeStruct((1, d), F32)] * (4 * rows))(g, w, m, v)
    return [flat[k * rows:(k + 1) * rows] for k in range(4)]


MESH = pl.DeviceIdType.MESH
HBM = pl.BlockSpec(memory_space=pl.ANY)


def _place():
    x, y, c = lax.axis_index("x"), lax.axis_index("y"), lax.axis_index("c")
    others = [(1 - x, y), (x, 1 - y), (1 - x, 1 - y)]
    return x, y, c, others


def _rcopy(src, dst, send_sems, recv_sems, k, to):
    return pltpu.make_async_remote_copy(src_ref=src, dst_ref=dst, send_sem=send_sems.at[k], recv_sem=recv_sems.at[k],
                                        device_id=to, device_id_type=MESH)


def _dma_sems(*counts):
    return [pltpu.SemaphoreType.DMA((n,)) for n in counts]


DMA_UNIT_ROWS = 128
DMA_PIECES = 4


def _row_pieces(h, n):
    units = h // DMA_UNIT_ROWS
    n = min(n, units)
    base, extra = divmod(units, n)
    sizes = [(base + (k < extra)) * DMA_UNIT_ROWS for k in range(n)]
    return [(sum(sizes[:k]), sizes[k]) for k in range(n)]


SEM = pl.BlockSpec(memory_space=pltpu.SEMAPHORE)
SPLIT_COPY = pltpu.CompilerParams(has_side_effects=pltpu.SideEffectType.DATAFLOW_SIDE_EFFECTING)
N_OTHERS = 3


def _hbm(a):
    return pltpu.with_memory_space_constraint(a, pltpu.HBM)


def _chip_rows(buf, chip, s0, sz):
    return buf.at[2 * chip[0] + chip[1], pl.ds(s0, sz)]


def allgather_start(bufs, *, name):
    n = len(bufs)

    def body(*refs):
        ins, send, recv, token = refs[:n], refs[n:2 * n], refs[2 * n:3 * n], refs[4 * n]
        x, y, c, others = _place()
        for i in range(n):
            h = bufs[i].shape[1] // 2
            for f, chip in enumerate(others):
                for s0, sz in _row_pieces(h, DMA_PIECES):
                    mine = _chip_rows(ins[i], (x, y), c * h + s0, sz)
                    _rcopy(mine, mine, send[i], recv[i], f, (*chip, c)).start()
        token[...] = jnp.zeros_like(token)

    res = pl.pallas_call(
        body, name=name, in_specs=[HBM] * n,
        out_specs=[SEM] * (2 * n) + [HBM] * n + [pl.BlockSpec(memory_space=pltpu.VMEM)],
        out_shape=[pltpu.SemaphoreType.DMA((N_OTHERS,))] * (2 * n) + [pltpu.HBM(b.shape, b.dtype) for b in bufs]
        + [jax.ShapeDtypeStruct((1, D_MODEL), F32)],
        input_output_aliases={i: 2 * n + i for i in range(n)},
        compiler_params=SPLIT_COPY,
    )(*[_hbm(b) for b in bufs])
    return res[:n], res[n:2 * n], res[2 * n:3 * n], res[3 * n]


def allgather_wait(buf, send, recv, after, *, name):
    h = buf.shape[1] // 2
    after = list(after)

    def body(buf_ref, send_sems, recv_sems, *rest):
        del rest
        x, y, c, others = _place()
        for f, chip in enumerate(others):
            mine = _chip_rows(buf_ref, (x, y), c * h, h)
            theirs = _chip_rows(buf_ref, chip, c * h, h)
            cp = _rcopy(mine, theirs, send_sems, recv_sems, f, (*chip, c))
            cp.wait_send()
            cp.wait_recv()

    return pl.pallas_call(
        body, name=name, in_specs=[HBM, SEM, SEM] + [HBM] * len(after), out_specs=HBM,
        out_shape=pltpu.HBM(buf.shape, buf.dtype), input_output_aliases={0: 0},
        compiler_params=SPLIT_COPY,
    )(buf, send, recv, *after)


def forward_start(buf, *, name):
    h = buf.shape[1] // 2

    def body(b_ref, send, recv, b_out, token):
        del b_out
        x, y, c, others = _place()
        for f, chip in enumerate(others):
            for s0, sz in _row_pieces(h, DMA_PIECES):
                rows = _chip_rows(b_ref, chip, c * h + s0, sz)
                _rcopy(rows, rows, send, recv, f, (x, y, 1 - c)).start()
        token[...] = jnp.zeros_like(token)

    return pl.pallas_call(
        body, name=name, in_specs=[HBM],
        out_specs=[SEM, SEM, HBM, pl.BlockSpec(memory_space=pltpu.VMEM)],
        out_shape=[pltpu.SemaphoreType.DMA((N_OTHERS,))] * 2 + [pltpu.HBM(buf.shape, buf.dtype),
                                                                 jax.ShapeDtypeStruct((1, D_MODEL), F32)],
        input_output_aliases={0: 2}, compiler_params=SPLIT_COPY,
    )(_hbm(buf))


def forward_wait(buf, send, recv, after, *, name):
    h = buf.shape[1] // 2

    def body(b_ref, send_sems, recv_sems, after_ref, b_out):
        del after_ref, b_out
        x, y, c, others = _place()
        for f, chip in enumerate(others):
            cp = _rcopy(_chip_rows(b_ref, chip, c * h, h), _chip_rows(b_ref, chip, (1 - c) * h, h),
                        send_sems, recv_sems, f, (x, y, 1 - c))
            cp.wait_send()
            cp.wait_recv()

    return pl.pallas_call(
        body, name=name, in_specs=[HBM, SEM, SEM, HBM], out_specs=HBM,
        out_shape=pltpu.HBM(buf.shape, buf.dtype), input_output_aliases={0: 0}, compiler_params=SPLIT_COPY,
    )(buf, send, recv, after)


def _sem1():
    return pltpu.SemaphoreType.DMA((1,))


TOKEN = jax.ShapeDtypeStruct((1, D_MODEL), F32)
VMEM_SPEC = pl.BlockSpec(memory_space=pltpu.VMEM)


def swap_start(grad, *, name):
    h = grad.shape[1] // 2

    def body(g_ref, land_ref, send, recv, g_out, land_out, token):
        del g_out, land_out
        x, y, c, _ = _place()
        for k in range(N_CHIPS):
            for s0, sz in _row_pieces(h, DMA_PIECES):
                _rcopy(g_ref.at[k, pl.ds((1 - c) * h + s0, sz)], land_ref.at[k, pl.ds(s0, sz)], send, recv, 0, (x, y, 1 - c)).start()
        token[...] = jnp.zeros_like(token)

    land = lax.empty((N_CHIPS, h, SLAB), grad.dtype)
    return pl.pallas_call(
        body, name=name, in_specs=[HBM, HBM], out_specs=[SEM, SEM, HBM, HBM, VMEM_SPEC],
        out_shape=[_sem1(), _sem1(), pltpu.HBM(grad.shape, grad.dtype), pltpu.HBM(land.shape, land.dtype), TOKEN],
        input_output_aliases={0: 2, 1: 3}, compiler_params=SPLIT_COPY,
    )(_hbm(grad), _hbm(land))


def swap_wait(grad, land, send, recv, after, *, name):
    h = land.shape[1]

    def body(g_ref, land_ref, send_sems, recv_sems, after_ref, g_out, land_out):
        del after_ref, g_out, land_out
        x, y, c, _ = _place()
        cp = _rcopy(g_ref.at[:, pl.ds((1 - c) * h, h)], land_ref, send_sems, recv_sems, 0, (x, y, 1 - c))
        cp.wait_send()
        cp.wait_recv()

    return pl.pallas_call(
        body, name=name, in_specs=[HBM, HBM, SEM, SEM, HBM], out_specs=[HBM, HBM],
        out_shape=[pltpu.HBM(grad.shape, grad.dtype), pltpu.HBM(land.shape, land.dtype)],
        input_output_aliases={0: 0, 1: 1}, compiler_params=SPLIT_COPY,
    )(grad, land, send, recv, after)


def scatter_start(part, *, name):
    h = part.shape[1]

    def body(part_ref, land_ref, send, recv, part_out, land_out, token):
        del part_out, land_out
        x, y, c, others = _place()
        for f, chip in enumerate(others):
            for s0, sz in _row_pieces(h, DMA_PIECES):
                _rcopy(_chip_rows(part_ref, chip, s0, sz), land_ref.at[f, pl.ds(s0, sz)], send, recv, f, (*chip, c)).start()
        token[...] = jnp.zeros_like(token)

    land = lax.empty((N_OTHERS,) + part.shape[1:], part.dtype)
    return pl.pallas_call(
        body, name=name, in_specs=[HBM, HBM],
        out_specs=[SEM, SEM, HBM, HBM, pl.BlockSpec(memory_space=pltpu.VMEM)],
        out_shape=[pltpu.SemaphoreType.DMA((N_OTHERS,))] * 2 + [pltpu.HBM(part.shape, part.dtype), pltpu.HBM(land.shape, land.dtype),
                                                                 jax.ShapeDtypeStruct((1, D_MODEL), F32)],
        input_output_aliases={0: 2, 1: 3},
        compiler_params=SPLIT_COPY,
    )(_hbm(part), _hbm(land))


def scatter_wait(part, land, send, recv, after, *, name):
    h = part.shape[1]

    def body(part_ref, land_ref, send_sems, recv_sems, after_ref, part_out, land_out):
        del after_ref, part_out, land_out
        x, y, c, others = _place()
        for f, chip in enumerate(others):
            cp = _rcopy(_chip_rows(part_ref, chip, 0, h), land_ref.at[f], send_sems, recv_sems, f, (*chip, c))
            cp.wait_send()
            cp.wait_recv()

    return pl.pallas_call(
        body, name=name, in_specs=[HBM, HBM, SEM, SEM, HBM], out_specs=[HBM, HBM],
        out_shape=[pltpu.HBM(part.shape, part.dtype), pltpu.HBM(land.shape, land.dtype)],
        input_output_aliases={0: 0, 1: 1},
        compiler_params=SPLIT_COPY,
    )(part, land, send, recv, after)


def join_start(buf, *, name):
    h = buf.shape[0] // 2

    def body(b_ref, send, recv, b_out, token):
        del b_out
        x, y, c, _ = _place()
        for s0, sz in _row_pieces(h, 2 * DMA_PIECES):
            rows = b_ref.at[pl.ds(c * h + s0, sz)]
            _rcopy(rows, rows, send, recv, 0, (x, y, 1 - c)).start()
        token[...] = jnp.zeros_like(token)

    return pl.pallas_call(
        body, name=name, in_specs=[HBM], out_specs=[SEM, SEM, HBM, VMEM_SPEC],
        out_shape=[_sem1(), _sem1(), pltpu.HBM(buf.shape, buf.dtype), TOKEN],
        input_output_aliases={0: 2}, compiler_params=SPLIT_COPY,
    )(_hbm(buf))


def join_wait(buf, send, recv, after, *, name):
    h = buf.shape[0] // 2

    def body(b_ref, send_sems, recv_sems, after_ref, b_out):
        del after_ref, b_out
        x, y, c, _ = _place()
        cp = _rcopy(b_ref.at[pl.ds(c * h, h)], b_ref.at[pl.ds((1 - c) * h, h)], send_sems, recv_sems, 0, (x, y, 1 - c))
        cp.wait_send()
        cp.wait_recv()

    return pl.pallas_call(
        body, name=name, in_specs=[HBM, SEM, SEM, HBM], out_specs=HBM,
        out_shape=pltpu.HBM(buf.shape, buf.dtype), input_output_aliases={0: 0}, compiler_params=SPLIT_COPY,
    )(buf, send, recv, after)


def allreduce_small(v, *, name):
    rows, n = v.shape

    def body(x_ref, sum_ref, all_ref, send_sems, recv_sems, local_sem):
        x, y, c, others = _place()
        me, sibling = (x, y, c), (x, y, 1 - c)

        def blk(px, py, pc):
            return all_ref.at[pl.ds((4 * px + 2 * py + pc) * rows, rows), :]

        def copy(k, block, to, src=None):
            return _rcopy(blk(*block) if src is None else src, blk(*block), send_sems, recv_sems, k, to)

        mine = pltpu.make_async_copy(x_ref, blk(*me), local_sem)
        mine.start()
        first = [copy(0, me, sibling, src=x_ref)]
        first += [copy(1 + f, me, (*chip, c), src=x_ref) for f, chip in enumerate(others)]
        for cp in first:
            cp.start()
        passed = [copy(4 + f, (*chip, c), sibling) for f, chip in enumerate(others)]
        for f, chip in enumerate(others):
            copy(1 + f, (*chip, c), me).wait_recv()
            passed[f].start()
        copy(0, sibling, me).wait_recv()
        for f, chip in enumerate(others):
            copy(4 + f, (*chip, 1 - c), me).wait_recv()
        for cp in first + passed:
            cp.wait_send()
        mine.wait()
        acc = all_ref[pl.ds(0, rows), :]
        for d in range(1, N_DEVICES):
            acc = acc + all_ref[pl.ds(d * rows, rows), :]
        sum_ref[...] = acc

    vm = pl.BlockSpec(memory_space=pltpu.VMEM)
    return pl.pallas_call(
        body, name=name, in_specs=[vm], out_specs=[vm, vm],
        out_shape=[jax.ShapeDtypeStruct((rows, n), F32), jax.ShapeDtypeStruct((N_DEVICES * rows, n), F32)],
        scratch_shapes=_dma_sems(7, 7) + [pltpu.SemaphoreType.DMA],
    )(v)[0]


def add_pairs(grad, theirs, where, *, name):
    h = theirs.shape[1]
    spec = pl.BlockSpec((None, h, SLAB), lambda k, w: (k, 0, 0))

    def body(w_ref, a_ref, b_ref, o_ref):
        del w_ref
        o_ref[...] = (a_ref[...].astype(F32) + b_ref[...].astype(F32)).astype(o_ref.dtype)

    return pl.pallas_call(
        body, name=name,
        grid_spec=pltpu.PrefetchScalarGridSpec(
            num_scalar_prefetch=1, grid=(N_CHIPS,),
            in_specs=[pl.BlockSpec((None, h, SLAB), lambda k, w: (k, w[1], 0)), spec], out_specs=spec),
        out_shape=jax.ShapeDtypeStruct(theirs.shape, theirs.dtype),
        compiler_params=_cparams(("parallel",)))(where, grad, theirs)


def add_chips(pair, got, where, *, name):
    h = pair.shape[1]
    tr = h // 2

    def body(w_ref, a_ref, b_ref, o_ref):
        del w_ref
        acc = a_ref[...].astype(F32)
        for f in range(3):
            acc = acc + b_ref[f].astype(F32)
        o_ref[...] = acc

    return pl.pallas_call(
        body, name=name,
        grid_spec=pltpu.PrefetchScalarGridSpec(
            num_scalar_prefetch=1, grid=(2,),
            in_specs=[pl.BlockSpec((None, tr, SLAB), lambda i, w: (w[0], i, 0)),
                      pl.BlockSpec((3, tr, SLAB), lambda i, w: (0, i, 0))],
            out_specs=pl.BlockSpec((tr, SLAB), lambda i, w: (2 * w[1] + i, 0))),
        out_shape=jax.ShapeDtypeStruct((2 * h, SLAB), F32),
        compiler_params=_cparams(("parallel",)))(where, pair, got)


DEPTH = 4
MIXER = (0, 1, 2, 0)
W_IN_COLS = (768, 320, 772)
W_IN_PAD = (768, 512, 1024)
MATS = ("up", "down", "inp", "out", "gate", "proj")
MAT_ARG = dict(up="w_up", down="w_down", inp="w_in", out="w_out", gate="w_ple_gate", proj="w_ple_proj")
GAINS = ("attn_norm", "mlp_norm", "ple_norm")
N_SMALL = 16
KINDS = ("grad_", "delta_", "new_m_", "new_v_")


def _layout(kind):
    ns_in = W_IN_PAD[kind] // SLAB
    off = 8192 + 1024 * ns_in
    lay = dict(up=(0, 1024, 4, False), down=(4096, 1024, 4, True), inp=(8192, 1024, ns_in, False),
               out=(off, 256, 4, True), gate=(off + 1024, 256, 4, True), proj=(off + 2048, 256, 1, False))
    return lay, off + 2304


def _to_slabs(w):
    k, c = w.shape
    return w.reshape(k, c // SLAB, SLAB).transpose(1, 0, 2).reshape(-1, SLAB)


def _pad_cols(w, n):
    return jnp.pad(w, ((0, 0), (0, n - w.shape[1])))


def _heads(x2d, n):
    return x2d.reshape(x2d.shape[0], n, HEAD_DIM).transpose(1, 0, 2)


def _unheads(x3d):
    n, s, _ = x3d.shape
    return x3d.transpose(1, 0, 2).reshape(s, n * HEAD_DIM)


def _chip_cols(x2d, c, cpad):
    return jnp.concatenate([_pad_cols(x2d[:, j * c:(j + 1) * c], cpad) for j in range(N_CHIPS)], axis=1)


def _unchip_cols(x2d, c, cpad):
    return jnp.concatenate([x2d[:, j * cpad:j * cpad + c] for j in range(N_CHIPS)], axis=1)


def _forget_cols(wg, t):
    off, K, _, _ = t
    cols = []
    for g in range(3 * N_HEADS * HEAD_DIM, 3 * N_HEADS * HEAD_DIM + N_HEADS):
        chip, local = divmod(g, W_IN_COLS[2])
        q, c = divmod(local, SLAB)
        cols.append(wg[chip, off + q * K:off + (q + 1) * K, c:c + 1])
    return jnp.concatenate(cols, axis=1)


def _add_res(acc, res):
    return (acc + res,)


def _relu2(acc):
    return acc, jnp.square(jnp.maximum(acc, 0.0))


def _relu2_bwd(acc, u):
    return (acc * (2.0 * jnp.maximum(u.astype(F32), 0.0)),)


def _ple_fwd(acc, x2, pp):
    return x2 + pp * _sigmoid(acc), acc


def _ple_bwd(dx, pp, gl):
    gate = _sigmoid(gl)
    return dx * gate, dx * pp * gate * (1.0 - gate)


def _layer_fwd(i, kind, x0, p_bf, wg, lay, gains, extra, tabs, mid):
    s = x0.shape[0]
    an, mn, pn = gains
    sv = dict(x0=x0)
    if kind == 0:
        proj, h1 = mm_nn(x0, wg, lay["inp"], name=f"w_in_{i}", norm_gain=an)
        a, tot = sb_fwd(proj, name=f"sb_fwd_{i}")
        sv.update(proj=proj, tot=tot)
    elif kind == 1:
        projp, h1 = mm_nn(x0, wg, lay["inp"], name=f"w_in_{i}", out_dtypes=(F32,), norm_gain=an)
        proj = _unchip_cols(projp, W_IN_COLS[1], W_IN_PAD[1])
        nq = N_HEADS * HEAD_DIM
        nqk = nq + SWA_KV_HEADS * HEAD_DIM
        qk = rope_fwd(proj[:, :nqk], tabs, name=f"rope_{i}")
        q = _heads(qk[:, :nq], N_HEADS).reshape(SWA_KV_HEADS, SWA_GROUP, s, HEAD_DIM)
        front = ((0, 0), (SWA_WINDOW, 0), (0, 0))
        kp = jnp.pad(_heads(qk[:, nq:], SWA_KV_HEADS), front)
        vp = jnp.pad(_heads(proj[:, nqk:].astype(BF16), SWA_KV_HEADS), front)
        sink = jnp.repeat(extra.reshape(SWA_KV_HEADS, SWA_GROUP), ATT_BLK, axis=1)[:, :, None]
        o4, lse = swa_fwd(q, kp, vp, sink, name=f"swa_fwd_{i}")
        a = _unheads(o4.reshape(N_HEADS, s, HEAD_DIM))
        sv.update(q=q, kp=kp, vp=vp, sink=sink, o4=o4, lse=lse)
    else:
        projp, h1 = mm_nn(x0, wg, lay["inp"], name=f"w_in_{i}", norm_gain=an)
        nqkv = 3 * N_HEADS * HEAD_DIM
        proj = _unchip_cols(projp, W_IN_COLS[2], W_IN_PAD[2])[:, :nqkv]
        fl = mm_plain(h1, _pad_cols(_forget_cols(wg, lay["inp"]), 128), name=f"w_forget_{i}")
        bp = _pad_cols(extra[None], 128)
        cum_t = fox_gate_fwd(fl, bp, name=f"gate_fwd_{i}")[:, :N_HEADS].T
        cq = cum_t[:, :, None]
        ck = cum_t.reshape(N_HEADS, s // min(ATT_BK, s), 1, min(ATT_BK, s))
        a, lse = fox_fwd(proj, cq, ck, name=f"fox_fwd_{i}")
        sv.update(proj=proj, fl=fl, bp=bp, cq=cq, ck=ck, lse=lse)
    zero = mid(a)
    if zero is not None:
        mn = mn + zero
    x1 = mm_nn(a, wg, lay["out"], name=f"w_out_{i}", epi=_add_res, extras=(x0,), out_dtypes=(F32,))[0]
    u, r, h2 = mm_nn(x1, wg, lay["up"], name=f"w_up_{i}", epi=_relu2, out_dtypes=(BF16, BF16), norm_gain=mn)
    x2 = mm_nn(r, wg, lay["down"], name=f"w_down_{i}", epi=_add_res, extras=(x1,), out_dtypes=(F32,))[0]
    pp = mm_nn(p_bf, wg, lay["proj"], name=f"w_ple_proj_{i}", out_dtypes=(F32,))[0]
    x3, gl, h3 = mm_nn(x2, wg, lay["gate"], name=f"w_ple_gate_{i}", epi=_ple_fwd, extras=(x2, pp), out_dtypes=(F32, F32),
                       norm_gain=pn)
    sv.update(h1=h1, a=a, x1=x1, h2=h2, u=u, r=r, x2=x2, h3=h3, pp=pp, gl=gl)
    return x3, sv


def _layer_bwd(i, kind, dx3, sv, p_bf, wg, lay, n_rows, gains, tabs, mid, end):
    s = dx3.shape[0]
    an, mn, pn = gains
    g = lax.empty((N_CHIPS, n_rows, SLAB), BF16)
    d_pp, d_gl = ew(_ple_bwd, [dx3, sv["pp"], sv["gl"]], [BF16, BF16], name=f"ple_bwd_{i}")
    g = mm_tn(p_bf, d_pp, g, lay["proj"], name=f"dw_ple_proj_{i}")
    g = mm_tn(sv["h3"], d_gl, g, lay["gate"], name=f"dw_ple_gate_{i}")
    d_h3 = mm_nt(d_gl, wg, lay["gate"], name=f"dx_ple_gate_{i}", out_dtypes=(F32,))[0]
    dx2, dx2b, d_pn = rms_bwd(sv["x2"], pn, d_h3, dx3, name=f"ple_norm_bwd_{i}")
    zero = mid(dx2)
    if zero is not None:
        mn = mn + zero
    g = mm_tn(sv["r"], dx2b, g, lay["down"], name=f"dw_down_{i}")
    d_u = mm_nt(dx2b, wg, lay["down"], name=f"dx_down_{i}", epi=_relu2_bwd, extras=(sv["u"],))[0]
    g = mm_tn(sv["h2"], d_u, g, lay["up"], name=f"dw_up_{i}")
    dx1, dx1b, d_mn = mm_nt(d_u, wg, lay["up"], name=f"dx_up_{i}", rms=(sv["x1"], mn, dx2))
    g = mm_tn(sv["a"], dx1b, g, lay["out"], name=f"dw_out_{i}")
    d_a = mm_nt(dx1b, wg, lay["out"], name=f"dx_out_{i}")[0]
    d_extra = None
    if kind == 0:
        d_proj = jnp.concatenate(sb_bwd(sv["proj"], sv["tot"], d_a, name=f"sb_bwd_{i}"), axis=1)
    elif kind == 1:
        do4 = _heads(d_a, N_HEADS).reshape(SWA_KV_HEADS, SWA_GROUP, s, HEAD_DIM)
        dq, dkp, dvp, dsr = swa_bwd(sv["q"], sv["kp"], sv["vp"], sv["sink"], sv["o4"], sv["lse"], do4, name=f"swa_bwd_{i}")
        dqk = jnp.concatenate([_unheads(dq.reshape(N_HEADS, s, HEAD_DIM)), _unheads(dkp[:, SWA_WINDOW:])], axis=1)
        dqk = rope_bwd(dqk, tabs, name=f"rope_bwd_{i}")
        d_proj = jnp.concatenate([dqk, _unheads(dvp[:, SWA_WINDOW:]).astype(BF16)], axis=1)
        d_proj = _chip_cols(d_proj, W_IN_COLS[1], W_IN_PAD[1])
        d_extra = jnp.sum(dsr[..., 0], axis=2).reshape(N_HEADS)
    else:
        dq, dk, dv, dcq, dck = fox_bwd(sv["proj"], sv["a"], sv["lse"], sv["cq"], sv["ck"], d_a, name=f"fox_bwd_{i}")
        dcum = _pad_cols((dcq[:, :, 0] - dck.reshape(N_HEADS, s)).T, 128)
        dfl, dbp = fox_gate_bwd(dcum, sv["fl"], sv["bp"], name=f"gate_bwd_{i}")
        d_proj = jnp.concatenate([dq, dk, dv, dfl[:, :N_HEADS].astype(BF16)], axis=1)
        d_proj = _chip_cols(d_proj, W_IN_COLS[2], W_IN_PAD[2])
        d_extra = dbp[0, :N_HEADS]
    g = mm_tn(sv["h1"], d_proj, g, lay["inp"], name=f"dw_in_{i}")
    zero = end(g)
    if zero is not None:
        an = an + zero
    dx0, _, d_an = mm_nt(d_proj, wg, lay["inp"], name=f"dx_in_{i}", rms=(sv["x0"], an, dx1))
    return dx0, g, (d_an, d_mn, d_pn), d_extra


def _small_rows(a, prefix):
    rows = [a[f"{prefix}{n}_{i}"] for i in range(DEPTH) for n in GAINS] + [a[f"{prefix}final_norm"]]
    rows += [_pad_cols(a[f"{prefix}{n}"][None], D_MODEL)[0] for n in ("sinks_1", "b_forget_2")]
    return jnp.stack(rows + [jnp.zeros((D_MODEL,), F32)])


def _train_step(a):
    x = a["x"][0]
    tabs = rope_tables(x.shape[0])
    lays = [_layout(k) for k in MIXER]

    padded = {}

    def natural(prefix, i, m):
        w = a[f"{prefix}{MAT_ARG[m]}_{i}"]
        if m != "inp" or w.shape[1] == W_IN_PAD[MIXER[i]]:
            return w
        if (prefix, i) not in padded:
            padded[prefix, i] = _pad_cols(w, W_IN_PAD[MIXER[i]])
        return padded[prefix, i]

    chip = 2 * lax.axis_index("x") + lax.axis_index("y")
    where = jnp.stack([chip, lax.axis_index("c")]).astype(jnp.int32)
    def own_block(i, zero):
        pk = jnp.concatenate([_to_slabs((natural("", i, m) + zero).astype(BF16)) for m in MATS], axis=0)
        return lax.dynamic_update_slice(lax.empty((N_CHIPS,) + pk.shape, BF16), pk[None], (chip, 0, 0))

    sends, recvs, bufs, token = allgather_start([own_block(0, 0.0)], name="allgather_start_0")
    more = allgather_start([own_block(i, token[0, 0]) for i in range(1, DEPTH)], name="allgather_start_1")
    sends, recvs, bufs, token = sends + more[0], recvs + more[1], bufs + more[2], more[3]

    gains = [tuple(a[f"{n}_{i}"][None] for n in GAINS) for i in range(DEPTH)]
    extras = [None, a["sinks_1"], a["b_forget_2"], None]
    p_bf = [a["p"][i, 0].astype(BF16) for i in range(DEPTH)]

    small_state = [_small_rows(a, prefix) for prefix in ("", "m_", "v_")]
    after = [token, *tabs, *p_bf, *small_state] + [natural(prefix, i, "inp") for prefix in ("m_", "v_") for i in (1, 2)]
    saved, wgs = [], []
    landed = allgather_wait(bufs[0], sends[0], recvs[0], after, name="allgather_wait_0")
    moving = dict(next=forward_start(landed, name="forward_start_0"))
    for i in range(DEPTH):
        send, recv, buf, zero = moving["next"]
        wgs.append(forward_wait(buf, send, recv, x if i else zero, name=f"forward_wait_{i}"))

        def mid(attn_out):
            if i + 1 == DEPTH:
                return None
            arrived = allgather_wait(bufs[i + 1], sends[i + 1], recvs[i + 1], [attn_out], name=f"allgather_wait_{i + 1}")
            moving["next"] = forward_start(arrived, name=f"forward_start_{i + 1}")
            return moving["next"][3]

        x, sv = _layer_fwd(i, MIXER[i], x, p_bf[i], wgs[i], lays[i][0], gains[i], extras[i], tabs, mid)
        saved.append(sv)
    dx, d_final, loss = loss_head(x, a["final_norm"][None], a["loss_target"][0], name="loss_head")

    def pair_and_scatter(j, swapped, after):
        send, recv, grad, land, _ = swapped
        grad, theirs = swap_wait(grad, land, send, recv, after, name=f"swap_wait_{j}")
        return scatter_start(add_pairs(grad, theirs, where, name=f"add_pairs_{j}"), name=f"scatter_start_{j}")

    def sum_and_join(j, scattered, after):
        send, recv, part, land, _ = scattered
        part, got = scatter_wait(part, land, send, recv, after, name=f"scatter_wait_{j}")
        return join_start(add_chips(part, got, where, name=f"add_chips_{j}"), name=f"join_start_{j}")

    small = [None] * N_SMALL
    small[12] = d_final[0]
    small[15] = _pad_cols(loss[:, :1], D_MODEL)[0]
    joined = [None] * DEPTH
    state = dict(swapped=None, scattered=None)
    for i in reversed(range(DEPTH)):
        an, mn, pn = gains[i]
        if state["swapped"] is not None:
            pn = pn + state["swapped"][4]

        def mid(dx2):
            if state["swapped"] is None:
                return None
            state["scattered"] = pair_and_scatter(i + 1, state["swapped"], dx2)
            return state["scattered"][4]

        def end(grad):
            state["swapped"] = swap_start(grad, name=f"swap_start_{i}")
            return state["swapped"][4]

        dx, _, d_gains, d_extra = _layer_bwd(i, MIXER[i], dx, saved[i], p_bf[i], wgs[i], lays[i][0], lays[i][1],
                                             (an, mn, pn), tabs, mid, end)
        for j in range(3):
            small[3 * i + j] = d_gains[j][0]
        if d_extra is not None:
            small[12 + MIXER[i]] = _pad_cols(d_extra[None], D_MODEL)[0]
        if state["scattered"] is not None:
            joined[i + 1] = sum_and_join(i + 1, state["scattered"], dx)
    started = pair_and_scatter(0, state["swapped"], dx)
    small = allreduce_small(jnp.stack(small), name="allreduce_small")

    out = {"loss": small[15, 0], "grad_x": dx[None]}
    res = adamw_small(small, *small_state, name="adamw_small")
    prev = started[4]
    for i in reversed(range(DEPTH)):
        if i == 0:
            joined[0] = sum_and_join(0, started, prev)
        send, recv, buf, zero = joined[i]
        gfull = join_wait(buf, send, recv, prev if i else zero, name=f"join_wait_{i}")
        for m in MATS:
            upd = adamw(gfull, lays[i][0][m], natural("", i, m), natural("m_", i, m), natural("v_", i, m), prev,
                        name=f"adamw_{MAT_ARG[m]}_{i}")
            prev = upd[1]
            cols = a[f"{MAT_ARG[m]}_{i}"].shape[1]
            for kd, r in zip(KINDS, upd):
                out[f"{kd}{MAT_ARG[m]}_{i}"] = r[:, :cols]
    for kd, r in zip(KINDS, res):
        for i in range(DEPTH):
            for j, n in enumerate(GAINS):
                out[f"{kd}{n}_{i}"] = r[3 * i + j][0]
        out[f"{kd}final_norm"] = r[12][0]
        out[f"{kd}sinks_1"] = r[13][0, :N_HEADS]
        out[f"{kd}b_forget_2"] = r[14][0, :N_HEADS]
    return out


def _weight_names():
    names = []
    for i in range(DEPTH):
        names += [f"attn_norm_{i}", f"w_in_{i}", f"w_out_{i}"] + [[], ["sinks_1"], ["b_forget_2"]][MIXER[i]]
        names += [f"mlp_norm_{i}", f"w_up_{i}", f"w_down_{i}", f"ple_norm_{i}", f"w_ple_gate_{i}", f"w_ple_proj_{i}"]
    return names + ["final_norm"]


def kernel(x, p, attn_norm_0, w_in_0, w_out_0, mlp_norm_0, w_up_0, w_down_0, ple_norm_0, w_ple_gate_0, w_ple_proj_0, attn_norm_1, w_in_1, w_out_1, sinks_1, mlp_norm_1, w_up_1, w_down_1, ple_norm_1, w_ple_gate_1, w_ple_proj_1, attn_norm_2, w_in_2, w_out_2, b_forget_2, mlp_norm_2, w_up_2, w_down_2, ple_norm_2, w_ple_gate_2, w_ple_proj_2, attn_norm_3, w_in_3, w_out_3, mlp_norm_3, w_up_3, w_down_3, ple_norm_3, w_ple_gate_3, w_ple_proj_3, final_norm, loss_target, m_attn_norm_0, m_w_in_0, m_w_out_0, m_mlp_norm_0, m_w_up_0, m_w_down_0, m_ple_norm_0, m_w_ple_gate_0, m_w_ple_proj_0, m_attn_norm_1, m_w_in_1, m_w_out_1, m_sinks_1, m_mlp_norm_1, m_w_up_1, m_w_down_1, m_ple_norm_1, m_w_ple_gate_1, m_w_ple_proj_1, m_attn_norm_2, m_w_in_2, m_w_out_2, m_b_forget_2, m_mlp_norm_2, m_w_up_2, m_w_down_2, m_ple_norm_2, m_w_ple_gate_2, m_w_ple_proj_2, m_attn_norm_3, m_w_in_3, m_w_out_3, m_mlp_norm_3, m_w_up_3, m_w_down_3, m_ple_norm_3, m_w_ple_gate_3, m_w_ple_proj_3, m_final_norm, v_attn_norm_0, v_w_in_0, v_w_out_0, v_mlp_norm_0, v_w_up_0, v_w_down_0, v_ple_norm_0, v_w_ple_gate_0, v_w_ple_proj_0, v_attn_norm_1, v_w_in_1, v_w_out_1, v_sinks_1, v_mlp_norm_1, v_w_up_1, v_w_down_1, v_ple_norm_1, v_w_ple_gate_1, v_w_ple_proj_1, v_attn_norm_2, v_w_in_2, v_w_out_2, v_b_forget_2, v_mlp_norm_2, v_w_up_2, v_w_down_2, v_ple_norm_2, v_w_ple_gate_2, v_w_ple_proj_2, v_attn_norm_3, v_w_in_3, v_w_out_3, v_mlp_norm_3, v_w_up_3, v_w_down_3, v_ple_norm_3, v_w_ple_gate_3, v_w_ple_proj_3, v_final_norm):
    out = _train_step(dict(locals()))
    return (out["loss"], out["grad_x"], *[out[kd + n] for kd in KINDS for n in _weight_names()])
```
